```python
import math
import jax, jax.numpy as jnp
from jax import lax
import numpy as np

D_MODEL = 1024
BATCH = 4
SEQ = 4096
DEPTH = 1

HEAD_DIM = 64
ROT_DIM = HEAD_DIM // 4
ROPE_THETA = 500000.0
QBLK = 128
DA_HEADS = 4
DA_V_DIM = 2 * HEAD_DIM
DA_QK_W = DA_HEADS * 2 * HEAD_DIM
DA_V_W = DA_HEADS * DA_V_DIM
DIL_PAIRS = ((128, 1), (512, 4), (2048, 16))
DIL_GROUPS = len(DIL_PAIRS)
DIL_HEADS = 8
DIL_W = DIL_HEADS * HEAD_DIM
IN_COLS = 2 * DA_QK_W + DA_V_W + 3 * DIL_GROUPS * DIL_W + 2 * D_MODEL
N_EXPERTS = 32
TOP_K = 4
D_FF = D_MODEL
SWIGLU_ALPHA = 1.702
SWIGLU_LIMIT = 7.0
MOE_BLK = 128
DN_ALPHA = (2 * DEPTH) ** 0.25
DN_BETA = (8 * DEPTH) ** -0.25
EPS = 1e-5

kernel_name = "hybrid_diffattn_dilated_moe_deepnorm"


def rope_tables(seq):
    inv_freq = ROPE_THETA ** (-jnp.arange(0, ROT_DIM, 2, dtype=jnp.float32) / ROT_DIM)
    ang = jnp.arange(seq, dtype=jnp.float32)[:, None] * inv_freq[None, :]
    return jnp.cos(ang), jnp.sin(ang)


def apply_rope(t, cos, sin):
    half = ROT_DIM // 2
    shp = (t.shape[1],) + (1,) * (t.ndim - 3) + (half,)
    c = cos.reshape(shp).astype(t.dtype)
    s = sin.reshape(shp).astype(t.dtype)
    t1 = t[..., :half]
    t2 = t[..., half:ROT_DIM]
    return jnp.concatenate([t1 * c - t2 * s, t2 * c + t1 * s, t[..., ROT_DIM:]], axis=-1)


def layer_norm(x, g, b):
    xf = x.astype(jnp.float32)
    mu = jnp.mean(xf, axis=-1, keepdims=True)
    var = jnp.mean(jnp.square(xf - mu), axis=-1, keepdims=True)
    y = (xf - mu) * lax.rsqrt(var + EPS) * g.astype(jnp.float32) + b.astype(jnp.float32)
    return y.astype(x.dtype)


def diff_attention(q, k, v, lam_q1, lam_k1, lam_q2, lam_k2, subln_g, lambda_init):
    B, S = q.shape[0], q.shape[1]
    qh = q.transpose(0, 2, 3, 1, 4) * (HEAD_DIM ** -0.5)
    kh = k.transpose(0, 2, 3, 1, 4)
    vh = v.transpose(0, 2, 1, 3)
    lam = (jnp.exp(jnp.sum(lam_q1.astype(jnp.float32) * lam_k1.astype(jnp.float32)))
           - jnp.exp(jnp.sum(lam_q2.astype(jnp.float32) * lam_k2.astype(jnp.float32)))
           + lambda_init)
    outs = []
    for i in range(S // QBLK):
        q0 = i * QBLK
        kend = q0 + QBLK
        s = jnp.einsum('bhcqd,bhckd->bhcqk', qh[:, :, :, q0:kend], kh[:, :, :, :kend]).astype(jnp.float32)
        mask = (q0 + jnp.arange(QBLK))[:, None] >= jnp.arange(kend)[None, :]
        p = jax.nn.softmax(jnp.where(mask, s, -jnp.inf), axis=-1)
        a = p[:, :, 0] - lam * p[:, :, 1]
        outs.append(jnp.einsum('bhqk,bhkv->bhqv', a.astype(v.dtype), vh[:, :, :kend]))
    o = jnp.concatenate(outs, axis=2).astype(jnp.float32)
    o = o * lax.rsqrt(jnp.mean(jnp.square(o), axis=-1, keepdims=True) + EPS) * subln_g.astype(jnp.float32)
    o = o * (1.0 - lambda_init)
    return o.astype(v.dtype).transpose(0, 2, 1, 3).reshape(B, S, DA_HEADS * DA_V_DIM)


def dilated_group(q, k, v, window, dilation):
    B, S, H, dh = q.shape
    rel = window // dilation
    L = S // dilation
    nb = -(-L // QBLK)
    Lp = nb * QBLK

    def strided(t):
        t = t.reshape(B, L, dilation, H, dh).transpose(0, 2, 3, 1, 4)
        return jnp.pad(t, ((0, 0), (0, 0), (0, 0), (0, Lp - L), (0, 0)))

    def banded(t):
        tb = t.reshape(B, dilation, H, nb, QBLK, dh)
        prev = jnp.pad(tb, ((0, 0), (0, 0), (0, 0), (1, 0), (0, 0), (0, 0)))[:, :, :, :nb]
        return jnp.concatenate([prev, tb], axis=4)

    qb = (strided(q) * (dh ** -0.5)).reshape(B, dilation, H, nb, QBLK, dh)
    kk = banded(strided(k))
    vv = banded(strided(v))
    s = jnp.einsum('bdhnqe,bdhnke->bdhnqk', qb, kk).astype(jnp.float32)
    qi = jnp.arange(QBLK)[:, None]
    kj = jnp.arange(2 * QBLK)[None, :]
    dist = qi + QBLK - kj
    blk = jnp.arange(nb)[:, None, None]
    mask = (dist >= 0) & (dist <= rel) & (blk * QBLK + kj - QBLK >= 0)
    s = jnp.where(mask, s, -jnp.inf)
    m = jnp.max(s, axis=-1, keepdims=True)
    p = jnp.exp(s - m)
    den = jnp.sum(p, axis=-1, keepdims=True)
    o = jnp.einsum('bdhnqk,bdhnke->bdhnqe', (p / den).astype(v.dtype), vv)
    lse = (m + jnp.log(den))[..., 0]
    o = o.reshape(B, dilation, H, Lp, dh)[:, :, :, :L].transpose(0, 3, 1, 2, 4).reshape(B, S, H, dh)
    lse = lse.reshape(B, dilation, H, Lp)[..., :L].transpose(0, 3, 1, 2).reshape(B, S, H)
    return o, lse


def token_mixer(h, w_in, b_gate, lam_q1, lam_k1, lam_q2, lam_k2, subln_g, w_oa, w_ob, w_o,
                cos, sin, lambda_init):
    B, S, D = h.shape
    proj = h @ w_in
    off = 0
    qa = proj[..., off:off + DA_QK_W].reshape(B, S, DA_HEADS, 2, HEAD_DIM); off += DA_QK_W
    ka = proj[..., off:off + DA_QK_W].reshape(B, S, DA_HEADS, 2, HEAD_DIM); off += DA_QK_W
    va = proj[..., off:off + DA_V_W].reshape(B, S, DA_HEADS, DA_V_DIM); off += DA_V_W
    y_a = diff_attention(apply_rope(qa, cos, sin), apply_rope(ka, cos, sin), va,
                         lam_q1, lam_k1, lam_q2, lam_k2, subln_g, lambda_init)
    outs, lses = [], []
    for window, dilation in DIL_PAIRS:
        qg = proj[..., off:off + DIL_W].reshape(B, S, DIL_HEADS, HEAD_DIM); off += DIL_W
        kg = proj[..., off:off + DIL_W].reshape(B, S, DIL_HEADS, HEAD_DIM); off += DIL_W
        vg = proj[..., off:off + DIL_W].reshape(B, S, DIL_HEADS, HEAD_DIM); off += DIL_W
        o, lse = dilated_group(apply_rope(qg, cos, sin), apply_rope(kg, cos, sin), vg, window, dilation)
        outs.append(o)
        lses.append(lse)
    wts = jax.nn.softmax(jnp.stack(lses, axis=0), axis=0)
    y_b = jnp.sum(wts[..., None].astype(h.dtype) * jnp.stack(outs, axis=0), axis=0).reshape(B, S, DIL_W)
    gates = jax.nn.sigmoid(proj[..., off:off + 2 * D] + b_gate)
    merged = gates[..., :D] * (y_a @ w_oa) + gates[..., D:] * (y_b @ w_ob)
    return merged @ w_o


def moe_ffn(h, w_router, b_router, w_gu, b_gu, w_down, b_down):
    B, S, D = h.shape
    T = B * S
    xt = h.reshape(T, D)
    logits = (xt @ w_router).astype(jnp.float32) + b_router.astype(jnp.float32)
    top_val, top_idx = lax.top_k(logits, TOP_K)
    gate_w = jax.nn.softmax(top_val, axis=-1)
    n_assign = T * TOP_K
    flat_e = top_idx.reshape(-1).astype(jnp.int32)
    flat_tok = jnp.arange(n_assign, dtype=jnp.int32) // TOP_K
    flat_w = gate_w.reshape(-1)
    order = jnp.argsort(flat_e)
    sorted_e = flat_e[order]
    counts = jnp.bincount(flat_e, length=N_EXPERTS).astype(jnp.int32)
    padded = (counts + MOE_BLK - 1) // MOE_BLK * MOE_BLK
    ustart = jnp.cumsum(counts) - counts
    pend = jnp.cumsum(padded)
    pstart = pend - padded
    dest = pstart[sorted_e] + (jnp.arange(n_assign, dtype=jnp.int32) - ustart[sorted_e])
    n_rows = n_assign + N_EXPERTS * MOE_BLK
    n_blocks = n_rows // MOE_BLK
    row_tok = jnp.zeros((n_rows,), jnp.int32).at[dest].set(flat_tok[order])
    row_w = jnp.zeros((n_rows,), jnp.float32).at[dest].set(flat_w[order])
    block_e = jnp.minimum(jnp.searchsorted(pend, jnp.arange(n_blocks, dtype=jnp.int32) * MOE_BLK, side='right'),
                          N_EXPERTS - 1)
    xrows = xt[row_tok].reshape(n_blocks, MOE_BLK, D)

    def expert_block(args):
        xb, e = args
        gu = xb @ w_gu[e] + b_gu[e]
        gate = jnp.minimum(gu[:, :D_FF], SWIGLU_LIMIT)
        up = jnp.clip(gu[:, D_FF:], -SWIGLU_LIMIT, SWIGLU_LIMIT)
        act = (up + 1.0) * gate * jax.nn.sigmoid(SWIGLU_ALPHA * gate)
        return act @ w_down[e] + b_down[e]

    yrows = lax.map(expert_block, (xrows, block_e)).reshape(n_rows, D)
    y = jnp.zeros((T, D), jnp.float32).at[row_tok].add(yrows.astype(jnp.float32) * row_w[:, None])
    return y.astype(h.dtype).reshape(B, S, D)


def setup_inputs(seed: int = 0) -> dict:
    key = jax.random.key(seed)
    ks = jax.random.split(key, 24)
    f32 = jnp.float32
    L = DEPTH

    def nrm(k, shape, scale):
        return jax.random.normal(k, shape, f32) * scale

    return {
        "x": nrm(ks[0], (BATCH, SEQ, D_MODEL), 1.0),
        "w_in": nrm(ks[1], (L, D_MODEL, IN_COLS), D_MODEL ** -0.5),
        "b_gate": nrm(ks[2], (L, 2 * D_MODEL), 0.02),
        "lam_q1": nrm(ks[3], (L, HEAD_DIM), 0.1),
        "lam_k1": nrm(ks[4], (L, HEAD_DIM), 0.1),
        "lam_q2": nrm(ks[5], (L, HEAD_DIM), 0.1),
        "lam_k2": nrm(ks[6], (L, HEAD_DIM), 0.1),
        "subln_g": 1.0 + nrm(ks[7], (L, DA_V_DIM), 0.02),
        "w_oa": nrm(ks[8], (L, DA_V_W, D_MODEL), DA_V_W ** -0.5),
        "w_ob": nrm(ks[9], (L, DIL_W, D_MODEL), DIL_W ** -0.5),
        "w_o": nrm(ks[10], (L, D_MODEL, D_MODEL), DN_BETA * D_MODEL ** -0.5),
        "ln1_g": 1.0 + nrm(ks[11], (L, D_MODEL), 0.02),
        "ln1_b": nrm(ks[12], (L, D_MODEL), 0.02),
        "w_router": nrm(ks[13], (L, D_MODEL, N_EXPERTS), D_MODEL ** -0.5),
        "b_router": nrm(ks[14], (L, N_EXPERTS), 0.01),
        "w_gu": nrm(ks[15], (L, N_EXPERTS, D_MODEL, 2 * D_FF), D_MODEL ** -0.5),
        "b_gu": nrm(ks[16], (L, N_EXPERTS, 2 * D_FF), 0.02),
        "w_down": nrm(ks[17], (L, N_EXPERTS, D_FF, D_MODEL), DN_BETA * D_FF ** -0.5),
        "b_down": nrm(ks[18], (L, N_EXPERTS, D_MODEL), 0.02),
        "ln2_g": 1.0 + nrm(ks[19], (L, D_MODEL), 0.02),
        "ln2_b": nrm(ks[20], (L, D_MODEL), 0.02),
    }


def reference(x, w_in, b_gate, lam_q1, lam_k1, lam_q2, lam_k2, subln_g, w_oa, w_ob, w_o,
              ln1_g, ln1_b, w_router, b_router, w_gu, b_gu, w_down, b_down, ln2_g, ln2_b):
    cos, sin = rope_tables(x.shape[1])
    h = x
    for l in range(DEPTH):
        lambda_init = 0.8 - 0.6 * math.exp(-0.3 * l)
        m = token_mixer(h, w_in[l], b_gate[l], lam_q1[l], lam_k1[l], lam_q2[l], lam_k2[l], subln_g[l],
                        w_oa[l], w_ob[l], w_o[l], cos, sin, lambda_init)
        h = layer_norm(DN_ALPHA * h + m, ln1_g[l], ln1_b[l])
        f = moe_ffn(h, w_router[l], b_router[l], w_gu[l], b_gu[l], w_down[l], b_down[l])
        h = layer_norm(DN_ALPHA * h + f, ln2_g[l], ln2_b[l])
    return h
```

```python
import functools
import math

import jax
import jax.numpy as jnp
from jax import lax
from jax.experimental import pallas as pl
from jax.experimental.pallas import tpu as pltpu

F32 = jnp.float32
BF16 = jnp.bfloat16

D_MODEL = 1024
HEAD_DIM = 64
ROT_DIM = HEAD_DIM // 4
ROPE_THETA = 500000.0
QBLK = 128
DA_HEADS = 4
DIL_PAIRS = ((128, 1), (512, 4), (2048, 16))
DIL_W = 512
IN_COLS = 8192
N_EXPERTS = 32
TOP_K = 4
D_FF = D_MODEL
SWIGLU_ALPHA = 1.702
SWIGLU_LIMIT = 7.0
DEPTH = 1
DN_ALPHA = (2 * DEPTH) ** 0.25
EPS = 1e-5
LAMBDA_INIT = 0.8 - 0.6 * math.exp(-0.3 * 0)

LANES = 128
PROJ_TN = 512
PROJ_TM = 1024
ATT_TQ = 256
MERGE_TM = 256
MOE_BM = 256
MOE_TM = 256
VMEM_LIMIT = 48 * 1024 * 1024


def _cparams(n_axes):
    return pltpu.CompilerParams(dimension_semantics=("arbitrary",) * n_axes,
                                vmem_limit_bytes=VMEM_LIMIT)


def _proj_kernel(x_ref, w_ref, bg_ref, c_ref, s1_ref, s2_ref, o_ref, xb_ref):
    j = pl.program_id(1)

    @pl.when(j == 0)
    def _():
        xb_ref[...] = x_ref[...].astype(BF16)

    acc = jnp.dot(xb_ref[...], w_ref[...], preferred_element_type=F32)
    seg = j % 3
    is_attn = j < 12

    @pl.when(is_attn & (seg != 2))
    def _():
        scale = jnp.where(seg == 0, HEAD_DIM ** -0.5, 1.0).astype(F32)
        c, s1, s2 = c_ref[...], s1_ref[...], s2_ref[...]
        for k in range(PROJ_TN // LANES):
            t = acc[:, k * LANES:(k + 1) * LANES]
            r = t * c + pltpu.roll(t, ROT_DIM // 2, 1) * s1 + pltpu.roll(t, LANES - ROT_DIM // 2, 1) * s2
            o_ref[:, k * LANES:(k + 1) * LANES] = (r * scale).astype(BF16)

    @pl.when(is_attn & (seg == 2))
    def _():
        o_ref[...] = acc.astype(BF16)

    @pl.when(jnp.logical_not(is_attn))
    def _():
        o_ref[...] = jax.nn.sigmoid(acc + bg_ref[...]).astype(BF16)


def _rope_lane_tables(seq):
    half = ROT_DIM // 2
    inv_freq = ROPE_THETA ** (-jnp.arange(0, ROT_DIM, 2, dtype=F32) / ROT_DIM)
    ang = jnp.arange(seq, dtype=F32)[:, None] * inv_freq[None, :]
    cos, sin = jnp.cos(ang), jnp.sin(ang)
    ones = jnp.ones((seq, HEAD_DIM - ROT_DIM), F32)
    zeros = jnp.zeros((seq, HEAD_DIM - ROT_DIM), F32)
    zh = jnp.zeros((seq, half), F32)
    c = jnp.concatenate([cos, cos, ones], axis=1)
    s1 = jnp.concatenate([zh, sin, zeros], axis=1)
    s2 = jnp.concatenate([-sin, zh, zeros], axis=1)
    rep = LANES // HEAD_DIM
    return tuple(jnp.tile(t, (1, rep)) for t in (c, s1, s2))


def _project(x2, w_in_b, b_gate, seq):
    tokens = x2.shape[0]
    tm = min(PROJ_TM, seq)
    c, s1, s2 = _rope_lane_tables(seq)
    per_seq = seq // tm
    n_gate0 = (IN_COLS - 2 * D_MODEL) // PROJ_TN
    tab_spec = pl.BlockSpec((tm, LANES), lambda i, j: (i % per_seq, 0))
    return pl.pallas_call(
        _proj_kernel,
        grid=(tokens // tm, IN_COLS // PROJ_TN),
        in_specs=[
            pl.BlockSpec((tm, D_MODEL), lambda i, j: (i, 0)),
            pl.BlockSpec((D_MODEL, PROJ_TN), lambda i, j: (0, j)),
            pl.BlockSpec((1, PROJ_TN), lambda i, j: (0, jnp.maximum(j - n_gate0, 0))),
            tab_spec, tab_spec, tab_spec,
        ],
        out_specs=pl.BlockSpec((tm, PROJ_TN), lambda i, j: (i, j)),
        out_shape=jax.ShapeDtypeStruct((tokens, IN_COLS), BF16),
        scratch_shapes=[pltpu.VMEM((tm, D_MODEL), BF16)],
        compiler_params=_cparams(2),
        name="proj",
    )(x2, w_in_b, b_gate.reshape(1, -1), c, s1, s2)


def _nt_dot(a, b):
    return lax.dot_general(a, b, (((1,), (1,)), ((), ())), preferred_element_type=F32)


def _diff_kernel(q_ref, k_ref, v_ref, lq1_ref, lk1_ref, lq2_ref, lk2_ref, g_ref, o_ref, *, tq):
    i = pl.program_id(2)
    q = q_ref[...]
    lane = lax.broadcasted_iota(jnp.int32, q.shape, 1)
    zero = jnp.zeros_like(q)
    q1 = jnp.where(lane < HEAD_DIM, q, zero)
    q2 = jnp.where(lane >= HEAD_DIM, q, zero)

    def one(s, m, l, acc, vb):
        m_new = jnp.maximum(m, jnp.max(s, axis=1, keepdims=True))
        a = jnp.exp(m - m_new)
        p = jnp.exp(s - m_new)
        l = a * l + jnp.sum(p, axis=1, keepdims=True)
        acc = a * acc + jnp.dot(p.astype(BF16), vb, preferred_element_type=F32)
        return m_new, l, acc

    def step(j, carry, masked):
        m1, l1, a1, m2, l2, a2 = carry
        start = pl.multiple_of(j * tq, tq)
        kb = k_ref[pl.ds(start, tq), :]
        vb = v_ref[pl.ds(start, tq), :]
        s1 = _nt_dot(q1, kb)
        s2 = _nt_dot(q2, kb)
        if masked:
            row = lax.broadcasted_iota(jnp.int32, s1.shape, 0)
            col = lax.broadcasted_iota(jnp.int32, s1.shape, 1)
            keep = row >= col
            s1 = jnp.where(keep, s1, -jnp.inf)
            s2 = jnp.where(keep, s2, -jnp.inf)
        m1, l1, a1 = one(s1, m1, l1, a1, vb)
        m2, l2, a2 = one(s2, m2, l2, a2, vb)
        return m1, l1, a1, m2, l2, a2

    neg = jnp.full((tq, 1), -jnp.inf, F32)
    z1 = jnp.zeros((tq, 1), F32)
    za = jnp.zeros((tq, LANES), F32)
    carry = lax.fori_loop(0, i, lambda j, c: step(j, c, False), (neg, z1, za, neg, z1, za))
    m1, l1, a1, m2, l2, a2 = step(i, carry, True)

    lam = (jnp.exp(jnp.sum(lq1_ref[...] * lk1_ref[...], axis=1, keepdims=True))
           - jnp.exp(jnp.sum(lq2_ref[...] * lk2_ref[...], axis=1, keepdims=True)) + LAMBDA_INIT)
    o = a1 / l1 - lam * (a2 / l2)
    ms = jnp.mean(o * o, axis=1, keepdims=True)
    o = o * lax.rsqrt(ms + EPS) * g_ref[...]
    o_ref[...] = (o * (1.0 - LAMBDA_INIT)).astype(BF16)


def _diff_attention(proj3, lam_q1, lam_k1, lam_q2, lam_k2, subln_g):
    b, s, _ = proj3.shape
    tq = min(ATT_TQ, s)
    k_blk0 = DA_HEADS
    v_blk0 = 2 * DA_HEADS
    vec = lambda n: pl.BlockSpec((1, n), lambda bb, h, i: (0, 0))
    return pl.pallas_call(
        functools.partial(_diff_kernel, tq=tq),
        grid=(b, DA_HEADS, s // tq),
        in_specs=[
            pl.BlockSpec((None, tq, LANES), lambda bb, h, i: (bb, i, h)),
            pl.BlockSpec((None, s, LANES), lambda bb, h, i: (bb, 0, k_blk0 + h)),
            pl.BlockSpec((None, s, LANES), lambda bb, h, i: (bb, 0, v_blk0 + h)),
            vec(HEAD_DIM), vec(HEAD_DIM), vec(HEAD_DIM), vec(HEAD_DIM), vec(2 * HEAD_DIM),
        ],
        out_specs=pl.BlockSpec((None, tq, LANES), lambda bb, h, i: (bb, i, h)),
        out_shape=jax.ShapeDtypeStruct((b, s, DA_HEADS * 2 * HEAD_DIM), BF16),
        compiler_params=_cparams(3),
        name="diff_attn",
    )(proj3, proj3, proj3, lam_q1.reshape(1, -1), lam_k1.reshape(1, -1), lam_q2.reshape(1, -1),
      lam_k2.reshape(1, -1), subln_g.reshape(1, -1))


def _dil_kernel(q_ref, kp_ref, kc_ref, vp_ref, vc_ref, o_ref, lse_ref, *, rel):
    n = pl.program_id(2)
    qi = lax.broadcasted_iota(jnp.int32, (QBLK, 2 * QBLK), 0)
    kj = lax.broadcasted_iota(jnp.int32, (QBLK, 2 * QBLK), 1)
    dist = qi + QBLK - kj
    valid = (dist >= 0) & (dist <= rel) & ((kj >= QBLK) | (n > 0))
    lane = lax.broadcasted_iota(jnp.int32, (QBLK, LANES), 1)
    low = lane < HEAD_DIM
    for p in range(DIL_W // LANES):
        sl = slice(p * LANES, (p + 1) * LANES)
        q2 = q_ref[:, sl]
        k2 = jnp.concatenate([kp_ref[:, sl], kc_ref[:, sl]], axis=0)
        v2 = jnp.concatenate([vp_ref[:, sl], vc_ref[:, sl]], axis=0)
        outs, lses = [], []
        for sel in (low, jnp.logical_not(low)):
            qm = jnp.where(sel, q2, jnp.zeros_like(q2))
            s = jnp.where(valid, _nt_dot(qm, k2), -jnp.inf)
            m = jnp.max(s, axis=1, keepdims=True)
            pe = jnp.exp(s - m)
            den = jnp.sum(pe, axis=1, keepdims=True)
            outs.append(jnp.dot(pe.astype(BF16), v2, preferred_element_type=F32) / den)
            lses.append(m + jnp.log(den))
        o_ref[:, sl] = jnp.where(low, outs[0], outs[1]).astype(BF16)
        lse_ref[:, sl] = jnp.where(low, lses[0], lses[1])


def _dilated_group(proj, batch, seq, group):
    window, dil = DIL_PAIRS[group]
    rel = window // dil
    length = seq // dil
    nb = length // QBLK
    cols = IN_COLS // DIL_W
    cq = 3 + 3 * group
    view = proj.reshape(batch, length, dil * IN_COLS)

    def spec(cblk, prev):
        if prev:
            return pl.BlockSpec((None, QBLK, DIL_W), lambda b, r, n: (b, jnp.maximum(n - 1, 0), r * cols + cblk))
        return pl.BlockSpec((None, QBLK, DIL_W), lambda b, r, n: (b, n, r * cols + cblk))

    out_spec = pl.BlockSpec((None, QBLK, DIL_W), lambda b, r, n: (b, n, r))
    o, lse = pl.pallas_call(
        functools.partial(_dil_kernel, rel=rel),
        grid=(batch, dil, nb),
        in_specs=[spec(cq, False), spec(cq + 1, True), spec(cq + 1, False),
                  spec(cq + 2, True), spec(cq + 2, False)],
        out_specs=[out_spec, out_spec],
        out_shape=[jax.ShapeDtypeStruct((batch, length, dil * DIL_W), BF16),
                   jax.ShapeDtypeStruct((batch, length, dil * DIL_W), F32)],
        compiler_params=_cparams(3),
        name=f"dilated{group}",
    )(view, view, view, view, view)
    tokens = batch * seq
    return o.reshape(tokens, DIL_W), lse.reshape(tokens, DIL_W)


def _layer_norm(z, g, b):
    mu = jnp.mean(z, axis=1, keepdims=True)
    zc = z - mu
    var = jnp.mean(zc * zc, axis=1, keepdims=True)
    return zc * lax.rsqrt(var + EPS) * g + b


def _merge_kernel(ya_ref, o1_ref, o2_ref, o3_ref, l1_ref, l2_ref, l3_ref, gate_ref, x_ref,
                  woa_ref, wob_ref, wo_ref, g_ref, b_ref, wrh_ref, wrl_ref, br_ref,
                  h_ref, topi_ref, topw_ref, rank_ref, cnt_ref, carry_ref):
    i = pl.program_id(0)
    tm = x_ref.shape[0]

    @pl.when(i == 0)
    def _():
        carry_ref[...] = jnp.zeros_like(carry_ref)

    la, lb, lc = l1_ref[...], l2_ref[...], l3_ref[...]
    mx = jnp.maximum(jnp.maximum(la, lb), lc)
    ea, eb, ec = jnp.exp(la - mx), jnp.exp(lb - mx), jnp.exp(lc - mx)
    yb = (ea * o1_ref[...].astype(F32) + eb * o2_ref[...].astype(F32) + ec * o3_ref[...].astype(F32)) / (ea + eb + ec)
    pa = jnp.dot(ya_ref[...], woa_ref[...], preferred_element_type=F32)
    pb = jnp.dot(yb.astype(BF16), wob_ref[...], preferred_element_type=F32)
    merged = gate_ref[:, :D_MODEL].astype(F32) * pa + gate_ref[:, D_MODEL:].astype(F32) * pb
    mix = jnp.dot(merged.astype(BF16), wo_ref[...], preferred_element_type=F32)
    h = _layer_norm(DN_ALPHA * x_ref[...] + mix, g_ref[...], b_ref[...])
    h_ref[...] = h

    hh = h.astype(BF16)
    hl = (h - hh.astype(F32)).astype(BF16)
    logits = (jnp.dot(hh, wrh_ref[...], preferred_element_type=F32)
              + jnp.dot(hl, wrh_ref[...], preferred_element_type=F32)
              + jnp.dot(hh, wrl_ref[...], preferred_element_type=F32) + br_ref[...])
    lane = lax.broadcasted_iota(jnp.int32, (tm, LANES), 1)
    lg = jnp.where(lane < N_EXPERTS, logits, -jnp.inf)
    vals, idxs = [], []
    for _ in range(TOP_K):
        mv = jnp.max(lg, axis=1, keepdims=True)
        ik = jnp.min(jnp.where(lg == mv, lane, LANES), axis=1, keepdims=True)
        vals.append(mv)
        idxs.append(ik)
        lg = jnp.where(lane == ik, -jnp.inf, lg)
    es = [jnp.exp(v - vals[0]) for v in vals]
    tot = es[0] + es[1] + es[2] + es[3]

    onehot = jnp.zeros((tm, LANES), F32)
    for ik in idxs:
        onehot = onehot + (lane == ik).astype(F32)
    r_i = lax.broadcasted_iota(jnp.int32, (tm, tm), 0)
    c_i = lax.broadcasted_iota(jnp.int32, (tm, tm), 1)
    tri = (r_i > c_i).astype(BF16)
    pre = jnp.dot(tri, onehot.astype(BF16), preferred_element_type=F32) + carry_ref[...]

    topi = jnp.zeros((tm, LANES), jnp.int32)
    topw = jnp.zeros((tm, LANES), F32)
    rank = jnp.zeros((tm, LANES), F32)
    for k in range(TOP_K):
        rk = jnp.sum(jnp.where(lane == idxs[k], pre, 0.0), axis=1, keepdims=True)
        topi = jnp.where(lane == k, idxs[k], topi)
        topw = jnp.where(lane == k, es[k] / tot, topw)
        rank = jnp.where(lane == k, rk, rank)
    topi_ref[...] = topi
    topw_ref[...] = topw
    rank_ref[...] = rank.astype(jnp.int32)
    carry_ref[...] = carry_ref[...] + jnp.sum(onehot, axis=0, keepdims=True)
    cnt_ref[...] = carry_ref[...]


def _merge(ya, dil_outs, proj, x2, w_oa_b, w_ob_b, w_o_b, ln1_g, ln1_b, w_router, b_router):
    tokens = x2.shape[0]
    tm = MERGE_TM
    wr = jnp.zeros((D_MODEL, LANES), F32).at[:, :N_EXPERTS].set(w_router)
    wr_hi = wr.astype(BF16)
    wr_lo = (wr - wr_hi.astype(F32)).astype(BF16)
    br = jnp.zeros((1, LANES), F32).at[0, :N_EXPERTS].set(b_router)
    row = lambda w: pl.BlockSpec((tm, w), lambda i: (i, 0))
    full = lambda a: pl.BlockSpec(a.shape, lambda i: (0,) * a.ndim)
    gate_blk = (IN_COLS - 2 * D_MODEL) // (2 * D_MODEL)
    (o1, l1), (o2, l2), (o3, l3) = dil_outs
    g1 = ln1_g.reshape(1, -1)
    b1 = ln1_b.reshape(1, -1)
    lane_out = lambda dt: jax.ShapeDtypeStruct((tokens, LANES), dt)
    return pl.pallas_call(
        _merge_kernel,
        grid=(tokens // tm,),
        in_specs=[row(DIL_W), row(DIL_W), row(DIL_W), row(DIL_W), row(DIL_W), row(DIL_W), row(DIL_W),
                  pl.BlockSpec((tm, 2 * D_MODEL), lambda i: (i, gate_blk)),
                  row(D_MODEL), full(w_oa_b), full(w_ob_b), full(w_o_b), full(g1), full(b1),
                  full(wr_hi), full(wr_lo), full(br)],
        out_specs=[row(D_MODEL), row(LANES), row(LANES), row(LANES), pl.BlockSpec((1, LANES), lambda i: (0, 0))],
        out_shape=[jax.ShapeDtypeStruct((tokens, D_MODEL), F32), lane_out(jnp.int32), lane_out(F32),
                   lane_out(jnp.int32), jax.ShapeDtypeStruct((1, LANES), F32)],
        scratch_shapes=[pltpu.VMEM((1, LANES), F32)],
        compiler_params=_cparams(1),
        name="merge",
    )(ya, o1, o2, o3, l1, l2, l3, proj, x2, w_oa_b, w_ob_b, w_o_b, g1, b1, wr_hi, wr_lo, br)


def _dispatch_kernel(dest_ref, h_ref, init_ref, rows_ref, sem):
    del init_ref
    i = pl.program_id(0)
    tm = h_ref.shape[0]

    def body(t, c):
        base = (i * tm + t) * TOP_K
        for k in range(TOP_K):
            d = dest_ref[base + k]
            pltpu.make_async_copy(h_ref.at[pl.ds(t, 1)], rows_ref.at[pl.ds(d, 1)], sem).start()
        return c

    lax.fori_loop(0, tm, body, 0)
    for _ in range(TOP_K):
        pltpu.make_async_copy(h_ref, rows_ref.at[pl.ds(0, tm)], sem).wait()


def _dispatch(h, dest, n_rows):
    tokens = h.shape[0]
    tm = MOE_TM
    return pl.pallas_call(
        _dispatch_kernel,
        grid_spec=pltpu.PrefetchScalarGridSpec(
            num_scalar_prefetch=1,
            grid=(tokens // tm,),
            in_specs=[pl.BlockSpec((tm, D_MODEL), lambda i, d: (i, 0)),
                      pl.BlockSpec(memory_space=pl.ANY)],
            out_specs=pl.BlockSpec(memory_space=pl.ANY),
            scratch_shapes=[pltpu.SemaphoreType.DMA(())],
        ),
        out_shape=jax.ShapeDtypeStruct((n_rows, D_MODEL), F32),
        input_output_aliases={2: 0},
        compiler_params=_cparams(1),
        name="dispatch",
    )(dest, h, jnp.zeros((n_rows, D_MODEL), F32))


def _expert_kernel(be_ref, nu_ref, x_ref, wgu_ref, bgu_ref, wd_ref, bd_ref, y_ref):
    i = pl.program_id(0)

    @pl.when(i < nu_ref[0])
    def _():
        xb = x_ref[...].astype(BF16)
        acc = jnp.zeros(y_ref.shape, F32)
        chunk = 512
        for c in range(D_FF // chunk):
            gs = slice(c * chunk, (c + 1) * chunk)
            us = slice(D_FF + c * chunk, D_FF + (c + 1) * chunk)
            gate = jnp.dot(xb, wgu_ref[:, gs], preferred_element_type=F32) + bgu_ref[:, gs]
            up = jnp.dot(xb, wgu_ref[:, us], preferred_element_type=F32) + bgu_ref[:, us]
            gate = jnp.minimum(gate, SWIGLU_LIMIT)
            up = jnp.clip(up, -SWIGLU_LIMIT, SWIGLU_LIMIT)
            act = (up + 1.0) * gate * jax.nn.sigmoid(SWIGLU_ALPHA * gate)
            acc = acc + jnp.dot(act.astype(BF16), wd_ref[gs, :], preferred_element_type=F32)
        y_ref[...] = acc + bd_ref[...]

    @pl.when(i >= nu_ref[0])
    def _():
        y_ref[...] = jnp.zeros_like(y_ref)


def _experts(xrows, block_e, n_used, w_gu_b, b_gu, w_down_b, b_down):
    n_rows = xrows.shape[0]
    bm = MOE_BM
    return pl.pallas_call(
        _expert_kernel,
        grid_spec=pltpu.PrefetchScalarGridSpec(
            num_scalar_prefetch=2,
            grid=(n_rows // bm,),
            in_specs=[
                pl.BlockSpec((bm, D_MODEL), lambda i, be, nu: (jnp.minimum(i, nu[0] - 1), 0)),
                pl.BlockSpec((None, D_MODEL, 2 * D_FF), lambda i, be, nu: (be[i], 0, 0)),
                pl.BlockSpec((None, 1, 2 * D_FF), lambda i, be, nu: (be[i], 0, 0)),
                pl.BlockSpec((None, D_FF, D_MODEL), lambda i, be, nu: (be[i], 0, 0)),
                pl.BlockSpec((None, 1, D_MODEL), lambda i, be, nu: (be[i], 0, 0)),
            ],
            out_specs=pl.BlockSpec((bm, D_MODEL), lambda i, be, nu: (i, 0)),
        ),
        out_shape=jax.ShapeDtypeStruct((n_rows, D_MODEL), F32),
        compiler_params=_cparams(1),
        name="experts",
    )(block_e, n_used, xrows, w_gu_b, b_gu.reshape(N_EXPERTS, 1, -1), w_down_b, b_down.reshape(N_EXPERTS, 1, -1))


def _combine_kernel(dest_ref, yrows_ref, w_ref, h_ref, g_ref, b_ref, o_ref, buf, sem):
    i = pl.program_id(0)
    tm = h_ref.shape[0]

    def body(t, c):
        base = (i * tm + t) * TOP_K
        for k in range(TOP_K):
            d = dest_ref[base + k]
            pltpu.make_async_copy(yrows_ref.at[pl.ds(d, 1)], buf.at[k, pl.ds(t, 1)], sem).start()
        return c

    lax.fori_loop(0, tm, body, 0)
    for k in range(TOP_K):
        pltpu.make_async_copy(yrows_ref.at[pl.ds(0, tm)], buf.at[k], sem).wait()
    w = w_ref[...]
    f = jnp.zeros(h_ref.shape, F32)
    for k in range(TOP_K):
        f = f + buf[k] * w[:, k:k + 1]
    o_ref[...] = _layer_norm(DN_ALPHA * h_ref[...] + f, g_ref[...], b_ref[...])


def _combine(yrows, dest, topw, h, ln2_g, ln2_b):
    tokens = h.shape[0]
    tm = MOE_TM
    return pl.pallas_call(
        _combine_kernel,
        grid_spec=pltpu.PrefetchScalarGridSpec(
            num_scalar_prefetch=1,
            grid=(tokens // tm,),
            in_specs=[pl.BlockSpec(memory_space=pl.ANY),
                      pl.BlockSpec((tm, LANES), lambda i, d: (i, 0)),
                      pl.BlockSpec((tm, D_MODEL), lambda i, d: (i, 0)),
                      pl.BlockSpec((1, D_MODEL), lambda i, d: (0, 0)),
                      pl.BlockSpec((1, D_MODEL), lambda i, d: (0, 0))],
            out_specs=pl.BlockSpec((tm, D_MODEL), lambda i, d: (i, 0)),
            scratch_shapes=[pltpu.VMEM((TOP_K, tm, D_MODEL), F32), pltpu.SemaphoreType.DMA(())],
        ),
        out_shape=jax.ShapeDtypeStruct((tokens, D_MODEL), F32),
        compiler_params=_cparams(1),
        name="combine",
    )(dest, yrows, topw, h, ln2_g.reshape(1, -1), ln2_b.reshape(1, -1))


def _routing_tables(topi, rank, cnt, tokens):
    bm = MOE_BM
    counts = cnt[0, :N_EXPERTS].astype(jnp.int32)
    padded = (counts + bm - 1) // bm * bm
    pend = jnp.cumsum(padded)
    pstart = pend - padded
    e = topi[:, :TOP_K]
    sel = e[:, :, None] == jnp.arange(N_EXPERTS, dtype=jnp.int32)[None, None, :]
    dest = rank[:, :TOP_K] + jnp.sum(jnp.where(sel, pstart[None, None, :], 0), axis=-1)
    n_rows = tokens * TOP_K + N_EXPERTS * bm
    n_blocks = n_rows // bm
    starts = jnp.arange(n_blocks, dtype=jnp.int32) * bm
    block_e = jnp.minimum(jnp.sum((pend[None, :] <= starts[:, None]).astype(jnp.int32), axis=1), N_EXPERTS - 1)
    n_used = (pend[-1] // bm).reshape(1)
    return dest.reshape(-1).astype(jnp.int32), block_e.astype(jnp.int32), n_used.astype(jnp.int32), n_rows


def kernel(x, w_in, b_gate, lam_q1, lam_k1, lam_q2, lam_k2, subln_g, w_oa, w_ob, w_o, ln1_g, ln1_b,
           w_router, b_router, w_gu, b_gu, w_down, b_down, ln2_g, ln2_b):
    batch, seq, d = x.shape
    tokens = batch * seq
    h = x.reshape(tokens, d)
    for l in range(DEPTH):
        proj = _project(h, w_in[l].astype(BF16), b_gate[l], seq)
        ya = _diff_attention(proj.reshape(batch, seq, IN_COLS), lam_q1[l], lam_k1[l], lam_q2[l], lam_k2[l],
                             subln_g[l]).reshape(tokens, -1)
        dil = [_dilated_group(proj, batch, seq, g) for g in range(len(DIL_PAIRS))]
        h1, topi, topw, rank, cnt = _merge(ya, dil, proj, h, w_oa[l].astype(BF16), w_ob[l].astype(BF16),
                                           w_o[l].astype(BF16), ln1_g[l], ln1_b[l], w_router[l], b_router[l])
        dest, block_e, n_used, n_rows = _routing_tables(topi, rank, cnt, tokens)
        xrows = _dispatch(h1, dest, n_rows)
        yrows = _experts(xrows, block_e, n_used, w_gu[l].astype(BF16), b_gu[l], w_down[l].astype(BF16), b_down[l])
        h = _combine(yrows, dest, topw, h1, ln2_g[l], ln2_b[l])
    return h.reshape(batch, seq, d)
```

```python
import functools
import math

import jax
import jax.numpy as jnp
from jax import lax
from jax.experimental import pallas as pl
from jax.experimental.pallas import tpu as pltpu

F32 = jnp.float32
BF16 = jnp.bfloat16

D_MODEL = 1024
HEAD_DIM = 64
ROT_DIM = HEAD_DIM // 4
ROPE_THETA = 500000.0
QBLK = 128
DA_HEADS = 4
DIL_PAIRS = ((128, 1), (512, 4), (2048, 16))
SEG_W = 512
QKV_W = 3 * SEG_W
N_EXPERTS = 32
TOP_K = 4
D_FF = D_MODEL
SWIGLU_ALPHA = 1.702
SWIGLU_LIMIT = 7.0
DEPTH = 1
DN_ALPHA = (2 * DEPTH) ** 0.25
EPS = 1e-5
LAMBDA_INIT = 0.8 - 0.6 * math.exp(-0.3 * 0)

LANES = 128
QKV_TM = 512
QKV_CHUNK = 256
ATT_TQ = 256
MERGE_TM = 256
MOE_BM = 256
MOE_TM = 256
VMEM_LIMIT = 52 * 1024 * 1024


def _cparams(n_axes):
    return pltpu.CompilerParams(dimension_semantics=("arbitrary",) * n_axes,
                                vmem_limit_bytes=VMEM_LIMIT)


def _qkv_kernel(x_ref, w_ref, c_ref, s1_ref, s2_ref, o_ref):
    tm = x_ref.shape[0]
    for rc in range(tm // QKV_CHUNK):
        rows = slice(rc * QKV_CHUNK, (rc + 1) * QKV_CHUNK)
        xb = x_ref[rows, :]
        c, s1, s2 = c_ref[rows, :], s1_ref[rows, :], s2_ref[rows, :]
        for seg in range(3):
            acc = jnp.dot(xb, w_ref[:, seg * SEG_W:(seg + 1) * SEG_W], preferred_element_type=F32)
            if seg == 2:
                o_ref[rows, seg * SEG_W:(seg + 1) * SEG_W] = acc.astype(BF16)
                continue
            for k in range(SEG_W // LANES):
                t = acc[:, k * LANES:(k + 1) * LANES]
                r = t * c + pltpu.roll(t, ROT_DIM // 2, 1) * s1 + pltpu.roll(t, LANES - ROT_DIM // 2, 1) * s2
                if seg == 0:
                    r = r * (HEAD_DIM ** -0.5)
                lo = seg * SEG_W + k * LANES
                o_ref[rows, lo:lo + LANES] = r.astype(BF16)


def _rope_lane_tables(seq):
    half = ROT_DIM // 2
    inv_freq = ROPE_THETA ** (-jnp.arange(0, ROT_DIM, 2, dtype=F32) / ROT_DIM)
    ang = jnp.arange(seq, dtype=F32)[:, None] * inv_freq[None, :]
    cos, sin = jnp.cos(ang), jnp.sin(ang)
    ones = jnp.ones((seq, HEAD_DIM - ROT_DIM), F32)
    zeros = jnp.zeros((seq, HEAD_DIM - ROT_DIM), F32)
    zh = jnp.zeros((seq, half), F32)
    c = jnp.concatenate([cos, cos, ones], axis=1)
    s1 = jnp.concatenate([zh, sin, zeros], axis=1)
    s2 = jnp.concatenate([-sin, zh, zeros], axis=1)
    rep = LANES // HEAD_DIM
    return tuple(jnp.tile(t, (1, rep)) for t in (c, s1, s2))


def _residue_major(t, dil):
    s = t.shape[0]
    return t.reshape(s // dil, dil, *t.shape[1:]).swapaxes(0, 1).reshape(t.shape)


def _project_qkv(xb, w_seg, tables, seq, name):
    tokens = xb.shape[0]
    tm = min(QKV_TM, seq)
    per_seq = seq // tm
    tab_spec = pl.BlockSpec((tm, LANES), lambda i: (i % per_seq, 0))
    return pl.pallas_call(
        _qkv_kernel,
        grid=(tokens // tm,),
        in_specs=[pl.BlockSpec((tm, D_MODEL), lambda i: (i, 0)),
                  pl.BlockSpec((D_MODEL, QKV_W), lambda i: (0, 0)),
                  tab_spec, tab_spec, tab_spec],
        out_specs=pl.BlockSpec((tm, QKV_W), lambda i: (i, 0)),
        out_shape=jax.ShapeDtypeStruct((tokens, QKV_W), BF16),
        compiler_params=_cparams(1),
        name=name,
    )(xb, w_seg, *tables)


def _nt_dot(a, b):
    return lax.dot_general(a, b, (((1,), (1,)), ((), ())), preferred_element_type=F32)


def _diff_kernel(q_ref, k_ref, v_ref, lq1_ref, lk1_ref, lq2_ref, lk2_ref, g_ref, o_ref, *, tq):
    i = pl.program_id(1)
    lane = lax.broadcasted_iota(jnp.int32, (tq, LANES), 1)
    qs = []
    for h in range(DA_HEADS):
        q = q_ref[:, h * LANES:(h + 1) * LANES]
        zero = jnp.zeros_like(q)
        qs.append((jnp.where(lane < HEAD_DIM, q, zero), jnp.where(lane >= HEAD_DIM, q, zero)))

    def one(s, state, vb):
        m, l, acc = state
        m_new = jnp.maximum(m, jnp.max(s, axis=1, keepdims=True))
        a = jnp.exp(m - m_new)
        p = jnp.exp(s - m_new)
        l = a * l + jnp.sum(p, axis=1, keepdims=True)
        acc = a * acc + jnp.dot(p.astype(BF16), vb, preferred_element_type=F32)
        return m_new, l, acc

    def step(j, carry, masked):
        start = pl.multiple_of(j * tq, tq)
        if masked:
            row = lax.broadcasted_iota(jnp.int32, (tq, tq), 0)
            col = lax.broadcasted_iota(jnp.int32, (tq, tq), 1)
            keep = row >= col
        out = []
        for h in range(DA_HEADS):
            kb = k_ref[pl.ds(start, tq), h * LANES:(h + 1) * LANES]
            vb = v_ref[pl.ds(start, tq), h * LANES:(h + 1) * LANES]
            for comp in range(2):
                s = _nt_dot(qs[h][comp], kb)
                if masked:
                    s = jnp.where(keep, s, -jnp.inf)
                out.append(one(s, carry[2 * h + comp], vb))
        return tuple(out)

    init = tuple((jnp.full((tq, 1), -jnp.inf, F32), jnp.zeros((tq, 1), F32), jnp.zeros((tq, LANES), F32))
                 for _ in range(2 * DA_HEADS))
    carry = lax.fori_loop(0, i, lambda j, c: step(j, c, False), init)
    carry = step(i, carry, True)

    lam = (jnp.exp(jnp.sum(lq1_ref[...] * lk1_ref[...], axis=1, keepdims=True))
           - jnp.exp(jnp.sum(lq2_ref[...] * lk2_ref[...], axis=1, keepdims=True)) + LAMBDA_INIT)
    for h in range(DA_HEADS):
        (_, l1, a1), (_, l2, a2) = carry[2 * h], carry[2 * h + 1]
        o = a1 / l1 - lam * (a2 / l2)
        ms = jnp.mean(o * o, axis=1, keepdims=True)
        o = o * lax.rsqrt(ms + EPS) * g_ref[...]
        o_ref[:, h * LANES:(h + 1) * LANES] = (o * (1.0 - LAMBDA_INIT)).astype(BF16)


def _diff_attention(qkv3, lam_q1, lam_k1, lam_q2, lam_k2, subln_g):
    b, s, _ = qkv3.shape
    tq = min(ATT_TQ, s)
    vec = lambda n: pl.BlockSpec((1, n), lambda bb, i: (0, 0))
    return pl.pallas_call(
        functools.partial(_diff_kernel, tq=tq),
        grid=(b, s // tq),
        in_specs=[
            pl.BlockSpec((None, tq, SEG_W), lambda bb, i: (bb, i, 0)),
            pl.BlockSpec((None, s, SEG_W), lambda bb, i: (bb, 0, 1)),
            pl.BlockSpec((None, s, SEG_W), lambda bb, i: (bb, 0, 2)),
            vec(HEAD_DIM), vec(HEAD_DIM), vec(HEAD_DIM), vec(HEAD_DIM), vec(2 * HEAD_DIM),
        ],
        out_specs=pl.BlockSpec((None, tq, SEG_W), lambda bb, i: (bb, i, 0)),
        out_shape=jax.ShapeDtypeStruct((b, s, SEG_W), BF16),
        compiler_params=_cparams(2),
        name="diff_attn",
    )(qkv3, qkv3, qkv3, lam_q1.reshape(1, -1), lam_k1.reshape(1, -1), lam_q2.reshape(1, -1),
      lam_k2.reshape(1, -1), subln_g.reshape(1, -1))


def _dil_kernel(q_ref, kp_ref, kc_ref, vp_ref, vc_ref, o_ref, lse_ref, *, rel):
    n = pl.program_id(2)
    qi = lax.broadcasted_iota(jnp.int32, (QBLK, 2 * QBLK), 0)
    kj = lax.broadcasted_iota(jnp.int32, (QBLK, 2 * QBLK), 1)
    dist = qi + QBLK - kj
    valid = (dist >= 0) & (dist <= rel) & ((kj >= QBLK) | (n > 0))
    lane = lax.broadcasted_iota(jnp.int32, (QBLK, LANES), 1)
    low = lane < HEAD_DIM
    for p in range(SEG_W // LANES):
        sl = slice(p * LANES, (p + 1) * LANES)
        q2 = q_ref[:, sl]
        k2 = jnp.concatenate([kp_ref[:, sl], kc_ref[:, sl]], axis=0)
        v2 = jnp.concatenate([vp_ref[:, sl], vc_ref[:, sl]], axis=0)
        outs, lses = [], []
        for sel in (low, jnp.logical_not(low)):
            qm = jnp.where(sel, q2, jnp.zeros_like(q2))
            s = jnp.where(valid, _nt_dot(qm, k2), -jnp.inf)
            m = jnp.max(s, axis=1, keepdims=True)
            pe = jnp.exp(s - m)
            den = jnp.sum(pe, axis=1, keepdims=True)
            outs.append(jnp.dot(pe.astype(BF16), v2, preferred_element_type=F32) / den)
            lses.append(m + jnp.log(den))
        o_ref[:, sl] = jnp.where(low, outs[0], outs[1]).astype(BF16)
        lse_ref[:, sl] = jnp.where(low, lses[0], lses[1])


def _dilated_group(qkv, batch, seq, group):
    window, dil = DIL_PAIRS[group]
    rel = window // dil
    length = seq // dil
    nb = length // QBLK
    view = qkv.reshape(batch, dil, length, QKV_W)

    def spec(seg, prev):
        if prev:
            return pl.BlockSpec((None, None, QBLK, SEG_W), lambda b, r, n: (b, r, jnp.maximum(n - 1, 0), seg))
        return pl.BlockSpec((None, None, QBLK, SEG_W), lambda b, r, n: (b, r, n, seg))

    out_spec = pl.BlockSpec((None, None, QBLK, SEG_W), lambda b, r, n: (b, r, n, 0))
    o, lse = pl.pallas_call(
        functools.partial(_dil_kernel, rel=rel),
        grid=(batch, dil, nb),
        in_specs=[spec(0, False), spec(1, True), spec(1, False), spec(2, True), spec(2, False)],
        out_specs=[out_spec, out_spec],
        out_shape=[jax.ShapeDtypeStruct((batch, dil, length, SEG_W), BF16),
                   jax.ShapeDtypeStruct((batch, dil, length, SEG_W), F32)],
        compiler_params=_cparams(3),
        name=f"dilated{group}",
    )(view, view, view, view, view)
    tokens = batch * seq
    back = lambda t: t.swapaxes(1, 2).reshape(tokens, SEG_W)
    return back(o), back(lse)


def _layer_norm(z, g, b):
    mu = jnp.mean(z, axis=1, keepdims=True)
    zc = z - mu
    var = jnp.mean(zc * zc, axis=1, keepdims=True)
    return zc * lax.rsqrt(var + EPS) * g + b


def _merge_kernel(ya_ref, o1_ref, o2_ref, o3_ref, l1_ref, l2_ref, l3_ref, x_ref,
                  wg_ref, bg_ref, woa_ref, wob_ref, wo_ref, g_ref, b_ref, wrh_ref, wrl_ref, br_ref,
                  h_ref, topi_ref, topw_ref, rank_ref, cnt_ref, carry_ref):
    i = pl.program_id(0)
    tm = x_ref.shape[0]

    @pl.when(i == 0)
    def _():
        carry_ref[...] = jnp.zeros_like(carry_ref)

    x = x_ref[...]
    gates = jax.nn.sigmoid(jnp.dot(x.astype(BF16), wg_ref[...], preferred_element_type=F32) + bg_ref[...])
    la, lb, lc = l1_ref[...], l2_ref[...], l3_ref[...]
    mx = jnp.maximum(jnp.maximum(la, lb), lc)
    ea, eb, ec = jnp.exp(la - mx), jnp.exp(lb - mx), jnp.exp(lc - mx)
    yb = (ea * o1_ref[...].astype(F32) + eb * o2_ref[...].astype(F32) + ec * o3_ref[...].astype(F32)) / (ea + eb + ec)
    pa = jnp.dot(ya_ref[...], woa_ref[...], preferred_element_type=F32)
    pb = jnp.dot(yb.astype(BF16), wob_ref[...], preferred_element_type=F32)
    merged = gates[:, :D_MODEL] * pa + gates[:, D_MODEL:] * pb
    mix = jnp.dot(merged.astype(BF16), wo_ref[...], preferred_element_type=F32)
    h = _layer_norm(DN_ALPHA * x + mix, g_ref[...], b_ref[...])
    h_ref[...] = h

    hh = h.astype(BF16)
    hl = (h - hh.astype(F32)).astype(BF16)
    logits = (jnp.dot(hh, wrh_ref[...], preferred_element_type=F32)
              + jnp.dot(hl, wrh_ref[...], preferred_element_type=F32)
              + jnp.dot(hh, wrl_ref[...], preferred_element_type=F32) + br_ref[...])
    lane = lax.broadcasted_iota(jnp.int32, (tm, LANES), 1)
    lg = jnp.where(lane < N_EXPERTS, logits, -jnp.inf)
    vals, idxs = [], []
    for _ in range(TOP_K):
        mv = jnp.max(lg, axis=1, keepdims=True)
        ik = jnp.min(jnp.where(lg == mv, lane, LANES), axis=1, keepdims=True)
        vals.append(mv)
        idxs.append(ik)
        lg = jnp.where(lane == ik, -jnp.inf, lg)
    es = [jnp.exp(v - vals[0]) for v in vals]
    tot = es[0] + es[1] + es[2] + es[3]

    onehot = jnp.zeros((tm, LANES), F32)
    for ik in idxs:
        onehot = onehot + (lane == ik).astype(F32)
    r_i = lax.broadcasted_iota(jnp.int32, (tm, tm), 0)
    c_i = lax.broadcasted_iota(jnp.int32, (tm, tm), 1)
    tri = (r_i > c_i).astype(BF16)
    pre = jnp.dot(tri, onehot.astype(BF16), preferred_element_type=F32) + carry_ref[...]

    topi = jnp.zeros((tm, LANES), jnp.int32)
    topw = jnp.zeros((tm, LANES), F32)
    rank = jnp.zeros((tm, LANES), F32)
    for k in range(TOP_K):
        rk = jnp.sum(jnp.where(lane == idxs[k], pre, 0.0), axis=1, keepdims=True)
        topi = jnp.where(lane == k, idxs[k], topi)
        topw = jnp.where(lane == k, es[k] / tot, topw)
        rank = jnp.where(lane == k, rk, rank)
    topi_ref[...] = topi
    topw_ref[...] = topw
    rank_ref[...] = rank.astype(jnp.int32)
    carry_ref[...] = carry_ref[...] + jnp.sum(onehot, axis=0, keepdims=True)
    cnt_ref[...] = carry_ref[...]


def _merge(ya, dil_outs, x2, w_gate_b, b_gate, w_oa_b, w_ob_b, w_o_b, ln1_g, ln1_b, w_router, b_router):
    tokens = x2.shape[0]
    tm = MERGE_TM
    wr = jnp.zeros((D_MODEL, LANES), F32).at[:, :N_EXPERTS].set(w_router)
    wr_hi = wr.astype(BF16)
    wr_lo = (wr - wr_hi.astype(F32)).astype(BF16)
    br = jnp.zeros((1, LANES), F32).at[0, :N_EXPERTS].set(b_router)
    row = lambda w: pl.BlockSpec((tm, w), lambda i: (i, 0))
    full = lambda a: pl.BlockSpec(a.shape, lambda i: (0,) * a.ndim)
    (o1, l1), (o2, l2), (o3, l3) = dil_outs
    bg = b_gate.reshape(1, -1)
    g1 = ln1_g.reshape(1, -1)
    b1 = ln1_b.reshape(1, -1)
    lane_out = lambda dt: jax.ShapeDtypeStruct((tokens, LANES), dt)
    return pl.pallas_call(
        _merge_kernel,
        grid=(tokens // tm,),
        in_specs=[row(SEG_W), row(SEG_W), row(SEG_W), row(SEG_W), row(SEG_W), row(SEG_W), row(SEG_W),
                  row(D_MODEL), full(w_gate_b), full(bg), full(w_oa_b), full(w_ob_b), full(w_o_b), full(g1), full(b1),
                  full(wr_hi), full(wr_lo), full(br)],
        out_specs=[row(D_MODEL), row(LANES), row(LANES), row(LANES), pl.BlockSpec((1, LANES), lambda i: (0, 0))],
        out_shape=[jax.ShapeDtypeStruct((tokens, D_MODEL), F32), lane_out(jnp.int32), lane_out(F32),
                   lane_out(jnp.int32), jax.ShapeDtypeStruct((1, LANES), F32)],
        scratch_shapes=[pltpu.VMEM((1, LANES), F32)],
        compiler_params=_cparams(1),
        name="merge",
    )(ya, o1, o2, o3, l1, l2, l3, x2, w_gate_b, bg, w_oa_b, w_ob_b, w_o_b, g1, b1, wr_hi, wr_lo, br)


def _dispatch_kernel(dest_ref, h_ref, init_ref, rows_ref, sem):
    del init_ref
    i = pl.program_id(0)
    tm = h_ref.shape[0]

    def body(t, c):
        base = (i * tm + t) * TOP_K
        for k in range(TOP_K):
            d = dest_ref[base + k]
            pltpu.make_async_copy(h_ref.at[pl.ds(t, 1)], rows_ref.at[pl.ds(d, 1)], sem).start()
        return c

    lax.fori_loop(0, tm, body, 0)
    for _ in range(TOP_K):
        pltpu.make_async_copy(h_ref, rows_ref.at[pl.ds(0, tm)], sem).wait()


def _dispatch(h, dest, n_rows):
    tokens = h.shape[0]
    tm = MOE_TM
    return pl.pallas_call(
        _dispatch_kernel,
        grid_spec=pltpu.PrefetchScalarGridSpec(
            num_scalar_prefetch=1,
            grid=(tokens // tm,),
            in_specs=[pl.BlockSpec((tm, D_MODEL), lambda i, d: (i, 0)),
                      pl.BlockSpec(memory_space=pl.ANY)],
            out_specs=pl.BlockSpec(memory_space=pl.ANY),
            scratch_shapes=[pltpu.SemaphoreType.DMA(())],
        ),
        out_shape=jax.ShapeDtypeStruct((n_rows, D_MODEL), F32),
        input_output_aliases={2: 0},
        compiler_params=_cparams(1),
        name="dispatch",
    )(dest, h, jnp.zeros((n_rows, D_MODEL), F32))


def _expert_kernel(be_ref, nu_ref, x_ref, wgu_ref, bgu_ref, wd_ref, bd_ref, y_ref):
    i = pl.program_id(0)

    @pl.when(i < nu_ref[0])
    def _():
        xb = x_ref[...].astype(BF16)
        acc = jnp.zeros(y_ref.shape, F32)
        chunk = 512
        for c in range(D_FF // chunk):
            gs = slice(c * chunk, (c + 1) * chunk)
            us = slice(D_FF + c * chunk, D_FF + (c + 1) * chunk)
            gate = jnp.dot(xb, wgu_ref[:, gs].astype(BF16), preferred_element_type=F32) + bgu_ref[:, gs]
            up = jnp.dot(xb, wgu_ref[:, us].astype(BF16), preferred_element_type=F32) + bgu_ref[:, us]
            gate = jnp.minimum(gate, SWIGLU_LIMIT)
            up = jnp.clip(up, -SWIGLU_LIMIT, SWIGLU_LIMIT)
            act = (up + 1.0) * gate * jax.nn.sigmoid(SWIGLU_ALPHA * gate)
            acc = acc + jnp.dot(act.astype(BF16), wd_ref[gs, :].astype(BF16), preferred_element_type=F32)
        y_ref[...] = acc + bd_ref[...]

    @pl.when(i >= nu_ref[0])
    def _():
        y_ref[...] = jnp.zeros_like(y_ref)


def _experts(xrows, block_e, n_used, w_gu, b_gu, w_down, b_down):
    n_rows = xrows.shape[0]
    bm = MOE_BM
    return pl.pallas_call(
        _expert_kernel,
        grid_spec=pltpu.PrefetchScalarGridSpec(
            num_scalar_prefetch=2,
            grid=(n_rows // bm,),
            in_specs=[
                pl.BlockSpec((bm, D_MODEL), lambda i, be, nu: (jnp.minimum(i, nu[0] - 1), 0)),
                pl.BlockSpec((None, D_MODEL, 2 * D_FF), lambda i, be, nu: (be[i], 0, 0)),
                pl.BlockSpec((None, 1, 2 * D_FF), lambda i, be, nu: (be[i], 0, 0)),
                pl.BlockSpec((None, D_FF, D_MODEL), lambda i, be, nu: (be[i], 0, 0)),
                pl.BlockSpec((None, 1, D_MODEL), lambda i, be, nu: (be[i], 0, 0)),
            ],
            out_specs=pl.BlockSpec((bm, D_MODEL), lambda i, be, nu: (i, 0)),
        ),
        out_shape=jax.ShapeDtypeStruct((n_rows, D_MODEL), F32),
        compiler_params=_cparams(1),
        name="experts",
    )(block_e, n_used, xrows, w_gu, b_gu.reshape(N_EXPERTS, 1, -1), w_down, b_down.reshape(N_EXPERTS, 1, -1))


def _combine_kernel(dest_ref, yrows_ref, w_ref, h_ref, g_ref, b_ref, o_ref, buf, sem):
    i = pl.program_id(0)
    tm = h_ref.shape[0]

    def body(t, c):
        base = (i * tm + t) * TOP_K
        for k in range(TOP_K):
            d = dest_ref[base + k]
            pltpu.make_async_copy(yrows_ref.at[pl.ds(d, 1)], buf.at[k, pl.ds(t, 1)], sem).start()
        return c

    lax.fori_loop(0, tm, body, 0)
    for k in range(TOP_K):
        pltpu.make_async_copy(yrows_ref.at[pl.ds(0, tm)], buf.at[k], sem).wait()
    w = w_ref[...]
    f = jnp.zeros(h_ref.shape, F32)
    for k in range(TOP_K):
        f = f + buf[k] * w[:, k:k + 1]
    o_ref[...] = _layer_norm(DN_ALPHA * h_ref[...] + f, g_ref[...], b_ref[...])


def _combine(yrows, dest, topw, h, ln2_g, ln2_b):
    tokens = h.shape[0]
    tm = MOE_TM
    return pl.pallas_call(
        _combine_kernel,
        grid_spec=pltpu.PrefetchScalarGridSpec(
            num_scalar_prefetch=1,
            grid=(tokens // tm,),
            in_specs=[pl.BlockSpec(memory_space=pl.ANY),
                      pl.BlockSpec((tm, LANES), lambda i, d: (i, 0)),
                      pl.BlockSpec((tm, D_MODEL), lambda i, d: (i, 0)),
                      pl.BlockSpec((1, D_MODEL), lambda i, d: (0, 0)),
                      pl.BlockSpec((1, D_MODEL), lambda i, d: (0, 0))],
            out_specs=pl.BlockSpec((tm, D_MODEL), lambda i, d: (i, 0)),
            scratch_shapes=[pltpu.VMEM((TOP_K, tm, D_MODEL), F32), pltpu.SemaphoreType.DMA(())],
        ),
        out_shape=jax.ShapeDtypeStruct((tokens, D_MODEL), F32),
        compiler_params=_cparams(1),
        name="combine",
    )(dest, yrows, topw, h, ln2_g.reshape(1, -1), ln2_b.reshape(1, -1))


def _routing_tables(topi, rank, cnt, tokens):
    bm = MOE_BM
    counts = cnt[0, :N_EXPERTS].astype(jnp.int32)
    padded = (counts + bm - 1) // bm * bm
    pend = jnp.cumsum(padded)
    pstart = pend - padded
    e = topi[:, :TOP_K]
    sel = e[:, :, None] == jnp.arange(N_EXPERTS, dtype=jnp.int32)[None, None, :]
    dest = rank[:, :TOP_K] + jnp.sum(jnp.where(sel, pstart[None, None, :], 0), axis=-1)
    n_rows = tokens * TOP_K + N_EXPERTS * bm
    n_blocks = n_rows // bm
    starts = jnp.arange(n_blocks, dtype=jnp.int32) * bm
    block_e = jnp.minimum(jnp.sum((pend[None, :] <= starts[:, None]).astype(jnp.int32), axis=1), N_EXPERTS - 1)
    n_used = (pend[-1] // bm).reshape(1)
    return dest.reshape(-1).astype(jnp.int32), block_e.astype(jnp.int32), n_used.astype(jnp.int32), n_rows


def kernel(x, w_in, b_gate, lam_q1, lam_k1, lam_q2, lam_k2, subln_g, w_oa, w_ob, w_o, ln1_g, ln1_b,
           w_router, b_router, w_gu, b_gu, w_down, b_down, ln2_g, ln2_b):
    batch, seq, d = x.shape
    tokens = batch * seq
    h = x.reshape(tokens, d)
    tables = _rope_lane_tables(seq)
    for l in range(DEPTH):
        w_b = w_in[l].astype(BF16)
        hb = h.astype(BF16)
        qkv_a = _project_qkv(hb, w_b[:, :QKV_W], tables, seq, "qkv_diff")
        ya = _diff_attention(qkv_a.reshape(batch, seq, QKV_W), lam_q1[l], lam_k1[l], lam_q2[l], lam_k2[l],
                             subln_g[l]).reshape(tokens, -1)
        dil = []
        for g, (_, dilation) in enumerate(DIL_PAIRS):
            w_g = w_b[:, (g + 1) * QKV_W:(g + 2) * QKV_W]
            if dilation == 1:
                hb_g, tab_g = hb, tables
            else:
                hb_g = hb.reshape(batch, seq // dilation, dilation, d).swapaxes(1, 2).reshape(tokens, d)
                tab_g = tuple(_residue_major(t, dilation) for t in tables)
            qkv_g = _project_qkv(hb_g, w_g, tab_g, seq, f"qkv_dil{g}")
            dil.append(_dilated_group(qkv_g, batch, seq, g))
        h1, topi, topw, rank, cnt = _merge(ya, dil, h, w_b[:, 4 * QKV_W:], b_gate[l], w_oa[l].astype(BF16),
                                           w_ob[l].astype(BF16), w_o[l].astype(BF16), ln1_g[l], ln1_b[l],
                                           w_router[l], b_router[l])
        dest, block_e, n_used, n_rows = _routing_tables(topi, rank, cnt, tokens)
        xrows = _dispatch(h1, dest, n_rows)
        yrows = _experts(xrows, block_e, n_used, w_gu[l], b_gu[l], w_down[l], b_down[l])
        h = _combine(yrows, dest, topw, h1, ln2_g[l], ln2_b[l])
    return h.reshape(batch, seq, d)
```

```python
import functools
import math

import jax
import jax.numpy as jnp
from jax import lax
from jax.experimental import pallas as pl
from jax.experimental.pallas import tpu as pltpu

F32 = jnp.float32
BF16 = jnp.bfloat16

D_MODEL = 1024
HEAD_DIM = 64
ROT_DIM = HEAD_DIM // 4
ROPE_THETA = 500000.0
QBLK = 128
DA_HEADS = 4
DIL_PAIRS = ((128, 1), (512, 4), (2048, 16))
SEG_W = 512
QKV_W = 3 * SEG_W
N_EXPERTS = 32
TOP_K = 4
D_FF = D_MODEL
SWIGLU_ALPHA = 1.702
SWIGLU_LIMIT = 7.0
DEPTH = 1
DN_ALPHA = (2 * DEPTH) ** 0.25
EPS = 1e-5
LAMBDA_INIT = 0.8 - 0.6 * math.exp(-0.3 * 0)

LANES = 128
QKV_TM = 512
QKV_CHUNK = 256
ATT_TQ = 256
MERGE_TM = 256
MOE_BM = 256
MOE_TM = 256
VMEM_LIMIT = 52 * 1024 * 1024


def _cparams(n_axes):
    return pltpu.CompilerParams(dimension_semantics=("arbitrary",) * n_axes,
                                vmem_limit_bytes=VMEM_LIMIT)


def _qkv_kernel(x_ref, w_ref, c_ref, s1_ref, s2_ref, o_ref, *maybe_vt_ref, q_scale):
    tm = x_ref.shape[0]
    for rc in range(tm // QKV_CHUNK):
        rows = slice(rc * QKV_CHUNK, (rc + 1) * QKV_CHUNK)
        xb = x_ref[rows, :]
        c, s1, s2 = c_ref[rows, :], s1_ref[rows, :], s2_ref[rows, :]
        for seg in range(3):
            acc = jnp.dot(xb, w_ref[:, seg * SEG_W:(seg + 1) * SEG_W], preferred_element_type=F32)
            if seg == 2:
                if maybe_vt_ref:
                    maybe_vt_ref[0][:, rows] = acc.T.astype(BF16)
                else:
                    o_ref[rows, seg * SEG_W:(seg + 1) * SEG_W] = acc.astype(BF16)
                continue
            for k in range(SEG_W // LANES):
                t = acc[:, k * LANES:(k + 1) * LANES]
                r = t * c + pltpu.roll(t, ROT_DIM // 2, 1) * s1 + pltpu.roll(t, LANES - ROT_DIM // 2, 1) * s2
                if seg == 0:
                    r = r * q_scale
                lo = seg * SEG_W + k * LANES
                o_ref[rows, lo:lo + LANES] = r.astype(BF16)


def _rope_lane_tables(seq):
    half = ROT_DIM // 2
    inv_freq = ROPE_THETA ** (-jnp.arange(0, ROT_DIM, 2, dtype=F32) / ROT_DIM)
    ang = jnp.arange(seq, dtype=F32)[:, None] * inv_freq[None, :]
    cos, sin = jnp.cos(ang), jnp.sin(ang)
    ones = jnp.ones((seq, HEAD_DIM - ROT_DIM), F32)
    zeros = jnp.zeros((seq, HEAD_DIM - ROT_DIM), F32)
    zh = jnp.zeros((seq, half), F32)
    c = jnp.concatenate([cos, cos, ones], axis=1)
    s1 = jnp.concatenate([zh, sin, zeros], axis=1)
    s2 = jnp.concatenate([-sin, zh, zeros], axis=1)
    rep = LANES // HEAD_DIM
    return tuple(jnp.tile(t, (1, rep)) for t in (c, s1, s2))


def _residue_major(t, dil):
    s = t.shape[0]
    return t.reshape(s // dil, dil, *t.shape[1:]).swapaxes(0, 1).reshape(t.shape)


def _project_qkv(xb, w_seg, tables, seq, name, q_scale, v_feature_major=False):
    tokens = xb.shape[0]
    tm = min(QKV_TM, seq)
    per_seq = seq // tm
    tab_spec = pl.BlockSpec((tm, LANES), lambda i: (i % per_seq, 0))
    out_w = 2 * SEG_W if v_feature_major else QKV_W
    out_specs = [pl.BlockSpec((tm, out_w), lambda i: (i, 0))]
    out_shape = [jax.ShapeDtypeStruct((tokens, out_w), BF16)]
    if v_feature_major:
        out_specs.append(pl.BlockSpec((None, SEG_W, tm), lambda i: (i // per_seq, 0, i % per_seq)))
        out_shape.append(jax.ShapeDtypeStruct((tokens // seq, SEG_W, seq), BF16))
    outs = pl.pallas_call(
        functools.partial(_qkv_kernel, q_scale=q_scale),
        grid=(tokens // tm,),
        in_specs=[pl.BlockSpec((tm, D_MODEL), lambda i: (i, 0)),
                  pl.BlockSpec((D_MODEL, QKV_W), lambda i: (0, 0)),
                  tab_spec, tab_spec, tab_spec],
        out_specs=out_specs,
        out_shape=out_shape,
        compiler_params=_cparams(1),
        name=name,
    )(xb, w_seg, *tables)
    return outs if v_feature_major else outs[0]


def _nt_dot(a, b):
    return lax.dot_general(a, b, (((1,), (1,)), ((), ())), preferred_element_type=F32)


def _diff_kernel(q_ref, k_ref, vt_ref, lq1_ref, lk1_ref, lq2_ref, lk2_ref, g_ref, o_ref, *, tq):
    i = pl.program_id(1)
    lane = lax.broadcasted_iota(jnp.int32, (tq, LANES), 1)
    qs = []
    for h in range(DA_HEADS):
        q = q_ref[:, h * LANES:(h + 1) * LANES]
        zero = jnp.zeros_like(q)
        qs.append((jnp.where(lane < HEAD_DIM, q, zero), jnp.where(lane >= HEAD_DIM, q, zero)))

    def step(j, carry, masked):
        start = pl.multiple_of(j * tq, tq)
        if masked:
            key = lax.broadcasted_iota(jnp.int32, (tq, tq), 0)
            qry = lax.broadcasted_iota(jnp.int32, (tq, tq), 1)
            keep = key <= qry
        scores = []
        for h in range(DA_HEADS):
            kb = k_ref[pl.ds(start, tq), h * LANES:(h + 1) * LANES]
            for comp in range(2):
                scores.append(_nt_dot(kb, qs[h][comp]))
        probs = []
        for idx, st in enumerate(scores):
            m, l, acc = carry[idx]
            if masked:
                st = jnp.where(keep, st, -jnp.inf)
            m_new = jnp.maximum(m, jnp.max(st, axis=0, keepdims=True))
            a = jnp.exp2(m - m_new)
            p = jnp.exp2(st - m_new)
            l = a * l + jnp.sum(p, axis=0, keepdims=True)
            probs.append((m_new, l, a, p.astype(BF16)))
        out = []
        for idx, (m_new, l, a, p) in enumerate(probs):
            h = idx // 2
            vt = vt_ref[h * LANES:(h + 1) * LANES, pl.ds(start, tq)]
            acc = a * carry[idx][2] + jnp.dot(vt, p, preferred_element_type=F32)
            out.append((m_new, l, acc))
        return tuple(out)

    init = tuple((jnp.full((1, tq), -jnp.inf, F32), jnp.zeros((1, tq), F32), jnp.zeros((LANES, tq), F32))
                 for _ in range(2 * DA_HEADS))
    carry = lax.fori_loop(0, i, lambda j, c: step(j, c, False), init)
    carry = step(i, carry, True)

    lam = (jnp.exp(jnp.sum(lq1_ref[...] * lk1_ref[...], axis=1, keepdims=True))
           - jnp.exp(jnp.sum(lq2_ref[...] * lk2_ref[...], axis=1, keepdims=True)) + LAMBDA_INIT)
    for h in range(DA_HEADS):
        (_, l1, a1), (_, l2, a2) = carry[2 * h], carry[2 * h + 1]
        o = (a1 / l1 - lam * (a2 / l2)).T
        ms = jnp.mean(o * o, axis=1, keepdims=True)
        o = o * lax.rsqrt(ms + EPS) * g_ref[...]
        o_ref[:, h * LANES:(h + 1) * LANES] = (o * (1.0 - LAMBDA_INIT)).astype(BF16)


def _diff_attention(qk3, vt3, lam_q1, lam_k1, lam_q2, lam_k2, subln_g):
    b, s, _ = qk3.shape
    tq = min(ATT_TQ, s)
    vec = lambda n: pl.BlockSpec((1, n), lambda bb, i: (0, 0))
    return pl.pallas_call(
        functools.partial(_diff_kernel, tq=tq),
        grid=(b, s // tq),
        in_specs=[
            pl.BlockSpec((None, tq, SEG_W), lambda bb, i: (bb, i, 0)),
            pl.BlockSpec((None, s, SEG_W), lambda bb, i: (bb, 0, 1)),
            pl.BlockSpec((None, SEG_W, s), lambda bb, i: (bb, 0, 0)),
            vec(HEAD_DIM), vec(HEAD_DIM), vec(HEAD_DIM), vec(HEAD_DIM), vec(2 * HEAD_DIM),
        ],
        out_specs=pl.BlockSpec((None, tq, SEG_W), lambda bb, i: (bb, i, 0)),
        out_shape=jax.ShapeDtypeStruct((b, s, SEG_W), BF16),
        compiler_params=_cparams(2),
        name="diff_attn",
    )(qk3, qk3, vt3, lam_q1.reshape(1, -1), lam_k1.reshape(1, -1), lam_q2.reshape(1, -1),
      lam_k2.reshape(1, -1), subln_g.reshape(1, -1))


def _dil_kernel(q_ref, kp_ref, kc_ref, vp_ref, vc_ref, o_ref, lse_ref, *, rel):
    n = pl.program_id(2)
    qi = lax.broadcasted_iota(jnp.int32, (QBLK, 2 * QBLK), 0)
    kj = lax.broadcasted_iota(jnp.int32, (QBLK, 2 * QBLK), 1)
    dist = qi + QBLK - kj
    valid = (dist >= 0) & (dist <= rel) & ((kj >= QBLK) | (n > 0))
    lane = lax.broadcasted_iota(jnp.int32, (QBLK, LANES), 1)
    low = lane < HEAD_DIM
    slabs = [slice(p * LANES, (p + 1) * LANES) for p in range(SEG_W // LANES)]
    scores = []
    for sl in slabs:
        q2 = q_ref[:, sl]
        k2 = jnp.concatenate([kp_ref[:, sl], kc_ref[:, sl]], axis=0)
        for sel in (low, jnp.logical_not(low)):
            scores.append(_nt_dot(jnp.where(sel, q2, jnp.zeros_like(q2)), k2))
    probs = []
    for s in scores:
        s = jnp.where(valid, s, -jnp.inf)
        m = jnp.max(s, axis=1, keepdims=True)
        pe = jnp.exp(s - m)
        den = jnp.sum(pe, axis=1, keepdims=True)
        probs.append((pe.astype(BF16), den, m + jnp.log(den)))
    for p, sl in enumerate(slabs):
        v2 = jnp.concatenate([vp_ref[:, sl], vc_ref[:, sl]], axis=0)
        outs = [jnp.dot(pe, v2, preferred_element_type=F32) / den for pe, den, _ in probs[2 * p:2 * p + 2]]
        o_ref[:, sl] = jnp.where(low, outs[0], outs[1]).astype(BF16)
        lse_ref[:, sl] = jnp.where(low, probs[2 * p][2], probs[2 * p + 1][2])


def _dilated_group(qkv, batch, seq, group):
    window, dil = DIL_PAIRS[group]
    rel = window // dil
    length = seq // dil
    nb = length // QBLK
    view = qkv.reshape(batch, dil, length, QKV_W)

    def spec(seg, prev):
        if prev:
            return pl.BlockSpec((None, None, QBLK, SEG_W), lambda b, r, n: (b, r, jnp.maximum(n - 1, 0), seg))
        return pl.BlockSpec((None, None, QBLK, SEG_W), lambda b, r, n: (b, r, n, seg))

    out_spec = pl.BlockSpec((None, None, QBLK, SEG_W), lambda b, r, n: (b, r, n, 0))
    o, lse = pl.pallas_call(
        functools.partial(_dil_kernel, rel=rel),
        grid=(batch, dil, nb),
        in_specs=[spec(0, False), spec(1, True), spec(1, False), spec(2, True), spec(2, False)],
        out_specs=[out_spec, out_spec],
        out_shape=[jax.ShapeDtypeStruct((batch, dil, length, SEG_W), BF16),
                   jax.ShapeDtypeStruct((batch, dil, length, SEG_W), F32)],
        compiler_params=_cparams(3),
        name=f"dilated{group}",
    )(view, view, view, view, view)
    tokens = batch * seq
    back = lambda t: t.swapaxes(1, 2).reshape(tokens, SEG_W)
    return back(o), back(lse)


def _layer_norm(z, g, b):
    mu = jnp.mean(z, axis=1, keepdims=True)
    zc = z - mu
    var = jnp.mean(zc * zc, axis=1, keepdims=True)
    return zc * lax.rsqrt(var + EPS) * g + b


def _merge_kernel(ya_ref, o1_ref, o2_ref, o3_ref, l1_ref, l2_ref, l3_ref, x_ref,
                  wg_ref, bg_ref, woa_ref, wob_ref, wo_ref, g_ref, b_ref, wrh_ref, wrl_ref, br_ref,
                  h_ref, topi_ref, topw_ref, rank_ref, cnt_ref, carry_ref):
    i = pl.program_id(0)
    tm = x_ref.shape[0]

    @pl.when(i == 0)
    def _():
        carry_ref[...] = jnp.zeros_like(carry_ref)

    x = x_ref[...]
    gates = jax.nn.sigmoid(jnp.dot(x.astype(BF16), wg_ref[...], preferred_element_type=F32) + bg_ref[...])
    la, lb, lc = l1_ref[...], l2_ref[...], l3_ref[...]
    mx = jnp.maximum(jnp.maximum(la, lb), lc)
    ea, eb, ec = jnp.exp(la - mx), jnp.exp(lb - mx), jnp.exp(lc - mx)
    yb = (ea * o1_ref[...].astype(F32) + eb * o2_ref[...].astype(F32) + ec * o3_ref[...].astype(F32)) / (ea + eb + ec)
    pa = jnp.dot(ya_ref[...], woa_ref[...], preferred_element_type=F32)
    pb = jnp.dot(yb.astype(BF16), wob_ref[...], preferred_element_type=F32)
    merged = gates[:, :D_MODEL] * pa + gates[:, D_MODEL:] * pb
    mix = jnp.dot(merged.astype(BF16), wo_ref[...], preferred_element_type=F32)
    h = _layer_norm(DN_ALPHA * x + mix, g_ref[...], b_ref[...])
    h_ref[...] = h

    hh = h.astype(BF16)
    hl = (h - hh.astype(F32)).astype(BF16)
    logits = (jnp.dot(hh, wrh_ref[...], preferred_element_type=F32)
              + jnp.dot(hl, wrh_ref[...], preferred_element_type=F32)
              + jnp.dot(hh, wrl_ref[...], preferred_element_type=F32) + br_ref[...])
    lane = lax.broadcasted_iota(jnp.int32, (tm, LANES), 1)
    lg = jnp.where(lane < N_EXPERTS, logits, -jnp.inf)
    vals, idxs = [], []
    for _ in range(TOP_K):
        mv = jnp.max(lg, axis=1, keepdims=True)
        ik = jnp.min(jnp.where(lg == mv, lane, LANES), axis=1, keepdims=True)
        vals.append(mv)
        idxs.append(ik)
        lg = jnp.where(lane == ik, -jnp.inf, lg)
    es = [jnp.exp(v - vals[0]) for v in vals]
    tot = es[0] + es[1] + es[2] + es[3]

    onehot = jnp.zeros((tm, LANES), F32)
    for ik in idxs:
        onehot = onehot + (lane == ik).astype(F32)
    r_i = lax.broadcasted_iota(jnp.int32, (tm, tm), 0)
    c_i = lax.broadcasted_iota(jnp.int32, (tm, tm), 1)
    tri = (r_i > c_i).astype(BF16)
    pre = jnp.dot(tri, onehot.astype(BF16), preferred_element_type=F32) + carry_ref[...]

    topi = jnp.zeros((tm, LANES), jnp.int32)
    topw = jnp.zeros((tm, LANES), F32)
    rank = jnp.zeros((tm, LANES), F32)
    for k in range(TOP_K):
        rk = jnp.sum(jnp.where(lane == idxs[k], pre, 0.0), axis=1, keepdims=True)
        topi = jnp.where(lane == k, idxs[k], topi)
        topw = jnp.where(lane == k, es[k] / tot, topw)
        rank = jnp.where(lane == k, rk, rank)
    topi_ref[...] = topi
    topw_ref[...] = topw
    rank_ref[...] = rank.astype(jnp.int32)
    carry_ref[...] = carry_ref[...] + jnp.sum(onehot, axis=0, keepdims=True)
    cnt_ref[...] = carry_ref[...]


def _merge(ya, dil_outs, x2, w_gate_b, b_gate, w_oa_b, w_ob_b, w_o_b, ln1_g, ln1_b, w_router, b_router):
    tokens = x2.shape[0]
    tm = MERGE_TM
    wr = jnp.zeros((D_MODEL, LANES), F32).at[:, :N_EXPERTS].set(w_router)
    wr_hi = wr.astype(BF16)
    wr_lo = (wr - wr_hi.astype(F32)).astype(BF16)
    br = jnp.zeros((1, LANES), F32).at[0, :N_EXPERTS].set(b_router)
    row = lambda w: pl.BlockSpec((tm, w), lambda i: (i, 0))
    full = lambda a: pl.BlockSpec(a.shape, lambda i: (0,) * a.ndim)
    (o1, l1), (o2, l2), (o3, l3) = dil_outs
    bg = b_gate.reshape(1, -1)
    g1 = ln1_g.reshape(1, -1)
    b1 = ln1_b.reshape(1, -1)
    lane_out = lambda dt: jax.ShapeDtypeStruct((tokens, LANES), dt)
    return pl.pallas_call(
        _merge_kernel,
        grid=(tokens // tm,),
        in_specs=[row(SEG_W), row(SEG_W), row(SEG_W), row(SEG_W), row(SEG_W), row(SEG_W), row(SEG_W),
                  row(D_MODEL), full(w_gate_b), full(bg), full(w_oa_b), full(w_ob_b), full(w_o_b), full(g1), full(b1),
                  full(wr_hi), full(wr_lo), full(br)],
        out_specs=[row(D_MODEL), row(LANES), row(LANES), row(LANES), pl.BlockSpec((1, LANES), lambda i: (0, 0))],
        out_shape=[jax.ShapeDtypeStruct((tokens, D_MODEL), F32), lane_out(jnp.int32), lane_out(F32),
                   lane_out(jnp.int32), jax.ShapeDtypeStruct((1, LANES), F32)],
        scratch_shapes=[pltpu.VMEM((1, LANES), F32)],
        compiler_params=_cparams(1),
        name="merge",
    )(ya, o1, o2, o3, l1, l2, l3, x2, w_gate_b, bg, w_oa_b, w_ob_b, w_o_b, g1, b1, wr_hi, wr_lo, br)


def _dispatch_kernel(dest_ref, h_ref, init_ref, rows_ref, sem):
    del init_ref
    i = pl.program_id(0)
    tm = h_ref.shape[0]

    def body(t, c):
        base = (i * tm + t) * TOP_K
        for k in range(TOP_K):
            d = dest_ref[base + k]
            pltpu.make_async_copy(h_ref.at[pl.ds(t, 1)], rows_ref.at[pl.ds(d, 1)], sem).start()
        return c

    lax.fori_loop(0, tm, body, 0)
    for _ in range(TOP_K):
        pltpu.make_async_copy(h_ref, rows_ref.at[pl.ds(0, tm)], sem).wait()


def _dispatch(h, dest, n_rows):
    tokens = h.shape[0]
    tm = MOE_TM
    return pl.pallas_call(
        _dispatch_kernel,
        grid_spec=pltpu.PrefetchScalarGridSpec(
            num_scalar_prefetch=1,
            grid=(tokens // tm,),
            in_specs=[pl.BlockSpec((tm, D_MODEL), lambda i, d: (i, 0)),
                      pl.BlockSpec(memory_space=pl.ANY)],
            out_specs=pl.BlockSpec(memory_space=pl.ANY),
            scratch_shapes=[pltpu.SemaphoreType.DMA(())],
        ),
        out_shape=jax.ShapeDtypeStruct((n_rows, D_MODEL), F32),
        input_output_aliases={2: 0},
        compiler_params=_cparams(1),
        name="dispatch",
    )(dest, h, jnp.zeros((n_rows, D_MODEL), F32))


def _expert_kernel(be_ref, nu_ref, x_ref, wgu_ref, bgu_ref, wd_ref, bd_ref, y_ref):
    i = pl.program_id(0)

    @pl.when(i < nu_ref[0])
    def _():
        xb = x_ref[...].astype(BF16)
        chunk = 512
        cols = [slice(c * chunk, (c + 1) * chunk) for c in range(D_FF // chunk)]
        pre = []
        for gs in cols:
            us = slice(D_FF + gs.start, D_FF + gs.stop)
            gate = jnp.dot(xb, wgu_ref[:, gs].astype(BF16), preferred_element_type=F32) + bgu_ref[:, gs]
            up = jnp.dot(xb, wgu_ref[:, us].astype(BF16), preferred_element_type=F32) + bgu_ref[:, us]
            pre.append((gate, up))
        acc = jnp.zeros(y_ref.shape, F32)
        for gs, (gate, up) in zip(cols, pre):
            gate = jnp.minimum(gate, SWIGLU_LIMIT)
            up = jnp.clip(up, -SWIGLU_LIMIT, SWIGLU_LIMIT)
            act = (up + 1.0) * gate * jax.nn.sigmoid(SWIGLU_ALPHA * gate)
            acc = acc + jnp.dot(act.astype(BF16), wd_ref[gs, :].astype(BF16), preferred_element_type=F32)
        y_ref[...] = acc + bd_ref[...]

    @pl.when(i >= nu_ref[0])
    def _():
        y_ref[...] = jnp.zeros_like(y_ref)


def _experts(xrows, block_e, n_used, w_gu, b_gu, w_down, b_down):
    n_rows = xrows.shape[0]
    bm = MOE_BM
    return pl.pallas_call(
        _expert_kernel,
        grid_spec=pltpu.PrefetchScalarGridSpec(
            num_scalar_prefetch=2,
            grid=(n_rows // bm,),
            in_specs=[
                pl.BlockSpec((bm, D_MODEL), lambda i, be, nu: (jnp.minimum(i, nu[0] - 1), 0)),
                pl.BlockSpec((None, D_MODEL, 2 * D_FF), lambda i, be, nu: (be[i], 0, 0)),
                pl.BlockSpec((None, 1, 2 * D_FF), lambda i, be, nu: (be[i], 0, 0)),
                pl.BlockSpec((None, D_FF, D_MODEL), lambda i, be, nu: (be[i], 0, 0)),
                pl.BlockSpec((None, 1, D_MODEL), lambda i, be, nu: (be[i], 0, 0)),
            ],
            out_specs=pl.BlockSpec((bm, D_MODEL), lambda i, be, nu: (i, 0)),
        ),
        out_shape=jax.ShapeDtypeStruct((n_rows, D_MODEL), F32),
        compiler_params=_cparams(1),
        name="experts",
    )(block_e, n_used, xrows, w_gu, b_gu.reshape(N_EXPERTS, 1, -1), w_down, b_down.reshape(N_EXPERTS, 1, -1))


def _combine_kernel(dest_ref, yrows_ref, w_ref, h_ref, g_ref, b_ref, o_ref, buf, sem):
    i = pl.program_id(0)
    tm = h_ref.shape[0]

    def body(t, c):
        base = (i * tm + t) * TOP_K
        for k in range(TOP_K):
            d = dest_ref[base + k]
            pltpu.make_async_copy(yrows_ref.at[pl.ds(d, 1)], buf.at[k, pl.ds(t, 1)], sem).start()
        return c

    lax.fori_loop(0, tm, body, 0)
    for k in range(TOP_K):
        pltpu.make_async_copy(yrows_ref.at[pl.ds(0, tm)], buf.at[k], sem).wait()
    w = w_ref[...]
    f = jnp.zeros(h_ref.shape, F32)
    for k in range(TOP_K):
        f = f + buf[k] * w[:, k:k + 1]
    o_ref[...] = _layer_norm(DN_ALPHA * h_ref[...] + f, g_ref[...], b_ref[...])


def _combine(yrows, dest, topw, h, ln2_g, ln2_b):
    tokens = h.shape[0]
    tm = MOE_TM
    return pl.pallas_call(
        _combine_kernel,
        grid_spec=pltpu.PrefetchScalarGridSpec(
            num_scalar_prefetch=1,
            grid=(tokens // tm,),
            in_specs=[pl.BlockSpec(memory_space=pl.ANY),
                      pl.BlockSpec((tm, LANES), lambda i, d: (i, 0)),
                      pl.BlockSpec((tm, D_MODEL), lambda i, d: (i, 0)),
                      pl.BlockSpec((1, D_MODEL), lambda i, d: (0, 0)),
                      pl.BlockSpec((1, D_MODEL), lambda i, d: (0, 0))],
            out_specs=pl.BlockSpec((tm, D_MODEL), lambda i, d: (i, 0)),
            scratch_shapes=[pltpu.VMEM((TOP_K, tm, D_MODEL), F32), pltpu.SemaphoreType.DMA(())],
        ),
        out_shape=jax.ShapeDtypeStruct((tokens, D_MODEL), F32),
        compiler_params=_cparams(1),
        name="combine",
    )(dest, yrows, topw, h, ln2_g.reshape(1, -1), ln2_b.reshape(1, -1))


def _routing_tables(topi, rank, cnt, tokens):
    bm = MOE_BM
    counts = cnt[0, :N_EXPERTS].astype(jnp.int32)
    padded = (counts + bm - 1) // bm * bm
    pend = jnp.cumsum(padded)
    pstart = pend - padded
    e = topi[:, :TOP_K]
    sel = e[:, :, None] == jnp.arange(N_EXPERTS, dtype=jnp.int32)[None, None, :]
    dest = rank[:, :TOP_K] + jnp.sum(jnp.where(sel, pstart[None, None, :], 0), axis=-1)
    n_rows = tokens * TOP_K + N_EXPERTS * bm
    n_blocks = n_rows // bm
    starts = jnp.arange(n_blocks, dtype=jnp.int32) * bm
    block_e = jnp.minimum(jnp.sum((pend[None, :] <= starts[:, None]).astype(jnp.int32), axis=1), N_EXPERTS - 1)
    n_used = (pend[-1] // bm).reshape(1)
    return dest.reshape(-1).astype(jnp.int32), block_e.astype(jnp.int32), n_used.astype(jnp.int32), n_rows


def kernel(x, w_in, b_gate, lam_q1, lam_k1, lam_q2, lam_k2, subln_g, w_oa, w_ob, w_o, ln1_g, ln1_b,
           w_router, b_router, w_gu, b_gu, w_down, b_down, ln2_g, ln2_b):
    batch, seq, d = x.shape
    tokens = batch * seq
    h = x.reshape(tokens, d)
    tables = _rope_lane_tables(seq)
    for l in range(DEPTH):
        w_b = w_in[l].astype(BF16)
        hb = h.astype(BF16)
        qk_a, vt_a = _project_qkv(hb, w_b[:, :QKV_W], tables, seq, "qkv_diff",
                                  q_scale=HEAD_DIM ** -0.5 * math.log2(math.e), v_feature_major=True)
        ya = _diff_attention(qk_a.reshape(batch, seq, 2 * SEG_W), vt_a, lam_q1[l], lam_k1[l], lam_q2[l],
                             lam_k2[l], subln_g[l]).reshape(tokens, -1)
        dil = []
        for g, (_, dilation) in enumerate(DIL_PAIRS):
            w_g = w_b[:, (g + 1) * QKV_W:(g + 2) * QKV_W]
            if dilation == 1:
                hb_g, tab_g = hb, tables
            else:
                hb_g = hb.reshape(batch, seq // dilation, dilation, d).swapaxes(1, 2).reshape(tokens, d)
                tab_g = tuple(_residue_major(t, dilation) for t in tables)
            qkv_g = _project_qkv(hb_g, w_g, tab_g, seq, f"qkv_dil{g}", q_scale=HEAD_DIM ** -0.5)
            dil.append(_dilated_group(qkv_g, batch, seq, g))
        h1, topi, topw, rank, cnt = _merge(ya, dil, h, w_b[:, 4 * QKV_W:], b_gate[l], w_oa[l].astype(BF16),
                                           w_ob[l].astype(BF16), w_o[l].astype(BF16), ln1_g[l], ln1_b[l],
                                           w_router[l], b_router[l])
        dest, block_e, n_used, n_rows = _routing_tables(topi, rank, cnt, tokens)
        xrows = _dispatch(h1, dest, n_rows)
        yrows = _experts(xrows, block_e, n_used, w_gu[l], b_gu[l], w_down[l], b_down[l])
        h = _combine(yrows, dest, topw, h1, ln2_g[l], ln2_b[l])
    return h.reshape(batch, seq, d)
```

```python
import functools
import math

import jax
import jax.numpy as jnp
from jax import lax
from jax.experimental import pallas as pl
from jax.experimental.pallas import tpu as pltpu

F32 = jnp.float32
BF16 = jnp.bfloat16

D_MODEL = 1024
HEAD_DIM = 64
ROT_DIM = HEAD_DIM // 4
ROPE_THETA = 500000.0
QBLK = 128
DA_HEADS = 4
DIL_PAIRS = ((128, 1), (512, 4), (2048, 16))
SEG_W = 512
QKV_W = 3 * SEG_W
N_EXPERTS = 32
TOP_K = 4
D_FF = D_MODEL
SWIGLU_ALPHA = 1.702
SWIGLU_LIMIT = 7.0
DEPTH = 1
DN_ALPHA = (2 * DEPTH) ** 0.25
EPS = 1e-5
LAMBDA_INIT = 0.8 - 0.6 * math.exp(-0.3 * 0)

LANES = 128
QKV_TM = 512
QKV_CHUNK = 256
ATT_TQ = 256
MERGE_TM = 256
MOE_BM = 256
MOE_TM = 256
VMEM_LIMIT = 52 * 1024 * 1024


def _cparams(n_axes):
    return pltpu.CompilerParams(dimension_semantics=("arbitrary",) * n_axes,
                                vmem_limit_bytes=VMEM_LIMIT)


def _qkv_kernel(x_ref, w_ref, c_ref, s1_ref, s2_ref, o_ref, *maybe_vt_ref, q_scale):
    tm = x_ref.shape[0]
    for rc in range(tm // QKV_CHUNK):
        rows = slice(rc * QKV_CHUNK, (rc + 1) * QKV_CHUNK)
        xb = x_ref[rows, :]
        c, s1, s2 = c_ref[rows, :], s1_ref[rows, :], s2_ref[rows, :]
        for seg in range(3):
            acc = jnp.dot(xb, w_ref[:, seg * SEG_W:(seg + 1) * SEG_W], preferred_element_type=F32)
            if seg == 2:
                if maybe_vt_ref:
                    maybe_vt_ref[0][:, rows] = acc.T.astype(BF16)
                else:
                    o_ref[rows, seg * SEG_W:(seg + 1) * SEG_W] = acc.astype(BF16)
                continue
            for k in range(SEG_W // LANES):
                t = acc[:, k * LANES:(k + 1) * LANES]
                r = t * c + pltpu.roll(t, ROT_DIM // 2, 1) * s1 + pltpu.roll(t, LANES - ROT_DIM // 2, 1) * s2
                if seg == 0:
                    r = r * q_scale
                lo = seg * SEG_W + k * LANES
                o_ref[rows, lo:lo + LANES] = r.astype(BF16)


def _rope_lane_tables(seq):
    half = ROT_DIM // 2
    inv_freq = ROPE_THETA ** (-jnp.arange(0, ROT_DIM, 2, dtype=F32) / ROT_DIM)
    ang = jnp.arange(seq, dtype=F32)[:, None] * inv_freq[None, :]
    cos, sin = jnp.cos(ang), jnp.sin(ang)
    ones = jnp.ones((seq, HEAD_DIM - ROT_DIM), F32)
    zeros = jnp.zeros((seq, HEAD_DIM - ROT_DIM), F32)
    zh = jnp.zeros((seq, half), F32)
    c = jnp.concatenate([cos, cos, ones], axis=1)
    s1 = jnp.concatenate([zh, sin, zeros], axis=1)
    s2 = jnp.concatenate([-sin, zh, zeros], axis=1)
    rep = LANES // HEAD_DIM
    return tuple(jnp.tile(t, (1, rep)) for t in (c, s1, s2))


def _residue_major(t, dil):
    s = t.shape[0]
    return t.reshape(s // dil, dil, *t.shape[1:]).swapaxes(0, 1).reshape(t.shape)


def _project_qkv(xb, w_seg, tables, seq, name, q_scale, v_feature_major=False):
    tokens = xb.shape[0]
    tm = min(QKV_TM, seq)
    per_seq = seq // tm
    tab_spec = pl.BlockSpec((tm, LANES), lambda i: (i % per_seq, 0))
    out_w = 2 * SEG_W if v_feature_major else QKV_W
    out_specs = [pl.BlockSpec((tm, out_w), lambda i: (i, 0))]
    out_shape = [jax.ShapeDtypeStruct((tokens, out_w), BF16)]
    if v_feature_major:
        out_specs.append(pl.BlockSpec((None, SEG_W, tm), lambda i: (i // per_seq, 0, i % per_seq)))
        out_shape.append(jax.ShapeDtypeStruct((tokens // seq, SEG_W, seq), BF16))
    outs = pl.pallas_call(
        functools.partial(_qkv_kernel, q_scale=q_scale),
        grid=(tokens // tm,),
        in_specs=[pl.BlockSpec((tm, D_MODEL), lambda i: (i, 0)),
                  pl.BlockSpec((D_MODEL, QKV_W), lambda i: (0, 0)),
                  tab_spec, tab_spec, tab_spec],
        out_specs=out_specs,
        out_shape=out_shape,
        compiler_params=_cparams(1),
        name=name,
    )(xb, w_seg, *tables)
    return outs if v_feature_major else outs[0]


def _nt_dot(a, b):
    return lax.dot_general(a, b, (((1,), (1,)), ((), ())), preferred_element_type=F32)


def _diff_kernel(q_ref, k_ref, vt_ref, lq1_ref, lk1_ref, lq2_ref, lk2_ref, g_ref, o_ref, *, tq):
    i = pl.program_id(1)
    lane = lax.broadcasted_iota(jnp.int32, (tq, LANES), 1)
    qs = []
    for h in range(DA_HEADS):
        q = q_ref[:, h * LANES:(h + 1) * LANES]
        zero = jnp.zeros_like(q)
        qs.append((jnp.where(lane < HEAD_DIM, q, zero), jnp.where(lane >= HEAD_DIM, q, zero)))

    def step(j, carry, masked):
        start = pl.multiple_of(j * tq, tq)
        if masked:
            key = lax.broadcasted_iota(jnp.int32, (tq, tq), 0)
            qry = lax.broadcasted_iota(jnp.int32, (tq, tq), 1)
            keep = key <= qry
        scores = []
        for h in range(DA_HEADS):
            kb = k_ref[pl.ds(start, tq), h * LANES:(h + 1) * LANES]
            for comp in range(2):
                scores.append(_nt_dot(kb, qs[h][comp]))
        probs = []
        for idx, st in enumerate(scores):
            m, l, acc = carry[idx]
            if masked:
                st = jnp.where(keep, st, -jnp.inf)
            m_new = jnp.maximum(m, jnp.max(st, axis=0, keepdims=True))
            a = jnp.exp2(m - m_new)
            p = jnp.exp2(st - m_new)
            l = a * l + jnp.sum(p, axis=0, keepdims=True)
            probs.append((m_new, l, a, p.astype(BF16)))
        out = []
        for idx, (m_new, l, a, p) in enumerate(probs):
            h = idx // 2
            vt = vt_ref[h * LANES:(h + 1) * LANES, pl.ds(start, tq)]
            acc = a * carry[idx][2] + jnp.dot(vt, p, preferred_element_type=F32)
            out.append((m_new, l, acc))
        return tuple(out)

    init = tuple((jnp.full((1, tq), -jnp.inf, F32), jnp.zeros((1, tq), F32), jnp.zeros((LANES, tq), F32))
                 for _ in range(2 * DA_HEADS))
    carry = lax.fori_loop(0, i, lambda j, c: step(j, c, False), init)
    carry = step(i, carry, True)

    lam = (jnp.exp(jnp.sum(lq1_ref[...] * lk1_ref[...], axis=1, keepdims=True))
           - jnp.exp(jnp.sum(lq2_ref[...] * lk2_ref[...], axis=1, keepdims=True)) + LAMBDA_INIT)
    for h in range(DA_HEADS):
        (_, l1, a1), (_, l2, a2) = carry[2 * h], carry[2 * h + 1]
        o = (a1 / l1 - lam * (a2 / l2)).T
        ms = jnp.mean(o * o, axis=1, keepdims=True)
        o = o * lax.rsqrt(ms + EPS) * g_ref[...]
        o_ref[:, h * LANES:(h + 1) * LANES] = (o * (1.0 - LAMBDA_INIT)).astype(BF16)


def _diff_attention(qk3, vt3, lam_q1, lam_k1, lam_q2, lam_k2, subln_g):
    b, s, _ = qk3.shape
    tq = min(ATT_TQ, s)
    vec = lambda n: pl.BlockSpec((1, n), lambda bb, i: (0, 0))
    return pl.pallas_call(
        functools.partial(_diff_kernel, tq=tq),
        grid=(b, s // tq),
        in_specs=[
            pl.BlockSpec((None, tq, SEG_W), lambda bb, i: (bb, i, 0)),
            pl.BlockSpec((None, s, SEG_W), lambda bb, i: (bb, 0, 1)),
            pl.BlockSpec((None, SEG_W, s), lambda bb, i: (bb, 0, 0)),
            vec(HEAD_DIM), vec(HEAD_DIM), vec(HEAD_DIM), vec(HEAD_DIM), vec(2 * HEAD_DIM),
        ],
        out_specs=pl.BlockSpec((None, tq, SEG_W), lambda bb, i: (bb, i, 0)),
        out_shape=jax.ShapeDtypeStruct((b, s, SEG_W), BF16),
        compiler_params=_cparams(2),
        name="diff_attn",
    )(qk3, qk3, vt3, lam_q1.reshape(1, -1), lam_k1.reshape(1, -1), lam_q2.reshape(1, -1),
      lam_k2.reshape(1, -1), subln_g.reshape(1, -1))


def _dil_kernel(q_ref, kp_ref, kc_ref, vp_ref, vc_ref, o_ref, lse_ref, *, rel):
    n = pl.program_id(2)
    qi = lax.broadcasted_iota(jnp.int32, (QBLK, 2 * QBLK), 0)
    kj = lax.broadcasted_iota(jnp.int32, (QBLK, 2 * QBLK), 1)
    dist = qi + QBLK - kj
    valid = (dist >= 0) & (dist <= rel) & ((kj >= QBLK) | (n > 0))
    lane = lax.broadcasted_iota(jnp.int32, (QBLK, LANES), 1)
    low = lane < HEAD_DIM
    slabs = [slice(p * LANES, (p + 1) * LANES) for p in range(SEG_W // LANES)]
    scores = []
    for sl in slabs:
        q2 = q_ref[:, sl]
        k2 = jnp.concatenate([kp_ref[:, sl], kc_ref[:, sl]], axis=0)
        for sel in (low, jnp.logical_not(low)):
            scores.append(_nt_dot(jnp.where(sel, q2, jnp.zeros_like(q2)), k2))
    probs = []
    for s in scores:
        s = jnp.where(valid, s, -jnp.inf)
        m = jnp.max(s, axis=1, keepdims=True)
        pe = jnp.exp(s - m)
        den = jnp.sum(pe, axis=1, keepdims=True)
        probs.append((pe.astype(BF16), den, m + jnp.log(den)))
    for p, sl in enumerate(slabs):
        v2 = jnp.concatenate([vp_ref[:, sl], vc_ref[:, sl]], axis=0)
        outs = [jnp.dot(pe, v2, preferred_element_type=F32) / den for pe, den, _ in probs[2 * p:2 * p + 2]]
        o_ref[:, sl] = jnp.where(low, outs[0], outs[1]).astype(BF16)
        lse_ref[:, sl] = jnp.where(low, probs[2 * p][2], probs[2 * p + 1][2])


def _dilated_group(qkv, batch, seq, group):
    window, dil = DIL_PAIRS[group]
    rel = window // dil
    length = seq // dil
    nb = length // QBLK
    view = qkv.reshape(batch, dil, length, QKV_W)

    def spec(seg, prev):
        if prev:
            return pl.BlockSpec((None, None, QBLK, SEG_W), lambda b, r, n: (b, r, jnp.maximum(n - 1, 0), seg))
        return pl.BlockSpec((None, None, QBLK, SEG_W), lambda b, r, n: (b, r, n, seg))

    out_spec = pl.BlockSpec((None, None, QBLK, SEG_W), lambda b, r, n: (b, r, n, 0))
    o, lse = pl.pallas_call(
        functools.partial(_dil_kernel, rel=rel),
        grid=(batch, dil, nb),
        in_specs=[spec(0, False), spec(1, True), spec(1, False), spec(2, True), spec(2, False)],
        out_specs=[out_spec, out_spec],
        out_shape=[jax.ShapeDtypeStruct((batch, dil, length, SEG_W), BF16),
                   jax.ShapeDtypeStruct((batch, dil, length, SEG_W), F32)],
        compiler_params=_cparams(3),
        name=f"dilated{group}",
    )(view, view, view, view, view)
    tokens = batch * seq
    back = lambda t: t.swapaxes(1, 2).reshape(tokens, SEG_W)
    return back(o), back(lse)


def _layer_norm(z, g, b):
    mu = jnp.mean(z, axis=1, keepdims=True)
    zc = z - mu
    var = jnp.mean(zc * zc, axis=1, keepdims=True)
    return zc * lax.rsqrt(var + EPS) * g + b


def _merge_kernel(ya_ref, o1_ref, o2_ref, o3_ref, l1_ref, l2_ref, l3_ref, x_ref,
                  wg_ref, bg_ref, woa_ref, wob_ref, wo_ref, g_ref, b_ref, wrh_ref, wrl_ref, br_ref,
                  h_ref, topi_ref, topw_ref, rank_ref, cnt_ref, carry_ref):
    i = pl.program_id(0)
    tm = x_ref.shape[0]

    @pl.when(i == 0)
    def _():
        carry_ref[...] = jnp.zeros_like(carry_ref)

    x = x_ref[...]
    gates = jax.nn.sigmoid(jnp.dot(x.astype(BF16), wg_ref[...], preferred_element_type=F32) + bg_ref[...])
    la, lb, lc = l1_ref[...], l2_ref[...], l3_ref[...]
    mx = jnp.maximum(jnp.maximum(la, lb), lc)
    ea, eb, ec = jnp.exp(la - mx), jnp.exp(lb - mx), jnp.exp(lc - mx)
    yb = (ea * o1_ref[...].astype(F32) + eb * o2_ref[...].astype(F32) + ec * o3_ref[...].astype(F32)) / (ea + eb + ec)
    pa = jnp.dot(ya_ref[...], woa_ref[...], preferred_element_type=F32)
    pb = jnp.dot(yb.astype(BF16), wob_ref[...], preferred_element_type=F32)
    merged = gates[:, :D_MODEL] * pa + gates[:, D_MODEL:] * pb
    mix = jnp.dot(merged.astype(BF16), wo_ref[...], preferred_element_type=F32)
    h = _layer_norm(DN_ALPHA * x + mix, g_ref[...], b_ref[...])
    h_ref[...] = h

    hh = h.astype(BF16)
    hl = (h - hh.astype(F32)).astype(BF16)
    logits = (jnp.dot(hh, wrh_ref[...], preferred_element_type=F32)
              + jnp.dot(hl, wrh_ref[...], preferred_element_type=F32)
              + jnp.dot(hh, wrl_ref[...], preferred_element_type=F32) + br_ref[...])
    lane = lax.broadcasted_iota(jnp.int32, (tm, LANES), 1)
    lg = jnp.where(lane < N_EXPERTS, logits, -jnp.inf)
    vals, idxs = [], []
    for _ in range(TOP_K):
        mv = jnp.max(lg, axis=1, keepdims=True)
        ik = jnp.min(jnp.where(lg == mv, lane, LANES), axis=1, keepdims=True)
        vals.append(mv)
        idxs.append(ik)
        lg = jnp.where(lane == ik, -jnp.inf, lg)
    es = [jnp.exp(v - vals[0]) for v in vals]
    tot = es[0] + es[1] + es[2] + es[3]

    onehot = jnp.zeros((tm, LANES), F32)
    for ik in idxs:
        onehot = onehot + (lane == ik).astype(F32)
    r_i = lax.broadcasted_iota(jnp.int32, (tm, tm), 0)
    c_i = lax.broadcasted_iota(jnp.int32, (tm, tm), 1)
    tri = (r_i > c_i).astype(BF16)
    pre = jnp.dot(tri, onehot.astype(BF16), preferred_element_type=F32) + carry_ref[...]

    topi = jnp.zeros((tm, LANES), jnp.int32)
    topw = jnp.zeros((tm, LANES), F32)
    rank = jnp.zeros((tm, LANES), F32)
    for k in range(TOP_K):
        rk = jnp.sum(jnp.where(lane == idxs[k], pre, 0.0), axis=1, keepdims=True)
        topi = jnp.where(lane == k, idxs[k], topi)
        topw = jnp.where(lane == k, es[k] / tot, topw)
        rank = jnp.where(lane == k, rk, rank)
    topi_ref[...] = topi
    topw_ref[...] = topw
    rank_ref[...] = rank.astype(jnp.int32)
    carry_ref[...] = carry_ref[...] + jnp.sum(onehot, axis=0, keepdims=True)
    cnt_ref[...] = carry_ref[...]


def _merge(ya, dil_outs, x2, w_gate_b, b_gate, w_oa_b, w_ob_b, w_o_b, ln1_g, ln1_b, w_router, b_router):
    tokens = x2.shape[0]
    tm = MERGE_TM
    wr = jnp.zeros((D_MODEL, LANES), F32).at[:, :N_EXPERTS].set(w_router)
    wr_hi = wr.astype(BF16)
    wr_lo = (wr - wr_hi.astype(F32)).astype(BF16)
    br = jnp.zeros((1, LANES), F32).at[0, :N_EXPERTS].set(b_router)
    row = lambda w: pl.BlockSpec((tm, w), lambda i: (i, 0))
    full = lambda a: pl.BlockSpec(a.shape, lambda i: (0,) * a.ndim)
    (o1, l1), (o2, l2), (o3, l3) = dil_outs
    bg = b_gate.reshape(1, -1)
    g1 = ln1_g.reshape(1, -1)
    b1 = ln1_b.reshape(1, -1)
    lane_out = lambda dt: jax.ShapeDtypeStruct((tokens, LANES), dt)
    return pl.pallas_call(
        _merge_kernel,
        grid=(tokens // tm,),
        in_specs=[row(SEG_W), row(SEG_W), row(SEG_W), row(SEG_W), row(SEG_W), row(SEG_W), row(SEG_W),
                  row(D_MODEL), full(w_gate_b), full(bg), full(w_oa_b), full(w_ob_b), full(w_o_b), full(g1), full(b1),
                  full(wr_hi), full(wr_lo), full(br)],
        out_specs=[row(D_MODEL), row(LANES), row(LANES), row(LANES), pl.BlockSpec((1, LANES), lambda i: (0, 0))],
        out_shape=[jax.ShapeDtypeStruct((tokens, D_MODEL), F32), lane_out(jnp.int32), lane_out(F32),
                   lane_out(jnp.int32), jax.ShapeDtypeStruct((1, LANES), F32)],
        scratch_shapes=[pltpu.VMEM((1, LANES), F32)],
        compiler_params=_cparams(1),
        name="merge",
    )(ya, o1, o2, o3, l1, l2, l3, x2, w_gate_b, bg, w_oa_b, w_ob_b, w_o_b, g1, b1, wr_hi, wr_lo, br)


def _expert_kernel(be_ref, src_ref, h_hbm, wgu_ref, bgu_ref, wd_ref, bd_ref, y_hbm, xbuf, ybuf, gsem, ssem,
                   *, tokens, n_blocks):
    del be_ref
    i = pl.program_id(0)
    last = pl.num_programs(0) - 1
    slot = lax.rem(i, 2)
    other = 1 - slot
    bm = xbuf.shape[1]

    def gather_wait(s):
        pltpu.make_async_copy(h_hbm.at[pl.ds(0, bm)], xbuf.at[s], gsem.at[s]).wait()

    def scatter_wait(s):
        pltpu.make_async_copy(ybuf.at[s], y_hbm.at[pl.ds(0, bm)], ssem.at[s]).wait()

    def gather_row(base, s, r):
        tok = jnp.bitwise_and(src_ref[base + r], tokens - 1)
        pltpu.make_async_copy(h_hbm.at[pl.ds(tok, 1)], xbuf.at[s, pl.ds(r, 1)], gsem.at[s]).start()

    def scatter_row(base, s, r):
        pltpu.make_async_copy(ybuf.at[s, pl.ds(r, 1)], y_hbm.at[pl.ds(src_ref[base + r], 1)], ssem.at[s]).start()

    @pl.when(i == 0)
    def _():
        ybuf[...] = jnp.zeros_like(ybuf)
        for r in range(bm):
            gather_row(0, 0, r)

    gather_wait(slot)

    @pl.when(i >= 1)
    def _():
        scatter_wait(slot)

    xb = xbuf[slot].astype(BF16)
    gbase = jnp.minimum(i + 1, n_blocks - 1) * bm
    sbase = jnp.where(i == 0, n_blocks - 1, jnp.minimum(i - 1, n_blocks - 1)) * bm
    jobs = [(kind, r) for r in range(bm) for kind in ("gather", "scatter")]

    def start_rows(count):
        for _ in range(count):
            kind, r = jobs.pop(0)
            (gather_row(gbase, other, r) if kind == "gather" else scatter_row(sbase, other, r))

    chunk = 512
    cols = [slice(c * chunk, (c + 1) * chunk) for c in range(D_FF // chunk)]
    per_phase = 2 * bm // (2 * len(cols))
    pre = []
    for gs in cols:
        us = slice(D_FF + gs.start, D_FF + gs.stop)
        gate = jnp.dot(xb, wgu_ref[:, gs].astype(BF16), preferred_element_type=F32) + bgu_ref[:, gs]
        up = jnp.dot(xb, wgu_ref[:, us].astype(BF16), preferred_element_type=F32) + bgu_ref[:, us]
        pre.append((gate, up))
        start_rows(per_phase)
    acc = jnp.zeros((bm, D_MODEL), F32)
    for gs, (gate, up) in zip(cols, pre):
        gate = jnp.minimum(gate, SWIGLU_LIMIT)
        up = jnp.clip(up, -SWIGLU_LIMIT, SWIGLU_LIMIT)
        act = (up + 1.0) * gate * jax.nn.sigmoid(SWIGLU_ALPHA * gate)
        acc = acc + jnp.dot(act.astype(BF16), wd_ref[gs, :].astype(BF16), preferred_element_type=F32)
        start_rows(per_phase)
    assert not jobs
    ybuf[slot] = acc + bd_ref[...]

    @pl.when(i == last)
    def _():
        gather_wait(other)
        scatter_wait(other)


def _experts(h, block_e, row_src, w_gu, b_gu, w_down, b_down):
    tokens = h.shape[0]
    assert tokens & (tokens - 1) == 0, "row source encoding uses a power-of-two token count"
    bm = MOE_BM
    n_blocks = row_src.shape[0] // bm
    steps = n_blocks + 2
    wmap = lambda i, be, src: (be[jnp.minimum(i, n_blocks - 1)], 0, 0)
    return pl.pallas_call(
        functools.partial(_expert_kernel, tokens=tokens, n_blocks=n_blocks),
        grid_spec=pltpu.PrefetchScalarGridSpec(
            num_scalar_prefetch=2,
            grid=(steps,),
            in_specs=[
                pl.BlockSpec(memory_space=pl.ANY),
                pl.BlockSpec((None, D_MODEL, 2 * D_FF), wmap),
                pl.BlockSpec((None, 1, 2 * D_FF), wmap),
                pl.BlockSpec((None, D_FF, D_MODEL), wmap),
                pl.BlockSpec((None, 1, D_MODEL), wmap),
            ],
            out_specs=pl.BlockSpec(memory_space=pl.ANY),
            scratch_shapes=[pltpu.VMEM((2, bm, D_MODEL), F32), pltpu.VMEM((2, bm, D_MODEL), F32),
                            pltpu.SemaphoreType.DMA((2,)), pltpu.SemaphoreType.DMA((2,))],
        ),
        out_shape=jax.ShapeDtypeStruct((TOP_K * tokens + bm, D_MODEL), F32),
        compiler_params=_cparams(1),
        name="experts",
    )(block_e, row_src, h, w_gu, b_gu.reshape(N_EXPERTS, 1, -1), w_down, b_down.reshape(N_EXPERTS, 1, -1))


def _combine_kernel(y0_ref, y1_ref, y2_ref, y3_ref, w_ref, h_ref, g_ref, b_ref, o_ref):
    w = w_ref[...]
    f = jnp.zeros(h_ref.shape, F32)
    for k, y_ref in enumerate((y0_ref, y1_ref, y2_ref, y3_ref)):
        f = f + y_ref[...] * w[:, k:k + 1]
    o_ref[...] = _layer_norm(DN_ALPHA * h_ref[...] + f, g_ref[...], b_ref[...])


def _combine(yplanes, topw, h, ln2_g, ln2_b):
    tokens = h.shape[0]
    tm = MOE_TM
    per_plane = tokens // tm
    plane = lambda k: pl.BlockSpec((tm, D_MODEL), lambda i: (k * per_plane + i, 0))
    return pl.pallas_call(
        _combine_kernel,
        grid=(per_plane,),
        in_specs=[plane(0), plane(1), plane(2), plane(3),
                  pl.BlockSpec((tm, LANES), lambda i: (i, 0)),
                  pl.BlockSpec((tm, D_MODEL), lambda i: (i, 0)),
                  pl.BlockSpec((1, D_MODEL), lambda i: (0, 0)),
                  pl.BlockSpec((1, D_MODEL), lambda i: (0, 0))],
        out_specs=pl.BlockSpec((tm, D_MODEL), lambda i: (i, 0)),
        out_shape=jax.ShapeDtypeStruct((tokens, D_MODEL), F32),
        compiler_params=_cparams(1),
        name="combine",
    )(yplanes, yplanes, yplanes, yplanes, topw, h, ln2_g.reshape(1, -1), ln2_b.reshape(1, -1))


def _routing_tables(topi, rank, cnt, tokens):
    bm = MOE_BM
    counts = cnt[0, :N_EXPERTS].astype(jnp.int32)
    padded = (counts + bm - 1) // bm * bm
    pend = jnp.cumsum(padded)
    pstart = pend - padded
    e = topi[:, :TOP_K]
    sel = e[:, :, None] == jnp.arange(N_EXPERTS, dtype=jnp.int32)[None, None, :]
    dest = rank[:, :TOP_K] + jnp.sum(jnp.where(sel, pstart[None, None, :], 0), axis=-1)
    n_rows = tokens * TOP_K + N_EXPERTS * bm
    n_blocks = n_rows // bm
    starts = jnp.arange(n_blocks, dtype=jnp.int32) * bm
    block_e = jnp.minimum(jnp.sum((pend[None, :] <= starts[:, None]).astype(jnp.int32), axis=1), N_EXPERTS - 1)
    src = (jnp.arange(TOP_K, dtype=jnp.int32)[None, :] * tokens + jnp.arange(tokens, dtype=jnp.int32)[:, None])
    pad_src = TOP_K * tokens + jnp.arange(n_rows, dtype=jnp.int32) % bm
    row_src = pad_src.at[dest.reshape(-1)].set(src.reshape(-1), unique_indices=True, mode="promise_in_bounds")
    return block_e.astype(jnp.int32), row_src


def kernel(x, w_in, b_gate, lam_q1, lam_k1, lam_q2, lam_k2, subln_g, w_oa, w_ob, w_o, ln1_g, ln1_b,
           w_router, b_router, w_gu, b_gu, w_down, b_down, ln2_g, ln2_b):
    batch, seq, d = x.shape
    tokens = batch * seq
    h = x.reshape(tokens, d)
    tables = _rope_lane_tables(seq)
    for l in range(DEPTH):
        w_b = w_in[l].astype(BF16)
        hb = h.astype(BF16)
        qk_a, vt_a = _project_qkv(hb, w_b[:, :QKV_W], tables, seq, "qkv_diff",
                                  q_scale=HEAD_DIM ** -0.5 * math.log2(math.e), v_feature_major=True)
        ya = _diff_attention(qk_a.reshape(batch, seq, 2 * SEG_W), vt_a, lam_q1[l], lam_k1[l], lam_q2[l],
                             lam_k2[l], subln_g[l]).reshape(tokens, -1)
        dil = []
        for g, (_, dilation) in enumerate(DIL_PAIRS):
            w_g = w_b[:, (g + 1) * QKV_W:(g + 2) * QKV_W]
            if dilation == 1:
                hb_g, tab_g = hb, tables
            else:
                hb_g = hb.reshape(batch, seq // dilation, dilation, d).swapaxes(1, 2).reshape(tokens, d)
                tab_g = tuple(_residue_major(t, dilation) for t in tables)
            qkv_g = _project_qkv(hb_g, w_g, tab_g, seq, f"qkv_dil{g}", q_scale=HEAD_DIM ** -0.5)
            dil.append(_dilated_group(qkv_g, batch, seq, g))
        h1, topi, topw, rank, cnt = _merge(ya, dil, h, w_b[:, 4 * QKV_W:], b_gate[l], w_oa[l].astype(BF16),
                                           w_ob[l].astype(BF16), w_o[l].astype(BF16), ln1_g[l], ln1_b[l],
                                           w_router[l], b_router[l])
        block_e, row_src = _routing_tables(topi, rank, cnt, tokens)
        yplanes = _experts(h1, block_e, row_src, w_gu[l], b_gu[l], w_down[l], b_down[l])
        h = _combine(yplanes, topw, h1, ln2_g[l], ln2_b[l])
    return h.reshape(batch, seq, d)
```

```python
import functools
import math

import jax
import jax.numpy as jnp
from jax import lax
from jax.experimental import pallas as pl
from jax.experimental.pallas import tpu as pltpu
from jax.experimental.pallas import tpu_sc as plsc

F32 = jnp.float32
BF16 = jnp.bfloat16
U32 = jnp.uint32

D_MODEL = 1024
HEAD_DIM = 64
ROT_DIM = HEAD_DIM // 4
ROPE_THETA = 500000.0
QBLK = 128
DA_HEADS = 4
DIL_PAIRS = ((128, 1), (512, 4), (2048, 16))
SEG_W = 512
QKV_W = 3 * SEG_W
N_EXPERTS = 32
TOP_K = 4
D_FF = D_MODEL
SWIGLU_ALPHA = 1.702
SWIGLU_LIMIT = 7.0
DEPTH = 1
DN_ALPHA = (2 * DEPTH) ** 0.25
EPS = 1e-5
LAMBDA_INIT = 0.8 - 0.6 * math.exp(-0.3 * 0)

LANES = 128
QKV_TM = 512
QKV_CHUNK = 256
ATT_TQ = 256
MERGE_TM = 256
MOE_BM = 256
MOE_TM = 256
SC_CHUNK = 128
VMEM_LIMIT = 52 * 1024 * 1024


def _cparams(n_axes):
    return pltpu.CompilerParams(dimension_semantics=("arbitrary",) * n_axes,
                                vmem_limit_bytes=VMEM_LIMIT)


def _qkv_kernel(x_ref, w_ref, c_ref, s1_ref, s2_ref, o_ref, *maybe_vt_ref, q_scale):
    tm = x_ref.shape[0]
    for rc in range(tm // QKV_CHUNK):
        rows = slice(rc * QKV_CHUNK, (rc + 1) * QKV_CHUNK)
        xb = x_ref[rows, :]
        c, s1, s2 = c_ref[rows, :], s1_ref[rows, :], s2_ref[rows, :]
        for seg in range(3):
            acc = jnp.dot(xb, w_ref[:, seg * SEG_W:(seg + 1) * SEG_W], preferred_element_type=F32)
            if seg == 2:
                if maybe_vt_ref:
                    maybe_vt_ref[0][:, rows] = acc.T.astype(BF16)
                else:
                    o_ref[rows, seg * SEG_W:(seg + 1) * SEG_W] = acc.astype(BF16)
                continue
            for k in range(SEG_W // LANES):
                t = acc[:, k * LANES:(k + 1) * LANES]
                r = t * c + pltpu.roll(t, ROT_DIM // 2, 1) * s1 + pltpu.roll(t, LANES - ROT_DIM // 2, 1) * s2
                if seg == 0:
                    r = r * q_scale
                lo = seg * SEG_W + k * LANES
                o_ref[rows, lo:lo + LANES] = r.astype(BF16)


def _rope_lane_tables(seq):
    half = ROT_DIM // 2
    inv_freq = ROPE_THETA ** (-jnp.arange(0, ROT_DIM, 2, dtype=F32) / ROT_DIM)
    ang = jnp.arange(seq, dtype=F32)[:, None] * inv_freq[None, :]
    cos, sin = jnp.cos(ang), jnp.sin(ang)
    ones = jnp.ones((seq, HEAD_DIM - ROT_DIM), F32)
    zeros = jnp.zeros((seq, HEAD_DIM - ROT_DIM), F32)
    zh = jnp.zeros((seq, half), F32)
    c = jnp.concatenate([cos, cos, ones], axis=1)
    s1 = jnp.concatenate([zh, sin, zeros], axis=1)
    s2 = jnp.concatenate([-sin, zh, zeros], axis=1)
    rep = LANES // HEAD_DIM
    return tuple(jnp.tile(t, (1, rep)) for t in (c, s1, s2))


def _residue_major(t, dil):
    s = t.shape[0]
    return t.reshape(s // dil, dil, *t.shape[1:]).swapaxes(0, 1).reshape(t.shape)


def _project_qkv(xb, w_seg, tables, seq, name, q_scale, v_feature_major=False):
    tokens = xb.shape[0]
    tm = min(QKV_TM, seq)
    per_seq = seq // tm
    tab_spec = pl.BlockSpec((tm, LANES), lambda i: (i % per_seq, 0))
    out_w = 2 * SEG_W if v_feature_major else QKV_W
    out_specs = [pl.BlockSpec((tm, out_w), lambda i: (i, 0))]
    out_shape = [jax.ShapeDtypeStruct((tokens, out_w), BF16)]
    if v_feature_major:
        out_specs.append(pl.BlockSpec((None, SEG_W, tm), lambda i: (i // per_seq, 0, i % per_seq)))
        out_shape.append(jax.ShapeDtypeStruct((tokens // seq, SEG_W, seq), BF16))
    outs = pl.pallas_call(
        functools.partial(_qkv_kernel, q_scale=q_scale),
        grid=(tokens // tm,),
        in_specs=[pl.BlockSpec((tm, D_MODEL), lambda i: (i, 0)),
                  pl.BlockSpec((D_MODEL, QKV_W), lambda i: (0, 0)),
                  tab_spec, tab_spec, tab_spec],
        out_specs=out_specs,
        out_shape=out_shape,
        compiler_params=_cparams(1),
        name=name,
    )(xb, w_seg, *tables)
    return outs if v_feature_major else outs[0]


def _nt_dot(a, b):
    return lax.dot_general(a, b, (((1,), (1,)), ((), ())), preferred_element_type=F32)


def _diff_kernel(q_ref, k_ref, vt_ref, lq1_ref, lk1_ref, lq2_ref, lk2_ref, g_ref, o_ref, *, tq):
    i = pl.program_id(1)
    lane = lax.broadcasted_iota(jnp.int32, (tq, LANES), 1)
    qs = []
    for h in range(DA_HEADS):
        q = q_ref[:, h * LANES:(h + 1) * LANES]
        zero = jnp.zeros_like(q)
        qs.append((jnp.where(lane < HEAD_DIM, q, zero), jnp.where(lane >= HEAD_DIM, q, zero)))

    def step(j, carry, masked):
        start = pl.multiple_of(j * tq, tq)
        if masked:
            key = lax.broadcasted_iota(jnp.int32, (tq, tq), 0)
            qry = lax.broadcasted_iota(jnp.int32, (tq, tq), 1)
            keep = key <= qry
        scores = []
        for h in range(DA_HEADS):
            kb = k_ref[pl.ds(start, tq), h * LANES:(h + 1) * LANES]
            for comp in range(2):
                scores.append(_nt_dot(kb, qs[h][comp]))
        probs = []
        for idx, st in enumerate(scores):
            m, l, acc = carry[idx]
            if masked:
                st = jnp.where(keep, st, -jnp.inf)
            m_new = jnp.maximum(m, jnp.max(st, axis=0, keepdims=True))
            a = jnp.exp2(m - m_new)
            p = jnp.exp2(st - m_new)
            l = a * l + jnp.sum(p, axis=0, keepdims=True)
            probs.append((m_new, l, a, p.astype(BF16)))
        out = []
        for idx, (m_new, l, a, p) in enumerate(probs):
            h = idx // 2
            vt = vt_ref[h * LANES:(h + 1) * LANES, pl.ds(start, tq)]
            acc = a * carry[idx][2] + jnp.dot(vt, p, preferred_element_type=F32)
            out.append((m_new, l, acc))
        return tuple(out)

    init = tuple((jnp.full((1, tq), -jnp.inf, F32), jnp.zeros((1, tq), F32), jnp.zeros((LANES, tq), F32))
                 for _ in range(2 * DA_HEADS))
    carry = lax.fori_loop(0, i, lambda j, c: step(j, c, False), init)
    carry = step(i, carry, True)

    lam = (jnp.exp(jnp.sum(lq1_ref[...] * lk1_ref[...], axis=1, keepdims=True))
           - jnp.exp(jnp.sum(lq2_ref[...] * lk2_ref[...], axis=1, keepdims=True)) + LAMBDA_INIT)
    for h in range(DA_HEADS):
        (_, l1, a1), (_, l2, a2) = carry[2 * h], carry[2 * h + 1]
        o = (a1 / l1 - lam * (a2 / l2)).T
        ms = jnp.mean(o * o, axis=1, keepdims=True)
        o = o * lax.rsqrt(ms + EPS) * g_ref[...]
        o_ref[:, h * LANES:(h + 1) * LANES] = (o * (1.0 - LAMBDA_INIT)).astype(BF16)


def _diff_attention(qk3, vt3, lam_q1, lam_k1, lam_q2, lam_k2, subln_g):
    b, s, _ = qk3.shape
    tq = min(ATT_TQ, s)
    vec = lambda n: pl.BlockSpec((1, n), lambda bb, i: (0, 0))
    return pl.pallas_call(
        functools.partial(_diff_kernel, tq=tq),
        grid=(b, s // tq),
        in_specs=[
            pl.BlockSpec((None, tq, SEG_W), lambda bb, i: (bb, i, 0)),
            pl.BlockSpec((None, s, SEG_W), lambda bb, i: (bb, 0, 1)),
            pl.BlockSpec((None, SEG_W, s), lambda bb, i: (bb, 0, 0)),
            vec(HEAD_DIM), vec(HEAD_DIM), vec(HEAD_DIM), vec(HEAD_DIM), vec(2 * HEAD_DIM),
        ],
        out_specs=pl.BlockSpec((None, tq, SEG_W), lambda bb, i: (bb, i, 0)),
        out_shape=jax.ShapeDtypeStruct((b, s, SEG_W), BF16),
        compiler_params=_cparams(2),
        name="diff_attn",
    )(qk3, qk3, vt3, lam_q1.reshape(1, -1), lam_k1.reshape(1, -1), lam_q2.reshape(1, -1),
      lam_k2.reshape(1, -1), subln_g.reshape(1, -1))


def _dil_kernel(q_ref, kp_ref, kc_ref, vp_ref, vc_ref, o_ref, lse_ref, *, rel):
    n = pl.program_id(2)
    qi = lax.broadcasted_iota(jnp.int32, (QBLK, 2 * QBLK), 0)
    kj = lax.broadcasted_iota(jnp.int32, (QBLK, 2 * QBLK), 1)
    dist = qi + QBLK - kj
    valid = (dist >= 0) & (dist <= rel) & ((kj >= QBLK) | (n > 0))
    lane = lax.broadcasted_iota(jnp.int32, (QBLK, LANES), 1)
    low = lane < HEAD_DIM
    slabs = [slice(p * LANES, (p + 1) * LANES) for p in range(SEG_W // LANES)]
    scores = []
    for sl in slabs:
        q2 = q_ref[:, sl]
        k2 = jnp.concatenate([kp_ref[:, sl], kc_ref[:, sl]], axis=0)
        for sel in (low, jnp.logical_not(low)):
            scores.append(_nt_dot(jnp.where(sel, q2, jnp.zeros_like(q2)), k2))
    probs = []
    for s in scores:
        s = jnp.where(valid, s, -jnp.inf)
        m = jnp.max(s, axis=1, keepdims=True)
        pe = jnp.exp(s - m)
        den = jnp.sum(pe, axis=1, keepdims=True)
        probs.append((pe.astype(BF16), den, m + jnp.log(den)))
    for p, sl in enumerate(slabs):
        v2 = jnp.concatenate([vp_ref[:, sl], vc_ref[:, sl]], axis=0)
        outs = [jnp.dot(pe, v2, preferred_element_type=F32) / den for pe, den, _ in probs[2 * p:2 * p + 2]]
        o_ref[:, sl] = jnp.where(low, outs[0], outs[1]).astype(BF16)
        lse_ref[:, sl] = jnp.where(low, probs[2 * p][2], probs[2 * p + 1][2])


def _dilated_group(qkv, batch, seq, group):
    window, dil = DIL_PAIRS[group]
    rel = window // dil
    length = seq // dil
    nb = length // QBLK
    view = qkv.reshape(batch, dil, length, QKV_W)

    def spec(seg, prev):
        if prev:
            return pl.BlockSpec((None, None, QBLK, SEG_W), lambda b, r, n: (b, r, jnp.maximum(n - 1, 0), seg))
        return pl.BlockSpec((None, None, QBLK, SEG_W), lambda b, r, n: (b, r, n, seg))

    out_spec = pl.BlockSpec((None, None, QBLK, SEG_W), lambda b, r, n: (b, r, n, 0))
    o, lse = pl.pallas_call(
        functools.partial(_dil_kernel, rel=rel),
        grid=(batch, dil, nb),
        in_specs=[spec(0, False), spec(1, True), spec(1, False), spec(2, True), spec(2, False)],
        out_specs=[out_spec, out_spec],
        out_shape=[jax.ShapeDtypeStruct((batch, dil, length, SEG_W), BF16),
                   jax.ShapeDtypeStruct((batch, dil, length, SEG_W), F32)],
        compiler_params=_cparams(3),
        name=f"dilated{group}",
    )(view, view, view, view, view)
    tokens = batch * seq
    back = lambda t: t.swapaxes(1, 2).reshape(tokens, SEG_W)
    return back(o), back(lse)


def _layer_norm(z, g, b):
    mu = jnp.mean(z, axis=1, keepdims=True)
    zc = z - mu
    var = jnp.mean(zc * zc, axis=1, keepdims=True)
    return zc * lax.rsqrt(var + EPS) * g + b


def _pack_bf16_pairs(v):
    n = v.shape[1] // 2
    lo = pltpu.bitcast(v[:, :n].astype(BF16).astype(F32), U32)
    hi = pltpu.bitcast(v[:, n:].astype(BF16).astype(F32), U32)
    return jnp.bitwise_or(jnp.right_shift(lo, jnp.uint32(16)), jnp.bitwise_and(hi, jnp.uint32(0xFFFF0000)))


def _unpack_bf16_pairs(w):
    lo = pltpu.bitcast(jnp.left_shift(w, jnp.uint32(16)), F32)
    hi = pltpu.bitcast(jnp.bitwise_and(w, jnp.uint32(0xFFFF0000)), F32)
    return lo, hi


def _merge_kernel(ya_ref, o1_ref, o2_ref, o3_ref, l1_ref, l2_ref, l3_ref, x_ref,
                  wg_ref, bg_ref, woa_ref, wob_ref, wo_ref, g_ref, b_ref, wrh_ref, wrl_ref, br_ref,
                  h_ref, hp_ref, topi_ref, topw_ref, rank_ref, cnt_ref, carry_ref):
    i = pl.program_id(0)
    tm = x_ref.shape[0]

    @pl.when(i == 0)
    def _():
        carry_ref[...] = jnp.zeros_like(carry_ref)

    x = x_ref[...]
    gates = jax.nn.sigmoid(jnp.dot(x.astype(BF16), wg_ref[...], preferred_element_type=F32) + bg_ref[...])
    la, lb, lc = l1_ref[...], l2_ref[...], l3_ref[...]
    mx = jnp.maximum(jnp.maximum(la, lb), lc)
    ea, eb, ec = jnp.exp(la - mx), jnp.exp(lb - mx), jnp.exp(lc - mx)
    yb = (ea * o1_ref[...].astype(F32) + eb * o2_ref[...].astype(F32) + ec * o3_ref[...].astype(F32)) / (ea + eb + ec)
    pa = jnp.dot(ya_ref[...], woa_ref[...], preferred_element_type=F32)
    pb = jnp.dot(yb.astype(BF16), wob_ref[...], preferred_element_type=F32)
    merged = gates[:, :D_MODEL] * pa + gates[:, D_MODEL:] * pb
    mix = jnp.dot(merged.astype(BF16), wo_ref[...], preferred_element_type=F32)
    h = _layer_norm(DN_ALPHA * x + mix, g_ref[...], b_ref[...])
    h_ref[...] = h
    hp_ref[...] = _pack_bf16_pairs(h)

    hh = h.astype(BF16)
    hl = (h - hh.astype(F32)).astype(BF16)
    logits = (jnp.dot(hh, wrh_ref[...], preferred_element_type=F32)
              + jnp.dot(hl, wrh_ref[...], preferred_element_type=F32)
              + jnp.dot(hh, wrl_ref[...], preferred_element_type=F32) + br_ref[...])
    lane = lax.broadcasted_iota(jnp.int32, (tm, LANES), 1)
    lg = jnp.where(lane < N_EXPERTS, logits, -jnp.inf)
    vals, idxs = [], []
    for _ in range(TOP_K):
        mv = jnp.max(lg, axis=1, keepdims=True)
        ik = jnp.min(jnp.where(lg == mv, lane, LANES), axis=1, keepdims=True)
        vals.append(mv)
        idxs.append(ik)
        lg = jnp.where(lane == ik, -jnp.inf, lg)
    es = [jnp.exp(v - vals[0]) for v in vals]
    tot = es[0] + es[1] + es[2] + es[3]

    onehot = jnp.zeros((tm, LANES), F32)
    for ik in idxs:
        onehot = onehot + (lane == ik).astype(F32)
    r_i = lax.broadcasted_iota(jnp.int32, (tm, tm), 0)
    c_i = lax.broadcasted_iota(jnp.int32, (tm, tm), 1)
    tri = (r_i > c_i).astype(BF16)
    pre = jnp.dot(tri, onehot.astype(BF16), preferred_element_type=F32) + carry_ref[...]

    topi = jnp.zeros((tm, LANES), jnp.int32)
    topw = jnp.zeros((tm, LANES), F32)
    rank = jnp.zeros((tm, LANES), F32)
    for k in range(TOP_K):
        rk = jnp.sum(jnp.where(lane == idxs[k], pre, 0.0), axis=1, keepdims=True)
        topi = jnp.where(lane == k, idxs[k], topi)
        topw = jnp.where(lane == k, es[k] / tot, topw)
        rank = jnp.where(lane == k, rk, rank)
    topi_ref[...] = topi
    topw_ref[...] = topw
    rank_ref[...] = rank.astype(jnp.int32)
    carry_ref[...] = carry_ref[...] + jnp.sum(onehot, axis=0, keepdims=True)
    cnt_ref[...] = carry_ref[...]


def _merge(ya, dil_outs, x2, w_gate_b, b_gate, w_oa_b, w_ob_b, w_o_b, ln1_g, ln1_b, w_router, b_router):
    tokens = x2.shape[0]
    tm = MERGE_TM
    wr = jnp.zeros((D_MODEL, LANES), F32).at[:, :N_EXPERTS].set(w_router)
    wr_hi = wr.astype(BF16)
    wr_lo = (wr - wr_hi.astype(F32)).astype(BF16)
    br = jnp.zeros((1, LANES), F32).at[0, :N_EXPERTS].set(b_router)
    row = lambda w: pl.BlockSpec((tm, w), lambda i: (i, 0))
    full = lambda a: pl.BlockSpec(a.shape, lambda i: (0,) * a.ndim)
    (o1, l1), (o2, l2), (o3, l3) = dil_outs
    bg = b_gate.reshape(1, -1)
    g1 = ln1_g.reshape(1, -1)
    b1 = ln1_b.reshape(1, -1)
    lane_out = lambda dt: jax.ShapeDtypeStruct((tokens, LANES), dt)
    return pl.pallas_call(
        _merge_kernel,
        grid=(tokens // tm,),
        in_specs=[row(SEG_W), row(SEG_W), row(SEG_W), row(SEG_W), row(SEG_W), row(SEG_W), row(SEG_W),
                  row(D_MODEL), full(w_gate_b), full(bg), full(w_oa_b), full(w_ob_b), full(w_o_b), full(g1), full(b1),
                  full(wr_hi), full(wr_lo), full(br)],
        out_specs=[row(D_MODEL), row(D_MODEL // 2), row(LANES), row(LANES), row(LANES),
                   pl.BlockSpec((1, LANES), lambda i: (0, 0))],
        out_shape=[jax.ShapeDtypeStruct((tokens, D_MODEL), F32), jax.ShapeDtypeStruct((tokens, D_MODEL // 2), U32),
                   lane_out(jnp.int32), lane_out(F32), lane_out(jnp.int32), jax.ShapeDtypeStruct((1, LANES), F32)],
        scratch_shapes=[pltpu.VMEM((1, LANES), F32)],
        compiler_params=_cparams(1),
        name="merge",
    )(ya, o1, o2, o3, l1, l2, l3, x2, w_gate_b, bg, w_oa_b, w_ob_b, w_o_b, g1, b1, wr_hi, wr_lo, br)


def _sc_mesh():
    return plsc.VectorSubcoreMesh(core_axis_name="core", subcore_axis_name="subcore")


def _sc_scatter_rows(table, src_idx, dst_idx):
    n = src_idx.shape[0]
    d = table.shape[1]
    mesh = _sc_mesh()
    workers = mesh.num_cores * mesh.num_subcores
    per = n // (SC_CHUNK * workers)
    assert per * SC_CHUNK * workers == n

    @pl.kernel(out_type=jax.ShapeDtypeStruct((n, d), table.dtype), mesh=mesh,
               scratch_types=[pltpu.VMEM((1, SC_CHUNK), jnp.int32), pltpu.VMEM((1, SC_CHUNK), jnp.int32),
                              pltpu.VMEM((SC_CHUNK, d), table.dtype)])
    def copy(t_hbm, s_hbm, d_hbm, o_hbm, s_vm, d_vm, buf):
        wid = lax.axis_index("core") * mesh.num_subcores + lax.axis_index("subcore")

        @pl.loop(0, per)
        def _(j):
            blk = wid * per + j
            pltpu.sync_copy(s_hbm.at[pl.ds(blk, 1)], s_vm)
            pltpu.sync_copy(d_hbm.at[pl.ds(blk, 1)], d_vm)
            pltpu.sync_copy(t_hbm.at[s_vm.at[0]], buf)
            pltpu.sync_copy(buf, o_hbm.at[d_vm.at[0]])

    return copy(table, src_idx.reshape(-1, SC_CHUNK), dst_idx.reshape(-1, SC_CHUNK))


def _sc_gather_rows(table, idx):
    n = idx.shape[0]
    d = table.shape[1]
    mesh = _sc_mesh()
    workers = mesh.num_cores * mesh.num_subcores
    per = n // (SC_CHUNK * workers)
    assert per * SC_CHUNK * workers == n

    @pl.kernel(out_type=jax.ShapeDtypeStruct((n, d), table.dtype), mesh=mesh,
               scratch_types=[pltpu.VMEM((1, SC_CHUNK), jnp.int32), pltpu.VMEM((SC_CHUNK, d), table.dtype)])
    def gather(t_hbm, i_hbm, o_hbm, i_vm, buf):
        wid = lax.axis_index("core") * mesh.num_subcores + lax.axis_index("subcore")

        @pl.loop(0, per)
        def _(j):
            blk = wid * per + j
            pltpu.sync_copy(i_hbm.at[pl.ds(blk, 1)], i_vm)
            pltpu.sync_copy(t_hbm.at[i_vm.at[0]], buf)
            pltpu.sync_copy(buf, o_hbm.at[pl.ds(blk * SC_CHUNK, SC_CHUNK)])

    return gather(table, idx.reshape(-1, SC_CHUNK))


def _expert_kernel(be_ref, nu_ref, x_ref, wgu_ref, bgu_ref, wd_ref, bd_ref, y_ref):
    i = pl.program_id(0)

    @pl.when(i < nu_ref[0])
    def _():
        lo, hi = _unpack_bf16_pairs(x_ref[...])
        xb = jnp.concatenate([lo, hi], axis=1).astype(BF16)
        chunk = 512
        cols = [slice(c * chunk, (c + 1) * chunk) for c in range(D_FF // chunk)]
        pre = []
        for gs in cols:
            us = slice(D_FF + gs.start, D_FF + gs.stop)
            gate = jnp.dot(xb, wgu_ref[:, gs].astype(BF16), preferred_element_type=F32) + bgu_ref[:, gs]
            up = jnp.dot(xb, wgu_ref[:, us].astype(BF16), preferred_element_type=F32) + bgu_ref[:, us]
            pre.append((gate, up))
        acc = jnp.zeros((x_ref.shape[0], D_MODEL), F32)
        for gs, (gate, up) in zip(cols, pre):
            gate = jnp.minimum(gate, SWIGLU_LIMIT)
            up = jnp.clip(up, -SWIGLU_LIMIT, SWIGLU_LIMIT)
            act = (up + 1.0) * gate * jax.nn.sigmoid(SWIGLU_ALPHA * gate)
            acc = acc + jnp.dot(act.astype(BF16), wd_ref[gs, :].astype(BF16), preferred_element_type=F32)
        y_ref[...] = _pack_bf16_pairs(acc + bd_ref[...])

    @pl.when(i >= nu_ref[0])
    def _():
        y_ref[...] = jnp.zeros_like(y_ref)


def _experts(xrows, block_e, n_used, w_gu, b_gu, w_down, b_down):
    n_rows = xrows.shape[0]
    bm = MOE_BM
    return pl.pallas_call(
        _expert_kernel,
        grid_spec=pltpu.PrefetchScalarGridSpec(
            num_scalar_prefetch=2,
            grid=(n_rows // bm,),
            in_specs=[
                pl.BlockSpec((bm, D_MODEL // 2), lambda i, be, nu: (i, 0)),
                pl.BlockSpec((None, D_MODEL, 2 * D_FF), lambda i, be, nu: (be[i], 0, 0)),
                pl.BlockSpec((None, 1, 2 * D_FF), lambda i, be, nu: (be[i], 0, 0)),
                pl.BlockSpec((None, D_FF, D_MODEL), lambda i, be, nu: (be[i], 0, 0)),
                pl.BlockSpec((None, 1, D_MODEL), lambda i, be, nu: (be[i], 0, 0)),
            ],
            out_specs=pl.BlockSpec((bm, D_MODEL // 2), lambda i, be, nu: (i, 0)),
        ),
        out_shape=jax.ShapeDtypeStruct((n_rows, D_MODEL // 2), U32),
        compiler_params=_cparams(1),
        name="experts",
    )(block_e, n_used, xrows, w_gu, b_gu.reshape(N_EXPERTS, 1, -1), w_down, b_down.reshape(N_EXPERTS, 1, -1))


def _combine_kernel(y0_ref, y1_ref, y2_ref, y3_ref, w_ref, h_ref, g_ref, b_ref, o_ref):
    w = w_ref[...]
    half = D_MODEL // 2
    f_lo = jnp.zeros((h_ref.shape[0], half), F32)
    f_hi = jnp.zeros((h_ref.shape[0], half), F32)
    for k, y_ref in enumerate((y0_ref, y1_ref, y2_ref, y3_ref)):
        lo, hi = _unpack_bf16_pairs(y_ref[...])
        f_lo = f_lo + lo * w[:, k:k + 1]
        f_hi = f_hi + hi * w[:, k:k + 1]
    f = jnp.concatenate([f_lo, f_hi], axis=1)
    o_ref[...] = _layer_norm(DN_ALPHA * h_ref[...] + f, g_ref[...], b_ref[...])


def _combine(yplanes, topw, h, ln2_g, ln2_b):
    tokens = h.shape[0]
    tm = MOE_TM
    per_plane = tokens // tm
    plane = lambda k: pl.BlockSpec((tm, D_MODEL // 2), lambda i: (k * per_plane + i, 0))
    return pl.pallas_call(
        _combine_kernel,
        grid=(per_plane,),
        in_specs=[plane(0), plane(1), plane(2), plane(3),
                  pl.BlockSpec((tm, LANES), lambda i: (i, 0)),
                  pl.BlockSpec((tm, D_MODEL), lambda i: (i, 0)),
                  pl.BlockSpec((1, D_MODEL), lambda i: (0, 0)),
                  pl.BlockSpec((1, D_MODEL), lambda i: (0, 0))],
        out_specs=pl.BlockSpec((tm, D_MODEL), lambda i: (i, 0)),
        out_shape=jax.ShapeDtypeStruct((tokens, D_MODEL), F32),
        compiler_params=_cparams(1),
        name="combine",
    )(yplanes, yplanes, yplanes, yplanes, topw, h, ln2_g.reshape(1, -1), ln2_b.reshape(1, -1))


def _routing_tables(topi, rank, cnt, tokens):
    bm = MOE_BM
    i32 = jnp.int32
    experts = jnp.arange(N_EXPERTS, dtype=i32)
    counts = cnt[0, :N_EXPERTS].astype(i32)
    padded = (counts + bm - 1) // bm * bm
    pend = jnp.cumsum(padded)
    pstart = pend - padded
    sel = topi[:, :TOP_K, None] == experts[None, None, :]
    dest = rank[:, :TOP_K] + jnp.sum(jnp.where(sel, pstart[None, None, :], 0), axis=-1)
    n_pad = N_EXPERTS * bm
    n_rows = tokens * TOP_K + n_pad
    starts = jnp.arange(n_rows // bm, dtype=i32) * bm
    block_e = jnp.minimum(jnp.sum((pend[None, :] <= starts[:, None]).astype(i32), axis=1), N_EXPERTS - 1)
    n_used = (pend[-1] // bm).reshape(1)
    pad_cnt = padded - counts
    pad_end = jnp.cumsum(pad_cnt)
    j = jnp.arange(n_pad, dtype=i32)
    owner = jnp.sum((pad_end[None, :] <= j[:, None]).astype(i32), axis=1)
    is_owner = owner[:, None] == experts[None, :]
    pick = lambda v: jnp.sum(jnp.where(is_owner, v[None, :], 0), axis=1)
    in_expert = pick(pstart + counts) + (j - pick(pad_end - pad_cnt))
    pad_rows = jnp.where(owner < N_EXPERTS, in_expert, pend[-1] + (j - pad_end[-1]))
    src_tok = jnp.concatenate([jnp.repeat(jnp.arange(tokens, dtype=i32), TOP_K), j % tokens])
    dst_row = jnp.concatenate([dest.reshape(-1), pad_rows]).astype(i32)
    return dest.astype(i32), block_e.astype(i32), n_used.astype(i32), src_tok, dst_row


def kernel(x, w_in, b_gate, lam_q1, lam_k1, lam_q2, lam_k2, subln_g, w_oa, w_ob, w_o, ln1_g, ln1_b,
           w_router, b_router, w_gu, b_gu, w_down, b_down, ln2_g, ln2_b):
    batch, seq, d = x.shape
    tokens = batch * seq
    h = x.reshape(tokens, d)
    tables = _rope_lane_tables(seq)
    for l in range(DEPTH):
        w_b = w_in[l].astype(BF16)
        hb = h.astype(BF16)
        qk_a, vt_a = _project_qkv(hb, w_b[:, :QKV_W], tables, seq, "qkv_diff",
                                  q_scale=HEAD_DIM ** -0.5 * math.log2(math.e), v_feature_major=True)
        ya = _diff_attention(qk_a.reshape(batch, seq, 2 * SEG_W), vt_a, lam_q1[l], lam_k1[l], lam_q2[l],
                             lam_k2[l], subln_g[l]).reshape(tokens, -1)
        dil = []
        for g, (_, dilation) in enumerate(DIL_PAIRS):
            w_g = w_b[:, (g + 1) * QKV_W:(g + 2) * QKV_W]
            if dilation == 1:
                hb_g, tab_g = hb, tables
            else:
                hb_g = hb.reshape(batch, seq // dilation, dilation, d).swapaxes(1, 2).reshape(tokens, d)
                tab_g = tuple(_residue_major(t, dilation) for t in tables)
            qkv_g = _project_qkv(hb_g, w_g, tab_g, seq, f"qkv_dil{g}", q_scale=HEAD_DIM ** -0.5)
            dil.append(_dilated_group(qkv_g, batch, seq, g))
        h1, h1p, topi, topw, rank, cnt = _merge(ya, dil, h, w_b[:, 4 * QKV_W:], b_gate[l], w_oa[l].astype(BF16),
                                                w_ob[l].astype(BF16), w_o[l].astype(BF16), ln1_g[l], ln1_b[l],
                                                w_router[l], b_router[l])
        dest, block_e, n_used, src_tok, dst_row = _routing_tables(topi, rank, cnt, tokens)
        xrows = _sc_scatter_rows(h1p, src_tok, dst_row)
        yrows = _experts(xrows, block_e, n_used, w_gu[l], b_gu[l], w_down[l], b_down[l])
        yplanes = _sc_gather_rows(yrows, dest.T.reshape(-1))
        h = _combine(yplanes, topw, h1, ln2_g[l], ln2_b[l])
    return h.reshape(batch, seq, d)
```

```python
import functools
import math

import jax
import jax.numpy as jnp
from jax import lax
from jax.experimental import pallas as pl
from jax.experimental.pallas import tpu as pltpu
from jax.experimental.pallas import tpu_sc as plsc

F32 = jnp.float32
BF16 = jnp.bfloat16
U32 = jnp.uint32

D_MODEL = 1024
HEAD_DIM = 64
ROT_DIM = HEAD_DIM // 4
ROPE_THETA = 500000.0
QBLK = 128
DA_HEADS = 4
DIL_PAIRS = ((128, 1), (512, 4), (2048, 16))
SEG_W = 512
QKV_W = 3 * SEG_W
N_EXPERTS = 32
TOP_K = 4
D_FF = D_MODEL
SWIGLU_ALPHA = 1.702
SWIGLU_LIMIT = 7.0
DEPTH = 1
DN_ALPHA = (2 * DEPTH) ** 0.25
EPS = 1e-5
LAMBDA_INIT = 0.8 - 0.6 * math.exp(-0.3 * 0)

LANES = 128
QKV_TM = 512
QKV_CHUNK = 256
ATT_TQ = 256
MERGE_TM = 256
MOE_BM = 256
MOE_TM = 256
SC_CHUNK = 128
VMEM_LIMIT = 52 * 1024 * 1024


def _cparams(n_axes):
    return pltpu.CompilerParams(dimension_semantics=("arbitrary",) * n_axes,
                                vmem_limit_bytes=VMEM_LIMIT)


def _qkv_kernel(x_ref, w_ref, c_ref, s1_ref, s2_ref, o_ref, *maybe_vt_ref, q_scale):
    tm = x_ref.shape[0]
    for rc in range(tm // QKV_CHUNK):
        rows = slice(rc * QKV_CHUNK, (rc + 1) * QKV_CHUNK)
        xb = x_ref[rows, :]
        c, s1, s2 = c_ref[rows, :], s1_ref[rows, :], s2_ref[rows, :]
        for seg in range(3):
            acc = jnp.dot(xb, w_ref[:, seg * SEG_W:(seg + 1) * SEG_W], preferred_element_type=F32)
            if seg == 2:
                if maybe_vt_ref:
                    maybe_vt_ref[0][:, rows] = acc.T.astype(BF16)
                else:
                    o_ref[rows, seg * SEG_W:(seg + 1) * SEG_W] = acc.astype(BF16)
                continue
            for k in range(SEG_W // LANES):
                t = acc[:, k * LANES:(k + 1) * LANES]
                r = t * c + pltpu.roll(t, ROT_DIM // 2, 1) * s1 + pltpu.roll(t, LANES - ROT_DIM // 2, 1) * s2
                if seg == 0:
                    r = r * q_scale
                lo = seg * SEG_W + k * LANES
                o_ref[rows, lo:lo + LANES] = r.astype(BF16)


def _rope_lane_tables(seq):
    half = ROT_DIM // 2
    inv_freq = ROPE_THETA ** (-jnp.arange(0, ROT_DIM, 2, dtype=F32) / ROT_DIM)
    ang = jnp.arange(seq, dtype=F32)[:, None] * inv_freq[None, :]
    cos, sin = jnp.cos(ang), jnp.sin(ang)
    ones = jnp.ones((seq, HEAD_DIM - ROT_DIM), F32)
    zeros = jnp.zeros((seq, HEAD_DIM - ROT_DIM), F32)
    zh = jnp.zeros((seq, half), F32)
    c = jnp.concatenate([cos, cos, ones], axis=1)
    s1 = jnp.concatenate([zh, sin, zeros], axis=1)
    s2 = jnp.concatenate([-sin, zh, zeros], axis=1)
    rep = LANES // HEAD_DIM
    return tuple(jnp.tile(t, (1, rep)) for t in (c, s1, s2))


def _residue_major(t, dil):
    s = t.shape[0]
    return t.reshape(s // dil, dil, *t.shape[1:]).swapaxes(0, 1).reshape(t.shape)


def _project_qkv(xb, w_seg, tables, seq, name, q_scale, v_feature_major=False):
    tokens = xb.shape[0]
    tm = min(QKV_TM, seq)
    per_seq = seq // tm
    tab_spec = pl.BlockSpec((tm, LANES), lambda i: (i % per_seq, 0))
    out_w = 2 * SEG_W if v_feature_major else QKV_W
    out_specs = [pl.BlockSpec((tm, out_w), lambda i: (i, 0))]
    out_shape = [jax.ShapeDtypeStruct((tokens, out_w), BF16)]
    if v_feature_major:
        out_specs.append(pl.BlockSpec((None, SEG_W, tm), lambda i: (i // per_seq, 0, i % per_seq)))
        out_shape.append(jax.ShapeDtypeStruct((tokens // seq, SEG_W, seq), BF16))
    outs = pl.pallas_call(
        functools.partial(_qkv_kernel, q_scale=q_scale),
        grid=(tokens // tm,),
        in_specs=[pl.BlockSpec((tm, D_MODEL), lambda i: (i, 0)),
                  pl.BlockSpec((D_MODEL, QKV_W), lambda i: (0, 0)),
                  tab_spec, tab_spec, tab_spec],
        out_specs=out_specs,
        out_shape=out_shape,
        compiler_params=_cparams(1),
        name=name,
    )(xb, w_seg, *tables)
    return outs if v_feature_major else outs[0]


def _nt_dot(a, b):
    return lax.dot_general(a, b, (((1,), (1,)), ((), ())), preferred_element_type=F32)


def _diff_kernel(q_ref, k_ref, vt_ref, lq1_ref, lk1_ref, lq2_ref, lk2_ref, g_ref, o_ref, s_ref, acc_ref, *, tq):
    i = pl.program_id(1)
    chains = 2 * DA_HEADS
    half = tq // 2
    lane = lax.broadcasted_iota(jnp.int32, (tq, LANES), 1)
    qs = []
    for h in range(DA_HEADS):
        q = q_ref[:, h * LANES:(h + 1) * LANES]
        zero = jnp.zeros_like(q)
        qs += [jnp.where(lane < HEAD_DIM, q, zero), jnp.where(lane >= HEAD_DIM, q, zero)]
    acc_ref[...] = jnp.zeros_like(acc_ref)

    def score_block(j, buf):
        start = pl.multiple_of(j * tq, tq)
        for c in range(chains):
            kb = k_ref[pl.ds(start, tq), (c // 2) * LANES:(c // 2 + 1) * LANES]
            buf[c] = _nt_dot(kb, qs[c])

    def absorb(j, buf, state, masked):
        start = pl.multiple_of(j * tq, tq)
        out = []
        for c in range(chains):
            m, l = state[c]
            if masked:
                key = lax.broadcasted_iota(jnp.int32, (tq, tq), 0)
                qry = lax.broadcasted_iota(jnp.int32, (tq, tq), 1)
                buf[c] = jnp.where(key <= qry, buf[c], -jnp.inf)
            m_new = jnp.maximum(m, jnp.max(buf[c], axis=0, keepdims=True))
            a = jnp.exp2(m - m_new)
            p = [jnp.exp2(buf[c, u * half:(u + 1) * half, :] - m_new) for u in range(2)]
            l = a * l + jnp.sum(p[0], axis=0, keepdims=True) + jnp.sum(p[1], axis=0, keepdims=True)
            vt = vt_ref[(c // 2) * LANES:(c // 2 + 1) * LANES, pl.ds(start, tq)]
            pv = jnp.dot(vt, jnp.concatenate(p, axis=0).astype(BF16), preferred_element_type=F32)
            acc_ref[c] = a * acc_ref[c] + pv
            out.append((m_new, l))
        return tuple(out)

    buf_a, buf_b = s_ref.at[0], s_ref.at[1]

    def two_blocks(jj, state):
        j = 2 * jj
        score_block(j + 1, buf_b)
        state = absorb(j, buf_a, state, False)
        score_block(j + 2, buf_a)
        return absorb(j + 1, buf_b, state, False)

    def last_from_a(state):
        return absorb(i, buf_a, state, True)

    def last_from_b(state):
        score_block(i, buf_b)
        state = absorb(i - 1, buf_a, state, False)
        return absorb(i, buf_b, state, True)

    init = tuple((jnp.full((1, tq), -jnp.inf, F32), jnp.zeros((1, tq), F32)) for _ in range(chains))
    score_block(0, buf_a)
    state = lax.fori_loop(0, i // 2, two_blocks, init)
    state = lax.cond(lax.rem(i, 2) == 0, last_from_a, last_from_b, state)

    lam = (jnp.exp(jnp.sum(lq1_ref[...] * lk1_ref[...], axis=1, keepdims=True))
           - jnp.exp(jnp.sum(lq2_ref[...] * lk2_ref[...], axis=1, keepdims=True)) + LAMBDA_INIT)
    for h in range(DA_HEADS):
        l1, l2 = state[2 * h][1], state[2 * h + 1][1]
        o = (acc_ref[2 * h] / l1 - lam * (acc_ref[2 * h + 1] / l2)).T
        ms = jnp.mean(o * o, axis=1, keepdims=True)
        o = o * lax.rsqrt(ms + EPS) * g_ref[...]
        o_ref[:, h * LANES:(h + 1) * LANES] = (o * (1.0 - LAMBDA_INIT)).astype(BF16)


def _diff_attention(qk3, vt3, lam_q1, lam_k1, lam_q2, lam_k2, subln_g):
    b, s, _ = qk3.shape
    tq = min(ATT_TQ, s)
    vec = lambda n: pl.BlockSpec((1, n), lambda bb, i: (0, 0))
    return pl.pallas_call(
        functools.partial(_diff_kernel, tq=tq),
        grid=(b, s // tq),
        in_specs=[
            pl.BlockSpec((None, tq, SEG_W), lambda bb, i: (bb, i, 0)),
            pl.BlockSpec((None, s, SEG_W), lambda bb, i: (bb, 0, 1)),
            pl.BlockSpec((None, SEG_W, s), lambda bb, i: (bb, 0, 0)),
            vec(HEAD_DIM), vec(HEAD_DIM), vec(HEAD_DIM), vec(HEAD_DIM), vec(2 * HEAD_DIM),
        ],
        out_specs=pl.BlockSpec((None, tq, SEG_W), lambda bb, i: (bb, i, 0)),
        out_shape=jax.ShapeDtypeStruct((b, s, SEG_W), BF16),
        scratch_shapes=[pltpu.VMEM((2, 2 * DA_HEADS, tq, tq), F32), pltpu.VMEM((2 * DA_HEADS, LANES, tq), F32)],
        compiler_params=_cparams(2),
        name="diff_attn",
    )(qk3, qk3, vt3, lam_q1.reshape(1, -1), lam_k1.reshape(1, -1), lam_q2.reshape(1, -1),
      lam_k2.reshape(1, -1), subln_g.reshape(1, -1))


def _dil_kernel(q_ref, kp_ref, kc_ref, vp_ref, vc_ref, o_ref, lse_ref, *, rel):
    n = pl.program_id(2)
    qi = lax.broadcasted_iota(jnp.int32, (QBLK, 2 * QBLK), 0)
    kj = lax.broadcasted_iota(jnp.int32, (QBLK, 2 * QBLK), 1)
    dist = qi + QBLK - kj
    valid = (dist >= 0) & (dist <= rel) & ((kj >= QBLK) | (n > 0))
    lane = lax.broadcasted_iota(jnp.int32, (QBLK, LANES), 1)
    low = lane < HEAD_DIM
    slabs = [slice(p * LANES, (p + 1) * LANES) for p in range(SEG_W // LANES)]
    scores = []
    for sl in slabs:
        q2 = q_ref[:, sl]
        k2 = jnp.concatenate([kp_ref[:, sl], kc_ref[:, sl]], axis=0)
        for sel in (low, jnp.logical_not(low)):
            scores.append(_nt_dot(jnp.where(sel, q2, jnp.zeros_like(q2)), k2))
    probs = []
    for s in scores:
        s = jnp.where(valid, s, -jnp.inf)
        m = jnp.max(s, axis=1, keepdims=True)
        pe = jnp.exp(s - m)
        den = jnp.sum(pe, axis=1, keepdims=True)
        probs.append((pe.astype(BF16), den, m + jnp.log(den)))
    for p, sl in enumerate(slabs):
        v2 = jnp.concatenate([vp_ref[:, sl], vc_ref[:, sl]], axis=0)
        outs = [jnp.dot(pe, v2, preferred_element_type=F32) / den for pe, den, _ in probs[2 * p:2 * p + 2]]
        o_ref[:, sl] = jnp.where(low, outs[0], outs[1]).astype(BF16)
        lse_ref[:, sl] = jnp.where(low, probs[2 * p][2], probs[2 * p + 1][2])


def _dilated_group(qkv, batch, seq, group):
    window, dil = DIL_PAIRS[group]
    rel = window // dil
    length = seq // dil
    nb = length // QBLK
    view = qkv.reshape(batch, dil, length, QKV_W)

    def spec(seg, prev):
        if prev:
            return pl.BlockSpec((None, None, QBLK, SEG_W), lambda b, r, n: (b, r, jnp.maximum(n - 1, 0), seg))
        return pl.BlockSpec((None, None, QBLK, SEG_W), lambda b, r, n: (b, r, n, seg))

    out_spec = pl.BlockSpec((None, None, QBLK, SEG_W), lambda b, r, n: (b, r, n, 0))
    o, lse = pl.pallas_call(
        functools.partial(_dil_kernel, rel=rel),
        grid=(batch, dil, nb),
        in_specs=[spec(0, False), spec(1, True), spec(1, False), spec(2, True), spec(2, False)],
        out_specs=[out_spec, out_spec],
        out_shape=[jax.ShapeDtypeStruct((batch, dil, length, SEG_W), BF16),
                   jax.ShapeDtypeStruct((batch, dil, length, SEG_W), F32)],
        compiler_params=_cparams(3),
        name=f"dilated{group}",
    )(view, view, view, view, view)
    tokens = batch * seq
    back = lambda t: t.swapaxes(1, 2).reshape(tokens, SEG_W)
    return back(o), back(lse)


def _layer_norm(z, g, b):
    mu = jnp.mean(z, axis=1, keepdims=True)
    zc = z - mu
    var = jnp.mean(zc * zc, axis=1, keepdims=True)
    return zc * lax.rsqrt(var + EPS) * g + b


def _pack_bf16_pairs(v):
    n = v.shape[1] // 2
    lo = pltpu.bitcast(v[:, :n].astype(BF16).astype(F32), U32)
    hi = pltpu.bitcast(v[:, n:].astype(BF16).astype(F32), U32)
    return jnp.bitwise_or(jnp.right_shift(lo, jnp.uint32(16)), jnp.bitwise_and(hi, jnp.uint32(0xFFFF0000)))


def _unpack_bf16_pairs(w):
    lo = pltpu.bitcast(jnp.left_shift(w, jnp.uint32(16)), F32)
    hi = pltpu.bitcast(jnp.bitwise_and(w, jnp.uint32(0xFFFF0000)), F32)
    return lo, hi


def _merge_kernel(ya_ref, o1_ref, o2_ref, o3_ref, l1_ref, l2_ref, l3_ref, x_ref,
                  wg_ref, bg_ref, woa_ref, wob_ref, wo_ref, g_ref, b_ref, wrh_ref, wrl_ref, br_ref,
                  h_ref, hp_ref, topi_ref, topw_ref, rank_ref, cnt_ref, carry_ref):
    i = pl.program_id(0)
    tm = x_ref.shape[0]

    @pl.when(i == 0)
    def _():
        carry_ref[...] = jnp.zeros_like(carry_ref)

    x = x_ref[...]
    gates = jax.nn.sigmoid(jnp.dot(x.astype(BF16), wg_ref[...], preferred_element_type=F32) + bg_ref[...])
    la, lb, lc = l1_ref[...], l2_ref[...], l3_ref[...]
    mx = jnp.maximum(jnp.maximum(la, lb), lc)
    ea, eb, ec = jnp.exp(la - mx), jnp.exp(lb - mx), jnp.exp(lc - mx)
    yb = (ea * o1_ref[...].astype(F32) + eb * o2_ref[...].astype(F32) + ec * o3_ref[...].astype(F32)) / (ea + eb + ec)
    pa = jnp.dot(ya_ref[...], woa_ref[...], preferred_element_type=F32)
    pb = jnp.dot(yb.astype(BF16), wob_ref[...], preferred_element_type=F32)
    merged = gates[:, :D_MODEL] * pa + gates[:, D_MODEL:] * pb
    mix = jnp.dot(merged.astype(BF16), wo_ref[...], preferred_element_type=F32)
    h = _layer_norm(DN_ALPHA * x + mix, g_ref[...], b_ref[...])
    h_ref[...] = h
    hp_ref[...] = _pack_bf16_pairs(h)

    hh = h.astype(BF16)
    hl = (h - hh.astype(F32)).astype(BF16)
    logits = (jnp.dot(hh, wrh_ref[...], preferred_element_type=F32)
              + jnp.dot(hl, wrh_ref[...], preferred_element_type=F32)
              + jnp.dot(hh, wrl_ref[...], preferred_element_type=F32) + br_ref[...])
    lane = lax.broadcasted_iota(jnp.int32, (tm, LANES), 1)
    lg = jnp.where(lane < N_EXPERTS, logits, -jnp.inf)
    vals, idxs = [], []
    for _ in range(TOP_K):
        mv = jnp.max(lg, axis=1, keepdims=True)
        ik = jnp.min(jnp.where(lg == mv, lane, LANES), axis=1, keepdims=True)
        vals.append(mv)
        idxs.append(ik)
        lg = jnp.where(lane == ik, -jnp.inf, lg)
    es = [jnp.exp(v - vals[0]) for v in vals]
    tot = es[0] + es[1] + es[2] + es[3]

    onehot = jnp.zeros((tm, LANES), F32)
    for ik in idxs:
        onehot = onehot + (lane == ik).astype(F32)
    r_i = lax.broadcasted_iota(jnp.int32, (tm, tm), 0)
    c_i = lax.broadcasted_iota(jnp.int32, (tm, tm), 1)
    tri = (r_i > c_i).astype(BF16)
    pre = jnp.dot(tri, onehot.astype(BF16), preferred_element_type=F32) + carry_ref[...]

    topi = jnp.zeros((tm, LANES), jnp.int32)
    topw = jnp.zeros((tm, LANES), F32)
    rank = jnp.zeros((tm, LANES), F32)
    for k in range(TOP_K):
        rk = jnp.sum(jnp.where(lane == idxs[k], pre, 0.0), axis=1, keepdims=True)
        topi = jnp.where(lane == k, idxs[k], topi)
        topw = jnp.where(lane == k, es[k] / tot, topw)
        rank = jnp.where(lane == k, rk, rank)
    topi_ref[...] = topi
    topw_ref[...] = topw
    rank_ref[...] = rank.astype(jnp.int32)
    carry_ref[...] = carry_ref[...] + jnp.sum(onehot, axis=0, keepdims=True)
    cnt_ref[...] = carry_ref[...]


def _merge(ya, dil_outs, x2, w_gate_b, b_gate, w_oa_b, w_ob_b, w_o_b, ln1_g, ln1_b, w_router, b_router):
    tokens = x2.shape[0]
    tm = MERGE_TM
    wr = jnp.zeros((D_MODEL, LANES), F32).at[:, :N_EXPERTS].set(w_router)
    wr_hi = wr.astype(BF16)
    wr_lo = (wr - wr_hi.astype(F32)).astype(BF16)
    br = jnp.zeros((1, LANES), F32).at[0, :N_EXPERTS].set(b_router)
    row = lambda w: pl.BlockSpec((tm, w), lambda i: (i, 0))
    full = lambda a: pl.BlockSpec(a.shape, lambda i: (0,) * a.ndim)
    (o1, l1), (o2, l2), (o3, l3) = dil_outs
    bg = b_gate.reshape(1, -1)
    g1 = ln1_g.reshape(1, -1)
    b1 = ln1_b.reshape(1, -1)
    lane_out = lambda dt: jax.ShapeDtypeStruct((tokens, LANES), dt)
    return pl.pallas_call(
        _merge_kernel,
        grid=(tokens // tm,),
        in_specs=[row(SEG_W), row(SEG_W), row(SEG_W), row(SEG_W), row(SEG_W), row(SEG_W), row(SEG_W),
                  row(D_MODEL), full(w_gate_b), full(bg), full(w_oa_b), full(w_ob_b), full(w_o_b), full(g1), full(b1),
                  full(wr_hi), full(wr_lo), full(br)],
        out_specs=[row(D_MODEL), row(D_MODEL // 2), row(LANES), row(LANES), row(LANES),
                   pl.BlockSpec((1, LANES), lambda i: (0, 0))],
        out_shape=[jax.ShapeDtypeStruct((tokens, D_MODEL), F32), jax.ShapeDtypeStruct((tokens, D_MODEL // 2), U32),
                   lane_out(jnp.int32), lane_out(F32), lane_out(jnp.int32), jax.ShapeDtypeStruct((1, LANES), F32)],
        scratch_shapes=[pltpu.VMEM((1, LANES), F32)],
        compiler_params=_cparams(1),
        name="merge",
    )(ya, o1, o2, o3, l1, l2, l3, x2, w_gate_b, bg, w_oa_b, w_ob_b, w_o_b, g1, b1, wr_hi, wr_lo, br)


def _sc_mesh():
    return plsc.VectorSubcoreMesh(core_axis_name="core", subcore_axis_name="subcore")


def _sc_scatter_rows(table, src_idx, dst_idx):
    n = src_idx.shape[0]
    d = table.shape[1]
    mesh = _sc_mesh()
    workers = mesh.num_cores * mesh.num_subcores
    per = n // (SC_CHUNK * workers)
    assert per * SC_CHUNK * workers == n

    @pl.kernel(out_type=jax.ShapeDtypeStruct((n, d), table.dtype), mesh=mesh,
               scratch_types=[pltpu.VMEM((1, SC_CHUNK), jnp.int32), pltpu.VMEM((1, SC_CHUNK), jnp.int32),
                              pltpu.VMEM((SC_CHUNK, d), table.dtype)])
    def copy(t_hbm, s_hbm, d_hbm, o_hbm, s_vm, d_vm, buf):
        wid = lax.axis_index("core") * mesh.num_subcores + lax.axis_index("subcore")

        @pl.loop(0, per)
        def _(j):
            blk = wid * per + j
            pltpu.sync_copy(s_hbm.at[pl.ds(blk, 1)], s_vm)
            pltpu.sync_copy(d_hbm.at[pl.ds(blk, 1)], d_vm)
            pltpu.sync_copy(t_hbm.at[s_vm.at[0]], buf)
            pltpu.sync_copy(buf, o_hbm.at[d_vm.at[0]])

    return copy(table, src_idx.reshape(-1, SC_CHUNK), dst_idx.reshape(-1, SC_CHUNK))


def _sc_gather_rows(table, idx):
    n = idx.shape[0]
    d = table.shape[1]
    mesh = _sc_mesh()
    workers = mesh.num_cores * mesh.num_subcores
    per = n // (SC_CHUNK * workers)
    assert per * SC_CHUNK * workers == n

    @pl.kernel(out_type=jax.ShapeDtypeStruct((n, d), table.dtype), mesh=mesh,
               scratch_types=[pltpu.VMEM((1, SC_CHUNK), jnp.int32), pltpu.VMEM((SC_CHUNK, d), table.dtype)])
    def gather(t_hbm, i_hbm, o_hbm, i_vm, buf):
        wid = lax.axis_index("core") * mesh.num_subcores + lax.axis_index("subcore")

        @pl.loop(0, per)
        def _(j):
            blk = wid * per + j
            pltpu.sync_copy(i_hbm.at[pl.ds(blk, 1)], i_vm)
            pltpu.sync_copy(t_hbm.at[i_vm.at[0]], buf)
            pltpu.sync_copy(buf, o_hbm.at[pl.ds(blk * SC_CHUNK, SC_CHUNK)])

    return gather(table, idx.reshape(-1, SC_CHUNK))


def _expert_kernel(be_ref, nu_ref, x_ref, wgu_ref, bgu_ref, wd_ref, bd_ref, y_ref):
    i = pl.program_id(0)

    @pl.when(i < nu_ref[0])
    def _():
        lo, hi = _unpack_bf16_pairs(x_ref[...])
        xb = jnp.concatenate([lo, hi], axis=1).astype(BF16)
        chunk = 512
        cols = [slice(c * chunk, (c + 1) * chunk) for c in range(D_FF // chunk)]
        pre = []
        for gs in cols:
            us = slice(D_FF + gs.start, D_FF + gs.stop)
            gate = jnp.dot(xb, wgu_ref[:, gs].astype(BF16), preferred_element_type=F32) + bgu_ref[:, gs]
            up = jnp.dot(xb, wgu_ref[:, us].astype(BF16), preferred_element_type=F32) + bgu_ref[:, us]
            pre.append((gate, up))
        acc = jnp.zeros((x_ref.shape[0], D_MODEL), F32)
        for gs, (gate, up) in zip(cols, pre):
            gate = jnp.minimum(gate, SWIGLU_LIMIT)
            up = jnp.clip(up, -SWIGLU_LIMIT, SWIGLU_LIMIT)
            act = (up + 1.0) * gate * jax.nn.sigmoid(SWIGLU_ALPHA * gate)
            acc = acc + jnp.dot(act.astype(BF16), wd_ref[gs, :].astype(BF16), preferred_element_type=F32)
        y_ref[...] = _pack_bf16_pairs(acc + bd_ref[...])

    @pl.when(i >= nu_ref[0])
    def _():
        y_ref[...] = jnp.zeros_like(y_ref)


def _experts(xrows, block_e, n_used, w_gu, b_gu, w_down, b_down):
    n_rows = xrows.shape[0]
    bm = MOE_BM
    return pl.pallas_call(
        _expert_kernel,
        grid_spec=pltpu.PrefetchScalarGridSpec(
            num_scalar_prefetch=2,
            grid=(n_rows // bm,),
            in_specs=[
                pl.BlockSpec((bm, D_MODEL // 2), lambda i, be, nu: (i, 0)),
                pl.BlockSpec((None, D_MODEL, 2 * D_FF), lambda i, be, nu: (be[i], 0, 0)),
                pl.BlockSpec((None, 1, 2 * D_FF), lambda i, be, nu: (be[i], 0, 0)),
                pl.BlockSpec((None, D_FF, D_MODEL), lambda i, be, nu: (be[i], 0, 0)),
                pl.BlockSpec((None, 1, D_MODEL), lambda i, be, nu: (be[i], 0, 0)),
            ],
            out_specs=pl.BlockSpec((bm, D_MODEL // 2), lambda i, be, nu: (i, 0)),
        ),
        out_shape=jax.ShapeDtypeStruct((n_rows, D_MODEL // 2), U32),
        compiler_params=_cparams(1),
        name="experts",
    )(block_e, n_used, xrows, w_gu, b_gu.reshape(N_EXPERTS, 1, -1), w_down, b_down.reshape(N_EXPERTS, 1, -1))


def _combine_kernel(y0_ref, y1_ref, y2_ref, y3_ref, w_ref, h_ref, g_ref, b_ref, o_ref):
    w = w_ref[...]
    half = D_MODEL // 2
    f_lo = jnp.zeros((h_ref.shape[0], half), F32)
    f_hi = jnp.zeros((h_ref.shape[0], half), F32)
    for k, y_ref in enumerate((y0_ref, y1_ref, y2_ref, y3_ref)):
        lo, hi = _unpack_bf16_pairs(y_ref[...])
        f_lo = f_lo + lo * w[:, k:k + 1]
        f_hi = f_hi + hi * w[:, k:k + 1]
    f = jnp.concatenate([f_lo, f_hi], axis=1)
    o_ref[...] = _layer_norm(DN_ALPHA * h_ref[...] + f, g_ref[...], b_ref[...])


def _combine(yplanes, topw, h, ln2_g, ln2_b):
    tokens = h.shape[0]
    tm = MOE_TM
    per_plane = tokens // tm
    plane = lambda k: pl.BlockSpec((tm, D_MODEL // 2), lambda i: (k * per_plane + i, 0))
    return pl.pallas_call(
        _combine_kernel,
        grid=(per_plane,),
        in_specs=[plane(0), plane(1), plane(2), plane(3),
                  pl.BlockSpec((tm, LANES), lambda i: (i, 0)),
                  pl.BlockSpec((tm, D_MODEL), lambda i: (i, 0)),
                  pl.BlockSpec((1, D_MODEL), lambda i: (0, 0)),
                  pl.BlockSpec((1, D_MODEL), lambda i: (0, 0))],
        out_specs=pl.BlockSpec((tm, D_MODEL), lambda i: (i, 0)),
        out_shape=jax.ShapeDtypeStruct((tokens, D_MODEL), F32),
        compiler_params=_cparams(1),
        name="combine",
    )(yplanes, yplanes, yplanes, yplanes, topw, h, ln2_g.reshape(1, -1), ln2_b.reshape(1, -1))


def _routing_tables(topi, rank, cnt, tokens):
    bm = MOE_BM
    i32 = jnp.int32
    experts = jnp.arange(N_EXPERTS, dtype=i32)
    counts = cnt[0, :N_EXPERTS].astype(i32)
    padded = (counts + bm - 1) // bm * bm
    pend = jnp.cumsum(padded)
    pstart = pend - padded
    sel = topi[:, :TOP_K, None] == experts[None, None, :]
    dest = rank[:, :TOP_K] + jnp.sum(jnp.where(sel, pstart[None, None, :], 0), axis=-1)
    n_pad = N_EXPERTS * bm
    n_rows = tokens * TOP_K + n_pad
    starts = jnp.arange(n_rows // bm, dtype=i32) * bm
    block_e = jnp.minimum(jnp.sum((pend[None, :] <= starts[:, None]).astype(i32), axis=1), N_EXPERTS - 1)
    n_used = (pend[-1] // bm).reshape(1)
    pad_cnt = padded - counts
    pad_end = jnp.cumsum(pad_cnt)
    j = jnp.arange(n_pad, dtype=i32)
    owner = jnp.sum((pad_end[None, :] <= j[:, None]).astype(i32), axis=1)
    is_owner = owner[:, None] == experts[None, :]
    pick = lambda v: jnp.sum(jnp.where(is_owner, v[None, :], 0), axis=1)
    in_expert = pick(pstart + counts) + (j - pick(pad_end - pad_cnt))
    pad_rows = jnp.where(owner < N_EXPERTS, in_expert, pend[-1] + (j - pad_end[-1]))
    src_tok = jnp.concatenate([jnp.repeat(jnp.arange(tokens, dtype=i32), TOP_K), j % tokens])
    dst_row = jnp.concatenate([dest.reshape(-1), pad_rows]).astype(i32)
    return dest.astype(i32), block_e.astype(i32), n_used.astype(i32), src_tok, dst_row


def kernel(x, w_in, b_gate, lam_q1, lam_k1, lam_q2, lam_k2, subln_g, w_oa, w_ob, w_o, ln1_g, ln1_b,
           w_router, b_router, w_gu, b_gu, w_down, b_down, ln2_g, ln2_b):
    batch, seq, d = x.shape
    tokens = batch * seq
    h = x.reshape(tokens, d)
    tables = _rope_lane_tables(seq)
    for l in range(DEPTH):
        w_b = w_in[l].astype(BF16)
        hb = h.astype(BF16)
        qk_a, vt_a = _project_qkv(hb, w_b[:, :QKV_W], tables, seq, "qkv_diff",
                                  q_scale=HEAD_DIM ** -0.5 * math.log2(math.e), v_feature_major=True)
        ya = _diff_attention(qk_a.reshape(batch, seq, 2 * SEG_W), vt_a, lam_q1[l], lam_k1[l], lam_q2[l],
                             lam_k2[l], subln_g[l]).reshape(tokens, -1)
        dil = []
        for g, (_, dilation) in enumerate(DIL_PAIRS):
            w_g = w_b[:, (g + 1) * QKV_W:(g + 2) * QKV_W]
            if dilation == 1:
                hb_g, tab_g = hb, tables
            else:
                hb_g = hb.reshape(batch, seq // dilation, dilation, d).swapaxes(1, 2).reshape(tokens, d)
                tab_g = tuple(_residue_major(t, dilation) for t in tables)
            qkv_g = _project_qkv(hb_g, w_g, tab_g, seq, f"qkv_dil{g}", q_scale=HEAD_DIM ** -0.5)
            dil.append(_dilated_group(qkv_g, batch, seq, g))
        h1, h1p, topi, topw, rank, cnt = _merge(ya, dil, h, w_b[:, 4 * QKV_W:], b_gate[l], w_oa[l].astype(BF16),
                                                w_ob[l].astype(BF16), w_o[l].astype(BF16), ln1_g[l], ln1_b[l],
                                                w_router[l], b_router[l])
        dest, block_e, n_used, src_tok, dst_row = _routing_tables(topi, rank, cnt, tokens)
        xrows = _sc_scatter_rows(h1p, src_tok, dst_row)
        yrows = _experts(xrows, block_e, n_used, w_gu[l], b_gu[l], w_down[l], b_down[l])
        yplanes = _sc_gather_rows(yrows, dest.T.reshape(-1))
        h = _combine(yplanes, topw, h1, ln2_g[l], ln2_b[l])
    return h.reshape(batch, seq, d)
```

```python
import functools
import math

import jax
import jax.numpy as jnp
from jax import lax
from jax.experimental import pallas as pl
from jax.experimental.pallas import tpu as pltpu
from jax.experimental.pallas import tpu_sc as plsc

F32 = jnp.float32
BF16 = jnp.bfloat16
U32 = jnp.uint32

D_MODEL = 1024
HEAD_DIM = 64
ROT_DIM = HEAD_DIM // 4
ROPE_THETA = 500000.0
QBLK = 128
DA_HEADS = 4
DIL_PAIRS = ((128, 1), (512, 4), (2048, 16))
SEG_W = 512
QKV_W = 3 * SEG_W
N_EXPERTS = 32
TOP_K = 4
D_FF = D_MODEL
SWIGLU_ALPHA = 1.702
SWIGLU_LIMIT = 7.0
DEPTH = 1
DN_ALPHA = (2 * DEPTH) ** 0.25
EPS = 1e-5
LAMBDA_INIT = 0.8 - 0.6 * math.exp(-0.3 * 0)

LANES = 128
QKV_TM = 512
QKV_CHUNK = 256
ATT_TQ = 256
MERGE_TM = 256
MOE_BM = 256
MOE_TM = 256
SC_CHUNK = 128
VMEM_LIMIT = 52 * 1024 * 1024


def _cparams(n_axes):
    return pltpu.CompilerParams(dimension_semantics=("arbitrary",) * n_axes,
                                vmem_limit_bytes=VMEM_LIMIT)


def _qkv_kernel(x_ref, w_ref, cs_ref, o_ref, *maybe_vt_ref, q_scale):
    tm = x_ref.shape[0]
    half = ROT_DIM // 2
    in_head = lax.broadcasted_iota(jnp.int32, (QKV_CHUNK, LANES), 1) % HEAD_DIM
    first, second = in_head < half, (in_head >= half) & (in_head < ROT_DIM)
    for rc in range(tm // QKV_CHUNK):
        rows = slice(rc * QKV_CHUNK, (rc + 1) * QKV_CHUNK)
        xb = x_ref[rows, :]
        cs = cs_ref[rows, :]
        c = jnp.where(first, cs, jnp.where(second, pltpu.roll(cs, half, 1), 1.0))
        s1 = jnp.where(second, cs, 0.0)
        s2 = jnp.where(first, -pltpu.roll(cs, LANES - half, 1), 0.0)
        for seg in range(3):
            acc = jnp.dot(xb, w_ref[:, seg * SEG_W:(seg + 1) * SEG_W], preferred_element_type=F32)
            if seg == 2:
                if maybe_vt_ref:
                    maybe_vt_ref[0][:, rows] = acc.T.astype(BF16)
                else:
                    o_ref[rows, seg * SEG_W:(seg + 1) * SEG_W] = acc.astype(BF16)
                continue
            for k in range(SEG_W // LANES):
                t = acc[:, k * LANES:(k + 1) * LANES]
                r = t * c + pltpu.roll(t, ROT_DIM // 2, 1) * s1 + pltpu.roll(t, LANES - ROT_DIM // 2, 1) * s2
                if seg == 0:
                    r = r * q_scale
                lo = seg * SEG_W + k * LANES
                o_ref[rows, lo:lo + LANES] = r.astype(BF16)


def _rope_lane_table(seq, dil):
    row = jnp.arange(seq, dtype=jnp.int32)
    length = seq // dil
    pos = ((row % length) * dil + row // length).astype(F32)
    inv_freq = ROPE_THETA ** (-jnp.arange(0, ROT_DIM, 2, dtype=F32) / ROT_DIM)
    ang = pos[:, None] * inv_freq[None, :]
    head = jnp.concatenate([jnp.cos(ang), jnp.sin(ang), jnp.zeros((seq, HEAD_DIM - ROT_DIM), F32)], axis=1)
    return jnp.tile(head, (1, LANES // HEAD_DIM))


def _project_qkv(xb, w_b, col_block, table, seq, name, q_scale, v_feature_major=False):
    tokens = xb.shape[0]
    tm = min(QKV_TM, seq)
    per_seq = seq // tm
    tab_spec = pl.BlockSpec((tm, LANES), lambda i: (i % per_seq, 0))
    out_w = 2 * SEG_W if v_feature_major else QKV_W
    out_specs = [pl.BlockSpec((tm, out_w), lambda i: (i, 0))]
    out_shape = [jax.ShapeDtypeStruct((tokens, out_w), BF16)]
    if v_feature_major:
        out_specs.append(pl.BlockSpec((None, SEG_W, tm), lambda i: (i // per_seq, 0, i % per_seq)))
        out_shape.append(jax.ShapeDtypeStruct((tokens // seq, SEG_W, seq), BF16))
    outs = pl.pallas_call(
        functools.partial(_qkv_kernel, q_scale=q_scale),
        grid=(tokens // tm,),
        in_specs=[pl.BlockSpec((tm, D_MODEL), lambda i: (i, 0)),
                  pl.BlockSpec((D_MODEL, QKV_W), lambda i: (0, col_block)),
                  tab_spec],
        out_specs=out_specs,
        out_shape=out_shape,
        compiler_params=_cparams(1),
        name=name,
    )(xb, w_b, table)
    return outs if v_feature_major else outs[0]


def _nt_dot(a, b):
    return lax.dot_general(a, b, (((1,), (1,)), ((), ())), preferred_element_type=F32)


def _diff_kernel(q_ref, k_ref, vt_ref, lq1_ref, lk1_ref, lq2_ref, lk2_ref, g_ref, o_ref, s_ref, acc_ref, *, tq):
    i = pl.program_id(1)
    chains = 2 * DA_HEADS
    half = tq // 2
    lane = lax.broadcasted_iota(jnp.int32, (tq, LANES), 1)
    qs = []
    for h in range(DA_HEADS):
        q = q_ref[:, h * LANES:(h + 1) * LANES]
        zero = jnp.zeros_like(q)
        qs += [jnp.where(lane < HEAD_DIM, q, zero), jnp.where(lane >= HEAD_DIM, q, zero)]
    acc_ref[...] = jnp.zeros_like(acc_ref)

    def score_block(j, buf):
        start = pl.multiple_of(j * tq, tq)
        for c in range(chains):
            kb = k_ref[pl.ds(start, tq), (c // 2) * LANES:(c // 2 + 1) * LANES]
            buf[c] = _nt_dot(kb, qs[c])

    def absorb(j, buf, state, masked):
        start = pl.multiple_of(j * tq, tq)
        out = []
        for c in range(chains):
            m, l = state[c]
            if masked:
                key = lax.broadcasted_iota(jnp.int32, (tq, tq), 0)
                qry = lax.broadcasted_iota(jnp.int32, (tq, tq), 1)
                buf[c] = jnp.where(key <= qry, buf[c], -jnp.inf)
            m_new = jnp.maximum(m, jnp.max(buf[c], axis=0, keepdims=True))
            a = jnp.exp2(m - m_new)
            p = [jnp.exp2(buf[c, u * half:(u + 1) * half, :] - m_new) for u in range(2)]
            l = a * l + jnp.sum(p[0], axis=0, keepdims=True) + jnp.sum(p[1], axis=0, keepdims=True)
            vt = vt_ref[(c // 2) * LANES:(c // 2 + 1) * LANES, pl.ds(start, tq)]
            pv = jnp.dot(vt, jnp.concatenate(p, axis=0).astype(BF16), preferred_element_type=F32)
            acc_ref[c] = a * acc_ref[c] + pv
            out.append((m_new, l))
        return tuple(out)

    buf_a, buf_b = s_ref.at[0], s_ref.at[1]

    def two_blocks(jj, state):
        j = 2 * jj
        score_block(j + 1, buf_b)
        state = absorb(j, buf_a, state, False)
        score_block(j + 2, buf_a)
        return absorb(j + 1, buf_b, state, False)

    def last_from_a(state):
        return absorb(i, buf_a, state, True)

    def last_from_b(state):
        score_block(i, buf_b)
        state = absorb(i - 1, buf_a, state, False)
        return absorb(i, buf_b, state, True)

    init = tuple((jnp.full((1, tq), -jnp.inf, F32), jnp.zeros((1, tq), F32)) for _ in range(chains))
    score_block(0, buf_a)
    state = lax.fori_loop(0, i // 2, two_blocks, init)
    state = lax.cond(lax.rem(i, 2) == 0, last_from_a, last_from_b, state)

    lam = (jnp.exp(jnp.sum(lq1_ref[...] * lk1_ref[...], axis=1, keepdims=True))
           - jnp.exp(jnp.sum(lq2_ref[...] * lk2_ref[...], axis=1, keepdims=True)) + LAMBDA_INIT)
    for h in range(DA_HEADS):
        l1, l2 = state[2 * h][1], state[2 * h + 1][1]
        o = (acc_ref[2 * h] / l1 - lam * (acc_ref[2 * h + 1] / l2)).T
        ms = jnp.mean(o * o, axis=1, keepdims=True)
        o = o * lax.rsqrt(ms + EPS) * g_ref[...]
        o_ref[:, h * LANES:(h + 1) * LANES] = (o * (1.0 - LAMBDA_INIT)).astype(BF16)


def _diff_attention(qk3, vt3, lam_q1, lam_k1, lam_q2, lam_k2, subln_g):
    b, s, _ = qk3.shape
    tq = min(ATT_TQ, s)
    vec = lambda n: pl.BlockSpec((1, n), lambda bb, i: (0, 0))
    return pl.pallas_call(
        functools.partial(_diff_kernel, tq=tq),
        grid=(b, s // tq),
        in_specs=[
            pl.BlockSpec((None, tq, SEG_W), lambda bb, i: (bb, i, 0)),
            pl.BlockSpec((None, s, SEG_W), lambda bb, i: (bb, 0, 1)),
            pl.BlockSpec((None, SEG_W, s), lambda bb, i: (bb, 0, 0)),
            vec(HEAD_DIM), vec(HEAD_DIM), vec(HEAD_DIM), vec(HEAD_DIM), vec(2 * HEAD_DIM),
        ],
        out_specs=pl.BlockSpec((None, tq, SEG_W), lambda bb, i: (bb, i, 0)),
        out_shape=jax.ShapeDtypeStruct((b, s, SEG_W), BF16),
        scratch_shapes=[pltpu.VMEM((2, 2 * DA_HEADS, tq, tq), F32), pltpu.VMEM((2 * DA_HEADS, LANES, tq), F32)],
        compiler_params=_cparams(2),
        name="diff_attn",
    )(qk3, qk3, vt3, lam_q1.reshape(1, -1), lam_k1.reshape(1, -1), lam_q2.reshape(1, -1),
      lam_k2.reshape(1, -1), subln_g.reshape(1, -1))


def _dil_kernel(q_ref, kp_ref, kc_ref, vp_ref, vc_ref, o_ref, lse_ref, *, rel):
    n = pl.program_id(2)
    qi = lax.broadcasted_iota(jnp.int32, (QBLK, 2 * QBLK), 0)
    kj = lax.broadcasted_iota(jnp.int32, (QBLK, 2 * QBLK), 1)
    dist = qi + QBLK - kj
    valid = (dist >= 0) & (dist <= rel) & ((kj >= QBLK) | (n > 0))
    lane = lax.broadcasted_iota(jnp.int32, (QBLK, LANES), 1)
    low = lane < HEAD_DIM
    slabs = [slice(p * LANES, (p + 1) * LANES) for p in range(SEG_W // LANES)]
    scores = []
    for sl in slabs:
        q2 = q_ref[:, sl]
        k2 = jnp.concatenate([kp_ref[:, sl], kc_ref[:, sl]], axis=0)
        for sel in (low, jnp.logical_not(low)):
            scores.append(_nt_dot(jnp.where(sel, q2, jnp.zeros_like(q2)), k2))
    probs = []
    for s in scores:
        s = jnp.where(valid, s, -jnp.inf)
        m = jnp.max(s, axis=1, keepdims=True)
        pe = jnp.exp(s - m)
        den = jnp.sum(pe, axis=1, keepdims=True)
        probs.append((pe.astype(BF16), den, m + jnp.log(den)))
    for p, sl in enumerate(slabs):
        v2 = jnp.concatenate([vp_ref[:, sl], vc_ref[:, sl]], axis=0)
        outs = [jnp.dot(pe, v2, preferred_element_type=F32) / den for pe, den, _ in probs[2 * p:2 * p + 2]]
        o_ref[:, sl] = jnp.where(low, outs[0], outs[1]).astype(BF16)
        lse_ref[:, sl] = jnp.where(low, probs[2 * p][2], probs[2 * p + 1][2])


def _dilated_group(qkv, batch, seq, group):
    window, dil = DIL_PAIRS[group]
    rel = window // dil
    length = seq // dil
    nb = length // QBLK
    view = qkv.reshape(batch, dil, length, QKV_W)

    def spec(seg, prev):
        if prev:
            return pl.BlockSpec((None, None, QBLK, SEG_W), lambda b, r, n: (b, r, jnp.maximum(n - 1, 0), seg))
        return pl.BlockSpec((None, None, QBLK, SEG_W), lambda b, r, n: (b, r, n, seg))

    out_spec = pl.BlockSpec((None, None, QBLK, SEG_W), lambda b, r, n: (b, r, n, 0))
    o, lse = pl.pallas_call(
        functools.partial(_dil_kernel, rel=rel),
        grid=(batch, dil, nb),
        in_specs=[spec(0, False), spec(1, True), spec(1, False), spec(2, True), spec(2, False)],
        out_specs=[out_spec, out_spec],
        out_shape=[jax.ShapeDtypeStruct((batch, dil, length, SEG_W), BF16),
                   jax.ShapeDtypeStruct((batch, dil, length, SEG_W), F32)],
        compiler_params=_cparams(3),
        name=f"dilated{group}",
    )(view, view, view, view, view)
    return o, lse


def _layer_norm(z, g, b):
    mu = jnp.mean(z, axis=1, keepdims=True)
    zc = z - mu
    var = jnp.mean(zc * zc, axis=1, keepdims=True)
    return zc * lax.rsqrt(var + EPS) * g + b


def _pack_bf16_pairs(v):
    n = v.shape[1] // 2
    lo = pltpu.bitcast(v[:, :n].astype(BF16).astype(F32), U32)
    hi = pltpu.bitcast(v[:, n:].astype(BF16).astype(F32), U32)
    return jnp.bitwise_or(jnp.right_shift(lo, jnp.uint32(16)), jnp.bitwise_and(hi, jnp.uint32(0xFFFF0000)))


def _unpack_bf16_pairs(w):
    lo = pltpu.bitcast(jnp.left_shift(w, jnp.uint32(16)), F32)
    hi = pltpu.bitcast(jnp.bitwise_and(w, jnp.uint32(0xFFFF0000)), F32)
    return lo, hi


def _merge_kernel(ya_ref, o1_ref, o2_ref, o3_ref, l1_ref, l2_ref, l3_ref, x_ref,
                  wg_ref, bg_ref, woa_ref, wob_ref, wo_ref, g_ref, b_ref, wrh_ref, wrl_ref, br_ref,
                  h_ref, hp_ref, topi_ref, topw_ref, rank_ref, cnt_ref, carry_ref, scr_ref):
    i = pl.program_id(0)
    tm = x_ref.shape[0]

    @pl.when(i == 0)
    def _():
        carry_ref[...] = jnp.zeros_like(carry_ref)

    x = x_ref[...]
    gates = jax.nn.sigmoid(jnp.dot(x.astype(BF16), wg_ref[...], preferred_element_type=F32) + bg_ref[...])

    def token_major(ref, scr):
        dil, n = ref.shape[0], ref.shape[1]
        if dil == 1:
            return ref[0].astype(F32)
        for r in range(dil):
            blk = ref[r].astype(F32)
            for c in range(SEG_W // LANES):
                scr[c, pl.ds(r, n, stride=dil), :] = blk[:, c * LANES:(c + 1) * LANES]
        return jnp.concatenate([scr[c] for c in range(SEG_W // LANES)], axis=1)

    oa, ob, oc = (token_major(r, scr_ref.at[k]) for k, r in enumerate((o1_ref, o2_ref, o3_ref)))
    la, lb, lc = (token_major(r, scr_ref.at[3 + k]) for k, r in enumerate((l1_ref, l2_ref, l3_ref)))
    mx = jnp.maximum(jnp.maximum(la, lb), lc)
    ea, eb, ec = jnp.exp(la - mx), jnp.exp(lb - mx), jnp.exp(lc - mx)
    yb = (ea * oa + eb * ob + ec * oc) / (ea + eb + ec)
    pa = jnp.dot(ya_ref[...], woa_ref[...], preferred_element_type=F32)
    pb = jnp.dot(yb.astype(BF16), wob_ref[...], preferred_element_type=F32)
    merged = gates[:, :D_MODEL] * pa + gates[:, D_MODEL:] * pb
    mix = jnp.dot(merged.astype(BF16), wo_ref[...], preferred_element_type=F32)
    h = _layer_norm(DN_ALPHA * x + mix, g_ref[...], b_ref[...])
    h_ref[...] = h
    hp_ref[...] = _pack_bf16_pairs(h)

    hh = h.astype(BF16)
    hl = (h - hh.astype(F32)).astype(BF16)
    logits = (jnp.dot(hh, wrh_ref[...], preferred_element_type=F32)
              + jnp.dot(hl, wrh_ref[...], preferred_element_type=F32)
              + jnp.dot(hh, wrl_ref[...], preferred_element_type=F32) + br_ref[...])
    lane = lax.broadcasted_iota(jnp.int32, (tm, LANES), 1)
    lg = jnp.where(lane < N_EXPERTS, logits, -jnp.inf)
    vals, idxs = [], []
    for _ in range(TOP_K):
        mv = jnp.max(lg, axis=1, keepdims=True)
        ik = jnp.min(jnp.where(lg == mv, lane, LANES), axis=1, keepdims=True)
        vals.append(mv)
        idxs.append(ik)
        lg = jnp.where(lane == ik, -jnp.inf, lg)
    es = [jnp.exp(v - vals[0]) for v in vals]
    tot = es[0] + es[1] + es[2] + es[3]

    onehot = jnp.zeros((tm, LANES), F32)
    for ik in idxs:
        onehot = onehot + (lane == ik).astype(F32)
    r_i = lax.broadcasted_iota(jnp.int32, (tm, tm), 0)
    c_i = lax.broadcasted_iota(jnp.int32, (tm, tm), 1)
    tri = (r_i > c_i).astype(BF16)
    pre = jnp.dot(tri, onehot.astype(BF16), preferred_element_type=F32) + carry_ref[...]

    topi = jnp.zeros((tm, LANES), jnp.int32)
    topw = jnp.zeros((tm, LANES), F32)
    rank = jnp.zeros((tm, LANES), F32)
    for k in range(TOP_K):
        rk = jnp.sum(jnp.where(lane == idxs[k], pre, 0.0), axis=1, keepdims=True)
        topi = jnp.where(lane == k, idxs[k], topi)
        topw = jnp.where(lane == k, es[k] / tot, topw)
        rank = jnp.where(lane == k, rk, rank)
    topi_ref[...] = topi
    topw_ref[...] = topw
    rank_ref[...] = rank.astype(jnp.int32)
    carry_ref[...] = carry_ref[...] + jnp.sum(onehot, axis=0, keepdims=True)
    cnt_ref[...] = carry_ref[...]


def _merge(ya, dil_outs, x2, seq, w_b, b_gate, w_oa_b, w_ob_b, w_o_b, ln1_g, ln1_b, w_router, b_router):
    tokens = x2.shape[0]
    tm = MERGE_TM
    per_seq = seq // tm

    def group(a):
        dil = a.shape[1]
        return pl.BlockSpec((None, dil, tm // dil, SEG_W), lambda i: (i // per_seq, 0, i % per_seq, 0))

    gate_blk = w_b.shape[1] // (2 * D_MODEL) - 1
    wr = jnp.zeros((D_MODEL, LANES), F32).at[:, :N_EXPERTS].set(w_router)
    wr_hi = wr.astype(BF16)
    wr_lo = (wr - wr_hi.astype(F32)).astype(BF16)
    br = jnp.zeros((1, LANES), F32).at[0, :N_EXPERTS].set(b_router)
    row = lambda w: pl.BlockSpec((tm, w), lambda i: (i, 0))
    full = lambda a: pl.BlockSpec(a.shape, lambda i: (0,) * a.ndim)
    (o1, l1), (o2, l2), (o3, l3) = dil_outs
    bg = b_gate.reshape(1, -1)
    g1 = ln1_g.reshape(1, -1)
    b1 = ln1_b.reshape(1, -1)
    lane_out = lambda dt: jax.ShapeDtypeStruct((tokens, LANES), dt)
    return pl.pallas_call(
        _merge_kernel,
        grid=(tokens // tm,),
        in_specs=[row(SEG_W), group(o1), group(o2), group(o3), group(l1), group(l2), group(l3),
                  row(D_MODEL), pl.BlockSpec((D_MODEL, 2 * D_MODEL), lambda i: (0, gate_blk)),
                  full(bg), full(w_oa_b), full(w_ob_b), full(w_o_b), full(g1), full(b1),
                  full(wr_hi), full(wr_lo), full(br)],
        out_specs=[row(D_MODEL), row(D_MODEL // 2), row(LANES), row(LANES), row(LANES),
                   pl.BlockSpec((1, LANES), lambda i: (0, 0))],
        out_shape=[jax.ShapeDtypeStruct((tokens, D_MODEL), F32), jax.ShapeDtypeStruct((tokens, D_MODEL // 2), U32),
                   lane_out(jnp.int32), lane_out(F32), lane_out(jnp.int32), jax.ShapeDtypeStruct((1, LANES), F32)],
        scratch_shapes=[pltpu.VMEM((1, LANES), F32), pltpu.VMEM((6, SEG_W // LANES, tm, LANES), F32)],
        compiler_params=_cparams(1),
        name="merge",
    )(ya, o1, o2, o3, l1, l2, l3, x2, w_b, bg, w_oa_b, w_ob_b, w_o_b, g1, b1, wr_hi, wr_lo, br)


def _sc_mesh():
    return plsc.VectorSubcoreMesh(core_axis_name="core", subcore_axis_name="subcore")


def _sc_scatter_rows(table, src_idx, dst_idx):
    n = src_idx.shape[0]
    d = table.shape[1]
    mesh = _sc_mesh()
    workers = mesh.num_cores * mesh.num_subcores
    per = n // (SC_CHUNK * workers)
    assert per * SC_CHUNK * workers == n

    @pl.kernel(out_type=jax.ShapeDtypeStruct((n, d), table.dtype), mesh=mesh,
               scratch_types=[pltpu.VMEM((1, SC_CHUNK), jnp.int32), pltpu.VMEM((1, SC_CHUNK), jnp.int32),
                              pltpu.VMEM((SC_CHUNK, d), table.dtype)])
    def copy(t_hbm, s_hbm, d_hbm, o_hbm, s_vm, d_vm, buf):
        wid = lax.axis_index("core") * mesh.num_subcores + lax.axis_index("subcore")

        @pl.loop(0, per)
        def _(j):
            blk = wid * per + j
            pltpu.sync_copy(s_hbm.at[pl.ds(blk, 1)], s_vm)
            pltpu.sync_copy(d_hbm.at[pl.ds(blk, 1)], d_vm)
            pltpu.sync_copy(t_hbm.at[s_vm.at[0]], buf)
            pltpu.sync_copy(buf, o_hbm.at[d_vm.at[0]])

    return copy(table, src_idx.reshape(-1, SC_CHUNK), dst_idx.reshape(-1, SC_CHUNK))


def _sc_gather_rows(table, idx):
    n = idx.shape[0]
    d = table.shape[1]
    mesh = _sc_mesh()
    workers = mesh.num_cores * mesh.num_subcores
    per = n // (SC_CHUNK * workers)
    assert per * SC_CHUNK * workers == n

    @pl.kernel(out_type=jax.ShapeDtypeStruct((n, d), table.dtype), mesh=mesh,
               scratch_types=[pltpu.VMEM((1, SC_CHUNK), jnp.int32), pltpu.VMEM((SC_CHUNK, d), table.dtype)])
    def gather(t_hbm, i_hbm, o_hbm, i_vm, buf):
        wid = lax.axis_index("core") * mesh.num_subcores + lax.axis_index("subcore")

        @pl.loop(0, per)
        def _(j):
            blk = wid * per + j
            pltpu.sync_copy(i_hbm.at[pl.ds(blk, 1)], i_vm)
            pltpu.sync_copy(t_hbm.at[i_vm.at[0]], buf)
            pltpu.sync_copy(buf, o_hbm.at[pl.ds(blk * SC_CHUNK, SC_CHUNK)])

    return gather(table, idx.reshape(-1, SC_CHUNK))


def _expert_kernel(be_ref, nu_ref, x_ref, wgu_ref, bgu_ref, wd_ref, bd_ref, y_ref):
    i = pl.program_id(0)

    @pl.when(i < nu_ref[0])
    def _():
        lo, hi = _unpack_bf16_pairs(x_ref[...])
        xb = jnp.concatenate([lo, hi], axis=1).astype(BF16)
        chunk = 512
        cols = [slice(c * chunk, (c + 1) * chunk) for c in range(D_FF // chunk)]
        pre = []
        for gs in cols:
            us = slice(D_FF + gs.start, D_FF + gs.stop)
            gate = jnp.dot(xb, wgu_ref[:, gs].astype(BF16), preferred_element_type=F32) + bgu_ref[:, gs]
            up = jnp.dot(xb, wgu_ref[:, us].astype(BF16), preferred_element_type=F32) + bgu_ref[:, us]
            pre.append((gate, up))
        acc = jnp.zeros((x_ref.shape[0], D_MODEL), F32)
        for gs, (gate, up) in zip(cols, pre):
            gate = jnp.minimum(gate, SWIGLU_LIMIT)
            up = jnp.clip(up, -SWIGLU_LIMIT, SWIGLU_LIMIT)
            act = (up + 1.0) * gate * jax.nn.sigmoid(SWIGLU_ALPHA * gate)
            acc = acc + jnp.dot(act.astype(BF16), wd_ref[gs, :].astype(BF16), preferred_element_type=F32)
        y_ref[...] = _pack_bf16_pairs(acc + bd_ref[...])

    @pl.when(i >= nu_ref[0])
    def _():
        y_ref[...] = jnp.zeros_like(y_ref)


def _experts(xrows, block_e, n_used, w_gu, b_gu, w_down, b_down):
    n_rows = xrows.shape[0]
    bm = MOE_BM
    return pl.pallas_call(
        _expert_kernel,
        grid_spec=pltpu.PrefetchScalarGridSpec(
            num_scalar_prefetch=2,
            grid=(n_rows // bm,),
            in_specs=[
                pl.BlockSpec((bm, D_MODEL // 2), lambda i, be, nu: (i, 0)),
                pl.BlockSpec((None, D_MODEL, 2 * D_FF), lambda i, be, nu: (be[i], 0, 0)),
                pl.BlockSpec((None, 1, 2 * D_FF), lambda i, be, nu: (be[i], 0, 0)),
                pl.BlockSpec((None, D_FF, D_MODEL), lambda i, be, nu: (be[i], 0, 0)),
                pl.BlockSpec((None, 1, D_MODEL), lambda i, be, nu: (be[i], 0, 0)),
            ],
            out_specs=pl.BlockSpec((bm, D_MODEL // 2), lambda i, be, nu: (i, 0)),
        ),
        out_shape=jax.ShapeDtypeStruct((n_rows, D_MODEL // 2), U32),
        compiler_params=_cparams(1),
        name="experts",
    )(block_e, n_used, xrows, w_gu, b_gu.reshape(N_EXPERTS, 1, -1), w_down, b_down.reshape(N_EXPERTS, 1, -1))


def _combine_kernel(y0_ref, y1_ref, y2_ref, y3_ref, w_ref, h_ref, g_ref, b_ref, o_ref):
    w = w_ref[...]
    half = D_MODEL // 2
    f_lo = jnp.zeros((h_ref.shape[0], half), F32)
    f_hi = jnp.zeros((h_ref.shape[0], half), F32)
    for k, y_ref in enumerate((y0_ref, y1_ref, y2_ref, y3_ref)):
        lo, hi = _unpack_bf16_pairs(y_ref[...])
        f_lo = f_lo + lo * w[:, k:k + 1]
        f_hi = f_hi + hi * w[:, k:k + 1]
    f = jnp.concatenate([f_lo, f_hi], axis=1)
    o_ref[...] = _layer_norm(DN_ALPHA * h_ref[...] + f, g_ref[...], b_ref[...])


def _combine(yplanes, topw, h, ln2_g, ln2_b):
    tokens = h.shape[0]
    tm = MOE_TM
    per_plane = tokens // tm
    plane = lambda k: pl.BlockSpec((tm, D_MODEL // 2), lambda i: (k * per_plane + i, 0))
    return pl.pallas_call(
        _combine_kernel,
        grid=(per_plane,),
        in_specs=[plane(0), plane(1), plane(2), plane(3),
                  pl.BlockSpec((tm, LANES), lambda i: (i, 0)),
                  pl.BlockSpec((tm, D_MODEL), lambda i: (i, 0)),
                  pl.BlockSpec((1, D_MODEL), lambda i: (0, 0)),
                  pl.BlockSpec((1, D_MODEL), lambda i: (0, 0))],
        out_specs=pl.BlockSpec((tm, D_MODEL), lambda i: (i, 0)),
        out_shape=jax.ShapeDtypeStruct((tokens, D_MODEL), F32),
        compiler_params=_cparams(1),
        name="combine",
    )(yplanes, yplanes, yplanes, yplanes, topw, h, ln2_g.reshape(1, -1), ln2_b.reshape(1, -1))


def _routing_tables(topi, rank, cnt, tokens):
    bm = MOE_BM
    i32 = jnp.int32
    experts = jnp.arange(N_EXPERTS, dtype=i32)
    counts = cnt[0, :N_EXPERTS].astype(i32)
    padded = (counts + bm - 1) // bm * bm
    pend = jnp.cumsum(padded)
    pstart = pend - padded
    sel = topi[:, :TOP_K, None] == experts[None, None, :]
    dest = rank[:, :TOP_K] + jnp.sum(jnp.where(sel, pstart[None, None, :], 0), axis=-1)
    n_pad = N_EXPERTS * bm
    n_rows = tokens * TOP_K + n_pad
    starts = jnp.arange(n_rows // bm, dtype=i32) * bm
    block_e = jnp.minimum(jnp.sum((pend[None, :] <= starts[:, None]).astype(i32), axis=1), N_EXPERTS - 1)
    n_used = (pend[-1] // bm).reshape(1)
    pad_cnt = padded - counts
    pad_end = jnp.cumsum(pad_cnt)
    j = jnp.arange(n_pad, dtype=i32)
    owner = jnp.sum((pad_end[None, :] <= j[:, None]).astype(i32), axis=1)
    is_owner = owner[:, None] == experts[None, :]
    pick = lambda v: jnp.sum(jnp.where(is_owner, v[None, :], 0), axis=1)
    in_expert = pick(pstart + counts) + (j - pick(pad_end - pad_cnt))
    pad_rows = jnp.where(owner < N_EXPERTS, in_expert, pend[-1] + (j - pad_end[-1]))
    src_tok = jnp.concatenate([jnp.repeat(jnp.arange(tokens, dtype=i32), TOP_K), j % tokens])
    dst_row = jnp.concatenate([dest.reshape(-1), pad_rows]).astype(i32)
    return dest.astype(i32), block_e.astype(i32), n_used.astype(i32), src_tok, dst_row


def kernel(x, w_in, b_gate, lam_q1, lam_k1, lam_q2, lam_k2, subln_g, w_oa, w_ob, w_o, ln1_g, ln1_b,
           w_router, b_router, w_gu, b_gu, w_down, b_down, ln2_g, ln2_b):
    batch, seq, d = x.shape
    tokens = batch * seq
    h = x.reshape(tokens, d)
    tables = {dilation: _rope_lane_table(seq, dilation) for _, dilation in DIL_PAIRS}
    for l in range(DEPTH):
        w_b = w_in[l].astype(BF16)
        hb = h.astype(BF16)
        qk_a, vt_a = _project_qkv(hb, w_b, 0, tables[1], seq, "qkv_diff",
                                  q_scale=HEAD_DIM ** -0.5 * math.log2(math.e), v_feature_major=True)
        ya = _diff_attention(qk_a.reshape(batch, seq, 2 * SEG_W), vt_a, lam_q1[l], lam_k1[l], lam_q2[l],
                             lam_k2[l], subln_g[l]).reshape(tokens, -1)
        dil = []
        for g, (_, dilation) in enumerate(DIL_PAIRS):
            if dilation == 1:
                hb_g = hb
            else:
                hb_g = hb.reshape(batch, seq // dilation, dilation, d).swapaxes(1, 2).reshape(tokens, d)
            qkv_g = _project_qkv(hb_g, w_b, g + 1, tables[dilation], seq, f"qkv_dil{g}", q_scale=HEAD_DIM ** -0.5)
            dil.append(_dilated_group(qkv_g, batch, seq, g))
        h1, h1p, topi, topw, rank, cnt = _merge(ya, dil, h, seq, w_b, b_gate[l], w_oa[l].astype(BF16),
                                                w_ob[l].astype(BF16), w_o[l].astype(BF16), ln1_g[l], ln1_b[l],
                                                w_router[l], b_router[l])
        dest, block_e, n_used, src_tok, dst_row = _routing_tables(topi, rank, cnt, tokens)
        xrows = _sc_scatter_rows(h1p, src_tok, dst_row)
        yrows = _experts(xrows, block_e, n_used, w_gu[l], b_gu[l], w_down[l], b_down[l])
        yplanes = _sc_gather_rows(yrows, dest.T.reshape(-1))
        h = _combine(yplanes, topw, h1, ln2_g[l], ln2_b[l])
    return h.reshape(batch, seq, d)
```

```python
import functools
import math

import jax
import jax.numpy as jnp
from jax import lax
from jax.experimental import pallas as pl
from jax.experimental.pallas import tpu as pltpu
from jax.experimental.pallas import tpu_sc as plsc

F32 = jnp.float32
BF16 = jnp.bfloat16
U32 = jnp.uint32

D_MODEL = 1024
HEAD_DIM = 64
ROT_DIM = HEAD_DIM // 4
ROPE_THETA = 500000.0
QBLK = 128
DA_HEADS = 4
DIL_PAIRS = ((128, 1), (512, 4), (2048, 16))
SEG_W = 512
QKV_W = 3 * SEG_W
N_EXPERTS = 32
TOP_K = 4
D_FF = D_MODEL
SWIGLU_ALPHA = 1.702
SWIGLU_LIMIT = 7.0
DEPTH = 1
DN_ALPHA = (2 * DEPTH) ** 0.25
EPS = 1e-5
LAMBDA_INIT = 0.8 - 0.6 * math.exp(-0.3 * 0)

LANES = 128
QKV_TM = 512
QKV_CHUNK = 256
ATT_TQ = 256
MERGE_TM = 256
MOE_BM = 256
MOE_TM = 256
SC_CHUNK = 128
VMEM_LIMIT = 52 * 1024 * 1024


def _cparams(n_axes):
    return pltpu.CompilerParams(dimension_semantics=("arbitrary",) * n_axes,
                                vmem_limit_bytes=VMEM_LIMIT)


def _qkv_kernel(x_ref, w_ref, cs_ref, o_ref, *maybe_vt_ref, q_scale):
    tm = x_ref.shape[0]
    half = ROT_DIM // 2
    in_head = lax.broadcasted_iota(jnp.int32, (QKV_CHUNK, LANES), 1) % HEAD_DIM
    first, second = in_head < half, (in_head >= half) & (in_head < ROT_DIM)
    for rc in range(tm // QKV_CHUNK):
        rows = slice(rc * QKV_CHUNK, (rc + 1) * QKV_CHUNK)
        xb = x_ref[rows, :]
        cs = cs_ref[rows, :]
        c = jnp.where(first, cs, jnp.where(second, pltpu.roll(cs, half, 1), 1.0))
        s1 = jnp.where(second, cs, 0.0)
        s2 = jnp.where(first, -pltpu.roll(cs, LANES - half, 1), 0.0)
        for seg in range(3):
            acc = jnp.dot(xb, w_ref[:, seg * SEG_W:(seg + 1) * SEG_W], preferred_element_type=F32)
            if seg == 2:
                if maybe_vt_ref:
                    maybe_vt_ref[0][:, rows] = acc.T.astype(BF16)
                else:
                    o_ref[rows, seg * SEG_W:(seg + 1) * SEG_W] = acc.astype(BF16)
                continue
            for k in range(SEG_W // LANES):
                t = acc[:, k * LANES:(k + 1) * LANES]
                r = t * c + pltpu.roll(t, ROT_DIM // 2, 1) * s1 + pltpu.roll(t, LANES - ROT_DIM // 2, 1) * s2
                if seg == 0:
                    r = r * q_scale
                lo = seg * SEG_W + k * LANES
                o_ref[rows, lo:lo + LANES] = r.astype(BF16)


def _rope_lane_table(seq, dil):
    row = jnp.arange(seq, dtype=jnp.int32)
    length = seq // dil
    pos = ((row % length) * dil + row // length).astype(F32)
    inv_freq = ROPE_THETA ** (-jnp.arange(0, ROT_DIM, 2, dtype=F32) / ROT_DIM)
    ang = pos[:, None] * inv_freq[None, :]
    head = jnp.concatenate([jnp.cos(ang), jnp.sin(ang), jnp.zeros((seq, HEAD_DIM - ROT_DIM), F32)], axis=1)
    return jnp.tile(head, (1, LANES // HEAD_DIM))


def _project_qkv(xb, w_b, col_block, table, seq, name, q_scale, v_feature_major=False):
    tokens = xb.shape[0]
    tm = min(QKV_TM, seq)
    per_seq = seq // tm
    tab_spec = pl.BlockSpec((tm, LANES), lambda i: (i % per_seq, 0))
    out_w = 2 * SEG_W if v_feature_major else QKV_W
    out_specs = [pl.BlockSpec((tm, out_w), lambda i: (i, 0))]
    out_shape = [jax.ShapeDtypeStruct((tokens, out_w), BF16)]
    if v_feature_major:
        out_specs.append(pl.BlockSpec((None, SEG_W, tm), lambda i: (i // per_seq, 0, i % per_seq)))
        out_shape.append(jax.ShapeDtypeStruct((tokens // seq, SEG_W, seq), BF16))
    outs = pl.pallas_call(
        functools.partial(_qkv_kernel, q_scale=q_scale),
        grid=(tokens // tm,),
        in_specs=[pl.BlockSpec((tm, D_MODEL), lambda i: (i, 0)),
                  pl.BlockSpec((D_MODEL, QKV_W), lambda i: (0, col_block)),
                  tab_spec],
        out_specs=out_specs,
        out_shape=out_shape,
        compiler_params=_cparams(1),
        name=name,
    )(xb, w_b, table)
    return outs if v_feature_major else outs[0]


def _nt_dot(a, b):
    return lax.dot_general(a, b, (((1,), (1,)), ((), ())), preferred_element_type=F32)


def _diff_kernel(q_ref, k_ref, vt_ref, lq1_ref, lk1_ref, lq2_ref, lk2_ref, g_ref, o_ref, s_ref, acc_ref, *, tq):
    i = pl.program_id(1)
    chains = 2 * DA_HEADS
    half = tq // 2
    lane = lax.broadcasted_iota(jnp.int32, (tq, LANES), 1)
    qs = []
    for h in range(DA_HEADS):
        q = q_ref[:, h * LANES:(h + 1) * LANES]
        zero = jnp.zeros_like(q)
        qs += [jnp.where(lane < HEAD_DIM, q, zero), jnp.where(lane >= HEAD_DIM, q, zero)]
    acc_ref[...] = jnp.zeros_like(acc_ref)

    def score_block(j, buf):
        start = pl.multiple_of(j * tq, tq)
        for c in range(chains):
            kb = k_ref[pl.ds(start, tq), (c // 2) * LANES:(c // 2 + 1) * LANES]
            buf[c] = _nt_dot(kb, qs[c])

    def absorb(j, buf, state, masked):
        start = pl.multiple_of(j * tq, tq)
        out = []
        for c in range(chains):
            m, l = state[c]
            if masked:
                key = lax.broadcasted_iota(jnp.int32, (tq, tq), 0)
                qry = lax.broadcasted_iota(jnp.int32, (tq, tq), 1)
                buf[c] = jnp.where(key <= qry, buf[c], -jnp.inf)
            m_new = jnp.maximum(m, jnp.max(buf[c], axis=0, keepdims=True))
            a = jnp.exp2(m - m_new)
            p = [jnp.exp2(buf[c, u * half:(u + 1) * half, :] - m_new) for u in range(2)]
            l = a * l + jnp.sum(p[0], axis=0, keepdims=True) + jnp.sum(p[1], axis=0, keepdims=True)
            vt = vt_ref[(c // 2) * LANES:(c // 2 + 1) * LANES, pl.ds(start, tq)]
            pv = jnp.dot(vt, jnp.concatenate(p, axis=0).astype(BF16), preferred_element_type=F32)
            acc_ref[c] = a * acc_ref[c] + pv
            out.append((m_new, l))
        return tuple(out)

    buf_a, buf_b = s_ref.at[0], s_ref.at[1]

    def two_blocks(jj, state):
        j = 2 * jj
        score_block(j + 1, buf_b)
        state = absorb(j, buf_a, state, False)
        score_block(j + 2, buf_a)
        return absorb(j + 1, buf_b, state, False)

    def last_from_a(state):
        return absorb(i, buf_a, state, True)

    def last_from_b(state):
        score_block(i, buf_b)
        state = absorb(i - 1, buf_a, state, False)
        return absorb(i, buf_b, state, True)

    init = tuple((jnp.full((1, tq), -jnp.inf, F32), jnp.zeros((1, tq), F32)) for _ in range(chains))
    score_block(0, buf_a)
    state = lax.fori_loop(0, i // 2, two_blocks, init)
    state = lax.cond(lax.rem(i, 2) == 0, last_from_a, last_from_b, state)

    lam = (jnp.exp(jnp.sum(lq1_ref[...] * lk1_ref[...], axis=1, keepdims=True))
           - jnp.exp(jnp.sum(lq2_ref[...] * lk2_ref[...], axis=1, keepdims=True)) + LAMBDA_INIT)
    for h in range(DA_HEADS):
        l1, l2 = state[2 * h][1], state[2 * h + 1][1]
        o = (acc_ref[2 * h] / l1 - lam * (acc_ref[2 * h + 1] / l2)).T
        ms = jnp.mean(o * o, axis=1, keepdims=True)
        o = o * lax.rsqrt(ms + EPS) * g_ref[...]
        o_ref[:, h * LANES:(h + 1) * LANES] = (o * (1.0 - LAMBDA_INIT)).astype(BF16)


def _diff_attention(qk3, vt3, lam_q1, lam_k1, lam_q2, lam_k2, subln_g):
    b, s, _ = qk3.shape
    tq = min(ATT_TQ, s)
    vec = lambda n: pl.BlockSpec((1, n), lambda bb, i: (0, 0))
    return pl.pallas_call(
        functools.partial(_diff_kernel, tq=tq),
        grid=(b, s // tq),
        in_specs=[
            pl.BlockSpec((None, tq, SEG_W), lambda bb, i: (bb, i, 0)),
            pl.BlockSpec((None, s, SEG_W), lambda bb, i: (bb, 0, 1)),
            pl.BlockSpec((None, SEG_W, s), lambda bb, i: (bb, 0, 0)),
            vec(HEAD_DIM), vec(HEAD_DIM), vec(HEAD_DIM), vec(HEAD_DIM), vec(2 * HEAD_DIM),
        ],
        out_specs=pl.BlockSpec((None, tq, SEG_W), lambda bb, i: (bb, i, 0)),
        out_shape=jax.ShapeDtypeStruct((b, s, SEG_W), BF16),
        scratch_shapes=[pltpu.VMEM((2, 2 * DA_HEADS, tq, tq), F32), pltpu.VMEM((2 * DA_HEADS, LANES, tq), F32)],
        compiler_params=_cparams(2),
        name="diff_attn",
    )(qk3, qk3, vt3, lam_q1.reshape(1, -1), lam_k1.reshape(1, -1), lam_q2.reshape(1, -1),
      lam_k2.reshape(1, -1), subln_g.reshape(1, -1))


def _dil_kernel(q_ref, kp_ref, kc_ref, vp_ref, vc_ref, o_ref, lse_ref, *, rel):
    n = pl.program_id(2)
    qi = lax.broadcasted_iota(jnp.int32, (QBLK, 2 * QBLK), 0)
    kj = lax.broadcasted_iota(jnp.int32, (QBLK, 2 * QBLK), 1)
    dist = qi + QBLK - kj
    valid = (dist >= 0) & (dist <= rel) & ((kj >= QBLK) | (n > 0))
    lane = lax.broadcasted_iota(jnp.int32, (QBLK, LANES), 1)
    low = lane < HEAD_DIM
    slabs = [slice(p * LANES, (p + 1) * LANES) for p in range(SEG_W // LANES)]
    scores = []
    for sl in slabs:
        q2 = q_ref[:, sl]
        k2 = jnp.concatenate([kp_ref[:, sl], kc_ref[:, sl]], axis=0)
        for sel in (low, jnp.logical_not(low)):
            scores.append(_nt_dot(jnp.where(sel, q2, jnp.zeros_like(q2)), k2))
    probs = []
    for s in scores:
        s = jnp.where(valid, s, -jnp.inf)
        m = jnp.max(s, axis=1, keepdims=True)
        pe = jnp.exp(s - m)
        den = jnp.sum(pe, axis=1, keepdims=True)
        probs.append((pe.astype(BF16), den, m + jnp.log(den)))
    for p, sl in enumerate(slabs):
        v2 = jnp.concatenate([vp_ref[:, sl], vc_ref[:, sl]], axis=0)
        outs = [jnp.dot(pe, v2, preferred_element_type=F32) / den for pe, den, _ in probs[2 * p:2 * p + 2]]
        o_ref[:, sl] = jnp.where(low, outs[0], outs[1]).astype(BF16)
        lse_ref[:, sl] = jnp.where(low, probs[2 * p][2], probs[2 * p + 1][2])


def _dilated_group(qkv, batch, seq, group):
    window, dil = DIL_PAIRS[group]
    rel = window // dil
    length = seq // dil
    nb = length // QBLK
    view = qkv.reshape(batch, dil, length, QKV_W)

    def spec(seg, prev):
        if prev:
            return pl.BlockSpec((None, None, QBLK, SEG_W), lambda b, r, n: (b, r, jnp.maximum(n - 1, 0), seg))
        return pl.BlockSpec((None, None, QBLK, SEG_W), lambda b, r, n: (b, r, n, seg))

    out_spec = pl.BlockSpec((None, None, QBLK, SEG_W), lambda b, r, n: (b, r, n, 0))
    o, lse = pl.pallas_call(
        functools.partial(_dil_kernel, rel=rel),
        grid=(batch, dil, nb),
        in_specs=[spec(0, False), spec(1, True), spec(1, False), spec(2, True), spec(2, False)],
        out_specs=[out_spec, out_spec],
        out_shape=[jax.ShapeDtypeStruct((batch, dil, length, SEG_W), BF16),
                   jax.ShapeDtypeStruct((batch, dil, length, SEG_W), F32)],
        compiler_params=_cparams(3),
        name=f"dilated{group}",
    )(view, view, view, view, view)
    return o, lse


def _layer_norm(z, g, b):
    mu = jnp.mean(z, axis=1, keepdims=True)
    zc = z - mu
    var = jnp.mean(zc * zc, axis=1, keepdims=True)
    return zc * lax.rsqrt(var + EPS) * g + b


def _pack_bf16_pairs(v):
    n = v.shape[1] // 2
    lo = pltpu.bitcast(v[:, :n].astype(BF16).astype(F32), U32)
    hi = pltpu.bitcast(v[:, n:].astype(BF16).astype(F32), U32)
    return jnp.bitwise_or(jnp.right_shift(lo, jnp.uint32(16)), jnp.bitwise_and(hi, jnp.uint32(0xFFFF0000)))


def _unpack_bf16_pairs(w):
    lo = pltpu.bitcast(jnp.left_shift(w, jnp.uint32(16)), F32)
    hi = pltpu.bitcast(jnp.bitwise_and(w, jnp.uint32(0xFFFF0000)), F32)
    return lo, hi


def _merge_kernel(ya_ref, o1_ref, o2_ref, o3_ref, l1_ref, l2_ref, l3_ref, x_ref,
                  wg_ref, bg_ref, woa_ref, wob_ref, wo_ref, g_ref, b_ref, wrh_ref, wrl_ref, br_ref,
                  h_ref, hp_ref, topi_ref, topw_ref, rank_ref, cnt_ref, carry_ref, scr_ref):
    i = pl.program_id(0)
    tm = x_ref.shape[0]

    @pl.when(i == 0)
    def _():
        carry_ref[...] = jnp.zeros_like(carry_ref)

    x = x_ref[...]
    gates = jax.nn.sigmoid(jnp.dot(x.astype(BF16), wg_ref[...], preferred_element_type=F32) + bg_ref[...])

    def token_major(ref, scr):
        dil, n = ref.shape[0], ref.shape[1]
        if dil == 1:
            return ref[0].astype(F32)
        for r in range(dil):
            blk = ref[r].astype(F32)
            for c in range(SEG_W // LANES):
                scr[c, pl.ds(r, n, stride=dil), :] = blk[:, c * LANES:(c + 1) * LANES]
        return jnp.concatenate([scr[c] for c in range(SEG_W // LANES)], axis=1)

    oa, ob, oc = (token_major(r, scr_ref.at[k]) for k, r in enumerate((o1_ref, o2_ref, o3_ref)))
    la, lb, lc = (token_major(r, scr_ref.at[3 + k]) for k, r in enumerate((l1_ref, l2_ref, l3_ref)))
    mx = jnp.maximum(jnp.maximum(la, lb), lc)
    ea, eb, ec = jnp.exp(la - mx), jnp.exp(lb - mx), jnp.exp(lc - mx)
    yb = (ea * oa + eb * ob + ec * oc) / (ea + eb + ec)
    pa = jnp.dot(ya_ref[...], woa_ref[...], preferred_element_type=F32)
    pb = jnp.dot(yb.astype(BF16), wob_ref[...], preferred_element_type=F32)
    merged = gates[:, :D_MODEL] * pa + gates[:, D_MODEL:] * pb
    mix = jnp.dot(merged.astype(BF16), wo_ref[...], preferred_element_type=F32)
    h = _layer_norm(DN_ALPHA * x + mix, g_ref[...], b_ref[...])
    h_ref[...] = h
    hp_ref[...] = _pack_bf16_pairs(h)

    hh = h.astype(BF16)
    hl = (h - hh.astype(F32)).astype(BF16)
    logits = (jnp.dot(hh, wrh_ref[...], preferred_element_type=F32)
              + jnp.dot(hl, wrh_ref[...], preferred_element_type=F32)
              + jnp.dot(hh, wrl_ref[...], preferred_element_type=F32) + br_ref[...])
    lane = lax.broadcasted_iota(jnp.int32, (tm, LANES), 1)
    lg = jnp.where(lane < N_EXPERTS, logits, -jnp.inf)
    vals, idxs = [], []
    for _ in range(TOP_K):
        mv = jnp.max(lg, axis=1, keepdims=True)
        ik = jnp.min(jnp.where(lg == mv, lane, LANES), axis=1, keepdims=True)
        vals.append(mv)
        idxs.append(ik)
        lg = jnp.where(lane == ik, -jnp.inf, lg)
    es = [jnp.exp(v - vals[0]) for v in vals]
    tot = es[0] + es[1] + es[2] + es[3]

    onehot = jnp.zeros((tm, LANES), F32)
    for ik in idxs:
        onehot = onehot + (lane == ik).astype(F32)
    r_i = lax.broadcasted_iota(jnp.int32, (tm, tm), 0)
    c_i = lax.broadcasted_iota(jnp.int32, (tm, tm), 1)
    tri = (r_i > c_i).astype(BF16)
    pre = jnp.dot(tri, onehot.astype(BF16), preferred_element_type=F32) + carry_ref[...]

    topi = jnp.zeros((tm, LANES), jnp.int32)
    topw = jnp.zeros((tm, LANES), F32)
    rank = jnp.zeros((tm, LANES), F32)
    for k in range(TOP_K):
        rk = jnp.sum(jnp.where(lane == idxs[k], pre, 0.0), axis=1, keepdims=True)
        topi = jnp.where(lane == k, idxs[k], topi)
        topw = jnp.where(lane == k, es[k] / tot, topw)
        rank = jnp.where(lane == k, rk, rank)
    topi_ref[...] = topi
    topw_ref[...] = topw
    rank_ref[...] = rank.astype(jnp.int32)
    carry_ref[...] = carry_ref[...] + jnp.sum(onehot, axis=0, keepdims=True)
    cnt_ref[...] = carry_ref[...]


def _merge(ya, dil_outs, x2, seq, w_b, b_gate, w_oa_b, w_ob_b, w_o_b, ln1_g, ln1_b, w_router, b_router):
    tokens = x2.shape[0]
    tm = MERGE_TM
    per_seq = seq // tm

    def group(a):
        dil = a.shape[1]
        return pl.BlockSpec((None, dil, tm // dil, SEG_W), lambda i: (i // per_seq, 0, i % per_seq, 0))

    gate_blk = w_b.shape[1] // (2 * D_MODEL) - 1
    wr = jnp.zeros((D_MODEL, LANES), F32).at[:, :N_EXPERTS].set(w_router)
    wr_hi = wr.astype(BF16)
    wr_lo = (wr - wr_hi.astype(F32)).astype(BF16)
    br = jnp.zeros((1, LANES), F32).at[0, :N_EXPERTS].set(b_router)
    row = lambda w: pl.BlockSpec((tm, w), lambda i: (i, 0))
    full = lambda a: pl.BlockSpec(a.shape, lambda i: (0,) * a.ndim)
    (o1, l1), (o2, l2), (o3, l3) = dil_outs
    bg = b_gate.reshape(1, -1)
    g1 = ln1_g.reshape(1, -1)
    b1 = ln1_b.reshape(1, -1)
    lane_out = lambda dt: jax.ShapeDtypeStruct((tokens, LANES), dt)
    return pl.pallas_call(
        _merge_kernel,
        grid=(tokens // tm,),
        in_specs=[row(SEG_W), group(o1), group(o2), group(o3), group(l1), group(l2), group(l3),
                  row(D_MODEL), pl.BlockSpec((D_MODEL, 2 * D_MODEL), lambda i: (0, gate_blk)),
                  full(bg), full(w_oa_b), full(w_ob_b), full(w_o_b), full(g1), full(b1),
                  full(wr_hi), full(wr_lo), full(br)],
        out_specs=[row(D_MODEL), row(D_MODEL // 2), row(LANES), row(LANES), row(LANES),
                   pl.BlockSpec((1, LANES), lambda i: (0, 0))],
        out_shape=[jax.ShapeDtypeStruct((tokens, D_MODEL), F32), jax.ShapeDtypeStruct((tokens, D_MODEL // 2), U32),
                   lane_out(jnp.int32), lane_out(F32), lane_out(jnp.int32), jax.ShapeDtypeStruct((1, LANES), F32)],
        scratch_shapes=[pltpu.VMEM((1, LANES), F32), pltpu.VMEM((6, SEG_W // LANES, tm, LANES), F32)],
        compiler_params=_cparams(1),
        name="merge",
    )(ya, o1, o2, o3, l1, l2, l3, x2, w_b, bg, w_oa_b, w_ob_b, w_o_b, g1, b1, wr_hi, wr_lo, br)


def _sc_mesh():
    return plsc.VectorSubcoreMesh(core_axis_name="core", subcore_axis_name="subcore")


def _sc_scatter_rows(table, src_idx, dst_idx):
    n = src_idx.shape[0]
    d = table.shape[1]
    mesh = _sc_mesh()
    workers = mesh.num_cores * mesh.num_subcores
    per = n // (SC_CHUNK * workers)
    assert per * SC_CHUNK * workers == n

    @pl.kernel(out_type=jax.ShapeDtypeStruct((n, d), table.dtype), mesh=mesh,
               scratch_types=[pltpu.VMEM((1, SC_CHUNK), jnp.int32), pltpu.VMEM((1, SC_CHUNK), jnp.int32),
                              pltpu.VMEM((SC_CHUNK, d), table.dtype)])
    def copy(t_hbm, s_hbm, d_hbm, o_hbm, s_vm, d_vm, buf):
        wid = lax.axis_index("core") * mesh.num_subcores + lax.axis_index("subcore")

        @pl.loop(0, per)
        def _(j):
            blk = wid * per + j
            pltpu.sync_copy(s_hbm.at[pl.ds(blk, 1)], s_vm)
            pltpu.sync_copy(d_hbm.at[pl.ds(blk, 1)], d_vm)
            pltpu.sync_copy(t_hbm.at[s_vm.at[0]], buf)
            pltpu.sync_copy(buf, o_hbm.at[d_vm.at[0]])

    return copy(table, src_idx.reshape(-1, SC_CHUNK), dst_idx.reshape(-1, SC_CHUNK))


def _sc_gather_rows(table, idx):
    n = idx.shape[0]
    d = table.shape[1]
    mesh = _sc_mesh()
    workers = mesh.num_cores * mesh.num_subcores
    per = n // (SC_CHUNK * workers)
    assert per * SC_CHUNK * workers == n

    @pl.kernel(out_type=jax.ShapeDtypeStruct((n, d), table.dtype), mesh=mesh,
               scratch_types=[pltpu.VMEM((1, SC_CHUNK), jnp.int32), pltpu.VMEM((SC_CHUNK, d), table.dtype)])
    def gather(t_hbm, i_hbm, o_hbm, i_vm, buf):
        wid = lax.axis_index("core") * mesh.num_subcores + lax.axis_index("subcore")

        @pl.loop(0, per)
        def _(j):
            blk = wid * per + j
            pltpu.sync_copy(i_hbm.at[pl.ds(blk, 1)], i_vm)
            pltpu.sync_copy(t_hbm.at[i_vm.at[0]], buf)
            pltpu.sync_copy(buf, o_hbm.at[pl.ds(blk * SC_CHUNK, SC_CHUNK)])

    return gather(table, idx.reshape(-1, SC_CHUNK))


def _expert_kernel(be_ref, nx_ref, nu_ref, x_ref, wgu_hbm, bgu_ref, wd_hbm, bd_ref, y_ref,
                   wgu_stage, wd_stage, wgu_b, wd_b, sem):
    i = pl.program_id(0)
    used = i < nu_ref[0]
    expert = be_ref[i]

    def weight_copies(e):
        return (pltpu.make_async_copy(wgu_hbm.at[e], wgu_stage, sem.at[0]),
                pltpu.make_async_copy(wd_hbm.at[e], wd_stage, sem.at[1]))

    @pl.when(i == 0)
    def _():
        for cp in weight_copies(expert):
            cp.start()

    @pl.when(used & ((i == 0) | (expert != be_ref[jnp.maximum(i - 1, 0)])))
    def _():
        for cp in weight_copies(expert):
            cp.wait()
        wgu_b[...] = wgu_stage[...].astype(BF16)
        wd_b[...] = wd_stage[...].astype(BF16)

        @pl.when(nx_ref[i] >= 0)
        def _():
            for cp in weight_copies(nx_ref[i]):
                cp.start()

    @pl.when(used)
    def _():
        lo, hi = _unpack_bf16_pairs(x_ref[...])
        xb = jnp.concatenate([lo, hi], axis=1).astype(BF16)
        chunk = 512
        cols = [slice(c * chunk, (c + 1) * chunk) for c in range(D_FF // chunk)]
        pre = []
        for gs in cols:
            us = slice(D_FF + gs.start, D_FF + gs.stop)
            gate = jnp.dot(xb, wgu_b[:, gs], preferred_element_type=F32) + bgu_ref[:, gs]
            up = jnp.dot(xb, wgu_b[:, us], preferred_element_type=F32) + bgu_ref[:, us]
            pre.append((gate, up))
        acc = jnp.zeros((x_ref.shape[0], D_MODEL), F32)
        for gs, (gate, up) in zip(cols, pre):
            gate = jnp.minimum(gate, SWIGLU_LIMIT)
            up = jnp.clip(up, -SWIGLU_LIMIT, SWIGLU_LIMIT)
            act = (up + 1.0) * gate * jax.nn.sigmoid(SWIGLU_ALPHA * gate)
            acc = acc + jnp.dot(act.astype(BF16), wd_b[gs, :], preferred_element_type=F32)
        y_ref[...] = _pack_bf16_pairs(acc + bd_ref[...])

    @pl.when(i >= nu_ref[0])
    def _():
        y_ref[...] = jnp.zeros_like(y_ref)


def _experts(xrows, block_e, next_e, n_used, w_gu, b_gu, w_down, b_down):
    n_rows = xrows.shape[0]
    bm = MOE_BM
    bias = lambda i, be, nx, nu: (be[i], 0, 0)
    return pl.pallas_call(
        _expert_kernel,
        grid_spec=pltpu.PrefetchScalarGridSpec(
            num_scalar_prefetch=3,
            grid=(n_rows // bm,),
            in_specs=[
                pl.BlockSpec((bm, D_MODEL // 2), lambda i, be, nx, nu: (i, 0)),
                pl.BlockSpec(memory_space=pl.ANY),
                pl.BlockSpec((None, 1, 2 * D_FF), bias),
                pl.BlockSpec(memory_space=pl.ANY),
                pl.BlockSpec((None, 1, D_MODEL), bias),
            ],
            out_specs=pl.BlockSpec((bm, D_MODEL // 2), lambda i, be, nx, nu: (i, 0)),
            scratch_shapes=[pltpu.VMEM((D_MODEL, 2 * D_FF), F32), pltpu.VMEM((D_FF, D_MODEL), F32),
                            pltpu.VMEM((D_MODEL, 2 * D_FF), BF16), pltpu.VMEM((D_FF, D_MODEL), BF16),
                            pltpu.SemaphoreType.DMA((2,))],
        ),
        out_shape=jax.ShapeDtypeStruct((n_rows, D_MODEL // 2), U32),
        compiler_params=_cparams(1),
        name="experts",
    )(block_e, next_e, n_used, xrows, w_gu, b_gu.reshape(N_EXPERTS, 1, -1), w_down, b_down.reshape(N_EXPERTS, 1, -1))


def _combine_kernel(y0_ref, y1_ref, y2_ref, y3_ref, w_ref, h_ref, g_ref, b_ref, o_ref):
    w = w_ref[...]
    half = D_MODEL // 2
    f_lo = jnp.zeros((h_ref.shape[0], half), F32)
    f_hi = jnp.zeros((h_ref.shape[0], half), F32)
    for k, y_ref in enumerate((y0_ref, y1_ref, y2_ref, y3_ref)):
        lo, hi = _unpack_bf16_pairs(y_ref[...])
        f_lo = f_lo + lo * w[:, k:k + 1]
        f_hi = f_hi + hi * w[:, k:k + 1]
    f = jnp.concatenate([f_lo, f_hi], axis=1)
    o_ref[...] = _layer_norm(DN_ALPHA * h_ref[...] + f, g_ref[...], b_ref[...])


def _combine(yplanes, topw, h, ln2_g, ln2_b):
    tokens = h.shape[0]
    tm = MOE_TM
    per_plane = tokens // tm
    plane = lambda k: pl.BlockSpec((tm, D_MODEL // 2), lambda i: (k * per_plane + i, 0))
    return pl.pallas_call(
        _combine_kernel,
        grid=(per_plane,),
        in_specs=[plane(0), plane(1), plane(2), plane(3),
                  pl.BlockSpec((tm, LANES), lambda i: (i, 0)),
                  pl.BlockSpec((tm, D_MODEL), lambda i: (i, 0)),
                  pl.BlockSpec((1, D_MODEL), lambda i: (0, 0)),
                  pl.BlockSpec((1, D_MODEL), lambda i: (0, 0))],
        out_specs=pl.BlockSpec((tm, D_MODEL), lambda i: (i, 0)),
        out_shape=jax.ShapeDtypeStruct((tokens, D_MODEL), F32),
        compiler_params=_cparams(1),
        name="combine",
    )(yplanes, yplanes, yplanes, yplanes, topw, h, ln2_g.reshape(1, -1), ln2_b.reshape(1, -1))


def _routing_tables(topi, rank, cnt, tokens):
    bm = MOE_BM
    i32 = jnp.int32
    experts = jnp.arange(N_EXPERTS, dtype=i32)
    counts = cnt[0, :N_EXPERTS].astype(i32)
    padded = (counts + bm - 1) // bm * bm
    pend = jnp.cumsum(padded)
    pstart = pend - padded
    sel = topi[:, :TOP_K, None] == experts[None, None, :]
    dest = rank[:, :TOP_K] + jnp.sum(jnp.where(sel, pstart[None, None, :], 0), axis=-1)
    n_pad = N_EXPERTS * bm
    n_rows = tokens * TOP_K + n_pad
    starts = jnp.arange(n_rows // bm, dtype=i32) * bm
    block_e = jnp.minimum(jnp.sum((pend[None, :] <= starts[:, None]).astype(i32), axis=1), N_EXPERTS - 1)
    n_used = (pend[-1] // bm).reshape(1)
    is_block_e = block_e[:, None] == experts[None, :]
    after = jnp.sum(jnp.where(is_block_e, pend[None, :], 0), axis=1) // bm
    e_after = jnp.sum(jnp.where(after[:, None] == jnp.arange(n_rows // bm, dtype=i32)[None, :], block_e[None, :], 0),
                      axis=1)
    next_e = jnp.where(after < n_used[0], e_after, -1)
    pad_cnt = padded - counts
    pad_end = jnp.cumsum(pad_cnt)
    j = jnp.arange(n_pad, dtype=i32)
    owner = jnp.sum((pad_end[None, :] <= j[:, None]).astype(i32), axis=1)
    is_owner = owner[:, None] == experts[None, :]
    pick = lambda v: jnp.sum(jnp.where(is_owner, v[None, :], 0), axis=1)
    in_expert = pick(pstart + counts) + (j - pick(pad_end - pad_cnt))
    pad_rows = jnp.where(owner < N_EXPERTS, in_expert, pend[-1] + (j - pad_end[-1]))
    src_tok = jnp.concatenate([jnp.repeat(jnp.arange(tokens, dtype=i32), TOP_K), j % tokens])
    dst_row = jnp.concatenate([dest.reshape(-1), pad_rows]).astype(i32)
    return dest.astype(i32), block_e.astype(i32), next_e.astype(i32), n_used.astype(i32), src_tok, dst_row


def kernel(x, w_in, b_gate, lam_q1, lam_k1, lam_q2, lam_k2, subln_g, w_oa, w_ob, w_o, ln1_g, ln1_b,
           w_router, b_router, w_gu, b_gu, w_down, b_down, ln2_g, ln2_b):
    batch, seq, d = x.shape
    tokens = batch * seq
    h = x.reshape(tokens, d)
    tables = {dilation: _rope_lane_table(seq, dilation) for _, dilation in DIL_PAIRS}
    for l in range(DEPTH):
        w_b = w_in[l].astype(BF16)
        hb = h.astype(BF16)
        qk_a, vt_a = _project_qkv(hb, w_b, 0, tables[1], seq, "qkv_diff",
                                  q_scale=HEAD_DIM ** -0.5 * math.log2(math.e), v_feature_major=True)
        ya = _diff_attention(qk_a.reshape(batch, seq, 2 * SEG_W), vt_a, lam_q1[l], lam_k1[l], lam_q2[l],
                             lam_k2[l], subln_g[l]).reshape(tokens, -1)
        dil = []
        for g, (_, dilation) in enumerate(DIL_PAIRS):
            if dilation == 1:
                hb_g = hb
            else:
                hb_g = hb.reshape(batch, seq // dilation, dilation, d).swapaxes(1, 2).reshape(tokens, d)
            qkv_g = _project_qkv(hb_g, w_b, g + 1, tables[dilation], seq, f"qkv_dil{g}", q_scale=HEAD_DIM ** -0.5)
            dil.append(_dilated_group(qkv_g, batch, seq, g))
        h1, h1p, topi, topw, rank, cnt = _merge(ya, dil, h, seq, w_b, b_gate[l], w_oa[l].astype(BF16),
                                                w_ob[l].astype(BF16), w_o[l].astype(BF16), ln1_g[l], ln1_b[l],
                                                w_router[l], b_router[l])
        dest, block_e, next_e, n_used, src_tok, dst_row = _routing_tables(topi, rank, cnt, tokens)
        xrows = _sc_scatter_rows(h1p, src_tok, dst_row)
        yrows = _experts(xrows, block_e, next_e, n_used, w_gu[l], b_gu[l], w_down[l], b_down[l])
        yplanes = _sc_gather_rows(yrows, dest.T.reshape(-1))
        h = _combine(yplanes, topw, h1, ln2_g[l], ln2_b[l])
    return h.reshape(batch, seq, d)
```

```python
import functools
import math

import jax
import jax.numpy as jnp
from jax import lax
from jax.experimental import pallas as pl
from jax.experimental.pallas import tpu as pltpu
from jax.experimental.pallas import tpu_sc as plsc

F32 = jnp.float32
BF16 = jnp.bfloat16
U32 = jnp.uint32

D_MODEL = 1024
HEAD_DIM = 64
ROT_DIM = HEAD_DIM // 4
ROPE_THETA = 500000.0
QBLK = 128
DA_HEADS = 4
DIL_PAIRS = ((128, 1), (512, 4), (2048, 16))
SEG_W = 512
QKV_W = 3 * SEG_W
N_EXPERTS = 32
TOP_K = 4
D_FF = D_MODEL
SWIGLU_ALPHA = 1.702
SWIGLU_LIMIT = 7.0
DEPTH = 1
DN_ALPHA = (2 * DEPTH) ** 0.25
EPS = 1e-5
LAMBDA_INIT = 0.8 - 0.6 * math.exp(-0.3 * 0)

LANES = 128
QKV_TM = 512
QKV_CHUNK = 256
ATT_TQ = 256
DIL_NQ = 4
MERGE_TM = 256
MOE_BM = 256
MOE_TM = 256
SC_CHUNK = 128
VMEM_LIMIT = 52 * 1024 * 1024


def _cparams(n_axes):
    return pltpu.CompilerParams(dimension_semantics=("arbitrary",) * n_axes,
                                vmem_limit_bytes=VMEM_LIMIT)


def _qkv_kernel(x_ref, w_ref, cs_ref, o_ref, *maybe_vt_ref, q_scale):
    tm = x_ref.shape[0]
    half = ROT_DIM // 2
    in_head = lax.broadcasted_iota(jnp.int32, (QKV_CHUNK, LANES), 1) % HEAD_DIM
    first, second = in_head < half, (in_head >= half) & (in_head < ROT_DIM)
    for rc in range(tm // QKV_CHUNK):
        rows = slice(rc * QKV_CHUNK, (rc + 1) * QKV_CHUNK)
        xb = x_ref[rows, :]
        cs = cs_ref[rows, :]
        c = jnp.where(first, cs, jnp.where(second, pltpu.roll(cs, half, 1), 1.0))
        s1 = jnp.where(second, cs, 0.0)
        s2 = jnp.where(first, -pltpu.roll(cs, LANES - half, 1), 0.0)
        for seg in range(3):
            acc = jnp.dot(xb, w_ref[:, seg * SEG_W:(seg + 1) * SEG_W], preferred_element_type=F32)
            if seg == 2:
                if maybe_vt_ref:
                    maybe_vt_ref[0][:, rows] = acc.T.astype(BF16)
                else:
                    o_ref[rows, seg * SEG_W:(seg + 1) * SEG_W] = acc.astype(BF16)
                continue
            for k in range(SEG_W // LANES):
                t = acc[:, k * LANES:(k + 1) * LANES]
                r = t * c + pltpu.roll(t, ROT_DIM // 2, 1) * s1 + pltpu.roll(t, LANES - ROT_DIM // 2, 1) * s2
                if seg == 0:
                    r = r * q_scale
                lo = seg * SEG_W + k * LANES
                o_ref[rows, lo:lo + LANES] = r.astype(BF16)


def _rope_lane_table(seq, dil):
    row = jnp.arange(seq, dtype=jnp.int32)
    length = seq // dil
    pos = ((row % length) * dil + row // length).astype(F32)
    inv_freq = ROPE_THETA ** (-jnp.arange(0, ROT_DIM, 2, dtype=F32) / ROT_DIM)
    ang = pos[:, None] * inv_freq[None, :]
    head = jnp.concatenate([jnp.cos(ang), jnp.sin(ang), jnp.zeros((seq, HEAD_DIM - ROT_DIM), F32)], axis=1)
    return jnp.tile(head, (1, LANES // HEAD_DIM))


def _project_qkv(xb, w_b, col_block, table, seq, name, q_scale, v_feature_major=False):
    tokens = xb.shape[0]
    tm = min(QKV_TM, seq)
    per_seq = seq // tm
    tab_spec = pl.BlockSpec((tm, LANES), lambda i: (i % per_seq, 0))
    out_w = 2 * SEG_W if v_feature_major else QKV_W
    out_specs = [pl.BlockSpec((tm, out_w), lambda i: (i, 0))]
    out_shape = [jax.ShapeDtypeStruct((tokens, out_w), BF16)]
    if v_feature_major:
        out_specs.append(pl.BlockSpec((None, SEG_W, tm), lambda i: (i // per_seq, 0, i % per_seq)))
        out_shape.append(jax.ShapeDtypeStruct((tokens // seq, SEG_W, seq), BF16))
    outs = pl.pallas_call(
        functools.partial(_qkv_kernel, q_scale=q_scale),
        grid=(tokens // tm,),
        in_specs=[pl.BlockSpec((tm, D_MODEL), lambda i: (i, 0)),
                  pl.BlockSpec((D_MODEL, QKV_W), lambda i: (0, col_block)),
                  tab_spec],
        out_specs=out_specs,
        out_shape=out_shape,
        compiler_params=_cparams(1),
        name=name,
    )(xb, w_b, table)
    return outs if v_feature_major else outs[0]


def _nt_dot(a, b):
    return lax.dot_general(a, b, (((1,), (1,)), ((), ())), preferred_element_type=F32)


def _diff_kernel(q_ref, k_ref, vt_ref, lq1_ref, lk1_ref, lq2_ref, lk2_ref, g_ref, o_ref, s_ref, acc_ref, *, tq):
    i = pl.program_id(1)
    chains = 2 * DA_HEADS
    half = tq // 2
    lane = lax.broadcasted_iota(jnp.int32, (tq, LANES), 1)
    qs = []
    for h in range(DA_HEADS):
        q = q_ref[:, h * LANES:(h + 1) * LANES]
        zero = jnp.zeros_like(q)
        qs += [jnp.where(lane < HEAD_DIM, q, zero), jnp.where(lane >= HEAD_DIM, q, zero)]
    acc_ref[...] = jnp.zeros_like(acc_ref)

    def score_block(j, buf):
        start = pl.multiple_of(j * tq, tq)
        for c in range(chains):
            kb = k_ref[pl.ds(start, tq), (c // 2) * LANES:(c // 2 + 1) * LANES]
            buf[c] = _nt_dot(kb, qs[c])

    def absorb(j, buf, state, masked):
        start = pl.multiple_of(j * tq, tq)
        out = []
        for c in range(chains):
            m, l = state[c]
            if masked:
                key = lax.broadcasted_iota(jnp.int32, (tq, tq), 0)
                qry = lax.broadcasted_iota(jnp.int32, (tq, tq), 1)
                buf[c] = jnp.where(key <= qry, buf[c], -jnp.inf)
            m_new = jnp.maximum(m, jnp.max(buf[c], axis=0, keepdims=True))
            a = jnp.exp2(m - m_new)
            p = [jnp.exp2(buf[c, u * half:(u + 1) * half, :] - m_new) for u in range(2)]
            l = a * l + jnp.sum(p[0], axis=0, keepdims=True) + jnp.sum(p[1], axis=0, keepdims=True)
            vt = vt_ref[(c // 2) * LANES:(c // 2 + 1) * LANES, pl.ds(start, tq)]
            pv = jnp.dot(vt, jnp.concatenate(p, axis=0).astype(BF16), preferred_element_type=F32)
            acc_ref[c] = a * acc_ref[c] + pv
            out.append((m_new, l))
        return tuple(out)

    buf_a, buf_b = s_ref.at[0], s_ref.at[1]

    def two_blocks(jj, state):
        j = 2 * jj
        score_block(j + 1, buf_b)
        state = absorb(j, buf_a, state, False)
        score_block(j + 2, buf_a)
        return absorb(j + 1, buf_b, state, False)

    def last_from_a(state):
        return absorb(i, buf_a, state, True)

    def last_from_b(state):
        score_block(i, buf_b)
        state = absorb(i - 1, buf_a, state, False)
        return absorb(i, buf_b, state, True)

    init = tuple((jnp.full((1, tq), -jnp.inf, F32), jnp.zeros((1, tq), F32)) for _ in range(chains))
    score_block(0, buf_a)
    state = lax.fori_loop(0, i // 2, two_blocks, init)
    state = lax.cond(lax.rem(i, 2) == 0, last_from_a, last_from_b, state)

    lam = (jnp.exp(jnp.sum(lq1_ref[...] * lk1_ref[...], axis=1, keepdims=True))
           - jnp.exp(jnp.sum(lq2_ref[...] * lk2_ref[...], axis=1, keepdims=True)) + LAMBDA_INIT)
    for h in range(DA_HEADS):
        l1, l2 = state[2 * h][1], state[2 * h + 1][1]
        o = (acc_ref[2 * h] / l1 - lam * (acc_ref[2 * h + 1] / l2)).T
        ms = jnp.mean(o * o, axis=1, keepdims=True)
        o = o * lax.rsqrt(ms + EPS) * g_ref[...]
        o_ref[:, h * LANES:(h + 1) * LANES] = (o * (1.0 - LAMBDA_INIT)).astype(BF16)


def _diff_attention(qk3, vt3, lam_q1, lam_k1, lam_q2, lam_k2, subln_g):
    b, s, _ = qk3.shape
    tq = min(ATT_TQ, s)
    vec = lambda n: pl.BlockSpec((1, n), lambda bb, i: (0, 0))
    return pl.pallas_call(
        functools.partial(_diff_kernel, tq=tq),
        grid=(b, s // tq),
        in_specs=[
            pl.BlockSpec((None, tq, SEG_W), lambda bb, i: (bb, i, 0)),
            pl.BlockSpec((None, s, SEG_W), lambda bb, i: (bb, 0, 1)),
            pl.BlockSpec((None, SEG_W, s), lambda bb, i: (bb, 0, 0)),
            vec(HEAD_DIM), vec(HEAD_DIM), vec(HEAD_DIM), vec(HEAD_DIM), vec(2 * HEAD_DIM),
        ],
        out_specs=pl.BlockSpec((None, tq, SEG_W), lambda bb, i: (bb, i, 0)),
        out_shape=jax.ShapeDtypeStruct((b, s, SEG_W), BF16),
        scratch_shapes=[pltpu.VMEM((2, 2 * DA_HEADS, tq, tq), F32), pltpu.VMEM((2 * DA_HEADS, LANES, tq), F32)],
        compiler_params=_cparams(2),
        name="diff_attn",
    )(qk3, qk3, vt3, lam_q1.reshape(1, -1), lam_k1.reshape(1, -1), lam_q2.reshape(1, -1),
      lam_k2.reshape(1, -1), subln_g.reshape(1, -1))


def _dil_kernel(q_ref, kp_ref, kc_ref, vp_ref, vc_ref, o_ref, lse_ref, *, rel, nq):
    n = pl.program_id(2)
    qi = lax.broadcasted_iota(jnp.int32, (QBLK, 2 * QBLK), 0)
    kj = lax.broadcasted_iota(jnp.int32, (QBLK, 2 * QBLK), 1)
    dist = qi + QBLK - kj
    band = (dist >= 0) & (dist <= rel)
    lane = lax.broadcasted_iota(jnp.int32, (QBLK, LANES), 1)
    low = lane < HEAD_DIM
    slabs = [slice(p * LANES, (p + 1) * LANES) for p in range(SEG_W // LANES)]

    def window(prev_ref, cur_ref, s, sl):
        if s == 0:
            return jnp.concatenate([prev_ref[:, sl], cur_ref[:QBLK, sl]], axis=0)
        return cur_ref[(s - 1) * QBLK:(s + 1) * QBLK, sl]

    def score_block(s):
        rows = slice(s * QBLK, (s + 1) * QBLK)
        scores = []
        for sl in slabs:
            q2 = q_ref[rows, sl]
            k2 = window(kp_ref, kc_ref, s, sl)
            for sel in (low, jnp.logical_not(low)):
                scores.append(_nt_dot(jnp.where(sel, q2, jnp.zeros_like(q2)), k2))
        return scores

    def finish_block(s, scores):
        rows = slice(s * QBLK, (s + 1) * QBLK)
        valid = band & ((kj >= QBLK) | (n * nq + s > 0))
        probs = []
        for sc in scores:
            sc = jnp.where(valid, sc, -jnp.inf)
            m = jnp.max(sc, axis=1, keepdims=True)
            pe = jnp.exp(sc - m)
            den = jnp.sum(pe, axis=1, keepdims=True)
            probs.append((pe.astype(BF16), den, m + jnp.log(den)))
        for p, sl in enumerate(slabs):
            v2 = window(vp_ref, vc_ref, s, sl)
            outs = [jnp.dot(pe, v2, preferred_element_type=F32) / den for pe, den, _ in probs[2 * p:2 * p + 2]]
            o_ref[rows, sl] = jnp.where(low, outs[0], outs[1]).astype(BF16)
            lse_ref[rows, sl] = jnp.where(low, probs[2 * p][2], probs[2 * p + 1][2])

    pending = score_block(0)
    for s in range(1, nq):
        nxt = score_block(s)
        finish_block(s - 1, pending)
        pending = nxt
    finish_block(nq - 1, pending)


def _dilated_group(qkv, batch, seq, group):
    window, dil = DIL_PAIRS[group]
    rel = window // dil
    length = seq // dil
    nb = length // QBLK
    nq = math.gcd(nb, DIL_NQ)
    view = qkv.reshape(batch, dil, length, QKV_W)

    def spec(seg, prev):
        if prev:
            return pl.BlockSpec((None, None, QBLK, SEG_W), lambda b, r, n: (b, r, jnp.maximum(n * nq - 1, 0), seg))
        return pl.BlockSpec((None, None, nq * QBLK, SEG_W), lambda b, r, n: (b, r, n, seg))

    out_spec = pl.BlockSpec((None, None, nq * QBLK, SEG_W), lambda b, r, n: (b, r, n, 0))
    o, lse = pl.pallas_call(
        functools.partial(_dil_kernel, rel=rel, nq=nq),
        grid=(batch, dil, nb // nq),
        in_specs=[spec(0, False), spec(1, True), spec(1, False), spec(2, True), spec(2, False)],
        out_specs=[out_spec, out_spec],
        out_shape=[jax.ShapeDtypeStruct((batch, dil, length, SEG_W), BF16),
                   jax.ShapeDtypeStruct((batch, dil, length, SEG_W), F32)],
        compiler_params=_cparams(3),
        name=f"dilated{group}",
    )(view, view, view, view, view)
    return o, lse


def _layer_norm(z, g, b):
    mu = jnp.mean(z, axis=1, keepdims=True)
    zc = z - mu
    var = jnp.mean(zc * zc, axis=1, keepdims=True)
    return zc * lax.rsqrt(var + EPS) * g + b


def _pack_bf16_pairs(v):
    n = v.shape[1] // 2
    lo = pltpu.bitcast(v[:, :n].astype(BF16).astype(F32), U32)
    hi = pltpu.bitcast(v[:, n:].astype(BF16).astype(F32), U32)
    return jnp.bitwise_or(jnp.right_shift(lo, jnp.uint32(16)), jnp.bitwise_and(hi, jnp.uint32(0xFFFF0000)))


def _unpack_bf16_pairs(w):
    lo = pltpu.bitcast(jnp.left_shift(w, jnp.uint32(16)), F32)
    hi = pltpu.bitcast(jnp.bitwise_and(w, jnp.uint32(0xFFFF0000)), F32)
    return lo, hi


def _merge_kernel(ya_ref, o1_ref, o2_ref, o3_ref, l1_ref, l2_ref, l3_ref, x_ref,
                  wg_ref, bg_ref, woa_ref, wob_ref, wo_ref, g_ref, b_ref, wrh_ref, wrl_ref, br_ref,
                  h_ref, hp_ref, topi_ref, topw_ref, rank_ref, cnt_ref, carry_ref, scr_ref):
    i = pl.program_id(0)
    tm = x_ref.shape[0]

    @pl.when(i == 0)
    def _():
        carry_ref[...] = jnp.zeros_like(carry_ref)

    x = x_ref[...]
    gates = jax.nn.sigmoid(jnp.dot(x.astype(BF16), wg_ref[...], preferred_element_type=F32) + bg_ref[...])

    def token_major(ref, scr):
        dil, n = ref.shape[0], ref.shape[1]
        if dil == 1:
            return ref[0].astype(F32)
        for r in range(dil):
            blk = ref[r].astype(F32)
            for c in range(SEG_W // LANES):
                scr[c, pl.ds(r, n, stride=dil), :] = blk[:, c * LANES:(c + 1) * LANES]
        return jnp.concatenate([scr[c] for c in range(SEG_W // LANES)], axis=1)

    oa, ob, oc = (token_major(r, scr_ref.at[k]) for k, r in enumerate((o1_ref, o2_ref, o3_ref)))
    la, lb, lc = (token_major(r, scr_ref.at[3 + k]) for k, r in enumerate((l1_ref, l2_ref, l3_ref)))
    mx = jnp.maximum(jnp.maximum(la, lb), lc)
    ea, eb, ec = jnp.exp(la - mx), jnp.exp(lb - mx), jnp.exp(lc - mx)
    yb = (ea * oa + eb * ob + ec * oc) / (ea + eb + ec)
    pa = jnp.dot(ya_ref[...], woa_ref[...], preferred_element_type=F32)
    pb = jnp.dot(yb.astype(BF16), wob_ref[...], preferred_element_type=F32)
    merged = gates[:, :D_MODEL] * pa + gates[:, D_MODEL:] * pb
    mix = jnp.dot(merged.astype(BF16), wo_ref[...], preferred_element_type=F32)
    h = _layer_norm(DN_ALPHA * x + mix, g_ref[...], b_ref[...])
    h_ref[...] = h
    hp_ref[...] = _pack_bf16_pairs(h)

    hh = h.astype(BF16)
    hl = (h - hh.astype(F32)).astype(BF16)
    logits = (jnp.dot(hh, wrh_ref[...], preferred_element_type=F32)
              + jnp.dot(hl, wrh_ref[...], preferred_element_type=F32)
              + jnp.dot(hh, wrl_ref[...], preferred_element_type=F32) + br_ref[...])
    lane = lax.broadcasted_iota(jnp.int32, (tm, LANES), 1)
    lg = jnp.where(lane < N_EXPERTS, logits, -jnp.inf)
    vals, idxs = [], []
    for _ in range(TOP_K):
        mv = jnp.max(lg, axis=1, keepdims=True)
        ik = jnp.min(jnp.where(lg == mv, lane, LANES), axis=1, keepdims=True)
        vals.append(mv)
        idxs.append(ik)
        lg = jnp.where(lane == ik, -jnp.inf, lg)
    es = [jnp.exp(v - vals[0]) for v in vals]
    tot = es[0] + es[1] + es[2] + es[3]

    onehot = jnp.zeros((tm, LANES), F32)
    for ik in idxs:
        onehot = onehot + (lane == ik).astype(F32)
    r_i = lax.broadcasted_iota(jnp.int32, (tm, tm), 0)
    c_i = lax.broadcasted_iota(jnp.int32, (tm, tm), 1)
    tri = (r_i > c_i).astype(BF16)
    pre = jnp.dot(tri, onehot.astype(BF16), preferred_element_type=F32) + carry_ref[...]

    topi = jnp.zeros((tm, LANES), jnp.int32)
    topw = jnp.zeros((tm, LANES), F32)
    rank = jnp.zeros((tm, LANES), F32)
    for k in range(TOP_K):
        rk = jnp.sum(jnp.where(lane == idxs[k], pre, 0.0), axis=1, keepdims=True)
        topi = jnp.where(lane == k, idxs[k], topi)
        topw = jnp.where(lane == k, es[k] / tot, topw)
        rank = jnp.where(lane == k, rk, rank)
    topi_ref[...] = topi
    topw_ref[...] = topw
    rank_ref[...] = rank.astype(jnp.int32)
    carry_ref[...] = carry_ref[...] + jnp.sum(onehot, axis=0, keepdims=True)
    cnt_ref[...] = carry_ref[...]


def _merge(ya, dil_outs, x2, seq, w_b, b_gate, w_oa_b, w_ob_b, w_o_b, ln1_g, ln1_b, w_router, b_router):
    tokens = x2.shape[0]
    tm = MERGE_TM
    per_seq = seq // tm

    def group(a):
        dil = a.shape[1]
        return pl.BlockSpec((None, dil, tm // dil, SEG_W), lambda i: (i // per_seq, 0, i % per_seq, 0))

    gate_blk = w_b.shape[1] // (2 * D_MODEL) - 1
    wr = jnp.zeros((D_MODEL, LANES), F32).at[:, :N_EXPERTS].set(w_router)
    wr_hi = wr.astype(BF16)
    wr_lo = (wr - wr_hi.astype(F32)).astype(BF16)
    br = jnp.zeros((1, LANES), F32).at[0, :N_EXPERTS].set(b_router)
    row = lambda w: pl.BlockSpec((tm, w), lambda i: (i, 0))
    full = lambda a: pl.BlockSpec(a.shape, lambda i: (0,) * a.ndim)
    (o1, l1), (o2, l2), (o3, l3) = dil_outs
    bg = b_gate.reshape(1, -1)
    g1 = ln1_g.reshape(1, -1)
    b1 = ln1_b.reshape(1, -1)
    lane_out = lambda dt: jax.ShapeDtypeStruct((tokens, LANES), dt)
    return pl.pallas_call(
        _merge_kernel,
        grid=(tokens // tm,),
        in_specs=[row(SEG_W), group(o1), group(o2), group(o3), group(l1), group(l2), group(l3),
                  row(D_MODEL), pl.BlockSpec((D_MODEL, 2 * D_MODEL), lambda i: (0, gate_blk)),
                  full(bg), full(w_oa_b), full(w_ob_b), full(w_o_b), full(g1), full(b1),
                  full(wr_hi), full(wr_lo), full(br)],
        out_specs=[row(D_MODEL), row(D_MODEL // 2), row(LANES), row(LANES), row(LANES),
                   pl.BlockSpec((1, LANES), lambda i: (0, 0))],
        out_shape=[jax.ShapeDtypeStruct((tokens, D_MODEL), F32), jax.ShapeDtypeStruct((tokens, D_MODEL // 2), U32),
                   lane_out(jnp.int32), lane_out(F32), lane_out(jnp.int32), jax.ShapeDtypeStruct((1, LANES), F32)],
        scratch_shapes=[pltpu.VMEM((1, LANES), F32), pltpu.VMEM((6, SEG_W // LANES, tm, LANES), F32)],
        compiler_params=_cparams(1),
        name="merge",
    )(ya, o1, o2, o3, l1, l2, l3, x2, w_b, bg, w_oa_b, w_ob_b, w_o_b, g1, b1, wr_hi, wr_lo, br)


def _sc_mesh():
    return plsc.VectorSubcoreMesh(core_axis_name="core", subcore_axis_name="subcore")


def _sc_scatter_rows(table, src_idx, dst_idx):
    n = src_idx.shape[0]
    d = table.shape[1]
    mesh = _sc_mesh()
    workers = mesh.num_cores * mesh.num_subcores
    per = n // (SC_CHUNK * workers)
    assert per * SC_CHUNK * workers == n

    @pl.kernel(out_type=jax.ShapeDtypeStruct((n, d), table.dtype), mesh=mesh,
               scratch_types=[pltpu.VMEM((1, SC_CHUNK), jnp.int32), pltpu.VMEM((1, SC_CHUNK), jnp.int32),
                              pltpu.VMEM((SC_CHUNK, d), table.dtype)])
    def copy(t_hbm, s_hbm, d_hbm, o_hbm, s_vm, d_vm, buf):
        wid = lax.axis_index("core") * mesh.num_subcores + lax.axis_index("subcore")

        @pl.loop(0, per)
        def _(j):
            blk = wid * per + j
            pltpu.sync_copy(s_hbm.at[pl.ds(blk, 1)], s_vm)
            pltpu.sync_copy(d_hbm.at[pl.ds(blk, 1)], d_vm)
            pltpu.sync_copy(t_hbm.at[s_vm.at[0]], buf)
            pltpu.sync_copy(buf, o_hbm.at[d_vm.at[0]])

    return copy(table, src_idx.reshape(-1, SC_CHUNK), dst_idx.reshape(-1, SC_CHUNK))


def _sc_gather_rows(table, idx):
    n = idx.shape[0]
    d = table.shape[1]
    mesh = _sc_mesh()
    workers = mesh.num_cores * mesh.num_subcores
    per = n // (SC_CHUNK * workers)
    assert per * SC_CHUNK * workers == n

    @pl.kernel(out_type=jax.ShapeDtypeStruct((n, d), table.dtype), mesh=mesh,
               scratch_types=[pltpu.VMEM((1, SC_CHUNK), jnp.int32), pltpu.VMEM((SC_CHUNK, d), table.dtype)])
    def gather(t_hbm, i_hbm, o_hbm, i_vm, buf):
        wid = lax.axis_index("core") * mesh.num_subcores + lax.axis_index("subcore")

        @pl.loop(0, per)
        def _(j):
            blk = wid * per + j
            pltpu.sync_copy(i_hbm.at[pl.ds(blk, 1)], i_vm)
            pltpu.sync_copy(t_hbm.at[i_vm.at[0]], buf)
            pltpu.sync_copy(buf, o_hbm.at[pl.ds(blk * SC_CHUNK, SC_CHUNK)])

    return gather(table, idx.reshape(-1, SC_CHUNK))


def _expert_kernel(be_ref, nx_ref, nu_ref, x_ref, wgu_hbm, bgu_ref, wd_hbm, bd_ref, y_ref,
                   wgu_stage, wd_stage, wgu_b, wd_b, sem):
    i = pl.program_id(0)
    used = i < nu_ref[0]
    expert = be_ref[i]

    def weight_copies(e):
        return (pltpu.make_async_copy(wgu_hbm.at[e], wgu_stage, sem.at[0]),
                pltpu.make_async_copy(wd_hbm.at[e], wd_stage, sem.at[1]))

    @pl.when(i == 0)
    def _():
        for cp in weight_copies(expert):
            cp.start()

    @pl.when(used & ((i == 0) | (expert != be_ref[jnp.maximum(i - 1, 0)])))
    def _():
        for cp in weight_copies(expert):
            cp.wait()
        wgu_b[...] = wgu_stage[...].astype(BF16)
        wd_b[...] = wd_stage[...].astype(BF16)

        @pl.when(nx_ref[i] >= 0)
        def _():
            for cp in weight_copies(nx_ref[i]):
                cp.start()

    @pl.when(used)
    def _():
        lo, hi = _unpack_bf16_pairs(x_ref[...])
        xb = jnp.concatenate([lo, hi], axis=1).astype(BF16)
        chunk = 512
        cols = [slice(c * chunk, (c + 1) * chunk) for c in range(D_FF // chunk)]
        pre = []
        for gs in cols:
            us = slice(D_FF + gs.start, D_FF + gs.stop)
            gate = jnp.dot(xb, wgu_b[:, gs], preferred_element_type=F32) + bgu_ref[:, gs]
            up = jnp.dot(xb, wgu_b[:, us], preferred_element_type=F32) + bgu_ref[:, us]
            pre.append((gate, up))
        acc = jnp.zeros((x_ref.shape[0], D_MODEL), F32)
        for gs, (gate, up) in zip(cols, pre):
            gate = jnp.minimum(gate, SWIGLU_LIMIT)
            up = jnp.clip(up, -SWIGLU_LIMIT, SWIGLU_LIMIT)
            act = (up + 1.0) * gate * jax.nn.sigmoid(SWIGLU_ALPHA * gate)
            acc = acc + jnp.dot(act.astype(BF16), wd_b[gs, :], preferred_element_type=F32)
        y_ref[...] = _pack_bf16_pairs(acc + bd_ref[...])

    @pl.when(i >= nu_ref[0])
    def _():
        y_ref[...] = jnp.zeros_like(y_ref)


def _experts(xrows, block_e, next_e, n_used, w_gu, b_gu, w_down, b_down):
    n_rows = xrows.shape[0]
    bm = MOE_BM
    bias = lambda i, be, nx, nu: (be[i], 0, 0)
    return pl.pallas_call(
        _expert_kernel,
        grid_spec=pltpu.PrefetchScalarGridSpec(
            num_scalar_prefetch=3,
            grid=(n_rows // bm,),
            in_specs=[
                pl.BlockSpec((bm, D_MODEL // 2), lambda i, be, nx, nu: (i, 0)),
                pl.BlockSpec(memory_space=pl.ANY),
                pl.BlockSpec((None, 1, 2 * D_FF), bias),
                pl.BlockSpec(memory_space=pl.ANY),
                pl.BlockSpec((None, 1, D_MODEL), bias),
            ],
            out_specs=pl.BlockSpec((bm, D_MODEL // 2), lambda i, be, nx, nu: (i, 0)),
            scratch_shapes=[pltpu.VMEM((D_MODEL, 2 * D_FF), F32), pltpu.VMEM((D_FF, D_MODEL), F32),
                            pltpu.VMEM((D_MODEL, 2 * D_FF), BF16), pltpu.VMEM((D_FF, D_MODEL), BF16),
                            pltpu.SemaphoreType.DMA((2,))],
        ),
        out_shape=jax.ShapeDtypeStruct((n_rows, D_MODEL // 2), U32),
        compiler_params=_cparams(1),
        name="experts",
    )(block_e, next_e, n_used, xrows, w_gu, b_gu.reshape(N_EXPERTS, 1, -1), w_down, b_down.reshape(N_EXPERTS, 1, -1))


def _combine_kernel(y0_ref, y1_ref, y2_ref, y3_ref, w_ref, h_ref, g_ref, b_ref, o_ref):
    w = w_ref[...]
    half = D_MODEL // 2
    f_lo = jnp.zeros((h_ref.shape[0], half), F32)
    f_hi = jnp.zeros((h_ref.shape[0], half), F32)
    for k, y_ref in enumerate((y0_ref, y1_ref, y2_ref, y3_ref)):
        lo, hi = _unpack_bf16_pairs(y_ref[...])
        f_lo = f_lo + lo * w[:, k:k + 1]
        f_hi = f_hi + hi * w[:, k:k + 1]
    f = jnp.concatenate([f_lo, f_hi], axis=1)
    o_ref[...] = _layer_norm(DN_ALPHA * h_ref[...] + f, g_ref[...], b_ref[...])


def _combine(yplanes, topw, h, ln2_g, ln2_b):
    tokens = h.shape[0]
    tm = MOE_TM
    per_plane = tokens // tm
    plane = lambda k: pl.BlockSpec((tm, D_MODEL // 2), lambda i: (k * per_plane + i, 0))
    return pl.pallas_call(
        _combine_kernel,
        grid=(per_plane,),
        in_specs=[plane(0), plane(1), plane(2), plane(3),
                  pl.BlockSpec((tm, LANES), lambda i: (i, 0)),
                  pl.BlockSpec((tm, D_MODEL), lambda i: (i, 0)),
                  pl.BlockSpec((1, D_MODEL), lambda i: (0, 0)),
                  pl.BlockSpec((1, D_MODEL), lambda i: (0, 0))],
        out_specs=pl.BlockSpec((tm, D_MODEL), lambda i: (i, 0)),
        out_shape=jax.ShapeDtypeStruct((tokens, D_MODEL), F32),
        compiler_params=_cparams(1),
        name="combine",
    )(yplanes, yplanes, yplanes, yplanes, topw, h, ln2_g.reshape(1, -1), ln2_b.reshape(1, -1))


def _routing_tables(topi, rank, cnt, tokens):
    bm = MOE_BM
    i32 = jnp.int32
    experts = jnp.arange(N_EXPERTS, dtype=i32)
    counts = cnt[0, :N_EXPERTS].astype(i32)
    padded = (counts + bm - 1) // bm * bm
    pend = jnp.cumsum(padded)
    pstart = pend - padded
    sel = topi[:, :TOP_K, None] == experts[None, None, :]
    dest = rank[:, :TOP_K] + jnp.sum(jnp.where(sel, pstart[None, None, :], 0), axis=-1)
    n_pad = N_EXPERTS * bm
    n_rows = tokens * TOP_K + n_pad
    starts = jnp.arange(n_rows // bm, dtype=i32) * bm
    block_e = jnp.minimum(jnp.sum((pend[None, :] <= starts[:, None]).astype(i32), axis=1), N_EXPERTS - 1)
    n_used = (pend[-1] // bm).reshape(1)
    is_block_e = block_e[:, None] == experts[None, :]
    after = jnp.sum(jnp.where(is_block_e, pend[None, :], 0), axis=1) // bm
    e_after = jnp.sum(jnp.where(after[:, None] == jnp.arange(n_rows // bm, dtype=i32)[None, :], block_e[None, :], 0),
                      axis=1)
    next_e = jnp.where(after < n_used[0], e_after, -1)
    pad_cnt = padded - counts
    pad_end = jnp.cumsum(pad_cnt)
    j = jnp.arange(n_pad, dtype=i32)
    owner = jnp.sum((pad_end[None, :] <= j[:, None]).astype(i32), axis=1)
    is_owner = owner[:, None] == experts[None, :]
    pick = lambda v: jnp.sum(jnp.where(is_owner, v[None, :], 0), axis=1)
    in_expert = pick(pstart + counts) + (j - pick(pad_end - pad_cnt))
    pad_rows = jnp.where(owner < N_EXPERTS, in_expert, pend[-1] + (j - pad_end[-1]))
    src_tok = jnp.concatenate([jnp.repeat(jnp.arange(tokens, dtype=i32), TOP_K), j % tokens])
    dst_row = jnp.concatenate([dest.reshape(-1), pad_rows]).astype(i32)
    return dest.astype(i32), block_e.astype(i32), next_e.astype(i32), n_used.astype(i32), src_tok, dst_row


def kernel(x, w_in, b_gate, lam_q1, lam_k1, lam_q2, lam_k2, subln_g, w_oa, w_ob, w_o, ln1_g, ln1_b,
           w_router, b_router, w_gu, b_gu, w_down, b_down, ln2_g, ln2_b):
    batch, seq, d = x.shape
    tokens = batch * seq
    h = x.reshape(tokens, d)
    tables = {dilation: _rope_lane_table(seq, dilation) for _, dilation in DIL_PAIRS}
    for l in range(DEPTH):
        w_b = w_in[l].astype(BF16)
        hb = h.astype(BF16)
        qk_a, vt_a = _project_qkv(hb, w_b, 0, tables[1], seq, "qkv_diff",
                                  q_scale=HEAD_DIM ** -0.5 * math.log2(math.e), v_feature_major=True)
        ya = _diff_attention(qk_a.reshape(batch, seq, 2 * SEG_W), vt_a, lam_q1[l], lam_k1[l], lam_q2[l],
                             lam_k2[l], subln_g[l]).reshape(tokens, -1)
        dil = []
        for g, (_, dilation) in enumerate(DIL_PAIRS):
            if dilation == 1:
                hb_g = hb
            else:
                hb_g = hb.reshape(batch, seq // dilation, dilation, d).swapaxes(1, 2).reshape(tokens, d)
            qkv_g = _project_qkv(hb_g, w_b, g + 1, tables[dilation], seq, f"qkv_dil{g}", q_scale=HEAD_DIM ** -0.5)
            dil.append(_dilated_group(qkv_g, batch, seq, g))
        h1, h1p, topi, topw, rank, cnt = _merge(ya, dil, h, seq, w_b, b_gate[l], w_oa[l].astype(BF16),
                                                w_ob[l].astype(BF16), w_o[l].astype(BF16), ln1_g[l], ln1_b[l],
                                                w_router[l], b_router[l])
        dest, block_e, next_e, n_used, src_tok, dst_row = _routing_tables(topi, rank, cnt, tokens)
        xrows = _sc_scatter_rows(h1p, src_tok, dst_row)
        yrows = _experts(xrows, block_e, next_e, n_used, w_gu[l], b_gu[l], w_down[l], b_down[l])
        yplanes = _sc_gather_rows(yrows, dest.T.reshape(-1))
        h = _combine(yplanes, topw, h1, ln2_g[l], ln2_b[l])
    return h.reshape(batch, seq, d)
```

```python
import functools
import math

import jax
import jax.numpy as jnp
from jax import lax
from jax.experimental import pallas as pl
from jax.experimental.pallas import tpu as pltpu
from jax.experimental.pallas import tpu_sc as plsc

F32 = jnp.float32
BF16 = jnp.bfloat16
U32 = jnp.uint32

D_MODEL = 1024
HEAD_DIM = 64
ROT_DIM = HEAD_DIM // 4
ROPE_THETA = 500000.0
QBLK = 128
DA_HEADS = 4
DIL_PAIRS = ((128, 1), (512, 4), (2048, 16))
SEG_W = 512
QKV_W = 3 * SEG_W
N_EXPERTS = 32
TOP_K = 4
D_FF = D_MODEL
SWIGLU_ALPHA = 1.702
SWIGLU_LIMIT = 7.0
DEPTH = 1
DN_ALPHA = (2 * DEPTH) ** 0.25
EPS = 1e-5
LAMBDA_INIT = 0.8 - 0.6 * math.exp(-0.3 * 0)

LANES = 128
QKV_TM = 512
QKV_CHUNK = 256
ATT_TQ = 256
DIL_NQ = 4
MERGE_TM = 512
MOE_BM = 256
MOE_TM = 256
SC_CHUNK = 128
VMEM_LIMIT = 52 * 1024 * 1024


def _cparams(n_axes):
    return pltpu.CompilerParams(dimension_semantics=("arbitrary",) * n_axes,
                                vmem_limit_bytes=VMEM_LIMIT)


def _qkv_kernel(x_ref, w_ref, cs_ref, o_ref, *maybe_vt_ref, q_scale):
    tm = x_ref.shape[0]
    half = ROT_DIM // 2
    in_head = lax.broadcasted_iota(jnp.int32, (QKV_CHUNK, LANES), 1) % HEAD_DIM
    first, second = in_head < half, (in_head >= half) & (in_head < ROT_DIM)
    for rc in range(tm // QKV_CHUNK):
        rows = slice(rc * QKV_CHUNK, (rc + 1) * QKV_CHUNK)
        xb = x_ref[rows, :]
        cs = cs_ref[rows, :]
        c = jnp.where(first, cs, jnp.where(second, pltpu.roll(cs, half, 1), 1.0))
        s1 = jnp.where(second, cs, 0.0)
        s2 = jnp.where(first, -pltpu.roll(cs, LANES - half, 1), 0.0)
        for seg in range(3):
            acc = jnp.dot(xb, w_ref[:, seg * SEG_W:(seg + 1) * SEG_W], preferred_element_type=F32)
            if seg == 2:
                if maybe_vt_ref:
                    maybe_vt_ref[0][:, rows] = acc.T.astype(BF16)
                else:
                    o_ref[rows, seg * SEG_W:(seg + 1) * SEG_W] = acc.astype(BF16)
                continue
            for k in range(SEG_W // LANES):
                t = acc[:, k * LANES:(k + 1) * LANES]
                r = t * c + pltpu.roll(t, ROT_DIM // 2, 1) * s1 + pltpu.roll(t, LANES - ROT_DIM // 2, 1) * s2
                if seg == 0:
                    r = r * q_scale
                lo = seg * SEG_W + k * LANES
                o_ref[rows, lo:lo + LANES] = r.astype(BF16)


def _rope_lane_table(seq, dil):
    row = jnp.arange(seq, dtype=jnp.int32)
    length = seq // dil
    pos = ((row % length) * dil + row // length).astype(F32)
    inv_freq = ROPE_THETA ** (-jnp.arange(0, ROT_DIM, 2, dtype=F32) / ROT_DIM)
    ang = pos[:, None] * inv_freq[None, :]
    head = jnp.concatenate([jnp.cos(ang), jnp.sin(ang), jnp.zeros((seq, HEAD_DIM - ROT_DIM), F32)], axis=1)
    return jnp.tile(head, (1, LANES // HEAD_DIM))


def _project_qkv(xb, w_b, col_block, table, seq, name, q_scale, v_feature_major=False):
    tokens = xb.shape[0]
    tm = min(QKV_TM, seq)
    per_seq = seq // tm
    tab_spec = pl.BlockSpec((tm, LANES), lambda i: (i % per_seq, 0))
    out_w = 2 * SEG_W if v_feature_major else QKV_W
    out_specs = [pl.BlockSpec((tm, out_w), lambda i: (i, 0))]
    out_shape = [jax.ShapeDtypeStruct((tokens, out_w), BF16)]
    if v_feature_major:
        out_specs.append(pl.BlockSpec((None, SEG_W, tm), lambda i: (i // per_seq, 0, i % per_seq)))
        out_shape.append(jax.ShapeDtypeStruct((tokens // seq, SEG_W, seq), BF16))
    outs = pl.pallas_call(
        functools.partial(_qkv_kernel, q_scale=q_scale),
        grid=(tokens // tm,),
        in_specs=[pl.BlockSpec((tm, D_MODEL), lambda i: (i, 0)),
                  pl.BlockSpec((D_MODEL, QKV_W), lambda i: (0, col_block)),
                  tab_spec],
        out_specs=out_specs,
        out_shape=out_shape,
        compiler_params=_cparams(1),
        name=name,
    )(xb, w_b, table)
    return outs if v_feature_major else outs[0]


def _nt_dot(a, b):
    return lax.dot_general(a, b, (((1,), (1,)), ((), ())), preferred_element_type=F32)


def _diff_kernel(q_ref, k_ref, vt_ref, lq1_ref, lk1_ref, lq2_ref, lk2_ref, g_ref, o_ref, s_ref, acc_ref, *, tq):
    i = pl.program_id(1)
    chains = 2 * DA_HEADS
    half = tq // 2
    lane = lax.broadcasted_iota(jnp.int32, (tq, LANES), 1)
    qs = []
    for h in range(DA_HEADS):
        q = q_ref[:, h * LANES:(h + 1) * LANES]
        zero = jnp.zeros_like(q)
        qs += [jnp.where(lane < HEAD_DIM, q, zero), jnp.where(lane >= HEAD_DIM, q, zero)]
    acc_ref[...] = jnp.zeros_like(acc_ref)

    def score_block(j, buf):
        start = pl.multiple_of(j * tq, tq)
        for c in range(chains):
            kb = k_ref[pl.ds(start, tq), (c // 2) * LANES:(c // 2 + 1) * LANES]
            buf[c] = _nt_dot(kb, qs[c])

    def absorb(j, buf, state, masked):
        start = pl.multiple_of(j * tq, tq)
        out = []
        for c in range(chains):
            m, l = state[c]
            if masked:
                key = lax.broadcasted_iota(jnp.int32, (tq, tq), 0)
                qry = lax.broadcasted_iota(jnp.int32, (tq, tq), 1)
                buf[c] = jnp.where(key <= qry, buf[c], -jnp.inf)
            m_new = jnp.maximum(m, jnp.max(buf[c], axis=0, keepdims=True))
            a = jnp.exp2(m - m_new)
            p = [jnp.exp2(buf[c, u * half:(u + 1) * half, :] - m_new) for u in range(2)]
            l = a * l + jnp.sum(p[0], axis=0, keepdims=True) + jnp.sum(p[1], axis=0, keepdims=True)
            vt = vt_ref[(c // 2) * LANES:(c // 2 + 1) * LANES, pl.ds(start, tq)]
            pv = jnp.dot(vt, jnp.concatenate(p, axis=0).astype(BF16), preferred_element_type=F32)
            acc_ref[c] = a * acc_ref[c] + pv
            out.append((m_new, l))
        return tuple(out)

    buf_a, buf_b = s_ref.at[0], s_ref.at[1]

    def two_blocks(jj, state):
        j = 2 * jj
        score_block(j + 1, buf_b)
        state = absorb(j, buf_a, state, False)
        score_block(j + 2, buf_a)
        return absorb(j + 1, buf_b, state, False)

    def last_from_a(state):
        return absorb(i, buf_a, state, True)

    def last_from_b(state):
        score_block(i, buf_b)
        state = absorb(i - 1, buf_a, state, False)
        return absorb(i, buf_b, state, True)

    init = tuple((jnp.full((1, tq), -jnp.inf, F32), jnp.zeros((1, tq), F32)) for _ in range(chains))
    score_block(0, buf_a)
    state = lax.fori_loop(0, i // 2, two_blocks, init)
    state = lax.cond(lax.rem(i, 2) == 0, last_from_a, last_from_b, state)

    lam = (jnp.exp(jnp.sum(lq1_ref[...] * lk1_ref[...], axis=1, keepdims=True))
           - jnp.exp(jnp.sum(lq2_ref[...] * lk2_ref[...], axis=1, keepdims=True)) + LAMBDA_INIT)
    for h in range(DA_HEADS):
        l1, l2 = state[2 * h][1], state[2 * h + 1][1]
        o = (acc_ref[2 * h] / l1 - lam * (acc_ref[2 * h + 1] / l2)).T
        ms = jnp.mean(o * o, axis=1, keepdims=True)
        o = o * lax.rsqrt(ms + EPS) * g_ref[...]
        o_ref[:, h * LANES:(h + 1) * LANES] = (o * (1.0 - LAMBDA_INIT)).astype(BF16)


def _diff_attention(qk3, vt3, lam_q1, lam_k1, lam_q2, lam_k2, subln_g):
    b, s, _ = qk3.shape
    tq = min(ATT_TQ, s)
    vec = lambda n: pl.BlockSpec((1, n), lambda bb, i: (0, 0))
    return pl.pallas_call(
        functools.partial(_diff_kernel, tq=tq),
        grid=(b, s // tq),
        in_specs=[
            pl.BlockSpec((None, tq, SEG_W), lambda bb, i: (bb, i, 0)),
            pl.BlockSpec((None, s, SEG_W), lambda bb, i: (bb, 0, 1)),
            pl.BlockSpec((None, SEG_W, s), lambda bb, i: (bb, 0, 0)),
            vec(HEAD_DIM), vec(HEAD_DIM), vec(HEAD_DIM), vec(HEAD_DIM), vec(2 * HEAD_DIM),
        ],
        out_specs=pl.BlockSpec((None, tq, SEG_W), lambda bb, i: (bb, i, 0)),
        out_shape=jax.ShapeDtypeStruct((b, s, SEG_W), BF16),
        scratch_shapes=[pltpu.VMEM((2, 2 * DA_HEADS, tq, tq), F32), pltpu.VMEM((2 * DA_HEADS, LANES, tq), F32)],
        compiler_params=_cparams(2),
        name="diff_attn",
    )(qk3, qk3, vt3, lam_q1.reshape(1, -1), lam_k1.reshape(1, -1), lam_q2.reshape(1, -1),
      lam_k2.reshape(1, -1), subln_g.reshape(1, -1))


def _dil_kernel(q_ref, kp_ref, kc_ref, vp_ref, vc_ref, o_ref, lse_ref, *, rel, nq):
    n = pl.program_id(2)
    qi = lax.broadcasted_iota(jnp.int32, (QBLK, 2 * QBLK), 0)
    kj = lax.broadcasted_iota(jnp.int32, (QBLK, 2 * QBLK), 1)
    dist = qi + QBLK - kj
    band = (dist >= 0) & (dist <= rel)
    lane = lax.broadcasted_iota(jnp.int32, (QBLK, LANES), 1)
    low = lane < HEAD_DIM
    slabs = [slice(p * LANES, (p + 1) * LANES) for p in range(SEG_W // LANES)]

    def window(prev_ref, cur_ref, s, sl):
        if s == 0:
            return jnp.concatenate([prev_ref[:, sl], cur_ref[:QBLK, sl]], axis=0)
        return cur_ref[(s - 1) * QBLK:(s + 1) * QBLK, sl]

    def score_block(s):
        rows = slice(s * QBLK, (s + 1) * QBLK)
        scores = []
        for sl in slabs:
            q2 = q_ref[rows, sl]
            k2 = window(kp_ref, kc_ref, s, sl)
            for sel in (low, jnp.logical_not(low)):
                scores.append(_nt_dot(jnp.where(sel, q2, jnp.zeros_like(q2)), k2))
        return scores

    def finish_block(s, scores):
        rows = slice(s * QBLK, (s + 1) * QBLK)
        valid = band & ((kj >= QBLK) | (n * nq + s > 0))
        probs = []
        for sc in scores:
            sc = jnp.where(valid, sc, -jnp.inf)
            m = jnp.max(sc, axis=1, keepdims=True)
            pe = jnp.exp(sc - m)
            den = jnp.sum(pe, axis=1, keepdims=True)
            probs.append((pe.astype(BF16), den, m + jnp.log(den)))
        for p, sl in enumerate(slabs):
            v2 = window(vp_ref, vc_ref, s, sl)
            outs = [jnp.dot(pe, v2, preferred_element_type=F32) / den for pe, den, _ in probs[2 * p:2 * p + 2]]
            o_ref[rows, sl] = jnp.where(low, outs[0], outs[1]).astype(BF16)
            lse_ref[rows, sl] = jnp.where(low, probs[2 * p][2], probs[2 * p + 1][2])

    pending = score_block(0)
    for s in range(1, nq):
        nxt = score_block(s)
        finish_block(s - 1, pending)
        pending = nxt
    finish_block(nq - 1, pending)


def _dilated_group(qkv, batch, seq, group):
    window, dil = DIL_PAIRS[group]
    rel = window // dil
    length = seq // dil
    nb = length // QBLK
    nq = math.gcd(nb, DIL_NQ)
    view = qkv.reshape(batch, dil, length, QKV_W)

    def spec(seg, prev):
        if prev:
            return pl.BlockSpec((None, None, QBLK, SEG_W), lambda b, r, n: (b, r, jnp.maximum(n * nq - 1, 0), seg))
        return pl.BlockSpec((None, None, nq * QBLK, SEG_W), lambda b, r, n: (b, r, n, seg))

    out_spec = pl.BlockSpec((None, None, nq * QBLK, SEG_W), lambda b, r, n: (b, r, n, 0))
    o, lse = pl.pallas_call(
        functools.partial(_dil_kernel, rel=rel, nq=nq),
        grid=(batch, dil, nb // nq),
        in_specs=[spec(0, False), spec(1, True), spec(1, False), spec(2, True), spec(2, False)],
        out_specs=[out_spec, out_spec],
        out_shape=[jax.ShapeDtypeStruct((batch, dil, length, SEG_W), BF16),
                   jax.ShapeDtypeStruct((batch, dil, length, SEG_W), F32)],
        compiler_params=_cparams(3),
        name=f"dilated{group}",
    )(view, view, view, view, view)
    return o, lse


def _layer_norm(z, g, b):
    mu = jnp.mean(z, axis=1, keepdims=True)
    zc = z - mu
    var = jnp.mean(zc * zc, axis=1, keepdims=True)
    return zc * lax.rsqrt(var + EPS) * g + b


def _pack_bf16_pairs(v):
    n = v.shape[1] // 2
    lo = pltpu.bitcast(v[:, :n].astype(BF16).astype(F32), U32)
    hi = pltpu.bitcast(v[:, n:].astype(BF16).astype(F32), U32)
    return jnp.bitwise_or(jnp.right_shift(lo, jnp.uint32(16)), jnp.bitwise_and(hi, jnp.uint32(0xFFFF0000)))


def _unpack_bf16_pairs(w):
    lo = pltpu.bitcast(jnp.left_shift(w, jnp.uint32(16)), F32)
    hi = pltpu.bitcast(jnp.bitwise_and(w, jnp.uint32(0xFFFF0000)), F32)
    return lo, hi


def _merge_kernel(ya_ref, o1_ref, o2_ref, o3_ref, l1_ref, l2_ref, l3_ref, x_ref,
                  wg_ref, bg_ref, woa_ref, wob_ref, wo_ref, g_ref, b_ref, wrh_ref, wrl_ref, br_ref,
                  h_ref, hp_ref, topi_ref, topw_ref, rank_ref, cnt_ref, carry_ref, scr_ref):
    i = pl.program_id(0)
    tm = x_ref.shape[0]

    @pl.when(i == 0)
    def _():
        carry_ref[...] = jnp.zeros_like(carry_ref)

    hm = tm // 2
    halves = (slice(0, hm), slice(hm, tm))
    xs = [x_ref[rows, :] for rows in halves]
    gate_dot = lambda x: jnp.dot(x.astype(BF16), wg_ref[...], preferred_element_type=F32) + bg_ref[...]
    gate_pre = [gate_dot(xs[0])]
    pas = [jnp.dot(ya_ref[rows, :], woa_ref[...], preferred_element_type=F32) for rows in halves]

    def token_major(ref, scr):
        dil, n = ref.shape[0], ref.shape[1]
        if dil == 1:
            return ref[0].astype(F32)
        for r in range(dil):
            blk = ref[r].astype(F32)
            for c in range(SEG_W // LANES):
                scr[c, pl.ds(r, n, stride=dil), :] = blk[:, c * LANES:(c + 1) * LANES]
        return jnp.concatenate([scr[c] for c in range(SEG_W // LANES)], axis=1)

    oa, ob, oc = (token_major(r, scr_ref.at[k]) for k, r in enumerate((o1_ref, o2_ref, o3_ref)))
    la, lb, lc = (token_major(r, scr_ref.at[3 + k]) for k, r in enumerate((l1_ref, l2_ref, l3_ref)))
    mx = jnp.maximum(jnp.maximum(la, lb), lc)
    ea, eb, ec = jnp.exp(la - mx), jnp.exp(lb - mx), jnp.exp(lc - mx)
    yb = ((ea * oa + eb * ob + ec * oc) / (ea + eb + ec)).astype(BF16)
    pbs = [jnp.dot(yb[rows, :], wob_ref[...], preferred_element_type=F32) for rows in halves]
    gate_pre.append(gate_dot(xs[1]))
    mixes = []
    for pre_act, pa, pb in zip(gate_pre, pas, pbs):
        gates = jax.nn.sigmoid(pre_act)
        merged = gates[:, :D_MODEL] * pa + gates[:, D_MODEL:] * pb
        mixes.append(jnp.dot(merged.astype(BF16), wo_ref[...], preferred_element_type=F32))
    logits = []
    for rows, x, mix in zip(halves, xs, mixes):
        h = _layer_norm(DN_ALPHA * x + mix, g_ref[...], b_ref[...])
        h_ref[rows, :] = h
        hp_ref[rows, :] = _pack_bf16_pairs(h)
        hh = h.astype(BF16)
        hl = (h - hh.astype(F32)).astype(BF16)
        logits.append(jnp.dot(hh, wrh_ref[...], preferred_element_type=F32)
                      + jnp.dot(hl, wrh_ref[...], preferred_element_type=F32)
                      + jnp.dot(hh, wrl_ref[...], preferred_element_type=F32) + br_ref[...])

    lane = lax.broadcasted_iota(jnp.int32, (hm, LANES), 1)
    r_i = lax.broadcasted_iota(jnp.int32, (hm, hm), 0)
    c_i = lax.broadcasted_iota(jnp.int32, (hm, hm), 1)
    tri = (r_i > c_i).astype(BF16)
    before = carry_ref[...]
    for rows, lg in zip(halves, logits):
        lg = jnp.where(lane < N_EXPERTS, lg, -jnp.inf)
        vals, idxs = [], []
        for _ in range(TOP_K):
            mv = jnp.max(lg, axis=1, keepdims=True)
            ik = jnp.min(jnp.where(lg == mv, lane, LANES), axis=1, keepdims=True)
            vals.append(mv)
            idxs.append(ik)
            lg = jnp.where(lane == ik, -jnp.inf, lg)
        es = [jnp.exp(v - vals[0]) for v in vals]
        tot = es[0] + es[1] + es[2] + es[3]
        onehot = jnp.zeros((hm, LANES), F32)
        for ik in idxs:
            onehot = onehot + (lane == ik).astype(F32)
        pre = jnp.dot(tri, onehot.astype(BF16), preferred_element_type=F32) + before
        topi = jnp.zeros((hm, LANES), jnp.int32)
        topw = jnp.zeros((hm, LANES), F32)
        rank = jnp.zeros((hm, LANES), F32)
        for k in range(TOP_K):
            rk = jnp.sum(jnp.where(lane == idxs[k], pre, 0.0), axis=1, keepdims=True)
            topi = jnp.where(lane == k, idxs[k], topi)
            topw = jnp.where(lane == k, es[k] / tot, topw)
            rank = jnp.where(lane == k, rk, rank)
        topi_ref[rows, :] = topi
        topw_ref[rows, :] = topw
        rank_ref[rows, :] = rank.astype(jnp.int32)
        before = before + jnp.sum(onehot, axis=0, keepdims=True)
    carry_ref[...] = before
    cnt_ref[...] = before


def _merge(ya, dil_outs, x2, seq, w_b, b_gate, w_oa_b, w_ob_b, w_o_b, ln1_g, ln1_b, w_router, b_router):
    tokens = x2.shape[0]
    tm = MERGE_TM
    per_seq = seq // tm

    def group(a):
        dil = a.shape[1]
        return pl.BlockSpec((None, dil, tm // dil, SEG_W), lambda i: (i // per_seq, 0, i % per_seq, 0))

    gate_blk = w_b.shape[1] // (2 * D_MODEL) - 1
    wr = jnp.zeros((D_MODEL, LANES), F32).at[:, :N_EXPERTS].set(w_router)
    wr_hi = wr.astype(BF16)
    wr_lo = (wr - wr_hi.astype(F32)).astype(BF16)
    br = jnp.zeros((1, LANES), F32).at[0, :N_EXPERTS].set(b_router)
    row = lambda w: pl.BlockSpec((tm, w), lambda i: (i, 0))
    full = lambda a: pl.BlockSpec(a.shape, lambda i: (0,) * a.ndim)
    (o1, l1), (o2, l2), (o3, l3) = dil_outs
    bg = b_gate.reshape(1, -1)
    g1 = ln1_g.reshape(1, -1)
    b1 = ln1_b.reshape(1, -1)
    lane_out = lambda dt: jax.ShapeDtypeStruct((tokens, LANES), dt)
    return pl.pallas_call(
        _merge_kernel,
        grid=(tokens // tm,),
        in_specs=[row(SEG_W), group(o1), group(o2), group(o3), group(l1), group(l2), group(l3),
                  row(D_MODEL), pl.BlockSpec((D_MODEL, 2 * D_MODEL), lambda i: (0, gate_blk)),
                  full(bg), full(w_oa_b), full(w_ob_b), full(w_o_b), full(g1), full(b1),
                  full(wr_hi), full(wr_lo), full(br)],
        out_specs=[row(D_MODEL), row(D_MODEL // 2), row(LANES), row(LANES), row(LANES),
                   pl.BlockSpec((1, LANES), lambda i: (0, 0))],
        out_shape=[jax.ShapeDtypeStruct((tokens, D_MODEL), F32), jax.ShapeDtypeStruct((tokens, D_MODEL // 2), U32),
                   lane_out(jnp.int32), lane_out(F32), lane_out(jnp.int32), jax.ShapeDtypeStruct((1, LANES), F32)],
        scratch_shapes=[pltpu.VMEM((1, LANES), F32), pltpu.VMEM((6, SEG_W // LANES, tm, LANES), F32)],
        compiler_params=_cparams(1),
        name="merge",
    )(ya, o1, o2, o3, l1, l2, l3, x2, w_b, bg, w_oa_b, w_ob_b, w_o_b, g1, b1, wr_hi, wr_lo, br)


def _sc_mesh():
    return plsc.VectorSubcoreMesh(core_axis_name="core", subcore_axis_name="subcore")


def _sc_scatter_rows(table, src_idx, dst_idx):
    n = src_idx.shape[0]
    d = table.shape[1]
    mesh = _sc_mesh()
    workers = mesh.num_cores * mesh.num_subcores
    per = n // (SC_CHUNK * workers)
    assert per * SC_CHUNK * workers == n

    @pl.kernel(out_type=jax.ShapeDtypeStruct((n, d), table.dtype), mesh=mesh,
               scratch_types=[pltpu.VMEM((1, SC_CHUNK), jnp.int32), pltpu.VMEM((1, SC_CHUNK), jnp.int32),
                              pltpu.VMEM((SC_CHUNK, d), table.dtype)])
    def copy(t_hbm, s_hbm, d_hbm, o_hbm, s_vm, d_vm, buf):
        wid = lax.axis_index("core") * mesh.num_subcores + lax.axis_index("subcore")

        @pl.loop(0, per)
        def _(j):
            blk = wid * per + j
            pltpu.sync_copy(s_hbm.at[pl.ds(blk, 1)], s_vm)
            pltpu.sync_copy(d_hbm.at[pl.ds(blk, 1)], d_vm)
            pltpu.sync_copy(t_hbm.at[s_vm.at[0]], buf)
            pltpu.sync_copy(buf, o_hbm.at[d_vm.at[0]])

    return copy(table, src_idx.reshape(-1, SC_CHUNK), dst_idx.reshape(-1, SC_CHUNK))


def _sc_gather_rows(table, idx):
    n = idx.shape[0]
    d = table.shape[1]
    mesh = _sc_mesh()
    workers = mesh.num_cores * mesh.num_subcores
    per = n // (SC_CHUNK * workers)
    assert per * SC_CHUNK * workers == n

    @pl.kernel(out_type=jax.ShapeDtypeStruct((n, d), table.dtype), mesh=mesh,
               scratch_types=[pltpu.VMEM((1, SC_CHUNK), jnp.int32), pltpu.VMEM((SC_CHUNK, d), table.dtype)])
    def gather(t_hbm, i_hbm, o_hbm, i_vm, buf):
        wid = lax.axis_index("core") * mesh.num_subcores + lax.axis_index("subcore")

        @pl.loop(0, per)
        def _(j):
            blk = wid * per + j
            pltpu.sync_copy(i_hbm.at[pl.ds(blk, 1)], i_vm)
            pltpu.sync_copy(t_hbm.at[i_vm.at[0]], buf)
            pltpu.sync_copy(buf, o_hbm.at[pl.ds(blk * SC_CHUNK, SC_CHUNK)])

    return gather(table, idx.reshape(-1, SC_CHUNK))


def _expert_kernel(be_ref, nx_ref, nu_ref, x_ref, wgu_hbm, bgu_ref, wd_hbm, bd_ref, y_ref,
                   wgu_stage, wd_stage, wgu_b, wd_b, sem):
    i = pl.program_id(0)
    used = i < nu_ref[0]
    expert = be_ref[i]

    def weight_copies(e):
        return (pltpu.make_async_copy(wgu_hbm.at[e], wgu_stage, sem.at[0]),
                pltpu.make_async_copy(wd_hbm.at[e], wd_stage, sem.at[1]))

    @pl.when(i == 0)
    def _():
        for cp in weight_copies(expert):
            cp.start()

    @pl.when(used & ((i == 0) | (expert != be_ref[jnp.maximum(i - 1, 0)])))
    def _():
        for cp in weight_copies(expert):
            cp.wait()
        wgu_b[...] = wgu_stage[...].astype(BF16)
        wd_b[...] = wd_stage[...].astype(BF16)

        @pl.when(nx_ref[i] >= 0)
        def _():
            for cp in weight_copies(nx_ref[i]):
                cp.start()

    @pl.when(used)
    def _():
        lo, hi = _unpack_bf16_pairs(x_ref[...])
        xb = jnp.concatenate([lo, hi], axis=1).astype(BF16)
        chunk = 512
        cols = [slice(c * chunk, (c + 1) * chunk) for c in range(D_FF // chunk)]
        pre = []
        for gs in cols:
            us = slice(D_FF + gs.start, D_FF + gs.stop)
            gate = jnp.dot(xb, wgu_b[:, gs], preferred_element_type=F32) + bgu_ref[:, gs]
            up = jnp.dot(xb, wgu_b[:, us], preferred_element_type=F32) + bgu_ref[:, us]
            pre.append((gate, up))
        acc = jnp.zeros((x_ref.shape[0], D_MODEL), F32)
        for gs, (gate, up) in zip(cols, pre):
            gate = jnp.minimum(gate, SWIGLU_LIMIT)
            up = jnp.clip(up, -SWIGLU_LIMIT, SWIGLU_LIMIT)
            act = (up + 1.0) * gate * jax.nn.sigmoid(SWIGLU_ALPHA * gate)
            acc = acc + jnp.dot(act.astype(BF16), wd_b[gs, :], preferred_element_type=F32)
        y_ref[...] = _pack_bf16_pairs(acc + bd_ref[...])

    @pl.when(i >= nu_ref[0])
    def _():
        y_ref[...] = jnp.zeros_like(y_ref)


def _experts(xrows, block_e, next_e, n_used, w_gu, b_gu, w_down, b_down):
    n_rows = xrows.shape[0]
    bm = MOE_BM
    bias = lambda i, be, nx, nu: (be[i], 0, 0)
    return pl.pallas_call(
        _expert_kernel,
        grid_spec=pltpu.PrefetchScalarGridSpec(
            num_scalar_prefetch=3,
            grid=(n_rows // bm,),
            in_specs=[
                pl.BlockSpec((bm, D_MODEL // 2), lambda i, be, nx, nu: (i, 0)),
                pl.BlockSpec(memory_space=pl.ANY),
                pl.BlockSpec((None, 1, 2 * D_FF), bias),
                pl.BlockSpec(memory_space=pl.ANY),
                pl.BlockSpec((None, 1, D_MODEL), bias),
            ],
            out_specs=pl.BlockSpec((bm, D_MODEL // 2), lambda i, be, nx, nu: (i, 0)),
            scratch_shapes=[pltpu.VMEM((D_MODEL, 2 * D_FF), F32), pltpu.VMEM((D_FF, D_MODEL), F32),
                            pltpu.VMEM((D_MODEL, 2 * D_FF), BF16), pltpu.VMEM((D_FF, D_MODEL), BF16),
                            pltpu.SemaphoreType.DMA((2,))],
        ),
        out_shape=jax.ShapeDtypeStruct((n_rows, D_MODEL // 2), U32),
        compiler_params=_cparams(1),
        name="experts",
    )(block_e, next_e, n_used, xrows, w_gu, b_gu.reshape(N_EXPERTS, 1, -1), w_down, b_down.reshape(N_EXPERTS, 1, -1))


def _combine_kernel(y0_ref, y1_ref, y2_ref, y3_ref, w_ref, h_ref, g_ref, b_ref, o_ref):
    w = w_ref[...]
    half = D_MODEL // 2
    f_lo = jnp.zeros((h_ref.shape[0], half), F32)
    f_hi = jnp.zeros((h_ref.shape[0], half), F32)
    for k, y_ref in enumerate((y0_ref, y1_ref, y2_ref, y3_ref)):
        lo, hi = _unpack_bf16_pairs(y_ref[...])
        f_lo = f_lo + lo * w[:, k:k + 1]
        f_hi = f_hi + hi * w[:, k:k + 1]
    f = jnp.concatenate([f_lo, f_hi], axis=1)
    o_ref[...] = _layer_norm(DN_ALPHA * h_ref[...] + f, g_ref[...], b_ref[...])


def _combine(yplanes, topw, h, ln2_g, ln2_b):
    tokens = h.shape[0]
    tm = MOE_TM
    per_plane = tokens // tm
    plane = lambda k: pl.BlockSpec((tm, D_MODEL // 2), lambda i: (k * per_plane + i, 0))
    return pl.pallas_call(
        _combine_kernel,
        grid=(per_plane,),
        in_specs=[plane(0), plane(1), plane(2), plane(3),
                  pl.BlockSpec((tm, LANES), lambda i: (i, 0)),
                  pl.BlockSpec((tm, D_MODEL), lambda i: (i, 0)),
                  pl.BlockSpec((1, D_MODEL), lambda i: (0, 0)),
                  pl.BlockSpec((1, D_MODEL), lambda i: (0, 0))],
        out_specs=pl.BlockSpec((tm, D_MODEL), lambda i: (i, 0)),
        out_shape=jax.ShapeDtypeStruct((tokens, D_MODEL), F32),
        compiler_params=_cparams(1),
        name="combine",
    )(yplanes, yplanes, yplanes, yplanes, topw, h, ln2_g.reshape(1, -1), ln2_b.reshape(1, -1))


def _routing_tables(topi, rank, cnt, tokens):
    bm = MOE_BM
    i32 = jnp.int32
    experts = jnp.arange(N_EXPERTS, dtype=i32)
    counts = cnt[0, :N_EXPERTS].astype(i32)
    padded = (counts + bm - 1) // bm * bm
    pend = jnp.cumsum(padded)
    pstart = pend - padded
    sel = topi[:, :TOP_K, None] == experts[None, None, :]
    dest = rank[:, :TOP_K] + jnp.sum(jnp.where(sel, pstart[None, None, :], 0), axis=-1)
    n_pad = N_EXPERTS * bm
    n_rows = tokens * TOP_K + n_pad
    starts = jnp.arange(n_rows // bm, dtype=i32) * bm
    block_e = jnp.minimum(jnp.sum((pend[None, :] <= starts[:, None]).astype(i32), axis=1), N_EXPERTS - 1)
    n_used = (pend[-1] // bm).reshape(1)
    is_block_e = block_e[:, None] == experts[None, :]
    after = jnp.sum(jnp.where(is_block_e, pend[None, :], 0), axis=1) // bm
    e_after = jnp.sum(jnp.where(after[:, None] == jnp.arange(n_rows // bm, dtype=i32)[None, :], block_e[None, :], 0),
                      axis=1)
    next_e = jnp.where(after < n_used[0], e_after, -1)
    pad_cnt = padded - counts
    pad_end = jnp.cumsum(pad_cnt)
    j = jnp.arange(n_pad, dtype=i32)
    owner = jnp.sum((pad_end[None, :] <= j[:, None]).astype(i32), axis=1)
    is_owner = owner[:, None] == experts[None, :]
    pick = lambda v: jnp.sum(jnp.where(is_owner, v[None, :], 0), axis=1)
    in_expert = pick(pstart + counts) + (j - pick(pad_end - pad_cnt))
    pad_rows = jnp.where(owner < N_EXPERTS, in_expert, pend[-1] + (j - pad_end[-1]))
    src_tok = jnp.concatenate([jnp.repeat(jnp.arange(tokens, dtype=i32), TOP_K), j % tokens])
    dst_row = jnp.concatenate([dest.reshape(-1), pad_rows]).astype(i32)
    return dest.astype(i32), block_e.astype(i32), next_e.astype(i32), n_used.astype(i32), src_tok, dst_row


def kernel(x, w_in, b_gate, lam_q1, lam_k1, lam_q2, lam_k2, subln_g, w_oa, w_ob, w_o, ln1_g, ln1_b,
           w_router, b_router, w_gu, b_gu, w_down, b_down, ln2_g, ln2_b):
    batch, seq, d = x.shape
    tokens = batch * seq
    h = x.reshape(tokens, d)
    tables = {dilation: _rope_lane_table(seq, dilation) for _, dilation in DIL_PAIRS}
    for l in range(DEPTH):
        w_b = w_in[l].astype(BF16)
        hb = h.astype(BF16)
        qk_a, vt_a = _project_qkv(hb, w_b, 0, tables[1], seq, "qkv_diff",
                                  q_scale=HEAD_DIM ** -0.5 * math.log2(math.e), v_feature_major=True)
        ya = _diff_attention(qk_a.reshape(batch, seq, 2 * SEG_W), vt_a, lam_q1[l], lam_k1[l], lam_q2[l],
                             lam_k2[l], subln_g[l]).reshape(tokens, -1)
        dil = []
        for g, (_, dilation) in enumerate(DIL_PAIRS):
            if dilation == 1:
                hb_g = hb
            else:
                hb_g = hb.reshape(batch, seq // dilation, dilation, d).swapaxes(1, 2).reshape(tokens, d)
            qkv_g = _project_qkv(hb_g, w_b, g + 1, tables[dilation], seq, f"qkv_dil{g}", q_scale=HEAD_DIM ** -0.5)
            dil.append(_dilated_group(qkv_g, batch, seq, g))
        h1, h1p, topi, topw, rank, cnt = _merge(ya, dil, h, seq, w_b, b_gate[l], w_oa[l].astype(BF16),
                                                w_ob[l].astype(BF16), w_o[l].astype(BF16), ln1_g[l], ln1_b[l],
                                                w_router[l], b_router[l])
        dest, block_e, next_e, n_used, src_tok, dst_row = _routing_tables(topi, rank, cnt, tokens)
        xrows = _sc_scatter_rows(h1p, src_tok, dst_row)
        yrows = _experts(xrows, block_e, next_e, n_used, w_gu[l], b_gu[l], w_down[l], b_down[l])
        yplanes = _sc_gather_rows(yrows, dest.T.reshape(-1))
        h = _combine(yplanes, topw, h1, ln2_g[l], ln2_b[l])
    return h.reshape(batch, seq, d)
```

```python
import functools
import math

import jax
import jax.numpy as jnp
from jax import lax
from jax.experimental import pallas as pl
from jax.experimental.pallas import tpu as pltpu
from jax.experimental.pallas import tpu_sc as plsc

F32 = jnp.float32
BF16 = jnp.bfloat16
U32 = jnp.uint32

D_MODEL = 1024
HEAD_DIM = 64
ROT_DIM = HEAD_DIM // 4
ROPE_THETA = 500000.0
QBLK = 128
DA_HEADS = 4
DIL_PAIRS = ((128, 1), (512, 4), (2048, 16))
SEG_W = 512
QKV_W = 3 * SEG_W
N_EXPERTS = 32
TOP_K = 4
D_FF = D_MODEL
SWIGLU_ALPHA = 1.702
SWIGLU_LIMIT = 7.0
DEPTH = 1
DN_ALPHA = (2 * DEPTH) ** 0.25
EPS = 1e-5
LAMBDA_INIT = 0.8 - 0.6 * math.exp(-0.3 * 0)

LANES = 128
QKV_TM = 512
QKV_CHUNK = 256
ATT_TQ = 256
DIL_NQ = 4
MERGE_TM = 512
MOE_BM = 256
MOE_TM = 256
SC_CHUNK = 128
VMEM_LIMIT = 52 * 1024 * 1024


def _cparams(n_axes):
    return pltpu.CompilerParams(dimension_semantics=("arbitrary",) * n_axes,
                                vmem_limit_bytes=VMEM_LIMIT)


def _qkv_kernel(x_ref, w_ref, cs_ref, o_ref, *rest, q_scale):
    *maybe_vt_ref, wb_ref = rest

    @pl.when(pl.program_id(0) == 0)
    def _():
        wb_ref[...] = w_ref[...].astype(BF16)

    tm = x_ref.shape[0]
    half = ROT_DIM // 2
    in_head = lax.broadcasted_iota(jnp.int32, (QKV_CHUNK, LANES), 1) % HEAD_DIM
    first, second = in_head < half, (in_head >= half) & (in_head < ROT_DIM)
    for rc in range(tm // QKV_CHUNK):
        rows = slice(rc * QKV_CHUNK, (rc + 1) * QKV_CHUNK)
        xb = x_ref[rows, :].astype(BF16)
        cs = cs_ref[rows, :]
        c = jnp.where(first, cs, jnp.where(second, pltpu.roll(cs, half, 1), 1.0))
        s1 = jnp.where(second, cs, 0.0)
        s2 = jnp.where(first, -pltpu.roll(cs, LANES - half, 1), 0.0)
        for seg in range(3):
            acc = jnp.dot(xb, wb_ref[:, seg * SEG_W:(seg + 1) * SEG_W], preferred_element_type=F32)
            if seg == 2:
                if maybe_vt_ref:
                    maybe_vt_ref[0][:, rows] = acc.T.astype(BF16)
                else:
                    o_ref[rows, seg * SEG_W:(seg + 1) * SEG_W] = acc.astype(BF16)
                continue
            for k in range(SEG_W // LANES):
                t = acc[:, k * LANES:(k + 1) * LANES]
                r = t * c + pltpu.roll(t, ROT_DIM // 2, 1) * s1 + pltpu.roll(t, LANES - ROT_DIM // 2, 1) * s2
                if seg == 0:
                    r = r * q_scale
                lo = seg * SEG_W + k * LANES
                o_ref[rows, lo:lo + LANES] = r.astype(BF16)


def _rope_lane_table(seq, dil):
    row = jnp.arange(seq, dtype=jnp.int32)
    length = seq // dil
    pos = ((row % length) * dil + row // length).astype(F32)
    inv_freq = ROPE_THETA ** (-jnp.arange(0, ROT_DIM, 2, dtype=F32) / ROT_DIM)
    ang = pos[:, None] * inv_freq[None, :]
    head = jnp.concatenate([jnp.cos(ang), jnp.sin(ang), jnp.zeros((seq, HEAD_DIM - ROT_DIM), F32)], axis=1)
    return jnp.tile(head, (1, LANES // HEAD_DIM))


def _project_qkv(xb, w_b, col_block, table, seq, name, q_scale, v_feature_major=False):
    tokens = xb.shape[0]
    tm = min(QKV_TM, seq)
    per_seq = seq // tm
    tab_spec = pl.BlockSpec((tm, LANES), lambda i: (i % per_seq, 0))
    out_w = 2 * SEG_W if v_feature_major else QKV_W
    out_specs = [pl.BlockSpec((tm, out_w), lambda i: (i, 0))]
    out_shape = [jax.ShapeDtypeStruct((tokens, out_w), BF16)]
    if v_feature_major:
        out_specs.append(pl.BlockSpec((None, SEG_W, tm), lambda i: (i // per_seq, 0, i % per_seq)))
        out_shape.append(jax.ShapeDtypeStruct((tokens // seq, SEG_W, seq), BF16))
    outs = pl.pallas_call(
        functools.partial(_qkv_kernel, q_scale=q_scale),
        grid=(tokens // tm,),
        in_specs=[pl.BlockSpec((tm, D_MODEL), lambda i: (i, 0)),
                  pl.BlockSpec((D_MODEL, QKV_W), lambda i: (0, col_block)),
                  tab_spec],
        out_specs=out_specs,
        out_shape=out_shape,
        scratch_shapes=[pltpu.VMEM((D_MODEL, QKV_W), BF16)],
        compiler_params=_cparams(1),
        name=name,
    )(xb, w_b, table)
    return outs if v_feature_major else outs[0]


def _nt_dot(a, b):
    return lax.dot_general(a, b, (((1,), (1,)), ((), ())), preferred_element_type=F32)


def _diff_kernel(q_ref, k_ref, vt_ref, lq1_ref, lk1_ref, lq2_ref, lk2_ref, g_ref, o_ref, s_ref, acc_ref, *, tq):
    i = pl.program_id(1)
    chains = 2 * DA_HEADS
    half = tq // 2
    lane = lax.broadcasted_iota(jnp.int32, (tq, LANES), 1)
    qs = []
    for h in range(DA_HEADS):
        q = q_ref[:, h * LANES:(h + 1) * LANES]
        zero = jnp.zeros_like(q)
        qs += [jnp.where(lane < HEAD_DIM, q, zero), jnp.where(lane >= HEAD_DIM, q, zero)]
    acc_ref[...] = jnp.zeros_like(acc_ref)

    def score_block(j, buf):
        start = pl.multiple_of(j * tq, tq)
        for c in range(chains):
            kb = k_ref[pl.ds(start, tq), (c // 2) * LANES:(c // 2 + 1) * LANES]
            buf[c] = _nt_dot(kb, qs[c])

    def absorb(j, buf, state, masked):
        start = pl.multiple_of(j * tq, tq)
        out = []
        for c in range(chains):
            m, l = state[c]
            if masked:
                key = lax.broadcasted_iota(jnp.int32, (tq, tq), 0)
                qry = lax.broadcasted_iota(jnp.int32, (tq, tq), 1)
                buf[c] = jnp.where(key <= qry, buf[c], -jnp.inf)
            m_new = jnp.maximum(m, jnp.max(buf[c], axis=0, keepdims=True))
            a = jnp.exp2(m - m_new)
            p = [jnp.exp2(buf[c, u * half:(u + 1) * half, :] - m_new) for u in range(2)]
            l = a * l + jnp.sum(p[0], axis=0, keepdims=True) + jnp.sum(p[1], axis=0, keepdims=True)
            vt = vt_ref[(c // 2) * LANES:(c // 2 + 1) * LANES, pl.ds(start, tq)]
            pv = jnp.dot(vt, jnp.concatenate(p, axis=0).astype(BF16), preferred_element_type=F32)
            acc_ref[c] = a * acc_ref[c] + pv
            out.append((m_new, l))
        return tuple(out)

    buf_a, buf_b = s_ref.at[0], s_ref.at[1]

    def two_blocks(jj, state):
        j = 2 * jj
        score_block(j + 1, buf_b)
        state = absorb(j, buf_a, state, False)
        score_block(j + 2, buf_a)
        return absorb(j + 1, buf_b, state, False)

    def last_from_a(state):
        return absorb(i, buf_a, state, True)

    def last_from_b(state):
        score_block(i, buf_b)
        state = absorb(i - 1, buf_a, state, False)
        return absorb(i, buf_b, state, True)

    init = tuple((jnp.full((1, tq), -jnp.inf, F32), jnp.zeros((1, tq), F32)) for _ in range(chains))
    score_block(0, buf_a)
    state = lax.fori_loop(0, i // 2, two_blocks, init)
    state = lax.cond(lax.rem(i, 2) == 0, last_from_a, last_from_b, state)

    lam = (jnp.exp(jnp.sum(lq1_ref[...] * lk1_ref[...], axis=1, keepdims=True))
           - jnp.exp(jnp.sum(lq2_ref[...] * lk2_ref[...], axis=1, keepdims=True)) + LAMBDA_INIT)
    for h in range(DA_HEADS):
        l1, l2 = state[2 * h][1], state[2 * h + 1][1]
        o = (acc_ref[2 * h] / l1 - lam * (acc_ref[2 * h + 1] / l2)).T
        ms = jnp.mean(o * o, axis=1, keepdims=True)
        o = o * lax.rsqrt(ms + EPS) * g_ref[...]
        o_ref[:, h * LANES:(h + 1) * LANES] = (o * (1.0 - LAMBDA_INIT)).astype(BF16)


def _diff_attention(qk3, vt3, lam_q1, lam_k1, lam_q2, lam_k2, subln_g):
    b, s, _ = qk3.shape
    tq = min(ATT_TQ, s)
    vec = lambda n: pl.BlockSpec((1, n), lambda bb, i: (0, 0))
    return pl.pallas_call(
        functools.partial(_diff_kernel, tq=tq),
        grid=(b, s // tq),
        in_specs=[
            pl.BlockSpec((None, tq, SEG_W), lambda bb, i: (bb, i, 0)),
            pl.BlockSpec((None, s, SEG_W), lambda bb, i: (bb, 0, 1)),
            pl.BlockSpec((None, SEG_W, s), lambda bb, i: (bb, 0, 0)),
            vec(HEAD_DIM), vec(HEAD_DIM), vec(HEAD_DIM), vec(HEAD_DIM), vec(2 * HEAD_DIM),
        ],
        out_specs=pl.BlockSpec((None, tq, SEG_W), lambda bb, i: (bb, i, 0)),
        out_shape=jax.ShapeDtypeStruct((b, s, SEG_W), BF16),
        scratch_shapes=[pltpu.VMEM((2, 2 * DA_HEADS, tq, tq), F32), pltpu.VMEM((2 * DA_HEADS, LANES, tq), F32)],
        compiler_params=_cparams(2),
        name="diff_attn",
    )(qk3, qk3, vt3, lam_q1.reshape(1, -1), lam_k1.reshape(1, -1), lam_q2.reshape(1, -1),
      lam_k2.reshape(1, -1), subln_g.reshape(1, -1))


def _dil_kernel(q_ref, kp_ref, kc_ref, vp_ref, vc_ref, o_ref, lse_ref, *, rel, nq):
    n = pl.program_id(2)
    qi = lax.broadcasted_iota(jnp.int32, (QBLK, 2 * QBLK), 0)
    kj = lax.broadcasted_iota(jnp.int32, (QBLK, 2 * QBLK), 1)
    dist = qi + QBLK - kj
    band = (dist >= 0) & (dist <= rel)
    lane = lax.broadcasted_iota(jnp.int32, (QBLK, LANES), 1)
    low = lane < HEAD_DIM
    slabs = [slice(p * LANES, (p + 1) * LANES) for p in range(SEG_W // LANES)]

    def window(prev_ref, cur_ref, s, sl):
        if s == 0:
            return jnp.concatenate([prev_ref[:, sl], cur_ref[:QBLK, sl]], axis=0)
        return cur_ref[(s - 1) * QBLK:(s + 1) * QBLK, sl]

    def score_block(s):
        rows = slice(s * QBLK, (s + 1) * QBLK)
        scores = []
        for sl in slabs:
            q2 = q_ref[rows, sl]
            k2 = window(kp_ref, kc_ref, s, sl)
            for sel in (low, jnp.logical_not(low)):
                scores.append(_nt_dot(jnp.where(sel, q2, jnp.zeros_like(q2)), k2))
        return scores

    def finish_block(s, scores):
        rows = slice(s * QBLK, (s + 1) * QBLK)
        valid = band & ((kj >= QBLK) | (n * nq + s > 0))
        probs = []
        for sc in scores:
            sc = jnp.where(valid, sc, -jnp.inf)
            m = jnp.max(sc, axis=1, keepdims=True)
            pe = jnp.exp(sc - m)
            den = jnp.sum(pe, axis=1, keepdims=True)
            probs.append((pe.astype(BF16), den, m + jnp.log(den)))
        for p, sl in enumerate(slabs):
            v2 = window(vp_ref, vc_ref, s, sl)
            outs = [jnp.dot(pe, v2, preferred_element_type=F32) / den for pe, den, _ in probs[2 * p:2 * p + 2]]
            o_ref[rows, sl] = jnp.where(low, outs[0], outs[1]).astype(BF16)
            lse_ref[rows, sl] = jnp.where(low, probs[2 * p][2], probs[2 * p + 1][2])

    pending = score_block(0)
    for s in range(1, nq):
        nxt = score_block(s)
        finish_block(s - 1, pending)
        pending = nxt
    finish_block(nq - 1, pending)


def _dilated_group(qkv, batch, seq, group):
    window, dil = DIL_PAIRS[group]
    rel = window // dil
    length = seq // dil
    nb = length // QBLK
    nq = math.gcd(nb, DIL_NQ)
    view = qkv.reshape(batch, dil, length, QKV_W)

    def spec(seg, prev):
        if prev:
            return pl.BlockSpec((None, None, QBLK, SEG_W), lambda b, r, n: (b, r, jnp.maximum(n * nq - 1, 0), seg))
        return pl.BlockSpec((None, None, nq * QBLK, SEG_W), lambda b, r, n: (b, r, n, seg))

    out_spec = pl.BlockSpec((None, None, nq * QBLK, SEG_W), lambda b, r, n: (b, r, n, 0))
    o, lse = pl.pallas_call(
        functools.partial(_dil_kernel, rel=rel, nq=nq),
        grid=(batch, dil, nb // nq),
        in_specs=[spec(0, False), spec(1, True), spec(1, False), spec(2, True), spec(2, False)],
        out_specs=[out_spec, out_spec],
        out_shape=[jax.ShapeDtypeStruct((batch, dil, length, SEG_W), BF16),
                   jax.ShapeDtypeStruct((batch, dil, length, SEG_W), F32)],
        compiler_params=_cparams(3),
        name=f"dilated{group}",
    )(view, view, view, view, view)
    return o, lse


def _layer_norm(z, g, b):
    mu = jnp.mean(z, axis=1, keepdims=True)
    zc = z - mu
    var = jnp.mean(zc * zc, axis=1, keepdims=True)
    return zc * lax.rsqrt(var + EPS) * g + b


def _pack_bf16_pairs(v):
    n = v.shape[1] // 2
    lo = pltpu.bitcast(v[:, :n].astype(BF16).astype(F32), U32)
    hi = pltpu.bitcast(v[:, n:].astype(BF16).astype(F32), U32)
    return jnp.bitwise_or(jnp.right_shift(lo, jnp.uint32(16)), jnp.bitwise_and(hi, jnp.uint32(0xFFFF0000)))


def _unpack_bf16_pairs(w):
    lo = pltpu.bitcast(jnp.left_shift(w, jnp.uint32(16)), F32)
    hi = pltpu.bitcast(jnp.bitwise_and(w, jnp.uint32(0xFFFF0000)), F32)
    return lo, hi


def _merge_kernel(ya_ref, o1_ref, o2_ref, o3_ref, l1_ref, l2_ref, l3_ref, x_ref,
                  wg_ref, bg_ref, woa_ref, wob_ref, wo_ref, g_ref, b_ref, wrh_ref, wrl_ref, br_ref,
                  h_ref, hp_ref, topi_ref, topw_ref, rank_ref, cnt_ref, carry_ref, scr_ref):
    i = pl.program_id(0)
    tm = x_ref.shape[0]

    @pl.when(i == 0)
    def _():
        carry_ref[...] = jnp.zeros_like(carry_ref)

    hm = tm // 2
    halves = (slice(0, hm), slice(hm, tm))
    xs = [x_ref[rows, :] for rows in halves]
    gate_dot = lambda x: jnp.dot(x.astype(BF16), wg_ref[...], preferred_element_type=F32) + bg_ref[...]
    gate_pre = [gate_dot(xs[0])]
    pas = [jnp.dot(ya_ref[rows, :], woa_ref[...], preferred_element_type=F32) for rows in halves]

    def token_major(ref, scr):
        dil, n = ref.shape[0], ref.shape[1]
        if dil == 1:
            return ref[0].astype(F32)
        for r in range(dil):
            blk = ref[r].astype(F32)
            for c in range(SEG_W // LANES):
                scr[c, pl.ds(r, n, stride=dil), :] = blk[:, c * LANES:(c + 1) * LANES]
        return jnp.concatenate([scr[c] for c in range(SEG_W // LANES)], axis=1)

    oa, ob, oc = (token_major(r, scr_ref.at[k]) for k, r in enumerate((o1_ref, o2_ref, o3_ref)))
    la, lb, lc = (token_major(r, scr_ref.at[3 + k]) for k, r in enumerate((l1_ref, l2_ref, l3_ref)))
    mx = jnp.maximum(jnp.maximum(la, lb), lc)
    ea, eb, ec = jnp.exp(la - mx), jnp.exp(lb - mx), jnp.exp(lc - mx)
    yb = ((ea * oa + eb * ob + ec * oc) / (ea + eb + ec)).astype(BF16)
    pbs = [jnp.dot(yb[rows, :], wob_ref[...], preferred_element_type=F32) for rows in halves]
    gate_pre.append(gate_dot(xs[1]))
    mixes = []
    for pre_act, pa, pb in zip(gate_pre, pas, pbs):
        gates = jax.nn.sigmoid(pre_act)
        merged = gates[:, :D_MODEL] * pa + gates[:, D_MODEL:] * pb
        mixes.append(jnp.dot(merged.astype(BF16), wo_ref[...], preferred_element_type=F32))
    logits = []
    for rows, x, mix in zip(halves, xs, mixes):
        h = _layer_norm(DN_ALPHA * x + mix, g_ref[...], b_ref[...])
        h_ref[rows, :] = h
        hp_ref[rows, :] = _pack_bf16_pairs(h)
        hh = h.astype(BF16)
        hl = (h - hh.astype(F32)).astype(BF16)
        logits.append(jnp.dot(hh, wrh_ref[...], preferred_element_type=F32)
                      + jnp.dot(hl, wrh_ref[...], preferred_element_type=F32)
                      + jnp.dot(hh, wrl_ref[...], preferred_element_type=F32) + br_ref[...])

    lane = lax.broadcasted_iota(jnp.int32, (hm, LANES), 1)
    r_i = lax.broadcasted_iota(jnp.int32, (hm, hm), 0)
    c_i = lax.broadcasted_iota(jnp.int32, (hm, hm), 1)
    tri = (r_i > c_i).astype(BF16)
    before = carry_ref[...]
    for rows, lg in zip(halves, logits):
        lg = jnp.where(lane < N_EXPERTS, lg, -jnp.inf)
        vals, idxs = [], []
        for _ in range(TOP_K):
            mv = jnp.max(lg, axis=1, keepdims=True)
            ik = jnp.min(jnp.where(lg == mv, lane, LANES), axis=1, keepdims=True)
            vals.append(mv)
            idxs.append(ik)
            lg = jnp.where(lane == ik, -jnp.inf, lg)
        es = [jnp.exp(v - vals[0]) for v in vals]
        tot = es[0] + es[1] + es[2] + es[3]
        onehot = jnp.zeros((hm, LANES), F32)
        for ik in idxs:
            onehot = onehot + (lane == ik).astype(F32)
        pre = jnp.dot(tri, onehot.astype(BF16), preferred_element_type=F32) + before
        topi = jnp.zeros((hm, LANES), jnp.int32)
        topw = jnp.zeros((hm, LANES), F32)
        rank = jnp.zeros((hm, LANES), F32)
        for k in range(TOP_K):
            rk = jnp.sum(jnp.where(lane == idxs[k], pre, 0.0), axis=1, keepdims=True)
            topi = jnp.where(lane == k, idxs[k], topi)
            topw = jnp.where(lane == k, es[k] / tot, topw)
            rank = jnp.where(lane == k, rk, rank)
        topi_ref[rows, :] = topi
        topw_ref[rows, :] = topw
        rank_ref[rows, :] = rank.astype(jnp.int32)
        before = before + jnp.sum(onehot, axis=0, keepdims=True)
    carry_ref[...] = before
    cnt_ref[...] = before


def _merge(ya, dil_outs, x2, seq, w_b, b_gate, w_oa_b, w_ob_b, w_o_b, ln1_g, ln1_b, w_router, b_router):
    tokens = x2.shape[0]
    tm = MERGE_TM
    per_seq = seq // tm

    def group(a):
        dil = a.shape[1]
        return pl.BlockSpec((None, dil, tm // dil, SEG_W), lambda i: (i // per_seq, 0, i % per_seq, 0))

    wr = jnp.zeros((D_MODEL, LANES), F32).at[:, :N_EXPERTS].set(w_router)
    wr_hi = wr.astype(BF16)
    wr_lo = (wr - wr_hi.astype(F32)).astype(BF16)
    br = jnp.zeros((1, LANES), F32).at[0, :N_EXPERTS].set(b_router)
    row = lambda w: pl.BlockSpec((tm, w), lambda i: (i, 0))
    full = lambda a: pl.BlockSpec(a.shape, lambda i: (0,) * a.ndim)
    (o1, l1), (o2, l2), (o3, l3) = dil_outs
    bg = b_gate.reshape(1, -1)
    g1 = ln1_g.reshape(1, -1)
    b1 = ln1_b.reshape(1, -1)
    lane_out = lambda dt: jax.ShapeDtypeStruct((tokens, LANES), dt)
    return pl.pallas_call(
        _merge_kernel,
        grid=(tokens // tm,),
        in_specs=[row(SEG_W), group(o1), group(o2), group(o3), group(l1), group(l2), group(l3),
                  row(D_MODEL), full(w_b),
                  full(bg), full(w_oa_b), full(w_ob_b), full(w_o_b), full(g1), full(b1),
                  full(wr_hi), full(wr_lo), full(br)],
        out_specs=[row(D_MODEL), row(D_MODEL // 2), row(LANES), row(LANES), row(LANES),
                   pl.BlockSpec((1, LANES), lambda i: (0, 0))],
        out_shape=[jax.ShapeDtypeStruct((tokens, D_MODEL), F32), jax.ShapeDtypeStruct((tokens, D_MODEL // 2), U32),
                   lane_out(jnp.int32), lane_out(F32), lane_out(jnp.int32), jax.ShapeDtypeStruct((1, LANES), F32)],
        scratch_shapes=[pltpu.VMEM((1, LANES), F32), pltpu.VMEM((6, SEG_W // LANES, tm, LANES), F32)],
        compiler_params=_cparams(1),
        name="merge",
    )(ya, o1, o2, o3, l1, l2, l3, x2, w_b, bg, w_oa_b, w_ob_b, w_o_b, g1, b1, wr_hi, wr_lo, br)


def _sc_mesh():
    return plsc.VectorSubcoreMesh(core_axis_name="core", subcore_axis_name="subcore")


def _sc_scatter_rows(table, src_idx, dst_idx):
    n = src_idx.shape[0]
    d = table.shape[1]
    mesh = _sc_mesh()
    workers = mesh.num_cores * mesh.num_subcores
    per = n // (SC_CHUNK * workers)
    assert per * SC_CHUNK * workers == n

    @pl.kernel(out_type=jax.ShapeDtypeStruct((n, d), table.dtype), mesh=mesh,
               scratch_types=[pltpu.VMEM((1, SC_CHUNK), jnp.int32), pltpu.VMEM((1, SC_CHUNK), jnp.int32),
                              pltpu.VMEM((SC_CHUNK, d), table.dtype)])
    def copy(t_hbm, s_hbm, d_hbm, o_hbm, s_vm, d_vm, buf):
        wid = lax.axis_index("core") * mesh.num_subcores + lax.axis_index("subcore")

        @pl.loop(0, per)
        def _(j):
            blk = wid * per + j
            pltpu.sync_copy(s_hbm.at[pl.ds(blk, 1)], s_vm)
            pltpu.sync_copy(d_hbm.at[pl.ds(blk, 1)], d_vm)
            pltpu.sync_copy(t_hbm.at[s_vm.at[0]], buf)
            pltpu.sync_copy(buf, o_hbm.at[d_vm.at[0]])

    return copy(table, src_idx.reshape(-1, SC_CHUNK), dst_idx.reshape(-1, SC_CHUNK))


def _sc_gather_rows(table, idx):
    n = idx.shape[0]
    d = table.shape[1]
    mesh = _sc_mesh()
    workers = mesh.num_cores * mesh.num_subcores
    per = n // (SC_CHUNK * workers)
    assert per * SC_CHUNK * workers == n

    @pl.kernel(out_type=jax.ShapeDtypeStruct((n, d), table.dtype), mesh=mesh,
               scratch_types=[pltpu.VMEM((1, SC_CHUNK), jnp.int32), pltpu.VMEM((SC_CHUNK, d), table.dtype)])
    def gather(t_hbm, i_hbm, o_hbm, i_vm, buf):
        wid = lax.axis_index("core") * mesh.num_subcores + lax.axis_index("subcore")

        @pl.loop(0, per)
        def _(j):
            blk = wid * per + j
            pltpu.sync_copy(i_hbm.at[pl.ds(blk, 1)], i_vm)
            pltpu.sync_copy(t_hbm.at[i_vm.at[0]], buf)
            pltpu.sync_copy(buf, o_hbm.at[pl.ds(blk * SC_CHUNK, SC_CHUNK)])

    return gather(table, idx.reshape(-1, SC_CHUNK))


def _expert_kernel(be_ref, nx_ref, nu_ref, x_ref, wgu_hbm, bgu_ref, wd_hbm, bd_ref, y_ref,
                   wgu_stage, wd_stage, wgu_b, wd_b, sem):
    i = pl.program_id(0)
    used = i < nu_ref[0]
    expert = be_ref[i]

    def weight_copies(e):
        return (pltpu.make_async_copy(wgu_hbm.at[e], wgu_stage, sem.at[0]),
                pltpu.make_async_copy(wd_hbm.at[e], wd_stage, sem.at[1]))

    @pl.when(i == 0)
    def _():
        for cp in weight_copies(expert):
            cp.start()

    @pl.when(used & ((i == 0) | (expert != be_ref[jnp.maximum(i - 1, 0)])))
    def _():
        for cp in weight_copies(expert):
            cp.wait()
        wgu_b[...] = wgu_stage[...].astype(BF16)
        wd_b[...] = wd_stage[...].astype(BF16)

        @pl.when(nx_ref[i] >= 0)
        def _():
            for cp in weight_copies(nx_ref[i]):
                cp.start()

    @pl.when(used)
    def _():
        lo, hi = _unpack_bf16_pairs(x_ref[...])
        xb = jnp.concatenate([lo, hi], axis=1).astype(BF16)
        chunk = 512
        cols = [slice(c * chunk, (c + 1) * chunk) for c in range(D_FF // chunk)]
        pre = []
        for gs in cols:
            us = slice(D_FF + gs.start, D_FF + gs.stop)
            gate = jnp.dot(xb, wgu_b[:, gs], preferred_element_type=F32) + bgu_ref[:, gs]
            up = jnp.dot(xb, wgu_b[:, us], preferred_element_type=F32) + bgu_ref[:, us]
            pre.append((gate, up))
        acc = jnp.zeros((x_ref.shape[0], D_MODEL), F32)
        for gs, (gate, up) in zip(cols, pre):
            gate = jnp.minimum(gate, SWIGLU_LIMIT)
            up = jnp.clip(up, -SWIGLU_LIMIT, SWIGLU_LIMIT)
            act = (up + 1.0) * gate * jax.nn.sigmoid(SWIGLU_ALPHA * gate)
            acc = acc + jnp.dot(act.astype(BF16), wd_b[gs, :], preferred_element_type=F32)
        y_ref[...] = _pack_bf16_pairs(acc + bd_ref[...])

    @pl.when(i >= nu_ref[0])
    def _():
        y_ref[...] = jnp.zeros_like(y_ref)


def _experts(xrows, block_e, next_e, n_used, w_gu, b_gu, w_down, b_down):
    n_rows = xrows.shape[0]
    bm = MOE_BM
    bias = lambda i, be, nx, nu: (be[i], 0, 0)
    return pl.pallas_call(
        _expert_kernel,
        grid_spec=pltpu.PrefetchScalarGridSpec(
            num_scalar_prefetch=3,
            grid=(n_rows // bm,),
            in_specs=[
                pl.BlockSpec((bm, D_MODEL // 2), lambda i, be, nx, nu: (i, 0)),
                pl.BlockSpec(memory_space=pl.ANY),
                pl.BlockSpec((None, 1, 2 * D_FF), bias),
                pl.BlockSpec(memory_space=pl.ANY),
                pl.BlockSpec((None, 1, D_MODEL), bias),
            ],
            out_specs=pl.BlockSpec((bm, D_MODEL // 2), lambda i, be, nx, nu: (i, 0)),
            scratch_shapes=[pltpu.VMEM((D_MODEL, 2 * D_FF), F32), pltpu.VMEM((D_FF, D_MODEL), F32),
                            pltpu.VMEM((D_MODEL, 2 * D_FF), BF16), pltpu.VMEM((D_FF, D_MODEL), BF16),
                            pltpu.SemaphoreType.DMA((2,))],
        ),
        out_shape=jax.ShapeDtypeStruct((n_rows, D_MODEL // 2), U32),
        compiler_params=_cparams(1),
        name="experts",
    )(block_e, next_e, n_used, xrows, w_gu, b_gu.reshape(N_EXPERTS, 1, -1), w_down, b_down.reshape(N_EXPERTS, 1, -1))


def _combine_kernel(y0_ref, y1_ref, y2_ref, y3_ref, w_ref, h_ref, g_ref, b_ref, o_ref):
    w = w_ref[...]
    half = D_MODEL // 2
    f_lo = jnp.zeros((h_ref.shape[0], half), F32)
    f_hi = jnp.zeros((h_ref.shape[0], half), F32)
    for k, y_ref in enumerate((y0_ref, y1_ref, y2_ref, y3_ref)):
        lo, hi = _unpack_bf16_pairs(y_ref[...])
        f_lo = f_lo + lo * w[:, k:k + 1]
        f_hi = f_hi + hi * w[:, k:k + 1]
    f = jnp.concatenate([f_lo, f_hi], axis=1)
    o_ref[...] = _layer_norm(DN_ALPHA * h_ref[...] + f, g_ref[...], b_ref[...])


def _combine(yplanes, topw, h, ln2_g, ln2_b):
    tokens = h.shape[0]
    tm = MOE_TM
    per_plane = tokens // tm
    plane = lambda k: pl.BlockSpec((tm, D_MODEL // 2), lambda i: (k * per_plane + i, 0))
    return pl.pallas_call(
        _combine_kernel,
        grid=(per_plane,),
        in_specs=[plane(0), plane(1), plane(2), plane(3),
                  pl.BlockSpec((tm, LANES), lambda i: (i, 0)),
                  pl.BlockSpec((tm, D_MODEL), lambda i: (i, 0)),
                  pl.BlockSpec((1, D_MODEL), lambda i: (0, 0)),
                  pl.BlockSpec((1, D_MODEL), lambda i: (0, 0))],
        out_specs=pl.BlockSpec((tm, D_MODEL), lambda i: (i, 0)),
        out_shape=jax.ShapeDtypeStruct((tokens, D_MODEL), F32),
        compiler_params=_cparams(1),
        name="combine",
    )(yplanes, yplanes, yplanes, yplanes, topw, h, ln2_g.reshape(1, -1), ln2_b.reshape(1, -1))


def _routing_tables(topi, rank, cnt, tokens):
    bm = MOE_BM
    i32 = jnp.int32
    experts = jnp.arange(N_EXPERTS, dtype=i32)
    counts = cnt[0, :N_EXPERTS].astype(i32)
    padded = (counts + bm - 1) // bm * bm
    pend = jnp.cumsum(padded)
    pstart = pend - padded
    sel = topi[:, :TOP_K, None] == experts[None, None, :]
    dest = rank[:, :TOP_K] + jnp.sum(jnp.where(sel, pstart[None, None, :], 0), axis=-1)
    n_pad = N_EXPERTS * bm
    n_rows = tokens * TOP_K + n_pad
    starts = jnp.arange(n_rows // bm, dtype=i32) * bm
    block_e = jnp.minimum(jnp.sum((pend[None, :] <= starts[:, None]).astype(i32), axis=1), N_EXPERTS - 1)
    n_used = (pend[-1] // bm).reshape(1)
    is_block_e = block_e[:, None] == experts[None, :]
    after = jnp.sum(jnp.where(is_block_e, pend[None, :], 0), axis=1) // bm
    e_after = jnp.sum(jnp.where(after[:, None] == jnp.arange(n_rows // bm, dtype=i32)[None, :], block_e[None, :], 0),
                      axis=1)
    next_e = jnp.where(after < n_used[0], e_after, -1)
    pad_cnt = padded - counts
    pad_end = jnp.cumsum(pad_cnt)
    j = jnp.arange(n_pad, dtype=i32)
    owner = jnp.sum((pad_end[None, :] <= j[:, None]).astype(i32), axis=1)
    is_owner = owner[:, None] == experts[None, :]
    pick = lambda v: jnp.sum(jnp.where(is_owner, v[None, :], 0), axis=1)
    in_expert = pick(pstart + counts) + (j - pick(pad_end - pad_cnt))
    pad_rows = jnp.where(owner < N_EXPERTS, in_expert, pend[-1] + (j - pad_end[-1]))
    src_tok = jnp.concatenate([jnp.repeat(jnp.arange(tokens, dtype=i32), TOP_K), j % tokens])
    dst_row = jnp.concatenate([dest.reshape(-1), pad_rows]).astype(i32)
    return dest.astype(i32), block_e.astype(i32), next_e.astype(i32), n_used.astype(i32), src_tok, dst_row


def kernel(x, w_in, b_gate, lam_q1, lam_k1, lam_q2, lam_k2, subln_g, w_oa, w_ob, w_o, ln1_g, ln1_b,
           w_router, b_router, w_gu, b_gu, w_down, b_down, ln2_g, ln2_b):
    batch, seq, d = x.shape
    tokens = batch * seq
    h = x.reshape(tokens, d)
    tables = {dilation: _rope_lane_table(seq, dilation) for _, dilation in DIL_PAIRS}
    for l in range(DEPTH):
        qk_a, vt_a = _project_qkv(h, w_in[l], 0, tables[1], seq, "qkv_diff",
                                  q_scale=HEAD_DIM ** -0.5 * math.log2(math.e), v_feature_major=True)
        ya = _diff_attention(qk_a.reshape(batch, seq, 2 * SEG_W), vt_a, lam_q1[l], lam_k1[l], lam_q2[l],
                             lam_k2[l], subln_g[l]).reshape(tokens, -1)
        dil = []
        for g, (_, dilation) in enumerate(DIL_PAIRS):
            if dilation == 1:
                h_g = h
            else:
                h_g = (h.reshape(batch, seq // dilation, dilation, d).swapaxes(1, 2).reshape(tokens, d)
                       .astype(BF16))
            qkv_g = _project_qkv(h_g, w_in[l], g + 1, tables[dilation], seq, f"qkv_dil{g}",
                                 q_scale=HEAD_DIM ** -0.5)
            dil.append(_dilated_group(qkv_g, batch, seq, g))
        w_gate_b = w_in[l][:, -2 * D_MODEL:].astype(BF16)
        h1, h1p, topi, topw, rank, cnt = _merge(ya, dil, h, seq, w_gate_b, b_gate[l], w_oa[l].astype(BF16),
                                                w_ob[l].astype(BF16), w_o[l].astype(BF16), ln1_g[l], ln1_b[l],
                                                w_router[l], b_router[l])
        dest, block_e, next_e, n_used, src_tok, dst_row = _routing_tables(topi, rank, cnt, tokens)
        xrows = _sc_scatter_rows(h1p, src_tok, dst_row)
        yrows = _experts(xrows, block_e, next_e, n_used, w_gu[l], b_gu[l], w_down[l], b_down[l])
        yplanes = _sc_gather_rows(yrows, dest.T.reshape(-1))
        h = _combine(yplanes, topw, h1, ln2_g[l], ln2_b[l])
    return h.reshape(batch, seq, d)
```

```python
import functools
import math

import jax
import jax.numpy as jnp
from jax import lax
from jax.experimental import pallas as pl
from jax.experimental.pallas import tpu as pltpu
from jax.experimental.pallas import tpu_sc as plsc

F32 = jnp.float32
BF16 = jnp.bfloat16
U32 = jnp.uint32

D_MODEL = 1024
HEAD_DIM = 64
ROT_DIM = HEAD_DIM // 4
ROPE_THETA = 500000.0
QBLK = 128
DA_HEADS = 4
DIL_PAIRS = ((128, 1), (512, 4), (2048, 16))
SEG_W = 512
QKV_W = 3 * SEG_W
N_EXPERTS = 32
TOP_K = 4
D_FF = D_MODEL
SWIGLU_ALPHA = 1.702
SWIGLU_LIMIT = 7.0
DEPTH = 1
DN_ALPHA = (2 * DEPTH) ** 0.25
EPS = 1e-5
LAMBDA_INIT = 0.8 - 0.6 * math.exp(-0.3 * 0)

LANES = 128
QKV_TM = 512
QKV_CHUNK = 256
ATT_TQ = 256
ONES_ROWS = 16
DIL_NQ = 4
MERGE_TM = 512
MOE_BM = 256
MOE_TM = 256
SC_CHUNK = 128
VMEM_LIMIT = 52 * 1024 * 1024


def _cparams(n_axes):
    return pltpu.CompilerParams(dimension_semantics=("arbitrary",) * n_axes,
                                vmem_limit_bytes=VMEM_LIMIT)


def _qkv_kernel(x_ref, w_ref, cs_ref, o_ref, *rest, q_scale):
    *maybe_vt_ref, wb_ref = rest

    @pl.when(pl.program_id(0) == 0)
    def _():
        wb_ref[...] = w_ref[...].astype(BF16)

    tm = x_ref.shape[0]
    half = ROT_DIM // 2
    in_head = lax.broadcasted_iota(jnp.int32, (QKV_CHUNK, LANES), 1) % HEAD_DIM
    first, second = in_head < half, (in_head >= half) & (in_head < ROT_DIM)
    for rc in range(tm // QKV_CHUNK):
        rows = slice(rc * QKV_CHUNK, (rc + 1) * QKV_CHUNK)
        xb = x_ref[rows, :].astype(BF16)
        cs = cs_ref[rows, :]
        c = jnp.where(first, cs, jnp.where(second, pltpu.roll(cs, half, 1), 1.0))
        s1 = jnp.where(second, cs, 0.0)
        s2 = jnp.where(first, -pltpu.roll(cs, LANES - half, 1), 0.0)
        for seg in range(3):
            acc = jnp.dot(xb, wb_ref[:, seg * SEG_W:(seg + 1) * SEG_W], preferred_element_type=F32)
            if seg == 2:
                if maybe_vt_ref:
                    maybe_vt_ref[0][:, rows] = acc.T.astype(BF16)
                else:
                    o_ref[rows, seg * SEG_W:(seg + 1) * SEG_W] = acc.astype(BF16)
                continue
            for k in range(SEG_W // LANES):
                t = acc[:, k * LANES:(k + 1) * LANES]
                r = t * c + pltpu.roll(t, ROT_DIM // 2, 1) * s1 + pltpu.roll(t, LANES - ROT_DIM // 2, 1) * s2
                if seg == 0:
                    r = r * q_scale
                lo = seg * SEG_W + k * LANES
                o_ref[rows, lo:lo + LANES] = r.astype(BF16)


def _rope_lane_table(seq, dil):
    row = jnp.arange(seq, dtype=jnp.int32)
    length = seq // dil
    pos = ((row % length) * dil + row // length).astype(F32)
    inv_freq = ROPE_THETA ** (-jnp.arange(0, ROT_DIM, 2, dtype=F32) / ROT_DIM)
    ang = pos[:, None] * inv_freq[None, :]
    head = jnp.concatenate([jnp.cos(ang), jnp.sin(ang), jnp.zeros((seq, HEAD_DIM - ROT_DIM), F32)], axis=1)
    return jnp.tile(head, (1, LANES // HEAD_DIM))


def _project_qkv(xb, w_b, col_block, table, seq, name, q_scale, v_feature_major=False):
    tokens = xb.shape[0]
    tm = min(QKV_TM, seq)
    per_seq = seq // tm
    tab_spec = pl.BlockSpec((tm, LANES), lambda i: (i % per_seq, 0))
    out_w = 2 * SEG_W if v_feature_major else QKV_W
    out_specs = [pl.BlockSpec((tm, out_w), lambda i: (i, 0))]
    out_shape = [jax.ShapeDtypeStruct((tokens, out_w), BF16)]
    if v_feature_major:
        out_specs.append(pl.BlockSpec((None, SEG_W, tm), lambda i: (i // per_seq, 0, i % per_seq)))
        out_shape.append(jax.ShapeDtypeStruct((tokens // seq, SEG_W, seq), BF16))
    outs = pl.pallas_call(
        functools.partial(_qkv_kernel, q_scale=q_scale),
        grid=(tokens // tm,),
        in_specs=[pl.BlockSpec((tm, D_MODEL), lambda i: (i, 0)),
                  pl.BlockSpec((D_MODEL, QKV_W), lambda i: (0, col_block)),
                  tab_spec],
        out_specs=out_specs,
        out_shape=out_shape,
        scratch_shapes=[pltpu.VMEM((D_MODEL, QKV_W), BF16)],
        compiler_params=_cparams(1),
        name=name,
    )(xb, w_b, table)
    return outs if v_feature_major else outs[0]


def _nt_dot(a, b):
    return lax.dot_general(a, b, (((1,), (1,)), ((), ())), preferred_element_type=F32)


def _diff_kernel(q_ref, k_ref, vt_ref, lq1_ref, lk1_ref, lq2_ref, lk2_ref, g_ref, o_ref, s_ref, acc_ref, *, tq):
    i = pl.program_id(1)
    chains = 2 * DA_HEADS
    half = tq // 2
    lane = lax.broadcasted_iota(jnp.int32, (tq, LANES), 1)
    qs = []
    for h in range(DA_HEADS):
        q = q_ref[:, h * LANES:(h + 1) * LANES]
        zero = jnp.zeros_like(q)
        qs += [jnp.where(lane < HEAD_DIM, q, zero), jnp.where(lane >= HEAD_DIM, q, zero)]
    acc_ref[...] = jnp.zeros_like(acc_ref)

    def score_chain(j, buf, c):
        start = pl.multiple_of(j * tq, tq)
        kb = k_ref[pl.ds(start, tq), (c // 2) * LANES:(c // 2 + 1) * LANES]
        buf[c] = _nt_dot(kb, qs[c])

    def score_block(j, buf):
        for c in range(chains):
            score_chain(j, buf, c)

    def absorb(j, buf, state, masked, nxt=None):
        start = pl.multiple_of(j * tq, tq)
        out = []
        for c in range(chains):
            if nxt is not None:
                score_chain(j + 1, nxt, c)
            m, l = state[c]
            if masked:
                key = lax.broadcasted_iota(jnp.int32, (tq, tq), 0)
                qry = lax.broadcasted_iota(jnp.int32, (tq, tq), 1)
                buf[c] = jnp.where(key <= qry, buf[c], -jnp.inf)
            m_new = jnp.maximum(m, jnp.max(buf[c], axis=0, keepdims=True))
            a = jnp.exp2(m - m_new)
            p = [jnp.exp2((buf[c, u * half:(u + 1) * half, :] - m_new).astype(BF16)) for u in range(2)]
            vt = vt_ref[(c // 2) * LANES:(c // 2 + 1) * LANES, pl.ds(start, tq)]
            vt1 = jnp.concatenate([vt, jnp.ones((ONES_ROWS, tq), BF16)], axis=0)
            pv = jnp.dot(vt1, jnp.concatenate(p, axis=0), preferred_element_type=F32)
            l = a * l + pv[LANES:LANES + 1, :]
            acc_ref[c] = a * acc_ref[c] + pv[:LANES, :]
            out.append((m_new, l))
        return tuple(out)

    buf_a, buf_b = s_ref.at[0], s_ref.at[1]

    def two_blocks(jj, state):
        j = 2 * jj
        state = absorb(j, buf_a, state, False, nxt=buf_b)
        return absorb(j + 1, buf_b, state, False, nxt=buf_a)

    def last_from_a(state):
        return absorb(i, buf_a, state, True)

    def last_from_b(state):
        state = absorb(i - 1, buf_a, state, False, nxt=buf_b)
        return absorb(i, buf_b, state, True)

    init = tuple((jnp.full((1, tq), -jnp.inf, F32), jnp.zeros((1, tq), F32)) for _ in range(chains))
    score_block(0, buf_a)
    state = lax.fori_loop(0, i // 2, two_blocks, init)
    state = lax.cond(lax.rem(i, 2) == 0, last_from_a, last_from_b, state)

    lam = (jnp.exp(jnp.sum(lq1_ref[...] * lk1_ref[...], axis=1, keepdims=True))
           - jnp.exp(jnp.sum(lq2_ref[...] * lk2_ref[...], axis=1, keepdims=True)) + LAMBDA_INIT)
    for h in range(DA_HEADS):
        l1, l2 = state[2 * h][1], state[2 * h + 1][1]
        o = (acc_ref[2 * h] / l1 - lam * (acc_ref[2 * h + 1] / l2)).T
        ms = jnp.mean(o * o, axis=1, keepdims=True)
        o = o * lax.rsqrt(ms + EPS) * g_ref[...]
        o_ref[:, h * LANES:(h + 1) * LANES] = (o * (1.0 - LAMBDA_INIT)).astype(BF16)


def _diff_attention(qk3, vt3, lam_q1, lam_k1, lam_q2, lam_k2, subln_g):
    b, s, _ = qk3.shape
    tq = min(ATT_TQ, s)
    vec = lambda n: pl.BlockSpec((1, n), lambda bb, i: (0, 0))
    return pl.pallas_call(
        functools.partial(_diff_kernel, tq=tq),
        grid=(b, s // tq),
        in_specs=[
            pl.BlockSpec((None, tq, SEG_W), lambda bb, i: (bb, i, 0)),
            pl.BlockSpec((None, s, SEG_W), lambda bb, i: (bb, 0, 1)),
            pl.BlockSpec((None, SEG_W, s), lambda bb, i: (bb, 0, 0)),
            vec(HEAD_DIM), vec(HEAD_DIM), vec(HEAD_DIM), vec(HEAD_DIM), vec(2 * HEAD_DIM),
        ],
        out_specs=pl.BlockSpec((None, tq, SEG_W), lambda bb, i: (bb, i, 0)),
        out_shape=jax.ShapeDtypeStruct((b, s, SEG_W), BF16),
        scratch_shapes=[pltpu.VMEM((2, 2 * DA_HEADS, tq, tq), F32), pltpu.VMEM((2 * DA_HEADS, LANES, tq), F32)],
        compiler_params=_cparams(2),
        name="diff_attn",
    )(qk3, qk3, vt3, lam_q1.reshape(1, -1), lam_k1.reshape(1, -1), lam_q2.reshape(1, -1),
      lam_k2.reshape(1, -1), subln_g.reshape(1, -1))


def _dil_kernel(q_ref, kp_ref, kc_ref, vp_ref, vc_ref, o_ref, lse_ref, *, rel, nq):
    n = pl.program_id(2)
    qi = lax.broadcasted_iota(jnp.int32, (QBLK, 2 * QBLK), 0)
    kj = lax.broadcasted_iota(jnp.int32, (QBLK, 2 * QBLK), 1)
    dist = qi + QBLK - kj
    band = (dist >= 0) & (dist <= rel)
    lane = lax.broadcasted_iota(jnp.int32, (QBLK, LANES), 1)
    low = lane < HEAD_DIM
    slabs = [slice(p * LANES, (p + 1) * LANES) for p in range(SEG_W // LANES)]

    def window(prev_ref, cur_ref, s, sl):
        if s == 0:
            return jnp.concatenate([prev_ref[:, sl], cur_ref[:QBLK, sl]], axis=0)
        return cur_ref[(s - 1) * QBLK:(s + 1) * QBLK, sl]

    def score_block(s):
        rows = slice(s * QBLK, (s + 1) * QBLK)
        scores = []
        for sl in slabs:
            q2 = q_ref[rows, sl]
            k2 = window(kp_ref, kc_ref, s, sl)
            for sel in (low, jnp.logical_not(low)):
                scores.append(_nt_dot(jnp.where(sel, q2, jnp.zeros_like(q2)), k2))
        return scores

    def finish_block(s, scores):
        rows = slice(s * QBLK, (s + 1) * QBLK)
        valid = band & ((kj >= QBLK) | (n * nq + s > 0))
        probs = []
        for sc in scores:
            sc = jnp.where(valid, sc, -jnp.inf)
            m = jnp.max(sc, axis=1, keepdims=True)
            pe = jnp.exp(sc - m)
            den = jnp.sum(pe, axis=1, keepdims=True)
            probs.append((pe.astype(BF16), den, m + jnp.log(den)))
        for p, sl in enumerate(slabs):
            v2 = window(vp_ref, vc_ref, s, sl)
            outs = [jnp.dot(pe, v2, preferred_element_type=F32) / den for pe, den, _ in probs[2 * p:2 * p + 2]]
            o_ref[rows, sl] = jnp.where(low, outs[0], outs[1]).astype(BF16)
            lse_ref[rows, sl] = jnp.where(low, probs[2 * p][2], probs[2 * p + 1][2])

    pending = score_block(0)
    for s in range(1, nq):
        nxt = score_block(s)
        finish_block(s - 1, pending)
        pending = nxt
    finish_block(nq - 1, pending)


def _dilated_group(qkv, batch, seq, group):
    window, dil = DIL_PAIRS[group]
    rel = window // dil
    length = seq // dil
    nb = length // QBLK
    nq = math.gcd(nb, DIL_NQ)
    view = qkv.reshape(batch, dil, length, QKV_W)

    def spec(seg, prev):
        if prev:
            return pl.BlockSpec((None, None, QBLK, SEG_W), lambda b, r, n: (b, r, jnp.maximum(n * nq - 1, 0), seg))
        return pl.BlockSpec((None, None, nq * QBLK, SEG_W), lambda b, r, n: (b, r, n, seg))

    out_spec = pl.BlockSpec((None, None, nq * QBLK, SEG_W), lambda b, r, n: (b, r, n, 0))
    o, lse = pl.pallas_call(
        functools.partial(_dil_kernel, rel=rel, nq=nq),
        grid=(batch, dil, nb // nq),
        in_specs=[spec(0, False), spec(1, True), spec(1, False), spec(2, True), spec(2, False)],
        out_specs=[out_spec, out_spec],
        out_shape=[jax.ShapeDtypeStruct((batch, dil, length, SEG_W), BF16),
                   jax.ShapeDtypeStruct((batch, dil, length, SEG_W), F32)],
        compiler_params=_cparams(3),
        name=f"dilated{group}",
    )(view, view, view, view, view)
    return o, lse


def _layer_norm(z, g, b):
    mu = jnp.mean(z, axis=1, keepdims=True)
    zc = z - mu
    var = jnp.mean(zc * zc, axis=1, keepdims=True)
    return zc * lax.rsqrt(var + EPS) * g + b


def _pack_bf16_pairs(v):
    n = v.shape[1] // 2
    lo = pltpu.bitcast(v[:, :n].astype(BF16).astype(F32), U32)
    hi = pltpu.bitcast(v[:, n:].astype(BF16).astype(F32), U32)
    return jnp.bitwise_or(jnp.right_shift(lo, jnp.uint32(16)), jnp.bitwise_and(hi, jnp.uint32(0xFFFF0000)))


def _unpack_bf16_pairs(w):
    lo = pltpu.bitcast(jnp.left_shift(w, jnp.uint32(16)), F32)
    hi = pltpu.bitcast(jnp.bitwise_and(w, jnp.uint32(0xFFFF0000)), F32)
    return lo, hi


def _merge_kernel(ya_ref, o1_ref, o2_ref, o3_ref, l1_ref, l2_ref, l3_ref, x_ref,
                  wg_ref, bg_ref, woa_ref, wob_ref, wo_ref, g_ref, b_ref, wrh_ref, wrl_ref, br_ref,
                  h_ref, hp_ref, topi_ref, topw_ref, rank_ref, cnt_ref, carry_ref, scr_ref):
    i = pl.program_id(0)
    tm = x_ref.shape[0]

    @pl.when(i == 0)
    def _():
        carry_ref[...] = jnp.zeros_like(carry_ref)

    hm = tm // 2
    halves = (slice(0, hm), slice(hm, tm))
    xs = [x_ref[rows, :] for rows in halves]
    gate_dot = lambda x: jnp.dot(x.astype(BF16), wg_ref[...], preferred_element_type=F32) + bg_ref[...]
    gate_pre = [gate_dot(xs[0])]
    pas = [jnp.dot(ya_ref[rows, :], woa_ref[...], preferred_element_type=F32) for rows in halves]

    def token_major(ref, scr):
        dil, n = ref.shape[0], ref.shape[1]
        if dil == 1:
            return ref[0].astype(F32)
        for r in range(dil):
            blk = ref[r].astype(F32)
            for c in range(SEG_W // LANES):
                scr[c, pl.ds(r, n, stride=dil), :] = blk[:, c * LANES:(c + 1) * LANES]
        return jnp.concatenate([scr[c] for c in range(SEG_W // LANES)], axis=1)

    oa, ob, oc = (token_major(r, scr_ref.at[k]) for k, r in enumerate((o1_ref, o2_ref, o3_ref)))
    la, lb, lc = (token_major(r, scr_ref.at[3 + k]) for k, r in enumerate((l1_ref, l2_ref, l3_ref)))
    mx = jnp.maximum(jnp.maximum(la, lb), lc)
    ea, eb, ec = jnp.exp(la - mx), jnp.exp(lb - mx), jnp.exp(lc - mx)
    yb = ((ea * oa + eb * ob + ec * oc) / (ea + eb + ec)).astype(BF16)
    pbs = [jnp.dot(yb[rows, :], wob_ref[...], preferred_element_type=F32) for rows in halves]
    gate_pre.append(gate_dot(xs[1]))
    mixes = []
    for pre_act, pa, pb in zip(gate_pre, pas, pbs):
        gates = jax.nn.sigmoid(pre_act)
        merged = gates[:, :D_MODEL] * pa + gates[:, D_MODEL:] * pb
        mixes.append(jnp.dot(merged.astype(BF16), wo_ref[...], preferred_element_type=F32))
    logits = []
    for rows, x, mix in zip(halves, xs, mixes):
        h = _layer_norm(DN_ALPHA * x + mix, g_ref[...], b_ref[...])
        h_ref[rows, :] = h
        hp_ref[rows, :] = _pack_bf16_pairs(h)
        hh = h.astype(BF16)
        hl = (h - hh.astype(F32)).astype(BF16)
        logits.append(jnp.dot(hh, wrh_ref[...], preferred_element_type=F32)
                      + jnp.dot(hl, wrh_ref[...], preferred_element_type=F32)
                      + jnp.dot(hh, wrl_ref[...], preferred_element_type=F32) + br_ref[...])

    lane = lax.broadcasted_iota(jnp.int32, (hm, LANES), 1)
    r_i = lax.broadcasted_iota(jnp.int32, (hm, hm), 0)
    c_i = lax.broadcasted_iota(jnp.int32, (hm, hm), 1)
    tri = (r_i > c_i).astype(BF16)
    before = carry_ref[...]
    for rows, lg in zip(halves, logits):
        lg = jnp.where(lane < N_EXPERTS, lg, -jnp.inf)
        vals, idxs = [], []
        for _ in range(TOP_K):
            mv = jnp.max(lg, axis=1, keepdims=True)
            ik = jnp.min(jnp.where(lg == mv, lane, LANES), axis=1, keepdims=True)
            vals.append(mv)
            idxs.append(ik)
            lg = jnp.where(lane == ik, -jnp.inf, lg)
        es = [jnp.exp(v - vals[0]) for v in vals]
        tot = es[0] + es[1] + es[2] + es[3]
        onehot = jnp.zeros((hm, LANES), F32)
        for ik in idxs:
            onehot = onehot + (lane == ik).astype(F32)
        pre = jnp.dot(tri, onehot.astype(BF16), preferred_element_type=F32) + before
        topi = jnp.zeros((hm, LANES), jnp.int32)
        topw = jnp.zeros((hm, LANES), F32)
        rank = jnp.zeros((hm, LANES), F32)
        for k in range(TOP_K):
            rk = jnp.sum(jnp.where(lane == idxs[k], pre, 0.0), axis=1, keepdims=True)
            topi = jnp.where(lane == k, idxs[k], topi)
            topw = jnp.where(lane == k, es[k] / tot, topw)
            rank = jnp.where(lane == k, rk, rank)
        topi_ref[rows, :] = topi
        topw_ref[rows, :] = topw
        rank_ref[rows, :] = rank.astype(jnp.int32)
        before = before + jnp.sum(onehot, axis=0, keepdims=True)
    carry_ref[...] = before
    cnt_ref[...] = before


def _merge(ya, dil_outs, x2, seq, w_b, b_gate, w_oa_b, w_ob_b, w_o_b, ln1_g, ln1_b, w_router, b_router):
    tokens = x2.shape[0]
    tm = MERGE_TM
    per_seq = seq // tm

    def group(a):
        dil = a.shape[1]
        return pl.BlockSpec((None, dil, tm // dil, SEG_W), lambda i: (i // per_seq, 0, i % per_seq, 0))

    gate_blk = w_b.shape[1] // (2 * D_MODEL) - 1
    wr = jnp.zeros((D_MODEL, LANES), F32).at[:, :N_EXPERTS].set(w_router)
    wr_hi = wr.astype(BF16)
    wr_lo = (wr - wr_hi.astype(F32)).astype(BF16)
    br = jnp.zeros((1, LANES), F32).at[0, :N_EXPERTS].set(b_router)
    row = lambda w: pl.BlockSpec((tm, w), lambda i: (i, 0))
    full = lambda a: pl.BlockSpec(a.shape, lambda i: (0,) * a.ndim)
    (o1, l1), (o2, l2), (o3, l3) = dil_outs
    bg = b_gate.reshape(1, -1)
    g1 = ln1_g.reshape(1, -1)
    b1 = ln1_b.reshape(1, -1)
    lane_out = lambda dt: jax.ShapeDtypeStruct((tokens, LANES), dt)
    return pl.pallas_call(
        _merge_kernel,
        grid=(tokens // tm,),
        in_specs=[row(SEG_W), group(o1), group(o2), group(o3), group(l1), group(l2), group(l3),
                  row(D_MODEL), pl.BlockSpec((D_MODEL, 2 * D_MODEL), lambda i: (0, gate_blk)),
                  full(bg), full(w_oa_b), full(w_ob_b), full(w_o_b), full(g1), full(b1),
                  full(wr_hi), full(wr_lo), full(br)],
        out_specs=[row(D_MODEL), row(D_MODEL // 2), row(LANES), row(LANES), row(LANES),
                   pl.BlockSpec((1, LANES), lambda i: (0, 0))],
        out_shape=[jax.ShapeDtypeStruct((tokens, D_MODEL), F32), jax.ShapeDtypeStruct((tokens, D_MODEL // 2), U32),
                   lane_out(jnp.int32), lane_out(F32), lane_out(jnp.int32), jax.ShapeDtypeStruct((1, LANES), F32)],
        scratch_shapes=[pltpu.VMEM((1, LANES), F32), pltpu.VMEM((6, SEG_W // LANES, tm, LANES), F32)],
        compiler_params=_cparams(1),
        name="merge",
    )(ya, o1, o2, o3, l1, l2, l3, x2, w_b, bg, w_oa_b, w_ob_b, w_o_b, g1, b1, wr_hi, wr_lo, br)


def _sc_mesh():
    return plsc.VectorSubcoreMesh(core_axis_name="core", subcore_axis_name="subcore")


def _sc_scatter_rows(table, src_idx, dst_idx):
    n = src_idx.shape[0]
    d = table.shape[1]
    mesh = _sc_mesh()
    workers = mesh.num_cores * mesh.num_subcores
    per = n // (SC_CHUNK * workers)
    assert per * SC_CHUNK * workers == n

    @pl.kernel(out_type=jax.ShapeDtypeStruct((n, d), table.dtype), mesh=mesh,
               scratch_types=[pltpu.VMEM((1, SC_CHUNK), jnp.int32), pltpu.VMEM((1, SC_CHUNK), jnp.int32),
                              pltpu.VMEM((SC_CHUNK, d), table.dtype)])
    def copy(t_hbm, s_hbm, d_hbm, o_hbm, s_vm, d_vm, buf):
        wid = lax.axis_index("core") * mesh.num_subcores + lax.axis_index("subcore")

        @pl.loop(0, per)
        def _(j):
            blk = wid * per + j
            pltpu.sync_copy(s_hbm.at[pl.ds(blk, 1)], s_vm)
            pltpu.sync_copy(d_hbm.at[pl.ds(blk, 1)], d_vm)
            pltpu.sync_copy(t_hbm.at[s_vm.at[0]], buf)
            pltpu.sync_copy(buf, o_hbm.at[d_vm.at[0]])

    return copy(table, src_idx.reshape(-1, SC_CHUNK), dst_idx.reshape(-1, SC_CHUNK))


def _sc_gather_rows(table, idx):
    n = idx.shape[0]
    d = table.shape[1]
    mesh = _sc_mesh()
    workers = mesh.num_cores * mesh.num_subcores
    per = n // (SC_CHUNK * workers)
    assert per * SC_CHUNK * workers == n

    @pl.kernel(out_type=jax.ShapeDtypeStruct((n, d), table.dtype), mesh=mesh,
               scratch_types=[pltpu.VMEM((1, SC_CHUNK), jnp.int32), pltpu.VMEM((SC_CHUNK, d), table.dtype)])
    def gather(t_hbm, i_hbm, o_hbm, i_vm, buf):
        wid = lax.axis_index("core") * mesh.num_subcores + lax.axis_index("subcore")

        @pl.loop(0, per)
        def _(j):
            blk = wid * per + j
            pltpu.sync_copy(i_hbm.at[pl.ds(blk, 1)], i_vm)
            pltpu.sync_copy(t_hbm.at[i_vm.at[0]], buf)
            pltpu.sync_copy(buf, o_hbm.at[pl.ds(blk * SC_CHUNK, SC_CHUNK)])

    return gather(table, idx.reshape(-1, SC_CHUNK))


def _expert_kernel(be_ref, nx_ref, nu_ref, x_ref, wgu_hbm, bgu_ref, wd_hbm, bd_ref, y_ref,
                   wgu_stage, wd_stage, wgu_b, wd_b, sem):
    i = pl.program_id(0)
    used = i < nu_ref[0]
    expert = be_ref[i]

    def weight_copies(e):
        return (pltpu.make_async_copy(wgu_hbm.at[e], wgu_stage, sem.at[0]),
                pltpu.make_async_copy(wd_hbm.at[e], wd_stage, sem.at[1]))

    @pl.when(i == 0)
    def _():
        for cp in weight_copies(expert):
            cp.start()

    @pl.when(used & ((i == 0) | (expert != be_ref[jnp.maximum(i - 1, 0)])))
    def _():
        for cp in weight_copies(expert):
            cp.wait()
        wgu_b[...] = wgu_stage[...].astype(BF16)
        wd_b[...] = wd_stage[...].astype(BF16)

        @pl.when(nx_ref[i] >= 0)
        def _():
            for cp in weight_copies(nx_ref[i]):
                cp.start()

    @pl.when(used)
    def _():
        lo, hi = _unpack_bf16_pairs(x_ref[...])
        xb = jnp.concatenate([lo, hi], axis=1).astype(BF16)
        chunk = 512
        cols = [slice(c * chunk, (c + 1) * chunk) for c in range(D_FF // chunk)]
        pre = []
        for gs in cols:
            us = slice(D_FF + gs.start, D_FF + gs.stop)
            gate = jnp.dot(xb, wgu_b[:, gs], preferred_element_type=F32) + bgu_ref[:, gs]
            up = jnp.dot(xb, wgu_b[:, us], preferred_element_type=F32) + bgu_ref[:, us]
            pre.append((gate, up))
        acc = jnp.zeros((x_ref.shape[0], D_MODEL), F32)
        for gs, (gate, up) in zip(cols, pre):
            gate = jnp.minimum(gate, SWIGLU_LIMIT)
            up = jnp.clip(up, -SWIGLU_LIMIT, SWIGLU_LIMIT)
            act = (up + 1.0) * gate * jax.nn.sigmoid(SWIGLU_ALPHA * gate)
            acc = acc + jnp.dot(act.astype(BF16), wd_b[gs, :], preferred_element_type=F32)
        y_ref[...] = _pack_bf16_pairs(acc + bd_ref[...])

    @pl.when(i >= nu_ref[0])
    def _():
        y_ref[...] = jnp.zeros_like(y_ref)


def _experts(xrows, block_e, next_e, n_used, w_gu, b_gu, w_down, b_down):
    n_rows = xrows.shape[0]
    bm = MOE_BM
    bias = lambda i, be, nx, nu: (be[i], 0, 0)
    return pl.pallas_call(
        _expert_kernel,
        grid_spec=pltpu.PrefetchScalarGridSpec(
            num_scalar_prefetch=3,
            grid=(n_rows // bm,),
            in_specs=[
                pl.BlockSpec((bm, D_MODEL // 2), lambda i, be, nx, nu: (i, 0)),
                pl.BlockSpec(memory_space=pl.ANY),
                pl.BlockSpec((None, 1, 2 * D_FF), bias),
                pl.BlockSpec(memory_space=pl.ANY),
                pl.BlockSpec((None, 1, D_MODEL), bias),
            ],
            out_specs=pl.BlockSpec((bm, D_MODEL // 2), lambda i, be, nx, nu: (i, 0)),
            scratch_shapes=[pltpu.VMEM((D_MODEL, 2 * D_FF), F32), pltpu.VMEM((D_FF, D_MODEL), F32),
                            pltpu.VMEM((D_MODEL, 2 * D_FF), BF16), pltpu.VMEM((D_FF, D_MODEL), BF16),
                            pltpu.SemaphoreType.DMA((2,))],
        ),
        out_shape=jax.ShapeDtypeStruct((n_rows, D_MODEL // 2), U32),
        compiler_params=_cparams(1),
        name="experts",
    )(block_e, next_e, n_used, xrows, w_gu, b_gu.reshape(N_EXPERTS, 1, -1), w_down, b_down.reshape(N_EXPERTS, 1, -1))


def _combine_kernel(y0_ref, y1_ref, y2_ref, y3_ref, w_ref, h_ref, g_ref, b_ref, o_ref):
    w = w_ref[...]
    half = D_MODEL // 2
    f_lo = jnp.zeros((h_ref.shape[0], half), F32)
    f_hi = jnp.zeros((h_ref.shape[0], half), F32)
    for k, y_ref in enumerate((y0_ref, y1_ref, y2_ref, y3_ref)):
        lo, hi = _unpack_bf16_pairs(y_ref[...])
        f_lo = f_lo + lo * w[:, k:k + 1]
        f_hi = f_hi + hi * w[:, k:k + 1]
    f = jnp.concatenate([f_lo, f_hi], axis=1)
    o_ref[...] = _layer_norm(DN_ALPHA * h_ref[...] + f, g_ref[...], b_ref[...])


def _combine(yplanes, topw, h, ln2_g, ln2_b):
    tokens = h.shape[0]
    tm = MOE_TM
    per_plane = tokens // tm
    plane = lambda k: pl.BlockSpec((tm, D_MODEL // 2), lambda i: (k * per_plane + i, 0))
    return pl.pallas_call(
        _combine_kernel,
        grid=(per_plane,),
        in_specs=[plane(0), plane(1), plane(2), plane(3),
                  pl.BlockSpec((tm, LANES), lambda i: (i, 0)),
                  pl.BlockSpec((tm, D_MODEL), lambda i: (i, 0)),
                  pl.BlockSpec((1, D_MODEL), lambda i: (0, 0)),
                  pl.BlockSpec((1, D_MODEL), lambda i: (0, 0))],
        out_specs=pl.BlockSpec((tm, D_MODEL), lambda i: (i, 0)),
        out_shape=jax.ShapeDtypeStruct((tokens, D_MODEL), F32),
        compiler_params=_cparams(1),
        name="combine",
    )(yplanes, yplanes, yplanes, yplanes, topw, h, ln2_g.reshape(1, -1), ln2_b.reshape(1, -1))


def _routing_tables(topi, rank, cnt, tokens):
    bm = MOE_BM
    i32 = jnp.int32
    experts = jnp.arange(N_EXPERTS, dtype=i32)
    counts = cnt[0, :N_EXPERTS].astype(i32)
    padded = (counts + bm - 1) // bm * bm
    pend = jnp.cumsum(padded)
    pstart = pend - padded
    sel = topi[:, :TOP_K, None] == experts[None, None, :]
    dest = rank[:, :TOP_K] + jnp.sum(jnp.where(sel, pstart[None, None, :], 0), axis=-1)
    n_pad = N_EXPERTS * bm
    n_rows = tokens * TOP_K + n_pad
    starts = jnp.arange(n_rows // bm, dtype=i32) * bm
    block_e = jnp.minimum(jnp.sum((pend[None, :] <= starts[:, None]).astype(i32), axis=1), N_EXPERTS - 1)
    n_used = (pend[-1] // bm).reshape(1)
    is_block_e = block_e[:, None] == experts[None, :]
    after = jnp.sum(jnp.where(is_block_e, pend[None, :], 0), axis=1) // bm
    e_after = jnp.sum(jnp.where(after[:, None] == jnp.arange(n_rows // bm, dtype=i32)[None, :], block_e[None, :], 0),
                      axis=1)
    next_e = jnp.where(after < n_used[0], e_after, -1)
    pad_cnt = padded - counts
    pad_end = jnp.cumsum(pad_cnt)
    j = jnp.arange(n_pad, dtype=i32)
    owner = jnp.sum((pad_end[None, :] <= j[:, None]).astype(i32), axis=1)
    is_owner = owner[:, None] == experts[None, :]
    pick = lambda v: jnp.sum(jnp.where(is_owner, v[None, :], 0), axis=1)
    in_expert = pick(pstart + counts) + (j - pick(pad_end - pad_cnt))
    pad_rows = jnp.where(owner < N_EXPERTS, in_expert, pend[-1] + (j - pad_end[-1]))
    src_tok = jnp.concatenate([jnp.repeat(jnp.arange(tokens, dtype=i32), TOP_K), j % tokens])
    dst_row = jnp.concatenate([dest.reshape(-1), pad_rows]).astype(i32)
    return dest.astype(i32), block_e.astype(i32), next_e.astype(i32), n_used.astype(i32), src_tok, dst_row


def kernel(x, w_in, b_gate, lam_q1, lam_k1, lam_q2, lam_k2, subln_g, w_oa, w_ob, w_o, ln1_g, ln1_b,
           w_router, b_router, w_gu, b_gu, w_down, b_down, ln2_g, ln2_b):
    batch, seq, d = x.shape
    tokens = batch * seq
    h = x.reshape(tokens, d)
    tables = {dilation: _rope_lane_table(seq, dilation) for _, dilation in DIL_PAIRS}
    for l in range(DEPTH):
        qk_a, vt_a = _project_qkv(h, w_in[l], 0, tables[1], seq, "qkv_diff",
                                  q_scale=HEAD_DIM ** -0.5 * math.log2(math.e), v_feature_major=True)
        ya = _diff_attention(qk_a.reshape(batch, seq, 2 * SEG_W), vt_a, lam_q1[l], lam_k1[l], lam_q2[l],
                             lam_k2[l], subln_g[l]).reshape(tokens, -1)
        dil = []
        for g, (_, dilation) in enumerate(DIL_PAIRS):
            if dilation == 1:
                h_g = h
            else:
                h_g = (h.reshape(batch, seq // dilation, dilation, d).swapaxes(1, 2).reshape(tokens, d)
                       .astype(BF16))
            qkv_g = _project_qkv(h_g, w_in[l], g + 1, tables[dilation], seq, f"qkv_dil{g}",
                                 q_scale=HEAD_DIM ** -0.5)
            dil.append(_dilated_group(qkv_g, batch, seq, g))
        h1, h1p, topi, topw, rank, cnt = _merge(ya, dil, h, seq, w_in[l].astype(BF16), b_gate[l], w_oa[l].astype(BF16),
                                                w_ob[l].astype(BF16), w_o[l].astype(BF16), ln1_g[l], ln1_b[l],
                                                w_router[l], b_router[l])
        dest, block_e, next_e, n_used, src_tok, dst_row = _routing_tables(topi, rank, cnt, tokens)
        xrows = _sc_scatter_rows(h1p, src_tok, dst_row)
        yrows = _experts(xrows, block_e, next_e, n_used, w_gu[l], b_gu[l], w_down[l], b_down[l])
        yplanes = _sc_gather_rows(yrows, dest.T.reshape(-1))
        h = _combine(yplanes, topw, h1, ln2_g[l], ln2_b[l])
    return h.reshape(batch, seq, d)
```

```python
import functools
import math

import jax
import jax.numpy as jnp
from jax import lax
from jax.experimental import pallas as pl
from jax.experimental.pallas import tpu as pltpu
from jax.experimental.pallas import tpu_sc as plsc

F32 = jnp.float32
BF16 = jnp.bfloat16
U32 = jnp.uint32

D_MODEL = 1024
HEAD_DIM = 64
ROT_DIM = HEAD_DIM // 4
ROPE_THETA = 500000.0
QBLK = 128
DA_HEADS = 4
DIL_PAIRS = ((128, 1), (512, 4), (2048, 16))
SEG_W = 512
QKV_W = 3 * SEG_W
N_EXPERTS = 32
TOP_K = 4
D_FF = D_MODEL
SWIGLU_ALPHA = 1.702
SWIGLU_LIMIT = 7.0
DEPTH = 1
DN_ALPHA = (2 * DEPTH) ** 0.25
EPS = 1e-5
LAMBDA_INIT = 0.8 - 0.6 * math.exp(-0.3 * 0)

LANES = 128
QKV_TM = 512
QKV_CHUNK = 256
ATT_TQ = 256
ONES_ROWS = 16
DIL_NQ = 4
MERGE_TM = 512
MOE_BM = 256
MOE_TM = 256
SC_CHUNK = 128
VMEM_LIMIT = 52 * 1024 * 1024


def _cparams(n_axes):
    return pltpu.CompilerParams(dimension_semantics=("arbitrary",) * n_axes,
                                vmem_limit_bytes=VMEM_LIMIT)


def _qkv_kernel(x_ref, w_ref, cs_ref, o_ref, *rest, q_scale):
    *maybe_vt_ref, wb_ref = rest

    @pl.when(pl.program_id(0) == 0)
    def _():
        wb_ref[...] = w_ref[...].astype(BF16)

    tm = x_ref.shape[0]
    half = ROT_DIM // 2
    in_head = lax.broadcasted_iota(jnp.int32, (QKV_CHUNK, LANES), 1) % HEAD_DIM
    first, second = in_head < half, (in_head >= half) & (in_head < ROT_DIM)
    for rc in range(tm // QKV_CHUNK):
        rows = slice(rc * QKV_CHUNK, (rc + 1) * QKV_CHUNK)
        xb = x_ref[rows, :].astype(BF16)
        cs = cs_ref[rows, :]
        c = jnp.where(first, cs, jnp.where(second, pltpu.roll(cs, half, 1), 1.0))
        s1 = jnp.where(second, cs, 0.0)
        s2 = jnp.where(first, -pltpu.roll(cs, LANES - half, 1), 0.0)
        for seg in range(3):
            acc = jnp.dot(xb, wb_ref[:, seg * SEG_W:(seg + 1) * SEG_W], preferred_element_type=F32)
            if seg == 2:
                if maybe_vt_ref:
                    maybe_vt_ref[0][:, rows] = acc.T.astype(BF16)
                else:
                    o_ref[rows, seg * SEG_W:(seg + 1) * SEG_W] = acc.astype(BF16)
                continue
            for k in range(SEG_W // LANES):
                t = acc[:, k * LANES:(k + 1) * LANES]
                r = t * c + pltpu.roll(t, ROT_DIM // 2, 1) * s1 + pltpu.roll(t, LANES - ROT_DIM // 2, 1) * s2
                if seg == 0:
                    r = r * q_scale
                lo = seg * SEG_W + k * LANES
                o_ref[rows, lo:lo + LANES] = r.astype(BF16)


def _rope_lane_table(seq, dil):
    row = jnp.arange(seq, dtype=jnp.int32)
    length = seq // dil
    pos = ((row % length) * dil + row // length).astype(F32)
    inv_freq = ROPE_THETA ** (-jnp.arange(0, ROT_DIM, 2, dtype=F32) / ROT_DIM)
    ang = pos[:, None] * inv_freq[None, :]
    head = jnp.concatenate([jnp.cos(ang), jnp.sin(ang), jnp.zeros((seq, HEAD_DIM - ROT_DIM), F32)], axis=1)
    return jnp.tile(head, (1, LANES // HEAD_DIM))


def _project_qkv(xb, w_b, col_block, table, seq, name, q_scale, v_feature_major=False):
    tokens = xb.shape[0]
    tm = min(QKV_TM, seq)
    per_seq = seq // tm
    tab_spec = pl.BlockSpec((tm, LANES), lambda i: (i % per_seq, 0))
    out_w = 2 * SEG_W if v_feature_major else QKV_W
    out_specs = [pl.BlockSpec((tm, out_w), lambda i: (i, 0))]
    out_shape = [jax.ShapeDtypeStruct((tokens, out_w), BF16)]
    if v_feature_major:
        out_specs.append(pl.BlockSpec((None, SEG_W, tm), lambda i: (i // per_seq, 0, i % per_seq)))
        out_shape.append(jax.ShapeDtypeStruct((tokens // seq, SEG_W, seq), BF16))
    outs = pl.pallas_call(
        functools.partial(_qkv_kernel, q_scale=q_scale),
        grid=(tokens // tm,),
        in_specs=[pl.BlockSpec((tm, D_MODEL), lambda i: (i, 0)),
                  pl.BlockSpec((D_MODEL, QKV_W), lambda i: (0, col_block)),
                  tab_spec],
        out_specs=out_specs,
        out_shape=out_shape,
        scratch_shapes=[pltpu.VMEM((D_MODEL, QKV_W), BF16)],
        compiler_params=_cparams(1),
        name=name,
    )(xb, w_b, table)
    return outs if v_feature_major else outs[0]


def _nt_dot(a, b):
    return lax.dot_general(a, b, (((1,), (1,)), ((), ())), preferred_element_type=F32)


def _diff_kernel(q_ref, k_ref, vt_ref, lq1_ref, lk1_ref, lq2_ref, lk2_ref, g_ref, o_ref, s_ref, acc_ref, *, tq):
    i = pl.program_id(1)
    chains = 2 * DA_HEADS
    half = tq // 2
    lane = lax.broadcasted_iota(jnp.int32, (tq, LANES), 1)
    qs = []
    for h in range(DA_HEADS):
        q = q_ref[:, h * LANES:(h + 1) * LANES]
        zero = jnp.zeros_like(q)
        qs += [jnp.where(lane < HEAD_DIM, q, zero), jnp.where(lane >= HEAD_DIM, q, zero)]
    acc_ref[...] = jnp.zeros_like(acc_ref)

    def score_chain(j, buf, c):
        start = pl.multiple_of(j * tq, tq)
        kb = k_ref[pl.ds(start, tq), (c // 2) * LANES:(c // 2 + 1) * LANES]
        buf[c] = _nt_dot(kb, qs[c])

    def score_block(j, buf):
        for c in range(chains):
            score_chain(j, buf, c)

    def absorb(j, buf, state, masked, nxt=None):
        start = pl.multiple_of(j * tq, tq)
        out = []
        for c in range(chains):
            if nxt is not None:
                score_chain(j + 1, nxt, c)
            m, l = state[c]
            if masked:
                key = lax.broadcasted_iota(jnp.int32, (tq, tq), 0)
                qry = lax.broadcasted_iota(jnp.int32, (tq, tq), 1)
                buf[c] = jnp.where(key <= qry, buf[c], -jnp.inf)
            m_new = jnp.maximum(m, jnp.max(buf[c], axis=0, keepdims=True))
            a = jnp.exp2(m - m_new)
            p = [jnp.exp2((buf[c, u * half:(u + 1) * half, :] - m_new).astype(BF16)) for u in range(2)]
            vt = vt_ref[(c // 2) * LANES:(c // 2 + 1) * LANES, pl.ds(start, tq)]
            vt1 = jnp.concatenate([vt, jnp.ones((ONES_ROWS, tq), BF16)], axis=0)
            pv = jnp.dot(vt1, jnp.concatenate(p, axis=0), preferred_element_type=F32)
            l = a * l + pv[LANES:LANES + 1, :]
            acc_ref[c] = a * acc_ref[c] + pv[:LANES, :]
            out.append((m_new, l))
        return tuple(out)

    buf_a, buf_b = s_ref.at[0], s_ref.at[1]

    def two_blocks(jj, state):
        j = 2 * jj
        state = absorb(j, buf_a, state, False, nxt=buf_b)
        return absorb(j + 1, buf_b, state, False, nxt=buf_a)

    def last_from_a(state):
        return absorb(i, buf_a, state, True)

    def last_from_b(state):
        state = absorb(i - 1, buf_a, state, False, nxt=buf_b)
        return absorb(i, buf_b, state, True)

    init = tuple((jnp.full((1, tq), -jnp.inf, F32), jnp.zeros((1, tq), F32)) for _ in range(chains))
    score_block(0, buf_a)
    state = lax.fori_loop(0, i // 2, two_blocks, init)
    state = lax.cond(lax.rem(i, 2) == 0, last_from_a, last_from_b, state)

    lam = (jnp.exp(jnp.sum(lq1_ref[...] * lk1_ref[...], axis=1, keepdims=True))
           - jnp.exp(jnp.sum(lq2_ref[...] * lk2_ref[...], axis=1, keepdims=True)) + LAMBDA_INIT)
    for h in range(DA_HEADS):
        l1, l2 = state[2 * h][1], state[2 * h + 1][1]
        o = (acc_ref[2 * h] / l1 - lam * (acc_ref[2 * h + 1] / l2)).T
        ms = jnp.mean(o * o, axis=1, keepdims=True)
        o = o * lax.rsqrt(ms + EPS) * g_ref[...]
        o_ref[:, h * LANES:(h + 1) * LANES] = (o * (1.0 - LAMBDA_INIT)).astype(BF16)


def _diff_attention(qk3, vt3, lam_q1, lam_k1, lam_q2, lam_k2, subln_g):
    b, s, _ = qk3.shape
    tq = min(ATT_TQ, s)
    vec = lambda n: pl.BlockSpec((1, n), lambda bb, i: (0, 0))
    return pl.pallas_call(
        functools.partial(_diff_kernel, tq=tq),
        grid=(b, s // tq),
        in_specs=[
            pl.BlockSpec((None, tq, SEG_W), lambda bb, i: (bb, i, 0)),
            pl.BlockSpec((None, s, SEG_W), lambda bb, i: (bb, 0, 1)),
            pl.BlockSpec((None, SEG_W, s), lambda bb, i: (bb, 0, 0)),
            vec(HEAD_DIM), vec(HEAD_DIM), vec(HEAD_DIM), vec(HEAD_DIM), vec(2 * HEAD_DIM),
        ],
        out_specs=pl.BlockSpec((None, tq, SEG_W), lambda bb, i: (bb, i, 0)),
        out_shape=jax.ShapeDtypeStruct((b, s, SEG_W), BF16),
        scratch_shapes=[pltpu.VMEM((2, 2 * DA_HEADS, tq, tq), F32), pltpu.VMEM((2 * DA_HEADS, LANES, tq), F32)],
        compiler_params=_cparams(2),
        name="diff_attn",
    )(qk3, qk3, vt3, lam_q1.reshape(1, -1), lam_k1.reshape(1, -1), lam_q2.reshape(1, -1),
      lam_k2.reshape(1, -1), subln_g.reshape(1, -1))


def _dil_kernel(q_ref, kp_ref, kc_ref, vp_ref, vc_ref, o_ref, lse_ref, *, rel, nq):
    n = pl.program_id(2)
    qi = lax.broadcasted_iota(jnp.int32, (QBLK, 2 * QBLK), 0)
    kj = lax.broadcasted_iota(jnp.int32, (QBLK, 2 * QBLK), 1)
    dist = qi + QBLK - kj
    band = (dist >= 0) & (dist <= rel)
    lane = lax.broadcasted_iota(jnp.int32, (QBLK, LANES), 1)
    low = lane < HEAD_DIM
    slabs = [slice(p * LANES, (p + 1) * LANES) for p in range(SEG_W // LANES)]

    def window(prev_ref, cur_ref, s, sl):
        if s == 0:
            return jnp.concatenate([prev_ref[:, sl], cur_ref[:QBLK, sl]], axis=0)
        return cur_ref[(s - 1) * QBLK:(s + 1) * QBLK, sl]

    def score_block(s):
        rows = slice(s * QBLK, (s + 1) * QBLK)
        scores = []
        for sl in slabs:
            q2 = q_ref[rows, sl]
            k2 = window(kp_ref, kc_ref, s, sl)
            for sel in (low, jnp.logical_not(low)):
                scores.append(_nt_dot(jnp.where(sel, q2, jnp.zeros_like(q2)), k2))
        return scores

    def finish_block(s, scores):
        rows = slice(s * QBLK, (s + 1) * QBLK)
        valid = band & ((kj >= QBLK) | (n * nq + s > 0))
        probs = []
        for sc in scores:
            sc = jnp.where(valid, sc, -jnp.inf)
            m = jnp.max(sc, axis=1, keepdims=True)
            probs.append((jnp.exp2((sc - m).astype(BF16)), m))
        one = jnp.ones((2 * QBLK, LANES), BF16)
        low_keys = lax.broadcasted_iota(jnp.int32, (2 * QBLK, LANES), 1) < HEAD_DIM
        for p, sl in enumerate(slabs):
            v2 = window(vp_ref, vc_ref, s, sl)
            (pe_lo, m_lo), (pe_hi, m_hi) = probs[2 * p:2 * p + 2]
            pv_lo = jnp.dot(pe_lo, jnp.where(low_keys, v2, one), preferred_element_type=F32)
            pv_hi = jnp.dot(pe_hi, jnp.where(low_keys, one, v2), preferred_element_type=F32)
            num = jnp.where(low, pv_lo, pv_hi)
            den = pltpu.roll(jnp.where(low, pv_hi, pv_lo), HEAD_DIM, 1)
            o_ref[rows, sl] = (num / den).astype(BF16)
            lse_ref[rows, sl] = jnp.where(low, m_lo, m_hi) + jnp.log2(den)

    pending = score_block(0)
    for s in range(1, nq):
        nxt = score_block(s)
        finish_block(s - 1, pending)
        pending = nxt
    finish_block(nq - 1, pending)


def _dilated_group(qkv, batch, seq, group):
    window, dil = DIL_PAIRS[group]
    rel = window // dil
    length = seq // dil
    nb = length // QBLK
    nq = math.gcd(nb, DIL_NQ)
    view = qkv.reshape(batch, dil, length, QKV_W)

    def spec(seg, prev):
        if prev:
            return pl.BlockSpec((None, None, QBLK, SEG_W), lambda b, r, n: (b, r, jnp.maximum(n * nq - 1, 0), seg))
        return pl.BlockSpec((None, None, nq * QBLK, SEG_W), lambda b, r, n: (b, r, n, seg))

    out_spec = pl.BlockSpec((None, None, nq * QBLK, SEG_W), lambda b, r, n: (b, r, n, 0))
    o, lse = pl.pallas_call(
        functools.partial(_dil_kernel, rel=rel, nq=nq),
        grid=(batch, dil, nb // nq),
        in_specs=[spec(0, False), spec(1, True), spec(1, False), spec(2, True), spec(2, False)],
        out_specs=[out_spec, out_spec],
        out_shape=[jax.ShapeDtypeStruct((batch, dil, length, SEG_W), BF16),
                   jax.ShapeDtypeStruct((batch, dil, length, SEG_W), F32)],
        compiler_params=_cparams(3),
        name=f"dilated{group}",
    )(view, view, view, view, view)
    return o, lse


def _layer_norm(z, g, b):
    mu = jnp.mean(z, axis=1, keepdims=True)
    zc = z - mu
    var = jnp.mean(zc * zc, axis=1, keepdims=True)
    return zc * lax.rsqrt(var + EPS) * g + b


def _pack_bf16_pairs(v):
    n = v.shape[1] // 2
    lo = pltpu.bitcast(v[:, :n].astype(BF16).astype(F32), U32)
    hi = pltpu.bitcast(v[:, n:].astype(BF16).astype(F32), U32)
    return jnp.bitwise_or(jnp.right_shift(lo, jnp.uint32(16)), jnp.bitwise_and(hi, jnp.uint32(0xFFFF0000)))


def _unpack_bf16_pairs(w):
    lo = pltpu.bitcast(jnp.left_shift(w, jnp.uint32(16)), F32)
    hi = pltpu.bitcast(jnp.bitwise_and(w, jnp.uint32(0xFFFF0000)), F32)
    return lo, hi


def _merge_kernel(ya_ref, o1_ref, o2_ref, o3_ref, l1_ref, l2_ref, l3_ref, x_ref,
                  wg_ref, bg_ref, woa_ref, wob_ref, wo_ref, g_ref, b_ref, wrh_ref, wrl_ref, br_ref,
                  h_ref, hp_ref, topi_ref, topw_ref, rank_ref, cnt_ref, carry_ref, scr_ref):
    i = pl.program_id(0)
    tm = x_ref.shape[0]

    @pl.when(i == 0)
    def _():
        carry_ref[...] = jnp.zeros_like(carry_ref)

    hm = tm // 2
    halves = (slice(0, hm), slice(hm, tm))
    xs = [x_ref[rows, :] for rows in halves]
    gate_dot = lambda x: jnp.dot(x.astype(BF16), wg_ref[...], preferred_element_type=F32) + bg_ref[...]
    gate_pre = [gate_dot(xs[0])]
    pas = [jnp.dot(ya_ref[rows, :], woa_ref[...], preferred_element_type=F32) for rows in halves]

    def token_major(ref, scr):
        dil, n = ref.shape[0], ref.shape[1]
        if dil == 1:
            return ref[0].astype(F32)
        for r in range(dil):
            blk = ref[r].astype(F32)
            for c in range(SEG_W // LANES):
                scr[c, pl.ds(r, n, stride=dil), :] = blk[:, c * LANES:(c + 1) * LANES]
        return jnp.concatenate([scr[c] for c in range(SEG_W // LANES)], axis=1)

    oa, ob, oc = (token_major(r, scr_ref.at[k]) for k, r in enumerate((o1_ref, o2_ref, o3_ref)))
    la, lb, lc = (token_major(r, scr_ref.at[3 + k]) for k, r in enumerate((l1_ref, l2_ref, l3_ref)))
    mx = jnp.maximum(jnp.maximum(la, lb), lc)
    ea, eb, ec = jnp.exp2(la - mx), jnp.exp2(lb - mx), jnp.exp2(lc - mx)
    yb = ((ea * oa + eb * ob + ec * oc) / (ea + eb + ec)).astype(BF16)
    pbs = [jnp.dot(yb[rows, :], wob_ref[...], preferred_element_type=F32) for rows in halves]
    gate_pre.append(gate_dot(xs[1]))
    mixes = []
    for pre_act, pa, pb in zip(gate_pre, pas, pbs):
        gates = jax.nn.sigmoid(pre_act)
        merged = gates[:, :D_MODEL] * pa + gates[:, D_MODEL:] * pb
        mixes.append(jnp.dot(merged.astype(BF16), wo_ref[...], preferred_element_type=F32))
    logits = []
    for rows, x, mix in zip(halves, xs, mixes):
        h = _layer_norm(DN_ALPHA * x + mix, g_ref[...], b_ref[...])
        h_ref[rows, :] = h
        hp_ref[rows, :] = _pack_bf16_pairs(h)
        hh = h.astype(BF16)
        hl = (h - hh.astype(F32)).astype(BF16)
        logits.append(jnp.dot(hh, wrh_ref[...], preferred_element_type=F32)
                      + jnp.dot(hl, wrh_ref[...], preferred_element_type=F32)
                      + jnp.dot(hh, wrl_ref[...], preferred_element_type=F32) + br_ref[...])

    lane = lax.broadcasted_iota(jnp.int32, (hm, LANES), 1)
    r_i = lax.broadcasted_iota(jnp.int32, (hm, hm), 0)
    c_i = lax.broadcasted_iota(jnp.int32, (hm, hm), 1)
    tri = (r_i > c_i).astype(BF16)
    before = carry_ref[...]
    for rows, lg in zip(halves, logits):
        lg = jnp.where(lane < N_EXPERTS, lg, -jnp.inf)
        vals, idxs = [], []
        for _ in range(TOP_K):
            mv = jnp.max(lg, axis=1, keepdims=True)
            ik = jnp.min(jnp.where(lg == mv, lane, LANES), axis=1, keepdims=True)
            vals.append(mv)
            idxs.append(ik)
            lg = jnp.where(lane == ik, -jnp.inf, lg)
        es = [jnp.exp(v - vals[0]) for v in vals]
        tot = es[0] + es[1] + es[2] + es[3]
        onehot = jnp.zeros((hm, LANES), F32)
        for ik in idxs:
            onehot = onehot + (lane == ik).astype(F32)
        pre = jnp.dot(tri, onehot.astype(BF16), preferred_element_type=F32) + before
        topi = jnp.zeros((hm, LANES), jnp.int32)
        topw = jnp.zeros((hm, LANES), F32)
        rank = jnp.zeros((hm, LANES), F32)
        for k in range(TOP_K):
            rk = jnp.sum(jnp.where(lane == idxs[k], pre, 0.0), axis=1, keepdims=True)
            topi = jnp.where(lane == k, idxs[k], topi)
            topw = jnp.where(lane == k, es[k] / tot, topw)
            rank = jnp.where(lane == k, rk, rank)
        topi_ref[rows, :] = topi
        topw_ref[rows, :] = topw
        rank_ref[rows, :] = rank.astype(jnp.int32)
        before = before + jnp.sum(onehot, axis=0, keepdims=True)
    carry_ref[...] = before
    cnt_ref[...] = before


def _merge(ya, dil_outs, x2, seq, w_b, b_gate, w_oa_b, w_ob_b, w_o_b, ln1_g, ln1_b, w_router, b_router):
    tokens = x2.shape[0]
    tm = MERGE_TM
    per_seq = seq // tm

    def group(a):
        dil = a.shape[1]
        return pl.BlockSpec((None, dil, tm // dil, SEG_W), lambda i: (i // per_seq, 0, i % per_seq, 0))

    gate_blk = w_b.shape[1] // (2 * D_MODEL) - 1
    wr = jnp.zeros((D_MODEL, LANES), F32).at[:, :N_EXPERTS].set(w_router)
    wr_hi = wr.astype(BF16)
    wr_lo = (wr - wr_hi.astype(F32)).astype(BF16)
    br = jnp.zeros((1, LANES), F32).at[0, :N_EXPERTS].set(b_router)
    row = lambda w: pl.BlockSpec((tm, w), lambda i: (i, 0))
    full = lambda a: pl.BlockSpec(a.shape, lambda i: (0,) * a.ndim)
    (o1, l1), (o2, l2), (o3, l3) = dil_outs
    bg = b_gate.reshape(1, -1)
    g1 = ln1_g.reshape(1, -1)
    b1 = ln1_b.reshape(1, -1)
    lane_out = lambda dt: jax.ShapeDtypeStruct((tokens, LANES), dt)
    return pl.pallas_call(
        _merge_kernel,
        grid=(tokens // tm,),
        in_specs=[row(SEG_W), group(o1), group(o2), group(o3), group(l1), group(l2), group(l3),
                  row(D_MODEL), pl.BlockSpec((D_MODEL, 2 * D_MODEL), lambda i: (0, gate_blk)),
                  full(bg), full(w_oa_b), full(w_ob_b), full(w_o_b), full(g1), full(b1),
                  full(wr_hi), full(wr_lo), full(br)],
        out_specs=[row(D_MODEL), row(D_MODEL // 2), row(LANES), row(LANES), row(LANES),
                   pl.BlockSpec((1, LANES), lambda i: (0, 0))],
        out_shape=[jax.ShapeDtypeStruct((tokens, D_MODEL), F32), jax.ShapeDtypeStruct((tokens, D_MODEL // 2), U32),
                   lane_out(jnp.int32), lane_out(F32), lane_out(jnp.int32), jax.ShapeDtypeStruct((1, LANES), F32)],
        scratch_shapes=[pltpu.VMEM((1, LANES), F32), pltpu.VMEM((6, SEG_W // LANES, tm, LANES), F32)],
        compiler_params=_cparams(1),
        name="merge",
    )(ya, o1, o2, o3, l1, l2, l3, x2, w_b, bg, w_oa_b, w_ob_b, w_o_b, g1, b1, wr_hi, wr_lo, br)


def _sc_mesh():
    return plsc.VectorSubcoreMesh(core_axis_name="core", subcore_axis_name="subcore")


def _sc_scatter_rows(table, src_idx, dst_idx):
    n = src_idx.shape[0]
    d = table.shape[1]
    mesh = _sc_mesh()
    workers = mesh.num_cores * mesh.num_subcores
    per = n // (SC_CHUNK * workers)
    assert per * SC_CHUNK * workers == n

    @pl.kernel(out_type=jax.ShapeDtypeStruct((n, d), table.dtype), mesh=mesh,
               scratch_types=[pltpu.VMEM((1, SC_CHUNK), jnp.int32), pltpu.VMEM((1, SC_CHUNK), jnp.int32),
                              pltpu.VMEM((SC_CHUNK, d), table.dtype)])
    def copy(t_hbm, s_hbm, d_hbm, o_hbm, s_vm, d_vm, buf):
        wid = lax.axis_index("core") * mesh.num_subcores + lax.axis_index("subcore")

        @pl.loop(0, per)
        def _(j):
            blk = wid * per + j
            pltpu.sync_copy(s_hbm.at[pl.ds(blk, 1)], s_vm)
            pltpu.sync_copy(d_hbm.at[pl.ds(blk, 1)], d_vm)
            pltpu.sync_copy(t_hbm.at[s_vm.at[0]], buf)
            pltpu.sync_copy(buf, o_hbm.at[d_vm.at[0]])

    return copy(table, src_idx.reshape(-1, SC_CHUNK), dst_idx.reshape(-1, SC_CHUNK))


def _sc_gather_rows(table, idx):
    n = idx.shape[0]
    d = table.shape[1]
    mesh = _sc_mesh()
    workers = mesh.num_cores * mesh.num_subcores
    per = n // (SC_CHUNK * workers)
    assert per * SC_CHUNK * workers == n

    @pl.kernel(out_type=jax.ShapeDtypeStruct((n, d), table.dtype), mesh=mesh,
               scratch_types=[pltpu.VMEM((1, SC_CHUNK), jnp.int32), pltpu.VMEM((SC_CHUNK, d), table.dtype)])
    def gather(t_hbm, i_hbm, o_hbm, i_vm, buf):
        wid = lax.axis_index("core") * mesh.num_subcores + lax.axis_index("subcore")

        @pl.loop(0, per)
        def _(j):
            blk = wid * per + j
            pltpu.sync_copy(i_hbm.at[pl.ds(blk, 1)], i_vm)
            pltpu.sync_copy(t_hbm.at[i_vm.at[0]], buf)
            pltpu.sync_copy(buf, o_hbm.at[pl.ds(blk * SC_CHUNK, SC_CHUNK)])

    return gather(table, idx.reshape(-1, SC_CHUNK))


def _expert_kernel(be_ref, nx_ref, nu_ref, x_ref, wgu_hbm, bgu_ref, wd_hbm, bd_ref, y_ref,
                   wgu_stage, wd_stage, wgu_b, wd_b, sem):
    i = pl.program_id(0)
    used = i < nu_ref[0]
    expert = be_ref[i]

    def weight_copies(e):
        return (pltpu.make_async_copy(wgu_hbm.at[e], wgu_stage, sem.at[0]),
                pltpu.make_async_copy(wd_hbm.at[e], wd_stage, sem.at[1]))

    @pl.when(i == 0)
    def _():
        for cp in weight_copies(expert):
            cp.start()

    @pl.when(used & ((i == 0) | (expert != be_ref[jnp.maximum(i - 1, 0)])))
    def _():
        for cp in weight_copies(expert):
            cp.wait()
        wgu_b[...] = wgu_stage[...].astype(BF16)
        wd_b[...] = wd_stage[...].astype(BF16)

        @pl.when(nx_ref[i] >= 0)
        def _():
            for cp in weight_copies(nx_ref[i]):
                cp.start()

    @pl.when(used)
    def _():
        lo, hi = _unpack_bf16_pairs(x_ref[...])
        xb = jnp.concatenate([lo, hi], axis=1).astype(BF16)
        chunk = 512
        cols = [slice(c * chunk, (c + 1) * chunk) for c in range(D_FF // chunk)]
        pre = []
        for gs in cols:
            us = slice(D_FF + gs.start, D_FF + gs.stop)
            gate = jnp.dot(xb, wgu_b[:, gs], preferred_element_type=F32) + bgu_ref[:, gs]
            up = jnp.dot(xb, wgu_b[:, us], preferred_element_type=F32) + bgu_ref[:, us]
            pre.append((gate, up))
        acc = jnp.zeros((x_ref.shape[0], D_MODEL), F32)
        for gs, (gate, up) in zip(cols, pre):
            gate = jnp.minimum(gate, SWIGLU_LIMIT)
            up = jnp.clip(up, -SWIGLU_LIMIT, SWIGLU_LIMIT)
            act = (up + 1.0) * gate * jax.nn.sigmoid(SWIGLU_ALPHA * gate)
            acc = acc + jnp.dot(act.astype(BF16), wd_b[gs, :], preferred_element_type=F32)
        y_ref[...] = _pack_bf16_pairs(acc + bd_ref[...])

    @pl.when(i >= nu_ref[0])
    def _():
        y_ref[...] = jnp.zeros_like(y_ref)


def _experts(xrows, block_e, next_e, n_used, w_gu, b_gu, w_down, b_down):
    n_rows = xrows.shape[0]
    bm = MOE_BM
    bias = lambda i, be, nx, nu: (be[i], 0, 0)
    return pl.pallas_call(
        _expert_kernel,
        grid_spec=pltpu.PrefetchScalarGridSpec(
            num_scalar_prefetch=3,
            grid=(n_rows // bm,),
            in_specs=[
                pl.BlockSpec((bm, D_MODEL // 2), lambda i, be, nx, nu: (i, 0)),
                pl.BlockSpec(memory_space=pl.ANY),
                pl.BlockSpec((None, 1, 2 * D_FF), bias),
                pl.BlockSpec(memory_space=pl.ANY),
                pl.BlockSpec((None, 1, D_MODEL), bias),
            ],
            out_specs=pl.BlockSpec((bm, D_MODEL // 2), lambda i, be, nx, nu: (i, 0)),
            scratch_shapes=[pltpu.VMEM((D_MODEL, 2 * D_FF), F32), pltpu.VMEM((D_FF, D_MODEL), F32),
                            pltpu.VMEM((D_MODEL, 2 * D_FF), BF16), pltpu.VMEM((D_FF, D_MODEL), BF16),
                            pltpu.SemaphoreType.DMA((2,))],
        ),
        out_shape=jax.ShapeDtypeStruct((n_rows, D_MODEL // 2), U32),
        compiler_params=_cparams(1),
        name="experts",
    )(block_e, next_e, n_used, xrows, w_gu, b_gu.reshape(N_EXPERTS, 1, -1), w_down, b_down.reshape(N_EXPERTS, 1, -1))


def _combine_kernel(y0_ref, y1_ref, y2_ref, y3_ref, w_ref, h_ref, g_ref, b_ref, o_ref):
    w = w_ref[...]
    half = D_MODEL // 2
    f_lo = jnp.zeros((h_ref.shape[0], half), F32)
    f_hi = jnp.zeros((h_ref.shape[0], half), F32)
    for k, y_ref in enumerate((y0_ref, y1_ref, y2_ref, y3_ref)):
        lo, hi = _unpack_bf16_pairs(y_ref[...])
        f_lo = f_lo + lo * w[:, k:k + 1]
        f_hi = f_hi + hi * w[:, k:k + 1]
    f = jnp.concatenate([f_lo, f_hi], axis=1)
    o_ref[...] = _layer_norm(DN_ALPHA * h_ref[...] + f, g_ref[...], b_ref[...])


def _combine(yplanes, topw, h, ln2_g, ln2_b):
    tokens = h.shape[0]
    tm = MOE_TM
    per_plane = tokens // tm
    plane = lambda k: pl.BlockSpec((tm, D_MODEL // 2), lambda i: (k * per_plane + i, 0))
    return pl.pallas_call(
        _combine_kernel,
        grid=(per_plane,),
        in_specs=[plane(0), plane(1), plane(2), plane(3),
                  pl.BlockSpec((tm, LANES), lambda i: (i, 0)),
                  pl.BlockSpec((tm, D_MODEL), lambda i: (i, 0)),
                  pl.BlockSpec((1, D_MODEL), lambda i: (0, 0)),
                  pl.BlockSpec((1, D_MODEL), lambda i: (0, 0))],
        out_specs=pl.BlockSpec((tm, D_MODEL), lambda i: (i, 0)),
        out_shape=jax.ShapeDtypeStruct((tokens, D_MODEL), F32),
        compiler_params=_cparams(1),
        name="combine",
    )(yplanes, yplanes, yplanes, yplanes, topw, h, ln2_g.reshape(1, -1), ln2_b.reshape(1, -1))


def _routing_tables(topi, rank, cnt, tokens):
    bm = MOE_BM
    i32 = jnp.int32
    experts = jnp.arange(N_EXPERTS, dtype=i32)
    counts = cnt[0, :N_EXPERTS].astype(i32)
    padded = (counts + bm - 1) // bm * bm
    pend = jnp.cumsum(padded)
    pstart = pend - padded
    sel = topi[:, :TOP_K, None] == experts[None, None, :]
    dest = rank[:, :TOP_K] + jnp.sum(jnp.where(sel, pstart[None, None, :], 0), axis=-1)
    n_pad = N_EXPERTS * bm
    n_rows = tokens * TOP_K + n_pad
    starts = jnp.arange(n_rows // bm, dtype=i32) * bm
    block_e = jnp.minimum(jnp.sum((pend[None, :] <= starts[:, None]).astype(i32), axis=1), N_EXPERTS - 1)
    n_used = (pend[-1] // bm).reshape(1)
    is_block_e = block_e[:, None] == experts[None, :]
    after = jnp.sum(jnp.where(is_block_e, pend[None, :], 0), axis=1) // bm
    e_after = jnp.sum(jnp.where(after[:, None] == jnp.arange(n_rows // bm, dtype=i32)[None, :], block_e[None, :], 0),
                      axis=1)
    next_e = jnp.where(after < n_used[0], e_after, -1)
    pad_cnt = padded - counts
    pad_end = jnp.cumsum(pad_cnt)
    j = jnp.arange(n_pad, dtype=i32)
    owner = jnp.sum((pad_end[None, :] <= j[:, None]).astype(i32), axis=1)
    is_owner = owner[:, None] == experts[None, :]
    pick = lambda v: jnp.sum(jnp.where(is_owner, v[None, :], 0), axis=1)
    in_expert = pick(pstart + counts) + (j - pick(pad_end - pad_cnt))
    pad_rows = jnp.where(owner < N_EXPERTS, in_expert, pend[-1] + (j - pad_end[-1]))
    src_tok = jnp.concatenate([jnp.repeat(jnp.arange(tokens, dtype=i32), TOP_K), j % tokens])
    dst_row = jnp.concatenate([dest.reshape(-1), pad_rows]).astype(i32)
    return dest.astype(i32), block_e.astype(i32), next_e.astype(i32), n_used.astype(i32), src_tok, dst_row


def kernel(x, w_in, b_gate, lam_q1, lam_k1, lam_q2, lam_k2, subln_g, w_oa, w_ob, w_o, ln1_g, ln1_b,
           w_router, b_router, w_gu, b_gu, w_down, b_down, ln2_g, ln2_b):
    batch, seq, d = x.shape
    tokens = batch * seq
    h = x.reshape(tokens, d)
    tables = {dilation: _rope_lane_table(seq, dilation) for _, dilation in DIL_PAIRS}
    for l in range(DEPTH):
        q_scale = HEAD_DIM ** -0.5 * math.log2(math.e)
        qk_a, vt_a = _project_qkv(h, w_in[l], 0, tables[1], seq, "qkv_diff", q_scale=q_scale, v_feature_major=True)
        ya = _diff_attention(qk_a.reshape(batch, seq, 2 * SEG_W), vt_a, lam_q1[l], lam_k1[l], lam_q2[l],
                             lam_k2[l], subln_g[l]).reshape(tokens, -1)
        dil = []
        for g, (_, dilation) in enumerate(DIL_PAIRS):
            if dilation == 1:
                h_g = h
            else:
                h_g = (h.reshape(batch, seq // dilation, dilation, d).swapaxes(1, 2).reshape(tokens, d)
                       .astype(BF16))
            qkv_g = _project_qkv(h_g, w_in[l], g + 1, tables[dilation], seq, f"qkv_dil{g}", q_scale=q_scale)
            dil.append(_dilated_group(qkv_g, batch, seq, g))
        h1, h1p, topi, topw, rank, cnt = _merge(ya, dil, h, seq, w_in[l].astype(BF16), b_gate[l], w_oa[l].astype(BF16),
                                                w_ob[l].astype(BF16), w_o[l].astype(BF16), ln1_g[l], ln1_b[l],
                                                w_router[l], b_router[l])
        dest, block_e, next_e, n_used, src_tok, dst_row = _routing_tables(topi, rank, cnt, tokens)
        xrows = _sc_scatter_rows(h1p, src_tok, dst_row)
        yrows = _experts(xrows, block_e, next_e, n_used, w_gu[l], b_gu[l], w_down[l], b_down[l])
        yplanes = _sc_gather_rows(yrows, dest.T.reshape(-1))
        h = _combine(yplanes, topw, h1, ln2_g[l], ln2_b[l])
    return h.reshape(batch, seq, d)
```

```python
import functools
import math

import jax
import jax.numpy as jnp
from jax import lax
from jax.experimental import pallas as pl
from jax.experimental.pallas import tpu as pltpu
from jax.experimental.pallas import tpu_sc as plsc

F32 = jnp.float32
BF16 = jnp.bfloat16
U32 = jnp.uint32

D_MODEL = 1024
HEAD_DIM = 64
ROT_DIM = HEAD_DIM // 4
ROPE_THETA = 500000.0
QBLK = 128
DA_HEADS = 4
DIL_PAIRS = ((128, 1), (512, 4), (2048, 16))
SEG_W = 512
QKV_W = 3 * SEG_W
N_EXPERTS = 32
TOP_K = 4
D_FF = D_MODEL
SWIGLU_ALPHA = 1.702
SWIGLU_LIMIT = 7.0
DEPTH = 1
DN_ALPHA = (2 * DEPTH) ** 0.25
EPS = 1e-5
LAMBDA_INIT = 0.8 - 0.6 * math.exp(-0.3 * 0)

LANES = 128
QKV_TM = 512
QKV_CHUNK = 256
ATT_TQ = 256
ONES_ROWS = 16
DIL_NQ = 4
MERGE_TM = 512
MOE_BM = 256
MOE_TM = 256
SC_CHUNK = 128
VMEM_LIMIT = 52 * 1024 * 1024


def _cparams(n_axes):
    return pltpu.CompilerParams(dimension_semantics=("arbitrary",) * n_axes,
                                vmem_limit_bytes=VMEM_LIMIT)


def _qkv_kernel(x_ref, w_ref, cs_ref, o_ref, *rest, q_scale):
    *maybe_vt_ref, wb_ref = rest

    @pl.when(pl.program_id(0) == 0)
    def _():
        wb_ref[...] = w_ref[...].astype(BF16)

    tm = x_ref.shape[0]
    half = ROT_DIM // 2
    in_head = lax.broadcasted_iota(jnp.int32, (QKV_CHUNK, LANES), 1) % HEAD_DIM
    first, second = in_head < half, (in_head >= half) & (in_head < ROT_DIM)
    for rc in range(tm // QKV_CHUNK):
        rows = slice(rc * QKV_CHUNK, (rc + 1) * QKV_CHUNK)
        xb = x_ref[rows, :].astype(BF16)
        cs = cs_ref[rows, :]
        c = jnp.where(first, cs, jnp.where(second, pltpu.roll(cs, half, 1), 1.0))
        s1 = jnp.where(second, cs, 0.0)
        s2 = jnp.where(first, -pltpu.roll(cs, LANES - half, 1), 0.0)
        for seg in range(3):
            acc = jnp.dot(xb, wb_ref[:, seg * SEG_W:(seg + 1) * SEG_W], preferred_element_type=F32)
            if seg == 2:
                if maybe_vt_ref:
                    maybe_vt_ref[0][:, rows] = acc.T.astype(BF16)
                else:
                    o_ref[rows, seg * SEG_W:(seg + 1) * SEG_W] = acc.astype(BF16)
                continue
            for k in range(SEG_W // LANES):
                t = acc[:, k * LANES:(k + 1) * LANES]
                r = t * c + pltpu.roll(t, ROT_DIM // 2, 1) * s1 + pltpu.roll(t, LANES - ROT_DIM // 2, 1) * s2
                if seg == 0:
                    r = r * q_scale
                lo = seg * SEG_W + k * LANES
                o_ref[rows, lo:lo + LANES] = r.astype(BF16)


def _rope_lane_table(seq, dil):
    row = jnp.arange(seq, dtype=jnp.int32)
    length = seq // dil
    pos = ((row % length) * dil + row // length).astype(F32)
    inv_freq = ROPE_THETA ** (-jnp.arange(0, ROT_DIM, 2, dtype=F32) / ROT_DIM)
    ang = pos[:, None] * inv_freq[None, :]
    head = jnp.concatenate([jnp.cos(ang), jnp.sin(ang), jnp.zeros((seq, HEAD_DIM - ROT_DIM), F32)], axis=1)
    return jnp.tile(head, (1, LANES // HEAD_DIM))


def _project_qkv(xb, w_b, col_block, table, seq, name, q_scale, v_feature_major=False):
    tokens = xb.shape[0]
    tm = min(QKV_TM, seq)
    per_seq = seq // tm
    tab_spec = pl.BlockSpec((tm, LANES), lambda i: (i % per_seq, 0))
    out_w = 2 * SEG_W if v_feature_major else QKV_W
    out_specs = [pl.BlockSpec((tm, out_w), lambda i: (i, 0))]
    out_shape = [jax.ShapeDtypeStruct((tokens, out_w), BF16)]
    if v_feature_major:
        out_specs.append(pl.BlockSpec((None, SEG_W, tm), lambda i: (i // per_seq, 0, i % per_seq)))
        out_shape.append(jax.ShapeDtypeStruct((tokens // seq, SEG_W, seq), BF16))
    outs = pl.pallas_call(
        functools.partial(_qkv_kernel, q_scale=q_scale),
        grid=(tokens // tm,),
        in_specs=[pl.BlockSpec((tm, D_MODEL), lambda i: (i, 0)),
                  pl.BlockSpec((D_MODEL, QKV_W), lambda i: (0, col_block)),
                  tab_spec],
        out_specs=out_specs,
        out_shape=out_shape,
        scratch_shapes=[pltpu.VMEM((D_MODEL, QKV_W), BF16)],
        compiler_params=_cparams(1),
        name=name,
    )(xb, w_b, table)
    return outs if v_feature_major else outs[0]


def _nt_dot(a, b):
    return lax.dot_general(a, b, (((1,), (1,)), ((), ())), preferred_element_type=F32)


def _diff_kernel(q_ref, k_ref, vt_ref, lq1_ref, lk1_ref, lq2_ref, lk2_ref, g_ref, o_ref, s_ref, acc_ref, *, tq):
    i = pl.program_id(1)
    chains = 2 * DA_HEADS
    half = tq // 2
    lane = lax.broadcasted_iota(jnp.int32, (tq, LANES), 1)
    qs = []
    for h in range(DA_HEADS):
        q = q_ref[:, h * LANES:(h + 1) * LANES]
        zero = jnp.zeros_like(q)
        qs += [jnp.where(lane < HEAD_DIM, q, zero), jnp.where(lane >= HEAD_DIM, q, zero)]
    acc_ref[...] = jnp.zeros_like(acc_ref)

    def score_chain(j, buf, c):
        start = pl.multiple_of(j * tq, tq)
        kb = k_ref[pl.ds(start, tq), (c // 2) * LANES:(c // 2 + 1) * LANES]
        buf[c] = _nt_dot(kb, qs[c])

    def score_block(j, buf):
        for c in range(chains):
            score_chain(j, buf, c)

    def absorb(j, buf, state, masked, nxt=None):
        start = pl.multiple_of(j * tq, tq)
        out = []
        for c in range(chains):
            if nxt is not None:
                score_chain(j + 1, nxt, c)
            m, l = state[c]
            if masked:
                key = lax.broadcasted_iota(jnp.int32, (tq, tq), 0)
                qry = lax.broadcasted_iota(jnp.int32, (tq, tq), 1)
                buf[c] = jnp.where(key <= qry, buf[c], -jnp.inf)
            m_new = jnp.maximum(m, jnp.max(buf[c], axis=0, keepdims=True))
            a = jnp.exp2(m - m_new)
            p = [jnp.exp2((buf[c, u * half:(u + 1) * half, :] - m_new).astype(BF16)) for u in range(2)]
            vt = vt_ref[(c // 2) * LANES:(c // 2 + 1) * LANES, pl.ds(start, tq)]
            vt1 = jnp.concatenate([vt, jnp.ones((ONES_ROWS, tq), BF16)], axis=0)
            pv = jnp.dot(vt1, jnp.concatenate(p, axis=0), preferred_element_type=F32)
            l = a * l + pv[LANES:LANES + 1, :]
            acc_ref[c] = a * acc_ref[c] + pv[:LANES, :]
            out.append((m_new, l))
        return tuple(out)

    buf_a, buf_b = s_ref.at[0], s_ref.at[1]

    def two_blocks(jj, state):
        j = 2 * jj
        state = absorb(j, buf_a, state, False, nxt=buf_b)
        return absorb(j + 1, buf_b, state, False, nxt=buf_a)

    def last_from_a(state):
        return absorb(i, buf_a, state, True)

    def last_from_b(state):
        state = absorb(i - 1, buf_a, state, False, nxt=buf_b)
        return absorb(i, buf_b, state, True)

    init = tuple((jnp.full((1, tq), -jnp.inf, F32), jnp.zeros((1, tq), F32)) for _ in range(chains))
    score_block(0, buf_a)
    state = lax.fori_loop(0, i // 2, two_blocks, init)
    state = lax.cond(lax.rem(i, 2) == 0, last_from_a, last_from_b, state)

    lam = (jnp.exp(jnp.sum(lq1_ref[...] * lk1_ref[...], axis=1, keepdims=True))
           - jnp.exp(jnp.sum(lq2_ref[...] * lk2_ref[...], axis=1, keepdims=True)) + LAMBDA_INIT)
    for h in range(DA_HEADS):
        l1, l2 = state[2 * h][1], state[2 * h + 1][1]
        o = (acc_ref[2 * h] / l1 - lam * (acc_ref[2 * h + 1] / l2)).T
        ms = jnp.mean(o * o, axis=1, keepdims=True)
        o = o * lax.rsqrt(ms + EPS) * g_ref[...]
        o_ref[:, h * LANES:(h + 1) * LANES] = (o * (1.0 - LAMBDA_INIT)).astype(BF16)


def _diff_attention(qk3, vt3, lam_q1, lam_k1, lam_q2, lam_k2, subln_g):
    b, s, _ = qk3.shape
    tq = min(ATT_TQ, s)
    vec = lambda n: pl.BlockSpec((1, n), lambda bb, i: (0, 0))
    return pl.pallas_call(
        functools.partial(_diff_kernel, tq=tq),
        grid=(b, s // tq),
        in_specs=[
            pl.BlockSpec((None, tq, SEG_W), lambda bb, i: (bb, i, 0)),
            pl.BlockSpec((None, s, SEG_W), lambda bb, i: (bb, 0, 1)),
            pl.BlockSpec((None, SEG_W, s), lambda bb, i: (bb, 0, 0)),
            vec(HEAD_DIM), vec(HEAD_DIM), vec(HEAD_DIM), vec(HEAD_DIM), vec(2 * HEAD_DIM),
        ],
        out_specs=pl.BlockSpec((None, tq, SEG_W), lambda bb, i: (bb, i, 0)),
        out_shape=jax.ShapeDtypeStruct((b, s, SEG_W), BF16),
        scratch_shapes=[pltpu.VMEM((2, 2 * DA_HEADS, tq, tq), F32), pltpu.VMEM((2 * DA_HEADS, LANES, tq), F32)],
        compiler_params=_cparams(2),
        name="diff_attn",
    )(qk3, qk3, vt3, lam_q1.reshape(1, -1), lam_k1.reshape(1, -1), lam_q2.reshape(1, -1),
      lam_k2.reshape(1, -1), subln_g.reshape(1, -1))


def _dil_kernel(q_ref, kp_ref, kc_ref, vp_ref, vc_ref, o_ref, lse_ref, *, rel, nq):
    n = pl.program_id(2)
    qi = lax.broadcasted_iota(jnp.int32, (QBLK, 2 * QBLK), 0)
    kj = lax.broadcasted_iota(jnp.int32, (QBLK, 2 * QBLK), 1)
    dist = qi + QBLK - kj
    band = (dist >= 0) & (dist <= rel)
    lane = lax.broadcasted_iota(jnp.int32, (QBLK, LANES), 1)
    low = lane < HEAD_DIM
    slabs = [slice(p * LANES, (p + 1) * LANES) for p in range(SEG_W // LANES)]

    def window(prev_ref, cur_ref, s, sl):
        if s == 0:
            return jnp.concatenate([prev_ref[:, sl], cur_ref[:QBLK, sl]], axis=0)
        return cur_ref[(s - 1) * QBLK:(s + 1) * QBLK, sl]

    def score_block(s):
        rows = slice(s * QBLK, (s + 1) * QBLK)
        scores = []
        for sl in slabs:
            q2 = q_ref[rows, sl]
            k2 = window(kp_ref, kc_ref, s, sl)
            for sel in (low, jnp.logical_not(low)):
                scores.append(_nt_dot(jnp.where(sel, q2, jnp.zeros_like(q2)), k2))
        return scores

    def finish_block(s, scores):
        rows = slice(s * QBLK, (s + 1) * QBLK)
        valid = band & ((kj >= QBLK) | (n * nq + s > 0))
        probs = []
        for sc in scores:
            sc = jnp.where(valid, sc, -jnp.inf)
            m = jnp.max(sc, axis=1, keepdims=True)
            probs.append((jnp.exp2((sc - m).astype(BF16)), m))
        one = jnp.ones((2 * QBLK, LANES), BF16)
        low_keys = lax.broadcasted_iota(jnp.int32, (2 * QBLK, LANES), 1) < HEAD_DIM
        for p, sl in enumerate(slabs):
            v2 = window(vp_ref, vc_ref, s, sl)
            (pe_lo, m_lo), (pe_hi, m_hi) = probs[2 * p:2 * p + 2]
            pv_lo = jnp.dot(pe_lo, jnp.where(low_keys, v2, one), preferred_element_type=F32)
            pv_hi = jnp.dot(pe_hi, jnp.where(low_keys, one, v2), preferred_element_type=F32)
            num = jnp.where(low, pv_lo, pv_hi)
            den = pltpu.roll(jnp.where(low, pv_hi, pv_lo), HEAD_DIM, 1)
            o_ref[rows, sl] = (num / den).astype(BF16)
            lse_ref[rows, sl] = jnp.where(low, m_lo, m_hi) + jnp.log2(den)

    pending = score_block(0)
    for s in range(1, nq):
        nxt = score_block(s)
        finish_block(s - 1, pending)
        pending = nxt
    finish_block(nq - 1, pending)


def _dilated_group(qkv, batch, seq, group):
    window, dil = DIL_PAIRS[group]
    rel = window // dil
    length = seq // dil
    nb = length // QBLK
    nq = math.gcd(nb, DIL_NQ)
    view = qkv.reshape(batch, dil, length, QKV_W)

    def spec(seg, prev):
        if prev:
            return pl.BlockSpec((None, None, QBLK, SEG_W), lambda b, r, n: (b, r, jnp.maximum(n * nq - 1, 0), seg))
        return pl.BlockSpec((None, None, nq * QBLK, SEG_W), lambda b, r, n: (b, r, n, seg))

    out_spec = pl.BlockSpec((None, None, nq * QBLK, SEG_W), lambda b, r, n: (b, r, n, 0))
    o, lse = pl.pallas_call(
        functools.partial(_dil_kernel, rel=rel, nq=nq),
        grid=(batch, dil, nb // nq),
        in_specs=[spec(0, False), spec(1, True), spec(1, False), spec(2, True), spec(2, False)],
        out_specs=[out_spec, out_spec],
        out_shape=[jax.ShapeDtypeStruct((batch, dil, length, SEG_W), BF16),
                   jax.ShapeDtypeStruct((batch, dil, length, SEG_W), F32)],
        compiler_params=_cparams(3),
        name=f"dilated{group}",
    )(view, view, view, view, view)
    return o, lse


def _layer_norm(z, g, b):
    mu = jnp.mean(z, axis=1, keepdims=True)
    zc = z - mu
    var = jnp.mean(zc * zc, axis=1, keepdims=True)
    return zc * lax.rsqrt(var + EPS) * g + b


def _pack_bf16_pairs(v):
    n = v.shape[1] // 2
    lo = pltpu.bitcast(v[:, :n].astype(BF16).astype(F32), U32)
    hi = pltpu.bitcast(v[:, n:].astype(BF16).astype(F32), U32)
    return jnp.bitwise_or(jnp.right_shift(lo, jnp.uint32(16)), jnp.bitwise_and(hi, jnp.uint32(0xFFFF0000)))


def _unpack_bf16_pairs(w):
    lo = pltpu.bitcast(jnp.left_shift(w, jnp.uint32(16)), F32)
    hi = pltpu.bitcast(jnp.bitwise_and(w, jnp.uint32(0xFFFF0000)), F32)
    return lo, hi


def _merge_kernel(ya_ref, o1_ref, o2_ref, o3_ref, l1_ref, l2_ref, l3_ref, x_ref,
                  wg_ref, bg_ref, woa_ref, wob_ref, wo_ref, g_ref, b_ref, wrh_ref, wrl_ref, br_ref,
                  h_ref, hp_ref, topi_ref, topw_ref, rank_ref, cnt_ref, carry_ref, scr_ref, wgb_ref):
    i = pl.program_id(0)
    tm = x_ref.shape[0]

    @pl.when(i == 0)
    def _():
        carry_ref[...] = jnp.zeros_like(carry_ref)
        wgb_ref[...] = wg_ref[...].astype(BF16)

    hm = tm // 2
    halves = (slice(0, hm), slice(hm, tm))
    xs = [x_ref[rows, :] for rows in halves]
    gate_dot = lambda x: jnp.dot(x.astype(BF16), wgb_ref[...], preferred_element_type=F32) + bg_ref[...]
    gate_pre = [gate_dot(xs[0])]
    pas = [jnp.dot(ya_ref[rows, :], woa_ref[...], preferred_element_type=F32) for rows in halves]

    def token_major(ref, scr):
        dil, n = ref.shape[0], ref.shape[1]
        if dil == 1:
            return ref[0].astype(F32)
        for r in range(dil):
            blk = ref[r].astype(F32)
            for c in range(SEG_W // LANES):
                scr[c, pl.ds(r, n, stride=dil), :] = blk[:, c * LANES:(c + 1) * LANES]
        return jnp.concatenate([scr[c] for c in range(SEG_W // LANES)], axis=1)

    oa, ob, oc = (token_major(r, scr_ref.at[k]) for k, r in enumerate((o1_ref, o2_ref, o3_ref)))
    la, lb, lc = (token_major(r, scr_ref.at[3 + k]) for k, r in enumerate((l1_ref, l2_ref, l3_ref)))
    mx = jnp.maximum(jnp.maximum(la, lb), lc)
    ea, eb, ec = jnp.exp2(la - mx), jnp.exp2(lb - mx), jnp.exp2(lc - mx)
    yb = ((ea * oa + eb * ob + ec * oc) / (ea + eb + ec)).astype(BF16)
    pbs = [jnp.dot(yb[rows, :], wob_ref[...], preferred_element_type=F32) for rows in halves]
    gate_pre.append(gate_dot(xs[1]))
    mixes = []
    for pre_act, pa, pb in zip(gate_pre, pas, pbs):
        gates = jax.nn.sigmoid(pre_act)
        merged = gates[:, :D_MODEL] * pa + gates[:, D_MODEL:] * pb
        mixes.append(jnp.dot(merged.astype(BF16), wo_ref[...], preferred_element_type=F32))
    logits = []
    for rows, x, mix in zip(halves, xs, mixes):
        h = _layer_norm(DN_ALPHA * x + mix, g_ref[...], b_ref[...])
        h_ref[rows, :] = h
        hp_ref[rows, :] = _pack_bf16_pairs(h)
        hh = h.astype(BF16)
        hl = (h - hh.astype(F32)).astype(BF16)
        logits.append(jnp.dot(hh, wrh_ref[...], preferred_element_type=F32)
                      + jnp.dot(hl, wrh_ref[...], preferred_element_type=F32)
                      + jnp.dot(hh, wrl_ref[...], preferred_element_type=F32) + br_ref[...])

    lane = lax.broadcasted_iota(jnp.int32, (hm, LANES), 1)
    r_i = lax.broadcasted_iota(jnp.int32, (hm, hm), 0)
    c_i = lax.broadcasted_iota(jnp.int32, (hm, hm), 1)
    tri = (r_i > c_i).astype(BF16)
    before = carry_ref[...]
    for rows, lg in zip(halves, logits):
        lg = jnp.where(lane < N_EXPERTS, lg, -jnp.inf)
        vals, idxs = [], []
        for _ in range(TOP_K):
            mv = jnp.max(lg, axis=1, keepdims=True)
            ik = jnp.min(jnp.where(lg == mv, lane, LANES), axis=1, keepdims=True)
            vals.append(mv)
            idxs.append(ik)
            lg = jnp.where(lane == ik, -jnp.inf, lg)
        es = [jnp.exp(v - vals[0]) for v in vals]
        tot = es[0] + es[1] + es[2] + es[3]
        onehot = jnp.zeros((hm, LANES), F32)
        for ik in idxs:
            onehot = onehot + (lane == ik).astype(F32)
        pre = jnp.dot(tri, onehot.astype(BF16), preferred_element_type=F32) + before
        topi = jnp.zeros((hm, LANES), jnp.int32)
        topw = jnp.zeros((hm, LANES), F32)
        rank = jnp.zeros((hm, LANES), F32)
        for k in range(TOP_K):
            rk = jnp.sum(jnp.where(lane == idxs[k], pre, 0.0), axis=1, keepdims=True)
            topi = jnp.where(lane == k, idxs[k], topi)
            topw = jnp.where(lane == k, es[k] / tot, topw)
            rank = jnp.where(lane == k, rk, rank)
        topi_ref[rows, :] = topi
        topw_ref[rows, :] = topw
        rank_ref[rows, :] = rank.astype(jnp.int32)
        before = before + jnp.sum(onehot, axis=0, keepdims=True)
    carry_ref[...] = before
    cnt_ref[...] = before


def _merge(ya, dil_outs, x2, seq, w_b, b_gate, w_oa_b, w_ob_b, w_o_b, ln1_g, ln1_b, w_router, b_router):
    tokens = x2.shape[0]
    tm = MERGE_TM
    per_seq = seq // tm

    def group(a):
        dil = a.shape[1]
        return pl.BlockSpec((None, dil, tm // dil, SEG_W), lambda i: (i // per_seq, 0, i % per_seq, 0))

    gate_blk = w_b.shape[1] // (2 * D_MODEL) - 1
    wr = jnp.zeros((D_MODEL, LANES), F32).at[:, :N_EXPERTS].set(w_router)
    wr_hi = wr.astype(BF16)
    wr_lo = (wr - wr_hi.astype(F32)).astype(BF16)
    br = jnp.zeros((1, LANES), F32).at[0, :N_EXPERTS].set(b_router)
    row = lambda w: pl.BlockSpec((tm, w), lambda i: (i, 0))
    full = lambda a: pl.BlockSpec(a.shape, lambda i: (0,) * a.ndim)
    (o1, l1), (o2, l2), (o3, l3) = dil_outs
    bg = b_gate.reshape(1, -1)
    g1 = ln1_g.reshape(1, -1)
    b1 = ln1_b.reshape(1, -1)
    lane_out = lambda dt: jax.ShapeDtypeStruct((tokens, LANES), dt)
    return pl.pallas_call(
        _merge_kernel,
        grid=(tokens // tm,),
        in_specs=[row(SEG_W), group(o1), group(o2), group(o3), group(l1), group(l2), group(l3),
                  row(D_MODEL), pl.BlockSpec((D_MODEL, 2 * D_MODEL), lambda i: (0, gate_blk)),
                  full(bg), full(w_oa_b), full(w_ob_b), full(w_o_b), full(g1), full(b1),
                  full(wr_hi), full(wr_lo), full(br)],
        out_specs=[row(D_MODEL), row(D_MODEL // 2), row(LANES), row(LANES), row(LANES),
                   pl.BlockSpec((1, LANES), lambda i: (0, 0))],
        out_shape=[jax.ShapeDtypeStruct((tokens, D_MODEL), F32), jax.ShapeDtypeStruct((tokens, D_MODEL // 2), U32),
                   lane_out(jnp.int32), lane_out(F32), lane_out(jnp.int32), jax.ShapeDtypeStruct((1, LANES), F32)],
        scratch_shapes=[pltpu.VMEM((1, LANES), F32), pltpu.VMEM((6, SEG_W // LANES, tm, LANES), F32),
                        pltpu.VMEM((D_MODEL, 2 * D_MODEL), BF16)],
        compiler_params=_cparams(1),
        name="merge",
    )(ya, o1, o2, o3, l1, l2, l3, x2, w_b, bg, w_oa_b, w_ob_b, w_o_b, g1, b1, wr_hi, wr_lo, br)


def _sc_mesh():
    return plsc.VectorSubcoreMesh(core_axis_name="core", subcore_axis_name="subcore")


def _sc_scatter_rows(table, src_idx, dst_idx):
    n = src_idx.shape[0]
    d = table.shape[1]
    mesh = _sc_mesh()
    workers = mesh.num_cores * mesh.num_subcores
    per = n // (SC_CHUNK * workers)
    assert per * SC_CHUNK * workers == n

    @pl.kernel(out_type=jax.ShapeDtypeStruct((n, d), table.dtype), mesh=mesh,
               scratch_types=[pltpu.VMEM((1, SC_CHUNK), jnp.int32), pltpu.VMEM((1, SC_CHUNK), jnp.int32),
                              pltpu.VMEM((SC_CHUNK, d), table.dtype)])
    def copy(t_hbm, s_hbm, d_hbm, o_hbm, s_vm, d_vm, buf):
        wid = lax.axis_index("core") * mesh.num_subcores + lax.axis_index("subcore")

        @pl.loop(0, per)
        def _(j):
            blk = wid * per + j
            pltpu.sync_copy(s_hbm.at[pl.ds(blk, 1)], s_vm)
            pltpu.sync_copy(d_hbm.at[pl.ds(blk, 1)], d_vm)
            pltpu.sync_copy(t_hbm.at[s_vm.at[0]], buf)
            pltpu.sync_copy(buf, o_hbm.at[d_vm.at[0]])

    return copy(table, src_idx.reshape(-1, SC_CHUNK), dst_idx.reshape(-1, SC_CHUNK))


def _sc_gather_rows(table, idx):
    n = idx.shape[0]
    d = table.shape[1]
    mesh = _sc_mesh()
    workers = mesh.num_cores * mesh.num_subcores
    per = n // (SC_CHUNK * workers)
    assert per * SC_CHUNK * workers == n

    @pl.kernel(out_type=jax.ShapeDtypeStruct((n, d), table.dtype), mesh=mesh,
               scratch_types=[pltpu.VMEM((1, SC_CHUNK), jnp.int32), pltpu.VMEM((SC_CHUNK, d), table.dtype)])
    def gather(t_hbm, i_hbm, o_hbm, i_vm, buf):
        wid = lax.axis_index("core") * mesh.num_subcores + lax.axis_index("subcore")

        @pl.loop(0, per)
        def _(j):
            blk = wid * per + j
            pltpu.sync_copy(i_hbm.at[pl.ds(blk, 1)], i_vm)
            pltpu.sync_copy(t_hbm.at[i_vm.at[0]], buf)
            pltpu.sync_copy(buf, o_hbm.at[pl.ds(blk * SC_CHUNK, SC_CHUNK)])

    return gather(table, idx.reshape(-1, SC_CHUNK))


def _expert_kernel(be_ref, nx_ref, nu_ref, x_ref, wgu_hbm, bgu_ref, wd_hbm, bd_ref, y_ref,
                   wgu_stage, wd_stage, wgu_b, wd_b, sem):
    i = pl.program_id(0)
    used = i < nu_ref[0]
    expert = be_ref[i]

    def weight_copies(e):
        return (pltpu.make_async_copy(wgu_hbm.at[e], wgu_stage, sem.at[0]),
                pltpu.make_async_copy(wd_hbm.at[e], wd_stage, sem.at[1]))

    @pl.when(i == 0)
    def _():
        for cp in weight_copies(expert):
            cp.start()

    @pl.when(used & ((i == 0) | (expert != be_ref[jnp.maximum(i - 1, 0)])))
    def _():
        for cp in weight_copies(expert):
            cp.wait()
        wgu_b[...] = wgu_stage[...].astype(BF16)
        wd_b[...] = wd_stage[...].astype(BF16)

        @pl.when(nx_ref[i] >= 0)
        def _():
            for cp in weight_copies(nx_ref[i]):
                cp.start()

    @pl.when(used)
    def _():
        lo, hi = _unpack_bf16_pairs(x_ref[...])
        xb = jnp.concatenate([lo, hi], axis=1).astype(BF16)
        chunk = 512
        cols = [slice(c * chunk, (c + 1) * chunk) for c in range(D_FF // chunk)]
        pre = []
        for gs in cols:
            us = slice(D_FF + gs.start, D_FF + gs.stop)
            gate = jnp.dot(xb, wgu_b[:, gs], preferred_element_type=F32) + bgu_ref[:, gs]
            up = jnp.dot(xb, wgu_b[:, us], preferred_element_type=F32) + bgu_ref[:, us]
            pre.append((gate, up))
        acc = jnp.zeros((x_ref.shape[0], D_MODEL), F32)
        for gs, (gate, up) in zip(cols, pre):
            gate = jnp.minimum(gate, SWIGLU_LIMIT)
            up = jnp.clip(up, -SWIGLU_LIMIT, SWIGLU_LIMIT)
            act = (up + 1.0) * gate * jax.nn.sigmoid(SWIGLU_ALPHA * gate)
            acc = acc + jnp.dot(act.astype(BF16), wd_b[gs, :], preferred_element_type=F32)
        y_ref[...] = _pack_bf16_pairs(acc + bd_ref[...])

    @pl.when(i >= nu_ref[0])
    def _():
        y_ref[...] = jnp.zeros_like(y_ref)


def _experts(xrows, block_e, next_e, n_used, w_gu, b_gu, w_down, b_down):
    n_rows = xrows.shape[0]
    bm = MOE_BM
    bias = lambda i, be, nx, nu: (be[i], 0, 0)
    return pl.pallas_call(
        _expert_kernel,
        grid_spec=pltpu.PrefetchScalarGridSpec(
            num_scalar_prefetch=3,
            grid=(n_rows // bm,),
            in_specs=[
                pl.BlockSpec((bm, D_MODEL // 2), lambda i, be, nx, nu: (i, 0)),
                pl.BlockSpec(memory_space=pl.ANY),
                pl.BlockSpec((None, 1, 2 * D_FF), bias),
                pl.BlockSpec(memory_space=pl.ANY),
                pl.BlockSpec((None, 1, D_MODEL), bias),
            ],
            out_specs=pl.BlockSpec((bm, D_MODEL // 2), lambda i, be, nx, nu: (i, 0)),
            scratch_shapes=[pltpu.VMEM((D_MODEL, 2 * D_FF), F32), pltpu.VMEM((D_FF, D_MODEL), F32),
                            pltpu.VMEM((D_MODEL, 2 * D_FF), BF16), pltpu.VMEM((D_FF, D_MODEL), BF16),
                            pltpu.SemaphoreType.DMA((2,))],
        ),
        out_shape=jax.ShapeDtypeStruct((n_rows, D_MODEL // 2), U32),
        compiler_params=_cparams(1),
        name="experts",
    )(block_e, next_e, n_used, xrows, w_gu, b_gu.reshape(N_EXPERTS, 1, -1), w_down, b_down.reshape(N_EXPERTS, 1, -1))


def _combine_kernel(y0_ref, y1_ref, y2_ref, y3_ref, w_ref, h_ref, g_ref, b_ref, o_ref):
    w = w_ref[...]
    half = D_MODEL // 2
    f_lo = jnp.zeros((h_ref.shape[0], half), F32)
    f_hi = jnp.zeros((h_ref.shape[0], half), F32)
    for k, y_ref in enumerate((y0_ref, y1_ref, y2_ref, y3_ref)):
        lo, hi = _unpack_bf16_pairs(y_ref[...])
        f_lo = f_lo + lo * w[:, k:k + 1]
        f_hi = f_hi + hi * w[:, k:k + 1]
    f = jnp.concatenate([f_lo, f_hi], axis=1)
    o_ref[...] = _layer_norm(DN_ALPHA * h_ref[...] + f, g_ref[...], b_ref[...])


def _combine(yplanes, topw, h, ln2_g, ln2_b):
    tokens = h.shape[0]
    tm = MOE_TM
    per_plane = tokens // tm
    plane = lambda k: pl.BlockSpec((tm, D_MODEL // 2), lambda i: (k * per_plane + i, 0))
    return pl.pallas_call(
        _combine_kernel,
        grid=(per_plane,),
        in_specs=[plane(0), plane(1), plane(2), plane(3),
                  pl.BlockSpec((tm, LANES), lambda i: (i, 0)),
                  pl.BlockSpec((tm, D_MODEL), lambda i: (i, 0)),
                  pl.BlockSpec((1, D_MODEL), lambda i: (0, 0)),
                  pl.BlockSpec((1, D_MODEL), lambda i: (0, 0))],
        out_specs=pl.BlockSpec((tm, D_MODEL), lambda i: (i, 0)),
        out_shape=jax.ShapeDtypeStruct((tokens, D_MODEL), F32),
        compiler_params=_cparams(1),
        name="combine",
    )(yplanes, yplanes, yplanes, yplanes, topw, h, ln2_g.reshape(1, -1), ln2_b.reshape(1, -1))


def _routing_tables(topi, rank, cnt, tokens):
    bm = MOE_BM
    i32 = jnp.int32
    experts = jnp.arange(N_EXPERTS, dtype=i32)
    counts = cnt[0, :N_EXPERTS].astype(i32)
    padded = (counts + bm - 1) // bm * bm
    pend = jnp.cumsum(padded)
    pstart = pend - padded
    sel = topi[:, :TOP_K, None] == experts[None, None, :]
    dest = rank[:, :TOP_K] + jnp.sum(jnp.where(sel, pstart[None, None, :], 0), axis=-1)
    n_pad = N_EXPERTS * bm
    n_rows = tokens * TOP_K + n_pad
    starts = jnp.arange(n_rows // bm, dtype=i32) * bm
    block_e = jnp.minimum(jnp.sum((pend[None, :] <= starts[:, None]).astype(i32), axis=1), N_EXPERTS - 1)
    n_used = (pend[-1] // bm).reshape(1)
    is_block_e = block_e[:, None] == experts[None, :]
    after = jnp.sum(jnp.where(is_block_e, pend[None, :], 0), axis=1) // bm
    e_after = jnp.sum(jnp.where(after[:, None] == jnp.arange(n_rows // bm, dtype=i32)[None, :], block_e[None, :], 0),
                      axis=1)
    next_e = jnp.where(after < n_used[0], e_after, -1)
    pad_cnt = padded - counts
    pad_end = jnp.cumsum(pad_cnt)
    j = jnp.arange(n_pad, dtype=i32)
    owner = jnp.sum((pad_end[None, :] <= j[:, None]).astype(i32), axis=1)
    is_owner = owner[:, None] == experts[None, :]
    pick = lambda v: jnp.sum(jnp.where(is_owner, v[None, :], 0), axis=1)
    in_expert = pick(pstart + counts) + (j - pick(pad_end - pad_cnt))
    pad_rows = jnp.where(owner < N_EXPERTS, in_expert, pend[-1] + (j - pad_end[-1]))
    src_tok = jnp.concatenate([jnp.repeat(jnp.arange(tokens, dtype=i32), TOP_K), j % tokens])
    dst_row = jnp.concatenate([dest.reshape(-1), pad_rows]).astype(i32)
    return dest.astype(i32), block_e.astype(i32), next_e.astype(i32), n_used.astype(i32), src_tok, dst_row


def kernel(x, w_in, b_gate, lam_q1, lam_k1, lam_q2, lam_k2, subln_g, w_oa, w_ob, w_o, ln1_g, ln1_b,
           w_router, b_router, w_gu, b_gu, w_down, b_down, ln2_g, ln2_b):
    batch, seq, d = x.shape
    tokens = batch * seq
    h = x.reshape(tokens, d)
    tables = {dilation: _rope_lane_table(seq, dilation) for _, dilation in DIL_PAIRS}
    for l in range(DEPTH):
        q_scale = HEAD_DIM ** -0.5 * math.log2(math.e)
        qk_a, vt_a = _project_qkv(h, w_in[l], 0, tables[1], seq, "qkv_diff", q_scale=q_scale, v_feature_major=True)
        ya = _diff_attention(qk_a.reshape(batch, seq, 2 * SEG_W), vt_a, lam_q1[l], lam_k1[l], lam_q2[l],
                             lam_k2[l], subln_g[l]).reshape(tokens, -1)
        dil = []
        for g, (_, dilation) in enumerate(DIL_PAIRS):
            if dilation == 1:
                h_g = h
            else:
                h_g = (h.reshape(batch, seq // dilation, dilation, d).swapaxes(1, 2).reshape(tokens, d)
                       .astype(BF16))
            qkv_g = _project_qkv(h_g, w_in[l], g + 1, tables[dilation], seq, f"qkv_dil{g}", q_scale=q_scale)
            dil.append(_dilated_group(qkv_g, batch, seq, g))
        h1, h1p, topi, topw, rank, cnt = _merge(ya, dil, h, seq, w_in[l], b_gate[l], w_oa[l].astype(BF16),
                                                w_ob[l].astype(BF16), w_o[l].astype(BF16), ln1_g[l], ln1_b[l],
                                                w_router[l], b_router[l])
        dest, block_e, next_e, n_used, src_tok, dst_row = _routing_tables(topi, rank, cnt, tokens)
        xrows = _sc_scatter_rows(h1p, src_tok, dst_row)
        yrows = _experts(xrows, block_e, next_e, n_used, w_gu[l], b_gu[l], w_down[l], b_down[l])
        yplanes = _sc_gather_rows(yrows, dest.T.reshape(-1))
        h = _combine(yplanes, topw, h1, ln2_g[l], ln2_b[l])
    return h.reshape(batch, seq, d)
```

```python
import functools
import math

import jax
import jax.numpy as jnp
from jax import lax
from jax.experimental import pallas as pl
from jax.experimental.pallas import tpu as pltpu
from jax.experimental.pallas import tpu_sc as plsc

F32 = jnp.float32
BF16 = jnp.bfloat16
U32 = jnp.uint32

D_MODEL = 1024
HEAD_DIM = 64
ROT_DIM = HEAD_DIM // 4
ROPE_THETA = 500000.0
QBLK = 128
DA_HEADS = 4
DIL_PAIRS = ((128, 1), (512, 4), (2048, 16))
SEG_W = 512
QKV_W = 3 * SEG_W
N_EXPERTS = 32
TOP_K = 4
D_FF = D_MODEL
SWIGLU_ALPHA = 1.702
SWIGLU_LIMIT = 7.0
DEPTH = 1
DN_ALPHA = (2 * DEPTH) ** 0.25
EPS = 1e-5
LAMBDA_INIT = 0.8 - 0.6 * math.exp(-0.3 * 0)

LANES = 128
QKV_TM = 512
QKV_CHUNK = 256
ATT_TQ = 256
ONES_ROWS = 16
DIL_NQ = 4
MERGE_TM = 512
MOE_BM = 256
MOE_TM = 256
SC_CHUNK = 128
VMEM_LIMIT = 52 * 1024 * 1024


def _cparams(n_axes):
    return pltpu.CompilerParams(dimension_semantics=("arbitrary",) * n_axes,
                                vmem_limit_bytes=VMEM_LIMIT)


def _qkv_kernel(x_ref, w_ref, cs_ref, o_ref, *rest, q_scale, dil):
    tm = x_ref.shape[0]
    n = tm // dil
    if dil > 1:
        *maybe_vt_ref, wb_ref, slab_ref, xp_ref = rest
    else:
        *maybe_vt_ref, wb_ref = rest

    @pl.when(pl.program_id(0) == 0)
    def _():
        wb_ref[...] = w_ref[...].astype(BF16)

    if dil > 1:
        for c in range(D_MODEL // LANES):
            slab_ref[c] = x_ref[:, c * LANES:(c + 1) * LANES]
        for r in range(dil):
            for c in range(D_MODEL // LANES):
                xp_ref[r * n:(r + 1) * n, c * LANES:(c + 1) * LANES] = (
                    slab_ref[c, pl.ds(r, n, stride=dil), :].astype(BF16))

    def store(rows, cols, val):
        if dil == 1:
            o_ref[rows, cols] = val
        elif n >= QKV_CHUNK:
            o_ref[rows.start // n, rows.start % n:rows.start % n + QKV_CHUNK, cols] = val
        else:
            per = QKV_CHUNK // n
            for q in range(per):
                o_ref[rows.start // n + q, :, cols] = val[q * n:(q + 1) * n]

    half = ROT_DIM // 2
    in_head = lax.broadcasted_iota(jnp.int32, (QKV_CHUNK, LANES), 1) % HEAD_DIM
    first, second = in_head < half, (in_head >= half) & (in_head < ROT_DIM)
    for rc in range(tm // QKV_CHUNK):
        rows = slice(rc * QKV_CHUNK, (rc + 1) * QKV_CHUNK)
        xb = xp_ref[rows, :] if dil > 1 else x_ref[rows, :].astype(BF16)
        cs = cs_ref[rows, :]
        c = jnp.where(first, cs, jnp.where(second, pltpu.roll(cs, half, 1), 1.0))
        s1 = jnp.where(second, cs, 0.0)
        s2 = jnp.where(first, -pltpu.roll(cs, LANES - half, 1), 0.0)
        for seg in range(3):
            acc = jnp.dot(xb, wb_ref[:, seg * SEG_W:(seg + 1) * SEG_W], preferred_element_type=F32)
            if seg == 2:
                if maybe_vt_ref:
                    maybe_vt_ref[0][:, rows] = acc.T.astype(BF16)
                else:
                    store(rows, slice(seg * SEG_W, (seg + 1) * SEG_W), acc.astype(BF16))
                continue
            for k in range(SEG_W // LANES):
                t = acc[:, k * LANES:(k + 1) * LANES]
                r = t * c + pltpu.roll(t, ROT_DIM // 2, 1) * s1 + pltpu.roll(t, LANES - ROT_DIM // 2, 1) * s2
                if seg == 0:
                    r = r * q_scale
                lo = seg * SEG_W + k * LANES
                store(rows, slice(lo, lo + LANES), r.astype(BF16))


def _rope_lane_table(seq, dil, tm):
    row = jnp.arange(seq, dtype=jnp.int32)
    n = tm // dil
    in_tile = row % tm
    pos = (row - in_tile + (in_tile % n) * dil + in_tile // n).astype(F32)
    inv_freq = ROPE_THETA ** (-jnp.arange(0, ROT_DIM, 2, dtype=F32) / ROT_DIM)
    ang = pos[:, None] * inv_freq[None, :]
    head = jnp.concatenate([jnp.cos(ang), jnp.sin(ang), jnp.zeros((seq, HEAD_DIM - ROT_DIM), F32)], axis=1)
    return jnp.tile(head, (1, LANES // HEAD_DIM))


def _project_qkv(x2, w, col_block, seq, name, q_scale, dil=1, v_feature_major=False):
    tokens = x2.shape[0]
    batch = tokens // seq
    tm = min(QKV_TM, seq)
    per_seq = seq // tm
    n = tm // dil
    out_w = 2 * SEG_W if v_feature_major else QKV_W
    scratch = [pltpu.VMEM((D_MODEL, QKV_W), BF16)]
    if dil == 1:
        out_specs = [pl.BlockSpec((tm, out_w), lambda i: (i, 0))]
        out_shape = [jax.ShapeDtypeStruct((tokens, out_w), BF16)]
    else:
        out_specs = [pl.BlockSpec((None, dil, n, out_w), lambda i: (i // per_seq, 0, i % per_seq, 0))]
        out_shape = [jax.ShapeDtypeStruct((batch, dil, seq // dil, out_w), BF16)]
        scratch += [pltpu.VMEM((D_MODEL // LANES, tm, LANES), F32), pltpu.VMEM((tm, D_MODEL), BF16)]
    if v_feature_major:
        out_specs.append(pl.BlockSpec((None, SEG_W, tm), lambda i: (i // per_seq, 0, i % per_seq)))
        out_shape.append(jax.ShapeDtypeStruct((batch, SEG_W, seq), BF16))
    outs = pl.pallas_call(
        functools.partial(_qkv_kernel, q_scale=q_scale, dil=dil),
        grid=(tokens // tm,),
        in_specs=[pl.BlockSpec((tm, D_MODEL), lambda i: (i, 0)),
                  pl.BlockSpec((D_MODEL, QKV_W), lambda i: (0, col_block)),
                  pl.BlockSpec((tm, LANES), lambda i: (i % per_seq, 0))],
        out_specs=out_specs,
        out_shape=out_shape,
        scratch_shapes=scratch,
        compiler_params=_cparams(1),
        name=name,
    )(x2, w, _rope_lane_table(seq, dil, tm))
    return outs if v_feature_major else outs[0]


def _nt_dot(a, b):
    return lax.dot_general(a, b, (((1,), (1,)), ((), ())), preferred_element_type=F32)


def _diff_kernel(q_ref, k_ref, vt_ref, lq1_ref, lk1_ref, lq2_ref, lk2_ref, g_ref, o_ref, s_ref, acc_ref, *, tq):
    i = pl.program_id(1)
    chains = 2 * DA_HEADS
    half = tq // 2
    lane = lax.broadcasted_iota(jnp.int32, (tq, LANES), 1)
    qs = []
    for h in range(DA_HEADS):
        q = q_ref[:, h * LANES:(h + 1) * LANES]
        zero = jnp.zeros_like(q)
        qs += [jnp.where(lane < HEAD_DIM, q, zero), jnp.where(lane >= HEAD_DIM, q, zero)]
    acc_ref[...] = jnp.zeros_like(acc_ref)

    def score_chain(j, buf, c):
        start = pl.multiple_of(j * tq, tq)
        kb = k_ref[pl.ds(start, tq), (c // 2) * LANES:(c // 2 + 1) * LANES]
        buf[c] = _nt_dot(kb, qs[c])

    def score_block(j, buf):
        for c in range(chains):
            score_chain(j, buf, c)

    def absorb(j, buf, state, masked, nxt=None):
        start = pl.multiple_of(j * tq, tq)
        out = []
        for c in range(chains):
            if nxt is not None:
                score_chain(j + 1, nxt, c)
            m, l = state[c]
            if masked:
                key = lax.broadcasted_iota(jnp.int32, (tq, tq), 0)
                qry = lax.broadcasted_iota(jnp.int32, (tq, tq), 1)
                buf[c] = jnp.where(key <= qry, buf[c], -jnp.inf)
            m_new = jnp.maximum(m, jnp.max(buf[c], axis=0, keepdims=True))
            a = jnp.exp2(m - m_new)
            p = [jnp.exp2((buf[c, u * half:(u + 1) * half, :] - m_new).astype(BF16)) for u in range(2)]
            vt = vt_ref[(c // 2) * LANES:(c // 2 + 1) * LANES, pl.ds(start, tq)]
            vt1 = jnp.concatenate([vt, jnp.ones((ONES_ROWS, tq), BF16)], axis=0)
            pv = jnp.dot(vt1, jnp.concatenate(p, axis=0), preferred_element_type=F32)
            l = a * l + pv[LANES:LANES + 1, :]
            acc_ref[c] = a * acc_ref[c] + pv[:LANES, :]
            out.append((m_new, l))
        return tuple(out)

    buf_a, buf_b = s_ref.at[0], s_ref.at[1]

    def two_blocks(jj, state):
        j = 2 * jj
        state = absorb(j, buf_a, state, False, nxt=buf_b)
        return absorb(j + 1, buf_b, state, False, nxt=buf_a)

    def last_from_a(state):
        return absorb(i, buf_a, state, True)

    def last_from_b(state):
        state = absorb(i - 1, buf_a, state, False, nxt=buf_b)
        return absorb(i, buf_b, state, True)

    init = tuple((jnp.full((1, tq), -jnp.inf, F32), jnp.zeros((1, tq), F32)) for _ in range(chains))
    score_block(0, buf_a)
    state = lax.fori_loop(0, i // 2, two_blocks, init)
    state = lax.cond(lax.rem(i, 2) == 0, last_from_a, last_from_b, state)

    lam = (jnp.exp(jnp.sum(lq1_ref[...] * lk1_ref[...], axis=1, keepdims=True))
           - jnp.exp(jnp.sum(lq2_ref[...] * lk2_ref[...], axis=1, keepdims=True)) + LAMBDA_INIT)
    for h in range(DA_HEADS):
        l1, l2 = state[2 * h][1], state[2 * h + 1][1]
        o = (acc_ref[2 * h] / l1 - lam * (acc_ref[2 * h + 1] / l2)).T
        ms = jnp.mean(o * o, axis=1, keepdims=True)
        o = o * lax.rsqrt(ms + EPS) * g_ref[...]
        o_ref[:, h * LANES:(h + 1) * LANES] = (o * (1.0 - LAMBDA_INIT)).astype(BF16)


def _diff_attention(qk3, vt3, lam_q1, lam_k1, lam_q2, lam_k2, subln_g):
    b, s, _ = qk3.shape
    tq = min(ATT_TQ, s)
    vec = lambda n: pl.BlockSpec((1, n), lambda bb, i: (0, 0))
    return pl.pallas_call(
        functools.partial(_diff_kernel, tq=tq),
        grid=(b, s // tq),
        in_specs=[
            pl.BlockSpec((None, tq, SEG_W), lambda bb, i: (bb, i, 0)),
            pl.BlockSpec((None, s, SEG_W), lambda bb, i: (bb, 0, 1)),
            pl.BlockSpec((None, SEG_W, s), lambda bb, i: (bb, 0, 0)),
            vec(HEAD_DIM), vec(HEAD_DIM), vec(HEAD_DIM), vec(HEAD_DIM), vec(2 * HEAD_DIM),
        ],
        out_specs=pl.BlockSpec((None, tq, SEG_W), lambda bb, i: (bb, i, 0)),
        out_shape=jax.ShapeDtypeStruct((b, s, SEG_W), BF16),
        scratch_shapes=[pltpu.VMEM((2, 2 * DA_HEADS, tq, tq), F32), pltpu.VMEM((2 * DA_HEADS, LANES, tq), F32)],
        compiler_params=_cparams(2),
        name="diff_attn",
    )(qk3, qk3, vt3, lam_q1.reshape(1, -1), lam_k1.reshape(1, -1), lam_q2.reshape(1, -1),
      lam_k2.reshape(1, -1), subln_g.reshape(1, -1))


def _dil_kernel(q_ref, kp_ref, kc_ref, vp_ref, vc_ref, o_ref, lse_ref, *, rel, nq):
    n = pl.program_id(2)
    qi = lax.broadcasted_iota(jnp.int32, (QBLK, 2 * QBLK), 0)
    kj = lax.broadcasted_iota(jnp.int32, (QBLK, 2 * QBLK), 1)
    dist = qi + QBLK - kj
    band = (dist >= 0) & (dist <= rel)
    lane = lax.broadcasted_iota(jnp.int32, (QBLK, LANES), 1)
    low = lane < HEAD_DIM
    slabs = [slice(p * LANES, (p + 1) * LANES) for p in range(SEG_W // LANES)]

    def window(prev_ref, cur_ref, s, sl):
        if s == 0:
            return jnp.concatenate([prev_ref[:, sl], cur_ref[:QBLK, sl]], axis=0)
        return cur_ref[(s - 1) * QBLK:(s + 1) * QBLK, sl]

    def score_block(s):
        rows = slice(s * QBLK, (s + 1) * QBLK)
        scores = []
        for sl in slabs:
            q2 = q_ref[rows, sl]
            k2 = window(kp_ref, kc_ref, s, sl)
            for sel in (low, jnp.logical_not(low)):
                scores.append(_nt_dot(jnp.where(sel, q2, jnp.zeros_like(q2)), k2))
        return scores

    def finish_block(s, scores):
        rows = slice(s * QBLK, (s + 1) * QBLK)
        valid = band & ((kj >= QBLK) | (n * nq + s > 0))
        probs = []
        for sc in scores:
            sc = jnp.where(valid, sc, -jnp.inf)
            m = jnp.max(sc, axis=1, keepdims=True)
            probs.append((jnp.exp2((sc - m).astype(BF16)), m))
        one = jnp.ones((2 * QBLK, LANES), BF16)
        low_keys = lax.broadcasted_iota(jnp.int32, (2 * QBLK, LANES), 1) < HEAD_DIM
        for p, sl in enumerate(slabs):
            v2 = window(vp_ref, vc_ref, s, sl)
            (pe_lo, m_lo), (pe_hi, m_hi) = probs[2 * p:2 * p + 2]
            pv_lo = jnp.dot(pe_lo, jnp.where(low_keys, v2, one), preferred_element_type=F32)
            pv_hi = jnp.dot(pe_hi, jnp.where(low_keys, one, v2), preferred_element_type=F32)
            num = jnp.where(low, pv_lo, pv_hi)
            den = pltpu.roll(jnp.where(low, pv_hi, pv_lo), HEAD_DIM, 1)
            o_ref[rows, sl] = (num / den).astype(BF16)
            lse_ref[rows, sl] = jnp.where(low, m_lo, m_hi) + jnp.log2(den)

    pending = score_block(0)
    for s in range(1, nq):
        nxt = score_block(s)
        finish_block(s - 1, pending)
        pending = nxt
    finish_block(nq - 1, pending)


def _dilated_group(qkv, batch, seq, group):
    window, dil = DIL_PAIRS[group]
    rel = window // dil
    length = seq // dil
    nb = length // QBLK
    nq = math.gcd(nb, DIL_NQ)
    view = qkv.reshape(batch, dil, length, QKV_W)

    def spec(seg, prev):
        if prev:
            return pl.BlockSpec((None, None, QBLK, SEG_W), lambda b, r, n: (b, r, jnp.maximum(n * nq - 1, 0), seg))
        return pl.BlockSpec((None, None, nq * QBLK, SEG_W), lambda b, r, n: (b, r, n, seg))

    out_spec = pl.BlockSpec((None, None, nq * QBLK, SEG_W), lambda b, r, n: (b, r, n, 0))
    o, lse = pl.pallas_call(
        functools.partial(_dil_kernel, rel=rel, nq=nq),
        grid=(batch, dil, nb // nq),
        in_specs=[spec(0, False), spec(1, True), spec(1, False), spec(2, True), spec(2, False)],
        out_specs=[out_spec, out_spec],
        out_shape=[jax.ShapeDtypeStruct((batch, dil, length, SEG_W), BF16),
                   jax.ShapeDtypeStruct((batch, dil, length, SEG_W), F32)],
        compiler_params=_cparams(3),
        name=f"dilated{group}",
    )(view, view, view, view, view)
    return o, lse


def _layer_norm(z, g, b):
    mu = jnp.mean(z, axis=1, keepdims=True)
    zc = z - mu
    var = jnp.mean(zc * zc, axis=1, keepdims=True)
    return zc * lax.rsqrt(var + EPS) * g + b


def _pack_bf16_pairs(v):
    n = v.shape[1] // 2
    lo = pltpu.bitcast(v[:, :n].astype(BF16).astype(F32), U32)
    hi = pltpu.bitcast(v[:, n:].astype(BF16).astype(F32), U32)
    return jnp.bitwise_or(jnp.right_shift(lo, jnp.uint32(16)), jnp.bitwise_and(hi, jnp.uint32(0xFFFF0000)))


def _unpack_bf16_pairs(w):
    lo = pltpu.bitcast(jnp.left_shift(w, jnp.uint32(16)), F32)
    hi = pltpu.bitcast(jnp.bitwise_and(w, jnp.uint32(0xFFFF0000)), F32)
    return lo, hi


def _merge_kernel(ya_ref, o1_ref, o2_ref, o3_ref, l1_ref, l2_ref, l3_ref, x_ref,
                  wg_ref, bg_ref, woa_ref, wob_ref, wo_ref, g_ref, b_ref, wrh_ref, wrl_ref, br_ref,
                  h_ref, hp_ref, topi_ref, topw_ref, rank_ref, cnt_ref, carry_ref, scr_ref, wgb_ref):
    i = pl.program_id(0)
    tm = x_ref.shape[0]

    @pl.when(i == 0)
    def _():
        carry_ref[...] = jnp.zeros_like(carry_ref)
        wgb_ref[...] = wg_ref[...].astype(BF16)

    hm = tm // 2
    halves = (slice(0, hm), slice(hm, tm))
    xs = [x_ref[rows, :] for rows in halves]
    gate_dot = lambda x: jnp.dot(x.astype(BF16), wgb_ref[...], preferred_element_type=F32) + bg_ref[...]
    gate_pre = [gate_dot(xs[0])]
    pas = [jnp.dot(ya_ref[rows, :], woa_ref[...], preferred_element_type=F32) for rows in halves]

    def token_major(ref, scr):
        dil, n = ref.shape[0], ref.shape[1]
        if dil == 1:
            return ref[0].astype(F32)
        for r in range(dil):
            blk = ref[r].astype(F32)
            for c in range(SEG_W // LANES):
                scr[c, pl.ds(r, n, stride=dil), :] = blk[:, c * LANES:(c + 1) * LANES]
        return jnp.concatenate([scr[c] for c in range(SEG_W // LANES)], axis=1)

    oa, ob, oc = (token_major(r, scr_ref.at[k]) for k, r in enumerate((o1_ref, o2_ref, o3_ref)))
    la, lb, lc = (token_major(r, scr_ref.at[3 + k]) for k, r in enumerate((l1_ref, l2_ref, l3_ref)))
    mx = jnp.maximum(jnp.maximum(la, lb), lc)
    ea, eb, ec = jnp.exp2(la - mx), jnp.exp2(lb - mx), jnp.exp2(lc - mx)
    yb = ((ea * oa + eb * ob + ec * oc) / (ea + eb + ec)).astype(BF16)
    pbs = [jnp.dot(yb[rows, :], wob_ref[...], preferred_element_type=F32) for rows in halves]
    gate_pre.append(gate_dot(xs[1]))
    mixes = []
    for pre_act, pa, pb in zip(gate_pre, pas, pbs):
        gates = jax.nn.sigmoid(pre_act)
        merged = gates[:, :D_MODEL] * pa + gates[:, D_MODEL:] * pb
        mixes.append(jnp.dot(merged.astype(BF16), wo_ref[...], preferred_element_type=F32))
    logits = []
    for rows, x, mix in zip(halves, xs, mixes):
        h = _layer_norm(DN_ALPHA * x + mix, g_ref[...], b_ref[...])
        h_ref[rows, :] = h
        hp_ref[rows, :] = _pack_bf16_pairs(h)
        hh = h.astype(BF16)
        hl = (h - hh.astype(F32)).astype(BF16)
        logits.append(jnp.dot(hh, wrh_ref[...], preferred_element_type=F32)
                      + jnp.dot(hl, wrh_ref[...], preferred_element_type=F32)
                      + jnp.dot(hh, wrl_ref[...], preferred_element_type=F32) + br_ref[...])

    lane = lax.broadcasted_iota(jnp.int32, (hm, LANES), 1)
    r_i = lax.broadcasted_iota(jnp.int32, (hm, hm), 0)
    c_i = lax.broadcasted_iota(jnp.int32, (hm, hm), 1)
    tri = (r_i > c_i).astype(BF16)
    before = carry_ref[...]
    for rows, lg in zip(halves, logits):
        lg = jnp.where(lane < N_EXPERTS, lg, -jnp.inf)
        vals, idxs = [], []
        for _ in range(TOP_K):
            mv = jnp.max(lg, axis=1, keepdims=True)
            ik = jnp.min(jnp.where(lg == mv, lane, LANES), axis=1, keepdims=True)
            vals.append(mv)
            idxs.append(ik)
            lg = jnp.where(lane == ik, -jnp.inf, lg)
        es = [jnp.exp(v - vals[0]) for v in vals]
        tot = es[0] + es[1] + es[2] + es[3]
        onehot = jnp.zeros((hm, LANES), F32)
        for ik in idxs:
            onehot = onehot + (lane == ik).astype(F32)
        pre = jnp.dot(tri, onehot.astype(BF16), preferred_element_type=F32) + before
        topi = jnp.zeros((hm, LANES), jnp.int32)
        topw = jnp.zeros((hm, LANES), F32)
        rank = jnp.zeros((hm, LANES), F32)
        for k in range(TOP_K):
            rk = jnp.sum(jnp.where(lane == idxs[k], pre, 0.0), axis=1, keepdims=True)
            topi = jnp.where(lane == k, idxs[k], topi)
            topw = jnp.where(lane == k, es[k] / tot, topw)
            rank = jnp.where(lane == k, rk, rank)
        topi_ref[rows, :] = topi
        topw_ref[rows, :] = topw
        rank_ref[rows, :] = rank.astype(jnp.int32)
        before = before + jnp.sum(onehot, axis=0, keepdims=True)
    carry_ref[...] = before
    cnt_ref[...] = before


def _merge(ya, dil_outs, x2, seq, w_b, b_gate, w_oa_b, w_ob_b, w_o_b, ln1_g, ln1_b, w_router, b_router):
    tokens = x2.shape[0]
    tm = MERGE_TM
    per_seq = seq // tm

    def group(a):
        dil = a.shape[1]
        return pl.BlockSpec((None, dil, tm // dil, SEG_W), lambda i: (i // per_seq, 0, i % per_seq, 0))

    gate_blk = w_b.shape[1] // (2 * D_MODEL) - 1
    wr = jnp.zeros((D_MODEL, LANES), F32).at[:, :N_EXPERTS].set(w_router)
    wr_hi = wr.astype(BF16)
    wr_lo = (wr - wr_hi.astype(F32)).astype(BF16)
    br = jnp.zeros((1, LANES), F32).at[0, :N_EXPERTS].set(b_router)
    row = lambda w: pl.BlockSpec((tm, w), lambda i: (i, 0))
    full = lambda a: pl.BlockSpec(a.shape, lambda i: (0,) * a.ndim)
    (o1, l1), (o2, l2), (o3, l3) = dil_outs
    bg = b_gate.reshape(1, -1)
    g1 = ln1_g.reshape(1, -1)
    b1 = ln1_b.reshape(1, -1)
    lane_out = lambda dt: jax.ShapeDtypeStruct((tokens, LANES), dt)
    return pl.pallas_call(
        _merge_kernel,
        grid=(tokens // tm,),
        in_specs=[row(SEG_W), group(o1), group(o2), group(o3), group(l1), group(l2), group(l3),
                  row(D_MODEL), pl.BlockSpec((D_MODEL, 2 * D_MODEL), lambda i: (0, gate_blk)),
                  full(bg), full(w_oa_b), full(w_ob_b), full(w_o_b), full(g1), full(b1),
                  full(wr_hi), full(wr_lo), full(br)],
        out_specs=[row(D_MODEL), row(D_MODEL // 2), row(LANES), row(LANES), row(LANES),
                   pl.BlockSpec((1, LANES), lambda i: (0, 0))],
        out_shape=[jax.ShapeDtypeStruct((tokens, D_MODEL), F32), jax.ShapeDtypeStruct((tokens, D_MODEL // 2), U32),
                   lane_out(jnp.int32), lane_out(F32), lane_out(jnp.int32), jax.ShapeDtypeStruct((1, LANES), F32)],
        scratch_shapes=[pltpu.VMEM((1, LANES), F32), pltpu.VMEM((6, SEG_W // LANES, tm, LANES), F32),
                        pltpu.VMEM((D_MODEL, 2 * D_MODEL), BF16)],
        compiler_params=_cparams(1),
        name="merge",
    )(ya, o1, o2, o3, l1, l2, l3, x2, w_b, bg, w_oa_b, w_ob_b, w_o_b, g1, b1, wr_hi, wr_lo, br)


def _sc_mesh():
    return plsc.VectorSubcoreMesh(core_axis_name="core", subcore_axis_name="subcore")


def _sc_scatter_rows(table, src_idx, dst_idx):
    n = src_idx.shape[0]
    d = table.shape[1]
    mesh = _sc_mesh()
    workers = mesh.num_cores * mesh.num_subcores
    per = n // (SC_CHUNK * workers)
    assert per * SC_CHUNK * workers == n

    @pl.kernel(out_type=jax.ShapeDtypeStruct((n, d), table.dtype), mesh=mesh,
               scratch_types=[pltpu.VMEM((1, SC_CHUNK), jnp.int32), pltpu.VMEM((1, SC_CHUNK), jnp.int32),
                              pltpu.VMEM((SC_CHUNK, d), table.dtype)])
    def copy(t_hbm, s_hbm, d_hbm, o_hbm, s_vm, d_vm, buf):
        wid = lax.axis_index("core") * mesh.num_subcores + lax.axis_index("subcore")

        @pl.loop(0, per)
        def _(j):
            blk = wid * per + j
            pltpu.sync_copy(s_hbm.at[pl.ds(blk, 1)], s_vm)
            pltpu.sync_copy(d_hbm.at[pl.ds(blk, 1)], d_vm)
            pltpu.sync_copy(t_hbm.at[s_vm.at[0]], buf)
            pltpu.sync_copy(buf, o_hbm.at[d_vm.at[0]])

    return copy(table, src_idx.reshape(-1, SC_CHUNK), dst_idx.reshape(-1, SC_CHUNK))


def _sc_gather_rows(table, idx):
    n = idx.shape[0]
    d = table.shape[1]
    mesh = _sc_mesh()
    workers = mesh.num_cores * mesh.num_subcores
    per = n // (SC_CHUNK * workers)
    assert per * SC_CHUNK * workers == n

    @pl.kernel(out_type=jax.ShapeDtypeStruct((n, d), table.dtype), mesh=mesh,
               scratch_types=[pltpu.VMEM((1, SC_CHUNK), jnp.int32), pltpu.VMEM((SC_CHUNK, d), table.dtype)])
    def gather(t_hbm, i_hbm, o_hbm, i_vm, buf):
        wid = lax.axis_index("core") * mesh.num_subcores + lax.axis_index("subcore")

        @pl.loop(0, per)
        def _(j):
            blk = wid * per + j
            pltpu.sync_copy(i_hbm.at[pl.ds(blk, 1)], i_vm)
            pltpu.sync_copy(t_hbm.at[i_vm.at[0]], buf)
            pltpu.sync_copy(buf, o_hbm.at[pl.ds(blk * SC_CHUNK, SC_CHUNK)])

    return gather(table, idx.reshape(-1, SC_CHUNK))


def _expert_kernel(be_ref, nx_ref, nu_ref, x_ref, wgu_hbm, bgu_ref, wd_hbm, bd_ref, y_ref,
                   wgu_stage, wd_stage, wgu_b, wd_b, sem):
    i = pl.program_id(0)
    used = i < nu_ref[0]
    expert = be_ref[i]

    def weight_copies(e):
        return (pltpu.make_async_copy(wgu_hbm.at[e], wgu_stage, sem.at[0]),
                pltpu.make_async_copy(wd_hbm.at[e], wd_stage, sem.at[1]))

    @pl.when(i == 0)
    def _():
        for cp in weight_copies(expert):
            cp.start()

    @pl.when(used & ((i == 0) | (expert != be_ref[jnp.maximum(i - 1, 0)])))
    def _():
        for cp in weight_copies(expert):
            cp.wait()
        wgu_b[...] = wgu_stage[...].astype(BF16)
        wd_b[...] = wd_stage[...].astype(BF16)

        @pl.when(nx_ref[i] >= 0)
        def _():
            for cp in weight_copies(nx_ref[i]):
                cp.start()

    @pl.when(used)
    def _():
        lo, hi = _unpack_bf16_pairs(x_ref[...])
        xb = jnp.concatenate([lo, hi], axis=1).astype(BF16)
        chunk = 512
        cols = [slice(c * chunk, (c + 1) * chunk) for c in range(D_FF // chunk)]
        pre = []
        for gs in cols:
            us = slice(D_FF + gs.start, D_FF + gs.stop)
            gate = jnp.dot(xb, wgu_b[:, gs], preferred_element_type=F32) + bgu_ref[:, gs]
            up = jnp.dot(xb, wgu_b[:, us], preferred_element_type=F32) + bgu_ref[:, us]
            pre.append((gate, up))
        acc = jnp.zeros((x_ref.shape[0], D_MODEL), F32)
        for gs, (gate, up) in zip(cols, pre):
            gate = jnp.minimum(gate, SWIGLU_LIMIT)
            up = jnp.clip(up, -SWIGLU_LIMIT, SWIGLU_LIMIT)
            act = (up + 1.0) * gate * jax.nn.sigmoid(SWIGLU_ALPHA * gate)
            acc = acc + jnp.dot(act.astype(BF16), wd_b[gs, :], preferred_element_type=F32)
        y_ref[...] = _pack_bf16_pairs(acc + bd_ref[...])

    @pl.when(i >= nu_ref[0])
    def _():
        y_ref[...] = jnp.zeros_like(y_ref)


def _experts(xrows, block_e, next_e, n_used, w_gu, b_gu, w_down, b_down):
    n_rows = xrows.shape[0]
    bm = MOE_BM
    bias = lambda i, be, nx, nu: (be[i], 0, 0)
    return pl.pallas_call(
        _expert_kernel,
        grid_spec=pltpu.PrefetchScalarGridSpec(
            num_scalar_prefetch=3,
            grid=(n_rows // bm,),
            in_specs=[
                pl.BlockSpec((bm, D_MODEL // 2), lambda i, be, nx, nu: (i, 0)),
                pl.BlockSpec(memory_space=pl.ANY),
                pl.BlockSpec((None, 1, 2 * D_FF), bias),
                pl.BlockSpec(memory_space=pl.ANY),
                pl.BlockSpec((None, 1, D_MODEL), bias),
            ],
            out_specs=pl.BlockSpec((bm, D_MODEL // 2), lambda i, be, nx, nu: (i, 0)),
            scratch_shapes=[pltpu.VMEM((D_MODEL, 2 * D_FF), F32), pltpu.VMEM((D_FF, D_MODEL), F32),
                            pltpu.VMEM((D_MODEL, 2 * D_FF), BF16), pltpu.VMEM((D_FF, D_MODEL), BF16),
                            pltpu.SemaphoreType.DMA((2,))],
        ),
        out_shape=jax.ShapeDtypeStruct((n_rows, D_MODEL // 2), U32),
        compiler_params=_cparams(1),
        name="experts",
    )(block_e, next_e, n_used, xrows, w_gu, b_gu.reshape(N_EXPERTS, 1, -1), w_down, b_down.reshape(N_EXPERTS, 1, -1))


def _combine_kernel(y0_ref, y1_ref, y2_ref, y3_ref, w_ref, h_ref, g_ref, b_ref, o_ref):
    w = w_ref[...]
    half = D_MODEL // 2
    f_lo = jnp.zeros((h_ref.shape[0], half), F32)
    f_hi = jnp.zeros((h_ref.shape[0], half), F32)
    for k, y_ref in enumerate((y0_ref, y1_ref, y2_ref, y3_ref)):
        lo, hi = _unpack_bf16_pairs(y_ref[...])
        f_lo = f_lo + lo * w[:, k:k + 1]
        f_hi = f_hi + hi * w[:, k:k + 1]
    f = jnp.concatenate([f_lo, f_hi], axis=1)
    o_ref[...] = _layer_norm(DN_ALPHA * h_ref[...] + f, g_ref[...], b_ref[...])


def _combine(yplanes, topw, h, ln2_g, ln2_b):
    tokens = h.shape[0]
    tm = MOE_TM
    per_plane = tokens // tm
    plane = lambda k: pl.BlockSpec((tm, D_MODEL // 2), lambda i: (k * per_plane + i, 0))
    return pl.pallas_call(
        _combine_kernel,
        grid=(per_plane,),
        in_specs=[plane(0), plane(1), plane(2), plane(3),
                  pl.BlockSpec((tm, LANES), lambda i: (i, 0)),
                  pl.BlockSpec((tm, D_MODEL), lambda i: (i, 0)),
                  pl.BlockSpec((1, D_MODEL), lambda i: (0, 0)),
                  pl.BlockSpec((1, D_MODEL), lambda i: (0, 0))],
        out_specs=pl.BlockSpec((tm, D_MODEL), lambda i: (i, 0)),
        out_shape=jax.ShapeDtypeStruct((tokens, D_MODEL), F32),
        compiler_params=_cparams(1),
        name="combine",
    )(yplanes, yplanes, yplanes, yplanes, topw, h, ln2_g.reshape(1, -1), ln2_b.reshape(1, -1))


def _routing_tables(topi, rank, cnt, tokens):
    bm = MOE_BM
    i32 = jnp.int32
    experts = jnp.arange(N_EXPERTS, dtype=i32)
    counts = cnt[0, :N_EXPERTS].astype(i32)
    padded = (counts + bm - 1) // bm * bm
    pend = jnp.cumsum(padded)
    pstart = pend - padded
    sel = topi[:, :TOP_K, None] == experts[None, None, :]
    dest = rank[:, :TOP_K] + jnp.sum(jnp.where(sel, pstart[None, None, :], 0), axis=-1)
    n_pad = N_EXPERTS * bm
    n_rows = tokens * TOP_K + n_pad
    starts = jnp.arange(n_rows // bm, dtype=i32) * bm
    block_e = jnp.minimum(jnp.sum((pend[None, :] <= starts[:, None]).astype(i32), axis=1), N_EXPERTS - 1)
    n_used = (pend[-1] // bm).reshape(1)
    is_block_e = block_e[:, None] == experts[None, :]
    after = jnp.sum(jnp.where(is_block_e, pend[None, :], 0), axis=1) // bm
    e_after = jnp.sum(jnp.where(after[:, None] == jnp.arange(n_rows // bm, dtype=i32)[None, :], block_e[None, :], 0),
                      axis=1)
    next_e = jnp.where(after < n_used[0], e_after, -1)
    pad_cnt = padded - counts
    pad_end = jnp.cumsum(pad_cnt)
    j = jnp.arange(n_pad, dtype=i32)
    owner = jnp.sum((pad_end[None, :] <= j[:, None]).astype(i32), axis=1)
    is_owner = owner[:, None] == experts[None, :]
    pick = lambda v: jnp.sum(jnp.where(is_owner, v[None, :], 0), axis=1)
    in_expert = pick(pstart + counts) + (j - pick(pad_end - pad_cnt))
    pad_rows = jnp.where(owner < N_EXPERTS, in_expert, pend[-1] + (j - pad_end[-1]))
    src_tok = jnp.concatenate([jnp.repeat(jnp.arange(tokens, dtype=i32), TOP_K), j % tokens])
    dst_row = jnp.concatenate([dest.reshape(-1), pad_rows]).astype(i32)
    return dest.astype(i32), block_e.astype(i32), next_e.astype(i32), n_used.astype(i32), src_tok, dst_row


def kernel(x, w_in, b_gate, lam_q1, lam_k1, lam_q2, lam_k2, subln_g, w_oa, w_ob, w_o, ln1_g, ln1_b,
           w_router, b_router, w_gu, b_gu, w_down, b_down, ln2_g, ln2_b):
    batch, seq, d = x.shape
    tokens = batch * seq
    h = x.reshape(tokens, d)
    for l in range(DEPTH):
        q_scale = HEAD_DIM ** -0.5 * math.log2(math.e)
        qk_a, vt_a = _project_qkv(h, w_in[l], 0, seq, "qkv_diff", q_scale=q_scale, v_feature_major=True)
        ya = _diff_attention(qk_a.reshape(batch, seq, 2 * SEG_W), vt_a, lam_q1[l], lam_k1[l], lam_q2[l],
                             lam_k2[l], subln_g[l]).reshape(tokens, -1)
        dil = []
        for g, (_, dilation) in enumerate(DIL_PAIRS):
            qkv_g = _project_qkv(h, w_in[l], g + 1, seq, f"qkv_dil{g}", q_scale=q_scale, dil=dilation)
            dil.append(_dilated_group(qkv_g, batch, seq, g))
        h1, h1p, topi, topw, rank, cnt = _merge(ya, dil, h, seq, w_in[l], b_gate[l], w_oa[l].astype(BF16),
                                                w_ob[l].astype(BF16), w_o[l].astype(BF16), ln1_g[l], ln1_b[l],
                                                w_router[l], b_router[l])
        dest, block_e, next_e, n_used, src_tok, dst_row = _routing_tables(topi, rank, cnt, tokens)
        xrows = _sc_scatter_rows(h1p, src_tok, dst_row)
        yrows = _experts(xrows, block_e, next_e, n_used, w_gu[l], b_gu[l], w_down[l], b_down[l])
        yplanes = _sc_gather_rows(yrows, dest.T.reshape(-1))
        h = _combine(yplanes, topw, h1, ln2_g[l], ln2_b[l])
    return h.reshape(batch, seq, d)
```

```python
import functools
import math

import jax
import jax.numpy as jnp
from jax import lax
from jax.experimental import pallas as pl
from jax.experimental.pallas import tpu as pltpu
from jax.experimental.pallas import tpu_sc as plsc

F32 = jnp.float32
BF16 = jnp.bfloat16
U32 = jnp.uint32

D_MODEL = 1024
HEAD_DIM = 64
ROT_DIM = HEAD_DIM // 4
ROPE_THETA = 500000.0
QBLK = 128
DA_HEADS = 4
DIL_PAIRS = ((128, 1), (512, 4), (2048, 16))
SEG_W = 512
QKV_W = 3 * SEG_W
N_EXPERTS = 32
TOP_K = 4
D_FF = D_MODEL
SWIGLU_ALPHA = 1.702
SWIGLU_LIMIT = 7.0
DEPTH = 1
DN_ALPHA = (2 * DEPTH) ** 0.25
EPS = 1e-5
LAMBDA_INIT = 0.8 - 0.6 * math.exp(-0.3 * 0)

LANES = 128
QKV_TM = 1024
QKV_CHUNK = 256
ATT_TQ = 256
ONES_ROWS = 16
DIL_NQ = 4
MERGE_TM = 512
MOE_BM = 256
MOE_TM = 512
SC_CHUNK = 128
VMEM_LIMIT = 52 * 1024 * 1024


def _cparams(n_axes):
    return pltpu.CompilerParams(dimension_semantics=("arbitrary",) * n_axes,
                                vmem_limit_bytes=VMEM_LIMIT)


def _qkv_kernel(x_ref, w_ref, cs_ref, o_ref, *rest, q_scale, dil):
    tm = x_ref.shape[0]
    n = tm // dil
    if dil > 1:
        *maybe_vt_ref, wb_ref, slab_ref, xp_ref = rest
    else:
        *maybe_vt_ref, wb_ref = rest

    @pl.when(pl.program_id(0) == 0)
    def _():
        wb_ref[...] = w_ref[...].astype(BF16)

    if dil > 1:
        for c in range(D_MODEL // LANES):
            slab_ref[c] = x_ref[:, c * LANES:(c + 1) * LANES]
        for r in range(dil):
            for c in range(D_MODEL // LANES):
                xp_ref[r * n:(r + 1) * n, c * LANES:(c + 1) * LANES] = (
                    slab_ref[c, pl.ds(r, n, stride=dil), :].astype(BF16))

    def store(rows, cols, val):
        if dil == 1:
            o_ref[rows, cols] = val
        elif n >= QKV_CHUNK:
            o_ref[rows.start // n, rows.start % n:rows.start % n + QKV_CHUNK, cols] = val
        else:
            per = QKV_CHUNK // n
            for q in range(per):
                o_ref[rows.start // n + q, :, cols] = val[q * n:(q + 1) * n]

    half = ROT_DIM // 2
    in_head = lax.broadcasted_iota(jnp.int32, (QKV_CHUNK, LANES), 1) % HEAD_DIM
    first, second = in_head < half, (in_head >= half) & (in_head < ROT_DIM)
    for rc in range(tm // QKV_CHUNK):
        rows = slice(rc * QKV_CHUNK, (rc + 1) * QKV_CHUNK)
        xb = xp_ref[rows, :] if dil > 1 else x_ref[rows, :].astype(BF16)
        cs = cs_ref[rows, :]
        c = jnp.where(first, cs, jnp.where(second, pltpu.roll(cs, half, 1), 1.0))
        s1 = jnp.where(second, cs, 0.0)
        s2 = jnp.where(first, -pltpu.roll(cs, LANES - half, 1), 0.0)
        for seg in range(3):
            acc = jnp.dot(xb, wb_ref[:, seg * SEG_W:(seg + 1) * SEG_W], preferred_element_type=F32)
            if seg == 2:
                if maybe_vt_ref:
                    maybe_vt_ref[0][:, rows] = acc.T.astype(BF16)
                else:
                    store(rows, slice(seg * SEG_W, (seg + 1) * SEG_W), acc.astype(BF16))
                continue
            for k in range(SEG_W // LANES):
                t = acc[:, k * LANES:(k + 1) * LANES]
                r = t * c + pltpu.roll(t, ROT_DIM // 2, 1) * s1 + pltpu.roll(t, LANES - ROT_DIM // 2, 1) * s2
                if seg == 0:
                    r = r * q_scale
                lo = seg * SEG_W + k * LANES
                store(rows, slice(lo, lo + LANES), r.astype(BF16))


def _rope_lane_table(seq, dil, tm):
    row = jnp.arange(seq, dtype=jnp.int32)
    n = tm // dil
    in_tile = row % tm
    pos = (row - in_tile + (in_tile % n) * dil + in_tile // n).astype(F32)
    inv_freq = ROPE_THETA ** (-jnp.arange(0, ROT_DIM, 2, dtype=F32) / ROT_DIM)
    ang = pos[:, None] * inv_freq[None, :]
    head = jnp.concatenate([jnp.cos(ang), jnp.sin(ang), jnp.zeros((seq, HEAD_DIM - ROT_DIM), F32)], axis=1)
    return jnp.tile(head, (1, LANES // HEAD_DIM))


def _project_qkv(x2, w, col_block, seq, name, q_scale, dil=1, v_feature_major=False):
    tokens = x2.shape[0]
    batch = tokens // seq
    tm = min(QKV_TM, seq)
    per_seq = seq // tm
    n = tm // dil
    out_w = 2 * SEG_W if v_feature_major else QKV_W
    scratch = [pltpu.VMEM((D_MODEL, QKV_W), BF16)]
    if dil == 1:
        out_specs = [pl.BlockSpec((tm, out_w), lambda i: (i, 0))]
        out_shape = [jax.ShapeDtypeStruct((tokens, out_w), BF16)]
    else:
        out_specs = [pl.BlockSpec((None, dil, n, out_w), lambda i: (i // per_seq, 0, i % per_seq, 0))]
        out_shape = [jax.ShapeDtypeStruct((batch, dil, seq // dil, out_w), BF16)]
        scratch += [pltpu.VMEM((D_MODEL // LANES, tm, LANES), F32), pltpu.VMEM((tm, D_MODEL), BF16)]
    if v_feature_major:
        out_specs.append(pl.BlockSpec((None, SEG_W, tm), lambda i: (i // per_seq, 0, i % per_seq)))
        out_shape.append(jax.ShapeDtypeStruct((batch, SEG_W, seq), BF16))
    outs = pl.pallas_call(
        functools.partial(_qkv_kernel, q_scale=q_scale, dil=dil),
        grid=(tokens // tm,),
        in_specs=[pl.BlockSpec((tm, D_MODEL), lambda i: (i, 0)),
                  pl.BlockSpec((D_MODEL, QKV_W), lambda i: (0, col_block)),
                  pl.BlockSpec((tm, LANES), lambda i: (i % per_seq, 0))],
        out_specs=out_specs,
        out_shape=out_shape,
        scratch_shapes=scratch,
        compiler_params=_cparams(1),
        name=name,
    )(x2, w, _rope_lane_table(seq, dil, tm))
    return outs if v_feature_major else outs[0]


def _nt_dot(a, b):
    return lax.dot_general(a, b, (((1,), (1,)), ((), ())), preferred_element_type=F32)


def _diff_kernel(q_ref, k_ref, vt_ref, lq1_ref, lk1_ref, lq2_ref, lk2_ref, g_ref, o_ref, s_ref, acc_ref, *, tq):
    i = pl.program_id(1)
    chains = 2 * DA_HEADS
    half = tq // 2
    lane = lax.broadcasted_iota(jnp.int32, (tq, LANES), 1)
    qs = []
    for h in range(DA_HEADS):
        q = q_ref[:, h * LANES:(h + 1) * LANES]
        zero = jnp.zeros_like(q)
        qs += [jnp.where(lane < HEAD_DIM, q, zero), jnp.where(lane >= HEAD_DIM, q, zero)]
    acc_ref[...] = jnp.zeros_like(acc_ref)

    def score_chain(j, buf, c):
        start = pl.multiple_of(j * tq, tq)
        kb = k_ref[pl.ds(start, tq), (c // 2) * LANES:(c // 2 + 1) * LANES]
        buf[c] = _nt_dot(kb, qs[c])

    def score_block(j, buf):
        for c in range(chains):
            score_chain(j, buf, c)

    def absorb(j, buf, state, masked, nxt=None):
        start = pl.multiple_of(j * tq, tq)
        out = []
        for c in range(chains):
            if nxt is not None:
                score_chain(j + 1, nxt, c)
            m, l = state[c]
            if masked:
                key = lax.broadcasted_iota(jnp.int32, (tq, tq), 0)
                qry = lax.broadcasted_iota(jnp.int32, (tq, tq), 1)
                buf[c] = jnp.where(key <= qry, buf[c], -jnp.inf)
            m_new = jnp.maximum(m, jnp.max(buf[c], axis=0, keepdims=True))
            a = jnp.exp2(m - m_new)
            p = [jnp.exp2((buf[c, u * half:(u + 1) * half, :] - m_new).astype(BF16)) for u in range(2)]
            vt = vt_ref[(c // 2) * LANES:(c // 2 + 1) * LANES, pl.ds(start, tq)]
            vt1 = jnp.concatenate([vt, jnp.ones((ONES_ROWS, tq), BF16)], axis=0)
            pv = jnp.dot(vt1, jnp.concatenate(p, axis=0), preferred_element_type=F32)
            l = a * l + pv[LANES:LANES + 1, :]
            acc_ref[c] = a * acc_ref[c] + pv[:LANES, :]
            out.append((m_new, l))
        return tuple(out)

    buf_a, buf_b = s_ref.at[0], s_ref.at[1]

    def two_blocks(jj, state):
        j = 2 * jj
        state = absorb(j, buf_a, state, False, nxt=buf_b)
        return absorb(j + 1, buf_b, state, False, nxt=buf_a)

    def last_from_a(state):
        return absorb(i, buf_a, state, True)

    def last_from_b(state):
        state = absorb(i - 1, buf_a, state, False, nxt=buf_b)
        return absorb(i, buf_b, state, True)

    init = tuple((jnp.full((1, tq), -jnp.inf, F32), jnp.zeros((1, tq), F32)) for _ in range(chains))
    score_block(0, buf_a)
    state = lax.fori_loop(0, i // 2, two_blocks, init)
    state = lax.cond(lax.rem(i, 2) == 0, last_from_a, last_from_b, state)

    lam = (jnp.exp(jnp.sum(lq1_ref[...] * lk1_ref[...], axis=1, keepdims=True))
           - jnp.exp(jnp.sum(lq2_ref[...] * lk2_ref[...], axis=1, keepdims=True)) + LAMBDA_INIT)
    for h in range(DA_HEADS):
        l1, l2 = state[2 * h][1], state[2 * h + 1][1]
        o = (acc_ref[2 * h] / l1 - lam * (acc_ref[2 * h + 1] / l2)).T
        ms = jnp.mean(o * o, axis=1, keepdims=True)
        o = o * lax.rsqrt(ms + EPS) * g_ref[...]
        o_ref[:, h * LANES:(h + 1) * LANES] = (o * (1.0 - LAMBDA_INIT)).astype(BF16)


def _diff_attention(qk3, vt3, lam_q1, lam_k1, lam_q2, lam_k2, subln_g):
    b, s, _ = qk3.shape
    tq = min(ATT_TQ, s)
    vec = lambda n: pl.BlockSpec((1, n), lambda bb, i: (0, 0))
    return pl.pallas_call(
        functools.partial(_diff_kernel, tq=tq),
        grid=(b, s // tq),
        in_specs=[
            pl.BlockSpec((None, tq, SEG_W), lambda bb, i: (bb, i, 0)),
            pl.BlockSpec((None, s, SEG_W), lambda bb, i: (bb, 0, 1)),
            pl.BlockSpec((None, SEG_W, s), lambda bb, i: (bb, 0, 0)),
            vec(HEAD_DIM), vec(HEAD_DIM), vec(HEAD_DIM), vec(HEAD_DIM), vec(2 * HEAD_DIM),
        ],
        out_specs=pl.BlockSpec((None, tq, SEG_W), lambda bb, i: (bb, i, 0)),
        out_shape=jax.ShapeDtypeStruct((b, s, SEG_W), BF16),
        scratch_shapes=[pltpu.VMEM((2, 2 * DA_HEADS, tq, tq), F32), pltpu.VMEM((2 * DA_HEADS, LANES, tq), F32)],
        compiler_params=_cparams(2),
        name="diff_attn",
    )(qk3, qk3, vt3, lam_q1.reshape(1, -1), lam_k1.reshape(1, -1), lam_q2.reshape(1, -1),
      lam_k2.reshape(1, -1), subln_g.reshape(1, -1))


def _dil_kernel(q_ref, kp_ref, kc_ref, vp_ref, vc_ref, o_ref, lse_ref, *, rel, nq):
    n = pl.program_id(2)
    qi = lax.broadcasted_iota(jnp.int32, (QBLK, 2 * QBLK), 0)
    kj = lax.broadcasted_iota(jnp.int32, (QBLK, 2 * QBLK), 1)
    dist = qi + QBLK - kj
    band = (dist >= 0) & (dist <= rel)
    lane = lax.broadcasted_iota(jnp.int32, (QBLK, LANES), 1)
    low = lane < HEAD_DIM
    slabs = [slice(p * LANES, (p + 1) * LANES) for p in range(SEG_W // LANES)]

    def window(prev_ref, cur_ref, s, sl):
        if s == 0:
            return jnp.concatenate([prev_ref[:, sl], cur_ref[:QBLK, sl]], axis=0)
        return cur_ref[(s - 1) * QBLK:(s + 1) * QBLK, sl]

    def score_block(s):
        rows = slice(s * QBLK, (s + 1) * QBLK)
        scores = []
        for sl in slabs:
            q2 = q_ref[rows, sl]
            k2 = window(kp_ref, kc_ref, s, sl)
            for sel in (low, jnp.logical_not(low)):
                scores.append(_nt_dot(jnp.where(sel, q2, jnp.zeros_like(q2)), k2))
        return scores

    def finish_block(s, scores):
        rows = slice(s * QBLK, (s + 1) * QBLK)
        valid = band & ((kj >= QBLK) | (n * nq + s > 0))
        probs = []
        for sc in scores:
            sc = jnp.where(valid, sc, -jnp.inf)
            m = jnp.max(sc, axis=1, keepdims=True)
            probs.append((jnp.exp2((sc - m).astype(BF16)), m))
        one = jnp.ones((2 * QBLK, LANES), BF16)
        low_keys = lax.broadcasted_iota(jnp.int32, (2 * QBLK, LANES), 1) < HEAD_DIM
        for p, sl in enumerate(slabs):
            v2 = window(vp_ref, vc_ref, s, sl)
            (pe_lo, m_lo), (pe_hi, m_hi) = probs[2 * p:2 * p + 2]
            pv_lo = jnp.dot(pe_lo, jnp.where(low_keys, v2, one), preferred_element_type=F32)
            pv_hi = jnp.dot(pe_hi, jnp.where(low_keys, one, v2), preferred_element_type=F32)
            num = jnp.where(low, pv_lo, pv_hi)
            den = pltpu.roll(jnp.where(low, pv_hi, pv_lo), HEAD_DIM, 1)
            o_ref[rows, sl] = (num / den).astype(BF16)
            lse_ref[rows, sl] = jnp.where(low, m_lo, m_hi) + jnp.log2(den)

    pending = score_block(0)
    for s in range(1, nq):
        nxt = score_block(s)
        finish_block(s - 1, pending)
        pending = nxt
    finish_block(nq - 1, pending)


def _dilated_group(qkv, batch, seq, group):
    window, dil = DIL_PAIRS[group]
    rel = window // dil
    length = seq // dil
    nb = length // QBLK
    nq = math.gcd(nb, DIL_NQ)
    view = qkv.reshape(batch, dil, length, QKV_W)

    def spec(seg, prev):
        if prev:
            return pl.BlockSpec((None, None, QBLK, SEG_W), lambda b, r, n: (b, r, jnp.maximum(n * nq - 1, 0), seg))
        return pl.BlockSpec((None, None, nq * QBLK, SEG_W), lambda b, r, n: (b, r, n, seg))

    out_spec = pl.BlockSpec((None, None, nq * QBLK, SEG_W), lambda b, r, n: (b, r, n, 0))
    o, lse = pl.pallas_call(
        functools.partial(_dil_kernel, rel=rel, nq=nq),
        grid=(batch, dil, nb // nq),
        in_specs=[spec(0, False), spec(1, True), spec(1, False), spec(2, True), spec(2, False)],
        out_specs=[out_spec, out_spec],
        out_shape=[jax.ShapeDtypeStruct((batch, dil, length, SEG_W), BF16),
                   jax.ShapeDtypeStruct((batch, dil, length, SEG_W), F32)],
        compiler_params=_cparams(3),
        name=f"dilated{group}",
    )(view, view, view, view, view)
    return o, lse


def _layer_norm(z, g, b):
    mu = jnp.mean(z, axis=1, keepdims=True)
    zc = z - mu
    var = jnp.mean(zc * zc, axis=1, keepdims=True)
    return zc * lax.rsqrt(var + EPS) * g + b


def _pack_bf16_pairs(v):
    n = v.shape[1] // 2
    lo = pltpu.bitcast(v[:, :n].astype(BF16).astype(F32), U32)
    hi = pltpu.bitcast(v[:, n:].astype(BF16).astype(F32), U32)
    return jnp.bitwise_or(jnp.right_shift(lo, jnp.uint32(16)), jnp.bitwise_and(hi, jnp.uint32(0xFFFF0000)))


def _unpack_bf16_pairs(w):
    lo = pltpu.bitcast(jnp.left_shift(w, jnp.uint32(16)), F32)
    hi = pltpu.bitcast(jnp.bitwise_and(w, jnp.uint32(0xFFFF0000)), F32)
    return lo, hi


def _merge_kernel(ya_ref, o1_ref, o2_ref, o3_ref, l1_ref, l2_ref, l3_ref, x_ref,
                  wg_ref, bg_ref, woa_ref, wob_ref, wo_ref, g_ref, b_ref, wrh_ref, wrl_ref, br_ref,
                  h_ref, hp_ref, topi_ref, topw_ref, rank_ref, cnt_ref, carry_ref, scr_ref, wgb_ref):
    i = pl.program_id(0)
    tm = x_ref.shape[0]

    @pl.when(i == 0)
    def _():
        carry_ref[...] = jnp.zeros_like(carry_ref)
        wgb_ref[...] = wg_ref[...].astype(BF16)

    hm = tm // 2
    halves = (slice(0, hm), slice(hm, tm))
    xs = [x_ref[rows, :] for rows in halves]
    gate_dot = lambda x: jnp.dot(x.astype(BF16), wgb_ref[...], preferred_element_type=F32) + bg_ref[...]
    gate_pre = [gate_dot(xs[0])]
    pas = [jnp.dot(ya_ref[rows, :], woa_ref[...], preferred_element_type=F32) for rows in halves]

    def token_major(ref, scr):
        dil, n = ref.shape[0], ref.shape[1]
        if dil == 1:
            return ref[0].astype(F32)
        for r in range(dil):
            blk = ref[r].astype(F32)
            for c in range(SEG_W // LANES):
                scr[c, pl.ds(r, n, stride=dil), :] = blk[:, c * LANES:(c + 1) * LANES]
        return jnp.concatenate([scr[c] for c in range(SEG_W // LANES)], axis=1)

    oa, ob, oc = (token_major(r, scr_ref.at[k]) for k, r in enumerate((o1_ref, o2_ref, o3_ref)))
    la, lb, lc = (token_major(r, scr_ref.at[3 + k]) for k, r in enumerate((l1_ref, l2_ref, l3_ref)))
    mx = jnp.maximum(jnp.maximum(la, lb), lc)
    ea, eb, ec = jnp.exp2(la - mx), jnp.exp2(lb - mx), jnp.exp2(lc - mx)
    yb = ((ea * oa + eb * ob + ec * oc) / (ea + eb + ec)).astype(BF16)
    pbs = [jnp.dot(yb[rows, :], wob_ref[...], preferred_element_type=F32) for rows in halves]
    gate_pre.append(gate_dot(xs[1]))
    mixes = []
    for pre_act, pa, pb in zip(gate_pre, pas, pbs):
        gates = jax.nn.sigmoid(pre_act)
        merged = gates[:, :D_MODEL] * pa + gates[:, D_MODEL:] * pb
        mixes.append(jnp.dot(merged.astype(BF16), wo_ref[...], preferred_element_type=F32))
    logits = []
    for rows, x, mix in zip(halves, xs, mixes):
        h = _layer_norm(DN_ALPHA * x + mix, g_ref[...], b_ref[...])
        h_ref[rows, :] = h
        hp_ref[rows, :] = _pack_bf16_pairs(h)
        hh = h.astype(BF16)
        hl = (h - hh.astype(F32)).astype(BF16)
        logits.append(jnp.dot(hh, wrh_ref[...], preferred_element_type=F32)
                      + jnp.dot(hl, wrh_ref[...], preferred_element_type=F32)
                      + jnp.dot(hh, wrl_ref[...], preferred_element_type=F32) + br_ref[...])

    lane = lax.broadcasted_iota(jnp.int32, (hm, LANES), 1)
    r_i = lax.broadcasted_iota(jnp.int32, (hm, hm), 0)
    c_i = lax.broadcasted_iota(jnp.int32, (hm, hm), 1)
    tri = (r_i > c_i).astype(BF16)
    before = carry_ref[...]
    for rows, lg in zip(halves, logits):
        lg = jnp.where(lane < N_EXPERTS, lg, -jnp.inf)
        vals, idxs = [], []
        for _ in range(TOP_K):
            mv = jnp.max(lg, axis=1, keepdims=True)
            ik = jnp.min(jnp.where(lg == mv, lane, LANES), axis=1, keepdims=True)
            vals.append(mv)
            idxs.append(ik)
            lg = jnp.where(lane == ik, -jnp.inf, lg)
        es = [jnp.exp(v - vals[0]) for v in vals]
        tot = es[0] + es[1] + es[2] + es[3]
        onehot = jnp.zeros((hm, LANES), F32)
        for ik in idxs:
            onehot = onehot + (lane == ik).astype(F32)
        pre = jnp.dot(tri, onehot.astype(BF16), preferred_element_type=F32) + before
        topi = jnp.zeros((hm, LANES), jnp.int32)
        topw = jnp.zeros((hm, LANES), F32)
        rank = jnp.zeros((hm, LANES), F32)
        for k in range(TOP_K):
            rk = jnp.sum(jnp.where(lane == idxs[k], pre, 0.0), axis=1, keepdims=True)
            topi = jnp.where(lane == k, idxs[k], topi)
            topw = jnp.where(lane == k, es[k] / tot, topw)
            rank = jnp.where(lane == k, rk, rank)
        topi_ref[rows, :] = topi
        topw_ref[rows, :] = topw
        rank_ref[rows, :] = rank.astype(jnp.int32)
        before = before + jnp.sum(onehot, axis=0, keepdims=True)
    carry_ref[...] = before
    cnt_ref[...] = before


def _merge(ya, dil_outs, x2, seq, w_b, b_gate, w_oa_b, w_ob_b, w_o_b, ln1_g, ln1_b, w_router, b_router):
    tokens = x2.shape[0]
    tm = MERGE_TM
    per_seq = seq // tm

    def group(a):
        dil = a.shape[1]
        return pl.BlockSpec((None, dil, tm // dil, SEG_W), lambda i: (i // per_seq, 0, i % per_seq, 0))

    gate_blk = w_b.shape[1] // (2 * D_MODEL) - 1
    wr = jnp.zeros((D_MODEL, LANES), F32).at[:, :N_EXPERTS].set(w_router)
    wr_hi = wr.astype(BF16)
    wr_lo = (wr - wr_hi.astype(F32)).astype(BF16)
    br = jnp.zeros((1, LANES), F32).at[0, :N_EXPERTS].set(b_router)
    row = lambda w: pl.BlockSpec((tm, w), lambda i: (i, 0))
    full = lambda a: pl.BlockSpec(a.shape, lambda i: (0,) * a.ndim)
    (o1, l1), (o2, l2), (o3, l3) = dil_outs
    bg = b_gate.reshape(1, -1)
    g1 = ln1_g.reshape(1, -1)
    b1 = ln1_b.reshape(1, -1)
    lane_out = lambda dt: jax.ShapeDtypeStruct((tokens, LANES), dt)
    return pl.pallas_call(
        _merge_kernel,
        grid=(tokens // tm,),
        in_specs=[row(SEG_W), group(o1), group(o2), group(o3), group(l1), group(l2), group(l3),
                  row(D_MODEL), pl.BlockSpec((D_MODEL, 2 * D_MODEL), lambda i: (0, gate_blk)),
                  full(bg), full(w_oa_b), full(w_ob_b), full(w_o_b), full(g1), full(b1),
                  full(wr_hi), full(wr_lo), full(br)],
        out_specs=[row(D_MODEL), row(D_MODEL // 2), row(LANES), row(LANES), row(LANES),
                   pl.BlockSpec((1, LANES), lambda i: (0, 0))],
        out_shape=[jax.ShapeDtypeStruct((tokens, D_MODEL), F32), jax.ShapeDtypeStruct((tokens, D_MODEL // 2), U32),
                   lane_out(jnp.int32), lane_out(F32), lane_out(jnp.int32), jax.ShapeDtypeStruct((1, LANES), F32)],
        scratch_shapes=[pltpu.VMEM((1, LANES), F32), pltpu.VMEM((6, SEG_W // LANES, tm, LANES), F32),
                        pltpu.VMEM((D_MODEL, 2 * D_MODEL), BF16)],
        compiler_params=_cparams(1),
        name="merge",
    )(ya, o1, o2, o3, l1, l2, l3, x2, w_b, bg, w_oa_b, w_ob_b, w_o_b, g1, b1, wr_hi, wr_lo, br)


def _sc_mesh():
    return plsc.VectorSubcoreMesh(core_axis_name="core", subcore_axis_name="subcore")


def _sc_scatter_rows(table, src_idx, dst_idx):
    n = src_idx.shape[0]
    d = table.shape[1]
    mesh = _sc_mesh()
    workers = mesh.num_cores * mesh.num_subcores
    per = n // (SC_CHUNK * workers)
    assert per * SC_CHUNK * workers == n

    @pl.kernel(out_type=jax.ShapeDtypeStruct((n, d), table.dtype), mesh=mesh,
               scratch_types=[pltpu.VMEM((1, SC_CHUNK), jnp.int32), pltpu.VMEM((1, SC_CHUNK), jnp.int32),
                              pltpu.VMEM((SC_CHUNK, d), table.dtype)])
    def copy(t_hbm, s_hbm, d_hbm, o_hbm, s_vm, d_vm, buf):
        wid = lax.axis_index("core") * mesh.num_subcores + lax.axis_index("subcore")

        @pl.loop(0, per)
        def _(j):
            blk = wid * per + j
            pltpu.sync_copy(s_hbm.at[pl.ds(blk, 1)], s_vm)
            pltpu.sync_copy(d_hbm.at[pl.ds(blk, 1)], d_vm)
            pltpu.sync_copy(t_hbm.at[s_vm.at[0]], buf)
            pltpu.sync_copy(buf, o_hbm.at[d_vm.at[0]])

    return copy(table, src_idx.reshape(-1, SC_CHUNK), dst_idx.reshape(-1, SC_CHUNK))


def _sc_gather_rows(table, idx):
    n = idx.shape[0]
    d = table.shape[1]
    mesh = _sc_mesh()
    workers = mesh.num_cores * mesh.num_subcores
    per = n // (SC_CHUNK * workers)
    assert per * SC_CHUNK * workers == n

    @pl.kernel(out_type=jax.ShapeDtypeStruct((n, d), table.dtype), mesh=mesh,
               scratch_types=[pltpu.VMEM((1, SC_CHUNK), jnp.int32), pltpu.VMEM((SC_CHUNK, d), table.dtype)])
    def gather(t_hbm, i_hbm, o_hbm, i_vm, buf):
        wid = lax.axis_index("core") * mesh.num_subcores + lax.axis_index("subcore")

        @pl.loop(0, per)
        def _(j):
            blk = wid * per + j
            pltpu.sync_copy(i_hbm.at[pl.ds(blk, 1)], i_vm)
            pltpu.sync_copy(t_hbm.at[i_vm.at[0]], buf)
            pltpu.sync_copy(buf, o_hbm.at[pl.ds(blk * SC_CHUNK, SC_CHUNK)])

    return gather(table, idx.reshape(-1, SC_CHUNK))


def _expert_kernel(be_ref, nx_ref, nu_ref, x_ref, wgu_hbm, bgu_ref, wd_hbm, bd_ref, y_ref,
                   wgu_stage, wd_stage, wgu_b, wd_b, sem):
    i = pl.program_id(0)
    used = i < nu_ref[0]
    expert = be_ref[i]

    def weight_copies(e):
        return (pltpu.make_async_copy(wgu_hbm.at[e], wgu_stage, sem.at[0]),
                pltpu.make_async_copy(wd_hbm.at[e], wd_stage, sem.at[1]))

    @pl.when(i == 0)
    def _():
        for cp in weight_copies(expert):
            cp.start()

    @pl.when(used & ((i == 0) | (expert != be_ref[jnp.maximum(i - 1, 0)])))
    def _():
        for cp in weight_copies(expert):
            cp.wait()
        wgu_b[...] = wgu_stage[...].astype(BF16)
        wd_b[...] = wd_stage[...].astype(BF16)

        @pl.when(nx_ref[i] >= 0)
        def _():
            for cp in weight_copies(nx_ref[i]):
                cp.start()

    @pl.when(used)
    def _():
        lo, hi = _unpack_bf16_pairs(x_ref[...])
        xb = jnp.concatenate([lo, hi], axis=1).astype(BF16)
        chunk = 512
        cols = [slice(c * chunk, (c + 1) * chunk) for c in range(D_FF // chunk)]
        pre = []
        for gs in cols:
            us = slice(D_FF + gs.start, D_FF + gs.stop)
            gate = jnp.dot(xb, wgu_b[:, gs], preferred_element_type=F32) + bgu_ref[:, gs]
            up = jnp.dot(xb, wgu_b[:, us], preferred_element_type=F32) + bgu_ref[:, us]
            pre.append((gate, up))
        acc = jnp.zeros((x_ref.shape[0], D_MODEL), F32)
        for gs, (gate, up) in zip(cols, pre):
            gate = jnp.minimum(gate, SWIGLU_LIMIT)
            up = jnp.clip(up, -SWIGLU_LIMIT, SWIGLU_LIMIT)
            act = (up + 1.0) * gate * jax.nn.sigmoid(SWIGLU_ALPHA * gate)
            acc = acc + jnp.dot(act.astype(BF16), wd_b[gs, :], preferred_element_type=F32)
        y_ref[...] = _pack_bf16_pairs(acc + bd_ref[...])

    @pl.when(i >= nu_ref[0])
    def _():
        y_ref[...] = jnp.zeros_like(y_ref)


def _experts(xrows, block_e, next_e, n_used, w_gu, b_gu, w_down, b_down):
    n_rows = xrows.shape[0]
    bm = MOE_BM
    bias = lambda i, be, nx, nu: (be[i], 0, 0)
    return pl.pallas_call(
        _expert_kernel,
        grid_spec=pltpu.PrefetchScalarGridSpec(
            num_scalar_prefetch=3,
            grid=(n_rows // bm,),
            in_specs=[
                pl.BlockSpec((bm, D_MODEL // 2), lambda i, be, nx, nu: (i, 0)),
                pl.BlockSpec(memory_space=pl.ANY),
                pl.BlockSpec((None, 1, 2 * D_FF), bias),
                pl.BlockSpec(memory_space=pl.ANY),
                pl.BlockSpec((None, 1, D_MODEL), bias),
            ],
            out_specs=pl.BlockSpec((bm, D_MODEL // 2), lambda i, be, nx, nu: (i, 0)),
            scratch_shapes=[pltpu.VMEM((D_MODEL, 2 * D_FF), F32), pltpu.VMEM((D_FF, D_MODEL), F32),
                            pltpu.VMEM((D_MODEL, 2 * D_FF), BF16), pltpu.VMEM((D_FF, D_MODEL), BF16),
                            pltpu.SemaphoreType.DMA((2,))],
        ),
        out_shape=jax.ShapeDtypeStruct((n_rows, D_MODEL // 2), U32),
        compiler_params=_cparams(1),
        name="experts",
    )(block_e, next_e, n_used, xrows, w_gu, b_gu.reshape(N_EXPERTS, 1, -1), w_down, b_down.reshape(N_EXPERTS, 1, -1))


def _combine_kernel(y0_ref, y1_ref, y2_ref, y3_ref, w_ref, h_ref, g_ref, b_ref, o_ref):
    w = w_ref[...]
    half = D_MODEL // 2
    f_lo = jnp.zeros((h_ref.shape[0], half), F32)
    f_hi = jnp.zeros((h_ref.shape[0], half), F32)
    for k, y_ref in enumerate((y0_ref, y1_ref, y2_ref, y3_ref)):
        lo, hi = _unpack_bf16_pairs(y_ref[...])
        f_lo = f_lo + lo * w[:, k:k + 1]
        f_hi = f_hi + hi * w[:, k:k + 1]
    f = jnp.concatenate([f_lo, f_hi], axis=1)
    o_ref[...] = _layer_norm(DN_ALPHA * h_ref[...] + f, g_ref[...], b_ref[...])


def _combine(yplanes, topw, h, ln2_g, ln2_b):
    tokens = h.shape[0]
    tm = MOE_TM
    per_plane = tokens // tm
    plane = lambda k: pl.BlockSpec((tm, D_MODEL // 2), lambda i: (k * per_plane + i, 0))
    return pl.pallas_call(
        _combine_kernel,
        grid=(per_plane,),
        in_specs=[plane(0), plane(1), plane(2), plane(3),
                  pl.BlockSpec((tm, LANES), lambda i: (i, 0)),
                  pl.BlockSpec((tm, D_MODEL), lambda i: (i, 0)),
                  pl.BlockSpec((1, D_MODEL), lambda i: (0, 0)),
                  pl.BlockSpec((1, D_MODEL), lambda i: (0, 0))],
        out_specs=pl.BlockSpec((tm, D_MODEL), lambda i: (i, 0)),
        out_shape=jax.ShapeDtypeStruct((tokens, D_MODEL), F32),
        compiler_params=_cparams(1),
        name="combine",
    )(yplanes, yplanes, yplanes, yplanes, topw, h, ln2_g.reshape(1, -1), ln2_b.reshape(1, -1))


def _routing_tables(topi, rank, cnt, tokens):
    bm = MOE_BM
    i32 = jnp.int32
    experts = jnp.arange(N_EXPERTS, dtype=i32)
    counts = cnt[0, :N_EXPERTS].astype(i32)
    padded = (counts + bm - 1) // bm * bm
    pend = jnp.cumsum(padded)
    pstart = pend - padded
    sel = topi[:, :TOP_K, None] == experts[None, None, :]
    dest = rank[:, :TOP_K] + jnp.sum(jnp.where(sel, pstart[None, None, :], 0), axis=-1)
    n_pad = N_EXPERTS * bm
    n_rows = tokens * TOP_K + n_pad
    starts = jnp.arange(n_rows // bm, dtype=i32) * bm
    block_e = jnp.minimum(jnp.sum((pend[None, :] <= starts[:, None]).astype(i32), axis=1), N_EXPERTS - 1)
    n_used = (pend[-1] // bm).reshape(1)
    is_block_e = block_e[:, None] == experts[None, :]
    after = jnp.sum(jnp.where(is_block_e, pend[None, :], 0), axis=1) // bm
    e_after = jnp.sum(jnp.where(after[:, None] == jnp.arange(n_rows // bm, dtype=i32)[None, :], block_e[None, :], 0),
                      axis=1)
    next_e = jnp.where(after < n_used[0], e_after, -1)
    pad_cnt = padded - counts
    pad_end = jnp.cumsum(pad_cnt)
    j = jnp.arange(n_pad, dtype=i32)
    owner = jnp.sum((pad_end[None, :] <= j[:, None]).astype(i32), axis=1)
    is_owner = owner[:, None] == experts[None, :]
    pick = lambda v: jnp.sum(jnp.where(is_owner, v[None, :], 0), axis=1)
    in_expert = pick(pstart + counts) + (j - pick(pad_end - pad_cnt))
    pad_rows = jnp.where(owner < N_EXPERTS, in_expert, pend[-1] + (j - pad_end[-1]))
    src_tok = jnp.concatenate([jnp.repeat(jnp.arange(tokens, dtype=i32), TOP_K), j % tokens])
    dst_row = jnp.concatenate([dest.reshape(-1), pad_rows]).astype(i32)
    return dest.astype(i32), block_e.astype(i32), next_e.astype(i32), n_used.astype(i32), src_tok, dst_row


def kernel(x, w_in, b_gate, lam_q1, lam_k1, lam_q2, lam_k2, subln_g, w_oa, w_ob, w_o, ln1_g, ln1_b,
           w_router, b_router, w_gu, b_gu, w_down, b_down, ln2_g, ln2_b):
    batch, seq, d = x.shape
    tokens = batch * seq
    h = x.reshape(tokens, d)
    for l in range(DEPTH):
        q_scale = HEAD_DIM ** -0.5 * math.log2(math.e)
        qk_a, vt_a = _project_qkv(h, w_in[l], 0, seq, "qkv_diff", q_scale=q_scale, v_feature_major=True)
        ya = _diff_attention(qk_a.reshape(batch, seq, 2 * SEG_W), vt_a, lam_q1[l], lam_k1[l], lam_q2[l],
                             lam_k2[l], subln_g[l]).reshape(tokens, -1)
        dil = []
        for g, (_, dilation) in enumerate(DIL_PAIRS):
            qkv_g = _project_qkv(h, w_in[l], g + 1, seq, f"qkv_dil{g}", q_scale=q_scale, dil=dilation)
            dil.append(_dilated_group(qkv_g, batch, seq, g))
        h1, h1p, topi, topw, rank, cnt = _merge(ya, dil, h, seq, w_in[l], b_gate[l], w_oa[l].astype(BF16),
                                                w_ob[l].astype(BF16), w_o[l].astype(BF16), ln1_g[l], ln1_b[l],
                                                w_router[l], b_router[l])
        dest, block_e, next_e, n_used, src_tok, dst_row = _routing_tables(topi, rank, cnt, tokens)
        xrows = _sc_scatter_rows(h1p, src_tok, dst_row)
        yrows = _experts(xrows, block_e, next_e, n_used, w_gu[l], b_gu[l], w_down[l], b_down[l])
        yplanes = _sc_gather_rows(yrows, dest.T.reshape(-1))
        h = _combine(yplanes, topw, h1, ln2_g[l], ln2_b[l])
    return h.reshape(batch, seq, d)
```

```python
import functools
import math

import jax
import jax.numpy as jnp
from jax import lax
from jax.experimental import pallas as pl
from jax.experimental.pallas import tpu as pltpu
from jax.experimental.pallas import tpu_sc as plsc

F32 = jnp.float32
BF16 = jnp.bfloat16
U32 = jnp.uint32

D_MODEL = 1024
HEAD_DIM = 64
ROT_DIM = HEAD_DIM // 4
ROPE_THETA = 500000.0
QBLK = 128
DA_HEADS = 4
DIL_PAIRS = ((128, 1), (512, 4), (2048, 16))
SEG_W = 512
QKV_W = 3 * SEG_W
N_EXPERTS = 32
TOP_K = 4
D_FF = D_MODEL
SWIGLU_ALPHA = 1.702
SWIGLU_LIMIT = 7.0
DEPTH = 1
DN_ALPHA = (2 * DEPTH) ** 0.25
EPS = 1e-5
LAMBDA_INIT = 0.8 - 0.6 * math.exp(-0.3 * 0)

LANES = 128
QKV_TM = 1024
QKV_CHUNK = 256
ATT_TQ = 256
ONES_ROWS = 16
DIL_NQ = 4
MERGE_TM = 512
MOE_BM = 256
MOE_TM = 512
SC_CHUNK = 128
VMEM_LIMIT = 52 * 1024 * 1024


def _cparams(n_axes):
    return pltpu.CompilerParams(dimension_semantics=("arbitrary",) * n_axes,
                                vmem_limit_bytes=VMEM_LIMIT)


def _qkv_kernel(x_ref, w_ref, cs_ref, o_ref, *rest, q_scale, dil):
    tm = x_ref.shape[0]
    n = tm // dil
    if dil > 1:
        *maybe_vt_ref, wb_ref, slab_ref = rest
    else:
        *maybe_vt_ref, wb_ref = rest

    @pl.when(pl.program_id(0) == 0)
    def _():
        wb_ref[...] = w_ref[...].astype(BF16)

    if dil > 1:
        for c in range(D_MODEL // LANES):
            slab_ref[c] = x_ref[:, c * LANES:(c + 1) * LANES]

    def regrouped_rows(start):
        pieces = []
        for p in range(start, start + QKV_CHUNK, min(n, QKV_CHUNK)):
            r, l0 = p // n, p % n
            pieces.append(jnp.concatenate(
                [slab_ref[c, pl.ds(l0 * dil + r, min(n, QKV_CHUNK), stride=dil), :] for c in range(D_MODEL // LANES)],
                axis=1))
        return jnp.concatenate(pieces, axis=0).astype(BF16)

    def store(rows, cols, val):
        if dil == 1:
            o_ref[rows, cols] = val
        elif n >= QKV_CHUNK:
            o_ref[rows.start // n, rows.start % n:rows.start % n + QKV_CHUNK, cols] = val
        else:
            per = QKV_CHUNK // n
            for q in range(per):
                o_ref[rows.start // n + q, :, cols] = val[q * n:(q + 1) * n]

    half = ROT_DIM // 2
    in_head = lax.broadcasted_iota(jnp.int32, (QKV_CHUNK, LANES), 1) % HEAD_DIM
    first, second = in_head < half, (in_head >= half) & (in_head < ROT_DIM)
    for rc in range(tm // QKV_CHUNK):
        rows = slice(rc * QKV_CHUNK, (rc + 1) * QKV_CHUNK)
        xb = regrouped_rows(rows.start) if dil > 1 else x_ref[rows, :].astype(BF16)
        cs = cs_ref[rows, :]
        c = jnp.where(first, cs, jnp.where(second, pltpu.roll(cs, half, 1), 1.0))
        s1 = jnp.where(second, cs, 0.0)
        s2 = jnp.where(first, -pltpu.roll(cs, LANES - half, 1), 0.0)
        for seg in range(3):
            acc = jnp.dot(xb, wb_ref[:, seg * SEG_W:(seg + 1) * SEG_W], preferred_element_type=F32)
            if seg == 2:
                if maybe_vt_ref:
                    maybe_vt_ref[0][:, rows] = acc.T.astype(BF16)
                else:
                    store(rows, slice(seg * SEG_W, (seg + 1) * SEG_W), acc.astype(BF16))
                continue
            for k in range(SEG_W // LANES):
                t = acc[:, k * LANES:(k + 1) * LANES]
                r = t * c + pltpu.roll(t, ROT_DIM // 2, 1) * s1 + pltpu.roll(t, LANES - ROT_DIM // 2, 1) * s2
                if seg == 0:
                    r = r * q_scale
                lo = seg * SEG_W + k * LANES
                store(rows, slice(lo, lo + LANES), r.astype(BF16))


def _rope_lane_table(seq, dil, tm):
    row = jnp.arange(seq, dtype=jnp.int32)
    n = tm // dil
    in_tile = row % tm
    pos = (row - in_tile + (in_tile % n) * dil + in_tile // n).astype(F32)
    inv_freq = ROPE_THETA ** (-jnp.arange(0, ROT_DIM, 2, dtype=F32) / ROT_DIM)
    ang = pos[:, None] * inv_freq[None, :]
    head = jnp.concatenate([jnp.cos(ang), jnp.sin(ang), jnp.zeros((seq, HEAD_DIM - ROT_DIM), F32)], axis=1)
    return jnp.tile(head, (1, LANES // HEAD_DIM))


def _project_qkv(x2, w, col_block, seq, name, q_scale, dil=1, v_feature_major=False):
    tokens = x2.shape[0]
    batch = tokens // seq
    tm = min(QKV_TM, seq)
    per_seq = seq // tm
    n = tm // dil
    out_w = 2 * SEG_W if v_feature_major else QKV_W
    scratch = [pltpu.VMEM((D_MODEL, QKV_W), BF16)]
    if dil == 1:
        out_specs = [pl.BlockSpec((tm, out_w), lambda i: (i, 0))]
        out_shape = [jax.ShapeDtypeStruct((tokens, out_w), BF16)]
    else:
        out_specs = [pl.BlockSpec((None, dil, n, out_w), lambda i: (i // per_seq, 0, i % per_seq, 0))]
        out_shape = [jax.ShapeDtypeStruct((batch, dil, seq // dil, out_w), BF16)]
        scratch += [pltpu.VMEM((D_MODEL // LANES, tm, LANES), F32)]
    if v_feature_major:
        out_specs.append(pl.BlockSpec((None, SEG_W, tm), lambda i: (i // per_seq, 0, i % per_seq)))
        out_shape.append(jax.ShapeDtypeStruct((batch, SEG_W, seq), BF16))
    outs = pl.pallas_call(
        functools.partial(_qkv_kernel, q_scale=q_scale, dil=dil),
        grid=(tokens // tm,),
        in_specs=[pl.BlockSpec((tm, D_MODEL), lambda i: (i, 0)),
                  pl.BlockSpec((D_MODEL, QKV_W), lambda i: (0, col_block)),
                  pl.BlockSpec((tm, LANES), lambda i: (i % per_seq, 0))],
        out_specs=out_specs,
        out_shape=out_shape,
        scratch_shapes=scratch,
        compiler_params=_cparams(1),
        name=name,
    )(x2, w, _rope_lane_table(seq, dil, tm))
    return outs if v_feature_major else outs[0]


def _nt_dot(a, b):
    return lax.dot_general(a, b, (((1,), (1,)), ((), ())), preferred_element_type=F32)


def _diff_kernel(q_ref, k_ref, vt_ref, lq1_ref, lk1_ref, lq2_ref, lk2_ref, g_ref, o_ref, s_ref, acc_ref, *, tq):
    i = pl.program_id(1)
    chains = 2 * DA_HEADS
    half = tq // 2
    lane = lax.broadcasted_iota(jnp.int32, (tq, LANES), 1)
    qs = []
    for h in range(DA_HEADS):
        q = q_ref[:, h * LANES:(h + 1) * LANES]
        zero = jnp.zeros_like(q)
        qs += [jnp.where(lane < HEAD_DIM, q, zero), jnp.where(lane >= HEAD_DIM, q, zero)]
    acc_ref[...] = jnp.zeros_like(acc_ref)

    def score_chain(j, buf, c):
        start = pl.multiple_of(j * tq, tq)
        kb = k_ref[pl.ds(start, tq), (c // 2) * LANES:(c // 2 + 1) * LANES]
        buf[c] = _nt_dot(kb, qs[c])

    def score_block(j, buf):
        for c in range(chains):
            score_chain(j, buf, c)

    def absorb(j, buf, state, masked, nxt=None):
        start = pl.multiple_of(j * tq, tq)
        out = []
        for c in range(chains):
            if nxt is not None:
                score_chain(j + 1, nxt, c)
            m, l = state[c]
            if masked:
                key = lax.broadcasted_iota(jnp.int32, (tq, tq), 0)
                qry = lax.broadcasted_iota(jnp.int32, (tq, tq), 1)
                buf[c] = jnp.where(key <= qry, buf[c], -jnp.inf)
            m_new = jnp.maximum(m, jnp.max(buf[c], axis=0, keepdims=True))
            a = jnp.exp2(m - m_new)
            p = [jnp.exp2((buf[c, u * half:(u + 1) * half, :] - m_new).astype(BF16)) for u in range(2)]
            vt = vt_ref[(c // 2) * LANES:(c // 2 + 1) * LANES, pl.ds(start, tq)]
            vt1 = jnp.concatenate([vt, jnp.ones((ONES_ROWS, tq), BF16)], axis=0)
            pv = jnp.dot(vt1, jnp.concatenate(p, axis=0), preferred_element_type=F32)
            l = a * l + pv[LANES:LANES + 1, :]
            acc_ref[c] = a * acc_ref[c] + pv[:LANES, :]
            out.append((m_new, l))
        return tuple(out)

    buf_a, buf_b = s_ref.at[0], s_ref.at[1]

    def two_blocks(jj, state):
        j = 2 * jj
        state = absorb(j, buf_a, state, False, nxt=buf_b)
        return absorb(j + 1, buf_b, state, False, nxt=buf_a)

    def last_from_a(state):
        return absorb(i, buf_a, state, True)

    def last_from_b(state):
        state = absorb(i - 1, buf_a, state, False, nxt=buf_b)
        return absorb(i, buf_b, state, True)

    init = tuple((jnp.full((1, tq), -jnp.inf, F32), jnp.zeros((1, tq), F32)) for _ in range(chains))
    score_block(0, buf_a)
    state = lax.fori_loop(0, i // 2, two_blocks, init)
    state = lax.cond(lax.rem(i, 2) == 0, last_from_a, last_from_b, state)

    lam = (jnp.exp(jnp.sum(lq1_ref[...] * lk1_ref[...], axis=1, keepdims=True))
           - jnp.exp(jnp.sum(lq2_ref[...] * lk2_ref[...], axis=1, keepdims=True)) + LAMBDA_INIT)
    for h in range(DA_HEADS):
        l1, l2 = state[2 * h][1], state[2 * h + 1][1]
        o = (acc_ref[2 * h] / l1 - lam * (acc_ref[2 * h + 1] / l2)).T
        ms = jnp.mean(o * o, axis=1, keepdims=True)
        o = o * lax.rsqrt(ms + EPS) * g_ref[...]
        o_ref[:, h * LANES:(h + 1) * LANES] = (o * (1.0 - LAMBDA_INIT)).astype(BF16)


def _diff_attention(qk3, vt3, lam_q1, lam_k1, lam_q2, lam_k2, subln_g):
    b, s, _ = qk3.shape
    tq = min(ATT_TQ, s)
    vec = lambda n: pl.BlockSpec((1, n), lambda bb, i: (0, 0))
    return pl.pallas_call(
        functools.partial(_diff_kernel, tq=tq),
        grid=(b, s // tq),
        in_specs=[
            pl.BlockSpec((None, tq, SEG_W), lambda bb, i: (bb, i, 0)),
            pl.BlockSpec((None, s, SEG_W), lambda bb, i: (bb, 0, 1)),
            pl.BlockSpec((None, SEG_W, s), lambda bb, i: (bb, 0, 0)),
            vec(HEAD_DIM), vec(HEAD_DIM), vec(HEAD_DIM), vec(HEAD_DIM), vec(2 * HEAD_DIM),
        ],
        out_specs=pl.BlockSpec((None, tq, SEG_W), lambda bb, i: (bb, i, 0)),
        out_shape=jax.ShapeDtypeStruct((b, s, SEG_W), BF16),
        scratch_shapes=[pltpu.VMEM((2, 2 * DA_HEADS, tq, tq), F32), pltpu.VMEM((2 * DA_HEADS, LANES, tq), F32)],
        compiler_params=_cparams(2),
        name="diff_attn",
    )(qk3, qk3, vt3, lam_q1.reshape(1, -1), lam_k1.reshape(1, -1), lam_q2.reshape(1, -1),
      lam_k2.reshape(1, -1), subln_g.reshape(1, -1))


def _dil_kernel(q_ref, kp_ref, kc_ref, vp_ref, vc_ref, o_ref, lse_ref, *, rel, nq, nres):
    n = pl.program_id(2)
    qi = lax.broadcasted_iota(jnp.int32, (QBLK, 2 * QBLK), 0)
    kj = lax.broadcasted_iota(jnp.int32, (QBLK, 2 * QBLK), 1)
    dist = qi + QBLK - kj
    band = (dist >= 0) & (dist <= rel)
    lane = lax.broadcasted_iota(jnp.int32, (QBLK, LANES), 1)
    low = lane < HEAD_DIM
    slabs = [slice(p * LANES, (p + 1) * LANES) for p in range(SEG_W // LANES)]

    def window(prev_ref, cur_ref, s, sl):
        if s == 0:
            return jnp.concatenate([prev_ref[:, sl], cur_ref[:QBLK, sl]], axis=0)
        return cur_ref[(s - 1) * QBLK:(s + 1) * QBLK, sl]

    def score_block(blk):
        res, s = blk
        rows = slice(s * QBLK, (s + 1) * QBLK)
        scores = []
        for sl in slabs:
            q2 = q_ref[res, rows, sl]
            k2 = window(kp_ref.at[res], kc_ref.at[res], s, sl)
            for sel in (low, jnp.logical_not(low)):
                scores.append(_nt_dot(jnp.where(sel, q2, jnp.zeros_like(q2)), k2))
        return scores

    def finish_block(blk, scores):
        res, s = blk
        rows = slice(s * QBLK, (s + 1) * QBLK)
        valid = band & ((kj >= QBLK) | (n * nq + s > 0))
        probs = []
        for sc in scores:
            sc = jnp.where(valid, sc, -jnp.inf)
            m = jnp.max(sc, axis=1, keepdims=True)
            probs.append((jnp.exp2((sc - m).astype(BF16)), m))
        one = jnp.ones((2 * QBLK, LANES), BF16)
        low_keys = lax.broadcasted_iota(jnp.int32, (2 * QBLK, LANES), 1) < HEAD_DIM
        for p, sl in enumerate(slabs):
            v2 = window(vp_ref.at[res], vc_ref.at[res], s, sl)
            (pe_lo, m_lo), (pe_hi, m_hi) = probs[2 * p:2 * p + 2]
            pv_lo = jnp.dot(pe_lo, jnp.where(low_keys, v2, one), preferred_element_type=F32)
            pv_hi = jnp.dot(pe_hi, jnp.where(low_keys, one, v2), preferred_element_type=F32)
            num = jnp.where(low, pv_lo, pv_hi)
            den = pltpu.roll(jnp.where(low, pv_hi, pv_lo), HEAD_DIM, 1)
            o_ref[res, rows, sl] = (num / den).astype(BF16)
            lse_ref[res, rows, sl] = jnp.where(low, m_lo, m_hi) + jnp.log2(den)

    blocks = [(res, s) for res in range(nres) for s in range(nq)]
    pending = score_block(blocks[0])
    for prev, blk in zip(blocks, blocks[1:]):
        nxt = score_block(blk)
        finish_block(prev, pending)
        pending = nxt
    finish_block(blocks[-1], pending)


def _dilated_group(qkv, batch, seq, group):
    window, dil = DIL_PAIRS[group]
    rel = window // dil
    length = seq // dil
    nb = length // QBLK
    nq = math.gcd(nb, DIL_NQ)
    nres = math.gcd(dil, DIL_NQ // nq)
    view = qkv.reshape(batch, dil, length, QKV_W)

    def spec(seg, prev):
        if prev:
            return pl.BlockSpec((None, nres, QBLK, SEG_W), lambda b, r, n: (b, r, jnp.maximum(n * nq - 1, 0), seg))
        return pl.BlockSpec((None, nres, nq * QBLK, SEG_W), lambda b, r, n: (b, r, n, seg))

    out_spec = pl.BlockSpec((None, nres, nq * QBLK, SEG_W), lambda b, r, n: (b, r, n, 0))
    o, lse = pl.pallas_call(
        functools.partial(_dil_kernel, rel=rel, nq=nq, nres=nres),
        grid=(batch, dil // nres, nb // nq),
        in_specs=[spec(0, False), spec(1, True), spec(1, False), spec(2, True), spec(2, False)],
        out_specs=[out_spec, out_spec],
        out_shape=[jax.ShapeDtypeStruct((batch, dil, length, SEG_W), BF16),
                   jax.ShapeDtypeStruct((batch, dil, length, SEG_W), F32)],
        compiler_params=_cparams(3),
        name=f"dilated{group}",
    )(view, view, view, view, view)
    return o, lse


def _layer_norm(z, g, b):
    mu = jnp.mean(z, axis=1, keepdims=True)
    zc = z - mu
    var = jnp.mean(zc * zc, axis=1, keepdims=True)
    return zc * lax.rsqrt(var + EPS) * g + b


def _pack_bf16_pairs(v):
    n = v.shape[1] // 2
    lo = pltpu.bitcast(v[:, :n].astype(BF16).astype(F32), U32)
    hi = pltpu.bitcast(v[:, n:].astype(BF16).astype(F32), U32)
    return jnp.bitwise_or(jnp.right_shift(lo, jnp.uint32(16)), jnp.bitwise_and(hi, jnp.uint32(0xFFFF0000)))


def _unpack_bf16_pairs(w):
    lo = pltpu.bitcast(jnp.left_shift(w, jnp.uint32(16)), F32)
    hi = pltpu.bitcast(jnp.bitwise_and(w, jnp.uint32(0xFFFF0000)), F32)
    return lo, hi


def _merge_kernel(ya_ref, o1_ref, o2_ref, o3_ref, l1_ref, l2_ref, l3_ref, x_ref,
                  wg_ref, bg_ref, woa_ref, wob_ref, wo_ref, g_ref, b_ref, wrh_ref, wrl_ref, br_ref,
                  h_ref, hp_ref, topi_ref, topw_ref, rank_ref, cnt_ref, carry_ref, scr_ref, wgb_ref):
    i = pl.program_id(0)
    tm = x_ref.shape[0]

    @pl.when(i == 0)
    def _():
        carry_ref[...] = jnp.zeros_like(carry_ref)
        wgb_ref[...] = wg_ref[...].astype(BF16)

    hm = tm // 2
    halves = (slice(0, hm), slice(hm, tm))
    xs = [x_ref[rows, :] for rows in halves]
    xbs = [x.astype(BF16) for x in xs]
    gate_dot = lambda xb: jnp.dot(xb, wgb_ref[...], preferred_element_type=F32) + bg_ref[...]
    gate_pre = [gate_dot(xbs[0])]
    pas = [jnp.dot(ya_ref[rows, :], woa_ref[...], preferred_element_type=F32) for rows in halves]

    def token_major(ref, scr):
        dil, n = ref.shape[0], ref.shape[1]
        if dil == 1:
            return ref[0].astype(F32)
        for r in range(dil):
            blk = ref[r].astype(F32)
            for c in range(SEG_W // LANES):
                scr[c, pl.ds(r, n, stride=dil), :] = blk[:, c * LANES:(c + 1) * LANES]
        return jnp.concatenate([scr[c] for c in range(SEG_W // LANES)], axis=1)

    oa, ob, oc = (token_major(r, scr_ref.at[k]) for k, r in enumerate((o1_ref, o2_ref, o3_ref)))
    la, lb, lc = (token_major(r, scr_ref.at[3 + k]) for k, r in enumerate((l1_ref, l2_ref, l3_ref)))
    mx = jnp.maximum(jnp.maximum(la, lb), lc)
    ea, eb, ec = jnp.exp2(la - mx), jnp.exp2(lb - mx), jnp.exp2(lc - mx)
    yb = ((ea * oa + eb * ob + ec * oc) / (ea + eb + ec)).astype(BF16)
    pbs = [jnp.dot(yb[rows, :], wob_ref[...], preferred_element_type=F32) for rows in halves]
    gate_pre.append(gate_dot(xbs[1]))
    mixes = []
    for pre_act, pa, pb in zip(gate_pre, pas, pbs):
        gates = jax.nn.sigmoid(pre_act)
        merged = gates[:, :D_MODEL] * pa + gates[:, D_MODEL:] * pb
        mixes.append(jnp.dot(merged.astype(BF16), wo_ref[...], preferred_element_type=F32))
    logits = []
    for rows, x, mix in zip(halves, xs, mixes):
        h = _layer_norm(DN_ALPHA * x + mix, g_ref[...], b_ref[...])
        h_ref[rows, :] = h
        hp_ref[rows, :] = _pack_bf16_pairs(h)
        hh = h.astype(BF16)
        hl = (h - hh.astype(F32)).astype(BF16)
        logits.append(jnp.dot(hh, wrh_ref[...], preferred_element_type=F32)
                      + jnp.dot(hl, wrh_ref[...], preferred_element_type=F32)
                      + jnp.dot(hh, wrl_ref[...], preferred_element_type=F32) + br_ref[...])

    lane = lax.broadcasted_iota(jnp.int32, (hm, LANES), 1)
    r_i = lax.broadcasted_iota(jnp.int32, (hm, hm), 0)
    c_i = lax.broadcasted_iota(jnp.int32, (hm, hm), 1)
    tri = (r_i > c_i).astype(BF16)
    before = carry_ref[...]
    for rows, lg in zip(halves, logits):
        lg = jnp.where(lane < N_EXPERTS, lg, -jnp.inf)
        vals, idxs = [], []
        for _ in range(TOP_K):
            mv = jnp.max(lg, axis=1, keepdims=True)
            ik = jnp.min(jnp.where(lg == mv, lane, LANES), axis=1, keepdims=True)
            vals.append(mv)
            idxs.append(ik)
            lg = jnp.where(lane == ik, -jnp.inf, lg)
        es = [jnp.exp(v - vals[0]) for v in vals]
        tot = es[0] + es[1] + es[2] + es[3]
        onehot = jnp.zeros((hm, LANES), F32)
        for ik in idxs:
            onehot = onehot + (lane == ik).astype(F32)
        pre = jnp.dot(tri, onehot.astype(BF16), preferred_element_type=F32) + before
        topi = jnp.zeros((hm, LANES), jnp.int32)
        topw = jnp.zeros((hm, LANES), F32)
        rank = jnp.zeros((hm, LANES), F32)
        for k in range(TOP_K):
            rk = jnp.sum(jnp.where(lane == idxs[k], pre, 0.0), axis=1, keepdims=True)
            topi = jnp.where(lane == k, idxs[k], topi)
            topw = jnp.where(lane == k, es[k] / tot, topw)
            rank = jnp.where(lane == k, rk, rank)
        topi_ref[rows, :] = topi
        topw_ref[rows, :] = topw
        rank_ref[rows, :] = rank.astype(jnp.int32)
        before = before + jnp.sum(onehot, axis=0, keepdims=True)
    carry_ref[...] = before
    cnt_ref[...] = before


def _merge(ya, dil_outs, x2, seq, w_b, b_gate, w_oa_b, w_ob_b, w_o_b, ln1_g, ln1_b, w_router, b_router):
    tokens = x2.shape[0]
    tm = MERGE_TM
    per_seq = seq // tm

    def group(a):
        dil = a.shape[1]
        return pl.BlockSpec((None, dil, tm // dil, SEG_W), lambda i: (i // per_seq, 0, i % per_seq, 0))

    gate_blk = w_b.shape[1] // (2 * D_MODEL) - 1
    wr = jnp.zeros((D_MODEL, LANES), F32).at[:, :N_EXPERTS].set(w_router)
    wr_hi = wr.astype(BF16)
    wr_lo = (wr - wr_hi.astype(F32)).astype(BF16)
    br = jnp.zeros((1, LANES), F32).at[0, :N_EXPERTS].set(b_router)
    row = lambda w: pl.BlockSpec((tm, w), lambda i: (i, 0))
    full = lambda a: pl.BlockSpec(a.shape, lambda i: (0,) * a.ndim)
    (o1, l1), (o2, l2), (o3, l3) = dil_outs
    bg = b_gate.reshape(1, -1)
    g1 = ln1_g.reshape(1, -1)
    b1 = ln1_b.reshape(1, -1)
    lane_out = lambda dt: jax.ShapeDtypeStruct((tokens, LANES), dt)
    return pl.pallas_call(
        _merge_kernel,
        grid=(tokens // tm,),
        in_specs=[row(SEG_W), group(o1), group(o2), group(o3), group(l1), group(l2), group(l3),
                  row(D_MODEL), pl.BlockSpec((D_MODEL, 2 * D_MODEL), lambda i: (0, gate_blk)),
                  full(bg), full(w_oa_b), full(w_ob_b), full(w_o_b), full(g1), full(b1),
                  full(wr_hi), full(wr_lo), full(br)],
        out_specs=[row(D_MODEL), row(D_MODEL // 2), row(LANES), row(LANES), row(LANES),
                   pl.BlockSpec((1, LANES), lambda i: (0, 0))],
        out_shape=[jax.ShapeDtypeStruct((tokens, D_MODEL), F32), jax.ShapeDtypeStruct((tokens, D_MODEL // 2), U32),
                   lane_out(jnp.int32), lane_out(F32), lane_out(jnp.int32), jax.ShapeDtypeStruct((1, LANES), F32)],
        scratch_shapes=[pltpu.VMEM((1, LANES), F32), pltpu.VMEM((6, SEG_W // LANES, tm, LANES), F32),
                        pltpu.VMEM((D_MODEL, 2 * D_MODEL), BF16)],
        compiler_params=_cparams(1),
        name="merge",
    )(ya, o1, o2, o3, l1, l2, l3, x2, w_b, bg, w_oa_b, w_ob_b, w_o_b, g1, b1, wr_hi, wr_lo, br)


def _sc_mesh():
    return plsc.VectorSubcoreMesh(core_axis_name="core", subcore_axis_name="subcore")


def _sc_scatter_rows(table, src_idx, dst_idx):
    n = src_idx.shape[0]
    d = table.shape[1]
    mesh = _sc_mesh()
    workers = mesh.num_cores * mesh.num_subcores
    per = n // (SC_CHUNK * workers)
    assert per * SC_CHUNK * workers == n

    @pl.kernel(out_type=jax.ShapeDtypeStruct((n, d), table.dtype), mesh=mesh,
               scratch_types=[pltpu.VMEM((1, SC_CHUNK), jnp.int32), pltpu.VMEM((1, SC_CHUNK), jnp.int32),
                              pltpu.VMEM((SC_CHUNK, d), table.dtype)])
    def copy(t_hbm, s_hbm, d_hbm, o_hbm, s_vm, d_vm, buf):
        wid = lax.axis_index("core") * mesh.num_subcores + lax.axis_index("subcore")

        @pl.loop(0, per)
        def _(j):
            blk = wid * per + j
            pltpu.sync_copy(s_hbm.at[pl.ds(blk, 1)], s_vm)
            pltpu.sync_copy(d_hbm.at[pl.ds(blk, 1)], d_vm)
            pltpu.sync_copy(t_hbm.at[s_vm.at[0]], buf)
            pltpu.sync_copy(buf, o_hbm.at[d_vm.at[0]])

    return copy(table, src_idx.reshape(-1, SC_CHUNK), dst_idx.reshape(-1, SC_CHUNK))


def _sc_gather_rows(table, idx):
    n = idx.shape[0]
    d = table.shape[1]
    mesh = _sc_mesh()
    workers = mesh.num_cores * mesh.num_subcores
    per = n // (SC_CHUNK * workers)
    assert per * SC_CHUNK * workers == n

    @pl.kernel(out_type=jax.ShapeDtypeStruct((n, d), table.dtype), mesh=mesh,
               scratch_types=[pltpu.VMEM((1, SC_CHUNK), jnp.int32), pltpu.VMEM((SC_CHUNK, d), table.dtype)])
    def gather(t_hbm, i_hbm, o_hbm, i_vm, buf):
        wid = lax.axis_index("core") * mesh.num_subcores + lax.axis_index("subcore")

        @pl.loop(0, per)
        def _(j):
            blk = wid * per + j
            pltpu.sync_copy(i_hbm.at[pl.ds(blk, 1)], i_vm)
            pltpu.sync_copy(t_hbm.at[i_vm.at[0]], buf)
            pltpu.sync_copy(buf, o_hbm.at[pl.ds(blk * SC_CHUNK, SC_CHUNK)])

    return gather(table, idx.reshape(-1, SC_CHUNK))


def _expert_kernel(be_ref, nx_ref, nu_ref, x_ref, wgu_hbm, bgu_ref, wd_hbm, bd_ref, y_ref,
                   wgu_stage, wd_stage, wgu_b, wd_b, sem):
    i = pl.program_id(0)
    used = i < nu_ref[0]
    expert = be_ref[i]

    def weight_copies(e):
        return (pltpu.make_async_copy(wgu_hbm.at[e], wgu_stage, sem.at[0]),
                pltpu.make_async_copy(wd_hbm.at[e], wd_stage, sem.at[1]))

    @pl.when(i == 0)
    def _():
        for cp in weight_copies(expert):
            cp.start()

    @pl.when(used & ((i == 0) | (expert != be_ref[jnp.maximum(i - 1, 0)])))
    def _():
        for cp in weight_copies(expert):
            cp.wait()
        wgu_b[...] = wgu_stage[...].astype(BF16)
        wd_b[...] = wd_stage[...].astype(BF16)

        @pl.when(nx_ref[i] >= 0)
        def _():
            for cp in weight_copies(nx_ref[i]):
                cp.start()

    @pl.when(used)
    def _():
        lo, hi = _unpack_bf16_pairs(x_ref[...])
        xb = jnp.concatenate([lo, hi], axis=1).astype(BF16)
        chunk = 512
        cols = [slice(c * chunk, (c + 1) * chunk) for c in range(D_FF // chunk)]
        pre = []
        for gs in cols:
            us = slice(D_FF + gs.start, D_FF + gs.stop)
            gate = jnp.dot(xb, wgu_b[:, gs], preferred_element_type=F32) + bgu_ref[:, gs]
            up = jnp.dot(xb, wgu_b[:, us], preferred_element_type=F32) + bgu_ref[:, us]
            pre.append((gate, up))
        acc = jnp.zeros((x_ref.shape[0], D_MODEL), F32)
        for gs, (gate, up) in zip(cols, pre):
            gate = jnp.minimum(gate, SWIGLU_LIMIT)
            up = jnp.clip(up, -SWIGLU_LIMIT, SWIGLU_LIMIT)
            act = (up + 1.0) * gate * jax.nn.sigmoid(SWIGLU_ALPHA * gate)
            acc = acc + jnp.dot(act.astype(BF16), wd_b[gs, :], preferred_element_type=F32)
        y_ref[...] = _pack_bf16_pairs(acc + bd_ref[...])

    @pl.when(i >= nu_ref[0])
    def _():
        y_ref[...] = jnp.zeros_like(y_ref)


def _experts(xrows, block_e, next_e, n_used, w_gu, b_gu, w_down, b_down):
    n_rows = xrows.shape[0]
    bm = MOE_BM
    bias = lambda i, be, nx, nu: (be[i], 0, 0)
    return pl.pallas_call(
        _expert_kernel,
        grid_spec=pltpu.PrefetchScalarGridSpec(
            num_scalar_prefetch=3,
            grid=(n_rows // bm,),
            in_specs=[
                pl.BlockSpec((bm, D_MODEL // 2), lambda i, be, nx, nu: (i, 0)),
                pl.BlockSpec(memory_space=pl.ANY),
                pl.BlockSpec((None, 1, 2 * D_FF), bias),
                pl.BlockSpec(memory_space=pl.ANY),
                pl.BlockSpec((None, 1, D_MODEL), bias),
            ],
            out_specs=pl.BlockSpec((bm, D_MODEL // 2), lambda i, be, nx, nu: (i, 0)),
            scratch_shapes=[pltpu.VMEM((D_MODEL, 2 * D_FF), F32), pltpu.VMEM((D_FF, D_MODEL), F32),
                            pltpu.VMEM((D_MODEL, 2 * D_FF), BF16), pltpu.VMEM((D_FF, D_MODEL), BF16),
                            pltpu.SemaphoreType.DMA((2,))],
        ),
        out_shape=jax.ShapeDtypeStruct((n_rows, D_MODEL // 2), U32),
        compiler_params=_cparams(1),
        name="experts",
    )(block_e, next_e, n_used, xrows, w_gu, b_gu.reshape(N_EXPERTS, 1, -1), w_down, b_down.reshape(N_EXPERTS, 1, -1))


def _combine_kernel(y0_ref, y1_ref, y2_ref, y3_ref, w_ref, h_ref, g_ref, b_ref, o_ref):
    w = w_ref[...]
    half = D_MODEL // 2
    f_lo = jnp.zeros((h_ref.shape[0], half), F32)
    f_hi = jnp.zeros((h_ref.shape[0], half), F32)
    for k, y_ref in enumerate((y0_ref, y1_ref, y2_ref, y3_ref)):
        lo, hi = _unpack_bf16_pairs(y_ref[...])
        f_lo = f_lo + lo * w[:, k:k + 1]
        f_hi = f_hi + hi * w[:, k:k + 1]
    f = jnp.concatenate([f_lo, f_hi], axis=1)
    o_ref[...] = _layer_norm(DN_ALPHA * h_ref[...] + f, g_ref[...], b_ref[...])


def _combine(yplanes, topw, h, ln2_g, ln2_b):
    tokens = h.shape[0]
    tm = MOE_TM
    per_plane = tokens // tm
    plane = lambda k: pl.BlockSpec((tm, D_MODEL // 2), lambda i: (k * per_plane + i, 0))
    return pl.pallas_call(
        _combine_kernel,
        grid=(per_plane,),
        in_specs=[plane(0), plane(1), plane(2), plane(3),
                  pl.BlockSpec((tm, LANES), lambda i: (i, 0)),
                  pl.BlockSpec((tm, D_MODEL), lambda i: (i, 0)),
                  pl.BlockSpec((1, D_MODEL), lambda i: (0, 0)),
                  pl.BlockSpec((1, D_MODEL), lambda i: (0, 0))],
        out_specs=pl.BlockSpec((tm, D_MODEL), lambda i: (i, 0)),
        out_shape=jax.ShapeDtypeStruct((tokens, D_MODEL), F32),
        compiler_params=_cparams(1),
        name="combine",
    )(yplanes, yplanes, yplanes, yplanes, topw, h, ln2_g.reshape(1, -1), ln2_b.reshape(1, -1))


def _routing_tables(topi, rank, cnt, tokens):
    bm = MOE_BM
    i32 = jnp.int32
    experts = jnp.arange(N_EXPERTS, dtype=i32)
    counts = cnt[0, :N_EXPERTS].astype(i32)
    padded = (counts + bm - 1) // bm * bm
    pend = jnp.cumsum(padded)
    pstart = pend - padded
    sel = topi[:, :TOP_K, None] == experts[None, None, :]
    dest = rank[:, :TOP_K] + jnp.sum(jnp.where(sel, pstart[None, None, :], 0), axis=-1)
    n_pad = N_EXPERTS * bm
    n_rows = tokens * TOP_K + n_pad
    starts = jnp.arange(n_rows // bm, dtype=i32) * bm
    block_e = jnp.minimum(jnp.sum((pend[None, :] <= starts[:, None]).astype(i32), axis=1), N_EXPERTS - 1)
    n_used = (pend[-1] // bm).reshape(1)
    is_block_e = block_e[:, None] == experts[None, :]
    after = jnp.sum(jnp.where(is_block_e, pend[None, :], 0), axis=1) // bm
    e_after = jnp.sum(jnp.where(after[:, None] == jnp.arange(n_rows // bm, dtype=i32)[None, :], block_e[None, :], 0),
                      axis=1)
    next_e = jnp.where(after < n_used[0], e_after, -1)
    pad_cnt = padded - counts
    pad_end = jnp.cumsum(pad_cnt)
    j = jnp.arange(n_pad, dtype=i32)
    owner = jnp.sum((pad_end[None, :] <= j[:, None]).astype(i32), axis=1)
    is_owner = owner[:, None] == experts[None, :]
    pick = lambda v: jnp.sum(jnp.where(is_owner, v[None, :], 0), axis=1)
    in_expert = pick(pstart + counts) + (j - pick(pad_end - pad_cnt))
    pad_rows = jnp.where(owner < N_EXPERTS, in_expert, pend[-1] + (j - pad_end[-1]))
    src_tok = jnp.concatenate([jnp.repeat(jnp.arange(tokens, dtype=i32), TOP_K), j % tokens])
    dst_row = jnp.concatenate([dest.reshape(-1), pad_rows]).astype(i32)
    return dest.astype(i32), block_e.astype(i32), next_e.astype(i32), n_used.astype(i32), src_tok, dst_row


def kernel(x, w_in, b_gate, lam_q1, lam_k1, lam_q2, lam_k2, subln_g, w_oa, w_ob, w_o, ln1_g, ln1_b,
           w_router, b_router, w_gu, b_gu, w_down, b_down, ln2_g, ln2_b):
    batch, seq, d = x.shape
    tokens = batch * seq
    h = x.reshape(tokens, d)
    for l in range(DEPTH):
        q_scale = HEAD_DIM ** -0.5 * math.log2(math.e)
        qk_a, vt_a = _project_qkv(h, w_in[l], 0, seq, "qkv_diff", q_scale=q_scale, v_feature_major=True)
        ya = _diff_attention(qk_a.reshape(batch, seq, 2 * SEG_W), vt_a, lam_q1[l], lam_k1[l], lam_q2[l],
                             lam_k2[l], subln_g[l]).reshape(tokens, -1)
        dil = []
        for g, (_, dilation) in enumerate(DIL_PAIRS):
            qkv_g = _project_qkv(h, w_in[l], g + 1, seq, f"qkv_dil{g}", q_scale=q_scale, dil=dilation)
            dil.append(_dilated_group(qkv_g, batch, seq, g))
        h1, h1p, topi, topw, rank, cnt = _merge(ya, dil, h, seq, w_in[l], b_gate[l], w_oa[l].astype(BF16),
                                                w_ob[l].astype(BF16), w_o[l].astype(BF16), ln1_g[l], ln1_b[l],
                                                w_router[l], b_router[l])
        dest, block_e, next_e, n_used, src_tok, dst_row = _routing_tables(topi, rank, cnt, tokens)
        xrows = _sc_scatter_rows(h1p, src_tok, dst_row)
        yrows = _experts(xrows, block_e, next_e, n_used, w_gu[l], b_gu[l], w_down[l], b_down[l])
        yplanes = _sc_gather_rows(yrows, dest.T.reshape(-1))
        h = _combine(yplanes, topw, h1, ln2_g[l], ln2_b[l])
    return h.reshape(batch, seq, d)
```

```python
import functools
import math

import jax
import jax.numpy as jnp
from jax import lax
from jax.experimental import pallas as pl
from jax.experimental.pallas import tpu as pltpu
from jax.experimental.pallas import tpu_sc as plsc

F32 = jnp.float32
BF16 = jnp.bfloat16
U32 = jnp.uint32

D_MODEL = 1024
HEAD_DIM = 64
ROT_DIM = HEAD_DIM // 4
ROPE_THETA = 500000.0
QBLK = 128
DA_HEADS = 4
DIL_PAIRS = ((128, 1), (512, 4), (2048, 16))
SEG_W = 512
QKV_W = 3 * SEG_W
N_EXPERTS = 32
TOP_K = 4
D_FF = D_MODEL
SWIGLU_ALPHA = 1.702
SWIGLU_LIMIT = 7.0
DEPTH = 1
DN_ALPHA = (2 * DEPTH) ** 0.25
EPS = 1e-5
LAMBDA_INIT = 0.8 - 0.6 * math.exp(-0.3 * 0)

LANES = 128
QKV_TM = 1024
QKV_CHUNK = 256
STRIDE_STAGE = 4
ATT_TQ = 256
ONES_ROWS = 16
DIL_NQ = 4
MERGE_TM = 512
MOE_BM = 256
MOE_TM = 512
SC_CHUNK = 128
VMEM_LIMIT = 52 * 1024 * 1024


def _cparams(n_axes):
    return pltpu.CompilerParams(dimension_semantics=("arbitrary",) * n_axes,
                                vmem_limit_bytes=VMEM_LIMIT)


def _qkv_kernel(x_ref, w_ref, cs_ref, o_ref, *rest, q_scale, dil):
    tm = x_ref.shape[0]
    n = tm // dil
    two_stage = dil > STRIDE_STAGE
    if two_stage:
        *maybe_vt_ref, wb_ref, slab_ref, slab2_ref = rest
    elif dil > 1:
        *maybe_vt_ref, wb_ref, slab_ref = rest
    else:
        *maybe_vt_ref, wb_ref = rest

    @pl.when(pl.program_id(0) == 0)
    def _():
        wb_ref[...] = w_ref[...].astype(BF16)

    if dil > 1:
        for c in range(D_MODEL // LANES):
            slab_ref[c] = x_ref[:, c * LANES:(c + 1) * LANES]
    if two_stage:
        part = tm // STRIDE_STAGE
        for c in range(D_MODEL // LANES):
            for q in range(STRIDE_STAGE):
                slab2_ref[c, q * part:(q + 1) * part, :] = slab_ref[c, pl.ds(q, part, stride=STRIDE_STAGE), :]

    def residue_rows(c, r, l0, count):
        if two_stage:
            first = (r % STRIDE_STAGE) * (tm // STRIDE_STAGE) + r // STRIDE_STAGE + l0 * (dil // STRIDE_STAGE)
            return slab2_ref[c, pl.ds(first, count, stride=dil // STRIDE_STAGE), :]
        return slab_ref[c, pl.ds(l0 * dil + r, count, stride=dil), :]

    def regrouped_rows(start):
        pieces = []
        for p in range(start, start + QKV_CHUNK, min(n, QKV_CHUNK)):
            pieces.append(jnp.concatenate(
                [residue_rows(c, p // n, p % n, min(n, QKV_CHUNK)) for c in range(D_MODEL // LANES)], axis=1))
        return jnp.concatenate(pieces, axis=0).astype(BF16)

    def store(rows, cols, val):
        if dil == 1:
            o_ref[rows, cols] = val
        elif n >= QKV_CHUNK:
            o_ref[rows.start // n, rows.start % n:rows.start % n + QKV_CHUNK, cols] = val
        else:
            per = QKV_CHUNK // n
            for q in range(per):
                o_ref[rows.start // n + q, :, cols] = val[q * n:(q + 1) * n]

    half = ROT_DIM // 2
    in_head = lax.broadcasted_iota(jnp.int32, (QKV_CHUNK, LANES), 1) % HEAD_DIM
    first, second = in_head < half, (in_head >= half) & (in_head < ROT_DIM)
    for rc in range(tm // QKV_CHUNK):
        rows = slice(rc * QKV_CHUNK, (rc + 1) * QKV_CHUNK)
        xb = regrouped_rows(rows.start) if dil > 1 else x_ref[rows, :].astype(BF16)
        cs = cs_ref[rows, :]
        c = jnp.where(first, cs, jnp.where(second, pltpu.roll(cs, half, 1), 1.0))
        s1 = jnp.where(second, cs, 0.0)
        s2 = jnp.where(first, -pltpu.roll(cs, LANES - half, 1), 0.0)
        for seg in range(3):
            acc = jnp.dot(xb, wb_ref[:, seg * SEG_W:(seg + 1) * SEG_W], preferred_element_type=F32)
            if seg == 2:
                if maybe_vt_ref:
                    maybe_vt_ref[0][:, rows] = acc.T.astype(BF16)
                else:
                    store(rows, slice(seg * SEG_W, (seg + 1) * SEG_W), acc.astype(BF16))
                continue
            for k in range(SEG_W // LANES):
                t = acc[:, k * LANES:(k + 1) * LANES]
                r = t * c + pltpu.roll(t, ROT_DIM // 2, 1) * s1 + pltpu.roll(t, LANES - ROT_DIM // 2, 1) * s2
                if seg == 0:
                    r = r * q_scale
                lo = seg * SEG_W + k * LANES
                store(rows, slice(lo, lo + LANES), r.astype(BF16))


def _rope_lane_table(seq, dil, tm):
    row = jnp.arange(seq, dtype=jnp.int32)
    n = tm // dil
    in_tile = row % tm
    pos = (row - in_tile + (in_tile % n) * dil + in_tile // n).astype(F32)
    inv_freq = ROPE_THETA ** (-jnp.arange(0, ROT_DIM, 2, dtype=F32) / ROT_DIM)
    ang = pos[:, None] * inv_freq[None, :]
    head = jnp.concatenate([jnp.cos(ang), jnp.sin(ang), jnp.zeros((seq, HEAD_DIM - ROT_DIM), F32)], axis=1)
    return jnp.tile(head, (1, LANES // HEAD_DIM))


def _project_qkv(x2, w, col_block, seq, name, q_scale, dil=1, v_feature_major=False):
    tokens = x2.shape[0]
    batch = tokens // seq
    tm = min(QKV_TM, seq)
    per_seq = seq // tm
    n = tm // dil
    out_w = 2 * SEG_W if v_feature_major else QKV_W
    scratch = [pltpu.VMEM((D_MODEL, QKV_W), BF16)]
    if dil == 1:
        out_specs = [pl.BlockSpec((tm, out_w), lambda i: (i, 0))]
        out_shape = [jax.ShapeDtypeStruct((tokens, out_w), BF16)]
    else:
        out_specs = [pl.BlockSpec((None, dil, n, out_w), lambda i: (i // per_seq, 0, i % per_seq, 0))]
        out_shape = [jax.ShapeDtypeStruct((batch, dil, seq // dil, out_w), BF16)]
        scratch += [pltpu.VMEM((D_MODEL // LANES, tm, LANES), F32)] * (2 if dil > STRIDE_STAGE else 1)
    if v_feature_major:
        out_specs.append(pl.BlockSpec((None, SEG_W, tm), lambda i: (i // per_seq, 0, i % per_seq)))
        out_shape.append(jax.ShapeDtypeStruct((batch, SEG_W, seq), BF16))
    outs = pl.pallas_call(
        functools.partial(_qkv_kernel, q_scale=q_scale, dil=dil),
        grid=(tokens // tm,),
        in_specs=[pl.BlockSpec((tm, D_MODEL), lambda i: (i, 0)),
                  pl.BlockSpec((D_MODEL, QKV_W), lambda i: (0, col_block)),
                  pl.BlockSpec((tm, LANES), lambda i: (i % per_seq, 0))],
        out_specs=out_specs,
        out_shape=out_shape,
        scratch_shapes=scratch,
        compiler_params=_cparams(1),
        name=name,
    )(x2, w, _rope_lane_table(seq, dil, tm))
    return outs if v_feature_major else outs[0]


def _nt_dot(a, b):
    return lax.dot_general(a, b, (((1,), (1,)), ((), ())), preferred_element_type=F32)


def _diff_kernel(q_ref, k_ref, vt_ref, lq1_ref, lk1_ref, lq2_ref, lk2_ref, g_ref, o_ref, s_ref, acc_ref, *, tq):
    i = pl.program_id(1)
    chains = 2 * DA_HEADS
    half = tq // 2
    lane = lax.broadcasted_iota(jnp.int32, (tq, LANES), 1)
    qs = []
    for h in range(DA_HEADS):
        q = q_ref[:, h * LANES:(h + 1) * LANES]
        zero = jnp.zeros_like(q)
        qs += [jnp.where(lane < HEAD_DIM, q, zero), jnp.where(lane >= HEAD_DIM, q, zero)]
    acc_ref[...] = jnp.zeros_like(acc_ref)

    def score_chain(j, buf, c):
        start = pl.multiple_of(j * tq, tq)
        kb = k_ref[pl.ds(start, tq), (c // 2) * LANES:(c // 2 + 1) * LANES]
        buf[c] = _nt_dot(kb, qs[c])

    def score_block(j, buf):
        for c in range(chains):
            score_chain(j, buf, c)

    def absorb(j, buf, state, masked, nxt=None):
        start = pl.multiple_of(j * tq, tq)
        out = []
        for c in range(chains):
            if nxt is not None:
                score_chain(j + 1, nxt, c)
            m, l = state[c]
            if masked:
                key = lax.broadcasted_iota(jnp.int32, (tq, tq), 0)
                qry = lax.broadcasted_iota(jnp.int32, (tq, tq), 1)
                buf[c] = jnp.where(key <= qry, buf[c], -jnp.inf)
            m_new = jnp.maximum(m, jnp.max(buf[c], axis=0, keepdims=True))
            a = jnp.exp2(m - m_new)
            p = [jnp.exp2((buf[c, u * half:(u + 1) * half, :] - m_new).astype(BF16)) for u in range(2)]
            vt = vt_ref[(c // 2) * LANES:(c // 2 + 1) * LANES, pl.ds(start, tq)]
            vt1 = jnp.concatenate([vt, jnp.ones((ONES_ROWS, tq), BF16)], axis=0)
            pv = jnp.dot(vt1, jnp.concatenate(p, axis=0), preferred_element_type=F32)
            l = a * l + pv[LANES:LANES + 1, :]
            acc_ref[c] = a * acc_ref[c] + pv[:LANES, :]
            out.append((m_new, l))
        return tuple(out)

    buf_a, buf_b = s_ref.at[0], s_ref.at[1]

    def two_blocks(jj, state):
        j = 2 * jj
        state = absorb(j, buf_a, state, False, nxt=buf_b)
        return absorb(j + 1, buf_b, state, False, nxt=buf_a)

    def last_from_a(state):
        return absorb(i, buf_a, state, True)

    def last_from_b(state):
        state = absorb(i - 1, buf_a, state, False, nxt=buf_b)
        return absorb(i, buf_b, state, True)

    init = tuple((jnp.full((1, tq), -jnp.inf, F32), jnp.zeros((1, tq), F32)) for _ in range(chains))
    score_block(0, buf_a)
    state = lax.fori_loop(0, i // 2, two_blocks, init)
    state = lax.cond(lax.rem(i, 2) == 0, last_from_a, last_from_b, state)

    lam = (jnp.exp(jnp.sum(lq1_ref[...] * lk1_ref[...], axis=1, keepdims=True))
           - jnp.exp(jnp.sum(lq2_ref[...] * lk2_ref[...], axis=1, keepdims=True)) + LAMBDA_INIT)
    for h in range(DA_HEADS):
        l1, l2 = state[2 * h][1], state[2 * h + 1][1]
        o = (acc_ref[2 * h] / l1 - lam * (acc_ref[2 * h + 1] / l2)).T
        ms = jnp.mean(o * o, axis=1, keepdims=True)
        o = o * lax.rsqrt(ms + EPS) * g_ref[...]
        o_ref[:, h * LANES:(h + 1) * LANES] = (o * (1.0 - LAMBDA_INIT)).astype(BF16)


def _diff_attention(qk3, vt3, lam_q1, lam_k1, lam_q2, lam_k2, subln_g):
    b, s, _ = qk3.shape
    tq = min(ATT_TQ, s)
    vec = lambda n: pl.BlockSpec((1, n), lambda bb, i: (0, 0))
    return pl.pallas_call(
        functools.partial(_diff_kernel, tq=tq),
        grid=(b, s // tq),
        in_specs=[
            pl.BlockSpec((None, tq, SEG_W), lambda bb, i: (bb, i, 0)),
            pl.BlockSpec((None, s, SEG_W), lambda bb, i: (bb, 0, 1)),
            pl.BlockSpec((None, SEG_W, s), lambda bb, i: (bb, 0, 0)),
            vec(HEAD_DIM), vec(HEAD_DIM), vec(HEAD_DIM), vec(HEAD_DIM), vec(2 * HEAD_DIM),
        ],
        out_specs=pl.BlockSpec((None, tq, SEG_W), lambda bb, i: (bb, i, 0)),
        out_shape=jax.ShapeDtypeStruct((b, s, SEG_W), BF16),
        scratch_shapes=[pltpu.VMEM((2, 2 * DA_HEADS, tq, tq), F32), pltpu.VMEM((2 * DA_HEADS, LANES, tq), F32)],
        compiler_params=_cparams(2),
        name="diff_attn",
    )(qk3, qk3, vt3, lam_q1.reshape(1, -1), lam_k1.reshape(1, -1), lam_q2.reshape(1, -1),
      lam_k2.reshape(1, -1), subln_g.reshape(1, -1))


def _dil_kernel(q_ref, kp_ref, kc_ref, vp_ref, vc_ref, o_ref, lse_ref, *, rel, nq, nres):
    n = pl.program_id(2)
    qi = lax.broadcasted_iota(jnp.int32, (QBLK, 2 * QBLK), 0)
    kj = lax.broadcasted_iota(jnp.int32, (QBLK, 2 * QBLK), 1)
    dist = qi + QBLK - kj
    band = (dist >= 0) & (dist <= rel)
    lane = lax.broadcasted_iota(jnp.int32, (QBLK, LANES), 1)
    low = lane < HEAD_DIM
    slabs = [slice(p * LANES, (p + 1) * LANES) for p in range(SEG_W // LANES)]

    def window(prev_ref, cur_ref, s, sl):
        if s == 0:
            return jnp.concatenate([prev_ref[:, sl], cur_ref[:QBLK, sl]], axis=0)
        return cur_ref[(s - 1) * QBLK:(s + 1) * QBLK, sl]

    def score_block(blk):
        res, s = blk
        rows = slice(s * QBLK, (s + 1) * QBLK)
        scores = []
        for sl in slabs:
            q2 = q_ref[res, rows, sl]
            k2 = window(kp_ref.at[res], kc_ref.at[res], s, sl)
            for sel in (low, jnp.logical_not(low)):
                scores.append(_nt_dot(jnp.where(sel, q2, jnp.zeros_like(q2)), k2))
        return scores

    def finish_block(blk, scores):
        res, s = blk
        rows = slice(s * QBLK, (s + 1) * QBLK)
        valid = band & ((kj >= QBLK) | (n * nq + s > 0))
        probs = []
        for sc in scores:
            sc = jnp.where(valid, sc, -jnp.inf)
            m = jnp.max(sc, axis=1, keepdims=True)
            probs.append((jnp.exp2((sc - m).astype(BF16)), m))
        one = jnp.ones((2 * QBLK, LANES), BF16)
        low_keys = lax.broadcasted_iota(jnp.int32, (2 * QBLK, LANES), 1) < HEAD_DIM
        for p, sl in enumerate(slabs):
            v2 = window(vp_ref.at[res], vc_ref.at[res], s, sl)
            (pe_lo, m_lo), (pe_hi, m_hi) = probs[2 * p:2 * p + 2]
            pv_lo = jnp.dot(pe_lo, jnp.where(low_keys, v2, one), preferred_element_type=F32)
            pv_hi = jnp.dot(pe_hi, jnp.where(low_keys, one, v2), preferred_element_type=F32)
            num = jnp.where(low, pv_lo, pv_hi)
            den = pltpu.roll(jnp.where(low, pv_hi, pv_lo), HEAD_DIM, 1)
            o_ref[res, rows, sl] = (num / den).astype(BF16)
            lse_ref[res, rows, sl] = jnp.where(low, m_lo, m_hi) + jnp.log2(den)

    blocks = [(res, s) for res in range(nres) for s in range(nq)]
    pending = score_block(blocks[0])
    for prev, blk in zip(blocks, blocks[1:]):
        nxt = score_block(blk)
        finish_block(prev, pending)
        pending = nxt
    finish_block(blocks[-1], pending)


def _dilated_group(qkv, batch, seq, group):
    window, dil = DIL_PAIRS[group]
    rel = window // dil
    length = seq // dil
    nb = length // QBLK
    nq = math.gcd(nb, DIL_NQ)
    nres = math.gcd(dil, DIL_NQ // nq)
    view = qkv.reshape(batch, dil, length, QKV_W)

    def spec(seg, prev):
        if prev:
            return pl.BlockSpec((None, nres, QBLK, SEG_W), lambda b, r, n: (b, r, jnp.maximum(n * nq - 1, 0), seg))
        return pl.BlockSpec((None, nres, nq * QBLK, SEG_W), lambda b, r, n: (b, r, n, seg))

    out_spec = pl.BlockSpec((None, nres, nq * QBLK, SEG_W), lambda b, r, n: (b, r, n, 0))
    o, lse = pl.pallas_call(
        functools.partial(_dil_kernel, rel=rel, nq=nq, nres=nres),
        grid=(batch, dil // nres, nb // nq),
        in_specs=[spec(0, False), spec(1, True), spec(1, False), spec(2, True), spec(2, False)],
        out_specs=[out_spec, out_spec],
        out_shape=[jax.ShapeDtypeStruct((batch, dil, length, SEG_W), BF16),
                   jax.ShapeDtypeStruct((batch, dil, length, SEG_W), F32)],
        compiler_params=_cparams(3),
        name=f"dilated{group}",
    )(view, view, view, view, view)
    return o, lse


def _layer_norm(z, g, b):
    mu = jnp.mean(z, axis=1, keepdims=True)
    zc = z - mu
    var = jnp.mean(zc * zc, axis=1, keepdims=True)
    return zc * lax.rsqrt(var + EPS) * g + b


def _pack_bf16_pairs(v):
    n = v.shape[1] // 2
    lo = pltpu.bitcast(v[:, :n].astype(BF16).astype(F32), U32)
    hi = pltpu.bitcast(v[:, n:].astype(BF16).astype(F32), U32)
    return jnp.bitwise_or(jnp.right_shift(lo, jnp.uint32(16)), jnp.bitwise_and(hi, jnp.uint32(0xFFFF0000)))


def _unpack_bf16_pairs(w):
    lo = pltpu.bitcast(jnp.left_shift(w, jnp.uint32(16)), F32)
    hi = pltpu.bitcast(jnp.bitwise_and(w, jnp.uint32(0xFFFF0000)), F32)
    return lo, hi


def _merge_kernel(ya_ref, o1_ref, o2_ref, o3_ref, l1_ref, l2_ref, l3_ref, x_ref,
                  wg_ref, bg_ref, woa_ref, wob_ref, wo_ref, g_ref, b_ref, wrh_ref, wrl_ref, br_ref,
                  h_ref, hp_ref, topi_ref, topw_ref, rank_ref, cnt_ref, carry_ref, scr_ref, wgb_ref):
    i = pl.program_id(0)
    tm = x_ref.shape[0]

    @pl.when(i == 0)
    def _():
        carry_ref[...] = jnp.zeros_like(carry_ref)
        wgb_ref[...] = wg_ref[...].astype(BF16)

    hm = tm // 2
    halves = (slice(0, hm), slice(hm, tm))
    xs = [x_ref[rows, :] for rows in halves]
    xbs = [x.astype(BF16) for x in xs]
    gate_dot = lambda xb: jnp.dot(xb, wgb_ref[...], preferred_element_type=F32) + bg_ref[...]
    gate_pre = [gate_dot(xbs[0])]
    pas = [jnp.dot(ya_ref[rows, :], woa_ref[...], preferred_element_type=F32) for rows in halves]

    def token_major(ref, scr):
        dil, n = ref.shape[0], ref.shape[1]
        if dil == 1:
            return ref[0].astype(F32)
        for r in range(dil):
            blk = ref[r].astype(F32)
            for c in range(SEG_W // LANES):
                scr[c, pl.ds(r, n, stride=dil), :] = blk[:, c * LANES:(c + 1) * LANES]
        return jnp.concatenate([scr[c] for c in range(SEG_W // LANES)], axis=1)

    oa, ob, oc = (token_major(r, scr_ref.at[k]) for k, r in enumerate((o1_ref, o2_ref, o3_ref)))
    la, lb, lc = (token_major(r, scr_ref.at[3 + k]) for k, r in enumerate((l1_ref, l2_ref, l3_ref)))
    mx = jnp.maximum(jnp.maximum(la, lb), lc)
    ea, eb, ec = jnp.exp2(la - mx), jnp.exp2(lb - mx), jnp.exp2(lc - mx)
    yb = ((ea * oa + eb * ob + ec * oc) / (ea + eb + ec)).astype(BF16)
    pbs = [jnp.dot(yb[rows, :], wob_ref[...], preferred_element_type=F32) for rows in halves]
    gate_pre.append(gate_dot(xbs[1]))
    mixes = []
    for pre_act, pa, pb in zip(gate_pre, pas, pbs):
        gates = jax.nn.sigmoid(pre_act)
        merged = gates[:, :D_MODEL] * pa + gates[:, D_MODEL:] * pb
        mixes.append(jnp.dot(merged.astype(BF16), wo_ref[...], preferred_element_type=F32))
    logits = []
    for rows, x, mix in zip(halves, xs, mixes):
        h = _layer_norm(DN_ALPHA * x + mix, g_ref[...], b_ref[...])
        h_ref[rows, :] = h
        hp_ref[rows, :] = _pack_bf16_pairs(h)
        hh = h.astype(BF16)
        hl = (h - hh.astype(F32)).astype(BF16)
        logits.append(jnp.dot(hh, wrh_ref[...], preferred_element_type=F32)
                      + jnp.dot(hl, wrh_ref[...], preferred_element_type=F32)
                      + jnp.dot(hh, wrl_ref[...], preferred_element_type=F32) + br_ref[...])

    lane = lax.broadcasted_iota(jnp.int32, (hm, LANES), 1)
    r_i = lax.broadcasted_iota(jnp.int32, (hm, hm), 0)
    c_i = lax.broadcasted_iota(jnp.int32, (hm, hm), 1)
    tri = (r_i > c_i).astype(BF16)
    before = carry_ref[...]
    for rows, lg in zip(halves, logits):
        lg = jnp.where(lane < N_EXPERTS, lg, -jnp.inf)
        vals, idxs = [], []
        for _ in range(TOP_K):
            mv = jnp.max(lg, axis=1, keepdims=True)
            ik = jnp.min(jnp.where(lg == mv, lane, LANES), axis=1, keepdims=True)
            vals.append(mv)
            idxs.append(ik)
            lg = jnp.where(lane == ik, -jnp.inf, lg)
        es = [jnp.exp(v - vals[0]) for v in vals]
        tot = es[0] + es[1] + es[2] + es[3]
        onehot = jnp.zeros((hm, LANES), F32)
        for ik in idxs:
            onehot = onehot + (lane == ik).astype(F32)
        pre = jnp.dot(tri, onehot.astype(BF16), preferred_element_type=F32) + before
        topi = jnp.zeros((hm, LANES), jnp.int32)
        topw = jnp.zeros((hm, LANES), F32)
        rank = jnp.zeros((hm, LANES), F32)
        for k in range(TOP_K):
            rk = jnp.sum(jnp.where(lane == idxs[k], pre, 0.0), axis=1, keepdims=True)
            topi = jnp.where(lane == k, idxs[k], topi)
            topw = jnp.where(lane == k, es[k] / tot, topw)
            rank = jnp.where(lane == k, rk, rank)
        topi_ref[rows, :] = topi
        topw_ref[rows, :] = topw
        rank_ref[rows, :] = rank.astype(jnp.int32)
        before = before + jnp.sum(onehot, axis=0, keepdims=True)
    carry_ref[...] = before
    cnt_ref[...] = before


def _merge(ya, dil_outs, x2, seq, w_b, b_gate, w_oa_b, w_ob_b, w_o_b, ln1_g, ln1_b, w_router, b_router):
    tokens = x2.shape[0]
    tm = MERGE_TM
    per_seq = seq // tm

    def group(a):
        dil = a.shape[1]
        return pl.BlockSpec((None, dil, tm // dil, SEG_W), lambda i: (i // per_seq, 0, i % per_seq, 0))

    gate_blk = w_b.shape[1] // (2 * D_MODEL) - 1
    wr = jnp.zeros((D_MODEL, LANES), F32).at[:, :N_EXPERTS].set(w_router)
    wr_hi = wr.astype(BF16)
    wr_lo = (wr - wr_hi.astype(F32)).astype(BF16)
    br = jnp.zeros((1, LANES), F32).at[0, :N_EXPERTS].set(b_router)
    row = lambda w: pl.BlockSpec((tm, w), lambda i: (i, 0))
    full = lambda a: pl.BlockSpec(a.shape, lambda i: (0,) * a.ndim)
    (o1, l1), (o2, l2), (o3, l3) = dil_outs
    bg = b_gate.reshape(1, -1)
    g1 = ln1_g.reshape(1, -1)
    b1 = ln1_b.reshape(1, -1)
    lane_out = lambda dt: jax.ShapeDtypeStruct((tokens, LANES), dt)
    return pl.pallas_call(
        _merge_kernel,
        grid=(tokens // tm,),
        in_specs=[row(SEG_W), group(o1), group(o2), group(o3), group(l1), group(l2), group(l3),
                  row(D_MODEL), pl.BlockSpec((D_MODEL, 2 * D_MODEL), lambda i: (0, gate_blk)),
                  full(bg), full(w_oa_b), full(w_ob_b), full(w_o_b), full(g1), full(b1),
                  full(wr_hi), full(wr_lo), full(br)],
        out_specs=[row(D_MODEL), row(D_MODEL // 2), row(LANES), row(LANES), row(LANES),
                   pl.BlockSpec((1, LANES), lambda i: (0, 0))],
        out_shape=[jax.ShapeDtypeStruct((tokens, D_MODEL), F32), jax.ShapeDtypeStruct((tokens, D_MODEL // 2), U32),
                   lane_out(jnp.int32), lane_out(F32), lane_out(jnp.int32), jax.ShapeDtypeStruct((1, LANES), F32)],
        scratch_shapes=[pltpu.VMEM((1, LANES), F32), pltpu.VMEM((6, SEG_W // LANES, tm, LANES), F32),
                        pltpu.VMEM((D_MODEL, 2 * D_MODEL), BF16)],
        compiler_params=_cparams(1),
        name="merge",
    )(ya, o1, o2, o3, l1, l2, l3, x2, w_b, bg, w_oa_b, w_ob_b, w_o_b, g1, b1, wr_hi, wr_lo, br)


def _sc_mesh():
    return plsc.VectorSubcoreMesh(core_axis_name="core", subcore_axis_name="subcore")


def _sc_scatter_rows(table, src_idx, dst_idx):
    n = src_idx.shape[0]
    d = table.shape[1]
    mesh = _sc_mesh()
    workers = mesh.num_cores * mesh.num_subcores
    per = n // (SC_CHUNK * workers)
    assert per * SC_CHUNK * workers == n

    @pl.kernel(out_type=jax.ShapeDtypeStruct((n, d), table.dtype), mesh=mesh,
               scratch_types=[pltpu.VMEM((1, SC_CHUNK), jnp.int32), pltpu.VMEM((1, SC_CHUNK), jnp.int32),
                              pltpu.VMEM((SC_CHUNK, d), table.dtype)])
    def copy(t_hbm, s_hbm, d_hbm, o_hbm, s_vm, d_vm, buf):
        wid = lax.axis_index("core") * mesh.num_subcores + lax.axis_index("subcore")

        @pl.loop(0, per)
        def _(j):
            blk = wid * per + j
            pltpu.sync_copy(s_hbm.at[pl.ds(blk, 1)], s_vm)
            pltpu.sync_copy(d_hbm.at[pl.ds(blk, 1)], d_vm)
            pltpu.sync_copy(t_hbm.at[s_vm.at[0]], buf)
            pltpu.sync_copy(buf, o_hbm.at[d_vm.at[0]])

    return copy(table, src_idx.reshape(-1, SC_CHUNK), dst_idx.reshape(-1, SC_CHUNK))


def _sc_gather_rows(table, idx):
    n = idx.shape[0]
    d = table.shape[1]
    mesh = _sc_mesh()
    workers = mesh.num_cores * mesh.num_subcores
    per = n // (SC_CHUNK * workers)
    assert per * SC_CHUNK * workers == n

    @pl.kernel(out_type=jax.ShapeDtypeStruct((n, d), table.dtype), mesh=mesh,
               scratch_types=[pltpu.VMEM((1, SC_CHUNK), jnp.int32), pltpu.VMEM((SC_CHUNK, d), table.dtype)])
    def gather(t_hbm, i_hbm, o_hbm, i_vm, buf):
        wid = lax.axis_index("core") * mesh.num_subcores + lax.axis_index("subcore")

        @pl.loop(0, per)
        def _(j):
            blk = wid * per + j
            pltpu.sync_copy(i_hbm.at[pl.ds(blk, 1)], i_vm)
            pltpu.sync_copy(t_hbm.at[i_vm.at[0]], buf)
            pltpu.sync_copy(buf, o_hbm.at[pl.ds(blk * SC_CHUNK, SC_CHUNK)])

    return gather(table, idx.reshape(-1, SC_CHUNK))


def _expert_kernel(be_ref, nx_ref, nu_ref, x_ref, wgu_hbm, bgu_ref, wd_hbm, bd_ref, y_ref,
                   wgu_stage, wd_stage, wgu_b, wd_b, sem):
    i = pl.program_id(0)
    used = i < nu_ref[0]
    expert = be_ref[i]

    def weight_copies(e):
        return (pltpu.make_async_copy(wgu_hbm.at[e], wgu_stage, sem.at[0]),
                pltpu.make_async_copy(wd_hbm.at[e], wd_stage, sem.at[1]))

    @pl.when(i == 0)
    def _():
        for cp in weight_copies(expert):
            cp.start()

    @pl.when(used & ((i == 0) | (expert != be_ref[jnp.maximum(i - 1, 0)])))
    def _():
        for cp in weight_copies(expert):
            cp.wait()
        wgu_b[...] = wgu_stage[...].astype(BF16)
        wd_b[...] = wd_stage[...].astype(BF16)

        @pl.when(nx_ref[i] >= 0)
        def _():
            for cp in weight_copies(nx_ref[i]):
                cp.start()

    @pl.when(used)
    def _():
        lo, hi = _unpack_bf16_pairs(x_ref[...])
        xb = jnp.concatenate([lo, hi], axis=1).astype(BF16)
        chunk = 512
        cols = [slice(c * chunk, (c + 1) * chunk) for c in range(D_FF // chunk)]
        pre = []
        for gs in cols:
            us = slice(D_FF + gs.start, D_FF + gs.stop)
            gate = jnp.dot(xb, wgu_b[:, gs], preferred_element_type=F32) + bgu_ref[:, gs]
            up = jnp.dot(xb, wgu_b[:, us], preferred_element_type=F32) + bgu_ref[:, us]
            pre.append((gate, up))
        acc = jnp.zeros((x_ref.shape[0], D_MODEL), F32)
        for gs, (gate, up) in zip(cols, pre):
            gate = jnp.minimum(gate, SWIGLU_LIMIT)
            up = jnp.clip(up, -SWIGLU_LIMIT, SWIGLU_LIMIT)
            act = (up + 1.0) * gate * jax.nn.sigmoid(SWIGLU_ALPHA * gate)
            acc = acc + jnp.dot(act.astype(BF16), wd_b[gs, :], preferred_element_type=F32)
        y_ref[...] = _pack_bf16_pairs(acc + bd_ref[...])

    @pl.when(i >= nu_ref[0])
    def _():
        y_ref[...] = jnp.zeros_like(y_ref)


def _experts(xrows, block_e, next_e, n_used, w_gu, b_gu, w_down, b_down):
    n_rows = xrows.shape[0]
    bm = MOE_BM
    bias = lambda i, be, nx, nu: (be[i], 0, 0)
    return pl.pallas_call(
        _expert_kernel,
        grid_spec=pltpu.PrefetchScalarGridSpec(
            num_scalar_prefetch=3,
            grid=(n_rows // bm,),
            in_specs=[
                pl.BlockSpec((bm, D_MODEL // 2), lambda i, be, nx, nu: (i, 0)),
                pl.BlockSpec(memory_space=pl.ANY),
                pl.BlockSpec((None, 1, 2 * D_FF), bias),
                pl.BlockSpec(memory_space=pl.ANY),
                pl.BlockSpec((None, 1, D_MODEL), bias),
            ],
            out_specs=pl.BlockSpec((bm, D_MODEL // 2), lambda i, be, nx, nu: (i, 0)),
            scratch_shapes=[pltpu.VMEM((D_MODEL, 2 * D_FF), F32), pltpu.VMEM((D_FF, D_MODEL), F32),
                            pltpu.VMEM((D_MODEL, 2 * D_FF), BF16), pltpu.VMEM((D_FF, D_MODEL), BF16),
                            pltpu.SemaphoreType.DMA((2,))],
        ),
        out_shape=jax.ShapeDtypeStruct((n_rows, D_MODEL // 2), U32),
        compiler_params=_cparams(1),
        name="experts",
    )(block_e, next_e, n_used, xrows, w_gu, b_gu.reshape(N_EXPERTS, 1, -1), w_down, b_down.reshape(N_EXPERTS, 1, -1))


def _combine_kernel(y0_ref, y1_ref, y2_ref, y3_ref, w_ref, h_ref, g_ref, b_ref, o_ref):
    w = w_ref[...]
    half = D_MODEL // 2
    f_lo = jnp.zeros((h_ref.shape[0], half), F32)
    f_hi = jnp.zeros((h_ref.shape[0], half), F32)
    for k, y_ref in enumerate((y0_ref, y1_ref, y2_ref, y3_ref)):
        lo, hi = _unpack_bf16_pairs(y_ref[...])
        f_lo = f_lo + lo * w[:, k:k + 1]
        f_hi = f_hi + hi * w[:, k:k + 1]
    f = jnp.concatenate([f_lo, f_hi], axis=1)
    o_ref[...] = _layer_norm(DN_ALPHA * h_ref[...] + f, g_ref[...], b_ref[...])


def _combine(yplanes, topw, h, ln2_g, ln2_b):
    tokens = h.shape[0]
    tm = MOE_TM
    per_plane = tokens // tm
    plane = lambda k: pl.BlockSpec((tm, D_MODEL // 2), lambda i: (k * per_plane + i, 0))
    return pl.pallas_call(
        _combine_kernel,
        grid=(per_plane,),
        in_specs=[plane(0), plane(1), plane(2), plane(3),
                  pl.BlockSpec((tm, LANES), lambda i: (i, 0)),
                  pl.BlockSpec((tm, D_MODEL), lambda i: (i, 0)),
                  pl.BlockSpec((1, D_MODEL), lambda i: (0, 0)),
                  pl.BlockSpec((1, D_MODEL), lambda i: (0, 0))],
        out_specs=pl.BlockSpec((tm, D_MODEL), lambda i: (i, 0)),
        out_shape=jax.ShapeDtypeStruct((tokens, D_MODEL), F32),
        compiler_params=_cparams(1),
        name="combine",
    )(yplanes, yplanes, yplanes, yplanes, topw, h, ln2_g.reshape(1, -1), ln2_b.reshape(1, -1))


def _routing_tables(topi, rank, cnt, tokens):
    bm = MOE_BM
    i32 = jnp.int32
    experts = jnp.arange(N_EXPERTS, dtype=i32)
    counts = cnt[0, :N_EXPERTS].astype(i32)
    padded = (counts + bm - 1) // bm * bm
    pend = jnp.cumsum(padded)
    pstart = pend - padded
    sel = topi[:, :TOP_K, None] == experts[None, None, :]
    dest = rank[:, :TOP_K] + jnp.sum(jnp.where(sel, pstart[None, None, :], 0), axis=-1)
    n_pad = N_EXPERTS * bm
    n_rows = tokens * TOP_K + n_pad
    starts = jnp.arange(n_rows // bm, dtype=i32) * bm
    block_e = jnp.minimum(jnp.sum((pend[None, :] <= starts[:, None]).astype(i32), axis=1), N_EXPERTS - 1)
    n_used = (pend[-1] // bm).reshape(1)
    is_block_e = block_e[:, None] == experts[None, :]
    after = jnp.sum(jnp.where(is_block_e, pend[None, :], 0), axis=1) // bm
    e_after = jnp.sum(jnp.where(after[:, None] == jnp.arange(n_rows // bm, dtype=i32)[None, :], block_e[None, :], 0),
                      axis=1)
    next_e = jnp.where(after < n_used[0], e_after, -1)
    pad_cnt = padded - counts
    pad_end = jnp.cumsum(pad_cnt)
    j = jnp.arange(n_pad, dtype=i32)
    owner = jnp.sum((pad_end[None, :] <= j[:, None]).astype(i32), axis=1)
    is_owner = owner[:, None] == experts[None, :]
    pick = lambda v: jnp.sum(jnp.where(is_owner, v[None, :], 0), axis=1)
    in_expert = pick(pstart + counts) + (j - pick(pad_end - pad_cnt))
    pad_rows = jnp.where(owner < N_EXPERTS, in_expert, pend[-1] + (j - pad_end[-1]))
    src_tok = jnp.concatenate([jnp.repeat(jnp.arange(tokens, dtype=i32), TOP_K), j % tokens])
    dst_row = jnp.concatenate([dest.reshape(-1), pad_rows]).astype(i32)
    return dest.astype(i32), block_e.astype(i32), next_e.astype(i32), n_used.astype(i32), src_tok, dst_row


def kernel(x, w_in, b_gate, lam_q1, lam_k1, lam_q2, lam_k2, subln_g, w_oa, w_ob, w_o, ln1_g, ln1_b,
           w_router, b_router, w_gu, b_gu, w_down, b_down, ln2_g, ln2_b):
    batch, seq, d = x.shape
    tokens = batch * seq
    h = x.reshape(tokens, d)
    for l in range(DEPTH):
        q_scale = HEAD_DIM ** -0.5 * math.log2(math.e)
        qk_a, vt_a = _project_qkv(h, w_in[l], 0, seq, "qkv_diff", q_scale=q_scale, v_feature_major=True)
        ya = _diff_attention(qk_a.reshape(batch, seq, 2 * SEG_W), vt_a, lam_q1[l], lam_k1[l], lam_q2[l],
                             lam_k2[l], subln_g[l]).reshape(tokens, -1)
        dil = []
        for g, (_, dilation) in enumerate(DIL_PAIRS):
            qkv_g = _project_qkv(h, w_in[l], g + 1, seq, f"qkv_dil{g}", q_scale=q_scale, dil=dilation)
            dil.append(_dilated_group(qkv_g, batch, seq, g))
        h1, h1p, topi, topw, rank, cnt = _merge(ya, dil, h, seq, w_in[l], b_gate[l], w_oa[l].astype(BF16),
                                                w_ob[l].astype(BF16), w_o[l].astype(BF16), ln1_g[l], ln1_b[l],
                                                w_router[l], b_router[l])
        dest, block_e, next_e, n_used, src_tok, dst_row = _routing_tables(topi, rank, cnt, tokens)
        xrows = _sc_scatter_rows(h1p, src_tok, dst_row)
        yrows = _experts(xrows, block_e, next_e, n_used, w_gu[l], b_gu[l], w_down[l], b_down[l])
        yplanes = _sc_gather_rows(yrows, dest.T.reshape(-1))
        h = _combine(yplanes, topw, h1, ln2_g[l], ln2_b[l])
    return h.reshape(batch, seq, d)
```

```python
import functools
import math

import jax
import jax.numpy as jnp
from jax import lax
from jax.experimental import pallas as pl
from jax.experimental.pallas import tpu as pltpu
from jax.experimental.pallas import tpu_sc as plsc

F32 = jnp.float32
BF16 = jnp.bfloat16
U32 = jnp.uint32

D_MODEL = 1024
HEAD_DIM = 64
ROT_DIM = HEAD_DIM // 4
ROPE_THETA = 500000.0
QBLK = 128
DA_HEADS = 4
DIL_PAIRS = ((128, 1), (512, 4), (2048, 16))
SEG_W = 512
QKV_W = 3 * SEG_W
N_EXPERTS = 32
TOP_K = 4
D_FF = D_MODEL
SWIGLU_ALPHA = 1.702
SWIGLU_LIMIT = 7.0
DEPTH = 1
DN_ALPHA = (2 * DEPTH) ** 0.25
EPS = 1e-5
LAMBDA_INIT = 0.8 - 0.6 * math.exp(-0.3 * 0)

LANES = 128
QKV_TM = 1024
QKV_CHUNK = 256
STRIDE_STAGE = 4
ATT_TQ = 256
ONES_ROWS = 16
DIL_NQ = 8
FF_CHUNK = 512
MERGE_TM = 512
MOE_BM = 256
MOE_TM = 512
SC_CHUNK = 128
VMEM_LIMIT = 52 * 1024 * 1024


def _cparams(n_axes):
    return pltpu.CompilerParams(dimension_semantics=("arbitrary",) * n_axes,
                                vmem_limit_bytes=VMEM_LIMIT)


def _qkv_kernel(x_ref, w_ref, cs_ref, o_ref, *rest, q_scale, dil):
    tm = x_ref.shape[0]
    n = tm // dil
    two_stage = dil > STRIDE_STAGE
    if two_stage:
        *maybe_vt_ref, wb_ref, slab_ref, slab2_ref = rest
    elif dil > 1:
        *maybe_vt_ref, wb_ref, slab_ref = rest
    else:
        *maybe_vt_ref, wb_ref = rest

    @pl.when(pl.program_id(0) == 0)
    def _():
        wb_ref[...] = w_ref[...].astype(BF16)

    if dil > 1:
        for c in range(D_MODEL // LANES):
            slab_ref[c] = x_ref[:, c * LANES:(c + 1) * LANES]
    if two_stage:
        part = tm // STRIDE_STAGE
        for c in range(D_MODEL // LANES):
            for q in range(STRIDE_STAGE):
                slab2_ref[c, q * part:(q + 1) * part, :] = slab_ref[c, pl.ds(q, part, stride=STRIDE_STAGE), :]

    def residue_rows(c, r, l0, count):
        if two_stage:
            first = (r % STRIDE_STAGE) * (tm // STRIDE_STAGE) + r // STRIDE_STAGE + l0 * (dil // STRIDE_STAGE)
            return slab2_ref[c, pl.ds(first, count, stride=dil // STRIDE_STAGE), :]
        return slab_ref[c, pl.ds(l0 * dil + r, count, stride=dil), :]

    def regrouped_rows(start):
        pieces = []
        for p in range(start, start + QKV_CHUNK, min(n, QKV_CHUNK)):
            pieces.append(jnp.concatenate(
                [residue_rows(c, p // n, p % n, min(n, QKV_CHUNK)) for c in range(D_MODEL // LANES)], axis=1))
        return jnp.concatenate(pieces, axis=0).astype(BF16)

    def store(rows, cols, val):
        if dil == 1:
            o_ref[rows, cols] = val
        elif n >= QKV_CHUNK:
            o_ref[rows.start // n, rows.start % n:rows.start % n + QKV_CHUNK, cols] = val
        else:
            per = QKV_CHUNK // n
            for q in range(per):
                o_ref[rows.start // n + q, :, cols] = val[q * n:(q + 1) * n]

    half = ROT_DIM // 2
    in_head = lax.broadcasted_iota(jnp.int32, (QKV_CHUNK, LANES), 1) % HEAD_DIM
    first, second = in_head < half, (in_head >= half) & (in_head < ROT_DIM)
    for rc in range(tm // QKV_CHUNK):
        rows = slice(rc * QKV_CHUNK, (rc + 1) * QKV_CHUNK)
        xb = regrouped_rows(rows.start) if dil > 1 else x_ref[rows, :].astype(BF16)
        cs = cs_ref[rows, :]
        c = jnp.where(first, cs, jnp.where(second, pltpu.roll(cs, half, 1), 1.0))
        s1 = jnp.where(second, cs, 0.0)
        s2 = jnp.where(first, -pltpu.roll(cs, LANES - half, 1), 0.0)
        for seg in range(3):
            acc = jnp.dot(xb, wb_ref[:, seg * SEG_W:(seg + 1) * SEG_W], preferred_element_type=F32)
            if seg == 2:
                if maybe_vt_ref:
                    maybe_vt_ref[0][:, rows] = acc.T.astype(BF16)
                else:
                    store(rows, slice(seg * SEG_W, (seg + 1) * SEG_W), acc.astype(BF16))
                continue
            for k in range(SEG_W // LANES):
                t = acc[:, k * LANES:(k + 1) * LANES]
                r = t * c + pltpu.roll(t, ROT_DIM // 2, 1) * s1 + pltpu.roll(t, LANES - ROT_DIM // 2, 1) * s2
                if seg == 0:
                    r = r * q_scale
                lo = seg * SEG_W + k * LANES
                store(rows, slice(lo, lo + LANES), r.astype(BF16))


def _rope_lane_table(seq, dil, tm):
    row = jnp.arange(seq, dtype=jnp.int32)
    n = tm // dil
    in_tile = row % tm
    pos = (row - in_tile + (in_tile % n) * dil + in_tile // n).astype(F32)
    inv_freq = ROPE_THETA ** (-jnp.arange(0, ROT_DIM, 2, dtype=F32) / ROT_DIM)
    ang = pos[:, None] * inv_freq[None, :]
    head = jnp.concatenate([jnp.cos(ang), jnp.sin(ang), jnp.zeros((seq, HEAD_DIM - ROT_DIM), F32)], axis=1)
    return jnp.tile(head, (1, LANES // HEAD_DIM))


def _project_qkv(x2, w, col_block, seq, name, q_scale, dil=1, v_feature_major=False):
    tokens = x2.shape[0]
    batch = tokens // seq
    tm = min(QKV_TM, seq)
    per_seq = seq // tm
    n = tm // dil
    out_w = 2 * SEG_W if v_feature_major else QKV_W
    scratch = [pltpu.VMEM((D_MODEL, QKV_W), BF16)]
    if dil == 1:
        out_specs = [pl.BlockSpec((tm, out_w), lambda i: (i, 0))]
        out_shape = [jax.ShapeDtypeStruct((tokens, out_w), BF16)]
    else:
        out_specs = [pl.BlockSpec((None, dil, n, out_w), lambda i: (i // per_seq, 0, i % per_seq, 0))]
        out_shape = [jax.ShapeDtypeStruct((batch, dil, seq // dil, out_w), BF16)]
        scratch += [pltpu.VMEM((D_MODEL // LANES, tm, LANES), F32)] * (2 if dil > STRIDE_STAGE else 1)
    if v_feature_major:
        out_specs.append(pl.BlockSpec((None, SEG_W, tm), lambda i: (i // per_seq, 0, i % per_seq)))
        out_shape.append(jax.ShapeDtypeStruct((batch, SEG_W, seq), BF16))
    outs = pl.pallas_call(
        functools.partial(_qkv_kernel, q_scale=q_scale, dil=dil),
        grid=(tokens // tm,),
        in_specs=[pl.BlockSpec((tm, D_MODEL), lambda i: (i, 0)),
                  pl.BlockSpec((D_MODEL, QKV_W), lambda i: (0, col_block)),
                  pl.BlockSpec((tm, LANES), lambda i: (i % per_seq, 0))],
        out_specs=out_specs,
        out_shape=out_shape,
        scratch_shapes=scratch,
        compiler_params=_cparams(1),
        name=name,
    )(x2, w, _rope_lane_table(seq, dil, tm))
    return outs if v_feature_major else outs[0]


def _nt_dot(a, b):
    return lax.dot_general(a, b, (((1,), (1,)), ((), ())), preferred_element_type=F32)


def _diff_kernel(q_ref, k_ref, vt_ref, lq1_ref, lk1_ref, lq2_ref, lk2_ref, g_ref, o_ref, s_ref, acc_ref, *, tq):
    i = pl.program_id(1)
    chains = 2 * DA_HEADS
    half = tq // 2
    lane = lax.broadcasted_iota(jnp.int32, (tq, LANES), 1)
    qs = []
    for h in range(DA_HEADS):
        q = q_ref[:, h * LANES:(h + 1) * LANES]
        zero = jnp.zeros_like(q)
        qs += [jnp.where(lane < HEAD_DIM, q, zero), jnp.where(lane >= HEAD_DIM, q, zero)]
    acc_ref[...] = jnp.zeros_like(acc_ref)

    def score_chain(j, buf, c):
        start = pl.multiple_of(j * tq, tq)
        kb = k_ref[pl.ds(start, tq), (c // 2) * LANES:(c // 2 + 1) * LANES]
        buf[c] = _nt_dot(kb, qs[c])

    def score_block(j, buf):
        for c in range(chains):
            score_chain(j, buf, c)

    def absorb(j, buf, state, masked, nxt=None):
        start = pl.multiple_of(j * tq, tq)
        out = []
        for c in range(chains):
            if nxt is not None:
                score_chain(j + 1, nxt, c)
            m, l = state[c]
            if masked:
                key = lax.broadcasted_iota(jnp.int32, (tq, tq), 0)
                qry = lax.broadcasted_iota(jnp.int32, (tq, tq), 1)
                buf[c] = jnp.where(key <= qry, buf[c], -jnp.inf)
            m_new = jnp.maximum(m, jnp.max(buf[c], axis=0, keepdims=True))
            a = jnp.exp2(m - m_new)
            p = [jnp.exp2((buf[c, u * half:(u + 1) * half, :] - m_new).astype(BF16)) for u in range(2)]
            vt = vt_ref[(c // 2) * LANES:(c // 2 + 1) * LANES, pl.ds(start, tq)]
            vt1 = jnp.concatenate([vt, jnp.ones((ONES_ROWS, tq), BF16)], axis=0)
            pv = jnp.dot(vt1, jnp.concatenate(p, axis=0), preferred_element_type=F32)
            l = a * l + pv[LANES:LANES + 1, :]
            acc_ref[c] = a * acc_ref[c] + pv[:LANES, :]
            out.append((m_new, l))
        return tuple(out)

    buf_a, buf_b = s_ref.at[0], s_ref.at[1]

    def two_blocks(jj, state):
        j = 2 * jj
        state = absorb(j, buf_a, state, False, nxt=buf_b)
        return absorb(j + 1, buf_b, state, False, nxt=buf_a)

    def last_from_a(state):
        return absorb(i, buf_a, state, True)

    def last_from_b(state):
        state = absorb(i - 1, buf_a, state, False, nxt=buf_b)
        return absorb(i, buf_b, state, True)

    init = tuple((jnp.full((1, tq), -jnp.inf, F32), jnp.zeros((1, tq), F32)) for _ in range(chains))
    score_block(0, buf_a)
    state = lax.fori_loop(0, i // 2, two_blocks, init)
    state = lax.cond(lax.rem(i, 2) == 0, last_from_a, last_from_b, state)

    lam = (jnp.exp(jnp.sum(lq1_ref[...] * lk1_ref[...], axis=1, keepdims=True))
           - jnp.exp(jnp.sum(lq2_ref[...] * lk2_ref[...], axis=1, keepdims=True)) + LAMBDA_INIT)
    for h in range(DA_HEADS):
        l1, l2 = state[2 * h][1], state[2 * h + 1][1]
        o = (acc_ref[2 * h] / l1 - lam * (acc_ref[2 * h + 1] / l2)).T
        ms = jnp.mean(o * o, axis=1, keepdims=True)
        o = o * lax.rsqrt(ms + EPS) * g_ref[...]
        o_ref[:, h * LANES:(h + 1) * LANES] = (o * (1.0 - LAMBDA_INIT)).astype(BF16)


def _diff_attention(qk3, vt3, lam_q1, lam_k1, lam_q2, lam_k2, subln_g):
    b, s, _ = qk3.shape
    tq = min(ATT_TQ, s)
    vec = lambda n: pl.BlockSpec((1, n), lambda bb, i: (0, 0))
    return pl.pallas_call(
        functools.partial(_diff_kernel, tq=tq),
        grid=(b, s // tq),
        in_specs=[
            pl.BlockSpec((None, tq, SEG_W), lambda bb, i: (bb, i, 0)),
            pl.BlockSpec((None, s, SEG_W), lambda bb, i: (bb, 0, 1)),
            pl.BlockSpec((None, SEG_W, s), lambda bb, i: (bb, 0, 0)),
            vec(HEAD_DIM), vec(HEAD_DIM), vec(HEAD_DIM), vec(HEAD_DIM), vec(2 * HEAD_DIM),
        ],
        out_specs=pl.BlockSpec((None, tq, SEG_W), lambda bb, i: (bb, i, 0)),
        out_shape=jax.ShapeDtypeStruct((b, s, SEG_W), BF16),
        scratch_shapes=[pltpu.VMEM((2, 2 * DA_HEADS, tq, tq), F32), pltpu.VMEM((2 * DA_HEADS, LANES, tq), F32)],
        compiler_params=_cparams(2),
        name="diff_attn",
    )(qk3, qk3, vt3, lam_q1.reshape(1, -1), lam_k1.reshape(1, -1), lam_q2.reshape(1, -1),
      lam_k2.reshape(1, -1), subln_g.reshape(1, -1))


def _dil_kernel(q_ref, kp_ref, kc_ref, vp_ref, vc_ref, o_ref, lse_ref, *, rel, nq, nres):
    n = pl.program_id(2)
    qi = lax.broadcasted_iota(jnp.int32, (QBLK, 2 * QBLK), 0)
    kj = lax.broadcasted_iota(jnp.int32, (QBLK, 2 * QBLK), 1)
    dist = qi + QBLK - kj
    band = (dist >= 0) & (dist <= rel)
    lane = lax.broadcasted_iota(jnp.int32, (QBLK, LANES), 1)
    low = lane < HEAD_DIM
    slabs = [slice(p * LANES, (p + 1) * LANES) for p in range(SEG_W // LANES)]

    def window(prev_ref, cur_ref, s, sl):
        if s == 0:
            return jnp.concatenate([prev_ref[:, sl], cur_ref[:QBLK, sl]], axis=0)
        return cur_ref[(s - 1) * QBLK:(s + 1) * QBLK, sl]

    def score_block(blk):
        res, s = blk
        rows = slice(s * QBLK, (s + 1) * QBLK)
        scores = []
        for sl in slabs:
            q2 = q_ref[res, rows, sl]
            k2 = window(kp_ref.at[res], kc_ref.at[res], s, sl)
            for sel in (low, jnp.logical_not(low)):
                scores.append(_nt_dot(jnp.where(sel, q2, jnp.zeros_like(q2)), k2))
        return scores

    def finish_block(blk, scores):
        res, s = blk
        rows = slice(s * QBLK, (s + 1) * QBLK)
        valid = band & ((kj >= QBLK) | (n * nq + s > 0))
        probs = []
        for sc in scores:
            sc = jnp.where(valid, sc, -jnp.inf)
            m = jnp.max(sc, axis=1, keepdims=True)
            probs.append((jnp.exp2((sc - m).astype(BF16)), m))
        one = jnp.ones((2 * QBLK, LANES), BF16)
        low_keys = lax.broadcasted_iota(jnp.int32, (2 * QBLK, LANES), 1) < HEAD_DIM
        for p, sl in enumerate(slabs):
            v2 = window(vp_ref.at[res], vc_ref.at[res], s, sl)
            (pe_lo, m_lo), (pe_hi, m_hi) = probs[2 * p:2 * p + 2]
            pv_lo = jnp.dot(pe_lo, jnp.where(low_keys, v2, one), preferred_element_type=F32)
            pv_hi = jnp.dot(pe_hi, jnp.where(low_keys, one, v2), preferred_element_type=F32)
            num = jnp.where(low, pv_lo, pv_hi)
            den = pltpu.roll(jnp.where(low, pv_hi, pv_lo), HEAD_DIM, 1)
            o_ref[res, rows, sl] = (num / den).astype(BF16)
            lse_ref[res, rows, sl] = jnp.where(low, m_lo, m_hi) + jnp.log2(den)

    blocks = [(res, s) for res in range(nres) for s in range(nq)]
    pending = score_block(blocks[0])
    for prev, blk in zip(blocks, blocks[1:]):
        nxt = score_block(blk)
        finish_block(prev, pending)
        pending = nxt
    finish_block(blocks[-1], pending)


def _dilated_group(qkv, batch, seq, group):
    window, dil = DIL_PAIRS[group]
    rel = window // dil
    length = seq // dil
    nb = length // QBLK
    nq = math.gcd(nb, DIL_NQ)
    nres = math.gcd(dil, DIL_NQ // nq)
    view = qkv.reshape(batch, dil, length, QKV_W)

    def spec(seg, prev):
        if prev:
            return pl.BlockSpec((None, nres, QBLK, SEG_W), lambda b, r, n: (b, r, jnp.maximum(n * nq - 1, 0), seg))
        return pl.BlockSpec((None, nres, nq * QBLK, SEG_W), lambda b, r, n: (b, r, n, seg))

    out_spec = pl.BlockSpec((None, nres, nq * QBLK, SEG_W), lambda b, r, n: (b, r, n, 0))
    o, lse = pl.pallas_call(
        functools.partial(_dil_kernel, rel=rel, nq=nq, nres=nres),
        grid=(batch, dil // nres, nb // nq),
        in_specs=[spec(0, False), spec(1, True), spec(1, False), spec(2, True), spec(2, False)],
        out_specs=[out_spec, out_spec],
        out_shape=[jax.ShapeDtypeStruct((batch, dil, length, SEG_W), BF16),
                   jax.ShapeDtypeStruct((batch, dil, length, SEG_W), F32)],
        compiler_params=_cparams(3),
        name=f"dilated{group}",
    )(view, view, view, view, view)
    return o, lse


def _layer_norm(z, g, b):
    mu = jnp.mean(z, axis=1, keepdims=True)
    zc = z - mu
    var = jnp.mean(zc * zc, axis=1, keepdims=True)
    return zc * lax.rsqrt(var + EPS) * g + b


def _pack_bf16_pairs(v):
    n = v.shape[1] // 2
    lo = pltpu.bitcast(v[:, :n].astype(BF16).astype(F32), U32)
    hi = pltpu.bitcast(v[:, n:].astype(BF16).astype(F32), U32)
    return jnp.bitwise_or(jnp.right_shift(lo, jnp.uint32(16)), jnp.bitwise_and(hi, jnp.uint32(0xFFFF0000)))


def _unpack_bf16_pairs(w):
    lo = pltpu.bitcast(jnp.left_shift(w, jnp.uint32(16)), F32)
    hi = pltpu.bitcast(jnp.bitwise_and(w, jnp.uint32(0xFFFF0000)), F32)
    return lo, hi


def _merge_kernel(ya_ref, o1_ref, o2_ref, o3_ref, l1_ref, l2_ref, l3_ref, x_ref,
                  wg_ref, bg_ref, woa_ref, wob_ref, wo_ref, g_ref, b_ref, wrh_ref, wrl_ref, br_ref,
                  h_ref, hp_ref, topi_ref, topw_ref, rank_ref, cnt_ref, carry_ref, scr_ref, wgb_ref):
    i = pl.program_id(0)
    tm = x_ref.shape[0]

    @pl.when(i == 0)
    def _():
        carry_ref[...] = jnp.zeros_like(carry_ref)
        wgb_ref[...] = wg_ref[...].astype(BF16)

    hm = tm // 2
    halves = (slice(0, hm), slice(hm, tm))
    xs = [x_ref[rows, :] for rows in halves]
    xbs = [x.astype(BF16) for x in xs]
    gate_dot = lambda xb: jnp.dot(xb, wgb_ref[...], preferred_element_type=F32) + bg_ref[...]
    gate_pre = [gate_dot(xbs[0])]
    pas = [jnp.dot(ya_ref[rows, :], woa_ref[...], preferred_element_type=F32) for rows in halves]

    def token_major(ref, scr):
        dil, n = ref.shape[0], ref.shape[1]
        if dil == 1:
            return ref[0].astype(F32)
        for r in range(dil):
            blk = ref[r].astype(F32)
            for c in range(SEG_W // LANES):
                scr[c, pl.ds(r, n, stride=dil), :] = blk[:, c * LANES:(c + 1) * LANES]
        return jnp.concatenate([scr[c] for c in range(SEG_W // LANES)], axis=1)

    oa, ob, oc = (token_major(r, scr_ref.at[k]) for k, r in enumerate((o1_ref, o2_ref, o3_ref)))
    la, lb, lc = (token_major(r, scr_ref.at[3 + k]) for k, r in enumerate((l1_ref, l2_ref, l3_ref)))
    mx = jnp.maximum(jnp.maximum(la, lb), lc)
    ea, eb, ec = jnp.exp2(la - mx), jnp.exp2(lb - mx), jnp.exp2(lc - mx)
    yb = ((ea * oa + eb * ob + ec * oc) / (ea + eb + ec)).astype(BF16)
    pbs = [jnp.dot(yb[rows, :], wob_ref[...], preferred_element_type=F32) for rows in halves]
    gate_pre.append(gate_dot(xbs[1]))
    mixes = []
    for pre_act, pa, pb in zip(gate_pre, pas, pbs):
        gates = jax.nn.sigmoid(pre_act)
        merged = gates[:, :D_MODEL] * pa + gates[:, D_MODEL:] * pb
        mixes.append(jnp.dot(merged.astype(BF16), wo_ref[...], preferred_element_type=F32))
    logits = []
    for rows, x, mix in zip(halves, xs, mixes):
        h = _layer_norm(DN_ALPHA * x + mix, g_ref[...], b_ref[...])
        h_ref[rows, :] = h
        hp_ref[rows, :] = _pack_bf16_pairs(h)
        hh = h.astype(BF16)
        hl = (h - hh.astype(F32)).astype(BF16)
        logits.append(jnp.dot(hh, wrh_ref[...], preferred_element_type=F32)
                      + jnp.dot(hl, wrh_ref[...], preferred_element_type=F32)
                      + jnp.dot(hh, wrl_ref[...], preferred_element_type=F32) + br_ref[...])

    lane = lax.broadcasted_iota(jnp.int32, (hm, LANES), 1)
    r_i = lax.broadcasted_iota(jnp.int32, (hm, hm), 0)
    c_i = lax.broadcasted_iota(jnp.int32, (hm, hm), 1)
    tri = (r_i > c_i).astype(BF16)
    before = carry_ref[...]
    for rows, lg in zip(halves, logits):
        lg = jnp.where(lane < N_EXPERTS, lg, -jnp.inf)
        vals, idxs = [], []
        for _ in range(TOP_K):
            mv = jnp.max(lg, axis=1, keepdims=True)
            ik = jnp.min(jnp.where(lg == mv, lane, LANES), axis=1, keepdims=True)
            vals.append(mv)
            idxs.append(ik)
            lg = jnp.where(lane == ik, -jnp.inf, lg)
        es = [jnp.exp(v - vals[0]) for v in vals]
        tot = es[0] + es[1] + es[2] + es[3]
        onehot = jnp.zeros((hm, LANES), F32)
        for ik in idxs:
            onehot = onehot + (lane == ik).astype(F32)
        pre = jnp.dot(tri, onehot.astype(BF16), preferred_element_type=F32) + before
        topi = jnp.zeros((hm, LANES), jnp.int32)
        topw = jnp.zeros((hm, LANES), F32)
        rank = jnp.zeros((hm, LANES), F32)
        for k in range(TOP_K):
            rk = jnp.sum(jnp.where(lane == idxs[k], pre, 0.0), axis=1, keepdims=True)
            topi = jnp.where(lane == k, idxs[k], topi)
            topw = jnp.where(lane == k, es[k] / tot, topw)
            rank = jnp.where(lane == k, rk, rank)
        topi_ref[rows, :] = topi
        topw_ref[rows, :] = topw
        rank_ref[rows, :] = rank.astype(jnp.int32)
        before = before + jnp.sum(onehot, axis=0, keepdims=True)
    carry_ref[...] = before
    cnt_ref[...] = before


def _merge(ya, dil_outs, x2, seq, w_b, b_gate, w_oa_b, w_ob_b, w_o_b, ln1_g, ln1_b, w_router, b_router):
    tokens = x2.shape[0]
    tm = MERGE_TM
    per_seq = seq // tm

    def group(a):
        dil = a.shape[1]
        return pl.BlockSpec((None, dil, tm // dil, SEG_W), lambda i: (i // per_seq, 0, i % per_seq, 0))

    gate_blk = w_b.shape[1] // (2 * D_MODEL) - 1
    wr = jnp.zeros((D_MODEL, LANES), F32).at[:, :N_EXPERTS].set(w_router)
    wr_hi = wr.astype(BF16)
    wr_lo = (wr - wr_hi.astype(F32)).astype(BF16)
    br = jnp.zeros((1, LANES), F32).at[0, :N_EXPERTS].set(b_router)
    row = lambda w: pl.BlockSpec((tm, w), lambda i: (i, 0))
    full = lambda a: pl.BlockSpec(a.shape, lambda i: (0,) * a.ndim)
    (o1, l1), (o2, l2), (o3, l3) = dil_outs
    bg = b_gate.reshape(1, -1)
    g1 = ln1_g.reshape(1, -1)
    b1 = ln1_b.reshape(1, -1)
    lane_out = lambda dt: jax.ShapeDtypeStruct((tokens, LANES), dt)
    return pl.pallas_call(
        _merge_kernel,
        grid=(tokens // tm,),
        in_specs=[row(SEG_W), group(o1), group(o2), group(o3), group(l1), group(l2), group(l3),
                  row(D_MODEL), pl.BlockSpec((D_MODEL, 2 * D_MODEL), lambda i: (0, gate_blk)),
                  full(bg), full(w_oa_b), full(w_ob_b), full(w_o_b), full(g1), full(b1),
                  full(wr_hi), full(wr_lo), full(br)],
        out_specs=[row(D_MODEL), row(D_MODEL // 2), row(LANES), row(LANES), row(LANES),
                   pl.BlockSpec((1, LANES), lambda i: (0, 0))],
        out_shape=[jax.ShapeDtypeStruct((tokens, D_MODEL), F32), jax.ShapeDtypeStruct((tokens, D_MODEL // 2), U32),
                   lane_out(jnp.int32), lane_out(F32), lane_out(jnp.int32), jax.ShapeDtypeStruct((1, LANES), F32)],
        scratch_shapes=[pltpu.VMEM((1, LANES), F32), pltpu.VMEM((6, SEG_W // LANES, tm, LANES), F32),
                        pltpu.VMEM((D_MODEL, 2 * D_MODEL), BF16)],
        compiler_params=_cparams(1),
        name="merge",
    )(ya, o1, o2, o3, l1, l2, l3, x2, w_b, bg, w_oa_b, w_ob_b, w_o_b, g1, b1, wr_hi, wr_lo, br)


def _sc_mesh():
    return plsc.VectorSubcoreMesh(core_axis_name="core", subcore_axis_name="subcore")


def _sc_scatter_rows(table, src_idx, dst_idx):
    n = src_idx.shape[0]
    d = table.shape[1]
    mesh = _sc_mesh()
    workers = mesh.num_cores * mesh.num_subcores
    per = n // (SC_CHUNK * workers)
    assert per * SC_CHUNK * workers == n

    @pl.kernel(out_type=jax.ShapeDtypeStruct((n, d), table.dtype), mesh=mesh,
               scratch_types=[pltpu.VMEM((1, SC_CHUNK), jnp.int32), pltpu.VMEM((1, SC_CHUNK), jnp.int32),
                              pltpu.VMEM((SC_CHUNK, d), table.dtype)])
    def copy(t_hbm, s_hbm, d_hbm, o_hbm, s_vm, d_vm, buf):
        wid = lax.axis_index("core") * mesh.num_subcores + lax.axis_index("subcore")

        @pl.loop(0, per)
        def _(j):
            blk = wid * per + j
            pltpu.sync_copy(s_hbm.at[pl.ds(blk, 1)], s_vm)
            pltpu.sync_copy(d_hbm.at[pl.ds(blk, 1)], d_vm)
            pltpu.sync_copy(t_hbm.at[s_vm.at[0]], buf)
            pltpu.sync_copy(buf, o_hbm.at[d_vm.at[0]])

    return copy(table, src_idx.reshape(-1, SC_CHUNK), dst_idx.reshape(-1, SC_CHUNK))


def _sc_gather_rows(table, idx):
    n = idx.shape[0]
    d = table.shape[1]
    mesh = _sc_mesh()
    workers = mesh.num_cores * mesh.num_subcores
    per = n // (SC_CHUNK * workers)
    assert per * SC_CHUNK * workers == n

    @pl.kernel(out_type=jax.ShapeDtypeStruct((n, d), table.dtype), mesh=mesh,
               scratch_types=[pltpu.VMEM((1, SC_CHUNK), jnp.int32), pltpu.VMEM((SC_CHUNK, d), table.dtype)])
    def gather(t_hbm, i_hbm, o_hbm, i_vm, buf):
        wid = lax.axis_index("core") * mesh.num_subcores + lax.axis_index("subcore")

        @pl.loop(0, per)
        def _(j):
            blk = wid * per + j
            pltpu.sync_copy(i_hbm.at[pl.ds(blk, 1)], i_vm)
            pltpu.sync_copy(t_hbm.at[i_vm.at[0]], buf)
            pltpu.sync_copy(buf, o_hbm.at[pl.ds(blk * SC_CHUNK, SC_CHUNK)])

    return gather(table, idx.reshape(-1, SC_CHUNK))


def _expert_kernel(be_ref, nx_ref, nu_ref, x_ref, wgu_hbm, bgu_ref, wd_hbm, bd_ref, y_ref,
                   wgu_stage, wd_stage, wgu_b, wd_b, sem):
    i = pl.program_id(0)
    used = i < nu_ref[0]
    expert = be_ref[i]

    def weight_copies(e):
        return (pltpu.make_async_copy(wgu_hbm.at[e], wgu_stage, sem.at[0]),
                pltpu.make_async_copy(wd_hbm.at[e], wd_stage, sem.at[1]))

    @pl.when(i == 0)
    def _():
        for cp in weight_copies(expert):
            cp.start()

    @pl.when(used & ((i == 0) | (expert != be_ref[jnp.maximum(i - 1, 0)])))
    def _():
        for cp in weight_copies(expert):
            cp.wait()
        wgu_b[...] = wgu_stage[...].astype(BF16)
        wd_b[...] = wd_stage[...].astype(BF16)

        @pl.when(nx_ref[i] >= 0)
        def _():
            for cp in weight_copies(nx_ref[i]):
                cp.start()

    @pl.when(used)
    def _():
        lo, hi = _unpack_bf16_pairs(x_ref[...])
        xb = jnp.concatenate([lo, hi], axis=1).astype(BF16)
        cols = [slice(c * FF_CHUNK, (c + 1) * FF_CHUNK) for c in range(D_FF // FF_CHUNK)]
        pre = []
        for gs in cols:
            us = slice(D_FF + gs.start, D_FF + gs.stop)
            gate = jnp.dot(xb, wgu_b[:, gs], preferred_element_type=F32) + bgu_ref[:, gs]
            up = jnp.dot(xb, wgu_b[:, us], preferred_element_type=F32) + bgu_ref[:, us]
            pre.append((gate, up))
        acc = jnp.zeros((x_ref.shape[0], D_MODEL), F32)
        for gs, (gate, up) in zip(cols, pre):
            gate = jnp.minimum(gate, SWIGLU_LIMIT)
            up = jnp.clip(up, -SWIGLU_LIMIT, SWIGLU_LIMIT)
            act = (up + 1.0) * gate * jax.nn.sigmoid(SWIGLU_ALPHA * gate)
            acc = acc + jnp.dot(act.astype(BF16), wd_b[gs, :], preferred_element_type=F32)
        y_ref[...] = _pack_bf16_pairs(acc + bd_ref[...])

    @pl.when(i >= nu_ref[0])
    def _():
        y_ref[...] = jnp.zeros_like(y_ref)


def _experts(xrows, block_e, next_e, n_used, w_gu, b_gu, w_down, b_down):
    n_rows = xrows.shape[0]
    bm = MOE_BM
    bias = lambda i, be, nx, nu: (be[i], 0, 0)
    return pl.pallas_call(
        _expert_kernel,
        grid_spec=pltpu.PrefetchScalarGridSpec(
            num_scalar_prefetch=3,
            grid=(n_rows // bm,),
            in_specs=[
                pl.BlockSpec((bm, D_MODEL // 2), lambda i, be, nx, nu: (i, 0)),
                pl.BlockSpec(memory_space=pl.ANY),
                pl.BlockSpec((None, 1, 2 * D_FF), bias),
                pl.BlockSpec(memory_space=pl.ANY),
                pl.BlockSpec((None, 1, D_MODEL), bias),
            ],
            out_specs=pl.BlockSpec((bm, D_MODEL // 2), lambda i, be, nx, nu: (i, 0)),
            scratch_shapes=[pltpu.VMEM((D_MODEL, 2 * D_FF), F32), pltpu.VMEM((D_FF, D_MODEL), F32),
                            pltpu.VMEM((D_MODEL, 2 * D_FF), BF16), pltpu.VMEM((D_FF, D_MODEL), BF16),
                            pltpu.SemaphoreType.DMA((2,))],
        ),
        out_shape=jax.ShapeDtypeStruct((n_rows, D_MODEL // 2), U32),
        compiler_params=_cparams(1),
        name="experts",
    )(block_e, next_e, n_used, xrows, w_gu, b_gu.reshape(N_EXPERTS, 1, -1), w_down, b_down.reshape(N_EXPERTS, 1, -1))


def _combine_kernel(y0_ref, y1_ref, y2_ref, y3_ref, w_ref, h_ref, g_ref, b_ref, o_ref):
    w = w_ref[...]
    half = D_MODEL // 2
    f_lo = jnp.zeros((h_ref.shape[0], half), F32)
    f_hi = jnp.zeros((h_ref.shape[0], half), F32)
    for k, y_ref in enumerate((y0_ref, y1_ref, y2_ref, y3_ref)):
        lo, hi = _unpack_bf16_pairs(y_ref[...])
        f_lo = f_lo + lo * w[:, k:k + 1]
        f_hi = f_hi + hi * w[:, k:k + 1]
    f = jnp.concatenate([f_lo, f_hi], axis=1)
    o_ref[...] = _layer_norm(DN_ALPHA * h_ref[...] + f, g_ref[...], b_ref[...])


def _combine(yplanes, topw, h, ln2_g, ln2_b):
    tokens = h.shape[0]
    tm = MOE_TM
    per_plane = tokens // tm
    plane = lambda k: pl.BlockSpec((tm, D_MODEL // 2), lambda i: (k * per_plane + i, 0))
    return pl.pallas_call(
        _combine_kernel,
        grid=(per_plane,),
        in_specs=[plane(0), plane(1), plane(2), plane(3),
                  pl.BlockSpec((tm, LANES), lambda i: (i, 0)),
                  pl.BlockSpec((tm, D_MODEL), lambda i: (i, 0)),
                  pl.BlockSpec((1, D_MODEL), lambda i: (0, 0)),
                  pl.BlockSpec((1, D_MODEL), lambda i: (0, 0))],
        out_specs=pl.BlockSpec((tm, D_MODEL), lambda i: (i, 0)),
        out_shape=jax.ShapeDtypeStruct((tokens, D_MODEL), F32),
        compiler_params=_cparams(1),
        name="combine",
    )(yplanes, yplanes, yplanes, yplanes, topw, h, ln2_g.reshape(1, -1), ln2_b.reshape(1, -1))


def _routing_tables(topi, rank, cnt, tokens):
    bm = MOE_BM
    i32 = jnp.int32
    experts = jnp.arange(N_EXPERTS, dtype=i32)
    counts = cnt[0, :N_EXPERTS].astype(i32)
    padded = (counts + bm - 1) // bm * bm
    pend = jnp.cumsum(padded)
    pstart = pend - padded
    sel = topi[:, :TOP_K, None] == experts[None, None, :]
    dest = rank[:, :TOP_K] + jnp.sum(jnp.where(sel, pstart[None, None, :], 0), axis=-1)
    n_pad = N_EXPERTS * bm
    n_rows = tokens * TOP_K + n_pad
    starts = jnp.arange(n_rows // bm, dtype=i32) * bm
    block_e = jnp.minimum(jnp.sum((pend[None, :] <= starts[:, None]).astype(i32), axis=1), N_EXPERTS - 1)
    n_used = (pend[-1] // bm).reshape(1)
    is_block_e = block_e[:, None] == experts[None, :]
    after = jnp.sum(jnp.where(is_block_e, pend[None, :], 0), axis=1) // bm
    e_after = jnp.sum(jnp.where(after[:, None] == jnp.arange(n_rows // bm, dtype=i32)[None, :], block_e[None, :], 0),
                      axis=1)
    next_e = jnp.where(after < n_used[0], e_after, -1)
    pad_cnt = padded - counts
    pad_end = jnp.cumsum(pad_cnt)
    j = jnp.arange(n_pad, dtype=i32)
    owner = jnp.sum((pad_end[None, :] <= j[:, None]).astype(i32), axis=1)
    is_owner = owner[:, None] == experts[None, :]
    pick = lambda v: jnp.sum(jnp.where(is_owner, v[None, :], 0), axis=1)
    in_expert = pick(pstart + counts) + (j - pick(pad_end - pad_cnt))
    pad_rows = jnp.where(owner < N_EXPERTS, in_expert, pend[-1] + (j - pad_end[-1]))
    src_tok = jnp.concatenate([jnp.repeat(jnp.arange(tokens, dtype=i32), TOP_K), j % tokens])
    dst_row = jnp.concatenate([dest.reshape(-1), pad_rows]).astype(i32)
    return dest.astype(i32), block_e.astype(i32), next_e.astype(i32), n_used.astype(i32), src_tok, dst_row


def kernel(x, w_in, b_gate, lam_q1, lam_k1, lam_q2, lam_k2, subln_g, w_oa, w_ob, w_o, ln1_g, ln1_b,
           w_router, b_router, w_gu, b_gu, w_down, b_down, ln2_g, ln2_b):
    batch, seq, d = x.shape
    tokens = batch * seq
    h = x.reshape(tokens, d)
    for l in range(DEPTH):
        q_scale = HEAD_DIM ** -0.5 * math.log2(math.e)
        qk_a, vt_a = _project_qkv(h, w_in[l], 0, seq, "qkv_diff", q_scale=q_scale, v_feature_major=True)
        ya = _diff_attention(qk_a.reshape(batch, seq, 2 * SEG_W), vt_a, lam_q1[l], lam_k1[l], lam_q2[l],
                             lam_k2[l], subln_g[l]).reshape(tokens, -1)
        dil = []
        for g, (_, dilation) in enumerate(DIL_PAIRS):
            qkv_g = _project_qkv(h, w_in[l], g + 1, seq, f"qkv_dil{g}", q_scale=q_scale, dil=dilation)
            dil.append(_dilated_group(qkv_g, batch, seq, g))
        h1, h1p, topi, topw, rank, cnt = _merge(ya, dil, h, seq, w_in[l], b_gate[l], w_oa[l].astype(BF16),
                                                w_ob[l].astype(BF16), w_o[l].astype(BF16), ln1_g[l], ln1_b[l],
                                                w_router[l], b_router[l])
        dest, block_e, next_e, n_used, src_tok, dst_row = _routing_tables(topi, rank, cnt, tokens)
        xrows = _sc_scatter_rows(h1p, src_tok, dst_row)
        yrows = _experts(xrows, block_e, next_e, n_used, w_gu[l], b_gu[l], w_down[l], b_down[l])
        yplanes = _sc_gather_rows(yrows, dest.T.reshape(-1))
        h = _combine(yplanes, topw, h1, ln2_g[l], ln2_b[l])
    return h.reshape(batch, seq, d)
```

```python
import functools
import math

import jax
import jax.numpy as jnp
from jax import lax
from jax.experimental import pallas as pl
from jax.experimental.pallas import tpu as pltpu
from jax.experimental.pallas import tpu_sc as plsc

F32 = jnp.float32
BF16 = jnp.bfloat16
U32 = jnp.uint32

D_MODEL = 1024
HEAD_DIM = 64
ROT_DIM = HEAD_DIM // 4
ROPE_THETA = 500000.0
QBLK = 128
DA_HEADS = 4
DIL_PAIRS = ((128, 1), (512, 4), (2048, 16))
SEG_W = 512
QKV_W = 3 * SEG_W
N_EXPERTS = 32
TOP_K = 4
D_FF = D_MODEL
SWIGLU_ALPHA = 1.702
SWIGLU_LIMIT = 7.0
DEPTH = 1
DN_ALPHA = (2 * DEPTH) ** 0.25
EPS = 1e-5
LAMBDA_INIT = 0.8 - 0.6 * math.exp(-0.3 * 0)

LANES = 128
QKV_TM = 1024
QKV_CHUNK = 256
STRIDE_STAGE = 4
ATT_TQ = 256
ONES_ROWS = 16
DIL_NQ = 8
FF_CHUNK = 512
MERGE_TM = 512
MOE_BM = 256
MOE_TM = 512
SC_CHUNK = 128
VMEM_LIMIT = 52 * 1024 * 1024


def _cparams(n_axes):
    return pltpu.CompilerParams(dimension_semantics=("arbitrary",) * n_axes,
                                vmem_limit_bytes=VMEM_LIMIT)


def _qkv_kernel(x_ref, w_ref, cs_ref, o_ref, *rest, q_scale, dil):
    tm = x_ref.shape[0]
    n = tm // dil
    two_stage = dil > STRIDE_STAGE
    if two_stage:
        *maybe_vt_ref, wb_ref, slab_ref, slab2_ref = rest
    elif dil > 1:
        *maybe_vt_ref, wb_ref, slab_ref = rest
    else:
        *maybe_vt_ref, wb_ref = rest

    @pl.when(pl.program_id(0) == 0)
    def _():
        wb_ref[...] = w_ref[...].astype(BF16)

    if dil > 1:
        for c in range(D_MODEL // LANES):
            slab_ref[c] = x_ref[:, c * LANES:(c + 1) * LANES]
    if two_stage:
        part = tm // STRIDE_STAGE
        for c in range(D_MODEL // LANES):
            for q in range(STRIDE_STAGE):
                slab2_ref[c, q * part:(q + 1) * part, :] = slab_ref[c, pl.ds(q, part, stride=STRIDE_STAGE), :]

    def residue_rows(c, r, l0, count):
        if two_stage:
            first = (r % STRIDE_STAGE) * (tm // STRIDE_STAGE) + r // STRIDE_STAGE + l0 * (dil // STRIDE_STAGE)
            return slab2_ref[c, pl.ds(first, count, stride=dil // STRIDE_STAGE), :]
        return slab_ref[c, pl.ds(l0 * dil + r, count, stride=dil), :]

    def regrouped_rows(start):
        pieces = []
        for p in range(start, start + QKV_CHUNK, min(n, QKV_CHUNK)):
            pieces.append(jnp.concatenate(
                [residue_rows(c, p // n, p % n, min(n, QKV_CHUNK)) for c in range(D_MODEL // LANES)], axis=1))
        return jnp.concatenate(pieces, axis=0).astype(BF16)

    def store(rows, cols, val):
        if dil == 1:
            o_ref[rows, cols] = val
        elif n >= QKV_CHUNK:
            o_ref[rows.start // n, rows.start % n:rows.start % n + QKV_CHUNK, cols] = val
        else:
            per = QKV_CHUNK // n
            for q in range(per):
                o_ref[rows.start // n + q, :, cols] = val[q * n:(q + 1) * n]

    half = ROT_DIM // 2
    in_head = lax.broadcasted_iota(jnp.int32, (QKV_CHUNK, LANES), 1) % HEAD_DIM
    first, second = in_head < half, (in_head >= half) & (in_head < ROT_DIM)
    for rc in range(tm // QKV_CHUNK):
        rows = slice(rc * QKV_CHUNK, (rc + 1) * QKV_CHUNK)
        xb = regrouped_rows(rows.start) if dil > 1 else x_ref[rows, :].astype(BF16)
        cs = cs_ref[rows, :]
        c = jnp.where(first, cs, jnp.where(second, pltpu.roll(cs, half, 1), 1.0))
        s1 = jnp.where(second, cs, 0.0)
        s2 = jnp.where(first, -pltpu.roll(cs, LANES - half, 1), 0.0)
        for seg in range(3):
            acc = jnp.dot(xb, wb_ref[:, seg * SEG_W:(seg + 1) * SEG_W], preferred_element_type=F32)
            if seg == 2:
                if maybe_vt_ref:
                    maybe_vt_ref[0][:, rows] = acc.T.astype(BF16)
                else:
                    store(rows, slice(seg * SEG_W, (seg + 1) * SEG_W), acc.astype(BF16))
                continue
            for k in range(SEG_W // LANES):
                t = acc[:, k * LANES:(k + 1) * LANES]
                r = t * c + pltpu.roll(t, ROT_DIM // 2, 1) * s1 + pltpu.roll(t, LANES - ROT_DIM // 2, 1) * s2
                if seg == 0:
                    r = r * q_scale
                lo = seg * SEG_W + k * LANES
                store(rows, slice(lo, lo + LANES), r.astype(BF16))


def _rope_lane_table(seq, dil, tm):
    row = jnp.arange(seq, dtype=jnp.int32)
    n = tm // dil
    in_tile = row % tm
    pos = (row - in_tile + (in_tile % n) * dil + in_tile // n).astype(F32)
    inv_freq = ROPE_THETA ** (-jnp.arange(0, ROT_DIM, 2, dtype=F32) / ROT_DIM)
    ang = pos[:, None] * inv_freq[None, :]
    head = jnp.concatenate([jnp.cos(ang), jnp.sin(ang), jnp.zeros((seq, HEAD_DIM - ROT_DIM), F32)], axis=1)
    return jnp.tile(head, (1, LANES // HEAD_DIM))


def _project_qkv(x2, w, col_block, seq, name, q_scale, dil=1, v_feature_major=False):
    tokens = x2.shape[0]
    batch = tokens // seq
    tm = min(QKV_TM, seq)
    per_seq = seq // tm
    n = tm // dil
    out_w = 2 * SEG_W if v_feature_major else QKV_W
    scratch = [pltpu.VMEM((D_MODEL, QKV_W), BF16)]
    if dil == 1:
        out_specs = [pl.BlockSpec((tm, out_w), lambda i: (i, 0))]
        out_shape = [jax.ShapeDtypeStruct((tokens, out_w), BF16)]
    else:
        out_specs = [pl.BlockSpec((None, dil, n, out_w), lambda i: (i // per_seq, 0, i % per_seq, 0))]
        out_shape = [jax.ShapeDtypeStruct((batch, dil, seq // dil, out_w), BF16)]
        scratch += [pltpu.VMEM((D_MODEL // LANES, tm, LANES), F32)] * (2 if dil > STRIDE_STAGE else 1)
    if v_feature_major:
        out_specs.append(pl.BlockSpec((None, SEG_W, tm), lambda i: (i // per_seq, 0, i % per_seq)))
        out_shape.append(jax.ShapeDtypeStruct((batch, SEG_W, seq), BF16))
    outs = pl.pallas_call(
        functools.partial(_qkv_kernel, q_scale=q_scale, dil=dil),
        grid=(tokens // tm,),
        in_specs=[pl.BlockSpec((tm, D_MODEL), lambda i: (i, 0)),
                  pl.BlockSpec((D_MODEL, QKV_W), lambda i: (0, col_block)),
                  pl.BlockSpec((tm, LANES), lambda i: (i % per_seq, 0))],
        out_specs=out_specs,
        out_shape=out_shape,
        scratch_shapes=scratch,
        compiler_params=_cparams(1),
        name=name,
    )(x2, w, _rope_lane_table(seq, dil, tm))
    return outs if v_feature_major else outs[0]


def _nt_dot(a, b):
    return lax.dot_general(a, b, (((1,), (1,)), ((), ())), preferred_element_type=F32)


def _diff_kernel(q_ref, k_ref, vt_ref, lq1_ref, lk1_ref, lq2_ref, lk2_ref, g_ref, o_ref, s_ref, acc_ref, *, tq):
    i = pl.program_id(1)
    chains = 2 * DA_HEADS
    half = tq // 2
    lane = lax.broadcasted_iota(jnp.int32, (tq, LANES), 1)
    qs = []
    for h in range(DA_HEADS):
        q = q_ref[:, h * LANES:(h + 1) * LANES]
        zero = jnp.zeros_like(q)
        qs += [jnp.where(lane < HEAD_DIM, q, zero), jnp.where(lane >= HEAD_DIM, q, zero)]
    acc_ref[...] = jnp.zeros_like(acc_ref)

    def score_chain(j, buf, c):
        start = pl.multiple_of(j * tq, tq)
        kb = k_ref[pl.ds(start, tq), (c // 2) * LANES:(c // 2 + 1) * LANES]
        buf[c] = _nt_dot(kb, qs[c])

    def score_block(j, buf):
        for c in range(chains):
            score_chain(j, buf, c)

    def absorb(j, buf, state, masked, nxt=None):
        start = pl.multiple_of(j * tq, tq)
        out = []
        for c in range(chains):
            if nxt is not None:
                score_chain(j + 1, nxt, c)
            m, l = state[c]
            if masked:
                key = lax.broadcasted_iota(jnp.int32, (tq, tq), 0)
                qry = lax.broadcasted_iota(jnp.int32, (tq, tq), 1)
                buf[c] = jnp.where(key <= qry, buf[c], -jnp.inf)
            m_new = jnp.maximum(m, jnp.max(buf[c], axis=0, keepdims=True))
            a = jnp.exp2(m - m_new)
            p = [jnp.exp2((buf[c, u * half:(u + 1) * half, :] - m_new).astype(BF16)) for u in range(2)]
            vt = vt_ref[(c // 2) * LANES:(c // 2 + 1) * LANES, pl.ds(start, tq)]
            vt1 = jnp.concatenate([vt, jnp.ones((ONES_ROWS, tq), BF16)], axis=0)
            pv = jnp.dot(vt1, jnp.concatenate(p, axis=0), preferred_element_type=F32)
            l = a * l + pv[LANES:LANES + 1, :]
            acc_ref[c] = a * acc_ref[c] + pv[:LANES, :]
            out.append((m_new, l))
        return tuple(out)

    buf_a, buf_b = s_ref.at[0], s_ref.at[1]

    def two_blocks(jj, state):
        j = 2 * jj
        state = absorb(j, buf_a, state, False, nxt=buf_b)
        return absorb(j + 1, buf_b, state, False, nxt=buf_a)

    def last_from_a(state):
        return absorb(i, buf_a, state, True)

    def last_from_b(state):
        state = absorb(i - 1, buf_a, state, False, nxt=buf_b)
        return absorb(i, buf_b, state, True)

    init = tuple((jnp.full((1, tq), -jnp.inf, F32), jnp.zeros((1, tq), F32)) for _ in range(chains))
    score_block(0, buf_a)
    state = lax.fori_loop(0, i // 2, two_blocks, init)
    state = lax.cond(lax.rem(i, 2) == 0, last_from_a, last_from_b, state)

    lam = (jnp.exp(jnp.sum(lq1_ref[...] * lk1_ref[...], axis=1, keepdims=True))
           - jnp.exp(jnp.sum(lq2_ref[...] * lk2_ref[...], axis=1, keepdims=True)) + LAMBDA_INIT)
    for h in range(DA_HEADS):
        l1, l2 = state[2 * h][1], state[2 * h + 1][1]
        o = (acc_ref[2 * h] / l1 - lam * (acc_ref[2 * h + 1] / l2)).T
        ms = jnp.mean(o * o, axis=1, keepdims=True)
        o = o * lax.rsqrt(ms + EPS) * g_ref[...]
        o_ref[:, h * LANES:(h + 1) * LANES] = (o * (1.0 - LAMBDA_INIT)).astype(BF16)


def _diff_attention(qk3, vt3, lam_q1, lam_k1, lam_q2, lam_k2, subln_g):
    b, s, _ = qk3.shape
    tq = min(ATT_TQ, s)
    vec = lambda n: pl.BlockSpec((1, n), lambda bb, i: (0, 0))
    return pl.pallas_call(
        functools.partial(_diff_kernel, tq=tq),
        grid=(b, s // tq),
        in_specs=[
            pl.BlockSpec((None, tq, SEG_W), lambda bb, i: (bb, i, 0)),
            pl.BlockSpec((None, s, SEG_W), lambda bb, i: (bb, 0, 1)),
            pl.BlockSpec((None, SEG_W, s), lambda bb, i: (bb, 0, 0)),
            vec(HEAD_DIM), vec(HEAD_DIM), vec(HEAD_DIM), vec(HEAD_DIM), vec(2 * HEAD_DIM),
        ],
        out_specs=pl.BlockSpec((None, tq, SEG_W), lambda bb, i: (bb, i, 0)),
        out_shape=jax.ShapeDtypeStruct((b, s, SEG_W), BF16),
        scratch_shapes=[pltpu.VMEM((2, 2 * DA_HEADS, tq, tq), F32), pltpu.VMEM((2 * DA_HEADS, LANES, tq), F32)],
        compiler_params=_cparams(2),
        name="diff_attn",
    )(qk3, qk3, vt3, lam_q1.reshape(1, -1), lam_k1.reshape(1, -1), lam_q2.reshape(1, -1),
      lam_k2.reshape(1, -1), subln_g.reshape(1, -1))


def _dil_kernel(q_ref, kp_ref, kc_ref, vp_ref, vc_ref, o_ref, lse_ref, *, rel, nq, nres):
    n = pl.program_id(2)
    qi = lax.broadcasted_iota(jnp.int32, (QBLK, 2 * QBLK), 0)
    kj = lax.broadcasted_iota(jnp.int32, (QBLK, 2 * QBLK), 1)
    dist = qi + QBLK - kj
    band = (dist >= 0) & (dist <= rel)
    lane = lax.broadcasted_iota(jnp.int32, (QBLK, LANES), 1)
    low = lane < HEAD_DIM
    slabs = [slice(p * LANES, (p + 1) * LANES) for p in range(SEG_W // LANES)]

    def window(prev_ref, cur_ref, s, sl):
        if s == 0:
            return jnp.concatenate([prev_ref[:, sl], cur_ref[:QBLK, sl]], axis=0)
        return cur_ref[(s - 1) * QBLK:(s + 1) * QBLK, sl]

    def score_block(blk):
        res, s = blk
        rows = slice(s * QBLK, (s + 1) * QBLK)
        scores = []
        for sl in slabs:
            q2 = q_ref[res, rows, sl]
            k2 = window(kp_ref.at[res], kc_ref.at[res], s, sl)
            for sel in (low, jnp.logical_not(low)):
                scores.append(_nt_dot(jnp.where(sel, q2, jnp.zeros_like(q2)), k2))
        return scores

    def finish_block(blk, scores):
        res, s = blk
        rows = slice(s * QBLK, (s + 1) * QBLK)
        valid = band & ((kj >= QBLK) | (n * nq + s > 0))
        probs = []
        for sc in scores:
            sc = jnp.where(valid, sc, -jnp.inf)
            m = jnp.max(sc, axis=1, keepdims=True)
            probs.append((jnp.exp2((sc - m).astype(BF16)), m))
        one = jnp.ones((2 * QBLK, LANES), BF16)
        low_keys = lax.broadcasted_iota(jnp.int32, (2 * QBLK, LANES), 1) < HEAD_DIM
        for p, sl in enumerate(slabs):
            v2 = window(vp_ref.at[res], vc_ref.at[res], s, sl)
            (pe_lo, m_lo), (pe_hi, m_hi) = probs[2 * p:2 * p + 2]
            pv_lo = jnp.dot(pe_lo, jnp.where(low_keys, v2, one), preferred_element_type=F32)
            pv_hi = jnp.dot(pe_hi, jnp.where(low_keys, one, v2), preferred_element_type=F32)
            num = jnp.where(low, pv_lo, pv_hi)
            den = pltpu.roll(jnp.where(low, pv_hi, pv_lo), HEAD_DIM, 1)
            o_ref[res, rows, sl] = (num / den).astype(BF16)
            lse_ref[res, rows, sl] = jnp.where(low, m_lo, m_hi) + jnp.log2(den)

    blocks = [(res, s) for res in range(nres) for s in range(nq)]
    pending = score_block(blocks[0])
    for prev, blk in zip(blocks, blocks[1:]):
        nxt = score_block(blk)
        finish_block(prev, pending)
        pending = nxt
    finish_block(blocks[-1], pending)


def _dilated_group(qkv, batch, seq, group):
    window, dil = DIL_PAIRS[group]
    rel = window // dil
    length = seq // dil
    nb = length // QBLK
    nq = math.gcd(nb, DIL_NQ)
    nres = math.gcd(dil, DIL_NQ // nq)
    view = qkv.reshape(batch, dil, length, QKV_W)

    def spec(seg, prev):
        if prev:
            return pl.BlockSpec((None, nres, QBLK, SEG_W), lambda b, r, n: (b, r, jnp.maximum(n * nq - 1, 0), seg))
        return pl.BlockSpec((None, nres, nq * QBLK, SEG_W), lambda b, r, n: (b, r, n, seg))

    out_spec = pl.BlockSpec((None, nres, nq * QBLK, SEG_W), lambda b, r, n: (b, r, n, 0))
    o, lse = pl.pallas_call(
        functools.partial(_dil_kernel, rel=rel, nq=nq, nres=nres),
        grid=(batch, dil // nres, nb // nq),
        in_specs=[spec(0, False), spec(1, True), spec(1, False), spec(2, True), spec(2, False)],
        out_specs=[out_spec, out_spec],
        out_shape=[jax.ShapeDtypeStruct((batch, dil, length, SEG_W), BF16),
                   jax.ShapeDtypeStruct((batch, dil, length, SEG_W), F32)],
        compiler_params=_cparams(3),
        name=f"dilated{group}",
    )(view, view, view, view, view)
    return o, lse


def _layer_norm(z, g, b):
    mu = jnp.mean(z, axis=1, keepdims=True)
    zc = z - mu
    var = jnp.mean(zc * zc, axis=1, keepdims=True)
    return zc * lax.rsqrt(var + EPS) * g + b


def _pack_bf16_pairs(v):
    n = v.shape[1] // 2
    lo = pltpu.bitcast(v[:, :n].astype(BF16).astype(F32), U32)
    hi = pltpu.bitcast(v[:, n:].astype(BF16).astype(F32), U32)
    return jnp.bitwise_or(jnp.right_shift(lo, jnp.uint32(16)), jnp.bitwise_and(hi, jnp.uint32(0xFFFF0000)))


def _unpack_bf16_pairs(w):
    lo = pltpu.bitcast(jnp.left_shift(w, jnp.uint32(16)), F32)
    hi = pltpu.bitcast(jnp.bitwise_and(w, jnp.uint32(0xFFFF0000)), F32)
    return lo, hi


def _merge_kernel(ya_ref, o1_ref, o2_ref, o3_ref, l1_ref, l2_ref, l3_ref, x_ref,
                  wg_ref, bg_ref, woa_ref, wob_ref, wo_ref, g_ref, b_ref, wrh_ref, wrl_ref, br_ref,
                  h_ref, hp_ref, topi_ref, topw_ref, rank_ref, cnt_ref, carry_ref, scr_ref, wgb_ref):
    i = pl.program_id(0)
    tm = x_ref.shape[0]

    @pl.when(i == 0)
    def _():
        carry_ref[...] = jnp.zeros_like(carry_ref)
        wgb_ref[...] = wg_ref[...].astype(BF16)

    hm = tm // 2
    halves = (slice(0, hm), slice(hm, tm))
    xs = [x_ref[rows, :] for rows in halves]
    xbs = [x.astype(BF16) for x in xs]
    gate_dot = lambda xb: jnp.dot(xb, wgb_ref[...], preferred_element_type=F32) + bg_ref[...]
    gate_pre = [gate_dot(xbs[0])]
    pas = [jnp.dot(ya_ref[rows, :], woa_ref[...], preferred_element_type=F32) for rows in halves]

    def token_major(ref, scr):
        dil, n = ref.shape[0], ref.shape[1]
        if dil == 1:
            return ref[0].astype(F32)
        for r in range(dil):
            blk = ref[r].astype(F32)
            for c in range(SEG_W // LANES):
                scr[c, pl.ds(r, n, stride=dil), :] = blk[:, c * LANES:(c + 1) * LANES]
        return jnp.concatenate([scr[c] for c in range(SEG_W // LANES)], axis=1)

    oa, ob, oc = (token_major(r, scr_ref.at[k]) for k, r in enumerate((o1_ref, o2_ref, o3_ref)))
    la, lb, lc = (token_major(r, scr_ref.at[3 + k]) for k, r in enumerate((l1_ref, l2_ref, l3_ref)))
    mx = jnp.maximum(jnp.maximum(la, lb), lc)
    ea, eb, ec = jnp.exp2(la - mx), jnp.exp2(lb - mx), jnp.exp2(lc - mx)
    yb = ((ea * oa + eb * ob + ec * oc) / (ea + eb + ec)).astype(BF16)
    pbs = [jnp.dot(yb[rows, :], wob_ref[...], preferred_element_type=F32) for rows in halves]
    gate_pre.append(gate_dot(xbs[1]))
    mixes = []
    for pre_act, pa, pb in zip(gate_pre, pas, pbs):
        gates = jax.nn.sigmoid(pre_act)
        merged = gates[:, :D_MODEL] * pa + gates[:, D_MODEL:] * pb
        mixes.append(jnp.dot(merged.astype(BF16), wo_ref[...], preferred_element_type=F32))
    logits = []
    for rows, x, mix in zip(halves, xs, mixes):
        h = _layer_norm(DN_ALPHA * x + mix, g_ref[...], b_ref[...])
        h_ref[rows, :] = h
        hp_ref[rows, :] = _pack_bf16_pairs(h)
        hh = h.astype(BF16)
        hl = (h - hh.astype(F32)).astype(BF16)
        logits.append(jnp.dot(hh, wrh_ref[...], preferred_element_type=F32)
                      + jnp.dot(hl, wrh_ref[...], preferred_element_type=F32)
                      + jnp.dot(hh, wrl_ref[...], preferred_element_type=F32) + br_ref[...])

    lane = lax.broadcasted_iota(jnp.int32, (hm, LANES), 1)
    r_i = lax.broadcasted_iota(jnp.int32, (hm, hm), 0)
    c_i = lax.broadcasted_iota(jnp.int32, (hm, hm), 1)
    tri = (r_i > c_i).astype(BF16)
    before = carry_ref[...]
    for rows, lg in zip(halves, logits):
        lg = jnp.where(lane < N_EXPERTS, lg, -jnp.inf)
        vals, idxs = [], []
        for _ in range(TOP_K):
            mv = jnp.max(lg, axis=1, keepdims=True)
            ik = jnp.min(jnp.where(lg == mv, lane, LANES), axis=1, keepdims=True)
            vals.append(mv)
            idxs.append(ik)
            lg = jnp.where(lane == ik, -jnp.inf, lg)
        es = [jnp.exp(v - vals[0]) for v in vals]
        tot = es[0] + es[1] + es[2] + es[3]
        onehot = jnp.zeros((hm, LANES), F32)
        for ik in idxs:
            onehot = onehot + (lane == ik).astype(F32)
        pre = jnp.dot(tri, onehot.astype(BF16), preferred_element_type=F32) + before
        topi = jnp.zeros((hm, LANES), jnp.int32)
        topw = jnp.zeros((hm, LANES), F32)
        rank = jnp.zeros((hm, LANES), F32)
        for k in range(TOP_K):
            rk = jnp.sum(jnp.where(lane == idxs[k], pre, 0.0), axis=1, keepdims=True)
            topi = jnp.where(lane == k, idxs[k], topi)
            topw = jnp.where(lane == k, es[k] / tot, topw)
            rank = jnp.where(lane == k, rk, rank)
        topi_ref[rows, :] = topi
        topw_ref[rows, :] = topw
        rank_ref[rows, :] = rank.astype(jnp.int32)
        before = before + jnp.sum(onehot, axis=0, keepdims=True)
    carry_ref[...] = before
    cnt_ref[...] = before


def _merge(ya, dil_outs, x2, seq, w_b, b_gate, w_oa_b, w_ob_b, w_o_b, ln1_g, ln1_b, w_router, b_router):
    tokens = x2.shape[0]
    tm = MERGE_TM
    per_seq = seq // tm

    def group(a):
        dil = a.shape[1]
        return pl.BlockSpec((None, dil, tm // dil, SEG_W), lambda i: (i // per_seq, 0, i % per_seq, 0))

    gate_blk = w_b.shape[1] // (2 * D_MODEL) - 1
    wr = jnp.zeros((D_MODEL, LANES), F32).at[:, :N_EXPERTS].set(w_router)
    wr_hi = wr.astype(BF16)
    wr_lo = (wr - wr_hi.astype(F32)).astype(BF16)
    br = jnp.zeros((1, LANES), F32).at[0, :N_EXPERTS].set(b_router)
    row = lambda w: pl.BlockSpec((tm, w), lambda i: (i, 0))
    full = lambda a: pl.BlockSpec(a.shape, lambda i: (0,) * a.ndim)
    (o1, l1), (o2, l2), (o3, l3) = dil_outs
    bg = b_gate.reshape(1, -1)
    g1 = ln1_g.reshape(1, -1)
    b1 = ln1_b.reshape(1, -1)
    lane_out = lambda dt: jax.ShapeDtypeStruct((tokens, LANES), dt)
    return pl.pallas_call(
        _merge_kernel,
        grid=(tokens // tm,),
        in_specs=[row(SEG_W), group(o1), group(o2), group(o3), group(l1), group(l2), group(l3),
                  row(D_MODEL), pl.BlockSpec((D_MODEL, 2 * D_MODEL), lambda i: (0, gate_blk)),
                  full(bg), full(w_oa_b), full(w_ob_b), full(w_o_b), full(g1), full(b1),
                  full(wr_hi), full(wr_lo), full(br)],
        out_specs=[row(D_MODEL), row(D_MODEL // 2), row(LANES), row(LANES), row(LANES),
                   pl.BlockSpec((1, LANES), lambda i: (0, 0))],
        out_shape=[jax.ShapeDtypeStruct((tokens, D_MODEL), F32), jax.ShapeDtypeStruct((tokens, D_MODEL // 2), U32),
                   lane_out(jnp.int32), lane_out(F32), lane_out(jnp.int32), jax.ShapeDtypeStruct((1, LANES), F32)],
        scratch_shapes=[pltpu.VMEM((1, LANES), F32), pltpu.VMEM((6, SEG_W // LANES, tm, LANES), F32),
                        pltpu.VMEM((D_MODEL, 2 * D_MODEL), BF16)],
        compiler_params=_cparams(1),
        name="merge",
    )(ya, o1, o2, o3, l1, l2, l3, x2, w_b, bg, w_oa_b, w_ob_b, w_o_b, g1, b1, wr_hi, wr_lo, br)


def _sc_mesh():
    return plsc.VectorSubcoreMesh(core_axis_name="core", subcore_axis_name="subcore")


def _sc_scatter_rows(table, src_idx, dst_idx):
    n = src_idx.shape[0]
    d = table.shape[1]
    mesh = _sc_mesh()
    workers = mesh.num_cores * mesh.num_subcores
    per = n // (SC_CHUNK * workers)
    assert per * SC_CHUNK * workers == n

    @pl.kernel(out_type=jax.ShapeDtypeStruct((n, d), table.dtype), mesh=mesh,
               scratch_types=[pltpu.VMEM((1, SC_CHUNK), jnp.int32), pltpu.VMEM((1, SC_CHUNK), jnp.int32),
                              pltpu.VMEM((SC_CHUNK, d), table.dtype)])
    def copy(t_hbm, s_hbm, d_hbm, o_hbm, s_vm, d_vm, buf):
        wid = lax.axis_index("core") * mesh.num_subcores + lax.axis_index("subcore")

        @pl.loop(0, per)
        def _(j):
            blk = wid * per + j
            pltpu.sync_copy(s_hbm.at[pl.ds(blk, 1)], s_vm)
            pltpu.sync_copy(d_hbm.at[pl.ds(blk, 1)], d_vm)
            pltpu.sync_copy(t_hbm.at[s_vm.at[0]], buf)
            pltpu.sync_copy(buf, o_hbm.at[d_vm.at[0]])

    return copy(table, src_idx.reshape(-1, SC_CHUNK), dst_idx.reshape(-1, SC_CHUNK))


def _sc_gather_rows(table, idx):
    n = idx.shape[0]
    d = table.shape[1]
    mesh = _sc_mesh()
    workers = mesh.num_cores * mesh.num_subcores
    per = n // (SC_CHUNK * workers)
    assert per * SC_CHUNK * workers == n

    @pl.kernel(out_type=jax.ShapeDtypeStruct((n, d), table.dtype), mesh=mesh,
               scratch_types=[pltpu.VMEM((1, SC_CHUNK), jnp.int32), pltpu.VMEM((SC_CHUNK, d), table.dtype)])
    def gather(t_hbm, i_hbm, o_hbm, i_vm, buf):
        wid = lax.axis_index("core") * mesh.num_subcores + lax.axis_index("subcore")

        @pl.loop(0, per)
        def _(j):
            blk = wid * per + j
            pltpu.sync_copy(i_hbm.at[pl.ds(blk, 1)], i_vm)
            pltpu.sync_copy(t_hbm.at[i_vm.at[0]], buf)
            pltpu.sync_copy(buf, o_hbm.at[pl.ds(blk * SC_CHUNK, SC_CHUNK)])

    return gather(table, idx.reshape(-1, SC_CHUNK))


def _expert_kernel(be_ref, nx_ref, first_ref, slot_ref, nu_ref, x_ref, wgu_hbm, bgu0_ref, bgu1_ref, wd_hbm,
                   bd0_ref, bd1_ref, y_ref, wgu_stage, wd_stage, wgu_b, wd_b, sem):
    i = pl.program_id(0)
    hb = x_ref.shape[0] // 2
    blocks = (2 * i, 2 * i + 1)
    biases = ((bgu0_ref, bd0_ref), (bgu1_ref, bd1_ref))

    def weight_copies(e):
        return (pltpu.make_async_copy(wgu_hbm.at[e], wgu_stage, sem.at[0]),
                pltpu.make_async_copy(wd_hbm.at[e], wd_stage, sem.at[1]))

    @pl.when(i == 0)
    def _():
        for cp in weight_copies(be_ref[0]):
            cp.start()

    for b in blocks:
        @pl.when(first_ref[b] == 1)
        def _():
            for cp in weight_copies(be_ref[b]):
                cp.wait()
            wgu_b[slot_ref[b]] = wgu_stage[...].astype(BF16)
            wd_b[slot_ref[b]] = wd_stage[...].astype(BF16)

            @pl.when(nx_ref[b] >= 0)
            def _():
                for cp in weight_copies(nx_ref[b]):
                    cp.start()

    cols = [slice(c * FF_CHUNK, (c + 1) * FF_CHUNK) for c in range(D_FF // FF_CHUNK)]

    def gate_up(h):
        slot = slot_ref[blocks[h]]
        lo, hi = _unpack_bf16_pairs(x_ref[h * hb:(h + 1) * hb, :])
        xb = jnp.concatenate([lo, hi], axis=1).astype(BF16)
        bgu_ref = biases[h][0]
        pre = []
        for gs in cols:
            us = slice(D_FF + gs.start, D_FF + gs.stop)
            pre.append((jnp.dot(xb, wgu_b[slot, :, gs], preferred_element_type=F32) + bgu_ref[:, gs],
                        jnp.dot(xb, wgu_b[slot, :, us], preferred_element_type=F32) + bgu_ref[:, us]))
        return pre

    def down(h, pre):
        slot = slot_ref[blocks[h]]
        acc = jnp.zeros((hb, D_MODEL), F32)
        for gs, (gate, up) in zip(cols, pre):
            gate = jnp.minimum(gate, SWIGLU_LIMIT)
            up = jnp.clip(up, -SWIGLU_LIMIT, SWIGLU_LIMIT)
            act = (up + 1.0) * gate * jax.nn.sigmoid(SWIGLU_ALPHA * gate)
            acc = acc + jnp.dot(act.astype(BF16), wd_b[slot, gs, :], preferred_element_type=F32)
        y_ref[h * hb:(h + 1) * hb, :] = _pack_bf16_pairs(acc + biases[h][1][...])

    both = blocks[1] < nu_ref[0]

    @pl.when(both)
    def _():
        pre = [gate_up(0), gate_up(1)]
        down(0, pre[0])
        down(1, pre[1])

    @pl.when(jnp.logical_not(both) & (blocks[0] < nu_ref[0]))
    def _():
        down(0, gate_up(0))
        y_ref[hb:, :] = jnp.zeros((hb, y_ref.shape[1]), y_ref.dtype)

    @pl.when(blocks[0] >= nu_ref[0])
    def _():
        y_ref[...] = jnp.zeros_like(y_ref)


def _experts(xrows, block_e, next_e, first_blk, slot, n_used, w_gu, b_gu, w_down, b_down):
    n_rows = xrows.shape[0]
    bm = MOE_BM
    assert (n_rows // bm) % 2 == 0
    bias = lambda h: (lambda i, be, nx, fb, sl, nu: (be[2 * i + h], 0, 0))
    rows = lambda i, be, nx, fb, sl, nu: (i, 0)
    b_gu3 = b_gu.reshape(N_EXPERTS, 1, -1)
    b_down3 = b_down.reshape(N_EXPERTS, 1, -1)
    return pl.pallas_call(
        _expert_kernel,
        grid_spec=pltpu.PrefetchScalarGridSpec(
            num_scalar_prefetch=5,
            grid=(n_rows // (2 * bm),),
            in_specs=[
                pl.BlockSpec((2 * bm, D_MODEL // 2), rows),
                pl.BlockSpec(memory_space=pl.ANY),
                pl.BlockSpec((None, 1, 2 * D_FF), bias(0)),
                pl.BlockSpec((None, 1, 2 * D_FF), bias(1)),
                pl.BlockSpec(memory_space=pl.ANY),
                pl.BlockSpec((None, 1, D_MODEL), bias(0)),
                pl.BlockSpec((None, 1, D_MODEL), bias(1)),
            ],
            out_specs=pl.BlockSpec((2 * bm, D_MODEL // 2), rows),
            scratch_shapes=[pltpu.VMEM((D_MODEL, 2 * D_FF), F32), pltpu.VMEM((D_FF, D_MODEL), F32),
                            pltpu.VMEM((2, D_MODEL, 2 * D_FF), BF16), pltpu.VMEM((2, D_FF, D_MODEL), BF16),
                            pltpu.SemaphoreType.DMA((2,))],
        ),
        out_shape=jax.ShapeDtypeStruct((n_rows, D_MODEL // 2), U32),
        compiler_params=_cparams(1),
        name="experts",
    )(block_e, next_e, first_blk, slot, n_used, xrows, w_gu, b_gu3, b_gu3, w_down, b_down3, b_down3)


def _combine_kernel(y0_ref, y1_ref, y2_ref, y3_ref, w_ref, h_ref, g_ref, b_ref, o_ref):
    w = w_ref[...]
    half = D_MODEL // 2
    f_lo = jnp.zeros((h_ref.shape[0], half), F32)
    f_hi = jnp.zeros((h_ref.shape[0], half), F32)
    for k, y_ref in enumerate((y0_ref, y1_ref, y2_ref, y3_ref)):
        lo, hi = _unpack_bf16_pairs(y_ref[...])
        f_lo = f_lo + lo * w[:, k:k + 1]
        f_hi = f_hi + hi * w[:, k:k + 1]
    f = jnp.concatenate([f_lo, f_hi], axis=1)
    o_ref[...] = _layer_norm(DN_ALPHA * h_ref[...] + f, g_ref[...], b_ref[...])


def _combine(yplanes, topw, h, ln2_g, ln2_b):
    tokens = h.shape[0]
    tm = MOE_TM
    per_plane = tokens // tm
    plane = lambda k: pl.BlockSpec((tm, D_MODEL // 2), lambda i: (k * per_plane + i, 0))
    return pl.pallas_call(
        _combine_kernel,
        grid=(per_plane,),
        in_specs=[plane(0), plane(1), plane(2), plane(3),
                  pl.BlockSpec((tm, LANES), lambda i: (i, 0)),
                  pl.BlockSpec((tm, D_MODEL), lambda i: (i, 0)),
                  pl.BlockSpec((1, D_MODEL), lambda i: (0, 0)),
                  pl.BlockSpec((1, D_MODEL), lambda i: (0, 0))],
        out_specs=pl.BlockSpec((tm, D_MODEL), lambda i: (i, 0)),
        out_shape=jax.ShapeDtypeStruct((tokens, D_MODEL), F32),
        compiler_params=_cparams(1),
        name="combine",
    )(yplanes, yplanes, yplanes, yplanes, topw, h, ln2_g.reshape(1, -1), ln2_b.reshape(1, -1))


def _routing_tables(topi, rank, cnt, tokens):
    bm = MOE_BM
    i32 = jnp.int32
    experts = jnp.arange(N_EXPERTS, dtype=i32)
    counts = cnt[0, :N_EXPERTS].astype(i32)
    padded = (counts + bm - 1) // bm * bm
    pend = jnp.cumsum(padded)
    pstart = pend - padded
    sel = topi[:, :TOP_K, None] == experts[None, None, :]
    dest = rank[:, :TOP_K] + jnp.sum(jnp.where(sel, pstart[None, None, :], 0), axis=-1)
    n_pad = N_EXPERTS * bm
    n_rows = tokens * TOP_K + n_pad
    starts = jnp.arange(n_rows // bm, dtype=i32) * bm
    block_e = jnp.minimum(jnp.sum((pend[None, :] <= starts[:, None]).astype(i32), axis=1), N_EXPERTS - 1)
    n_used = (pend[-1] // bm).reshape(1)
    is_block_e = block_e[:, None] == experts[None, :]
    after = jnp.sum(jnp.where(is_block_e, pend[None, :], 0), axis=1) // bm
    e_after = jnp.sum(jnp.where(after[:, None] == jnp.arange(n_rows // bm, dtype=i32)[None, :], block_e[None, :], 0),
                      axis=1)
    next_e = jnp.where(after < n_used[0], e_after, -1)
    blk = jnp.arange(n_rows // bm, dtype=i32)
    prev_e = jnp.concatenate([jnp.full((1,), -1, i32), block_e[:-1]])
    first_blk = ((blk < n_used[0]) & (block_e != prev_e)).astype(i32)
    slot = (jnp.cumsum(first_blk) - 1) % 2
    pad_cnt = padded - counts
    pad_end = jnp.cumsum(pad_cnt)
    j = jnp.arange(n_pad, dtype=i32)
    owner = jnp.sum((pad_end[None, :] <= j[:, None]).astype(i32), axis=1)
    is_owner = owner[:, None] == experts[None, :]
    pick = lambda v: jnp.sum(jnp.where(is_owner, v[None, :], 0), axis=1)
    in_expert = pick(pstart + counts) + (j - pick(pad_end - pad_cnt))
    pad_rows = jnp.where(owner < N_EXPERTS, in_expert, pend[-1] + (j - pad_end[-1]))
    src_tok = jnp.concatenate([jnp.repeat(jnp.arange(tokens, dtype=i32), TOP_K), j % tokens])
    dst_row = jnp.concatenate([dest.reshape(-1), pad_rows]).astype(i32)
    tables = (block_e.astype(i32), next_e.astype(i32), first_blk, jnp.maximum(slot, 0).astype(i32), n_used.astype(i32))
    return dest.astype(i32), tables, src_tok, dst_row


def kernel(x, w_in, b_gate, lam_q1, lam_k1, lam_q2, lam_k2, subln_g, w_oa, w_ob, w_o, ln1_g, ln1_b,
           w_router, b_router, w_gu, b_gu, w_down, b_down, ln2_g, ln2_b):
    batch, seq, d = x.shape
    tokens = batch * seq
    h = x.reshape(tokens, d)
    for l in range(DEPTH):
        q_scale = HEAD_DIM ** -0.5 * math.log2(math.e)
        qk_a, vt_a = _project_qkv(h, w_in[l], 0, seq, "qkv_diff", q_scale=q_scale, v_feature_major=True)
        ya = _diff_attention(qk_a.reshape(batch, seq, 2 * SEG_W), vt_a, lam_q1[l], lam_k1[l], lam_q2[l],
                             lam_k2[l], subln_g[l]).reshape(tokens, -1)
        dil = []
        for g, (_, dilation) in enumerate(DIL_PAIRS):
            qkv_g = _project_qkv(h, w_in[l], g + 1, seq, f"qkv_dil{g}", q_scale=q_scale, dil=dilation)
            dil.append(_dilated_group(qkv_g, batch, seq, g))
        h1, h1p, topi, topw, rank, cnt = _merge(ya, dil, h, seq, w_in[l], b_gate[l], w_oa[l].astype(BF16),
                                                w_ob[l].astype(BF16), w_o[l].astype(BF16), ln1_g[l], ln1_b[l],
                                                w_router[l], b_router[l])
        dest, expert_tables, src_tok, dst_row = _routing_tables(topi, rank, cnt, tokens)
        xrows = _sc_scatter_rows(h1p, src_tok, dst_row)
        yrows = _experts(xrows, *expert_tables, w_gu[l], b_gu[l], w_down[l], b_down[l])
        yplanes = _sc_gather_rows(yrows, dest.T.reshape(-1))
        h = _combine(yplanes, topw, h1, ln2_g[l], ln2_b[l])
    return h.reshape(batch, seq, d)
```

```python
import functools
import math

import jax
import jax.numpy as jnp
from jax import lax
from jax.experimental import pallas as pl
from jax.experimental.pallas import tpu as pltpu
from jax.experimental.pallas import tpu_sc as plsc

F32 = jnp.float32
BF16 = jnp.bfloat16
U32 = jnp.uint32

D_MODEL = 1024
HEAD_DIM = 64
ROT_DIM = HEAD_DIM // 4
ROPE_THETA = 500000.0
QBLK = 128
DA_HEADS = 4
DIL_PAIRS = ((128, 1), (512, 4), (2048, 16))
SEG_W = 512
QKV_W = 3 * SEG_W
N_EXPERTS = 32
TOP_K = 4
D_FF = D_MODEL
SWIGLU_ALPHA = 1.702
SWIGLU_LIMIT = 7.0
DEPTH = 1
DN_ALPHA = (2 * DEPTH) ** 0.25
EPS = 1e-5
LAMBDA_INIT = 0.8 - 0.6 * math.exp(-0.3 * 0)

LANES = 128
QKV_TM = 1024
QKV_CHUNK = 256
STRIDE_STAGE = 4
ATT_TQ = 256
ONES_ROWS = 16
DIL_NQ = 8
FF_CHUNK = 512
MERGE_TM = 512
MOE_BM = 256
MOE_BLOCKS_PER_STEP = 4
MOE_TM = 512
SC_CHUNK = 128
VMEM_LIMIT = 52 * 1024 * 1024


def _cparams(n_axes):
    return pltpu.CompilerParams(dimension_semantics=("arbitrary",) * n_axes,
                                vmem_limit_bytes=VMEM_LIMIT)


def _qkv_kernel(x_ref, w_ref, cs_ref, o_ref, *rest, q_scale, dil):
    tm = x_ref.shape[0]
    n = tm // dil
    two_stage = dil > STRIDE_STAGE
    if two_stage:
        *maybe_vt_ref, wb_ref, slab_ref, slab2_ref = rest
    elif dil > 1:
        *maybe_vt_ref, wb_ref, slab_ref = rest
    else:
        *maybe_vt_ref, wb_ref = rest

    @pl.when(pl.program_id(0) == 0)
    def _():
        wb_ref[...] = w_ref[...].astype(BF16)

    if dil > 1:
        for c in range(D_MODEL // LANES):
            slab_ref[c] = x_ref[:, c * LANES:(c + 1) * LANES]
    if two_stage:
        part = tm // STRIDE_STAGE
        for c in range(D_MODEL // LANES):
            for q in range(STRIDE_STAGE):
                slab2_ref[c, q * part:(q + 1) * part, :] = slab_ref[c, pl.ds(q, part, stride=STRIDE_STAGE), :]

    def residue_rows(c, r, l0, count):
        if two_stage:
            first = (r % STRIDE_STAGE) * (tm // STRIDE_STAGE) + r // STRIDE_STAGE + l0 * (dil // STRIDE_STAGE)
            return slab2_ref[c, pl.ds(first, count, stride=dil // STRIDE_STAGE), :]
        return slab_ref[c, pl.ds(l0 * dil + r, count, stride=dil), :]

    def regrouped_rows(start):
        pieces = []
        for p in range(start, start + QKV_CHUNK, min(n, QKV_CHUNK)):
            pieces.append(jnp.concatenate(
                [residue_rows(c, p // n, p % n, min(n, QKV_CHUNK)) for c in range(D_MODEL // LANES)], axis=1))
        return jnp.concatenate(pieces, axis=0).astype(BF16)

    def store(rows, cols, val):
        if dil == 1:
            o_ref[rows, cols] = val
        elif n >= QKV_CHUNK:
            o_ref[rows.start // n, rows.start % n:rows.start % n + QKV_CHUNK, cols] = val
        else:
            per = QKV_CHUNK // n
            for q in range(per):
                o_ref[rows.start // n + q, :, cols] = val[q * n:(q + 1) * n]

    half = ROT_DIM // 2
    in_head = lax.broadcasted_iota(jnp.int32, (QKV_CHUNK, LANES), 1) % HEAD_DIM
    first, second = in_head < half, (in_head >= half) & (in_head < ROT_DIM)
    for rc in range(tm // QKV_CHUNK):
        rows = slice(rc * QKV_CHUNK, (rc + 1) * QKV_CHUNK)
        xb = regrouped_rows(rows.start) if dil > 1 else x_ref[rows, :].astype(BF16)
        cs = cs_ref[rows, :]
        c = jnp.where(first, cs, jnp.where(second, pltpu.roll(cs, half, 1), 1.0))
        s1 = jnp.where(second, cs, 0.0)
        s2 = jnp.where(first, -pltpu.roll(cs, LANES - half, 1), 0.0)
        for seg in range(3):
            acc = jnp.dot(xb, wb_ref[:, seg * SEG_W:(seg + 1) * SEG_W], preferred_element_type=F32)
            if seg == 2:
                if maybe_vt_ref:
                    maybe_vt_ref[0][:, rows] = acc.T.astype(BF16)
                else:
                    store(rows, slice(seg * SEG_W, (seg + 1) * SEG_W), acc.astype(BF16))
                continue
            for k in range(SEG_W // LANES):
                t = acc[:, k * LANES:(k + 1) * LANES]
                r = t * c + pltpu.roll(t, ROT_DIM // 2, 1) * s1 + pltpu.roll(t, LANES - ROT_DIM // 2, 1) * s2
                if seg == 0:
                    r = r * q_scale
                lo = seg * SEG_W + k * LANES
                store(rows, slice(lo, lo + LANES), r.astype(BF16))


def _rope_lane_table(seq, dil, tm):
    row = jnp.arange(seq, dtype=jnp.int32)
    n = tm // dil
    in_tile = row % tm
    pos = (row - in_tile + (in_tile % n) * dil + in_tile // n).astype(F32)
    inv_freq = ROPE_THETA ** (-jnp.arange(0, ROT_DIM, 2, dtype=F32) / ROT_DIM)
    ang = pos[:, None] * inv_freq[None, :]
    head = jnp.concatenate([jnp.cos(ang), jnp.sin(ang), jnp.zeros((seq, HEAD_DIM - ROT_DIM), F32)], axis=1)
    return jnp.tile(head, (1, LANES // HEAD_DIM))


def _project_qkv(x2, w, col_block, seq, name, q_scale, dil=1, v_feature_major=False):
    tokens = x2.shape[0]
    batch = tokens // seq
    tm = min(QKV_TM, seq)
    per_seq = seq // tm
    n = tm // dil
    out_w = 2 * SEG_W if v_feature_major else QKV_W
    scratch = [pltpu.VMEM((D_MODEL, QKV_W), BF16)]
    if dil == 1:
        out_specs = [pl.BlockSpec((tm, out_w), lambda i: (i, 0))]
        out_shape = [jax.ShapeDtypeStruct((tokens, out_w), BF16)]
    else:
        out_specs = [pl.BlockSpec((None, dil, n, out_w), lambda i: (i // per_seq, 0, i % per_seq, 0))]
        out_shape = [jax.ShapeDtypeStruct((batch, dil, seq // dil, out_w), BF16)]
        scratch += [pltpu.VMEM((D_MODEL // LANES, tm, LANES), F32)] * (2 if dil > STRIDE_STAGE else 1)
    if v_feature_major:
        out_specs.append(pl.BlockSpec((None, SEG_W, tm), lambda i: (i // per_seq, 0, i % per_seq)))
        out_shape.append(jax.ShapeDtypeStruct((batch, SEG_W, seq), BF16))
    outs = pl.pallas_call(
        functools.partial(_qkv_kernel, q_scale=q_scale, dil=dil),
        grid=(tokens // tm,),
        in_specs=[pl.BlockSpec((tm, D_MODEL), lambda i: (i, 0)),
                  pl.BlockSpec((D_MODEL, QKV_W), lambda i: (0, col_block)),
                  pl.BlockSpec((tm, LANES), lambda i: (i % per_seq, 0))],
        out_specs=out_specs,
        out_shape=out_shape,
        scratch_shapes=scratch,
        compiler_params=_cparams(1),
        name=name,
    )(x2, w, _rope_lane_table(seq, dil, tm))
    return outs if v_feature_major else outs[0]


def _nt_dot(a, b):
    return lax.dot_general(a, b, (((1,), (1,)), ((), ())), preferred_element_type=F32)


def _diff_kernel(q_ref, k_ref, vt_ref, lq1_ref, lk1_ref, lq2_ref, lk2_ref, g_ref, o_ref, s_ref, acc_ref, *, tq):
    i = pl.program_id(1)
    chains = 2 * DA_HEADS
    half = tq // 2
    lane = lax.broadcasted_iota(jnp.int32, (tq, LANES), 1)
    qs = []
    for h in range(DA_HEADS):
        q = q_ref[:, h * LANES:(h + 1) * LANES]
        zero = jnp.zeros_like(q)
        qs += [jnp.where(lane < HEAD_DIM, q, zero), jnp.where(lane >= HEAD_DIM, q, zero)]
    acc_ref[...] = jnp.zeros_like(acc_ref)

    def score_chain(j, buf, c):
        start = pl.multiple_of(j * tq, tq)
        kb = k_ref[pl.ds(start, tq), (c // 2) * LANES:(c // 2 + 1) * LANES]
        buf[c] = _nt_dot(kb, qs[c])

    def score_block(j, buf):
        for c in range(chains):
            score_chain(j, buf, c)

    def absorb(j, buf, state, masked, nxt=None):
        start = pl.multiple_of(j * tq, tq)
        out = []
        for c in range(chains):
            if nxt is not None:
                score_chain(j + 1, nxt, c)
            m, l = state[c]
            if masked:
                key = lax.broadcasted_iota(jnp.int32, (tq, tq), 0)
                qry = lax.broadcasted_iota(jnp.int32, (tq, tq), 1)
                buf[c] = jnp.where(key <= qry, buf[c], -jnp.inf)
            m_new = jnp.maximum(m, jnp.max(buf[c], axis=0, keepdims=True))
            a = jnp.exp2(m - m_new)
            p = [jnp.exp2((buf[c, u * half:(u + 1) * half, :] - m_new).astype(BF16)) for u in range(2)]
            vt = vt_ref[(c // 2) * LANES:(c // 2 + 1) * LANES, pl.ds(start, tq)]
            vt1 = jnp.concatenate([vt, jnp.ones((ONES_ROWS, tq), BF16)], axis=0)
            pv = jnp.dot(vt1, jnp.concatenate(p, axis=0), preferred_element_type=F32)
            l = a * l + pv[LANES:LANES + 1, :]
            acc_ref[c] = a * acc_ref[c] + pv[:LANES, :]
            out.append((m_new, l))
        return tuple(out)

    buf_a, buf_b = s_ref.at[0], s_ref.at[1]

    def two_blocks(jj, state):
        j = 2 * jj
        state = absorb(j, buf_a, state, False, nxt=buf_b)
        return absorb(j + 1, buf_b, state, False, nxt=buf_a)

    def last_from_a(state):
        return absorb(i, buf_a, state, True)

    def last_from_b(state):
        state = absorb(i - 1, buf_a, state, False, nxt=buf_b)
        return absorb(i, buf_b, state, True)

    init = tuple((jnp.full((1, tq), -jnp.inf, F32), jnp.zeros((1, tq), F32)) for _ in range(chains))
    score_block(0, buf_a)
    state = lax.fori_loop(0, i // 2, two_blocks, init)
    state = lax.cond(lax.rem(i, 2) == 0, last_from_a, last_from_b, state)

    lam = (jnp.exp(jnp.sum(lq1_ref[...] * lk1_ref[...], axis=1, keepdims=True))
           - jnp.exp(jnp.sum(lq2_ref[...] * lk2_ref[...], axis=1, keepdims=True)) + LAMBDA_INIT)
    for h in range(DA_HEADS):
        l1, l2 = state[2 * h][1], state[2 * h + 1][1]
        o = (acc_ref[2 * h] / l1 - lam * (acc_ref[2 * h + 1] / l2)).T
        ms = jnp.mean(o * o, axis=1, keepdims=True)
        o = o * lax.rsqrt(ms + EPS) * g_ref[...]
        o_ref[:, h * LANES:(h + 1) * LANES] = (o * (1.0 - LAMBDA_INIT)).astype(BF16)


def _diff_attention(qk3, vt3, lam_q1, lam_k1, lam_q2, lam_k2, subln_g):
    b, s, _ = qk3.shape
    tq = min(ATT_TQ, s)
    vec = lambda n: pl.BlockSpec((1, n), lambda bb, i: (0, 0))
    return pl.pallas_call(
        functools.partial(_diff_kernel, tq=tq),
        grid=(b, s // tq),
        in_specs=[
            pl.BlockSpec((None, tq, SEG_W), lambda bb, i: (bb, i, 0)),
            pl.BlockSpec((None, s, SEG_W), lambda bb, i: (bb, 0, 1)),
            pl.BlockSpec((None, SEG_W, s), lambda bb, i: (bb, 0, 0)),
            vec(HEAD_DIM), vec(HEAD_DIM), vec(HEAD_DIM), vec(HEAD_DIM), vec(2 * HEAD_DIM),
        ],
        out_specs=pl.BlockSpec((None, tq, SEG_W), lambda bb, i: (bb, i, 0)),
        out_shape=jax.ShapeDtypeStruct((b, s, SEG_W), BF16),
        scratch_shapes=[pltpu.VMEM((2, 2 * DA_HEADS, tq, tq), F32), pltpu.VMEM((2 * DA_HEADS, LANES, tq), F32)],
        compiler_params=_cparams(2),
        name="diff_attn",
    )(qk3, qk3, vt3, lam_q1.reshape(1, -1), lam_k1.reshape(1, -1), lam_q2.reshape(1, -1),
      lam_k2.reshape(1, -1), subln_g.reshape(1, -1))


def _dil_kernel(q_ref, kp_ref, kc_ref, vp_ref, vc_ref, o_ref, lse_ref, *, rel, nq, nres):
    n = pl.program_id(2)
    qi = lax.broadcasted_iota(jnp.int32, (QBLK, 2 * QBLK), 0)
    kj = lax.broadcasted_iota(jnp.int32, (QBLK, 2 * QBLK), 1)
    dist = qi + QBLK - kj
    band = (dist >= 0) & (dist <= rel)
    lane = lax.broadcasted_iota(jnp.int32, (QBLK, LANES), 1)
    low = lane < HEAD_DIM
    slabs = [slice(p * LANES, (p + 1) * LANES) for p in range(SEG_W // LANES)]

    def window(prev_ref, cur_ref, s, sl):
        if s == 0:
            return jnp.concatenate([prev_ref[:, sl], cur_ref[:QBLK, sl]], axis=0)
        return cur_ref[(s - 1) * QBLK:(s + 1) * QBLK, sl]

    def score_block(blk):
        res, s = blk
        rows = slice(s * QBLK, (s + 1) * QBLK)
        scores = []
        for sl in slabs:
            q2 = q_ref[res, rows, sl]
            k2 = window(kp_ref.at[res], kc_ref.at[res], s, sl)
            for sel in (low, jnp.logical_not(low)):
                scores.append(_nt_dot(jnp.where(sel, q2, jnp.zeros_like(q2)), k2))
        return scores

    def finish_block(blk, scores):
        res, s = blk
        rows = slice(s * QBLK, (s + 1) * QBLK)
        valid = band & ((kj >= QBLK) | (n * nq + s > 0))
        probs = []
        for sc in scores:
            sc = jnp.where(valid, sc, -jnp.inf)
            m = jnp.max(sc, axis=1, keepdims=True)
            probs.append((jnp.exp2((sc - m).astype(BF16)), m))
        one = jnp.ones((2 * QBLK, LANES), BF16)
        low_keys = lax.broadcasted_iota(jnp.int32, (2 * QBLK, LANES), 1) < HEAD_DIM
        for p, sl in enumerate(slabs):
            v2 = window(vp_ref.at[res], vc_ref.at[res], s, sl)
            (pe_lo, m_lo), (pe_hi, m_hi) = probs[2 * p:2 * p + 2]
            pv_lo = jnp.dot(pe_lo, jnp.where(low_keys, v2, one), preferred_element_type=F32)
            pv_hi = jnp.dot(pe_hi, jnp.where(low_keys, one, v2), preferred_element_type=F32)
            num = jnp.where(low, pv_lo, pv_hi)
            den = pltpu.roll(jnp.where(low, pv_hi, pv_lo), HEAD_DIM, 1)
            o_ref[res, rows, sl] = (num / den).astype(BF16)
            lse_ref[res, rows, sl] = jnp.where(low, m_lo, m_hi) + jnp.log2(den)

    blocks = [(res, s) for res in range(nres) for s in range(nq)]
    pending = score_block(blocks[0])
    for prev, blk in zip(blocks, blocks[1:]):
        nxt = score_block(blk)
        finish_block(prev, pending)
        pending = nxt
    finish_block(blocks[-1], pending)


def _dilated_group(qkv, batch, seq, group):
    window, dil = DIL_PAIRS[group]
    rel = window // dil
    length = seq // dil
    nb = length // QBLK
    nq = math.gcd(nb, DIL_NQ)
    nres = math.gcd(dil, DIL_NQ // nq)
    view = qkv.reshape(batch, dil, length, QKV_W)

    def spec(seg, prev):
        if prev:
            return pl.BlockSpec((None, nres, QBLK, SEG_W), lambda b, r, n: (b, r, jnp.maximum(n * nq - 1, 0), seg))
        return pl.BlockSpec((None, nres, nq * QBLK, SEG_W), lambda b, r, n: (b, r, n, seg))

    out_spec = pl.BlockSpec((None, nres, nq * QBLK, SEG_W), lambda b, r, n: (b, r, n, 0))
    o, lse = pl.pallas_call(
        functools.partial(_dil_kernel, rel=rel, nq=nq, nres=nres),
        grid=(batch, dil // nres, nb // nq),
        in_specs=[spec(0, False), spec(1, True), spec(1, False), spec(2, True), spec(2, False)],
        out_specs=[out_spec, out_spec],
        out_shape=[jax.ShapeDtypeStruct((batch, dil, length, SEG_W), BF16),
                   jax.ShapeDtypeStruct((batch, dil, length, SEG_W), F32)],
        compiler_params=_cparams(3),
        name=f"dilated{group}",
    )(view, view, view, view, view)
    return o, lse


def _layer_norm(z, g, b):
    mu = jnp.mean(z, axis=1, keepdims=True)
    zc = z - mu
    var = jnp.mean(zc * zc, axis=1, keepdims=True)
    return zc * lax.rsqrt(var + EPS) * g + b


def _pack_bf16_pairs(v):
    n = v.shape[1] // 2
    lo = pltpu.bitcast(v[:, :n].astype(BF16).astype(F32), U32)
    hi = pltpu.bitcast(v[:, n:].astype(BF16).astype(F32), U32)
    return jnp.bitwise_or(jnp.right_shift(lo, jnp.uint32(16)), jnp.bitwise_and(hi, jnp.uint32(0xFFFF0000)))


def _unpack_bf16_pairs(w):
    lo = pltpu.bitcast(jnp.left_shift(w, jnp.uint32(16)), F32)
    hi = pltpu.bitcast(jnp.bitwise_and(w, jnp.uint32(0xFFFF0000)), F32)
    return lo, hi


def _merge_kernel(ya_ref, o1_ref, o2_ref, o3_ref, l1_ref, l2_ref, l3_ref, x_ref,
                  wg_ref, bg_ref, woa_ref, wob_ref, wo_ref, g_ref, b_ref, wrh_ref, wrl_ref, br_ref,
                  h_ref, hp_ref, topi_ref, topw_ref, rank_ref, cnt_ref, carry_ref, scr_ref, wgb_ref):
    i = pl.program_id(0)
    tm = x_ref.shape[0]

    @pl.when(i == 0)
    def _():
        carry_ref[...] = jnp.zeros_like(carry_ref)
        wgb_ref[...] = wg_ref[...].astype(BF16)

    hm = tm // 2
    halves = (slice(0, hm), slice(hm, tm))
    xs = [x_ref[rows, :] for rows in halves]
    xbs = [x.astype(BF16) for x in xs]
    gate_dot = lambda xb: jnp.dot(xb, wgb_ref[...], preferred_element_type=F32) + bg_ref[...]
    gate_pre = [gate_dot(xbs[0])]
    pas = [jnp.dot(ya_ref[rows, :], woa_ref[...], preferred_element_type=F32) for rows in halves]

    def token_major(ref, scr):
        dil, n = ref.shape[0], ref.shape[1]
        if dil == 1:
            return ref[0].astype(F32)
        for r in range(dil):
            blk = ref[r].astype(F32)
            for c in range(SEG_W // LANES):
                scr[c, pl.ds(r, n, stride=dil), :] = blk[:, c * LANES:(c + 1) * LANES]
        return jnp.concatenate([scr[c] for c in range(SEG_W // LANES)], axis=1)

    oa, ob, oc = (token_major(r, scr_ref.at[k]) for k, r in enumerate((o1_ref, o2_ref, o3_ref)))
    la, lb, lc = (token_major(r, scr_ref.at[3 + k]) for k, r in enumerate((l1_ref, l2_ref, l3_ref)))
    mx = jnp.maximum(jnp.maximum(la, lb), lc)
    ea, eb, ec = jnp.exp2(la - mx), jnp.exp2(lb - mx), jnp.exp2(lc - mx)
    yb = ((ea * oa + eb * ob + ec * oc) / (ea + eb + ec)).astype(BF16)
    pbs = [jnp.dot(yb[rows, :], wob_ref[...], preferred_element_type=F32) for rows in halves]
    gate_pre.append(gate_dot(xbs[1]))
    mixes = []
    for pre_act, pa, pb in zip(gate_pre, pas, pbs):
        gates = jax.nn.sigmoid(pre_act)
        merged = gates[:, :D_MODEL] * pa + gates[:, D_MODEL:] * pb
        mixes.append(jnp.dot(merged.astype(BF16), wo_ref[...], preferred_element_type=F32))
    logits = []
    for rows, x, mix in zip(halves, xs, mixes):
        h = _layer_norm(DN_ALPHA * x + mix, g_ref[...], b_ref[...])
        h_ref[rows, :] = h
        hp_ref[rows, :] = _pack_bf16_pairs(h)
        hh = h.astype(BF16)
        hl = (h - hh.astype(F32)).astype(BF16)
        logits.append(jnp.dot(hh, wrh_ref[...], preferred_element_type=F32)
                      + jnp.dot(hl, wrh_ref[...], preferred_element_type=F32)
                      + jnp.dot(hh, wrl_ref[...], preferred_element_type=F32) + br_ref[...])

    lane = lax.broadcasted_iota(jnp.int32, (hm, LANES), 1)
    r_i = lax.broadcasted_iota(jnp.int32, (hm, hm), 0)
    c_i = lax.broadcasted_iota(jnp.int32, (hm, hm), 1)
    tri = (r_i > c_i).astype(BF16)
    before = carry_ref[...]
    for rows, lg in zip(halves, logits):
        lg = jnp.where(lane < N_EXPERTS, lg, -jnp.inf)
        vals, idxs = [], []
        for _ in range(TOP_K):
            mv = jnp.max(lg, axis=1, keepdims=True)
            ik = jnp.min(jnp.where(lg == mv, lane, LANES), axis=1, keepdims=True)
            vals.append(mv)
            idxs.append(ik)
            lg = jnp.where(lane == ik, -jnp.inf, lg)
        es = [jnp.exp(v - vals[0]) for v in vals]
        tot = es[0] + es[1] + es[2] + es[3]
        onehot = jnp.zeros((hm, LANES), F32)
        for ik in idxs:
            onehot = onehot + (lane == ik).astype(F32)
        pre = jnp.dot(tri, onehot.astype(BF16), preferred_element_type=F32) + before
        topi = jnp.zeros((hm, LANES), jnp.int32)
        topw = jnp.zeros((hm, LANES), F32)
        rank = jnp.zeros((hm, LANES), F32)
        for k in range(TOP_K):
            rk = jnp.sum(jnp.where(lane == idxs[k], pre, 0.0), axis=1, keepdims=True)
            topi = jnp.where(lane == k, idxs[k], topi)
            topw = jnp.where(lane == k, es[k] / tot, topw)
            rank = jnp.where(lane == k, rk, rank)
        topi_ref[rows, :] = topi
        topw_ref[rows, :] = topw
        rank_ref[rows, :] = rank.astype(jnp.int32)
        before = before + jnp.sum(onehot, axis=0, keepdims=True)
    carry_ref[...] = before
    cnt_ref[...] = before


def _merge(ya, dil_outs, x2, seq, w_b, b_gate, w_oa_b, w_ob_b, w_o_b, ln1_g, ln1_b, w_router, b_router):
    tokens = x2.shape[0]
    tm = MERGE_TM
    per_seq = seq // tm

    def group(a):
        dil = a.shape[1]
        return pl.BlockSpec((None, dil, tm // dil, SEG_W), lambda i: (i // per_seq, 0, i % per_seq, 0))

    gate_blk = w_b.shape[1] // (2 * D_MODEL) - 1
    wr = jnp.zeros((D_MODEL, LANES), F32).at[:, :N_EXPERTS].set(w_router)
    wr_hi = wr.astype(BF16)
    wr_lo = (wr - wr_hi.astype(F32)).astype(BF16)
    br = jnp.zeros((1, LANES), F32).at[0, :N_EXPERTS].set(b_router)
    row = lambda w: pl.BlockSpec((tm, w), lambda i: (i, 0))
    full = lambda a: pl.BlockSpec(a.shape, lambda i: (0,) * a.ndim)
    (o1, l1), (o2, l2), (o3, l3) = dil_outs
    bg = b_gate.reshape(1, -1)
    g1 = ln1_g.reshape(1, -1)
    b1 = ln1_b.reshape(1, -1)
    lane_out = lambda dt: jax.ShapeDtypeStruct((tokens, LANES), dt)
    return pl.pallas_call(
        _merge_kernel,
        grid=(tokens // tm,),
        in_specs=[row(SEG_W), group(o1), group(o2), group(o3), group(l1), group(l2), group(l3),
                  row(D_MODEL), pl.BlockSpec((D_MODEL, 2 * D_MODEL), lambda i: (0, gate_blk)),
                  full(bg), full(w_oa_b), full(w_ob_b), full(w_o_b), full(g1), full(b1),
                  full(wr_hi), full(wr_lo), full(br)],
        out_specs=[row(D_MODEL), row(D_MODEL // 2), row(LANES), row(LANES), row(LANES),
                   pl.BlockSpec((1, LANES), lambda i: (0, 0))],
        out_shape=[jax.ShapeDtypeStruct((tokens, D_MODEL), F32), jax.ShapeDtypeStruct((tokens, D_MODEL // 2), U32),
                   lane_out(jnp.int32), lane_out(F32), lane_out(jnp.int32), jax.ShapeDtypeStruct((1, LANES), F32)],
        scratch_shapes=[pltpu.VMEM((1, LANES), F32), pltpu.VMEM((6, SEG_W // LANES, tm, LANES), F32),
                        pltpu.VMEM((D_MODEL, 2 * D_MODEL), BF16)],
        compiler_params=_cparams(1),
        name="merge",
    )(ya, o1, o2, o3, l1, l2, l3, x2, w_b, bg, w_oa_b, w_ob_b, w_o_b, g1, b1, wr_hi, wr_lo, br)


def _sc_mesh():
    return plsc.VectorSubcoreMesh(core_axis_name="core", subcore_axis_name="subcore")


def _sc_scatter_rows(table, src_idx, dst_idx):
    n = src_idx.shape[0]
    d = table.shape[1]
    mesh = _sc_mesh()
    workers = mesh.num_cores * mesh.num_subcores
    per = n // (SC_CHUNK * workers)
    assert per * SC_CHUNK * workers == n

    @pl.kernel(out_type=jax.ShapeDtypeStruct((n, d), table.dtype), mesh=mesh,
               scratch_types=[pltpu.VMEM((1, SC_CHUNK), jnp.int32), pltpu.VMEM((1, SC_CHUNK), jnp.int32),
                              pltpu.VMEM((SC_CHUNK, d), table.dtype)])
    def copy(t_hbm, s_hbm, d_hbm, o_hbm, s_vm, d_vm, buf):
        wid = lax.axis_index("core") * mesh.num_subcores + lax.axis_index("subcore")

        @pl.loop(0, per)
        def _(j):
            blk = wid * per + j
            pltpu.sync_copy(s_hbm.at[pl.ds(blk, 1)], s_vm)
            pltpu.sync_copy(d_hbm.at[pl.ds(blk, 1)], d_vm)
            pltpu.sync_copy(t_hbm.at[s_vm.at[0]], buf)
            pltpu.sync_copy(buf, o_hbm.at[d_vm.at[0]])

    return copy(table, src_idx.reshape(-1, SC_CHUNK), dst_idx.reshape(-1, SC_CHUNK))


def _sc_gather_rows(table, idx):
    n = idx.shape[0]
    d = table.shape[1]
    mesh = _sc_mesh()
    workers = mesh.num_cores * mesh.num_subcores
    per = n // (SC_CHUNK * workers)
    assert per * SC_CHUNK * workers == n

    @pl.kernel(out_type=jax.ShapeDtypeStruct((n, d), table.dtype), mesh=mesh,
               scratch_types=[pltpu.VMEM((1, SC_CHUNK), jnp.int32), pltpu.VMEM((SC_CHUNK, d), table.dtype)])
    def gather(t_hbm, i_hbm, o_hbm, i_vm, buf):
        wid = lax.axis_index("core") * mesh.num_subcores + lax.axis_index("subcore")

        @pl.loop(0, per)
        def _(j):
            blk = wid * per + j
            pltpu.sync_copy(i_hbm.at[pl.ds(blk, 1)], i_vm)
            pltpu.sync_copy(t_hbm.at[i_vm.at[0]], buf)
            pltpu.sync_copy(buf, o_hbm.at[pl.ds(blk * SC_CHUNK, SC_CHUNK)])

    return gather(table, idx.reshape(-1, SC_CHUNK))


def _expert_kernel(be_ref, nx_ref, first_ref, slot_ref, nu_ref, x_ref, wgu_hbm, wd_hbm, *rest):
    nparts = MOE_BLOCKS_PER_STEP
    bias_refs, (y_ref, wgu_stage, wd_stage, wgu_b, wd_b, sem) = rest[:2 * nparts], rest[2 * nparts:]
    biases = [(bias_refs[2 * h], bias_refs[2 * h + 1]) for h in range(nparts)]
    i = pl.program_id(0)
    hb = x_ref.shape[0] // nparts
    blocks = [nparts * i + h for h in range(nparts)]

    def weight_copies(e):
        return (pltpu.make_async_copy(wgu_hbm.at[e], wgu_stage, sem.at[0]),
                pltpu.make_async_copy(wd_hbm.at[e], wd_stage, sem.at[1]))

    @pl.when(i == 0)
    def _():
        for cp in weight_copies(be_ref[0]):
            cp.start()

    for b in blocks:
        @pl.when(first_ref[b] == 1)
        def _():
            for cp in weight_copies(be_ref[b]):
                cp.wait()
            wgu_b[slot_ref[b]] = wgu_stage[...].astype(BF16)
            wd_b[slot_ref[b]] = wd_stage[...].astype(BF16)

            @pl.when(nx_ref[b] >= 0)
            def _():
                for cp in weight_copies(nx_ref[b]):
                    cp.start()

    cols = [slice(c * FF_CHUNK, (c + 1) * FF_CHUNK) for c in range(D_FF // FF_CHUNK)]

    def gate_up(h):
        slot = slot_ref[blocks[h]]
        lo, hi = _unpack_bf16_pairs(x_ref[h * hb:(h + 1) * hb, :])
        xb = jnp.concatenate([lo, hi], axis=1).astype(BF16)
        bgu_ref = biases[h][0]
        pre = []
        for gs in cols:
            us = slice(D_FF + gs.start, D_FF + gs.stop)
            pre.append((jnp.dot(xb, wgu_b[slot, :, gs], preferred_element_type=F32) + bgu_ref[:, gs],
                        jnp.dot(xb, wgu_b[slot, :, us], preferred_element_type=F32) + bgu_ref[:, us]))
        return pre

    def down(h, pre):
        slot = slot_ref[blocks[h]]
        acc = jnp.zeros((hb, D_MODEL), F32)
        for gs, (gate, up) in zip(cols, pre):
            gate = jnp.minimum(gate, SWIGLU_LIMIT)
            up = jnp.clip(up, -SWIGLU_LIMIT, SWIGLU_LIMIT)
            act = (up + 1.0) * gate * jax.nn.sigmoid(SWIGLU_ALPHA * gate)
            acc = acc + jnp.dot(act.astype(BF16), wd_b[slot, gs, :], preferred_element_type=F32)
        y_ref[h * hb:(h + 1) * hb, :] = _pack_bf16_pairs(acc + biases[h][1][...])

    all_used = blocks[-1] < nu_ref[0]

    @pl.when(all_used)
    def _():
        pre = gate_up(0)
        for h in range(1, nparts):
            nxt = gate_up(h)
            down(h - 1, pre)
            pre = nxt
        down(nparts - 1, pre)

    for h in range(nparts):
        @pl.when(jnp.logical_not(all_used) & (blocks[h] < nu_ref[0]))
        def _():
            down(h, gate_up(h))

        @pl.when(blocks[h] >= nu_ref[0])
        def _():
            y_ref[h * hb:(h + 1) * hb, :] = jnp.zeros((hb, y_ref.shape[1]), y_ref.dtype)


def _experts(xrows, block_e, next_e, first_blk, slot, n_used, w_gu, b_gu, w_down, b_down):
    n_rows = xrows.shape[0]
    bm = MOE_BM
    nparts = MOE_BLOCKS_PER_STEP
    assert (n_rows // bm) % nparts == 0
    bias = lambda h: (lambda i, be, nx, fb, sl, nu: (be[nparts * i + h], 0, 0))
    rows = lambda i, be, nx, fb, sl, nu: (i, 0)
    b_gu3 = b_gu.reshape(N_EXPERTS, 1, -1)
    b_down3 = b_down.reshape(N_EXPERTS, 1, -1)
    bias_specs, bias_args = [], []
    for h in range(nparts):
        bias_specs += [pl.BlockSpec((None, 1, 2 * D_FF), bias(h)), pl.BlockSpec((None, 1, D_MODEL), bias(h))]
        bias_args += [b_gu3, b_down3]
    return pl.pallas_call(
        _expert_kernel,
        grid_spec=pltpu.PrefetchScalarGridSpec(
            num_scalar_prefetch=5,
            grid=(n_rows // (nparts * bm),),
            in_specs=[pl.BlockSpec((nparts * bm, D_MODEL // 2), rows),
                      pl.BlockSpec(memory_space=pl.ANY), pl.BlockSpec(memory_space=pl.ANY)] + bias_specs,
            out_specs=pl.BlockSpec((nparts * bm, D_MODEL // 2), rows),
            scratch_shapes=[pltpu.VMEM((D_MODEL, 2 * D_FF), F32), pltpu.VMEM((D_FF, D_MODEL), F32),
                            pltpu.VMEM((nparts, D_MODEL, 2 * D_FF), BF16), pltpu.VMEM((nparts, D_FF, D_MODEL), BF16),
                            pltpu.SemaphoreType.DMA((2,))],
        ),
        out_shape=jax.ShapeDtypeStruct((n_rows, D_MODEL // 2), U32),
        compiler_params=_cparams(1),
        name="experts",
    )(block_e, next_e, first_blk, slot, n_used, xrows, w_gu, w_down, *bias_args)


def _combine_kernel(y0_ref, y1_ref, y2_ref, y3_ref, w_ref, h_ref, g_ref, b_ref, o_ref):
    w = w_ref[...]
    half = D_MODEL // 2
    f_lo = jnp.zeros((h_ref.shape[0], half), F32)
    f_hi = jnp.zeros((h_ref.shape[0], half), F32)
    for k, y_ref in enumerate((y0_ref, y1_ref, y2_ref, y3_ref)):
        lo, hi = _unpack_bf16_pairs(y_ref[...])
        f_lo = f_lo + lo * w[:, k:k + 1]
        f_hi = f_hi + hi * w[:, k:k + 1]
    f = jnp.concatenate([f_lo, f_hi], axis=1)
    o_ref[...] = _layer_norm(DN_ALPHA * h_ref[...] + f, g_ref[...], b_ref[...])


def _combine(yplanes, topw, h, ln2_g, ln2_b):
    tokens = h.shape[0]
    tm = MOE_TM
    per_plane = tokens // tm
    plane = lambda k: pl.BlockSpec((tm, D_MODEL // 2), lambda i: (k * per_plane + i, 0))
    return pl.pallas_call(
        _combine_kernel,
        grid=(per_plane,),
        in_specs=[plane(0), plane(1), plane(2), plane(3),
                  pl.BlockSpec((tm, LANES), lambda i: (i, 0)),
                  pl.BlockSpec((tm, D_MODEL), lambda i: (i, 0)),
                  pl.BlockSpec((1, D_MODEL), lambda i: (0, 0)),
                  pl.BlockSpec((1, D_MODEL), lambda i: (0, 0))],
        out_specs=pl.BlockSpec((tm, D_MODEL), lambda i: (i, 0)),
        out_shape=jax.ShapeDtypeStruct((tokens, D_MODEL), F32),
        compiler_params=_cparams(1),
        name="combine",
    )(yplanes, yplanes, yplanes, yplanes, topw, h, ln2_g.reshape(1, -1), ln2_b.reshape(1, -1))


def _routing_tables(topi, rank, cnt, tokens):
    bm = MOE_BM
    i32 = jnp.int32
    experts = jnp.arange(N_EXPERTS, dtype=i32)
    counts = cnt[0, :N_EXPERTS].astype(i32)
    padded = (counts + bm - 1) // bm * bm
    pend = jnp.cumsum(padded)
    pstart = pend - padded
    sel = topi[:, :TOP_K, None] == experts[None, None, :]
    dest = rank[:, :TOP_K] + jnp.sum(jnp.where(sel, pstart[None, None, :], 0), axis=-1)
    n_pad = N_EXPERTS * bm
    n_rows = tokens * TOP_K + n_pad
    starts = jnp.arange(n_rows // bm, dtype=i32) * bm
    block_e = jnp.minimum(jnp.sum((pend[None, :] <= starts[:, None]).astype(i32), axis=1), N_EXPERTS - 1)
    n_used = (pend[-1] // bm).reshape(1)
    is_block_e = block_e[:, None] == experts[None, :]
    after = jnp.sum(jnp.where(is_block_e, pend[None, :], 0), axis=1) // bm
    e_after = jnp.sum(jnp.where(after[:, None] == jnp.arange(n_rows // bm, dtype=i32)[None, :], block_e[None, :], 0),
                      axis=1)
    next_e = jnp.where(after < n_used[0], e_after, -1)
    blk = jnp.arange(n_rows // bm, dtype=i32)
    prev_e = jnp.concatenate([jnp.full((1,), -1, i32), block_e[:-1]])
    first_blk = ((blk < n_used[0]) & (block_e != prev_e)).astype(i32)
    slot = (jnp.cumsum(first_blk) - 1) % MOE_BLOCKS_PER_STEP
    pad_cnt = padded - counts
    pad_end = jnp.cumsum(pad_cnt)
    j = jnp.arange(n_pad, dtype=i32)
    owner = jnp.sum((pad_end[None, :] <= j[:, None]).astype(i32), axis=1)
    is_owner = owner[:, None] == experts[None, :]
    pick = lambda v: jnp.sum(jnp.where(is_owner, v[None, :], 0), axis=1)
    in_expert = pick(pstart + counts) + (j - pick(pad_end - pad_cnt))
    pad_rows = jnp.where(owner < N_EXPERTS, in_expert, pend[-1] + (j - pad_end[-1]))
    src_tok = jnp.concatenate([jnp.repeat(jnp.arange(tokens, dtype=i32), TOP_K), j % tokens])
    dst_row = jnp.concatenate([dest.reshape(-1), pad_rows]).astype(i32)
    tables = (block_e.astype(i32), next_e.astype(i32), first_blk, jnp.maximum(slot, 0).astype(i32), n_used.astype(i32))
    return dest.astype(i32), tables, src_tok, dst_row


def kernel(x, w_in, b_gate, lam_q1, lam_k1, lam_q2, lam_k2, subln_g, w_oa, w_ob, w_o, ln1_g, ln1_b,
           w_router, b_router, w_gu, b_gu, w_down, b_down, ln2_g, ln2_b):
    batch, seq, d = x.shape
    tokens = batch * seq
    h = x.reshape(tokens, d)
    for l in range(DEPTH):
        q_scale = HEAD_DIM ** -0.5 * math.log2(math.e)
        qk_a, vt_a = _project_qkv(h, w_in[l], 0, seq, "qkv_diff", q_scale=q_scale, v_feature_major=True)
        ya = _diff_attention(qk_a.reshape(batch, seq, 2 * SEG_W), vt_a, lam_q1[l], lam_k1[l], lam_q2[l],
                             lam_k2[l], subln_g[l]).reshape(tokens, -1)
        dil = []
        for g, (_, dilation) in enumerate(DIL_PAIRS):
            qkv_g = _project_qkv(h, w_in[l], g + 1, seq, f"qkv_dil{g}", q_scale=q_scale, dil=dilation)
            dil.append(_dilated_group(qkv_g, batch, seq, g))
        h1, h1p, topi, topw, rank, cnt = _merge(ya, dil, h, seq, w_in[l], b_gate[l], w_oa[l].astype(BF16),
                                                w_ob[l].astype(BF16), w_o[l].astype(BF16), ln1_g[l], ln1_b[l],
                                                w_router[l], b_router[l])
        dest, expert_tables, src_tok, dst_row = _routing_tables(topi, rank, cnt, tokens)
        xrows = _sc_scatter_rows(h1p, src_tok, dst_row)
        yrows = _experts(xrows, *expert_tables, w_gu[l], b_gu[l], w_down[l], b_down[l])
        yplanes = _sc_gather_rows(yrows, dest.T.reshape(-1))
        h = _combine(yplanes, topw, h1, ln2_g[l], ln2_b[l])
    return h.reshape(batch, seq, d)
```

```python
import functools
import math

import jax
import jax.numpy as jnp
from jax import lax
from jax.experimental import pallas as pl
from jax.experimental.pallas import tpu as pltpu
from jax.experimental.pallas import tpu_sc as plsc

F32 = jnp.float32
BF16 = jnp.bfloat16
U32 = jnp.uint32

D_MODEL = 1024
HEAD_DIM = 64
ROT_DIM = HEAD_DIM // 4
ROPE_THETA = 500000.0
QBLK = 128
DA_HEADS = 4
DIL_PAIRS = ((128, 1), (512, 4), (2048, 16))
SEG_W = 512
QKV_W = 3 * SEG_W
N_EXPERTS = 32
TOP_K = 4
D_FF = D_MODEL
SWIGLU_ALPHA = 1.702
SWIGLU_LIMIT = 7.0
DEPTH = 1
DN_ALPHA = (2 * DEPTH) ** 0.25
EPS = 1e-5
LAMBDA_INIT = 0.8 - 0.6 * math.exp(-0.3 * 0)

LANES = 128
QKV_TM = 1024
QKV_CHUNK = 256
STRIDE_STAGE = 4
ATT_TQ = 256
ONES_ROWS = 16
DIL_NQ = 8
FF_CHUNK = 512
MERGE_TM = 512
MOE_BM = 256
MOE_BLOCKS_PER_STEP = 4
MOE_TM = 512
SC_CHUNK = 128
VMEM_LIMIT = 52 * 1024 * 1024


def _cparams(n_axes):
    return pltpu.CompilerParams(dimension_semantics=("arbitrary",) * n_axes,
                                vmem_limit_bytes=VMEM_LIMIT)


def _qkv_kernel(x_ref, w_ref, cs_ref, o_ref, *rest, q_scale, dil):
    tm = x_ref.shape[0]
    n = tm // dil
    two_stage = dil > STRIDE_STAGE
    if two_stage:
        *maybe_vt_ref, wb_ref, slab_ref, slab2_ref = rest
    elif dil > 1:
        *maybe_vt_ref, wb_ref, slab_ref = rest
    else:
        *maybe_vt_ref, wb_ref = rest

    @pl.when(pl.program_id(0) == 0)
    def _():
        wb_ref[...] = w_ref[...].astype(BF16)

    if dil > 1:
        for c in range(D_MODEL // LANES):
            slab_ref[c] = x_ref[:, c * LANES:(c + 1) * LANES]
    if two_stage:
        part = tm // STRIDE_STAGE
        for c in range(D_MODEL // LANES):
            for q in range(STRIDE_STAGE):
                slab2_ref[c, q * part:(q + 1) * part, :] = slab_ref[c, pl.ds(q, part, stride=STRIDE_STAGE), :]

    def residue_rows(c, r, l0, count):
        if two_stage:
            first = (r % STRIDE_STAGE) * (tm // STRIDE_STAGE) + r // STRIDE_STAGE + l0 * (dil // STRIDE_STAGE)
            return slab2_ref[c, pl.ds(first, count, stride=dil // STRIDE_STAGE), :]
        return slab_ref[c, pl.ds(l0 * dil + r, count, stride=dil), :]

    def regrouped_rows(start):
        pieces = []
        for p in range(start, start + QKV_CHUNK, min(n, QKV_CHUNK)):
            pieces.append(jnp.concatenate(
                [residue_rows(c, p // n, p % n, min(n, QKV_CHUNK)) for c in range(D_MODEL // LANES)], axis=1))
        return jnp.concatenate(pieces, axis=0).astype(BF16)

    def store(rows, cols, val):
        if dil == 1:
            o_ref[rows, cols] = val
        elif n >= QKV_CHUNK:
            o_ref[rows.start // n, rows.start % n:rows.start % n + QKV_CHUNK, cols] = val
        else:
            per = QKV_CHUNK // n
            for q in range(per):
                o_ref[rows.start // n + q, :, cols] = val[q * n:(q + 1) * n]

    half = ROT_DIM // 2
    in_head = lax.broadcasted_iota(jnp.int32, (QKV_CHUNK, LANES), 1) % HEAD_DIM
    first, second = in_head < half, (in_head >= half) & (in_head < ROT_DIM)
    for rc in range(tm // QKV_CHUNK):
        rows = slice(rc * QKV_CHUNK, (rc + 1) * QKV_CHUNK)
        xb = regrouped_rows(rows.start) if dil > 1 else x_ref[rows, :].astype(BF16)
        cs = cs_ref[rows, :]
        c = jnp.where(first, cs, jnp.where(second, pltpu.roll(cs, half, 1), 1.0))
        s1 = jnp.where(second, cs, 0.0)
        s2 = jnp.where(first, -pltpu.roll(cs, LANES - half, 1), 0.0)
        for seg in range(3):
            acc = jnp.dot(xb, wb_ref[:, seg * SEG_W:(seg + 1) * SEG_W], preferred_element_type=F32)
            if seg == 2:
                if maybe_vt_ref:
                    maybe_vt_ref[0][:, rows] = acc.T.astype(BF16)
                else:
                    store(rows, slice(seg * SEG_W, (seg + 1) * SEG_W), acc.astype(BF16))
                continue
            for k in range(SEG_W // LANES):
                t = acc[:, k * LANES:(k + 1) * LANES]
                r = t * c + pltpu.roll(t, ROT_DIM // 2, 1) * s1 + pltpu.roll(t, LANES - ROT_DIM // 2, 1) * s2
                if seg == 0:
                    r = r * q_scale
                lo = seg * SEG_W + k * LANES
                store(rows, slice(lo, lo + LANES), r.astype(BF16))


def _rope_lane_table(seq, dil, tm):
    row = jnp.arange(seq, dtype=jnp.int32)
    n = tm // dil
    in_tile = row % tm
    pos = (row - in_tile + (in_tile % n) * dil + in_tile // n).astype(F32)
    inv_freq = ROPE_THETA ** (-jnp.arange(0, ROT_DIM, 2, dtype=F32) / ROT_DIM)
    ang = pos[:, None] * inv_freq[None, :]
    head = jnp.concatenate([jnp.cos(ang), jnp.sin(ang), jnp.zeros((seq, HEAD_DIM - ROT_DIM), F32)], axis=1)
    return jnp.tile(head, (1, LANES // HEAD_DIM))


def _project_qkv(x2, w, col_block, seq, name, q_scale, dil=1, v_feature_major=False):
    tokens = x2.shape[0]
    batch = tokens // seq
    tm = min(QKV_TM, seq)
    per_seq = seq // tm
    n = tm // dil
    out_w = 2 * SEG_W if v_feature_major else QKV_W
    scratch = [pltpu.VMEM((D_MODEL, QKV_W), BF16)]
    if dil == 1:
        out_specs = [pl.BlockSpec((tm, out_w), lambda i: (i, 0))]
        out_shape = [jax.ShapeDtypeStruct((tokens, out_w), BF16)]
    else:
        out_specs = [pl.BlockSpec((None, dil, n, out_w), lambda i: (i // per_seq, 0, i % per_seq, 0))]
        out_shape = [jax.ShapeDtypeStruct((batch, dil, seq // dil, out_w), BF16)]
        scratch += [pltpu.VMEM((D_MODEL // LANES, tm, LANES), F32)] * (2 if dil > STRIDE_STAGE else 1)
    if v_feature_major:
        out_specs.append(pl.BlockSpec((None, SEG_W, tm), lambda i: (i // per_seq, 0, i % per_seq)))
        out_shape.append(jax.ShapeDtypeStruct((batch, SEG_W, seq), BF16))
    outs = pl.pallas_call(
        functools.partial(_qkv_kernel, q_scale=q_scale, dil=dil),
        grid=(tokens // tm,),
        in_specs=[pl.BlockSpec((tm, D_MODEL), lambda i: (i, 0)),
                  pl.BlockSpec((D_MODEL, QKV_W), lambda i: (0, col_block)),
                  pl.BlockSpec((tm, LANES), lambda i: (i % per_seq, 0))],
        out_specs=out_specs,
        out_shape=out_shape,
        scratch_shapes=scratch,
        compiler_params=_cparams(1),
        name=name,
    )(x2, w, _rope_lane_table(seq, dil, tm))
    return outs if v_feature_major else outs[0]


def _nt_dot(a, b):
    return lax.dot_general(a, b, (((1,), (1,)), ((), ())), preferred_element_type=F32)


def _diff_kernel(q_ref, k_ref, vt_ref, lq1_ref, lk1_ref, lq2_ref, lk2_ref, g_ref, o_ref, s_ref, acc_ref, *, tq):
    i = pl.program_id(1)
    chains = 2 * DA_HEADS
    half = tq // 2
    lane = lax.broadcasted_iota(jnp.int32, (tq, LANES), 1)
    qs = []
    for h in range(DA_HEADS):
        q = q_ref[:, h * LANES:(h + 1) * LANES]
        zero = jnp.zeros_like(q)
        qs += [jnp.where(lane < HEAD_DIM, q, zero), jnp.where(lane >= HEAD_DIM, q, zero)]
    acc_ref[...] = jnp.zeros_like(acc_ref)

    def score_chain(j, buf, c):
        start = pl.multiple_of(j * tq, tq)
        kb = k_ref[pl.ds(start, tq), (c // 2) * LANES:(c // 2 + 1) * LANES]
        buf[c] = _nt_dot(kb, qs[c])

    def score_block(j, buf):
        for c in range(chains):
            score_chain(j, buf, c)

    def absorb(j, buf, state, masked, nxt=None):
        start = pl.multiple_of(j * tq, tq)
        out = []
        for c in range(chains):
            if nxt is not None:
                score_chain(j + 1, nxt, c)
            m, l = state[c]
            if masked:
                key = lax.broadcasted_iota(jnp.int32, (tq, tq), 0)
                qry = lax.broadcasted_iota(jnp.int32, (tq, tq), 1)
                buf[c] = jnp.where(key <= qry, buf[c], -jnp.inf)
            m_new = jnp.maximum(m, jnp.max(buf[c], axis=0, keepdims=True))
            a = jnp.exp2(m - m_new)
            p = [jnp.exp2((buf[c, u * half:(u + 1) * half, :] - m_new).astype(BF16)) for u in range(2)]
            vt = vt_ref[(c // 2) * LANES:(c // 2 + 1) * LANES, pl.ds(start, tq)]
            vt1 = jnp.concatenate([vt, jnp.ones((ONES_ROWS, tq), BF16)], axis=0)
            pv = jnp.dot(vt1, jnp.concatenate(p, axis=0), preferred_element_type=F32)
            l = a * l + pv[LANES:LANES + 1, :]
            acc_ref[c] = a * acc_ref[c] + pv[:LANES, :]
            out.append((m_new, l))
        return tuple(out)

    buf_a, buf_b = s_ref.at[0], s_ref.at[1]

    def two_blocks(jj, state):
        j = 2 * jj
        state = absorb(j, buf_a, state, False, nxt=buf_b)
        return absorb(j + 1, buf_b, state, False, nxt=buf_a)

    def last_from_a(state):
        return absorb(i, buf_a, state, True)

    def last_from_b(state):
        state = absorb(i - 1, buf_a, state, False, nxt=buf_b)
        return absorb(i, buf_b, state, True)

    init = tuple((jnp.full((1, tq), -jnp.inf, F32), jnp.zeros((1, tq), F32)) for _ in range(chains))
    score_block(0, buf_a)
    state = lax.fori_loop(0, i // 2, two_blocks, init)
    state = lax.cond(lax.rem(i, 2) == 0, last_from_a, last_from_b, state)

    lam = (jnp.exp(jnp.sum(lq1_ref[...] * lk1_ref[...], axis=1, keepdims=True))
           - jnp.exp(jnp.sum(lq2_ref[...] * lk2_ref[...], axis=1, keepdims=True)) + LAMBDA_INIT)
    for h in range(DA_HEADS):
        l1, l2 = state[2 * h][1], state[2 * h + 1][1]
        o = (acc_ref[2 * h] / l1 - lam * (acc_ref[2 * h + 1] / l2)).T
        ms = jnp.mean(o * o, axis=1, keepdims=True)
        o = o * lax.rsqrt(ms + EPS) * g_ref[...]
        o_ref[:, h * LANES:(h + 1) * LANES] = (o * (1.0 - LAMBDA_INIT)).astype(BF16)


def _diff_attention(qk3, vt3, lam_q1, lam_k1, lam_q2, lam_k2, subln_g):
    b, s, _ = qk3.shape
    tq = min(ATT_TQ, s)
    vec = lambda n: pl.BlockSpec((1, n), lambda bb, i: (0, 0))
    return pl.pallas_call(
        functools.partial(_diff_kernel, tq=tq),
        grid=(b, s // tq),
        in_specs=[
            pl.BlockSpec((None, tq, SEG_W), lambda bb, i: (bb, i, 0)),
            pl.BlockSpec((None, s, SEG_W), lambda bb, i: (bb, 0, 1)),
            pl.BlockSpec((None, SEG_W, s), lambda bb, i: (bb, 0, 0)),
            vec(HEAD_DIM), vec(HEAD_DIM), vec(HEAD_DIM), vec(HEAD_DIM), vec(2 * HEAD_DIM),
        ],
        out_specs=pl.BlockSpec((None, tq, SEG_W), lambda bb, i: (bb, i, 0)),
        out_shape=jax.ShapeDtypeStruct((b, s, SEG_W), BF16),
        scratch_shapes=[pltpu.VMEM((2, 2 * DA_HEADS, tq, tq), F32), pltpu.VMEM((2 * DA_HEADS, LANES, tq), F32)],
        compiler_params=_cparams(2),
        name="diff_attn",
    )(qk3, qk3, vt3, lam_q1.reshape(1, -1), lam_k1.reshape(1, -1), lam_q2.reshape(1, -1),
      lam_k2.reshape(1, -1), subln_g.reshape(1, -1))


def _dil_kernel(q_ref, kp_ref, kc_ref, vp_ref, vc_ref, o_ref, lse_ref, *, rel, nq, nres):
    n = pl.program_id(2)
    qi = lax.broadcasted_iota(jnp.int32, (QBLK, 2 * QBLK), 0)
    kj = lax.broadcasted_iota(jnp.int32, (QBLK, 2 * QBLK), 1)
    dist = qi + QBLK - kj
    band = (dist >= 0) & (dist <= rel)
    lane = lax.broadcasted_iota(jnp.int32, (QBLK, LANES), 1)
    low = lane < HEAD_DIM
    slabs = [slice(p * LANES, (p + 1) * LANES) for p in range(SEG_W // LANES)]

    def window(prev_ref, cur_ref, s, sl):
        if s == 0:
            return jnp.concatenate([prev_ref[:, sl], cur_ref[:QBLK, sl]], axis=0)
        return cur_ref[(s - 1) * QBLK:(s + 1) * QBLK, sl]

    def score_block(blk):
        res, s = blk
        rows = slice(s * QBLK, (s + 1) * QBLK)
        scores = []
        for sl in slabs:
            q2 = q_ref[res, rows, sl]
            k2 = window(kp_ref.at[res], kc_ref.at[res], s, sl)
            for sel in (low, jnp.logical_not(low)):
                scores.append(_nt_dot(jnp.where(sel, q2, jnp.zeros_like(q2)), k2))
        return scores

    def finish_block(blk, scores):
        res, s = blk
        rows = slice(s * QBLK, (s + 1) * QBLK)
        valid = band & ((kj >= QBLK) | (n * nq + s > 0))
        probs = []
        for sc in scores:
            sc = jnp.where(valid, sc, -jnp.inf)
            m = jnp.max(sc, axis=1, keepdims=True)
            probs.append((jnp.exp2((sc - m).astype(BF16)), m))
        one = jnp.ones((2 * QBLK, LANES), BF16)
        low_keys = lax.broadcasted_iota(jnp.int32, (2 * QBLK, LANES), 1) < HEAD_DIM
        for p, sl in enumerate(slabs):
            v2 = window(vp_ref.at[res], vc_ref.at[res], s, sl)
            (pe_lo, m_lo), (pe_hi, m_hi) = probs[2 * p:2 * p + 2]
            pv_lo = jnp.dot(pe_lo, jnp.where(low_keys, v2, one), preferred_element_type=F32)
            pv_hi = jnp.dot(pe_hi, jnp.where(low_keys, one, v2), preferred_element_type=F32)
            num = jnp.where(low, pv_lo, pv_hi)
            den = pltpu.roll(jnp.where(low, pv_hi, pv_lo), HEAD_DIM, 1)
            o_ref[res, rows, sl] = (num / den).astype(BF16)
            lse_ref[res, rows, sl] = jnp.where(low, m_lo, m_hi) + jnp.log2(den)

    blocks = [(res, s) for res in range(nres) for s in range(nq)]
    pending = score_block(blocks[0])
    for prev, blk in zip(blocks, blocks[1:]):
        nxt = score_block(blk)
        finish_block(prev, pending)
        pending = nxt
    finish_block(blocks[-1], pending)


def _dilated_group(qkv, batch, seq, group):
    window, dil = DIL_PAIRS[group]
    rel = window // dil
    length = seq // dil
    nb = length // QBLK
    nq = math.gcd(nb, DIL_NQ)
    nres = math.gcd(dil, DIL_NQ // nq)
    view = qkv.reshape(batch, dil, length, QKV_W)

    def spec(seg, prev):
        if prev:
            return pl.BlockSpec((None, nres, QBLK, SEG_W), lambda b, r, n: (b, r, jnp.maximum(n * nq - 1, 0), seg))
        return pl.BlockSpec((None, nres, nq * QBLK, SEG_W), lambda b, r, n: (b, r, n, seg))

    out_spec = pl.BlockSpec((None, nres, nq * QBLK, SEG_W), lambda b, r, n: (b, r, n, 0))
    o, lse = pl.pallas_call(
        functools.partial(_dil_kernel, rel=rel, nq=nq, nres=nres),
        grid=(batch, dil // nres, nb // nq),
        in_specs=[spec(0, False), spec(1, True), spec(1, False), spec(2, True), spec(2, False)],
        out_specs=[out_spec, out_spec],
        out_shape=[jax.ShapeDtypeStruct((batch, dil, length, SEG_W), BF16),
                   jax.ShapeDtypeStruct((batch, dil, length, SEG_W), F32)],
        compiler_params=_cparams(3),
        name=f"dilated{group}",
    )(view, view, view, view, view)
    return o, lse


def _layer_norm(z, g, b):
    mu = jnp.mean(z, axis=1, keepdims=True)
    zc = z - mu
    var = jnp.mean(zc * zc, axis=1, keepdims=True)
    return zc * lax.rsqrt(var + EPS) * g + b


def _pack_bf16_pairs(v):
    n = v.shape[1] // 2
    lo = pltpu.bitcast(v[:, :n].astype(BF16).astype(F32), U32)
    hi = pltpu.bitcast(v[:, n:].astype(BF16).astype(F32), U32)
    return jnp.bitwise_or(jnp.right_shift(lo, jnp.uint32(16)), jnp.bitwise_and(hi, jnp.uint32(0xFFFF0000)))


def _unpack_bf16_pairs(w):
    lo = pltpu.bitcast(jnp.left_shift(w, jnp.uint32(16)), F32)
    hi = pltpu.bitcast(jnp.bitwise_and(w, jnp.uint32(0xFFFF0000)), F32)
    return lo, hi


def _merge_kernel(ya_ref, o1_ref, o2_ref, o3_ref, l1_ref, l2_ref, l3_ref, x_ref,
                  wg_ref, bg_ref, woa_ref, wob_ref, wo_ref, g_ref, b_ref, wrh_ref, wrl_ref, br_ref,
                  h_ref, hp_ref, topi_ref, topw_ref, rank_ref, cnt_ref, carry_ref, scr_ref, wgb_ref):
    i = pl.program_id(0)
    tm = x_ref.shape[0]

    @pl.when(i == 0)
    def _():
        carry_ref[...] = jnp.zeros_like(carry_ref)
        wgb_ref[...] = wg_ref[...].astype(BF16)

    hm = tm // 2
    halves = (slice(0, hm), slice(hm, tm))
    xs = [x_ref[rows, :] for rows in halves]
    xbs = [x.astype(BF16) for x in xs]
    gate_dot = lambda xb: jnp.dot(xb, wgb_ref[...], preferred_element_type=F32) + bg_ref[...]
    gate_pre = [gate_dot(xbs[0])]
    pas = [jnp.dot(ya_ref[rows, :], woa_ref[...], preferred_element_type=F32) for rows in halves]

    def token_major(ref, scr):
        dil, n = ref.shape[0], ref.shape[1]
        if dil == 1:
            return ref[0].astype(F32)
        for r in range(dil):
            blk = ref[r].astype(F32)
            for c in range(SEG_W // LANES):
                scr[c, pl.ds(r, n, stride=dil), :] = blk[:, c * LANES:(c + 1) * LANES]
        return jnp.concatenate([scr[c] for c in range(SEG_W // LANES)], axis=1)

    oa, ob, oc = (token_major(r, scr_ref.at[k]) for k, r in enumerate((o1_ref, o2_ref, o3_ref)))
    la, lb, lc = (token_major(r, scr_ref.at[3 + k]) for k, r in enumerate((l1_ref, l2_ref, l3_ref)))
    mx = jnp.maximum(jnp.maximum(la, lb), lc)
    ea, eb, ec = jnp.exp2(la - mx), jnp.exp2(lb - mx), jnp.exp2(lc - mx)
    yb = ((ea * oa + eb * ob + ec * oc) / (ea + eb + ec)).astype(BF16)
    pbs = [jnp.dot(yb[rows, :], wob_ref[...], preferred_element_type=F32) for rows in halves]
    gate_pre.append(gate_dot(xbs[1]))
    mixes = []
    for pre_act, pa, pb in zip(gate_pre, pas, pbs):
        gates = jax.nn.sigmoid(pre_act)
        merged = gates[:, :D_MODEL] * pa + gates[:, D_MODEL:] * pb
        mixes.append(jnp.dot(merged.astype(BF16), wo_ref[...], preferred_element_type=F32))
    logits = []
    for rows, x, mix in zip(halves, xs, mixes):
        h = _layer_norm(DN_ALPHA * x + mix, g_ref[...], b_ref[...])
        h_ref[rows, :] = h
        hp_ref[rows, :] = _pack_bf16_pairs(h)
        hh = h.astype(BF16)
        hl = (h - hh.astype(F32)).astype(BF16)
        logits.append(jnp.dot(hh, wrh_ref[...], preferred_element_type=F32)
                      + jnp.dot(hl, wrh_ref[...], preferred_element_type=F32)
                      + jnp.dot(hh, wrl_ref[...], preferred_element_type=F32) + br_ref[...])

    lane = lax.broadcasted_iota(jnp.int32, (hm, LANES), 1)
    r_i = lax.broadcasted_iota(jnp.int32, (hm, hm), 0)
    c_i = lax.broadcasted_iota(jnp.int32, (hm, hm), 1)
    tri = (r_i > c_i).astype(BF16)
    before = carry_ref[...]
    for rows, lg in zip(halves, logits):
        lg = jnp.where(lane < N_EXPERTS, lg, -jnp.inf)
        vals, idxs = [], []
        for _ in range(TOP_K):
            mv = jnp.max(lg, axis=1, keepdims=True)
            ik = jnp.min(jnp.where(lg == mv, lane, LANES), axis=1, keepdims=True)
            vals.append(mv)
            idxs.append(ik)
            lg = jnp.where(lane == ik, -jnp.inf, lg)
        es = [jnp.exp(v - vals[0]) for v in vals]
        tot = es[0] + es[1] + es[2] + es[3]
        onehot = jnp.zeros((hm, LANES), F32)
        for ik in idxs:
            onehot = onehot + (lane == ik).astype(F32)
        pre = jnp.dot(tri, onehot.astype(BF16), preferred_element_type=F32) + before
        topi = jnp.zeros((hm, LANES), jnp.int32)
        topw = jnp.zeros((hm, LANES), F32)
        rank = jnp.zeros((hm, LANES), F32)
        for k in range(TOP_K):
            rk = jnp.sum(jnp.where(lane == idxs[k], pre, 0.0), axis=1, keepdims=True)
            topi = jnp.where(lane == k, idxs[k], topi)
            topw = jnp.where(lane == k, es[k] / tot, topw)
            rank = jnp.where(lane == k, rk, rank)
        topi_ref[rows, :] = topi
        topw_ref[rows, :] = topw
        rank_ref[rows, :] = rank.astype(jnp.int32)
        before = before + jnp.sum(onehot, axis=0, keepdims=True)
    carry_ref[...] = before
    cnt_ref[...] = before


def _merge(ya, dil_outs, x2, seq, w_b, b_gate, w_oa_b, w_ob_b, w_o_b, ln1_g, ln1_b, w_router, b_router):
    tokens = x2.shape[0]
    tm = MERGE_TM
    per_seq = seq // tm

    def group(a):
        dil = a.shape[1]
        return pl.BlockSpec((None, dil, tm // dil, SEG_W), lambda i: (i // per_seq, 0, i % per_seq, 0))

    gate_blk = w_b.shape[1] // (2 * D_MODEL) - 1
    wr = jnp.zeros((D_MODEL, LANES), F32).at[:, :N_EXPERTS].set(w_router)
    wr_hi = wr.astype(BF16)
    wr_lo = (wr - wr_hi.astype(F32)).astype(BF16)
    br = jnp.zeros((1, LANES), F32).at[0, :N_EXPERTS].set(b_router)
    row = lambda w: pl.BlockSpec((tm, w), lambda i: (i, 0))
    full = lambda a: pl.BlockSpec(a.shape, lambda i: (0,) * a.ndim)
    (o1, l1), (o2, l2), (o3, l3) = dil_outs
    bg = b_gate.reshape(1, -1)
    g1 = ln1_g.reshape(1, -1)
    b1 = ln1_b.reshape(1, -1)
    lane_out = lambda dt: jax.ShapeDtypeStruct((tokens, LANES), dt)
    return pl.pallas_call(
        _merge_kernel,
        grid=(tokens // tm,),
        in_specs=[row(SEG_W), group(o1), group(o2), group(o3), group(l1), group(l2), group(l3),
                  row(D_MODEL), pl.BlockSpec((D_MODEL, 2 * D_MODEL), lambda i: (0, gate_blk)),
                  full(bg), full(w_oa_b), full(w_ob_b), full(w_o_b), full(g1), full(b1),
                  full(wr_hi), full(wr_lo), full(br)],
        out_specs=[row(D_MODEL), row(D_MODEL // 2), row(LANES), row(LANES), row(LANES),
                   pl.BlockSpec((1, LANES), lambda i: (0, 0))],
        out_shape=[jax.ShapeDtypeStruct((tokens, D_MODEL), F32), jax.ShapeDtypeStruct((tokens, D_MODEL // 2), U32),
                   lane_out(jnp.int32), lane_out(F32), lane_out(jnp.int32), jax.ShapeDtypeStruct((1, LANES), F32)],
        scratch_shapes=[pltpu.VMEM((1, LANES), F32), pltpu.VMEM((6, SEG_W // LANES, tm, LANES), F32),
                        pltpu.VMEM((D_MODEL, 2 * D_MODEL), BF16)],
        compiler_params=_cparams(1),
        name="merge",
    )(ya, o1, o2, o3, l1, l2, l3, x2, w_b, bg, w_oa_b, w_ob_b, w_o_b, g1, b1, wr_hi, wr_lo, br)


def _sc_mesh():
    return plsc.VectorSubcoreMesh(core_axis_name="core", subcore_axis_name="subcore")


def _sc_dispatch_rows(table, dest_t, pad_src, pad_dst):
    tokens, d = table.shape
    slots = dest_t.shape[0] // tokens
    n_pad = pad_src.shape[0]
    mesh = _sc_mesh()
    workers = mesh.num_cores * mesh.num_subcores
    chunks = tokens // SC_CHUNK
    per = chunks // workers
    per_pad = n_pad // (SC_CHUNK * workers)
    assert per * workers * SC_CHUNK == tokens and per_pad * workers * SC_CHUNK == n_pad

    @pl.kernel(out_type=jax.ShapeDtypeStruct((slots * tokens + n_pad, d), table.dtype), mesh=mesh,
               scratch_types=[pltpu.VMEM((1, SC_CHUNK), jnp.int32), pltpu.VMEM((1, SC_CHUNK), jnp.int32),
                              pltpu.VMEM((SC_CHUNK, d), table.dtype)])
    def copy(t_hbm, d_hbm, ps_hbm, pd_hbm, o_hbm, s_vm, d_vm, buf):
        wid = lax.axis_index("core") * mesh.num_subcores + lax.axis_index("subcore")

        @pl.loop(0, per)
        def _(j):
            blk = wid * per + j
            pltpu.sync_copy(t_hbm.at[pl.ds(blk * SC_CHUNK, SC_CHUNK)], buf)
            for k in range(slots):
                pltpu.sync_copy(d_hbm.at[pl.ds(k * chunks + blk, 1)], d_vm)
                pltpu.sync_copy(buf, o_hbm.at[d_vm.at[0]])

        @pl.loop(0, per_pad)
        def _(j):
            blk = wid * per_pad + j
            pltpu.sync_copy(ps_hbm.at[pl.ds(blk, 1)], s_vm)
            pltpu.sync_copy(pd_hbm.at[pl.ds(blk, 1)], d_vm)
            pltpu.sync_copy(t_hbm.at[s_vm.at[0]], buf)
            pltpu.sync_copy(buf, o_hbm.at[d_vm.at[0]])

    return copy(table, dest_t.reshape(-1, SC_CHUNK), pad_src.reshape(-1, SC_CHUNK), pad_dst.reshape(-1, SC_CHUNK))


def _sc_gather_rows(table, idx):
    n = idx.shape[0]
    d = table.shape[1]
    mesh = _sc_mesh()
    workers = mesh.num_cores * mesh.num_subcores
    per = n // (SC_CHUNK * workers)
    assert per * SC_CHUNK * workers == n

    @pl.kernel(out_type=jax.ShapeDtypeStruct((n, d), table.dtype), mesh=mesh,
               scratch_types=[pltpu.VMEM((1, SC_CHUNK), jnp.int32), pltpu.VMEM((SC_CHUNK, d), table.dtype)])
    def gather(t_hbm, i_hbm, o_hbm, i_vm, buf):
        wid = lax.axis_index("core") * mesh.num_subcores + lax.axis_index("subcore")

        @pl.loop(0, per)
        def _(j):
            blk = wid * per + j
            pltpu.sync_copy(i_hbm.at[pl.ds(blk, 1)], i_vm)
            pltpu.sync_copy(t_hbm.at[i_vm.at[0]], buf)
            pltpu.sync_copy(buf, o_hbm.at[pl.ds(blk * SC_CHUNK, SC_CHUNK)])

    return gather(table, idx.reshape(-1, SC_CHUNK))


def _expert_kernel(be_ref, nx_ref, first_ref, slot_ref, nu_ref, x_ref, wgu_hbm, wd_hbm, *rest):
    nparts = MOE_BLOCKS_PER_STEP
    bias_refs, (y_ref, wgu_stage, wd_stage, wgu_b, wd_b, sem) = rest[:2 * nparts], rest[2 * nparts:]
    biases = [(bias_refs[2 * h], bias_refs[2 * h + 1]) for h in range(nparts)]
    i = pl.program_id(0)
    hb = x_ref.shape[0] // nparts
    blocks = [nparts * i + h for h in range(nparts)]

    def weight_copies(e):
        return (pltpu.make_async_copy(wgu_hbm.at[e], wgu_stage, sem.at[0]),
                pltpu.make_async_copy(wd_hbm.at[e], wd_stage, sem.at[1]))

    @pl.when(i == 0)
    def _():
        for cp in weight_copies(be_ref[0]):
            cp.start()

    for b in blocks:
        @pl.when(first_ref[b] == 1)
        def _():
            for cp in weight_copies(be_ref[b]):
                cp.wait()
            wgu_b[slot_ref[b]] = wgu_stage[...].astype(BF16)
            wd_b[slot_ref[b]] = wd_stage[...].astype(BF16)

            @pl.when(nx_ref[b] >= 0)
            def _():
                for cp in weight_copies(nx_ref[b]):
                    cp.start()

    cols = [slice(c * FF_CHUNK, (c + 1) * FF_CHUNK) for c in range(D_FF // FF_CHUNK)]

    def gate_up(h):
        slot = slot_ref[blocks[h]]
        lo, hi = _unpack_bf16_pairs(x_ref[h * hb:(h + 1) * hb, :])
        xb = jnp.concatenate([lo, hi], axis=1).astype(BF16)
        bgu_ref = biases[h][0]
        pre = []
        for gs in cols:
            us = slice(D_FF + gs.start, D_FF + gs.stop)
            pre.append((jnp.dot(xb, wgu_b[slot, :, gs], preferred_element_type=F32) + bgu_ref[:, gs],
                        jnp.dot(xb, wgu_b[slot, :, us], preferred_element_type=F32) + bgu_ref[:, us]))
        return pre

    def down(h, pre):
        slot = slot_ref[blocks[h]]
        acc = jnp.zeros((hb, D_MODEL), F32)
        for gs, (gate, up) in zip(cols, pre):
            gate = jnp.minimum(gate, SWIGLU_LIMIT)
            up = jnp.clip(up, -SWIGLU_LIMIT, SWIGLU_LIMIT)
            act = (up + 1.0) * gate * jax.nn.sigmoid(SWIGLU_ALPHA * gate)
            acc = acc + jnp.dot(act.astype(BF16), wd_b[slot, gs, :], preferred_element_type=F32)
        y_ref[h * hb:(h + 1) * hb, :] = _pack_bf16_pairs(acc + biases[h][1][...])

    all_used = blocks[-1] < nu_ref[0]

    @pl.when(all_used)
    def _():
        pre = gate_up(0)
        for h in range(1, nparts):
            nxt = gate_up(h)
            down(h - 1, pre)
            pre = nxt
        down(nparts - 1, pre)

    for h in range(nparts):
        @pl.when(jnp.logical_not(all_used) & (blocks[h] < nu_ref[0]))
        def _():
            down(h, gate_up(h))

        @pl.when(blocks[h] >= nu_ref[0])
        def _():
            y_ref[h * hb:(h + 1) * hb, :] = jnp.zeros((hb, y_ref.shape[1]), y_ref.dtype)


def _experts(xrows, block_e, next_e, first_blk, slot, n_used, w_gu, b_gu, w_down, b_down):
    n_rows = xrows.shape[0]
    bm = MOE_BM
    nparts = MOE_BLOCKS_PER_STEP
    assert (n_rows // bm) % nparts == 0
    bias = lambda h: (lambda i, be, nx, fb, sl, nu: (be[nparts * i + h], 0, 0))
    rows = lambda i, be, nx, fb, sl, nu: (i, 0)
    b_gu3 = b_gu.reshape(N_EXPERTS, 1, -1)
    b_down3 = b_down.reshape(N_EXPERTS, 1, -1)
    bias_specs, bias_args = [], []
    for h in range(nparts):
        bias_specs += [pl.BlockSpec((None, 1, 2 * D_FF), bias(h)), pl.BlockSpec((None, 1, D_MODEL), bias(h))]
        bias_args += [b_gu3, b_down3]
    return pl.pallas_call(
        _expert_kernel,
        grid_spec=pltpu.PrefetchScalarGridSpec(
            num_scalar_prefetch=5,
            grid=(n_rows // (nparts * bm),),
            in_specs=[pl.BlockSpec((nparts * bm, D_MODEL // 2), rows),
                      pl.BlockSpec(memory_space=pl.ANY), pl.BlockSpec(memory_space=pl.ANY)] + bias_specs,
            out_specs=pl.BlockSpec((nparts * bm, D_MODEL // 2), rows),
            scratch_shapes=[pltpu.VMEM((D_MODEL, 2 * D_FF), F32), pltpu.VMEM((D_FF, D_MODEL), F32),
                            pltpu.VMEM((nparts, D_MODEL, 2 * D_FF), BF16), pltpu.VMEM((nparts, D_FF, D_MODEL), BF16),
                            pltpu.SemaphoreType.DMA((2,))],
        ),
        out_shape=jax.ShapeDtypeStruct((n_rows, D_MODEL // 2), U32),
        compiler_params=_cparams(1),
        name="experts",
    )(block_e, next_e, first_blk, slot, n_used, xrows, w_gu, w_down, *bias_args)


def _combine_kernel(y0_ref, y1_ref, y2_ref, y3_ref, w_ref, h_ref, g_ref, b_ref, o_ref):
    w = w_ref[...]
    half = D_MODEL // 2
    f_lo = jnp.zeros((h_ref.shape[0], half), F32)
    f_hi = jnp.zeros((h_ref.shape[0], half), F32)
    for k, y_ref in enumerate((y0_ref, y1_ref, y2_ref, y3_ref)):
        lo, hi = _unpack_bf16_pairs(y_ref[...])
        f_lo = f_lo + lo * w[:, k:k + 1]
        f_hi = f_hi + hi * w[:, k:k + 1]
    f = jnp.concatenate([f_lo, f_hi], axis=1)
    o_ref[...] = _layer_norm(DN_ALPHA * h_ref[...] + f, g_ref[...], b_ref[...])


def _combine(yplanes, topw, h, ln2_g, ln2_b):
    tokens = h.shape[0]
    tm = MOE_TM
    per_plane = tokens // tm
    plane = lambda k: pl.BlockSpec((tm, D_MODEL // 2), lambda i: (k * per_plane + i, 0))
    return pl.pallas_call(
        _combine_kernel,
        grid=(per_plane,),
        in_specs=[plane(0), plane(1), plane(2), plane(3),
                  pl.BlockSpec((tm, LANES), lambda i: (i, 0)),
                  pl.BlockSpec((tm, D_MODEL), lambda i: (i, 0)),
                  pl.BlockSpec((1, D_MODEL), lambda i: (0, 0)),
                  pl.BlockSpec((1, D_MODEL), lambda i: (0, 0))],
        out_specs=pl.BlockSpec((tm, D_MODEL), lambda i: (i, 0)),
        out_shape=jax.ShapeDtypeStruct((tokens, D_MODEL), F32),
        compiler_params=_cparams(1),
        name="combine",
    )(yplanes, yplanes, yplanes, yplanes, topw, h, ln2_g.reshape(1, -1), ln2_b.reshape(1, -1))


def _routing_tables(topi, rank, cnt, tokens):
    bm = MOE_BM
    i32 = jnp.int32
    experts = jnp.arange(N_EXPERTS, dtype=i32)
    counts = cnt[0, :N_EXPERTS].astype(i32)
    padded = (counts + bm - 1) // bm * bm
    pend = jnp.cumsum(padded)
    pstart = pend - padded
    sel = topi[:, :TOP_K, None] == experts[None, None, :]
    dest = rank[:, :TOP_K] + jnp.sum(jnp.where(sel, pstart[None, None, :], 0), axis=-1)
    n_pad = N_EXPERTS * bm
    n_rows = tokens * TOP_K + n_pad
    starts = jnp.arange(n_rows // bm, dtype=i32) * bm
    block_e = jnp.minimum(jnp.sum((pend[None, :] <= starts[:, None]).astype(i32), axis=1), N_EXPERTS - 1)
    n_used = (pend[-1] // bm).reshape(1)
    is_block_e = block_e[:, None] == experts[None, :]
    after = jnp.sum(jnp.where(is_block_e, pend[None, :], 0), axis=1) // bm
    e_after = jnp.sum(jnp.where(after[:, None] == jnp.arange(n_rows // bm, dtype=i32)[None, :], block_e[None, :], 0),
                      axis=1)
    next_e = jnp.where(after < n_used[0], e_after, -1)
    blk = jnp.arange(n_rows // bm, dtype=i32)
    prev_e = jnp.concatenate([jnp.full((1,), -1, i32), block_e[:-1]])
    first_blk = ((blk < n_used[0]) & (block_e != prev_e)).astype(i32)
    slot = (jnp.cumsum(first_blk) - 1) % MOE_BLOCKS_PER_STEP
    pad_cnt = padded - counts
    pad_end = jnp.cumsum(pad_cnt)
    j = jnp.arange(n_pad, dtype=i32)
    owner = jnp.sum((pad_end[None, :] <= j[:, None]).astype(i32), axis=1)
    is_owner = owner[:, None] == experts[None, :]
    pick = lambda v: jnp.sum(jnp.where(is_owner, v[None, :], 0), axis=1)
    in_expert = pick(pstart + counts) + (j - pick(pad_end - pad_cnt))
    pad_rows = jnp.where(owner < N_EXPERTS, in_expert, pend[-1] + (j - pad_end[-1]))
    dest_t = dest.T.reshape(-1).astype(i32)
    tables = (block_e.astype(i32), next_e.astype(i32), first_blk, jnp.maximum(slot, 0).astype(i32), n_used.astype(i32))
    return dest_t, tables, j % tokens, pad_rows.astype(i32)


def kernel(x, w_in, b_gate, lam_q1, lam_k1, lam_q2, lam_k2, subln_g, w_oa, w_ob, w_o, ln1_g, ln1_b,
           w_router, b_router, w_gu, b_gu, w_down, b_down, ln2_g, ln2_b):
    batch, seq, d = x.shape
    tokens = batch * seq
    h = x.reshape(tokens, d)
    for l in range(DEPTH):
        q_scale = HEAD_DIM ** -0.5 * math.log2(math.e)
        qk_a, vt_a = _project_qkv(h, w_in[l], 0, seq, "qkv_diff", q_scale=q_scale, v_feature_major=True)
        ya = _diff_attention(qk_a.reshape(batch, seq, 2 * SEG_W), vt_a, lam_q1[l], lam_k1[l], lam_q2[l],
                             lam_k2[l], subln_g[l]).reshape(tokens, -1)
        dil = []
        for g, (_, dilation) in enumerate(DIL_PAIRS):
            qkv_g = _project_qkv(h, w_in[l], g + 1, seq, f"qkv_dil{g}", q_scale=q_scale, dil=dilation)
            dil.append(_dilated_group(qkv_g, batch, seq, g))
        h1, h1p, topi, topw, rank, cnt = _merge(ya, dil, h, seq, w_in[l], b_gate[l], w_oa[l].astype(BF16),
                                                w_ob[l].astype(BF16), w_o[l].astype(BF16), ln1_g[l], ln1_b[l],
                                                w_router[l], b_router[l])
        dest_t, expert_tables, pad_src, pad_dst = _routing_tables(topi, rank, cnt, tokens)
        xrows = _sc_dispatch_rows(h1p, dest_t, pad_src, pad_dst)
        yrows = _experts(xrows, *expert_tables, w_gu[l], b_gu[l], w_down[l], b_down[l])
        yplanes = _sc_gather_rows(yrows, dest_t)
        h = _combine(yplanes, topw, h1, ln2_g[l], ln2_b[l])
    return h.reshape(batch, seq, d)
```

```python
import functools
import math

import jax
import jax.numpy as jnp
from jax import lax
from jax.experimental import pallas as pl
from jax.experimental.pallas import tpu as pltpu
from jax.experimental.pallas import tpu_sc as plsc

F32 = jnp.float32
BF16 = jnp.bfloat16
U32 = jnp.uint32

D_MODEL = 1024
HEAD_DIM = 64
ROT_DIM = HEAD_DIM // 4
ROPE_THETA = 500000.0
QBLK = 128
DA_HEADS = 4
DIL_PAIRS = ((128, 1), (512, 4), (2048, 16))
SEG_W = 512
QKV_W = 3 * SEG_W
N_EXPERTS = 32
TOP_K = 4
D_FF = D_MODEL
SWIGLU_ALPHA = 1.702
SWIGLU_LIMIT = 7.0
DEPTH = 1
DN_ALPHA = (2 * DEPTH) ** 0.25
EPS = 1e-5
LAMBDA_INIT = 0.8 - 0.6 * math.exp(-0.3 * 0)

LANES = 128
QKV_TM = 1024
QKV_CHUNK = 256
STRIDE_STAGE = 4
ATT_TQ = 256
ONES_ROWS = 16
DIL_NQ = 16
FF_CHUNK = 512
MERGE_TM = 512
MOE_BM = 256
MOE_BLOCKS_PER_STEP = 4
MOE_TM = 512
SC_CHUNK = 128
VMEM_LIMIT = 52 * 1024 * 1024


def _cparams(n_axes):
    return pltpu.CompilerParams(dimension_semantics=("arbitrary",) * n_axes,
                                vmem_limit_bytes=VMEM_LIMIT)


def _qkv_kernel(x_ref, w_ref, cs_ref, o_ref, *rest, q_scale, dil):
    tm = x_ref.shape[0]
    n = tm // dil
    two_stage = dil > STRIDE_STAGE
    if two_stage:
        *maybe_vt_ref, wb_ref, slab_ref, slab2_ref = rest
    elif dil > 1:
        *maybe_vt_ref, wb_ref, slab_ref = rest
    else:
        *maybe_vt_ref, wb_ref = rest

    @pl.when(pl.program_id(0) == 0)
    def _():
        wb_ref[...] = w_ref[...].astype(BF16)

    if dil > 1:
        for c in range(D_MODEL // LANES):
            slab_ref[c] = x_ref[:, c * LANES:(c + 1) * LANES]
    if two_stage:
        part = tm // STRIDE_STAGE
        for c in range(D_MODEL // LANES):
            for q in range(STRIDE_STAGE):
                slab2_ref[c, q * part:(q + 1) * part, :] = slab_ref[c, pl.ds(q, part, stride=STRIDE_STAGE), :]

    def residue_rows(c, r, l0, count):
        if two_stage:
            first = (r % STRIDE_STAGE) * (tm // STRIDE_STAGE) + r // STRIDE_STAGE + l0 * (dil // STRIDE_STAGE)
            return slab2_ref[c, pl.ds(first, count, stride=dil // STRIDE_STAGE), :]
        return slab_ref[c, pl.ds(l0 * dil + r, count, stride=dil), :]

    def regrouped_rows(start):
        pieces = []
        for p in range(start, start + QKV_CHUNK, min(n, QKV_CHUNK)):
            pieces.append(jnp.concatenate(
                [residue_rows(c, p // n, p % n, min(n, QKV_CHUNK)) for c in range(D_MODEL // LANES)], axis=1))
        return jnp.concatenate(pieces, axis=0).astype(BF16)

    def store(rows, cols, val):
        if dil == 1:
            o_ref[rows, cols] = val
        elif n >= QKV_CHUNK:
            o_ref[rows.start // n, rows.start % n:rows.start % n + QKV_CHUNK, cols] = val
        else:
            per = QKV_CHUNK // n
            for q in range(per):
                o_ref[rows.start // n + q, :, cols] = val[q * n:(q + 1) * n]

    half = ROT_DIM // 2
    in_head = lax.broadcasted_iota(jnp.int32, (QKV_CHUNK, LANES), 1) % HEAD_DIM
    first, second = in_head < half, (in_head >= half) & (in_head < ROT_DIM)
    for rc in range(tm // QKV_CHUNK):
        rows = slice(rc * QKV_CHUNK, (rc + 1) * QKV_CHUNK)
        xb = regrouped_rows(rows.start) if dil > 1 else x_ref[rows, :].astype(BF16)
        cs = cs_ref[rows, :]
        c = jnp.where(first, cs, jnp.where(second, pltpu.roll(cs, half, 1), 1.0))
        s1 = jnp.where(second, cs, 0.0)
        s2 = jnp.where(first, -pltpu.roll(cs, LANES - half, 1), 0.0)
        for seg in range(3):
            acc = jnp.dot(xb, wb_ref[:, seg * SEG_W:(seg + 1) * SEG_W], preferred_element_type=F32)
            if seg == 2:
                if maybe_vt_ref:
                    maybe_vt_ref[0][:, rows] = acc.T.astype(BF16)
                else:
                    store(rows, slice(seg * SEG_W, (seg + 1) * SEG_W), acc.astype(BF16))
                continue
            for k in range(SEG_W // LANES):
                t = acc[:, k * LANES:(k + 1) * LANES]
                r = t * c + pltpu.roll(t, ROT_DIM // 2, 1) * s1 + pltpu.roll(t, LANES - ROT_DIM // 2, 1) * s2
                if seg == 0:
                    r = r * q_scale
                lo = seg * SEG_W + k * LANES
                store(rows, slice(lo, lo + LANES), r.astype(BF16))


def _rope_lane_table(seq, dil, tm):
    row = jnp.arange(seq, dtype=jnp.int32)
    n = tm // dil
    in_tile = row % tm
    pos = (row - in_tile + (in_tile % n) * dil + in_tile // n).astype(F32)
    inv_freq = ROPE_THETA ** (-jnp.arange(0, ROT_DIM, 2, dtype=F32) / ROT_DIM)
    ang = pos[:, None] * inv_freq[None, :]
    head = jnp.concatenate([jnp.cos(ang), jnp.sin(ang), jnp.zeros((seq, HEAD_DIM - ROT_DIM), F32)], axis=1)
    return jnp.tile(head, (1, LANES // HEAD_DIM))


def _project_qkv(x2, w, col_block, seq, name, q_scale, dil=1, v_feature_major=False):
    tokens = x2.shape[0]
    batch = tokens // seq
    tm = min(QKV_TM, seq)
    per_seq = seq // tm
    n = tm // dil
    out_w = 2 * SEG_W if v_feature_major else QKV_W
    scratch = [pltpu.VMEM((D_MODEL, QKV_W), BF16)]
    if dil == 1:
        out_specs = [pl.BlockSpec((tm, out_w), lambda i: (i, 0))]
        out_shape = [jax.ShapeDtypeStruct((tokens, out_w), BF16)]
    else:
        out_specs = [pl.BlockSpec((None, dil, n, out_w), lambda i: (i // per_seq, 0, i % per_seq, 0))]
        out_shape = [jax.ShapeDtypeStruct((batch, dil, seq // dil, out_w), BF16)]
        scratch += [pltpu.VMEM((D_MODEL // LANES, tm, LANES), F32)] * (2 if dil > STRIDE_STAGE else 1)
    if v_feature_major:
        out_specs.append(pl.BlockSpec((None, SEG_W, tm), lambda i: (i // per_seq, 0, i % per_seq)))
        out_shape.append(jax.ShapeDtypeStruct((batch, SEG_W, seq), BF16))
    outs = pl.pallas_call(
        functools.partial(_qkv_kernel, q_scale=q_scale, dil=dil),
        grid=(tokens // tm,),
        in_specs=[pl.BlockSpec((tm, D_MODEL), lambda i: (i, 0)),
                  pl.BlockSpec((D_MODEL, QKV_W), lambda i: (0, col_block)),
                  pl.BlockSpec((tm, LANES), lambda i: (i % per_seq, 0))],
        out_specs=out_specs,
        out_shape=out_shape,
        scratch_shapes=scratch,
        compiler_params=_cparams(1),
        name=name,
    )(x2, w, _rope_lane_table(seq, dil, tm))
    return outs if v_feature_major else outs[0]


def _nt_dot(a, b):
    return lax.dot_general(a, b, (((1,), (1,)), ((), ())), preferred_element_type=F32)


def _diff_kernel(q_ref, k_ref, vt_ref, lq1_ref, lk1_ref, lq2_ref, lk2_ref, g_ref, o_ref, s_ref, acc_ref, *, tq):
    i = pl.program_id(1)
    chains = 2 * DA_HEADS
    half = tq // 2
    lane = lax.broadcasted_iota(jnp.int32, (tq, LANES), 1)
    qs = []
    for h in range(DA_HEADS):
        q = q_ref[:, h * LANES:(h + 1) * LANES]
        zero = jnp.zeros_like(q)
        qs += [jnp.where(lane < HEAD_DIM, q, zero), jnp.where(lane >= HEAD_DIM, q, zero)]
    acc_ref[...] = jnp.zeros_like(acc_ref)

    def score_chain(j, buf, c):
        start = pl.multiple_of(j * tq, tq)
        kb = k_ref[pl.ds(start, tq), (c // 2) * LANES:(c // 2 + 1) * LANES]
        buf[c] = _nt_dot(kb, qs[c])

    def score_block(j, buf):
        for c in range(chains):
            score_chain(j, buf, c)

    def absorb(j, buf, state, masked, nxt=None):
        start = pl.multiple_of(j * tq, tq)
        out = []
        for c in range(chains):
            if nxt is not None:
                score_chain(j + 1, nxt, c)
            m, l = state[c]
            if masked:
                key = lax.broadcasted_iota(jnp.int32, (tq, tq), 0)
                qry = lax.broadcasted_iota(jnp.int32, (tq, tq), 1)
                buf[c] = jnp.where(key <= qry, buf[c], -jnp.inf)
            m_new = jnp.maximum(m, jnp.max(buf[c], axis=0, keepdims=True))
            a = jnp.exp2(m - m_new)
            p = [jnp.exp2((buf[c, u * half:(u + 1) * half, :] - m_new).astype(BF16)) for u in range(2)]
            vt = vt_ref[(c // 2) * LANES:(c // 2 + 1) * LANES, pl.ds(start, tq)]
            vt1 = jnp.concatenate([vt, jnp.ones((ONES_ROWS, tq), BF16)], axis=0)
            pv = jnp.dot(vt1, jnp.concatenate(p, axis=0), preferred_element_type=F32)
            l = a * l + pv[LANES:LANES + 1, :]
            acc_ref[c] = a * acc_ref[c] + pv[:LANES, :]
            out.append((m_new, l))
        return tuple(out)

    buf_a, buf_b = s_ref.at[0], s_ref.at[1]

    def two_blocks(jj, state):
        j = 2 * jj
        state = absorb(j, buf_a, state, False, nxt=buf_b)
        return absorb(j + 1, buf_b, state, False, nxt=buf_a)

    def last_from_a(state):
        return absorb(i, buf_a, state, True)

    def last_from_b(state):
        state = absorb(i - 1, buf_a, state, False, nxt=buf_b)
        return absorb(i, buf_b, state, True)

    init = tuple((jnp.full((1, tq), -jnp.inf, F32), jnp.zeros((1, tq), F32)) for _ in range(chains))
    score_block(0, buf_a)
    state = lax.fori_loop(0, i // 2, two_blocks, init)
    state = lax.cond(lax.rem(i, 2) == 0, last_from_a, last_from_b, state)

    lam = (jnp.exp(jnp.sum(lq1_ref[...] * lk1_ref[...], axis=1, keepdims=True))
           - jnp.exp(jnp.sum(lq2_ref[...] * lk2_ref[...], axis=1, keepdims=True)) + LAMBDA_INIT)
    for h in range(DA_HEADS):
        l1, l2 = state[2 * h][1], state[2 * h + 1][1]
        o = (acc_ref[2 * h] / l1 - lam * (acc_ref[2 * h + 1] / l2)).T
        ms = jnp.mean(o * o, axis=1, keepdims=True)
        o = o * lax.rsqrt(ms + EPS) * g_ref[...]
        o_ref[:, h * LANES:(h + 1) * LANES] = (o * (1.0 - LAMBDA_INIT)).astype(BF16)


def _diff_attention(qk3, vt3, lam_q1, lam_k1, lam_q2, lam_k2, subln_g):
    b, s, _ = qk3.shape
    tq = min(ATT_TQ, s)
    vec = lambda n: pl.BlockSpec((1, n), lambda bb, i: (0, 0))
    return pl.pallas_call(
        functools.partial(_diff_kernel, tq=tq),
        grid=(b, s // tq),
        in_specs=[
            pl.BlockSpec((None, tq, SEG_W), lambda bb, i: (bb, i, 0)),
            pl.BlockSpec((None, s, SEG_W), lambda bb, i: (bb, 0, 1)),
            pl.BlockSpec((None, SEG_W, s), lambda bb, i: (bb, 0, 0)),
            vec(HEAD_DIM), vec(HEAD_DIM), vec(HEAD_DIM), vec(HEAD_DIM), vec(2 * HEAD_DIM),
        ],
        out_specs=pl.BlockSpec((None, tq, SEG_W), lambda bb, i: (bb, i, 0)),
        out_shape=jax.ShapeDtypeStruct((b, s, SEG_W), BF16),
        scratch_shapes=[pltpu.VMEM((2, 2 * DA_HEADS, tq, tq), F32), pltpu.VMEM((2 * DA_HEADS, LANES, tq), F32)],
        compiler_params=_cparams(2),
        name="diff_attn",
    )(qk3, qk3, vt3, lam_q1.reshape(1, -1), lam_k1.reshape(1, -1), lam_q2.reshape(1, -1),
      lam_k2.reshape(1, -1), subln_g.reshape(1, -1))


def _dil_kernel(q_ref, kp_ref, kc_ref, vp_ref, vc_ref, o_ref, lse_ref, *, rel, nq, nres):
    n = pl.program_id(2)
    qi = lax.broadcasted_iota(jnp.int32, (QBLK, 2 * QBLK), 0)
    kj = lax.broadcasted_iota(jnp.int32, (QBLK, 2 * QBLK), 1)
    dist = qi + QBLK - kj
    band = (dist >= 0) & (dist <= rel)
    lane = lax.broadcasted_iota(jnp.int32, (QBLK, LANES), 1)
    low = lane < HEAD_DIM
    slabs = [slice(p * LANES, (p + 1) * LANES) for p in range(SEG_W // LANES)]

    def window(prev_ref, cur_ref, s, sl):
        if s == 0:
            return jnp.concatenate([prev_ref[:, sl], cur_ref[:QBLK, sl]], axis=0)
        return cur_ref[(s - 1) * QBLK:(s + 1) * QBLK, sl]

    def score_block(blk):
        res, s = blk
        rows = slice(s * QBLK, (s + 1) * QBLK)
        scores = []
        for sl in slabs:
            q2 = q_ref[res, rows, sl]
            k2 = window(kp_ref.at[res], kc_ref.at[res], s, sl)
            for sel in (low, jnp.logical_not(low)):
                scores.append(_nt_dot(jnp.where(sel, q2, jnp.zeros_like(q2)), k2))
        return scores

    def finish_block(blk, scores):
        res, s = blk
        rows = slice(s * QBLK, (s + 1) * QBLK)
        valid = band & ((kj >= QBLK) | (n * nq + s > 0))
        probs = []
        for sc in scores:
            sc = jnp.where(valid, sc, -jnp.inf)
            m = jnp.max(sc, axis=1, keepdims=True)
            probs.append((jnp.exp2((sc - m).astype(BF16)), m))
        one = jnp.ones((2 * QBLK, LANES), BF16)
        low_keys = lax.broadcasted_iota(jnp.int32, (2 * QBLK, LANES), 1) < HEAD_DIM
        for p, sl in enumerate(slabs):
            v2 = window(vp_ref.at[res], vc_ref.at[res], s, sl)
            (pe_lo, m_lo), (pe_hi, m_hi) = probs[2 * p:2 * p + 2]
            pv_lo = jnp.dot(pe_lo, jnp.where(low_keys, v2, one), preferred_element_type=F32)
            pv_hi = jnp.dot(pe_hi, jnp.where(low_keys, one, v2), preferred_element_type=F32)
            num = jnp.where(low, pv_lo, pv_hi)
            den = pltpu.roll(jnp.where(low, pv_hi, pv_lo), HEAD_DIM, 1)
            o_ref[res, rows, sl] = (num / den).astype(BF16)
            lse_ref[res, rows, sl] = jnp.where(low, m_lo, m_hi) + jnp.log2(den)

    blocks = [(res, s) for res in range(nres) for s in range(nq)]
    pending = score_block(blocks[0])
    for prev, blk in zip(blocks, blocks[1:]):
        nxt = score_block(blk)
        finish_block(prev, pending)
        pending = nxt
    finish_block(blocks[-1], pending)


def _dilated_group(qkv, batch, seq, group):
    window, dil = DIL_PAIRS[group]
    rel = window // dil
    length = seq // dil
    nb = length // QBLK
    nq = math.gcd(nb, DIL_NQ)
    nres = math.gcd(dil, DIL_NQ // nq)
    view = qkv.reshape(batch, dil, length, QKV_W)

    def spec(seg, prev):
        if prev:
            return pl.BlockSpec((None, nres, QBLK, SEG_W), lambda b, r, n: (b, r, jnp.maximum(n * nq - 1, 0), seg))
        return pl.BlockSpec((None, nres, nq * QBLK, SEG_W), lambda b, r, n: (b, r, n, seg))

    out_spec = pl.BlockSpec((None, nres, nq * QBLK, SEG_W), lambda b, r, n: (b, r, n, 0))
    o, lse = pl.pallas_call(
        functools.partial(_dil_kernel, rel=rel, nq=nq, nres=nres),
        grid=(batch, dil // nres, nb // nq),
        in_specs=[spec(0, False), spec(1, True), spec(1, False), spec(2, True), spec(2, False)],
        out_specs=[out_spec, out_spec],
        out_shape=[jax.ShapeDtypeStruct((batch, dil, length, SEG_W), BF16),
                   jax.ShapeDtypeStruct((batch, dil, length, SEG_W), F32)],
        compiler_params=_cparams(3),
        name=f"dilated{group}",
    )(view, view, view, view, view)
    return o, lse


def _layer_norm(z, g, b):
    mu = jnp.mean(z, axis=1, keepdims=True)
    zc = z - mu
    var = jnp.mean(zc * zc, axis=1, keepdims=True)
    return zc * lax.rsqrt(var + EPS) * g + b


def _pack_bf16_pairs(v):
    n = v.shape[1] // 2
    lo = pltpu.bitcast(v[:, :n].astype(BF16).astype(F32), U32)
    hi = pltpu.bitcast(v[:, n:].astype(BF16).astype(F32), U32)
    return jnp.bitwise_or(jnp.right_shift(lo, jnp.uint32(16)), jnp.bitwise_and(hi, jnp.uint32(0xFFFF0000)))


def _unpack_bf16_pairs(w):
    lo = pltpu.bitcast(jnp.left_shift(w, jnp.uint32(16)), F32)
    hi = pltpu.bitcast(jnp.bitwise_and(w, jnp.uint32(0xFFFF0000)), F32)
    return lo, hi


def _merge_kernel(ya_ref, o1_ref, o2_ref, o3_ref, l1_ref, l2_ref, l3_ref, x_ref,
                  wg_ref, bg_ref, woa_ref, wob_ref, wo_ref, g_ref, b_ref, wrh_ref, wrl_ref, br_ref,
                  h_ref, hp_ref, topi_ref, topw_ref, rank_ref, cnt_ref, carry_ref, scr_ref, wgb_ref):
    i = pl.program_id(0)
    tm = x_ref.shape[0]

    @pl.when(i == 0)
    def _():
        carry_ref[...] = jnp.zeros_like(carry_ref)
        wgb_ref[...] = wg_ref[...].astype(BF16)

    hm = tm // 2
    halves = (slice(0, hm), slice(hm, tm))
    xs = [x_ref[rows, :] for rows in halves]
    xbs = [x.astype(BF16) for x in xs]
    gate_dot = lambda xb: jnp.dot(xb, wgb_ref[...], preferred_element_type=F32) + bg_ref[...]
    gate_pre = [gate_dot(xbs[0])]
    pas = [jnp.dot(ya_ref[rows, :], woa_ref[...], preferred_element_type=F32) for rows in halves]

    def token_major(ref, scr):
        dil, n = ref.shape[0], ref.shape[1]
        if dil == 1:
            return ref[0].astype(F32)
        for r in range(dil):
            blk = ref[r].astype(F32)
            for c in range(SEG_W // LANES):
                scr[c, pl.ds(r, n, stride=dil), :] = blk[:, c * LANES:(c + 1) * LANES]
        return jnp.concatenate([scr[c] for c in range(SEG_W // LANES)], axis=1)

    oa, ob, oc = (token_major(r, scr_ref.at[k]) for k, r in enumerate((o1_ref, o2_ref, o3_ref)))
    la, lb, lc = (token_major(r, scr_ref.at[3 + k]) for k, r in enumerate((l1_ref, l2_ref, l3_ref)))
    mx = jnp.maximum(jnp.maximum(la, lb), lc)
    ea, eb, ec = jnp.exp2(la - mx), jnp.exp2(lb - mx), jnp.exp2(lc - mx)
    yb = ((ea * oa + eb * ob + ec * oc) / (ea + eb + ec)).astype(BF16)
    pbs = [jnp.dot(yb[rows, :], wob_ref[...], preferred_element_type=F32) for rows in halves]
    gate_pre.append(gate_dot(xbs[1]))
    mixes = []
    for pre_act, pa, pb in zip(gate_pre, pas, pbs):
        gates = jax.nn.sigmoid(pre_act)
        merged = gates[:, :D_MODEL] * pa + gates[:, D_MODEL:] * pb
        mixes.append(jnp.dot(merged.astype(BF16), wo_ref[...], preferred_element_type=F32))
    logits = []
    for rows, x, mix in zip(halves, xs, mixes):
        h = _layer_norm(DN_ALPHA * x + mix, g_ref[...], b_ref[...])
        h_ref[rows, :] = h
        hp_ref[rows, :] = _pack_bf16_pairs(h)
        hh = h.astype(BF16)
        hl = (h - hh.astype(F32)).astype(BF16)
        logits.append(jnp.dot(hh, wrh_ref[...], preferred_element_type=F32)
                      + jnp.dot(hl, wrh_ref[...], preferred_element_type=F32)
                      + jnp.dot(hh, wrl_ref[...], preferred_element_type=F32) + br_ref[...])

    lane = lax.broadcasted_iota(jnp.int32, (hm, LANES), 1)
    r_i = lax.broadcasted_iota(jnp.int32, (hm, hm), 0)
    c_i = lax.broadcasted_iota(jnp.int32, (hm, hm), 1)
    tri = (r_i > c_i).astype(BF16)
    before = carry_ref[...]
    for rows, lg in zip(halves, logits):
        lg = jnp.where(lane < N_EXPERTS, lg, -jnp.inf)
        vals, idxs = [], []
        for _ in range(TOP_K):
            mv = jnp.max(lg, axis=1, keepdims=True)
            ik = jnp.min(jnp.where(lg == mv, lane, LANES), axis=1, keepdims=True)
            vals.append(mv)
            idxs.append(ik)
            lg = jnp.where(lane == ik, -jnp.inf, lg)
        es = [jnp.exp(v - vals[0]) for v in vals]
        tot = es[0] + es[1] + es[2] + es[3]
        onehot = jnp.zeros((hm, LANES), F32)
        for ik in idxs:
            onehot = onehot + (lane == ik).astype(F32)
        pre = jnp.dot(tri, onehot.astype(BF16), preferred_element_type=F32) + before
        topi = jnp.zeros((hm, LANES), jnp.int32)
        topw = jnp.zeros((hm, LANES), F32)
        rank = jnp.zeros((hm, LANES), F32)
        for k in range(TOP_K):
            rk = jnp.sum(jnp.where(lane == idxs[k], pre, 0.0), axis=1, keepdims=True)
            topi = jnp.where(lane == k, idxs[k], topi)
            topw = jnp.where(lane == k, es[k] / tot, topw)
            rank = jnp.where(lane == k, rk, rank)
        topi_ref[rows, :] = topi
        topw_ref[rows, :] = topw
        rank_ref[rows, :] = rank.astype(jnp.int32)
        before = before + jnp.sum(onehot, axis=0, keepdims=True)
    carry_ref[...] = before
    cnt_ref[...] = before


def _merge(ya, dil_outs, x2, seq, w_b, b_gate, w_oa_b, w_ob_b, w_o_b, ln1_g, ln1_b, w_router, b_router):
    tokens = x2.shape[0]
    tm = MERGE_TM
    per_seq = seq // tm

    def group(a):
        dil = a.shape[1]
        return pl.BlockSpec((None, dil, tm // dil, SEG_W), lambda i: (i // per_seq, 0, i % per_seq, 0))

    gate_blk = w_b.shape[1] // (2 * D_MODEL) - 1
    wr = jnp.zeros((D_MODEL, LANES), F32).at[:, :N_EXPERTS].set(w_router)
    wr_hi = wr.astype(BF16)
    wr_lo = (wr - wr_hi.astype(F32)).astype(BF16)
    br = jnp.zeros((1, LANES), F32).at[0, :N_EXPERTS].set(b_router)
    row = lambda w: pl.BlockSpec((tm, w), lambda i: (i, 0))
    full = lambda a: pl.BlockSpec(a.shape, lambda i: (0,) * a.ndim)
    (o1, l1), (o2, l2), (o3, l3) = dil_outs
    bg = b_gate.reshape(1, -1)
    g1 = ln1_g.reshape(1, -1)
    b1 = ln1_b.reshape(1, -1)
    lane_out = lambda dt: jax.ShapeDtypeStruct((tokens, LANES), dt)
    return pl.pallas_call(
        _merge_kernel,
        grid=(tokens // tm,),
        in_specs=[row(SEG_W), group(o1), group(o2), group(o3), group(l1), group(l2), group(l3),
                  row(D_MODEL), pl.BlockSpec((D_MODEL, 2 * D_MODEL), lambda i: (0, gate_blk)),
                  full(bg), full(w_oa_b), full(w_ob_b), full(w_o_b), full(g1), full(b1),
                  full(wr_hi), full(wr_lo), full(br)],
        out_specs=[row(D_MODEL), row(D_MODEL // 2), row(LANES), row(LANES), row(LANES),
                   pl.BlockSpec((1, LANES), lambda i: (0, 0))],
        out_shape=[jax.ShapeDtypeStruct((tokens, D_MODEL), F32), jax.ShapeDtypeStruct((tokens, D_MODEL // 2), U32),
                   lane_out(jnp.int32), lane_out(F32), lane_out(jnp.int32), jax.ShapeDtypeStruct((1, LANES), F32)],
        scratch_shapes=[pltpu.VMEM((1, LANES), F32), pltpu.VMEM((6, SEG_W // LANES, tm, LANES), F32),
                        pltpu.VMEM((D_MODEL, 2 * D_MODEL), BF16)],
        compiler_params=_cparams(1),
        name="merge",
    )(ya, o1, o2, o3, l1, l2, l3, x2, w_b, bg, w_oa_b, w_ob_b, w_o_b, g1, b1, wr_hi, wr_lo, br)


def _sc_mesh():
    return plsc.VectorSubcoreMesh(core_axis_name="core", subcore_axis_name="subcore")


def _sc_dispatch_rows(table, dest_t, pad_dst):
    tokens, d = table.shape
    slots = dest_t.shape[0] // tokens
    n_pad = pad_dst.shape[0]
    mesh = _sc_mesh()
    workers = mesh.num_cores * mesh.num_subcores
    chunks = tokens // SC_CHUNK
    per = chunks // workers
    pad_every = chunks * SC_CHUNK // n_pad
    assert per * workers * SC_CHUNK == tokens and pad_every * n_pad == tokens

    @pl.kernel(out_type=jax.ShapeDtypeStruct((slots * tokens + n_pad, d), table.dtype), mesh=mesh,
               scratch_types=[pltpu.VMEM((1, SC_CHUNK), jnp.int32), pltpu.VMEM((SC_CHUNK, d), table.dtype)])
    def copy(t_hbm, d_hbm, pd_hbm, o_hbm, d_vm, buf):
        wid = lax.axis_index("core") * mesh.num_subcores + lax.axis_index("subcore")

        @pl.loop(0, per)
        def _(j):
            blk = wid * per + j
            pltpu.sync_copy(t_hbm.at[pl.ds(blk * SC_CHUNK, SC_CHUNK)], buf)
            for k in range(slots):
                pltpu.sync_copy(d_hbm.at[pl.ds(k * chunks + blk, 1)], d_vm)
                pltpu.sync_copy(buf, o_hbm.at[d_vm.at[0]])

            @pl.when(lax.rem(blk, pad_every) == 0)
            def _():
                pltpu.sync_copy(pd_hbm.at[pl.ds(blk // pad_every, 1)], d_vm)
                pltpu.sync_copy(buf, o_hbm.at[d_vm.at[0]])

    return copy(table, dest_t.reshape(-1, SC_CHUNK), pad_dst.reshape(-1, SC_CHUNK))


def _sc_gather_rows(table, idx):
    n = idx.shape[0]
    d = table.shape[1]
    mesh = _sc_mesh()
    workers = mesh.num_cores * mesh.num_subcores
    per = n // (SC_CHUNK * workers)
    assert per * SC_CHUNK * workers == n

    @pl.kernel(out_type=jax.ShapeDtypeStruct((n, d), table.dtype), mesh=mesh,
               scratch_types=[pltpu.VMEM((1, SC_CHUNK), jnp.int32), pltpu.VMEM((SC_CHUNK, d), table.dtype)])
    def gather(t_hbm, i_hbm, o_hbm, i_vm, buf):
        wid = lax.axis_index("core") * mesh.num_subcores + lax.axis_index("subcore")

        @pl.loop(0, per)
        def _(j):
            blk = wid * per + j
            pltpu.sync_copy(i_hbm.at[pl.ds(blk, 1)], i_vm)
            pltpu.sync_copy(t_hbm.at[i_vm.at[0]], buf)
            pltpu.sync_copy(buf, o_hbm.at[pl.ds(blk * SC_CHUNK, SC_CHUNK)])

    return gather(table, idx.reshape(-1, SC_CHUNK))


def _expert_kernel(be_ref, nx_ref, first_ref, slot_ref, nu_ref, x_ref, wgu_hbm, wd_hbm, *rest):
    nparts = MOE_BLOCKS_PER_STEP
    bias_refs, (y_ref, wgu_stage, wd_stage, wgu_b, wd_b, sem) = rest[:2 * nparts], rest[2 * nparts:]
    biases = [(bias_refs[2 * h], bias_refs[2 * h + 1]) for h in range(nparts)]
    i = pl.program_id(0)
    hb = x_ref.shape[0] // nparts
    blocks = [nparts * i + h for h in range(nparts)]

    def weight_copies(e):
        return (pltpu.make_async_copy(wgu_hbm.at[e], wgu_stage, sem.at[0]),
                pltpu.make_async_copy(wd_hbm.at[e], wd_stage, sem.at[1]))

    @pl.when(i == 0)
    def _():
        for cp in weight_copies(be_ref[0]):
            cp.start()

    for b in blocks:
        @pl.when(first_ref[b] == 1)
        def _():
            for cp in weight_copies(be_ref[b]):
                cp.wait()
            wgu_b[slot_ref[b]] = wgu_stage[...].astype(BF16)
            wd_b[slot_ref[b]] = wd_stage[...].astype(BF16)

            @pl.when(nx_ref[b] >= 0)
            def _():
                for cp in weight_copies(nx_ref[b]):
                    cp.start()

    cols = [slice(c * FF_CHUNK, (c + 1) * FF_CHUNK) for c in range(D_FF // FF_CHUNK)]

    def gate_up(h):
        slot = slot_ref[blocks[h]]
        lo, hi = _unpack_bf16_pairs(x_ref[h * hb:(h + 1) * hb, :])
        xb = jnp.concatenate([lo, hi], axis=1).astype(BF16)
        bgu_ref = biases[h][0]
        pre = []
        for gs in cols:
            us = slice(D_FF + gs.start, D_FF + gs.stop)
            pre.append((jnp.dot(xb, wgu_b[slot, :, gs], preferred_element_type=F32) + bgu_ref[:, gs],
                        jnp.dot(xb, wgu_b[slot, :, us], preferred_element_type=F32) + bgu_ref[:, us]))
        return pre

    def down(h, pre):
        slot = slot_ref[blocks[h]]
        acc = jnp.zeros((hb, D_MODEL), F32)
        for gs, (gate, up) in zip(cols, pre):
            gate = jnp.minimum(gate, SWIGLU_LIMIT)
            up = jnp.clip(up, -SWIGLU_LIMIT, SWIGLU_LIMIT)
            act = (up + 1.0) * gate * jax.nn.sigmoid(SWIGLU_ALPHA * gate)
            acc = acc + jnp.dot(act.astype(BF16), wd_b[slot, gs, :], preferred_element_type=F32)
        y_ref[h * hb:(h + 1) * hb, :] = _pack_bf16_pairs(acc + biases[h][1][...])

    all_used = blocks[-1] < nu_ref[0]

    @pl.when(all_used)
    def _():
        pre = gate_up(0)
        for h in range(1, nparts):
            nxt = gate_up(h)
            down(h - 1, pre)
            pre = nxt
        down(nparts - 1, pre)

    for h in range(nparts):
        @pl.when(jnp.logical_not(all_used) & (blocks[h] < nu_ref[0]))
        def _():
            down(h, gate_up(h))

        @pl.when(blocks[h] >= nu_ref[0])
        def _():
            y_ref[h * hb:(h + 1) * hb, :] = jnp.zeros((hb, y_ref.shape[1]), y_ref.dtype)


def _experts(xrows, block_e, next_e, first_blk, slot, n_used, w_gu, b_gu, w_down, b_down):
    n_rows = xrows.shape[0]
    bm = MOE_BM
    nparts = MOE_BLOCKS_PER_STEP
    assert (n_rows // bm) % nparts == 0
    bias = lambda h: (lambda i, be, nx, fb, sl, nu: (be[nparts * i + h], 0, 0))
    rows = lambda i, be, nx, fb, sl, nu: (i, 0)
    b_gu3 = b_gu.reshape(N_EXPERTS, 1, -1)
    b_down3 = b_down.reshape(N_EXPERTS, 1, -1)
    bias_specs, bias_args = [], []
    for h in range(nparts):
        bias_specs += [pl.BlockSpec((None, 1, 2 * D_FF), bias(h)), pl.BlockSpec((None, 1, D_MODEL), bias(h))]
        bias_args += [b_gu3, b_down3]
    return pl.pallas_call(
        _expert_kernel,
        grid_spec=pltpu.PrefetchScalarGridSpec(
            num_scalar_prefetch=5,
            grid=(n_rows // (nparts * bm),),
            in_specs=[pl.BlockSpec((nparts * bm, D_MODEL // 2), rows),
                      pl.BlockSpec(memory_space=pl.ANY), pl.BlockSpec(memory_space=pl.ANY)] + bias_specs,
            out_specs=pl.BlockSpec((nparts * bm, D_MODEL // 2), rows),
            scratch_shapes=[pltpu.VMEM((D_MODEL, 2 * D_FF), F32), pltpu.VMEM((D_FF, D_MODEL), F32),
                            pltpu.VMEM((nparts, D_MODEL, 2 * D_FF), BF16), pltpu.VMEM((nparts, D_FF, D_MODEL), BF16),
                            pltpu.SemaphoreType.DMA((2,))],
        ),
        out_shape=jax.ShapeDtypeStruct((n_rows, D_MODEL // 2), U32),
        compiler_params=_cparams(1),
        name="experts",
    )(block_e, next_e, first_blk, slot, n_used, xrows, w_gu, w_down, *bias_args)


def _combine_kernel(y0_ref, y1_ref, y2_ref, y3_ref, w_ref, h_ref, g_ref, b_ref, o_ref):
    w = w_ref[...]
    half = D_MODEL // 2
    f_lo = jnp.zeros((h_ref.shape[0], half), F32)
    f_hi = jnp.zeros((h_ref.shape[0], half), F32)
    for k, y_ref in enumerate((y0_ref, y1_ref, y2_ref, y3_ref)):
        lo, hi = _unpack_bf16_pairs(y_ref[...])
        f_lo = f_lo + lo * w[:, k:k + 1]
        f_hi = f_hi + hi * w[:, k:k + 1]
    f = jnp.concatenate([f_lo, f_hi], axis=1)
    o_ref[...] = _layer_norm(DN_ALPHA * h_ref[...] + f, g_ref[...], b_ref[...])


def _combine(yplanes, topw, h, ln2_g, ln2_b):
    tokens = h.shape[0]
    tm = MOE_TM
    per_plane = tokens // tm
    plane = lambda k: pl.BlockSpec((tm, D_MODEL // 2), lambda i: (k * per_plane + i, 0))
    return pl.pallas_call(
        _combine_kernel,
        grid=(per_plane,),
        in_specs=[plane(0), plane(1), plane(2), plane(3),
                  pl.BlockSpec((tm, LANES), lambda i: (i, 0)),
                  pl.BlockSpec((tm, D_MODEL), lambda i: (i, 0)),
                  pl.BlockSpec((1, D_MODEL), lambda i: (0, 0)),
                  pl.BlockSpec((1, D_MODEL), lambda i: (0, 0))],
        out_specs=pl.BlockSpec((tm, D_MODEL), lambda i: (i, 0)),
        out_shape=jax.ShapeDtypeStruct((tokens, D_MODEL), F32),
        compiler_params=_cparams(1),
        name="combine",
    )(yplanes, yplanes, yplanes, yplanes, topw, h, ln2_g.reshape(1, -1), ln2_b.reshape(1, -1))


def _routing_tables(topi, rank, cnt, tokens):
    bm = MOE_BM
    i32 = jnp.int32
    experts = jnp.arange(N_EXPERTS, dtype=i32)
    counts = cnt[0, :N_EXPERTS].astype(i32)
    padded = (counts + bm - 1) // bm * bm
    pend = jnp.cumsum(padded)
    pstart = pend - padded
    sel = topi[:, :TOP_K, None] == experts[None, None, :]
    dest = rank[:, :TOP_K] + jnp.sum(jnp.where(sel, pstart[None, None, :], 0), axis=-1)
    n_pad = N_EXPERTS * bm
    n_rows = tokens * TOP_K + n_pad
    starts = jnp.arange(n_rows // bm, dtype=i32) * bm
    block_e = jnp.minimum(jnp.sum((pend[None, :] <= starts[:, None]).astype(i32), axis=1), N_EXPERTS - 1)
    n_used = (pend[-1] // bm).reshape(1)
    is_block_e = block_e[:, None] == experts[None, :]
    after = jnp.sum(jnp.where(is_block_e, pend[None, :], 0), axis=1) // bm
    e_after = jnp.sum(jnp.where(after[:, None] == jnp.arange(n_rows // bm, dtype=i32)[None, :], block_e[None, :], 0),
                      axis=1)
    next_e = jnp.where(after < n_used[0], e_after, -1)
    blk = jnp.arange(n_rows // bm, dtype=i32)
    prev_e = jnp.concatenate([jnp.full((1,), -1, i32), block_e[:-1]])
    first_blk = ((blk < n_used[0]) & (block_e != prev_e)).astype(i32)
    slot = (jnp.cumsum(first_blk) - 1) % MOE_BLOCKS_PER_STEP
    pad_cnt = padded - counts
    pad_end = jnp.cumsum(pad_cnt)
    j = jnp.arange(n_pad, dtype=i32)
    owner = jnp.sum((pad_end[None, :] <= j[:, None]).astype(i32), axis=1)
    is_owner = owner[:, None] == experts[None, :]
    pick = lambda v: jnp.sum(jnp.where(is_owner, v[None, :], 0), axis=1)
    in_expert = pick(pstart + counts) + (j - pick(pad_end - pad_cnt))
    pad_rows = jnp.where(owner < N_EXPERTS, in_expert, pend[-1] + (j - pad_end[-1]))
    dest_t = dest.T.reshape(-1).astype(i32)
    tables = (block_e.astype(i32), next_e.astype(i32), first_blk, jnp.maximum(slot, 0).astype(i32), n_used.astype(i32))
    return dest_t, tables, pad_rows.astype(i32)


def kernel(x, w_in, b_gate, lam_q1, lam_k1, lam_q2, lam_k2, subln_g, w_oa, w_ob, w_o, ln1_g, ln1_b,
           w_router, b_router, w_gu, b_gu, w_down, b_down, ln2_g, ln2_b):
    batch, seq, d = x.shape
    tokens = batch * seq
    h = x.reshape(tokens, d)
    for l in range(DEPTH):
        q_scale = HEAD_DIM ** -0.5 * math.log2(math.e)
        qk_a, vt_a = _project_qkv(h, w_in[l], 0, seq, "qkv_diff", q_scale=q_scale, v_feature_major=True)
        ya = _diff_attention(qk_a.reshape(batch, seq, 2 * SEG_W), vt_a, lam_q1[l], lam_k1[l], lam_q2[l],
                             lam_k2[l], subln_g[l]).reshape(tokens, -1)
        dil = []
        for g, (_, dilation) in enumerate(DIL_PAIRS):
            qkv_g = _project_qkv(h, w_in[l], g + 1, seq, f"qkv_dil{g}", q_scale=q_scale, dil=dilation)
            dil.append(_dilated_group(qkv_g, batch, seq, g))
        h1, h1p, topi, topw, rank, cnt = _merge(ya, dil, h, seq, w_in[l], b_gate[l], w_oa[l].astype(BF16),
                                                w_ob[l].astype(BF16), w_o[l].astype(BF16), ln1_g[l], ln1_b[l],
                                                w_router[l], b_router[l])
        dest_t, expert_tables, pad_dst = _routing_tables(topi, rank, cnt, tokens)
        xrows = _sc_dispatch_rows(h1p, dest_t, pad_dst)
        yrows = _experts(xrows, *expert_tables, w_gu[l], b_gu[l], w_down[l], b_down[l])
        yplanes = _sc_gather_rows(yrows, dest_t)
        h = _combine(yplanes, topw, h1, ln2_g[l], ln2_b[l])
    return h.reshape(batch, seq, d)
```

```python
import functools
import math

import jax
import jax.numpy as jnp
from jax import lax
from jax.experimental import pallas as pl
from jax.experimental.pallas import tpu as pltpu
from jax.experimental.pallas import tpu_sc as plsc

F32 = jnp.float32
BF16 = jnp.bfloat16
U32 = jnp.uint32

D_MODEL = 1024
HEAD_DIM = 64
ROT_DIM = HEAD_DIM // 4
ROPE_THETA = 500000.0
QBLK = 128
DA_HEADS = 4
DIL_PAIRS = ((128, 1), (512, 4), (2048, 16))
SEG_W = 512
QKV_W = 3 * SEG_W
N_EXPERTS = 32
TOP_K = 4
D_FF = D_MODEL
SWIGLU_ALPHA = 1.702
SWIGLU_LIMIT = 7.0
DEPTH = 1
DN_ALPHA = (2 * DEPTH) ** 0.25
EPS = 1e-5
LAMBDA_INIT = 0.8 - 0.6 * math.exp(-0.3 * 0)

LANES = 128
QKV_TM = 1024
QKV_CHUNK = 256
STRIDE_STAGE = 4
ATT_TQ = 256
ONES_ROWS = 16
DIL_NQ = 8
FF_CHUNK = 512
MERGE_TM = 512
MOE_BM = 256
MOE_BLOCKS_PER_STEP = 4
MOE_TM = 1024
SC_CHUNK = 128
VMEM_LIMIT = 52 * 1024 * 1024


def _cparams(n_axes):
    return pltpu.CompilerParams(dimension_semantics=("arbitrary",) * n_axes,
                                vmem_limit_bytes=VMEM_LIMIT)


def _qkv_kernel(x_ref, w_ref, cs_ref, o_ref, *rest, q_scale, dil):
    tm = x_ref.shape[0]
    n = tm // dil
    two_stage = dil > STRIDE_STAGE
    if two_stage:
        *maybe_vt_ref, wb_ref, slab_ref, slab2_ref = rest
    elif dil > 1:
        *maybe_vt_ref, wb_ref, slab_ref = rest
    else:
        *maybe_vt_ref, wb_ref = rest

    @pl.when(pl.program_id(0) == 0)
    def _():
        wb_ref[...] = w_ref[...].astype(BF16)

    if dil > 1:
        for c in range(D_MODEL // LANES):
            slab_ref[c] = x_ref[:, c * LANES:(c + 1) * LANES]
    if two_stage:
        part = tm // STRIDE_STAGE
        for c in range(D_MODEL // LANES):
            for q in range(STRIDE_STAGE):
                slab2_ref[c, q * part:(q + 1) * part, :] = slab_ref[c, pl.ds(q, part, stride=STRIDE_STAGE), :]

    def residue_rows(c, r, l0, count):
        if two_stage:
            first = (r % STRIDE_STAGE) * (tm // STRIDE_STAGE) + r // STRIDE_STAGE + l0 * (dil // STRIDE_STAGE)
            return slab2_ref[c, pl.ds(first, count, stride=dil // STRIDE_STAGE), :]
        return slab_ref[c, pl.ds(l0 * dil + r, count, stride=dil), :]

    def regrouped_rows(start):
        pieces = []
        for p in range(start, start + QKV_CHUNK, min(n, QKV_CHUNK)):
            pieces.append(jnp.concatenate(
                [residue_rows(c, p // n, p % n, min(n, QKV_CHUNK)) for c in range(D_MODEL // LANES)], axis=1))
        return jnp.concatenate(pieces, axis=0).astype(BF16)

    def store(rows, cols, val):
        if dil == 1:
            o_ref[rows, cols] = val
        elif n >= QKV_CHUNK:
            o_ref[rows.start // n, rows.start % n:rows.start % n + QKV_CHUNK, cols] = val
        else:
            per = QKV_CHUNK // n
            for q in range(per):
                o_ref[rows.start // n + q, :, cols] = val[q * n:(q + 1) * n]

    half = ROT_DIM // 2
    in_head = lax.broadcasted_iota(jnp.int32, (QKV_CHUNK, LANES), 1) % HEAD_DIM
    first, second = in_head < half, (in_head >= half) & (in_head < ROT_DIM)
    for rc in range(tm // QKV_CHUNK):
        rows = slice(rc * QKV_CHUNK, (rc + 1) * QKV_CHUNK)
        xb = regrouped_rows(rows.start) if dil > 1 else x_ref[rows, :].astype(BF16)
        cs = cs_ref[rows, :]
        c = jnp.where(first, cs, jnp.where(second, pltpu.roll(cs, half, 1), 1.0))
        s1 = jnp.where(second, cs, 0.0)
        s2 = jnp.where(first, -pltpu.roll(cs, LANES - half, 1), 0.0)
        for seg in range(3):
            acc = jnp.dot(xb, wb_ref[:, seg * SEG_W:(seg + 1) * SEG_W], preferred_element_type=F32)
            if seg == 2:
                if maybe_vt_ref:
                    maybe_vt_ref[0][:, rows] = acc.T.astype(BF16)
                else:
                    store(rows, slice(seg * SEG_W, (seg + 1) * SEG_W), acc.astype(BF16))
                continue
            for k in range(SEG_W // LANES):
                t = acc[:, k * LANES:(k + 1) * LANES]
                r = t * c + pltpu.roll(t, ROT_DIM // 2, 1) * s1 + pltpu.roll(t, LANES - ROT_DIM // 2, 1) * s2
                if seg == 0:
                    r = r * q_scale
                lo = seg * SEG_W + k * LANES
                store(rows, slice(lo, lo + LANES), r.astype(BF16))


def _rope_lane_table(seq, dil, tm):
    row = jnp.arange(seq, dtype=jnp.int32)
    n = tm // dil
    in_tile = row % tm
    pos = (row - in_tile + (in_tile % n) * dil + in_tile // n).astype(F32)
    inv_freq = ROPE_THETA ** (-jnp.arange(0, ROT_DIM, 2, dtype=F32) / ROT_DIM)
    ang = pos[:, None] * inv_freq[None, :]
    head = jnp.concatenate([jnp.cos(ang), jnp.sin(ang), jnp.zeros((seq, HEAD_DIM - ROT_DIM), F32)], axis=1)
    return jnp.tile(head, (1, LANES // HEAD_DIM))


def _project_qkv(x2, w, col_block, seq, name, q_scale, dil=1, v_feature_major=False):
    tokens = x2.shape[0]
    batch = tokens // seq
    tm = min(QKV_TM, seq)
    per_seq = seq // tm
    n = tm // dil
    out_w = 2 * SEG_W if v_feature_major else QKV_W
    scratch = [pltpu.VMEM((D_MODEL, QKV_W), BF16)]
    if dil == 1:
        out_specs = [pl.BlockSpec((tm, out_w), lambda i: (i, 0))]
        out_shape = [jax.ShapeDtypeStruct((tokens, out_w), BF16)]
    else:
        out_specs = [pl.BlockSpec((None, dil, n, out_w), lambda i: (i // per_seq, 0, i % per_seq, 0))]
        out_shape = [jax.ShapeDtypeStruct((batch, dil, seq // dil, out_w), BF16)]
        scratch += [pltpu.VMEM((D_MODEL // LANES, tm, LANES), F32)] * (2 if dil > STRIDE_STAGE else 1)
    if v_feature_major:
        out_specs.append(pl.BlockSpec((None, SEG_W, tm), lambda i: (i // per_seq, 0, i % per_seq)))
        out_shape.append(jax.ShapeDtypeStruct((batch, SEG_W, seq), BF16))
    outs = pl.pallas_call(
        functools.partial(_qkv_kernel, q_scale=q_scale, dil=dil),
        grid=(tokens // tm,),
        in_specs=[pl.BlockSpec((tm, D_MODEL), lambda i: (i, 0)),
                  pl.BlockSpec((D_MODEL, QKV_W), lambda i: (0, col_block)),
                  pl.BlockSpec((tm, LANES), lambda i: (i % per_seq, 0))],
        out_specs=out_specs,
        out_shape=out_shape,
        scratch_shapes=scratch,
        compiler_params=_cparams(1),
        name=name,
    )(x2, w, _rope_lane_table(seq, dil, tm))
    return outs if v_feature_major else outs[0]


def _nt_dot(a, b):
    return lax.dot_general(a, b, (((1,), (1,)), ((), ())), preferred_element_type=F32)


def _diff_kernel(q_ref, k_ref, vt_ref, lq1_ref, lk1_ref, lq2_ref, lk2_ref, g_ref, o_ref, s_ref, acc_ref, *, tq):
    i = pl.program_id(1)
    chains = 2 * DA_HEADS
    half = tq // 2
    lane = lax.broadcasted_iota(jnp.int32, (tq, LANES), 1)
    qs = []
    for h in range(DA_HEADS):
        q = q_ref[:, h * LANES:(h + 1) * LANES]
        zero = jnp.zeros_like(q)
        qs += [jnp.where(lane < HEAD_DIM, q, zero), jnp.where(lane >= HEAD_DIM, q, zero)]
    acc_ref[...] = jnp.zeros_like(acc_ref)

    def score_chain(j, buf, c):
        start = pl.multiple_of(j * tq, tq)
        kb = k_ref[pl.ds(start, tq), (c // 2) * LANES:(c // 2 + 1) * LANES]
        buf[c] = _nt_dot(kb, qs[c])

    def score_block(j, buf):
        for c in range(chains):
            score_chain(j, buf, c)

    def absorb(j, buf, state, masked, nxt=None):
        start = pl.multiple_of(j * tq, tq)
        out = []
        for c in range(chains):
            if nxt is not None:
                score_chain(j + 1, nxt, c)
            m, l = state[c]
            if masked:
                key = lax.broadcasted_iota(jnp.int32, (tq, tq), 0)
                qry = lax.broadcasted_iota(jnp.int32, (tq, tq), 1)
                buf[c] = jnp.where(key <= qry, buf[c], -jnp.inf)
            m_new = jnp.maximum(m, jnp.max(buf[c], axis=0, keepdims=True))
            a = jnp.exp2(m - m_new)
            p = [jnp.exp2((buf[c, u * half:(u + 1) * half, :] - m_new).astype(BF16)) for u in range(2)]
            vt = vt_ref[(c // 2) * LANES:(c // 2 + 1) * LANES, pl.ds(start, tq)]
            vt1 = jnp.concatenate([vt, jnp.ones((ONES_ROWS, tq), BF16)], axis=0)
            pv = jnp.dot(vt1, jnp.concatenate(p, axis=0), preferred_element_type=F32)
            l = a * l + pv[LANES:LANES + 1, :]
            acc_ref[c] = a * acc_ref[c] + pv[:LANES, :]
            out.append((m_new, l))
        return tuple(out)

    buf_a, buf_b = s_ref.at[0], s_ref.at[1]

    def two_blocks(jj, state):
        j = 2 * jj
        state = absorb(j, buf_a, state, False, nxt=buf_b)
        return absorb(j + 1, buf_b, state, False, nxt=buf_a)

    def last_from_a(state):
        return absorb(i, buf_a, state, True)

    def last_from_b(state):
        state = absorb(i - 1, buf_a, state, False, nxt=buf_b)
        return absorb(i, buf_b, state, True)

    init = tuple((jnp.full((1, tq), -jnp.inf, F32), jnp.zeros((1, tq), F32)) for _ in range(chains))
    score_block(0, buf_a)
    state = lax.fori_loop(0, i // 2, two_blocks, init)
    state = lax.cond(lax.rem(i, 2) == 0, last_from_a, last_from_b, state)

    lam = (jnp.exp(jnp.sum(lq1_ref[...] * lk1_ref[...], axis=1, keepdims=True))
           - jnp.exp(jnp.sum(lq2_ref[...] * lk2_ref[...], axis=1, keepdims=True)) + LAMBDA_INIT)
    for h in range(DA_HEADS):
        l1, l2 = state[2 * h][1], state[2 * h + 1][1]
        o = (acc_ref[2 * h] / l1 - lam * (acc_ref[2 * h + 1] / l2)).T
        ms = jnp.mean(o * o, axis=1, keepdims=True)
        o = o * lax.rsqrt(ms + EPS) * g_ref[...]
        o_ref[:, h * LANES:(h + 1) * LANES] = (o * (1.0 - LAMBDA_INIT)).astype(BF16)


def _diff_attention(qk3, vt3, lam_q1, lam_k1, lam_q2, lam_k2, subln_g):
    b, s, _ = qk3.shape
    tq = min(ATT_TQ, s)
    vec = lambda n: pl.BlockSpec((1, n), lambda bb, i: (0, 0))
    return pl.pallas_call(
        functools.partial(_diff_kernel, tq=tq),
        grid=(b, s // tq),
        in_specs=[
            pl.BlockSpec((None, tq, SEG_W), lambda bb, i: (bb, i, 0)),
            pl.BlockSpec((None, s, SEG_W), lambda bb, i: (bb, 0, 1)),
            pl.BlockSpec((None, SEG_W, s), lambda bb, i: (bb, 0, 0)),
            vec(HEAD_DIM), vec(HEAD_DIM), vec(HEAD_DIM), vec(HEAD_DIM), vec(2 * HEAD_DIM),
        ],
        out_specs=pl.BlockSpec((None, tq, SEG_W), lambda bb, i: (bb, i, 0)),
        out_shape=jax.ShapeDtypeStruct((b, s, SEG_W), BF16),
        scratch_shapes=[pltpu.VMEM((2, 2 * DA_HEADS, tq, tq), F32), pltpu.VMEM((2 * DA_HEADS, LANES, tq), F32)],
        compiler_params=_cparams(2),
        name="diff_attn",
    )(qk3, qk3, vt3, lam_q1.reshape(1, -1), lam_k1.reshape(1, -1), lam_q2.reshape(1, -1),
      lam_k2.reshape(1, -1), subln_g.reshape(1, -1))


def _dil_kernel(q_ref, kp_ref, kc_ref, vp_ref, vc_ref, o_ref, lse_ref, *, rel, nq, nres):
    n = pl.program_id(2)
    qi = lax.broadcasted_iota(jnp.int32, (QBLK, 2 * QBLK), 0)
    kj = lax.broadcasted_iota(jnp.int32, (QBLK, 2 * QBLK), 1)
    dist = qi + QBLK - kj
    band = (dist >= 0) & (dist <= rel)
    lane = lax.broadcasted_iota(jnp.int32, (QBLK, LANES), 1)
    low = lane < HEAD_DIM
    slabs = [slice(p * LANES, (p + 1) * LANES) for p in range(SEG_W // LANES)]

    def window(prev_ref, cur_ref, s, sl):
        if s == 0:
            return jnp.concatenate([prev_ref[:, sl], cur_ref[:QBLK, sl]], axis=0)
        return cur_ref[(s - 1) * QBLK:(s + 1) * QBLK, sl]

    def score_block(blk):
        res, s = blk
        rows = slice(s * QBLK, (s + 1) * QBLK)
        scores = []
        for sl in slabs:
            q2 = q_ref[res, rows, sl]
            k2 = window(kp_ref.at[res], kc_ref.at[res], s, sl)
            for sel in (low, jnp.logical_not(low)):
                scores.append(_nt_dot(jnp.where(sel, q2, jnp.zeros_like(q2)), k2))
        return scores

    def finish_block(blk, scores):
        res, s = blk
        rows = slice(s * QBLK, (s + 1) * QBLK)
        valid = band & ((kj >= QBLK) | (n * nq + s > 0))
        probs = []
        for sc in scores:
            sc = jnp.where(valid, sc, -jnp.inf)
            m = jnp.max(sc, axis=1, keepdims=True)
            probs.append((jnp.exp2((sc - m).astype(BF16)), m))
        one = jnp.ones((2 * QBLK, LANES), BF16)
        low_keys = lax.broadcasted_iota(jnp.int32, (2 * QBLK, LANES), 1) < HEAD_DIM
        for p, sl in enumerate(slabs):
            v2 = window(vp_ref.at[res], vc_ref.at[res], s, sl)
            (pe_lo, m_lo), (pe_hi, m_hi) = probs[2 * p:2 * p + 2]
            pv_lo = jnp.dot(pe_lo, jnp.where(low_keys, v2, one), preferred_element_type=F32)
            pv_hi = jnp.dot(pe_hi, jnp.where(low_keys, one, v2), preferred_element_type=F32)
            num = jnp.where(low, pv_lo, pv_hi)
            den = pltpu.roll(jnp.where(low, pv_hi, pv_lo), HEAD_DIM, 1)
            o_ref[res, rows, sl] = (num / den).astype(BF16)
            lse_ref[res, rows, sl] = jnp.where(low, m_lo, m_hi) + jnp.log2(den)

    blocks = [(res, s) for res in range(nres) for s in range(nq)]
    pending = score_block(blocks[0])
    for prev, blk in zip(blocks, blocks[1:]):
        nxt = score_block(blk)
        finish_block(prev, pending)
        pending = nxt
    finish_block(blocks[-1], pending)


def _dilated_group(qkv, batch, seq, group):
    window, dil = DIL_PAIRS[group]
    rel = window // dil
    length = seq // dil
    nb = length // QBLK
    nq = math.gcd(nb, DIL_NQ)
    nres = math.gcd(dil, DIL_NQ // nq)
    view = qkv.reshape(batch, dil, length, QKV_W)

    def spec(seg, prev):
        if prev:
            return pl.BlockSpec((None, nres, QBLK, SEG_W), lambda b, r, n: (b, r, jnp.maximum(n * nq - 1, 0), seg))
        return pl.BlockSpec((None, nres, nq * QBLK, SEG_W), lambda b, r, n: (b, r, n, seg))

    out_spec = pl.BlockSpec((None, nres, nq * QBLK, SEG_W), lambda b, r, n: (b, r, n, 0))
    o, lse = pl.pallas_call(
        functools.partial(_dil_kernel, rel=rel, nq=nq, nres=nres),
        grid=(batch, dil // nres, nb // nq),
        in_specs=[spec(0, False), spec(1, True), spec(1, False), spec(2, True), spec(2, False)],
        out_specs=[out_spec, out_spec],
        out_shape=[jax.ShapeDtypeStruct((batch, dil, length, SEG_W), BF16),
                   jax.ShapeDtypeStruct((batch, dil, length, SEG_W), F32)],
        compiler_params=_cparams(3),
        name=f"dilated{group}",
    )(view, view, view, view, view)
    return o, lse


def _layer_norm(z, g, b):
    mu = jnp.mean(z, axis=1, keepdims=True)
    zc = z - mu
    var = jnp.mean(zc * zc, axis=1, keepdims=True)
    return zc * lax.rsqrt(var + EPS) * g + b


def _pack_bf16_pairs(v):
    n = v.shape[1] // 2
    lo = pltpu.bitcast(v[:, :n].astype(BF16).astype(F32), U32)
    hi = pltpu.bitcast(v[:, n:].astype(BF16).astype(F32), U32)
    return jnp.bitwise_or(jnp.right_shift(lo, jnp.uint32(16)), jnp.bitwise_and(hi, jnp.uint32(0xFFFF0000)))


def _unpack_bf16_pairs(w):
    lo = pltpu.bitcast(jnp.left_shift(w, jnp.uint32(16)), F32)
    hi = pltpu.bitcast(jnp.bitwise_and(w, jnp.uint32(0xFFFF0000)), F32)
    return lo, hi


def _merge_kernel(ya_ref, o1_ref, o2_ref, o3_ref, l1_ref, l2_ref, l3_ref, x_ref,
                  wg_ref, bg_ref, woa_ref, wob_ref, wo_ref, g_ref, b_ref, wrh_ref, wrl_ref, br_ref,
                  h_ref, hp_ref, topi_ref, topw_ref, rank_ref, cnt_ref, carry_ref, scr_ref, wgb_ref):
    i = pl.program_id(0)
    tm = x_ref.shape[0]

    @pl.when(i == 0)
    def _():
        carry_ref[...] = jnp.zeros_like(carry_ref)
        wgb_ref[...] = wg_ref[...].astype(BF16)

    hm = tm // 2
    halves = (slice(0, hm), slice(hm, tm))
    xs = [x_ref[rows, :] for rows in halves]
    xbs = [x.astype(BF16) for x in xs]
    gate_dot = lambda xb: jnp.dot(xb, wgb_ref[...], preferred_element_type=F32) + bg_ref[...]
    gate_pre = [gate_dot(xbs[0])]
    pas = [jnp.dot(ya_ref[rows, :], woa_ref[...], preferred_element_type=F32) for rows in halves]

    def token_major(ref, scr):
        dil, n = ref.shape[0], ref.shape[1]
        if dil == 1:
            return ref[0].astype(F32)
        for r in range(dil):
            blk = ref[r].astype(F32)
            for c in range(SEG_W // LANES):
                scr[c, pl.ds(r, n, stride=dil), :] = blk[:, c * LANES:(c + 1) * LANES]
        return jnp.concatenate([scr[c] for c in range(SEG_W // LANES)], axis=1)

    oa, ob, oc = (token_major(r, scr_ref.at[k]) for k, r in enumerate((o1_ref, o2_ref, o3_ref)))
    la, lb, lc = (token_major(r, scr_ref.at[3 + k]) for k, r in enumerate((l1_ref, l2_ref, l3_ref)))
    mx = jnp.maximum(jnp.maximum(la, lb), lc)
    ea, eb, ec = jnp.exp2(la - mx), jnp.exp2(lb - mx), jnp.exp2(lc - mx)
    yb = ((ea * oa + eb * ob + ec * oc) / (ea + eb + ec)).astype(BF16)
    pbs = [jnp.dot(yb[rows, :], wob_ref[...], preferred_element_type=F32) for rows in halves]
    gate_pre.append(gate_dot(xbs[1]))
    mixes = []
    for pre_act, pa, pb in zip(gate_pre, pas, pbs):
        gates = jax.nn.sigmoid(pre_act)
        merged = gates[:, :D_MODEL] * pa + gates[:, D_MODEL:] * pb
        mixes.append(jnp.dot(merged.astype(BF16), wo_ref[...], preferred_element_type=F32))
    logits = []
    for rows, x, mix in zip(halves, xs, mixes):
        h = _layer_norm(DN_ALPHA * x + mix, g_ref[...], b_ref[...])
        h_ref[rows, :] = h
        hp_ref[rows, :] = _pack_bf16_pairs(h)
        hh = h.astype(BF16)
        hl = (h - hh.astype(F32)).astype(BF16)
        logits.append(jnp.dot(hh, wrh_ref[...], preferred_element_type=F32)
                      + jnp.dot(hl, wrh_ref[...], preferred_element_type=F32)
                      + jnp.dot(hh, wrl_ref[...], preferred_element_type=F32) + br_ref[...])

    lane = lax.broadcasted_iota(jnp.int32, (hm, LANES), 1)
    r_i = lax.broadcasted_iota(jnp.int32, (hm, hm), 0)
    c_i = lax.broadcasted_iota(jnp.int32, (hm, hm), 1)
    tri = (r_i > c_i).astype(BF16)
    before = carry_ref[...]
    for rows, lg in zip(halves, logits):
        lg = jnp.where(lane < N_EXPERTS, lg, -jnp.inf)
        vals, idxs = [], []
        for _ in range(TOP_K):
            mv = jnp.max(lg, axis=1, keepdims=True)
            ik = jnp.min(jnp.where(lg == mv, lane, LANES), axis=1, keepdims=True)
            vals.append(mv)
            idxs.append(ik)
            lg = jnp.where(lane == ik, -jnp.inf, lg)
        es = [jnp.exp(v - vals[0]) for v in vals]
        tot = es[0] + es[1] + es[2] + es[3]
        onehot = jnp.zeros((hm, LANES), F32)
        for ik in idxs:
            onehot = onehot + (lane == ik).astype(F32)
        pre = jnp.dot(tri, onehot.astype(BF16), preferred_element_type=F32) + before
        topi = jnp.zeros((hm, LANES), jnp.int32)
        topw = jnp.zeros((hm, LANES), F32)
        rank = jnp.zeros((hm, LANES), F32)
        for k in range(TOP_K):
            rk = jnp.sum(jnp.where(lane == idxs[k], pre, 0.0), axis=1, keepdims=True)
            topi = jnp.where(lane == k, idxs[k], topi)
            topw = jnp.where(lane == k, es[k] / tot, topw)
            rank = jnp.where(lane == k, rk, rank)
        topi_ref[rows, :] = topi
        topw_ref[rows, :] = topw
        rank_ref[rows, :] = rank.astype(jnp.int32)
        before = before + jnp.sum(onehot, axis=0, keepdims=True)
    carry_ref[...] = before
    cnt_ref[...] = before


def _merge(ya, dil_outs, x2, seq, w_b, b_gate, w_oa_b, w_ob_b, w_o_b, ln1_g, ln1_b, w_router, b_router):
    tokens = x2.shape[0]
    tm = MERGE_TM
    per_seq = seq // tm

    def group(a):
        dil = a.shape[1]
        return pl.BlockSpec((None, dil, tm // dil, SEG_W), lambda i: (i // per_seq, 0, i % per_seq, 0))

    gate_blk = w_b.shape[1] // (2 * D_MODEL) - 1
    wr = jnp.zeros((D_MODEL, LANES), F32).at[:, :N_EXPERTS].set(w_router)
    wr_hi = wr.astype(BF16)
    wr_lo = (wr - wr_hi.astype(F32)).astype(BF16)
    br = jnp.zeros((1, LANES), F32).at[0, :N_EXPERTS].set(b_router)
    row = lambda w: pl.BlockSpec((tm, w), lambda i: (i, 0))
    full = lambda a: pl.BlockSpec(a.shape, lambda i: (0,) * a.ndim)
    (o1, l1), (o2, l2), (o3, l3) = dil_outs
    bg = b_gate.reshape(1, -1)
    g1 = ln1_g.reshape(1, -1)
    b1 = ln1_b.reshape(1, -1)
    lane_out = lambda dt: jax.ShapeDtypeStruct((tokens, LANES), dt)
    return pl.pallas_call(
        _merge_kernel,
        grid=(tokens // tm,),
        in_specs=[row(SEG_W), group(o1), group(o2), group(o3), group(l1), group(l2), group(l3),
                  row(D_MODEL), pl.BlockSpec((D_MODEL, 2 * D_MODEL), lambda i: (0, gate_blk)),
                  full(bg), full(w_oa_b), full(w_ob_b), full(w_o_b), full(g1), full(b1),
                  full(wr_hi), full(wr_lo), full(br)],
        out_specs=[row(D_MODEL), row(D_MODEL // 2), row(LANES), row(LANES), row(LANES),
                   pl.BlockSpec((1, LANES), lambda i: (0, 0))],
        out_shape=[jax.ShapeDtypeStruct((tokens, D_MODEL), F32), jax.ShapeDtypeStruct((tokens, D_MODEL // 2), U32),
                   lane_out(jnp.int32), lane_out(F32), lane_out(jnp.int32), jax.ShapeDtypeStruct((1, LANES), F32)],
        scratch_shapes=[pltpu.VMEM((1, LANES), F32), pltpu.VMEM((6, SEG_W // LANES, tm, LANES), F32),
                        pltpu.VMEM((D_MODEL, 2 * D_MODEL), BF16)],
        compiler_params=_cparams(1),
        name="merge",
    )(ya, o1, o2, o3, l1, l2, l3, x2, w_b, bg, w_oa_b, w_ob_b, w_o_b, g1, b1, wr_hi, wr_lo, br)


def _sc_mesh():
    return plsc.VectorSubcoreMesh(core_axis_name="core", subcore_axis_name="subcore")


def _sc_dispatch_rows(table, dest_t, pad_dst):
    tokens, d = table.shape
    slots = dest_t.shape[0] // tokens
    n_pad = pad_dst.shape[0]
    mesh = _sc_mesh()
    workers = mesh.num_cores * mesh.num_subcores
    chunks = tokens // SC_CHUNK
    per = chunks // workers
    pad_every = chunks * SC_CHUNK // n_pad
    assert per * workers * SC_CHUNK == tokens and pad_every * n_pad == tokens

    @pl.kernel(out_type=jax.ShapeDtypeStruct((slots * tokens + n_pad, d), table.dtype), mesh=mesh,
               scratch_types=[pltpu.VMEM((1, SC_CHUNK), jnp.int32), pltpu.VMEM((SC_CHUNK, d), table.dtype)])
    def copy(t_hbm, d_hbm, pd_hbm, o_hbm, d_vm, buf):
        wid = lax.axis_index("core") * mesh.num_subcores + lax.axis_index("subcore")

        @pl.loop(0, per)
        def _(j):
            blk = wid * per + j
            pltpu.sync_copy(t_hbm.at[pl.ds(blk * SC_CHUNK, SC_CHUNK)], buf)
            for k in range(slots):
                pltpu.sync_copy(d_hbm.at[pl.ds(k * chunks + blk, 1)], d_vm)
                pltpu.sync_copy(buf, o_hbm.at[d_vm.at[0]])

            @pl.when(lax.rem(blk, pad_every) == 0)
            def _():
                pltpu.sync_copy(pd_hbm.at[pl.ds(blk // pad_every, 1)], d_vm)
                pltpu.sync_copy(buf, o_hbm.at[d_vm.at[0]])

    return copy(table, dest_t.reshape(-1, SC_CHUNK), pad_dst.reshape(-1, SC_CHUNK))


def _sc_gather_rows(table, idx):
    n = idx.shape[0]
    d = table.shape[1]
    mesh = _sc_mesh()
    workers = mesh.num_cores * mesh.num_subcores
    per = n // (SC_CHUNK * workers)
    assert per * SC_CHUNK * workers == n

    @pl.kernel(out_type=jax.ShapeDtypeStruct((n, d), table.dtype), mesh=mesh,
               scratch_types=[pltpu.VMEM((1, SC_CHUNK), jnp.int32), pltpu.VMEM((SC_CHUNK, d), table.dtype)])
    def gather(t_hbm, i_hbm, o_hbm, i_vm, buf):
        wid = lax.axis_index("core") * mesh.num_subcores + lax.axis_index("subcore")

        @pl.loop(0, per)
        def _(j):
            blk = wid * per + j
            pltpu.sync_copy(i_hbm.at[pl.ds(blk, 1)], i_vm)
            pltpu.sync_copy(t_hbm.at[i_vm.at[0]], buf)
            pltpu.sync_copy(buf, o_hbm.at[pl.ds(blk * SC_CHUNK, SC_CHUNK)])

    return gather(table, idx.reshape(-1, SC_CHUNK))


def _expert_kernel(be_ref, nx_ref, first_ref, slot_ref, nu_ref, x_ref, wgu_hbm, wd_hbm, *rest):
    nparts = MOE_BLOCKS_PER_STEP
    bias_refs, (y_ref, wgu_stage, wd_stage, wgu_b, wd_b, sem) = rest[:2 * nparts], rest[2 * nparts:]
    biases = [(bias_refs[2 * h], bias_refs[2 * h + 1]) for h in range(nparts)]
    i = pl.program_id(0)
    hb = x_ref.shape[0] // nparts
    blocks = [nparts * i + h for h in range(nparts)]

    def weight_copies(e):
        return (pltpu.make_async_copy(wgu_hbm.at[e], wgu_stage, sem.at[0]),
                pltpu.make_async_copy(wd_hbm.at[e], wd_stage, sem.at[1]))

    @pl.when(i == 0)
    def _():
        for cp in weight_copies(be_ref[0]):
            cp.start()

    for b in blocks:
        @pl.when(first_ref[b] == 1)
        def _():
            for cp in weight_copies(be_ref[b]):
                cp.wait()
            wgu_b[slot_ref[b]] = wgu_stage[...].astype(BF16)
            wd_b[slot_ref[b]] = wd_stage[...].astype(BF16)

            @pl.when(nx_ref[b] >= 0)
            def _():
                for cp in weight_copies(nx_ref[b]):
                    cp.start()

    cols = [slice(c * FF_CHUNK, (c + 1) * FF_CHUNK) for c in range(D_FF // FF_CHUNK)]

    def gate_up(h):
        slot = slot_ref[blocks[h]]
        lo, hi = _unpack_bf16_pairs(x_ref[h * hb:(h + 1) * hb, :])
        xb = jnp.concatenate([lo, hi], axis=1).astype(BF16)
        bgu_ref = biases[h][0]
        pre = []
        for gs in cols:
            us = slice(D_FF + gs.start, D_FF + gs.stop)
            pre.append((jnp.dot(xb, wgu_b[slot, :, gs], preferred_element_type=F32) + bgu_ref[:, gs],
                        jnp.dot(xb, wgu_b[slot, :, us], preferred_element_type=F32) + bgu_ref[:, us]))
        return pre

    def down(h, pre):
        slot = slot_ref[blocks[h]]
        acc = jnp.zeros((hb, D_MODEL), F32)
        for gs, (gate, up) in zip(cols, pre):
            gate = jnp.minimum(gate, SWIGLU_LIMIT)
            up = jnp.clip(up, -SWIGLU_LIMIT, SWIGLU_LIMIT)
            act = (up + 1.0) * gate * jax.nn.sigmoid(SWIGLU_ALPHA * gate)
            acc = acc + jnp.dot(act.astype(BF16), wd_b[slot, gs, :], preferred_element_type=F32)
        y_ref[h * hb:(h + 1) * hb, :] = _pack_bf16_pairs(acc + biases[h][1][...])

    all_used = blocks[-1] < nu_ref[0]

    @pl.when(all_used)
    def _():
        pre = gate_up(0)
        for h in range(1, nparts):
            nxt = gate_up(h)
            down(h - 1, pre)
            pre = nxt
        down(nparts - 1, pre)

    for h in range(nparts):
        @pl.when(jnp.logical_not(all_used) & (blocks[h] < nu_ref[0]))
        def _():
            down(h, gate_up(h))

        @pl.when(blocks[h] >= nu_ref[0])
        def _():
            y_ref[h * hb:(h + 1) * hb, :] = jnp.zeros((hb, y_ref.shape[1]), y_ref.dtype)


def _experts(xrows, block_e, next_e, first_blk, slot, n_used, w_gu, b_gu, w_down, b_down):
    n_rows = xrows.shape[0]
    bm = MOE_BM
    nparts = MOE_BLOCKS_PER_STEP
    assert (n_rows // bm) % nparts == 0
    bias = lambda h: (lambda i, be, nx, fb, sl, nu: (be[nparts * i + h], 0, 0))
    rows = lambda i, be, nx, fb, sl, nu: (i, 0)
    b_gu3 = b_gu.reshape(N_EXPERTS, 1, -1)
    b_down3 = b_down.reshape(N_EXPERTS, 1, -1)
    bias_specs, bias_args = [], []
    for h in range(nparts):
        bias_specs += [pl.BlockSpec((None, 1, 2 * D_FF), bias(h)), pl.BlockSpec((None, 1, D_MODEL), bias(h))]
        bias_args += [b_gu3, b_down3]
    return pl.pallas_call(
        _expert_kernel,
        grid_spec=pltpu.PrefetchScalarGridSpec(
            num_scalar_prefetch=5,
            grid=(n_rows // (nparts * bm),),
            in_specs=[pl.BlockSpec((nparts * bm, D_MODEL // 2), rows),
                      pl.BlockSpec(memory_space=pl.ANY), pl.BlockSpec(memory_space=pl.ANY)] + bias_specs,
            out_specs=pl.BlockSpec((nparts * bm, D_MODEL // 2), rows),
            scratch_shapes=[pltpu.VMEM((D_MODEL, 2 * D_FF), F32), pltpu.VMEM((D_FF, D_MODEL), F32),
                            pltpu.VMEM((nparts, D_MODEL, 2 * D_FF), BF16), pltpu.VMEM((nparts, D_FF, D_MODEL), BF16),
                            pltpu.SemaphoreType.DMA((2,))],
        ),
        out_shape=jax.ShapeDtypeStruct((n_rows, D_MODEL // 2), U32),
        compiler_params=_cparams(1),
        name="experts",
    )(block_e, next_e, first_blk, slot, n_used, xrows, w_gu, w_down, *bias_args)


def _combine_kernel(y0_ref, y1_ref, y2_ref, y3_ref, w_ref, h_ref, g_ref, b_ref, o_ref):
    w = w_ref[...]
    half = D_MODEL // 2
    f_lo = jnp.zeros((h_ref.shape[0], half), F32)
    f_hi = jnp.zeros((h_ref.shape[0], half), F32)
    for k, y_ref in enumerate((y0_ref, y1_ref, y2_ref, y3_ref)):
        lo, hi = _unpack_bf16_pairs(y_ref[...])
        f_lo = f_lo + lo * w[:, k:k + 1]
        f_hi = f_hi + hi * w[:, k:k + 1]
    f = jnp.concatenate([f_lo, f_hi], axis=1)
    o_ref[...] = _layer_norm(DN_ALPHA * h_ref[...] + f, g_ref[...], b_ref[...])


def _combine(yplanes, topw, h, ln2_g, ln2_b):
    tokens = h.shape[0]
    tm = MOE_TM
    per_plane = tokens // tm
    plane = lambda k: pl.BlockSpec((tm, D_MODEL // 2), lambda i: (k * per_plane + i, 0))
    return pl.pallas_call(
        _combine_kernel,
        grid=(per_plane,),
        in_specs=[plane(0), plane(1), plane(2), plane(3),
                  pl.BlockSpec((tm, LANES), lambda i: (i, 0)),
                  pl.BlockSpec((tm, D_MODEL), lambda i: (i, 0)),
                  pl.BlockSpec((1, D_MODEL), lambda i: (0, 0)),
                  pl.BlockSpec((1, D_MODEL), lambda i: (0, 0))],
        out_specs=pl.BlockSpec((tm, D_MODEL), lambda i: (i, 0)),
        out_shape=jax.ShapeDtypeStruct((tokens, D_MODEL), F32),
        compiler_params=_cparams(1),
        name="combine",
    )(yplanes, yplanes, yplanes, yplanes, topw, h, ln2_g.reshape(1, -1), ln2_b.reshape(1, -1))


def _routing_tables(topi, rank, cnt, tokens):
    bm = MOE_BM
    i32 = jnp.int32
    experts = jnp.arange(N_EXPERTS, dtype=i32)
    counts = cnt[0, :N_EXPERTS].astype(i32)
    padded = (counts + bm - 1) // bm * bm
    pend = jnp.cumsum(padded)
    pstart = pend - padded
    sel = topi[:, :TOP_K, None] == experts[None, None, :]
    dest = rank[:, :TOP_K] + jnp.sum(jnp.where(sel, pstart[None, None, :], 0), axis=-1)
    n_pad = N_EXPERTS * bm
    n_rows = tokens * TOP_K + n_pad
    starts = jnp.arange(n_rows // bm, dtype=i32) * bm
    block_e = jnp.minimum(jnp.sum((pend[None, :] <= starts[:, None]).astype(i32), axis=1), N_EXPERTS - 1)
    n_used = (pend[-1] // bm).reshape(1)
    is_block_e = block_e[:, None] == experts[None, :]
    after = jnp.sum(jnp.where(is_block_e, pend[None, :], 0), axis=1) // bm
    e_after = jnp.sum(jnp.where(after[:, None] == jnp.arange(n_rows // bm, dtype=i32)[None, :], block_e[None, :], 0),
                      axis=1)
    next_e = jnp.where(after < n_used[0], e_after, -1)
    blk = jnp.arange(n_rows // bm, dtype=i32)
    prev_e = jnp.concatenate([jnp.full((1,), -1, i32), block_e[:-1]])
    first_blk = ((blk < n_used[0]) & (block_e != prev_e)).astype(i32)
    slot = (jnp.cumsum(first_blk) - 1) % MOE_BLOCKS_PER_STEP
    pad_cnt = padded - counts
    pad_end = jnp.cumsum(pad_cnt)
    j = jnp.arange(n_pad, dtype=i32)
    owner = jnp.sum((pad_end[None, :] <= j[:, None]).astype(i32), axis=1)
    is_owner = owner[:, None] == experts[None, :]
    pick = lambda v: jnp.sum(jnp.where(is_owner, v[None, :], 0), axis=1)
    in_expert = pick(pstart + counts) + (j - pick(pad_end - pad_cnt))
    pad_rows = jnp.where(owner < N_EXPERTS, in_expert, pend[-1] + (j - pad_end[-1]))
    dest_t = dest.T.reshape(-1).astype(i32)
    tables = (block_e.astype(i32), next_e.astype(i32), first_blk, jnp.maximum(slot, 0).astype(i32), n_used.astype(i32))
    return dest_t, tables, pad_rows.astype(i32)


def kernel(x, w_in, b_gate, lam_q1, lam_k1, lam_q2, lam_k2, subln_g, w_oa, w_ob, w_o, ln1_g, ln1_b,
           w_router, b_router, w_gu, b_gu, w_down, b_down, ln2_g, ln2_b):
    batch, seq, d = x.shape
    tokens = batch * seq
    h = x.reshape(tokens, d)
    for l in range(DEPTH):
        q_scale = HEAD_DIM ** -0.5 * math.log2(math.e)
        qk_a, vt_a = _project_qkv(h, w_in[l], 0, seq, "qkv_diff", q_scale=q_scale, v_feature_major=True)
        ya = _diff_attention(qk_a.reshape(batch, seq, 2 * SEG_W), vt_a, lam_q1[l], lam_k1[l], lam_q2[l],
                             lam_k2[l], subln_g[l]).reshape(tokens, -1)
        dil = []
        for g, (_, dilation) in enumerate(DIL_PAIRS):
            qkv_g = _project_qkv(h, w_in[l], g + 1, seq, f"qkv_dil{g}", q_scale=q_scale, dil=dilation)
            dil.append(_dilated_group(qkv_g, batch, seq, g))
        h1, h1p, topi, topw, rank, cnt = _merge(ya, dil, h, seq, w_in[l], b_gate[l], w_oa[l].astype(BF16),
                                                w_ob[l].astype(BF16), w_o[l].astype(BF16), ln1_g[l], ln1_b[l],
                                                w_router[l], b_router[l])
        dest_t, expert_tables, pad_dst = _routing_tables(topi, rank, cnt, tokens)
        xrows = _sc_dispatch_rows(h1p, dest_t, pad_dst)
        yrows = _experts(xrows, *expert_tables, w_gu[l], b_gu[l], w_down[l], b_down[l])
        yplanes = _sc_gather_rows(yrows, dest_t)
        h = _combine(yplanes, topw, h1, ln2_g[l], ln2_b[l])
    return h.reshape(batch, seq, d)
```
